```python
import jax, jax.numpy as jnp
from jax import lax
import numpy as np

D_MODEL = 1024
BATCH = 8
SEQ = 4096
DEPTH = 2

N_META = 16
CHUNK = 128
PAD_FRONT = (-N_META) % CHUNK
RET_HEADS = 4
RET_QK_DIM = D_MODEL // 8
RET_V_DIM = 2 * RET_QK_DIM
RET_QK_W = RET_HEADS * RET_QK_DIM
RET_V_W = RET_HEADS * RET_V_DIM
CONV_CH = D_MODEL
CONV_WIDTH = 31
MIX_IN_EVEN = 2 * RET_QK_W + 2 * RET_V_W + 2 * CONV_CH
MIX_OUT_EVEN = RET_V_W + CONV_CH
RET_DECAY_OFFSET = 5.0
ROPE_BASE = 10000.0
SB_HEADS = 16
SB_HEAD_DIM = D_MODEL // SB_HEADS
D_FF = 4 * D_MODEL
EPS = 1e-6
N_EVEN = (DEPTH + 1) // 2
N_ODD = DEPTH // 2

kernel_name = "hybrid_retention_conformer_stickbreaking_trunk"


def rmsnorm(x, g):
    xf = x.astype(jnp.float32)
    y = xf * lax.rsqrt(jnp.mean(xf * xf, axis=-1, keepdims=True) + EPS) * g.astype(jnp.float32)
    return y.astype(x.dtype)


def layernorm(x, g, b):
    xf = x.astype(jnp.float32)
    mu = jnp.mean(xf, axis=-1, keepdims=True)
    var = jnp.mean(jnp.square(xf - mu), axis=-1, keepdims=True)
    y = (xf - mu) * lax.rsqrt(var + EPS) * g.astype(jnp.float32) + b.astype(jnp.float32)
    return y.astype(x.dtype)


def rotary(x):
    P, d = x.shape[1], x.shape[-1]
    half = d // 2
    inv_freq = ROPE_BASE ** (-jnp.arange(half, dtype=jnp.float32) / half)
    ang = jnp.arange(P, dtype=jnp.float32)[:, None] * inv_freq[None, :]
    cos = jnp.cos(ang)[None, :, None, :]
    sin = jnp.sin(ang)[None, :, None, :]
    x1, x2 = x[..., :half], x[..., half:]
    return jnp.concatenate([x1 * cos - x2 * sin, x1 * sin + x2 * cos], axis=-1)


def retention_chunkwise(q, k, v):
    b, P, H, dk = q.shape
    dv = v.shape[-1]
    n = P // CHUNK
    log_g = jnp.log1p(-jnp.exp2(-RET_DECAY_OFFSET - jnp.arange(H, dtype=jnp.float32)))
    idx = jnp.arange(CHUNK, dtype=jnp.float32)
    diff = idx[:, None] - idx[None, :]
    inner_decay = jnp.where(diff[None] >= 0, jnp.exp(jnp.maximum(diff, 0.0)[None] * log_g[:, None, None]), 0.0)
    qc = q.reshape(b, n, CHUNK, H, dk)
    kc = k.reshape(b, n, CHUNK, H, dk)
    vc = v.reshape(b, n, CHUNK, H, dv)
    scores = jnp.einsum('bnihd,bnjhd->bnhij', qc, kc) * inner_decay
    o_inner = jnp.einsum('bnhij,bnjhe->bnihe', scores, vc)
    k_dec = kc * jnp.exp((CHUNK - 1 - idx)[:, None] * log_g[None, :])[:, :, None]
    kv = jnp.einsum('bnjhd,bnjhe->nbhde', k_dec, vc)
    chunk_decay = jnp.exp(CHUNK * log_g)[None, :, None, None]

    def step(state, kv_n):
        return chunk_decay * state + kv_n, state

    _, prev = lax.scan(step, jnp.zeros((b, H, dk, dv), jnp.float32), kv)
    q_dec = qc * jnp.exp((idx + 1.0)[:, None] * log_g[None, :])[:, :, None]
    o_cross = jnp.einsum('bnihd,nbhde->bnihe', q_dec, prev)
    return (o_inner + o_cross).reshape(b, P, H, dv)


def head_groupnorm(o, g):
    mu = jnp.mean(o, axis=-1, keepdims=True)
    var = jnp.mean(jnp.square(o - mu), axis=-1, keepdims=True)
    return (o - mu) * lax.rsqrt(var + EPS) * g.astype(jnp.float32)


def conformer_conv(u, conv_w, conv_b, ln_g, ln_b):
    a, gate = jnp.split(u, 2, axis=-1)
    hdn = a * jax.nn.sigmoid(gate)
    y = lax.conv_general_dilated(
        hdn, conv_w[:, None, :].astype(hdn.dtype), window_strides=(1,),
        padding=[(CONV_WIDTH - 1, 0)], dimension_numbers=('NWC', 'WIO', 'NWC'),
        feature_group_count=CONV_CH)
    y = y + conv_b.astype(y.dtype)
    return jax.nn.silu(layernorm(y, ln_g, ln_b))


def even_mixer(h, w_in, gn_g, conv_w, conv_b, ln_g, ln_b, w_out):
    b, L, _ = h.shape
    proj = h @ w_in.astype(h.dtype)
    q, k, v, g, u = jnp.split(proj, [RET_QK_W, 2 * RET_QK_W, 2 * RET_QK_W + RET_V_W,
                                     2 * RET_QK_W + 2 * RET_V_W], axis=-1)
    pad = ((0, 0), (PAD_FRONT, 0), (0, 0), (0, 0))
    q = jnp.pad(q.astype(jnp.float32).reshape(b, L, RET_HEADS, RET_QK_DIM), pad)
    k = jnp.pad(k.astype(jnp.float32).reshape(b, L, RET_HEADS, RET_QK_DIM), pad)
    v = jnp.pad(v.astype(jnp.float32).reshape(b, L, RET_HEADS, RET_V_DIM), pad)
    q = rotary(q)
    k = rotary(k) * (RET_QK_DIM ** -0.5)
    o = retention_chunkwise(q, k, v)[:, PAD_FRONT:]
    o = head_groupnorm(o, gn_g).reshape(b, L, RET_V_W).astype(h.dtype)
    o = jax.nn.silu(g) * o
    c = conformer_conv(u, conv_w, conv_b, ln_g, ln_b)
    return jnp.concatenate([o, c], axis=-1) @ w_out.astype(h.dtype)


def stick_breaking(q, k, v, n_pad):
    b, H, P, d = q.shape
    n = P // CHUNK
    scale = d ** -0.5
    key_pos = jnp.arange(P)

    def block(i):
        qb = lax.dynamic_slice_in_dim(q, i * CHUNK, CHUNK, axis=2)
        z = jnp.einsum('bhqd,bhkd->bhqk', qb, k) * scale
        q_pos = i * CHUNK + jnp.arange(CHUNK)
        valid = (key_pos[None, :] < q_pos[:, None]) & (key_pos[None, :] >= n_pad)
        log_keep = jnp.where(valid, jax.nn.log_sigmoid(-z), 0.0)
        after = lax.cumsum(log_keep, axis=3, reverse=True) - log_keep
        w = jnp.where(valid, jnp.exp(jax.nn.log_sigmoid(z) + after), 0.0)
        return jnp.einsum('bhqk,bhkd->bhqd', w, v)

    out = lax.map(block, jnp.arange(n))
    return jnp.transpose(out, (1, 0, 3, 2, 4)).reshape(b, P, H, d)


def odd_mixer(h, w_qkv, qn_g, kn_g, w_o):
    b, L, _ = h.shape
    qkv = h @ w_qkv.astype(h.dtype)
    q, k, v = jnp.split(qkv, 3, axis=-1)
    q = rmsnorm(q.reshape(b, L, SB_HEADS, SB_HEAD_DIM), qn_g)
    k = rmsnorm(k.reshape(b, L, SB_HEADS, SB_HEAD_DIM), kn_g)
    v = v.reshape(b, L, SB_HEADS, SB_HEAD_DIM)
    pad = ((0, 0), (PAD_FRONT, 0), (0, 0), (0, 0))
    to_bhpd = lambda t: jnp.transpose(jnp.pad(t.astype(jnp.float32), pad), (0, 2, 1, 3))
    o = stick_breaking(to_bhpd(q), to_bhpd(k), to_bhpd(v), PAD_FRONT)[:, PAD_FRONT:]
    o = o.reshape(b, L, D_MODEL).astype(h.dtype)
    return o @ w_o.astype(h.dtype)


def sq_relu_mlp(h, w1, w2):
    return jnp.square(jax.nn.relu(h @ w1.astype(h.dtype))) @ w2.astype(h.dtype)


def _fwd_setup_inputs(seed: int = 0) -> dict:
    key = jax.random.key(seed)
    ks = jax.random.split(key, 17)
    nrm = lambda kk, shape, s: jax.random.normal(kk, shape, jnp.float32) * s
    return {
        "x": nrm(ks[0], (BATCH, SEQ, D_MODEL), 1.0),
        "meta": nrm(ks[1], (N_META, D_MODEL), 1.0),
        "norm_mix_g": 1.0 + nrm(ks[2], (DEPTH, D_MODEL), 0.02),
        "norm_mlp_g": 1.0 + nrm(ks[3], (DEPTH, D_MODEL), 0.02),
        "even_w_in": nrm(ks[4], (N_EVEN, D_MODEL, MIX_IN_EVEN), D_MODEL ** -0.5),
        "even_ret_gn_g": 1.0 + nrm(ks[5], (N_EVEN, RET_HEADS, RET_V_DIM), 0.02),
        "even_conv_w": nrm(ks[6], (N_EVEN, CONV_WIDTH, CONV_CH), CONV_WIDTH ** -0.5),
        "even_conv_b": nrm(ks[7], (N_EVEN, CONV_CH), 0.01),
        "even_conv_ln_g": 1.0 + nrm(ks[8], (N_EVEN, CONV_CH), 0.02),
        "even_conv_ln_b": nrm(ks[9], (N_EVEN, CONV_CH), 0.01),
        "even_w_out": nrm(ks[10], (N_EVEN, MIX_OUT_EVEN, D_MODEL), MIX_OUT_EVEN ** -0.5),
        "odd_w_qkv": nrm(ks[11], (N_ODD, D_MODEL, 3 * D_MODEL), D_MODEL ** -0.5),
        "odd_q_norm_g": 1.0 + nrm(ks[12], (N_ODD, SB_HEAD_DIM), 0.02),
        "odd_k_norm_g": 1.0 + nrm(ks[13], (N_ODD, SB_HEAD_DIM), 0.02),
        "odd_w_o": nrm(ks[14], (N_ODD, D_MODEL, D_MODEL), D_MODEL ** -0.5),
        "mlp_w1": nrm(ks[15], (DEPTH, D_MODEL, D_FF), D_MODEL ** -0.5),
        "mlp_w2": nrm(ks[16], (DEPTH, D_FF, D_MODEL), D_FF ** -0.5),
    }


def _fwd_reference(x, meta, norm_mix_g, norm_mlp_g, even_w_in, even_ret_gn_g, even_conv_w,
              even_conv_b, even_conv_ln_g, even_conv_ln_b, even_w_out, odd_w_qkv,
              odd_q_norm_g, odd_k_norm_g, odd_w_o, mlp_w1, mlp_w2):
    b = x.shape[0]
    meta_b = jnp.broadcast_to(meta[None].astype(x.dtype), (b, N_META, D_MODEL))
    h = jnp.concatenate([meta_b, x], axis=1)
    for layer in range(DEPTH):
        j = layer // 2
        hn = rmsnorm(h, norm_mix_g[layer])
        if layer % 2 == 0:
            mix = even_mixer(hn, even_w_in[j], even_ret_gn_g[j], even_conv_w[j], even_conv_b[j],
                             even_conv_ln_g[j], even_conv_ln_b[j], even_w_out[j])
        else:
            mix = odd_mixer(hn, odd_w_qkv[j], odd_q_norm_g[j], odd_k_norm_g[j], odd_w_o[j])
        h = h + mix
        h = h + sq_relu_mlp(rmsnorm(h, norm_mlp_g[layer]), mlp_w1[layer], mlp_w2[layer])
    return h[:, N_META:]


import jax as _jax
import jax.numpy as _jnp

TWIN_FORMAT = 'train_step'
FWD_PARAMS = ['x', 'meta', 'norm_mix_g', 'norm_mlp_g', 'even_w_in', 'even_ret_gn_g', 'even_conv_w', 'even_conv_b', 'even_conv_ln_g', 'even_conv_ln_b', 'even_w_out', 'odd_w_qkv', 'odd_q_norm_g', 'odd_k_norm_g', 'odd_w_o', 'mlp_w1', 'mlp_w2']
TWIN_WEIGHTS = ['meta', 'norm_mix_g', 'norm_mlp_g', 'even_w_in', 'even_ret_gn_g', 'even_conv_w', 'even_conv_b', 'even_conv_ln_g', 'even_conv_ln_b', 'even_w_out', 'odd_w_qkv', 'odd_q_norm_g', 'odd_k_norm_g', 'odd_w_o', 'mlp_w1', 'mlp_w2']
TWIN_DIFF_INPUT = 'x'
TWIN_INPUTS = ['x', 'meta', 'norm_mix_g', 'norm_mlp_g', 'even_w_in', 'even_ret_gn_g', 'even_conv_w', 'even_conv_b', 'even_conv_ln_g', 'even_conv_ln_b', 'even_w_out', 'odd_w_qkv', 'odd_q_norm_g', 'odd_k_norm_g', 'odd_w_o', 'mlp_w1', 'mlp_w2', 'loss_target', 'm_meta', 'm_norm_mix_g', 'm_norm_mlp_g', 'm_even_w_in', 'm_even_ret_gn_g', 'm_even_conv_w', 'm_even_conv_b', 'm_even_conv_ln_g', 'm_even_conv_ln_b', 'm_even_w_out', 'm_odd_w_qkv', 'm_odd_q_norm_g', 'm_odd_k_norm_g', 'm_odd_w_o', 'm_mlp_w1', 'm_mlp_w2', 'v_meta', 'v_norm_mix_g', 'v_norm_mlp_g', 'v_even_w_in', 'v_even_ret_gn_g', 'v_even_conv_w', 'v_even_conv_b', 'v_even_conv_ln_g', 'v_even_conv_ln_b', 'v_even_w_out', 'v_odd_w_qkv', 'v_odd_q_norm_g', 'v_odd_k_norm_g', 'v_odd_w_o', 'v_mlp_w1', 'v_mlp_w2']
TWIN_OUTPUTS = ['loss', 'grad_x', 'grad_meta', 'grad_norm_mix_g', 'grad_norm_mlp_g', 'grad_even_w_in', 'grad_even_ret_gn_g', 'grad_even_conv_w', 'grad_even_conv_b', 'grad_even_conv_ln_g', 'grad_even_conv_ln_b', 'grad_even_w_out', 'grad_odd_w_qkv', 'grad_odd_q_norm_g', 'grad_odd_k_norm_g', 'grad_odd_w_o', 'grad_mlp_w1', 'grad_mlp_w2', 'delta_meta', 'delta_norm_mix_g', 'delta_norm_mlp_g', 'delta_even_w_in', 'delta_even_ret_gn_g', 'delta_even_conv_w', 'delta_even_conv_b', 'delta_even_conv_ln_g', 'delta_even_conv_ln_b', 'delta_even_w_out', 'delta_odd_w_qkv', 'delta_odd_q_norm_g', 'delta_odd_k_norm_g', 'delta_odd_w_o', 'delta_mlp_w1', 'delta_mlp_w2', 'new_m_meta', 'new_m_norm_mix_g', 'new_m_norm_mlp_g', 'new_m_even_w_in', 'new_m_even_ret_gn_g', 'new_m_even_conv_w', 'new_m_even_conv_b', 'new_m_even_conv_ln_g', 'new_m_even_conv_ln_b', 'new_m_even_w_out', 'new_m_odd_w_qkv', 'new_m_odd_q_norm_g', 'new_m_odd_k_norm_g', 'new_m_odd_w_o', 'new_m_mlp_w1', 'new_m_mlp_w2', 'new_v_meta', 'new_v_norm_mix_g', 'new_v_norm_mlp_g', 'new_v_even_w_in', 'new_v_even_ret_gn_g', 'new_v_even_conv_w', 'new_v_even_conv_b', 'new_v_even_conv_ln_g', 'new_v_even_conv_ln_b', 'new_v_even_w_out', 'new_v_odd_w_qkv', 'new_v_odd_q_norm_g', 'new_v_odd_k_norm_g', 'new_v_odd_w_o', 'new_v_mlp_w1', 'new_v_mlp_w2']
TWIN_LEAF_KINDS = {'loss': 'loss', 'grad_x': 'grad_x', 'grad_meta': 'grad_w', 'grad_norm_mix_g': 'grad_w', 'grad_norm_mlp_g': 'grad_w', 'grad_even_w_in': 'grad_w', 'grad_even_ret_gn_g': 'grad_w', 'grad_even_conv_w': 'grad_w', 'grad_even_conv_b': 'grad_w', 'grad_even_conv_ln_g': 'grad_w', 'grad_even_conv_ln_b': 'grad_w', 'grad_even_w_out': 'grad_w', 'grad_odd_w_qkv': 'grad_w', 'grad_odd_q_norm_g': 'grad_w', 'grad_odd_k_norm_g': 'grad_w', 'grad_odd_w_o': 'grad_w', 'grad_mlp_w1': 'grad_w', 'grad_mlp_w2': 'grad_w', 'delta_meta': 'delta_w', 'delta_norm_mix_g': 'delta_w', 'delta_norm_mlp_g': 'delta_w', 'delta_even_w_in': 'delta_w', 'delta_even_ret_gn_g': 'delta_w', 'delta_even_conv_w': 'delta_w', 'delta_even_conv_b': 'delta_w', 'delta_even_conv_ln_g': 'delta_w', 'delta_even_conv_ln_b': 'delta_w', 'delta_even_w_out': 'delta_w', 'delta_odd_w_qkv': 'delta_w', 'delta_odd_q_norm_g': 'delta_w', 'delta_odd_k_norm_g': 'delta_w', 'delta_odd_w_o': 'delta_w', 'delta_mlp_w1': 'delta_w', 'delta_mlp_w2': 'delta_w', 'new_m_meta': 'new_m', 'new_m_norm_mix_g': 'new_m', 'new_m_norm_mlp_g': 'new_m', 'new_m_even_w_in': 'new_m', 'new_m_even_ret_gn_g': 'new_m', 'new_m_even_conv_w': 'new_m', 'new_m_even_conv_b': 'new_m', 'new_m_even_conv_ln_g': 'new_m', 'new_m_even_conv_ln_b': 'new_m', 'new_m_even_w_out': 'new_m', 'new_m_odd_w_qkv': 'new_m', 'new_m_odd_q_norm_g': 'new_m', 'new_m_odd_k_norm_g': 'new_m', 'new_m_odd_w_o': 'new_m', 'new_m_mlp_w1': 'new_m', 'new_m_mlp_w2': 'new_m', 'new_v_meta': 'new_v', 'new_v_norm_mix_g': 'new_v', 'new_v_norm_mlp_g': 'new_v', 'new_v_even_w_in': 'new_v', 'new_v_even_ret_gn_g': 'new_v', 'new_v_even_conv_w': 'new_v', 'new_v_even_conv_b': 'new_v', 'new_v_even_conv_ln_g': 'new_v', 'new_v_even_conv_ln_b': 'new_v', 'new_v_even_w_out': 'new_v', 'new_v_odd_w_qkv': 'new_v', 'new_v_odd_q_norm_g': 'new_v', 'new_v_odd_k_norm_g': 'new_v', 'new_v_odd_w_o': 'new_v', 'new_v_mlp_w1': 'new_v', 'new_v_mlp_w2': 'new_v'}


def _forward(args):
    return _fwd_reference(*[args[k] for k in FWD_PARAMS])


def _output_shape():
    out = _jax.eval_shape(lambda: _forward(_fwd_setup_inputs(0)))
    return out.shape, out.dtype

N_MICROBATCH = 1
ADAM_LR = 0.001
ADAM_B1 = 0.9
ADAM_B2 = 0.999
ADAM_EPS = 1e-08
ADAM_WD = 0.01
ADAM_STEP = 10
PER_EXAMPLE_BATCH_AXIS = {'x': 0, 'loss_target': 0}
SHARED_INPUTS = []
_WEIGHT_DTYPES = {'meta': _jnp.float32, 'norm_mix_g': _jnp.float32, 'norm_mlp_g': _jnp.float32, 'even_w_in': _jnp.float32, 'even_ret_gn_g': _jnp.float32, 'even_conv_w': _jnp.float32, 'even_conv_b': _jnp.float32, 'even_conv_ln_g': _jnp.float32, 'even_conv_ln_b': _jnp.float32, 'even_w_out': _jnp.float32, 'odd_w_qkv': _jnp.float32, 'odd_q_norm_g': _jnp.float32, 'odd_k_norm_g': _jnp.float32, 'odd_w_o': _jnp.float32, 'mlp_w1': _jnp.float32, 'mlp_w2': _jnp.float32}
MOMENT_SCALE = {'meta': 6.500965e-02, 'norm_mix_g': 1.343244e+01, 'norm_mlp_g': 9.677246e+01, 'even_w_in': 5.371582e-01, 'even_ret_gn_g': 5.170500e+00, 'even_conv_w': 1.848176e+00, 'even_conv_b': 3.609424e+01, 'even_conv_ln_g': 1.540070e+01, 'even_conv_ln_b': 2.023233e+01, 'even_w_out': 7.684012e+00, 'odd_w_qkv': 6.416081e+00, 'odd_q_norm_g': 2.823623e+01, 'odd_k_norm_g': 2.820458e+01, 'odd_w_o': 1.046454e+01, 'mlp_w1': 4.751436e+00, 'mlp_w2': 1.829539e+01}


def _to_microbatches(a, axis):
    t = _jnp.moveaxis(a, axis, 0)
    t = t.reshape((N_MICROBATCH, t.shape[0] // N_MICROBATCH) + t.shape[1:])
    return _jnp.moveaxis(t, 1, axis + 1)


def setup_inputs(seed: int = 0) -> dict:
    inp = _fwd_setup_inputs(seed)
    key = _jax.random.fold_in(_jax.random.key(seed), 7919)
    shape, _ = _output_shape()
    out = dict(inp)
    out["loss_target"] = _jax.random.normal(_jax.random.fold_in(key, 0), shape, _jnp.float32)
    for i, name in enumerate(TWIN_WEIGHTS):
        w = inp[name].astype(_jnp.float32)
        if MOMENT_SCALE is None:
            s = _jnp.sqrt(_jnp.mean(_jnp.square(w)) + 1e-30)
        else:
            s = MOMENT_SCALE[name]
        km, kv = _jax.random.split(_jax.random.fold_in(key, i + 1))
        out[name] = w
        out["m_" + name] = s * _jax.random.normal(km, w.shape, _jnp.float32)
        out["v_" + name] = (s * s) * _jax.random.uniform(kv, w.shape, _jnp.float32, 0.5, 1.5)
    if N_MICROBATCH > 1:
        for name, axis in PER_EXAMPLE_BATCH_AXIS.items():
            out[name] = _to_microbatches(out[name], axis)
    return {'x': out['x'], 'meta': out['meta'], 'norm_mix_g': out['norm_mix_g'], 'norm_mlp_g': out['norm_mlp_g'], 'even_w_in': out['even_w_in'], 'even_ret_gn_g': out['even_ret_gn_g'], 'even_conv_w': out['even_conv_w'], 'even_conv_b': out['even_conv_b'], 'even_conv_ln_g': out['even_conv_ln_g'], 'even_conv_ln_b': out['even_conv_ln_b'], 'even_w_out': out['even_w_out'], 'odd_w_qkv': out['odd_w_qkv'], 'odd_q_norm_g': out['odd_q_norm_g'], 'odd_k_norm_g': out['odd_k_norm_g'], 'odd_w_o': out['odd_w_o'], 'mlp_w1': out['mlp_w1'], 'mlp_w2': out['mlp_w2'], 'loss_target': out['loss_target'], 'm_meta': out['m_meta'], 'm_norm_mix_g': out['m_norm_mix_g'], 'm_norm_mlp_g': out['m_norm_mlp_g'], 'm_even_w_in': out['m_even_w_in'], 'm_even_ret_gn_g': out['m_even_ret_gn_g'], 'm_even_conv_w': out['m_even_conv_w'], 'm_even_conv_b': out['m_even_conv_b'], 'm_even_conv_ln_g': out['m_even_conv_ln_g'], 'm_even_conv_ln_b': out['m_even_conv_ln_b'], 'm_even_w_out': out['m_even_w_out'], 'm_odd_w_qkv': out['m_odd_w_qkv'], 'm_odd_q_norm_g': out['m_odd_q_norm_g'], 'm_odd_k_norm_g': out['m_odd_k_norm_g'], 'm_odd_w_o': out['m_odd_w_o'], 'm_mlp_w1': out['m_mlp_w1'], 'm_mlp_w2': out['m_mlp_w2'], 'v_meta': out['v_meta'], 'v_norm_mix_g': out['v_norm_mix_g'], 'v_norm_mlp_g': out['v_norm_mlp_g'], 'v_even_w_in': out['v_even_w_in'], 'v_even_ret_gn_g': out['v_even_ret_gn_g'], 'v_even_conv_w': out['v_even_conv_w'], 'v_even_conv_b': out['v_even_conv_b'], 'v_even_conv_ln_g': out['v_even_conv_ln_g'], 'v_even_conv_ln_b': out['v_even_conv_ln_b'], 'v_even_w_out': out['v_even_w_out'], 'v_odd_w_qkv': out['v_odd_w_qkv'], 'v_odd_q_norm_g': out['v_odd_q_norm_g'], 'v_odd_k_norm_g': out['v_odd_k_norm_g'], 'v_odd_w_o': out['v_odd_w_o'], 'v_mlp_w1': out['v_mlp_w1'], 'v_mlp_w2': out['v_mlp_w2']}


def _loss(weights, diff, rest, loss_target):
    with _jax.named_scope("forward"):
        args = {**rest, TWIN_DIFF_INPUT: diff, **{k: w.astype(_WEIGHT_DTYPES[k]) for k, w in weights.items()}}
        y = _forward(args)
    with _jax.named_scope("loss_head"):
        err = _jnp.square(y.astype(_jnp.float32) - loss_target)
        return 0.5 * _jnp.sum(_jnp.mean(err, axis=-1)) if err.ndim else 0.5 * err


def _adamw(w, g, m, v):
    m = ADAM_B1 * m + (1.0 - ADAM_B1) * g
    v = ADAM_B2 * v + (1.0 - ADAM_B2) * _jnp.square(g)
    m_hat = m / (1.0 - ADAM_B1 ** ADAM_STEP)
    v_hat = v / (1.0 - ADAM_B2 ** ADAM_STEP)
    delta = -ADAM_LR * (m_hat / (_jnp.sqrt(v_hat) + ADAM_EPS) + ADAM_WD * w)
    return delta, m, v


def reference(x, meta, norm_mix_g, norm_mlp_g, even_w_in, even_ret_gn_g, even_conv_w, even_conv_b, even_conv_ln_g, even_conv_ln_b, even_w_out, odd_w_qkv, odd_q_norm_g, odd_k_norm_g, odd_w_o, mlp_w1, mlp_w2, loss_target, m_meta, m_norm_mix_g, m_norm_mlp_g, m_even_w_in, m_even_ret_gn_g, m_even_conv_w, m_even_conv_b, m_even_conv_ln_g, m_even_conv_ln_b, m_even_w_out, m_odd_w_qkv, m_odd_q_norm_g, m_odd_k_norm_g, m_odd_w_o, m_mlp_w1, m_mlp_w2, v_meta, v_norm_mix_g, v_norm_mlp_g, v_even_w_in, v_even_ret_gn_g, v_even_conv_w, v_even_conv_b, v_even_conv_ln_g, v_even_conv_ln_b, v_even_w_out, v_odd_w_qkv, v_odd_q_norm_g, v_odd_k_norm_g, v_odd_w_o, v_mlp_w1, v_mlp_w2):
    given = dict(x=x, meta=meta, norm_mix_g=norm_mix_g, norm_mlp_g=norm_mlp_g, even_w_in=even_w_in, even_ret_gn_g=even_ret_gn_g, even_conv_w=even_conv_w, even_conv_b=even_conv_b, even_conv_ln_g=even_conv_ln_g, even_conv_ln_b=even_conv_ln_b, even_w_out=even_w_out, odd_w_qkv=odd_w_qkv, odd_q_norm_g=odd_q_norm_g, odd_k_norm_g=odd_k_norm_g, odd_w_o=odd_w_o, mlp_w1=mlp_w1, mlp_w2=mlp_w2, loss_target=loss_target, m_meta=m_meta, m_norm_mix_g=m_norm_mix_g, m_norm_mlp_g=m_norm_mlp_g, m_even_w_in=m_even_w_in, m_even_ret_gn_g=m_even_ret_gn_g, m_even_conv_w=m_even_conv_w, m_even_conv_b=m_even_conv_b, m_even_conv_ln_g=m_even_conv_ln_g, m_even_conv_ln_b=m_even_conv_ln_b, m_even_w_out=m_even_w_out, m_odd_w_qkv=m_odd_w_qkv, m_odd_q_norm_g=m_odd_q_norm_g, m_odd_k_norm_g=m_odd_k_norm_g, m_odd_w_o=m_odd_w_o, m_mlp_w1=m_mlp_w1, m_mlp_w2=m_mlp_w2, v_meta=v_meta, v_norm_mix_g=v_norm_mix_g, v_norm_mlp_g=v_norm_mlp_g, v_even_w_in=v_even_w_in, v_even_ret_gn_g=v_even_ret_gn_g, v_even_conv_w=v_even_conv_w, v_even_conv_b=v_even_conv_b, v_even_conv_ln_g=v_even_conv_ln_g, v_even_conv_ln_b=v_even_conv_ln_b, v_even_w_out=v_even_w_out, v_odd_w_qkv=v_odd_w_qkv, v_odd_q_norm_g=v_odd_q_norm_g, v_odd_k_norm_g=v_odd_k_norm_g, v_odd_w_o=v_odd_w_o, v_mlp_w1=v_mlp_w1, v_mlp_w2=v_mlp_w2)
    weights = {n: given[n] for n in TWIN_WEIGHTS}
    shared = {n: given[n] for n in SHARED_INPUTS}
    per_example = {n: given[n] for n in ['x']}
    grad_fn = _jax.value_and_grad(_loss, argnums=(0, 1))

    def one_microbatch(ex, loss_target):
        ex = dict(ex)
        diff = ex.pop(TWIN_DIFF_INPUT)
        return grad_fn(weights, diff, {**shared, **ex}, loss_target)

    if N_MICROBATCH == 1:
        loss, (grad_w, grad_x) = one_microbatch(per_example, given["loss_target"])
    else:
        def body(carry, xs):
            loss_sum, grad_sum = carry
            l_k, (gw_k, gx_k) = one_microbatch(xs[0], xs[1])
            with _jax.named_scope("update"):
                return (loss_sum + l_k, _jax.tree.map(_jnp.add, grad_sum, gw_k)), gx_k

        init = (_jnp.zeros((), _jnp.float32), _jax.tree.map(_jnp.zeros_like, weights))
        (loss, grad_w), grad_x = _jax.lax.scan(body, init, (per_example, given["loss_target"]))
    with _jax.named_scope("update"):
        delta_w, new_m, new_v = {}, {}, {}
        for n in TWIN_WEIGHTS:
            delta_w[n], new_m[n], new_v[n] = _adamw(weights[n], grad_w[n], given["m_" + n], given["v_" + n])
    return (loss, grad_x, *[grad_w[n] for n in TWIN_WEIGHTS], *[delta_w[n] for n in TWIN_WEIGHTS],
            *[new_m[n] for n in TWIN_WEIGHTS], *[new_v[n] for n in TWIN_WEIGHTS])
```

```python
import functools

import jax
import jax.numpy as jnp
from jax import lax
from jax.experimental import pallas as pl
from jax.experimental.pallas import tpu as pltpu

F32 = jnp.float32
BF16 = jnp.bfloat16

D_MODEL = 1024
N_META = 16
CHUNK = 128
PAD_FRONT = (-N_META) % CHUNK
RET_HEADS = 4
RET_QK_DIM = 128
RET_V_DIM = 256
RET_QK_W = RET_HEADS * RET_QK_DIM
RET_V_W = RET_HEADS * RET_V_DIM
CONV_WIDTH = 31
CONV_HALO = 32
RET_DECAY_OFFSET = 5.0
ROPE_BASE = 10000.0
SB_HEADS = 16
SB_HEAD_DIM = 64
D_FF = 4 * D_MODEL
EPS = 1e-6
ADAM_LR = 0.001
ADAM_B1 = 0.9
ADAM_B2 = 0.999
ADAM_EPS = 1e-08
ADAM_WD = 0.01
ADAM_STEP = 10

N_CHIPS = 4
N_DEV = 8
VMEM_LIMIT = 56 * 1024 * 1024
MESH = pl.DeviceIdType.MESH
ANY = pl.BlockSpec(memory_space=pl.ANY)


def _params(*sem):
    return pltpu.CompilerParams(dimension_semantics=sem, vmem_limit_bytes=VMEM_LIMIT)


def _pick(n, cands):
    for c in cands:
        if n % c == 0:
            return c
    return n


def _sigmoid(x):
    return 1.0 / (1.0 + jnp.exp(-x))


def _dot(a, b):
    return lax.dot_general(a, b, (((1,), (0,)), ((), ())), preferred_element_type=F32)


def _dot_nt(a, b):
    return lax.dot_general(a, b, (((1,), (1,)), ((), ())), preferred_element_type=F32)


def _dot_tn(a, b):
    return lax.dot_general(a, b, (((0,), (0,)), ((), ())), preferred_element_type=F32)


def _split_dot(x, m):
    hi = x.astype(BF16)
    lo = (x - hi.astype(F32)).astype(BF16)
    return _dot(hi, m) + _dot(lo, m)


def _matmul(a, b, *, mode, out_dtypes, epilogue=None, extras=(), name):
    if mode == "nn":
        (m, k), (k2, n) = a.shape, b.shape
    elif mode == "nt":
        (m, k), (n, k2) = a.shape, b.shape
    else:
        (k, m), (k2, n) = a.shape, b.shape
    assert k == k2, (a.shape, b.shape, mode)
    tm = _pick(m, (1056, 1024, 768, 512, 384, 256, 128, 96))
    tn = _pick(n, (1024, 768, 512, 256, 128))
    tk = _pick(k, (1056, 1024, 768, 512, 384, 256, 128, 96))
    nk = k // tk
    dot = {"nn": _dot, "nt": _dot_nt, "tn": _dot_tn}[mode]
    n_extra, n_out = len(extras), len(out_dtypes)
    if epilogue is None:
        epilogue = lambda acc: (acc,)

    def body(a_ref, b_ref, *rest):
        extra_refs = rest[:n_extra]
        out_refs = rest[n_extra:n_extra + n_out]
        part = dot(a_ref[...].astype(BF16), b_ref[...].astype(BF16))

        def finish(acc):
            res = epilogue(acc, *[r[...] for r in extra_refs])
            for o_ref, r in zip(out_refs, res):
                o_ref[...] = r.astype(o_ref.dtype)

        if nk == 1:
            finish(part)
        else:
            acc_ref = rest[-1]
            kk = pl.program_id(2)

            @pl.when(kk == 0)
            def _():
                acc_ref[...] = part

            @pl.when(kk > 0)
            def _():
                acc_ref[...] += part

            @pl.when(kk == nk - 1)
            def _():
                finish(acc_ref[...])

    if mode == "nn":
        a_spec = pl.BlockSpec((tm, tk), lambda i, j, kk: (i, kk))
        b_spec = pl.BlockSpec((tk, tn), lambda i, j, kk: (kk, j))
    elif mode == "nt":
        a_spec = pl.BlockSpec((tm, tk), lambda i, j, kk: (i, kk))
        b_spec = pl.BlockSpec((tn, tk), lambda i, j, kk: (j, kk))
    else:
        a_spec = pl.BlockSpec((tk, tm), lambda i, j, kk: (kk, i))
        b_spec = pl.BlockSpec((tk, tn), lambda i, j, kk: (kk, j))
    o_spec = pl.BlockSpec((tm, tn), lambda i, j, kk: (i, j))
    outs = pl.pallas_call(
        body,
        out_shape=[jax.ShapeDtypeStruct((m, n), dt) for dt in out_dtypes],
        grid=(m // tm, n // tn, nk),
        in_specs=[a_spec, b_spec] + [o_spec] * n_extra,
        out_specs=[o_spec] * n_out,
        scratch_shapes=[pltpu.VMEM((tm, tn), F32)] if nk > 1 else [],
        compiler_params=_params("parallel", "parallel", "arbitrary"),
        name=name,
    )(a, b, *extras)
    return outs[0] if n_out == 1 else outs


def _add_epilogue(acc, res):
    return (res + acc,)


def _rmsnorm_fwd(x, g, *, name):
    p, d = x.shape
    rows = _pick(p, (384, 128, 96))

    def body(x_ref, g_ref, o_ref):
        xv = x_ref[...]
        r = lax.rsqrt(jnp.mean(xv * xv, axis=-1, keepdims=True) + EPS)
        o_ref[...] = (xv * r * g_ref[...]).astype(o_ref.dtype)

    return pl.pallas_call(
        body,
        out_shape=jax.ShapeDtypeStruct((p, d), BF16),
        grid=(p // rows,),
        in_specs=[pl.BlockSpec((rows, d), lambda i: (i, 0)), pl.BlockSpec((1, d), lambda i: (0, 0))],
        out_specs=pl.BlockSpec((rows, d), lambda i: (i, 0)),
        compiler_params=_params("parallel"),
        name=name,
    )(x, g)


def _rmsnorm_bwd(x, g, dy, dres, *, name):
    p, d = x.shape
    rows = _pick(p, (384, 128, 96))

    def body(x_ref, g_ref, dy_ref, dres_ref, dx_ref, dg_ref):
        xv = x_ref[...]
        r = lax.rsqrt(jnp.mean(xv * xv, axis=-1, keepdims=True) + EPS)
        dyv = dy_ref[...]
        gdy = dyv * g_ref[...]
        proj = jnp.mean(xv * gdy, axis=-1, keepdims=True)
        dx_ref[...] = dres_ref[...] + r * gdy - xv * (r * r * r) * proj
        part = jnp.sum(dyv * xv * r, axis=0, keepdims=True)

        @pl.when(pl.program_id(0) == 0)
        def _():
            dg_ref[...] = part

        @pl.when(pl.program_id(0) > 0)
        def _():
            dg_ref[...] += part

    row_spec = pl.BlockSpec((rows, d), lambda i: (i, 0))
    vec_spec = pl.BlockSpec((1, d), lambda i: (0, 0))
    return pl.pallas_call(
        body,
        out_shape=[jax.ShapeDtypeStruct((p, d), F32), jax.ShapeDtypeStruct((1, d), F32)],
        grid=(p // rows,),
        in_specs=[row_spec, vec_spec, row_spec, row_spec],
        out_specs=[row_spec, vec_spec],
        compiler_params=_params("arbitrary"),
        name=name,
    )(x, g, dy, dres)


def _mlp_fwd(h, g, w1, w2, *, name):
    hn = _rmsnorm_fwd(h, g, name=name + "_norm")

    def act(acc):
        r = jnp.maximum(acc, 0.0)
        return acc, r * r

    z, a2 = _matmul(hn, w1, mode="nn", out_dtypes=(F32, BF16), epilogue=act, name=name + "_up")
    out = _matmul(a2, w2, mode="nn", out_dtypes=(F32,), epilogue=_add_epilogue, extras=(h,), name=name + "_down")
    return out, (hn, z, a2)


def _mlp_bwd(h, g, w1, w2, saved, dout, *, name):
    hn, z, a2 = saved

    def dact(acc, zt):
        return (acc * (2.0 * jnp.maximum(zt, 0.0)),)

    dz = _matmul(dout, w2, mode="nt", out_dtypes=(BF16,), epilogue=dact, extras=(z,), name=name + "_dz")
    dw2 = _matmul(a2, dout, mode="tn", out_dtypes=(F32,), name=name + "_dw2")
    dw1 = _matmul(hn, dz, mode="tn", out_dtypes=(F32,), name=name + "_dw1")
    dhn = _matmul(dz, w1, mode="nt", out_dtypes=(F32,), name=name + "_dhn")
    dh, dg = _rmsnorm_bwd(h, g, dhn, dout, name=name + "_dnorm")
    return dh, dg, dw1, dw2


def _retention_tables(p):
    half = RET_QK_DIM // 2
    inv_freq = ROPE_BASE ** (-jnp.arange(half, dtype=F32) / half)
    ang = jnp.arange(p, dtype=F32)[:, None] * inv_freq[None, :]
    cos, sin = jnp.cos(ang), jnp.sin(ang)
    cosf = jnp.concatenate([cos, cos], axis=1)
    sins = jnp.concatenate([-sin, sin], axis=1)
    log_g = jnp.log1p(-jnp.exp2(-RET_DECAY_OFFSET - jnp.arange(RET_HEADS, dtype=F32)))
    idx = jnp.arange(CHUNK, dtype=F32)
    diff = idx[:, None] - idx[None, :]
    inner = jnp.where(diff[None] >= 0, jnp.exp(jnp.maximum(diff, 0.0)[None] * log_g[:, None, None]), 0.0)
    kdec = jnp.exp((CHUNK - 1 - idx)[None, :] * log_g[:, None])
    qdec = jnp.exp((idx + 1.0)[None, :] * log_g[:, None])
    cdec = jnp.exp(CHUNK * log_g)
    kdec = jnp.broadcast_to(kdec[:, :, None], (RET_HEADS, CHUNK, RET_QK_DIM))
    qdec = jnp.broadcast_to(qdec[:, :, None], (RET_HEADS, CHUNK, RET_QK_DIM))
    cdec = jnp.broadcast_to(cdec[:, None, None], (RET_HEADS, RET_QK_DIM, RET_V_DIM))
    return cosf, sins, inner, kdec, qdec, cdec


def _rot(x, cosf, sins):
    return x * cosf + pltpu.roll(x, RET_QK_DIM // 2, 1) * sins


def _rot_bwd(dy, cosf, sins):
    return dy * cosf + pltpu.roll(dy * sins, RET_QK_DIM // 2, 1)


def _ret_in_specs(chunk_of):
    nh = RET_HEADS
    q_spec = pl.BlockSpec((CHUNK, RET_QK_DIM), lambda h, s: (chunk_of(s), h))
    k_spec = pl.BlockSpec((CHUNK, RET_QK_DIM), lambda h, s: (chunk_of(s), nh + h))
    v_spec = pl.BlockSpec((CHUNK, RET_V_DIM), lambda h, s: (chunk_of(s), nh + h))
    g_spec = pl.BlockSpec((CHUNK, RET_V_DIM), lambda h, s: (chunk_of(s), 2 * nh + h))
    rope_spec = pl.BlockSpec((CHUNK, RET_QK_DIM), lambda h, s: (chunk_of(s), 0))
    head_sq = pl.BlockSpec((None, CHUNK, CHUNK), lambda h, s: (h, 0, 0))
    head_qk = pl.BlockSpec((None, CHUNK, RET_QK_DIM), lambda h, s: (h, 0, 0))
    head_st = pl.BlockSpec((None, RET_QK_DIM, RET_V_DIM), lambda h, s: (h, 0, 0))
    gam_spec = pl.BlockSpec((None, 1, RET_V_DIM), lambda h, s: (h, 0, 0))
    return [q_spec, k_spec, v_spec, g_spec, rope_spec, rope_spec, head_sq, head_qk, head_qk, head_st, gam_spec]


def _retention_fwd(proj, gn_g, tables, *, name):
    p = proj.shape[0]
    n_chunks = p // CHUNK
    scale = RET_QK_DIM ** -0.5

    def body(q_ref, k_ref, v_ref, g_ref, cos_ref, sin_ref, inner_ref, kdec_ref, qdec_ref, cdec_ref, gam_ref,
             og_ref, opre_ref, sprev_ref, s_scr):
        @pl.when(pl.program_id(1) == 0)
        def _():
            s_scr[...] = jnp.zeros_like(s_scr)

        cosf, sins = cos_ref[...], sin_ref[...]
        qr = _rot(q_ref[...], cosf, sins)
        kr = _rot(k_ref[...], cosf, sins) * scale
        vb = v_ref[...].astype(BF16)
        scores = _dot_nt(qr.astype(BF16), kr.astype(BF16)) * inner_ref[...]
        state = s_scr[...]
        sprev_ref[...] = state
        o = _dot(scores.astype(BF16), vb) + _dot((qr * qdec_ref[...]).astype(BF16), state.astype(BF16))
        kd = kr * kdec_ref[...]
        s_scr[...] = cdec_ref[...] * state + _dot(kd.T.astype(BF16), vb)
        opre_ref[...] = o
        mu = jnp.mean(o, axis=-1, keepdims=True)
        oc = o - mu
        var = jnp.mean(oc * oc, axis=-1, keepdims=True)
        on = oc * lax.rsqrt(var + EPS) * gam_ref[...]
        gv = g_ref[...]
        og_ref[...] = (gv * _sigmoid(gv) * on).astype(og_ref.dtype)

    chunk_of = lambda s: s
    out_v = pl.BlockSpec((CHUNK, RET_V_DIM), lambda h, s: (s, h))
    return pl.pallas_call(
        body,
        out_shape=[
            jax.ShapeDtypeStruct((p, RET_V_W), BF16),
            jax.ShapeDtypeStruct((p, RET_V_W), F32),
            jax.ShapeDtypeStruct((RET_HEADS, n_chunks, RET_QK_DIM, RET_V_DIM), F32),
        ],
        grid=(RET_HEADS, n_chunks),
        in_specs=_ret_in_specs(chunk_of),
        out_specs=[out_v, out_v, pl.BlockSpec((None, None, RET_QK_DIM, RET_V_DIM), lambda h, s: (h, s, 0, 0))],
        scratch_shapes=[pltpu.VMEM((RET_QK_DIM, RET_V_DIM), F32)],
        compiler_params=_params("parallel", "arbitrary"),
        name=name,
    )(proj, proj, proj, proj, *tables, gn_g.reshape(RET_HEADS, 1, RET_V_DIM))


def _retention_bwd(proj, gn_g, tables, opre, sprev, dog, *, name):
    p = proj.shape[0]
    n_chunks = p // CHUNK
    scale = RET_QK_DIM ** -0.5

    def body(q_ref, k_ref, v_ref, g_ref, cos_ref, sin_ref, inner_ref, kdec_ref, qdec_ref, cdec_ref, gam_ref,
             opre_ref, sprev_ref, dog_ref, dq_ref, dk_ref, dv_ref, dg_ref, dgam_ref, ds_scr):
        first = pl.program_id(1) == 0

        @pl.when(first)
        def _():
            ds_scr[...] = jnp.zeros_like(ds_scr)

        cosf, sins = cos_ref[...], sin_ref[...]
        qr = _rot(q_ref[...], cosf, sins)
        kr = _rot(k_ref[...], cosf, sins) * scale
        qb, kb = qr.astype(BF16), kr.astype(BF16)
        vb = v_ref[...].astype(BF16)
        inner = inner_ref[...]
        qdec, kdec = qdec_ref[...], kdec_ref[...]
        state_b = sprev_ref[...].astype(BF16)
        o = opre_ref[...]
        mu = jnp.mean(o, axis=-1, keepdims=True)
        oc = o - mu
        rstd = lax.rsqrt(jnp.mean(oc * oc, axis=-1, keepdims=True) + EPS)
        xhat = oc * rstd
        gam = gam_ref[...]
        on = xhat * gam
        gv = g_ref[...]
        sig = _sigmoid(gv)
        dogv = dog_ref[...]
        dg_ref[...] = (dogv * on * sig * (1.0 + gv * (1.0 - sig))).astype(dg_ref.dtype)
        don = dogv * gv * sig
        dgam_part = jnp.sum(don * xhat, axis=0, keepdims=True)

        @pl.when(first)
        def _():
            dgam_ref[...] = dgam_part

        @pl.when(jnp.logical_not(first))
        def _():
            dgam_ref[...] += dgam_part

        dxhat = don * gam
        do = rstd * (dxhat - jnp.mean(dxhat, axis=-1, keepdims=True)
                     - xhat * jnp.mean(dxhat * xhat, axis=-1, keepdims=True))
        dob = do.astype(BF16)
        scores_b = (_dot_nt(qb, kb) * inner).astype(BF16)
        da = (_dot_nt(dob, vb) * inner).astype(BF16)
        dv = _dot(scores_b.astype(F32).T.astype(BF16), dob)
        dqr = _dot(da, kb)
        dkr = _dot(da.astype(F32).T.astype(BF16), qb)
        dqr += _dot_nt(dob, state_b) * qdec
        ds_local = _dot((qr * qdec).T.astype(BF16), dob)
        gstate = ds_scr[...]
        gb = gstate.astype(BF16)
        kd_b = (kr * kdec).astype(BF16)
        dkr += _dot_nt(vb, gb) * kdec
        dv += _dot(kd_b, gb)
        ds_scr[...] = cdec_ref[...] * gstate + ds_local
        dq_ref[...] = _rot_bwd(dqr, cosf, sins).astype(dq_ref.dtype)
        dk_ref[...] = _rot_bwd(dkr * scale, cosf, sins).astype(dk_ref.dtype)
        dv_ref[...] = dv.astype(dv_ref.dtype)

    chunk_of = lambda s: n_chunks - 1 - s
    blk_v = pl.BlockSpec((CHUNK, RET_V_DIM), lambda h, s: (chunk_of(s), h))
    blk_qk = pl.BlockSpec((CHUNK, RET_QK_DIM), lambda h, s: (chunk_of(s), h))
    st_spec = pl.BlockSpec((None, None, RET_QK_DIM, RET_V_DIM), lambda h, s: (h, chunk_of(s), 0, 0))
    return pl.pallas_call(
        body,
        out_shape=[
            jax.ShapeDtypeStruct((p, RET_QK_W), BF16),
            jax.ShapeDtypeStruct((p, RET_QK_W), BF16),
            jax.ShapeDtypeStruct((p, RET_V_W), BF16),
            jax.ShapeDtypeStruct((p, RET_V_W), BF16),
            jax.ShapeDtypeStruct((RET_HEADS, 1, RET_V_DIM), F32),
        ],
        grid=(RET_HEADS, n_chunks),
        in_specs=_ret_in_specs(chunk_of) + [blk_v, st_spec, blk_v],
        out_specs=[blk_qk, blk_qk, blk_v, blk_v, pl.BlockSpec((None, 1, RET_V_DIM), lambda h, s: (h, 0, 0))],
        scratch_shapes=[pltpu.VMEM((RET_QK_DIM, RET_V_DIM), F32)],
        compiler_params=_params("parallel", "arbitrary"),
        name=name,
    )(proj, proj, proj, proj, *tables, gn_g.reshape(RET_HEADS, 1, RET_V_DIM), opre, sprev, dog)


def _conv_rows(p):
    return _pick(p, (384, 128))


def _ln_stats(y):
    mu = jnp.mean(y, axis=-1, keepdims=True)
    yc = y - mu
    rstd = lax.rsqrt(jnp.mean(yc * yc, axis=-1, keepdims=True) + EPS)
    return yc * rstd, rstd


def _conv_fwd(proj, conv_w, conv_b, ln_g, ln_b, *, name):
    p = proj.shape[0]
    c = D_MODEL
    rows = _conv_rows(p)
    hpb = rows // CONV_HALO
    a_col, gate_col = (2 * RET_QK_W + 2 * RET_V_W) // c, (2 * RET_QK_W + 2 * RET_V_W) // c + 1

    def body(a_ref, gate_ref, ah_ref, gateh_ref, w_ref, b_ref, lg_ref, lb_ref, c_ref, y_ref, hdn_scr):
        i = pl.program_id(0)
        hdn_scr[0:CONV_HALO, :] = ah_ref[...] * _sigmoid(gateh_ref[...])
        hdn_scr[CONV_HALO:, :] = a_ref[...] * _sigmoid(gate_ref[...])
        acc = jnp.zeros((rows, c), F32)
        for w in range(CONV_WIDTH):
            off = CONV_HALO - (CONV_WIDTH - 1) + w
            acc += hdn_scr[off:off + rows, :] * w_ref[w:w + 1, :]
        y = acc + b_ref[...]
        y_ref[...] = y
        yhat, _ = _ln_stats(y)
        ln = yhat * lg_ref[...] + lb_ref[...]
        row = i * rows + lax.broadcasted_iota(jnp.int32, (rows, 1), 0)
        c_ref[...] = jnp.where(row >= PAD_FRONT, ln * _sigmoid(ln), 0.0).astype(c_ref.dtype)

    halo_idx = lambda i: jnp.maximum(i * hpb - 1, 0)
    vec = pl.BlockSpec((1, c), lambda i: (0, 0))
    return pl.pallas_call(
        body,
        out_shape=[jax.ShapeDtypeStruct((p, c), BF16), jax.ShapeDtypeStruct((p, c), F32)],
        grid=(p // rows,),
        in_specs=[
            pl.BlockSpec((rows, c), lambda i: (i, a_col)),
            pl.BlockSpec((rows, c), lambda i: (i, gate_col)),
            pl.BlockSpec((CONV_HALO, c), lambda i: (halo_idx(i), a_col)),
            pl.BlockSpec((CONV_HALO, c), lambda i: (halo_idx(i), gate_col)),
            pl.BlockSpec((CONV_WIDTH, c), lambda i: (0, 0)),
            vec, vec, vec,
        ],
        out_specs=[pl.BlockSpec((rows, c), lambda i: (i, 0)), pl.BlockSpec((rows, c), lambda i: (i, 0))],
        scratch_shapes=[pltpu.VMEM((CONV_HALO + rows, c), F32)],
        compiler_params=_params("parallel"),
        name=name,
    )(proj, proj, proj, proj, conv_w, conv_b, ln_g, ln_b)


def _conv_bwd(proj, conv_w, ln_g, ln_b, y, dcat, *, name):
    p = proj.shape[0]
    c = D_MODEL
    rows = _conv_rows(p)
    hpb = rows // CONV_HALO
    n_blocks = p // rows
    a_col, gate_col = (2 * RET_QK_W + 2 * RET_V_W) // c, (2 * RET_QK_W + 2 * RET_V_W) // c + 1

    def body(a_ref, gate_ref, ah_ref, gateh_ref, w_ref, lg_ref, lb_ref, y_ref, yh_ref, dc_ref, dch_ref,
             da_ref, dgate_ref, dw_ref, db_ref, dlg_ref, dlb_ref, hdn_scr, dy_scr):
        i = pl.program_id(0)
        lg, lb = lg_ref[...], lb_ref[...]

        def ln_bwd(yv, dcv):
            yhat, rstd = _ln_stats(yv)
            ln = yhat * lg + lb
            sig = _sigmoid(ln)
            dln = dcv * sig * (1.0 + ln * (1.0 - sig))
            dyhat = dln * lg
            dyv = rstd * (dyhat - jnp.mean(dyhat, axis=-1, keepdims=True)
                          - yhat * jnp.mean(dyhat * yhat, axis=-1, keepdims=True))
            return dyv, dln, yhat

        row = i * rows + lax.broadcasted_iota(jnp.int32, (rows, 1), 0)
        dy, dln, yhat = ln_bwd(y_ref[...], jnp.where(row >= PAD_FRONT, dc_ref[...], 0.0))
        dy_halo, _, _ = ln_bwd(yh_ref[...], dch_ref[...])
        dy_scr[0:rows, :] = dy
        dy_scr[rows:, :] = jnp.where(i == n_blocks - 1, 0.0, dy_halo)
        sig_gate = _sigmoid(gate_ref[...])
        av = a_ref[...]
        hdn_scr[0:CONV_HALO, :] = ah_ref[...] * _sigmoid(gateh_ref[...])
        hdn_scr[CONV_HALO:, :] = av * sig_gate
        @pl.when(i == 0)
        def _():
            dw_ref[...] = jnp.zeros_like(dw_ref)
            db_ref[...] = jnp.zeros_like(db_ref)
            dlg_ref[...] = jnp.zeros_like(dlg_ref)
            dlb_ref[...] = jnp.zeros_like(dlb_ref)

        dhdn = jnp.zeros((rows, c), F32)
        for w in range(CONV_WIDTH):
            back = CONV_WIDTH - 1 - w
            dhdn += dy_scr[back:back + rows, :] * w_ref[w:w + 1, :]
            off = CONV_HALO - (CONV_WIDTH - 1) + w
            dw_ref[w:w + 1, :] += jnp.sum(dy * hdn_scr[off:off + rows, :], axis=0, keepdims=True)
        da_ref[...] = (dhdn * sig_gate).astype(da_ref.dtype)
        dgate_ref[...] = (dhdn * av * sig_gate * (1.0 - sig_gate)).astype(dgate_ref.dtype)
        db_ref[...] += jnp.sum(dy, axis=0, keepdims=True)
        dlg_ref[...] += jnp.sum(dln * yhat, axis=0, keepdims=True)
        dlb_ref[...] += jnp.sum(dln, axis=0, keepdims=True)

    prev_halo = lambda i: jnp.maximum(i * hpb - 1, 0)
    next_halo = lambda i: jnp.minimum((i + 1) * hpb, p // CONV_HALO - 1)
    vec = pl.BlockSpec((1, c), lambda i: (0, 0))
    blk = lambda col: pl.BlockSpec((rows, c), lambda i: (i, col))
    outs = pl.pallas_call(
        body,
        out_shape=[
            jax.ShapeDtypeStruct((p, c), BF16),
            jax.ShapeDtypeStruct((p, c), BF16),
            jax.ShapeDtypeStruct((CONV_WIDTH + 1, c), F32),
            jax.ShapeDtypeStruct((1, c), F32),
            jax.ShapeDtypeStruct((1, c), F32),
            jax.ShapeDtypeStruct((1, c), F32),
        ],
        grid=(n_blocks,),
        in_specs=[
            blk(a_col), blk(gate_col),
            pl.BlockSpec((CONV_HALO, c), lambda i: (prev_halo(i), a_col)),
            pl.BlockSpec((CONV_HALO, c), lambda i: (prev_halo(i), gate_col)),
            pl.BlockSpec((CONV_WIDTH, c), lambda i: (0, 0)),
            vec, vec,
            blk(0),
            pl.BlockSpec((CONV_HALO, c), lambda i: (next_halo(i), 0)),
            blk(1),
            pl.BlockSpec((CONV_HALO, c), lambda i: (next_halo(i), 1)),
        ],
        out_specs=[blk(0), blk(0), pl.BlockSpec((CONV_WIDTH + 1, c), lambda i: (0, 0)), vec, vec, vec],
        scratch_shapes=[pltpu.VMEM((CONV_HALO + rows, c), F32), pltpu.VMEM((rows + CONV_HALO, c), F32)],
        compiler_params=_params("arbitrary"),
        name=name,
    )(proj, proj, proj, proj, conv_w, ln_g, ln_b, y, y, dcat, dcat)
    da, dgate, dw, db, dlg, dlb = outs
    return da, dgate, dw[:CONV_WIDTH], db, dlg, dlb


def _group_matrix():
    r = jnp.arange(D_MODEL)[:, None] // SB_HEAD_DIM
    c = jnp.arange(D_MODEL)[None, :] // SB_HEAD_DIM
    return (r == c).astype(BF16)


def _qknorm_fwd(qkv, qg, kg, *, name):
    p = qkv.shape[0]
    d = D_MODEL
    rows = _pick(p, (384, 128, 96))

    def body(q_ref, k_ref, v_ref, qg_ref, kg_ref, gm_ref, qn_ref, kn_ref, vb_ref):
        gm = gm_ref[...]

        def norm(x, g):
            ms = _split_dot(x * x, gm) * (1.0 / SB_HEAD_DIM)
            return x * lax.rsqrt(ms + EPS) * g

        qn_ref[...] = norm(q_ref[...], qg_ref[...]).astype(BF16)
        kn_ref[...] = norm(k_ref[...], kg_ref[...]).astype(BF16)
        vb_ref[...] = v_ref[...].astype(BF16)

    blk = lambda col: pl.BlockSpec((rows, d), lambda i: (i, col))
    vec = pl.BlockSpec((1, d), lambda i: (0, 0))
    return pl.pallas_call(
        body,
        out_shape=[jax.ShapeDtypeStruct((p, d), BF16)] * 3,
        grid=(p // rows,),
        in_specs=[blk(0), blk(1), blk(2), vec, vec, pl.BlockSpec((d, d), lambda i: (0, 0))],
        out_specs=[blk(0)] * 3,
        compiler_params=_params("parallel"),
        name=name,
    )(qkv, qkv, qkv, qg, kg, _group_matrix())


def _qknorm_bwd(qkv, qg, kg, dqn, dkn, dv, *, name):
    p = qkv.shape[0]
    d = D_MODEL
    rows = _pick(p, (384, 128, 96))

    def body(q_ref, k_ref, qg_ref, kg_ref, gm_ref, dqn_ref, dkn_ref, dv_ref, dqkv_ref, dqg_ref, dkg_ref):
        gm = gm_ref[...]

        def bwd(x, g, dy):
            ms = _split_dot(x * x, gm) * (1.0 / SB_HEAD_DIM)
            r = lax.rsqrt(ms + EPS)
            gdy = dy * g
            proj = _split_dot(x * gdy, gm) * (1.0 / SB_HEAD_DIM)
            return r * gdy - x * (r * r * r) * proj, jnp.sum(dy * x * r, axis=0, keepdims=True)

        dq, dqg = bwd(q_ref[...], qg_ref[...], dqn_ref[...])
        dk, dkg = bwd(k_ref[...], kg_ref[...], dkn_ref[...])
        dqkv_ref[:, 0:d] = dq.astype(BF16)
        dqkv_ref[:, d:2 * d] = dk.astype(BF16)
        dqkv_ref[:, 2 * d:3 * d] = dv_ref[...].astype(BF16)

        @pl.when(pl.program_id(0) == 0)
        def _():
            dqg_ref[...] = dqg
            dkg_ref[...] = dkg

        @pl.when(pl.program_id(0) > 0)
        def _():
            dqg_ref[...] += dqg
            dkg_ref[...] += dkg

    blk = lambda col: pl.BlockSpec((rows, d), lambda i: (i, col))
    vec = pl.BlockSpec((1, d), lambda i: (0, 0))
    return pl.pallas_call(
        body,
        out_shape=[jax.ShapeDtypeStruct((p, 3 * d), BF16), jax.ShapeDtypeStruct((1, d), F32),
                   jax.ShapeDtypeStruct((1, d), F32)],
        grid=(p // rows,),
        in_specs=[blk(0), blk(1), vec, vec, pl.BlockSpec((d, d), lambda i: (0, 0)), blk(0), blk(0), blk(0)],
        out_specs=[pl.BlockSpec((rows, 3 * d), lambda i: (i, 0)), vec, vec],
        compiler_params=_params("arbitrary"),
        name=name,
    )(qkv, qkv, qg, kg, _group_matrix(), dqn, dkn, dv)


SB_PAIR = 2 * SB_HEAD_DIM


def _sb_tile(qh, kt, valid):
    z = _dot_nt(qh, kt) * (SB_HEAD_DIM ** -0.5)
    sp = jnp.log1p(jnp.exp(-jnp.abs(z)))
    ls_pos = jnp.minimum(z, 0.0) - sp
    ls_neg = jnp.minimum(-z, 0.0) - sp
    return ls_pos, jnp.where(valid, ls_neg, 0.0)


def _sb_masks():
    lane = lax.broadcasted_iota(jnp.int32, (CHUNK, SB_PAIR), 1)
    r = lax.broadcasted_iota(jnp.int32, (CHUNK, CHUNK), 0)
    c = lax.broadcasted_iota(jnp.int32, (CHUNK, CHUNK), 1)
    lo = (lane < SB_HEAD_DIM).astype(F32).astype(BF16)
    return (lo, 1.0 - lo), r, c


def _sb_halves(t, head_lanes):
    return t * head_lanes[0], t * head_lanes[1]


def _sb_fwd(qn, kn, vb, *, name):
    p = qn.shape[0]
    n_blocks = p // CHUNK
    n_pairs = SB_HEADS // 2

    def body(q_ref, k_ref, v_ref, o_ref, car_ref):
        head_lanes, r, c = _sb_masks()
        after_m = (r > c).astype(BF16)

        def q_block(qi, _):
            rows = pl.ds(pl.multiple_of(qi * CHUNK, CHUNK), CHUNK)
            qh = _sb_halves(q_ref[rows, :], head_lanes)
            qpos = qi * CHUNK + r

            def k_block(s, carry):
                acc, run0, run1, sav0, sav1 = carry
                kj = qi - s
                krows = pl.ds(pl.multiple_of(kj * CHUNK, CHUNK), CHUNK)
                kt = k_ref[krows, :]
                vt = v_ref[krows, :]
                kpos = kj * CHUNK + c
                valid = jnp.logical_and(kpos < qpos, kpos >= PAD_FRONT)
                this_col = (c == kj).astype(F32)
                ws, runs, savs = [], [], []
                for h, (run, sav) in enumerate(((run0, sav0), (run1, sav1))):
                    ls_pos, log_keep = _sb_tile(qh[h], kt, valid)
                    after = _split_dot(log_keep, after_m) + run
                    ws.append(jnp.where(valid, jnp.exp(ls_pos + after), 0.0).astype(BF16))
                    savs.append(sav + this_col * run)
                    runs.append(run + jnp.sum(log_keep, axis=-1, keepdims=True))
                v2 = jnp.concatenate(_sb_halves(vt, head_lanes), axis=0)
                acc = acc + _dot(jnp.concatenate(ws, axis=1), v2)
                return acc, runs[0], runs[1], savs[0], savs[1]

            zt = qh[0].astype(F32) * 0.0
            acc, _, _, sav0, sav1 = lax.fori_loop(0, qi + 1, k_block, (zt, zt, zt, zt, zt))
            o_ref[rows, :] = acc.astype(o_ref.dtype)
            car_ref[rows, 0:CHUNK] = sav0
            car_ref[rows, CHUNK:2 * CHUNK] = sav1
            return 0

        lax.fori_loop(0, n_blocks, q_block, 0)

    col = pl.BlockSpec((p, SB_PAIR), lambda g: (0, g))
    return pl.pallas_call(
        body,
        out_shape=[jax.ShapeDtypeStruct((p, D_MODEL), BF16), jax.ShapeDtypeStruct((p, n_pairs * 2 * CHUNK), F32)],
        grid=(n_pairs,),
        in_specs=[col, col, col],
        out_specs=[col, pl.BlockSpec((p, 2 * CHUNK), lambda g: (0, g))],
        compiler_params=_params("parallel"),
        name=name,
    )(qn, kn, vb)


def _sb_bwd(qn, kn, vb, carries, do, *, name):
    p = qn.shape[0]
    n_blocks = p // CHUNK
    n_pairs = SB_HEADS // 2
    scale = SB_HEAD_DIM ** -0.5

    def body(q_ref, k_ref, v_ref, car_ref, do_ref, dq_ref, dk_ref, dv_ref):
        head_lanes, r, c = _sb_masks()
        after_m = (r > c).astype(BF16)
        before_m = (r < c).astype(BF16)
        dk_ref[...] = jnp.zeros_like(dk_ref)
        dv_ref[...] = jnp.zeros_like(dv_ref)
        halves = functools.partial(_sb_halves, head_lanes=head_lanes)

        def q_block(qi, _):
            rows = pl.ds(pl.multiple_of(qi * CHUNK, CHUNK), CHUNK)
            qt = q_ref[rows, :]
            qh = halves(qt)
            dot_b = do_ref[rows, :].astype(BF16)
            doh = halves(dot_b)
            do2 = jnp.concatenate(doh, axis=0)
            q2 = jnp.concatenate(qh, axis=0)
            sav = (car_ref[rows, 0:CHUNK], car_ref[rows, CHUNK:2 * CHUNK])
            qpos = qi * CHUNK + r

            def k_block(kj, carry):
                dq_acc, pre0, pre1 = carry
                krows = pl.ds(pl.multiple_of(kj * CHUNK, CHUNK), CHUNK)
                kt = k_ref[krows, :]
                vt = v_ref[krows, :]
                kpos = kj * CHUNK + c
                valid = jnp.logical_and(kpos < qpos, kpos >= PAD_FRONT)
                this_col = (c == kj).astype(F32)
                wts, dzts, dzs, pres = [], [], [], []
                for h, pre in enumerate((pre0, pre1)):
                    ls_pos, log_keep = _sb_tile(qh[h], kt, valid)
                    run = jnp.sum(this_col * sav[h], axis=-1, keepdims=True)
                    after = _split_dot(log_keep, after_m) + run
                    w = jnp.where(valid, jnp.exp(ls_pos + after), 0.0)
                    e = w * _dot_nt(doh[h], vt)
                    before = _split_dot(e, before_m) + pre
                    sig = jnp.exp(ls_pos)
                    dz = jnp.where(valid, e * (1.0 - sig) - before * sig, 0.0) * scale
                    wts.append(w.T.astype(BF16))
                    dzts.append(dz.T.astype(BF16))
                    dzs.append(dz.astype(BF16))
                    pres.append(pre + jnp.sum(e, axis=-1, keepdims=True))
                k2 = jnp.concatenate(halves(kt), axis=0)
                dq_acc = dq_acc + _dot(jnp.concatenate(dzs, axis=1), k2)
                dv_ref[krows, :] += _dot(jnp.concatenate(wts, axis=1), do2)
                dk_ref[krows, :] += _dot(jnp.concatenate(dzts, axis=1), q2)
                return dq_acc, pres[0], pres[1]

            zt = qh[0].astype(F32) * 0.0
            dq_acc, _, _ = lax.fori_loop(0, qi + 1, k_block, (zt, zt, zt))
            dq_ref[rows, :] = dq_acc
            return 0

        lax.fori_loop(0, n_blocks, q_block, 0)

    col = pl.BlockSpec((p, SB_PAIR), lambda g: (0, g))
    return pl.pallas_call(
        body,
        out_shape=[jax.ShapeDtypeStruct((p, D_MODEL), F32)] * 3,
        grid=(n_pairs,),
        in_specs=[col, col, col, pl.BlockSpec((p, 2 * CHUNK), lambda g: (0, g)), col],
        out_specs=[col, col, col],
        compiler_params=_params("parallel"),
        name=name,
    )(qn, kn, vb, carries, do)


def _loss_head(h, target, *, name):
    p, d = h.shape
    n_blocks = p // CHUNK

    def body(h_ref, t_ref, sq_ref, dh_ref):
        i = pl.program_id(0)

        @pl.when(i == 0)
        def _():
            sq_ref[...] = jnp.zeros_like(sq_ref)
            dh_ref[...] = jnp.zeros_like(dh_ref)

        @pl.when(i > 0)
        def _():
            err = h_ref[...] - t_ref[...]
            sq_ref[...] += jnp.sum(err * err)
            dh_ref[...] = err * (1.0 / d)

    return pl.pallas_call(
        body,
        out_shape=[jax.ShapeDtypeStruct((8, 128), F32), jax.ShapeDtypeStruct((p, d), F32)],
        grid=(n_blocks,),
        in_specs=[pl.BlockSpec((CHUNK, d), lambda i: (i, 0)),
                  pl.BlockSpec((CHUNK, d), lambda i: (jnp.maximum(i - 1, 0), 0))],
        out_specs=[pl.BlockSpec((8, 128), lambda i: (0, 0)), pl.BlockSpec((CHUNK, d), lambda i: (i, 0))],
        compiler_params=_params("arbitrary"),
        name=name,
    )(h, target)


def _local_step(x, target, meta, norm_mix_g, norm_mlp_g, w_in, gn_g, conv_w, conv_b, ln_g, ln_b, w_out,
                w_qkv, qn_g, kn_g, w_o, w1, w2):
    seq = x.shape[0]
    p = PAD_FRONT + N_META + seq
    d = D_MODEL
    tables = _retention_tables(p)
    row = lambda v: v.reshape(1, -1)
    h0 = jnp.concatenate([jnp.zeros((PAD_FRONT, d), F32), meta, x], axis=0)

    hn0 = _rmsnorm_fwd(h0, row(norm_mix_g[0]), name="l0_mix_norm")
    proj = _matmul(hn0, w_in, mode="nn", out_dtypes=(F32,), name="l0_proj")
    og, opre, sprev = _retention_fwd(proj, gn_g, tables, name="l0_retention")
    cb, y_conv = _conv_fwd(proj, conv_w, row(conv_b), row(ln_g), row(ln_b), name="l0_conv")
    cat = jnp.concatenate([og, cb], axis=1)
    h1 = _matmul(cat, w_out, mode="nn", out_dtypes=(F32,), epilogue=_add_epilogue, extras=(h0,), name="l0_mix_out")
    h2, mlp0 = _mlp_fwd(h1, row(norm_mlp_g[0]), w1[0], w2[0], name="l0_mlp")

    hn1 = _rmsnorm_fwd(h2, row(norm_mix_g[1]), name="l1_mix_norm")
    qkv = _matmul(hn1, w_qkv, mode="nn", out_dtypes=(F32,), name="l1_qkv")
    qg_t, kg_t = jnp.tile(row(qn_g), (1, SB_HEADS)), jnp.tile(row(kn_g), (1, SB_HEADS))
    qn, kn, vb = _qknorm_fwd(qkv, qg_t, kg_t, name="l1_qknorm")
    o_sb, carries = _sb_fwd(qn, kn, vb, name="l1_stickbreak")
    h3 = _matmul(o_sb, w_o, mode="nn", out_dtypes=(F32,), epilogue=_add_epilogue, extras=(h2,), name="l1_mix_out")
    h4, mlp1 = _mlp_fwd(h3, row(norm_mlp_g[1]), w1[1], w2[1], name="l1_mlp")

    sq, dh4 = _loss_head(h4, target, name="loss_head")

    dh3, dg_mlp1, dw1_1, dw2_1 = _mlp_bwd(h3, row(norm_mlp_g[1]), w1[1], w2[1], mlp1, dh4, name="l1_mlp_bwd")
    do_sb = _matmul(dh3, w_o, mode="nt", out_dtypes=(F32,), name="l1_do")
    dw_o = _matmul(o_sb, dh3, mode="tn", out_dtypes=(F32,), name="l1_dwo")
    dqn, dkn, dv = _sb_bwd(qn, kn, vb, carries, do_sb, name="l1_stickbreak_bwd")
    dqkv, dqg_t, dkg_t = _qknorm_bwd(qkv, qg_t, kg_t, dqn, dkn, dv, name="l1_qknorm_bwd")
    dw_qkv = _matmul(hn1, dqkv, mode="tn", out_dtypes=(F32,), name="l1_dwqkv")
    dhn1 = _matmul(dqkv, w_qkv, mode="nt", out_dtypes=(F32,), name="l1_dhn")
    dh2, dg_mix1 = _rmsnorm_bwd(h2, row(norm_mix_g[1]), dhn1, dh3, name="l1_mix_dnorm")

    dh1, dg_mlp0, dw1_0, dw2_0 = _mlp_bwd(h1, row(norm_mlp_g[0]), w1[0], w2[0], mlp0, dh2, name="l0_mlp_bwd")
    dcat = _matmul(dh1, w_out, mode="nt", out_dtypes=(F32,), name="l0_dcat")
    dw_out = _matmul(cat, dh1, mode="tn", out_dtypes=(F32,), name="l0_dwout")
    dq, dk, dvr, dgate_r, dgn = _retention_bwd(proj, gn_g, tables, opre, sprev, dcat, name="l0_retention_bwd")
    da, dgate_c, dconv_w, dconv_b, dln_g, dln_b = _conv_bwd(proj, conv_w, row(ln_g), row(ln_b), y_conv, dcat,
                                                            name="l0_conv_bwd")
    dproj = jnp.concatenate([dq, dk, dvr, dgate_r, da, dgate_c], axis=1)
    dw_in = _matmul(hn0, dproj, mode="tn", out_dtypes=(F32,), name="l0_dwin")
    dhn0 = _matmul(dproj, w_in, mode="nt", out_dtypes=(F32,), name="l0_dhn")
    dh0, dg_mix0 = _rmsnorm_bwd(h0, row(norm_mix_g[0]), dhn0, dh1, name="l0_mix_dnorm")

    fold = lambda t: t.reshape(SB_HEADS, SB_HEAD_DIM).sum(axis=0)
    grads = dict(
        x=dh0[PAD_FRONT + N_META:],
        meta=dh0[PAD_FRONT:PAD_FRONT + N_META],
        norm_mix_g=jnp.concatenate([dg_mix0, dg_mix1], axis=0),
        norm_mlp_g=jnp.concatenate([dg_mlp0, dg_mlp1], axis=0),
        even_w_in=dw_in,
        even_ret_gn_g=dgn.reshape(RET_HEADS, RET_V_DIM),
        even_conv_w=dconv_w,
        even_conv_b=dconv_b,
        even_conv_ln_g=dln_g,
        even_conv_ln_b=dln_b,
        even_w_out=dw_out,
        odd_w_qkv=dw_qkv,
        odd_q_norm_g=fold(dqg_t)[None],
        odd_k_norm_g=fold(dkg_t)[None],
        odd_w_o=dw_o,
        mlp_w1=(dw1_0, dw1_1),
        mlp_w2=(dw2_0, dw2_1),
    )
    return sq[0, 0], grads


def _position():
    x, y, c = lax.axis_index("x"), lax.axis_index("y"), lax.axis_index("c")
    other_chips = [(1 - x, y), (x, 1 - y), (1 - x, 1 - y)]
    return x, y, c, other_chips


def _shard_of(ref, kind, s, n):
    rows, cols = ref.shape
    if kind == "col":
        return ref.at[:, pl.ds(s * (cols // n), cols // n)]
    return ref.at[pl.ds(s * (rows // n), rows // n), :]


def _half_of(ref, kind, c):
    rows, cols = ref.shape
    if kind == "col":
        return ref.at[pl.ds(c * (rows // 2), rows // 2), :]
    return ref.at[:, pl.ds(c * (cols // 2), cols // 2)]


def _remote(src, dst, send_sems, recv_sems, idx, device):
    return pltpu.make_async_remote_copy(src_ref=src, dst_ref=dst, send_sem=send_sems.at[idx], recv_sem=recv_sems.at[idx],
                                        device_id=device, device_id_type=MESH)


def _cast_bf16(w, *, name):
    rows, cols = w.shape
    tr = _pick(rows, (256, 128))

    def body(w_ref, o_ref):
        o_ref[...] = w_ref[...].astype(BF16)

    return pl.pallas_call(
        body,
        out_shape=jax.ShapeDtypeStruct((rows, cols), BF16),
        grid=(rows // tr,),
        in_specs=[pl.BlockSpec((tr, cols), lambda i: (i, 0))],
        out_specs=pl.BlockSpec((tr, cols), lambda i: (i, 0)),
        compiler_params=_params("parallel"),
        name=name,
    )(w)


def _allgather_weights(shards, kinds):
    n = len(shards)

    def body(*refs):
        ins, outs = refs[:n], refs[n:2 * n]
        send_sems, recv_sems, local_sems = refs[2 * n:]
        x, y, c, chips = _position()
        me_chip = 2 * x + y
        sibling = (x, y, 1 - c)
        own = [pltpu.make_async_copy(ins[t], _shard_of(outs[t], kinds[t], me_chip, N_CHIPS), local_sems.at[t])
               for t in range(n)]
        for cp in own:
            cp.start()
        sends = []
        for t in range(n):
            for k, (cx, cy) in enumerate(chips):
                dst = _half_of(_shard_of(outs[t], kinds[t], me_chip, N_CHIPS), kinds[t], c)
                sends.append(_remote(_half_of(ins[t], kinds[t], c), dst, send_sems, recv_sems, 6 * t + k, (cx, cy, c)))
        for cp in sends:
            cp.start()
        passed = []
        for t in range(n):
            for k, (cx, cy) in enumerate(chips):
                landed = _half_of(_shard_of(outs[t], kinds[t], 2 * cx + cy, N_CHIPS), kinds[t], c)
                _remote(landed, landed, send_sems, recv_sems, 6 * t + k, (cx, cy, c)).wait_recv()
                fwd = _remote(landed, landed, send_sems, recv_sems, 6 * t + 3 + k, sibling)
                fwd.start()
                passed.append(fwd)
        for t in range(n):
            for k, (cx, cy) in enumerate(chips):
                theirs = _half_of(_shard_of(outs[t], kinds[t], 2 * cx + cy, N_CHIPS), kinds[t], 1 - c)
                _remote(theirs, theirs, send_sems, recv_sems, 6 * t + 3 + k, sibling).wait_recv()
        for cp in sends + passed:
            cp.wait_send()
        for cp in own:
            cp.wait()

    def whole(s, kind):
        rows, cols = s.shape
        return (rows, cols * N_CHIPS) if kind == "col" else (rows * N_CHIPS, cols)

    return pl.pallas_call(
        body,
        out_shape=[jax.ShapeDtypeStruct(whole(s, k), BF16) for s, k in zip(shards, kinds)],
        in_specs=[ANY] * n,
        out_specs=[ANY] * n,
        scratch_shapes=[pltpu.SemaphoreType.DMA((6 * n,)), pltpu.SemaphoreType.DMA((6 * n,)),
                        pltpu.SemaphoreType.DMA((n,))],
        name="allgather_weights",
    )(*shards)


def _allgather8(block, *, name):
    rows, cols = block.shape

    def body(in_ref, out_ref, send_sems, recv_sems, local_sem):
        x, y, c, _ = _position()
        me = 4 * x + 2 * y + c
        mine = pltpu.make_async_copy(in_ref, out_ref.at[me], local_sem)
        mine.start()
        peers = []
        for flip in range(1, N_DEV):
            fx, fy, fc = (flip >> 2) & 1, (flip >> 1) & 1, flip & 1
            peers.append(((1 - x if fx else x), (1 - y if fy else y), (1 - c if fc else c)))
        sends = [_remote(in_ref, out_ref.at[me], send_sems, recv_sems, j, peer) for j, peer in enumerate(peers)]
        for cp in sends:
            cp.start()
        for j, (px, py, pc) in enumerate(peers):
            slot = out_ref.at[4 * px + 2 * py + pc]
            _remote(slot, slot, send_sems, recv_sems, j, (px, py, pc)).wait_recv()
        for cp in sends:
            cp.wait_send()
        mine.wait()

    vmem = pl.BlockSpec(memory_space=pltpu.VMEM)
    return pl.pallas_call(
        body,
        out_shape=jax.ShapeDtypeStruct((N_DEV, rows, cols), F32),
        in_specs=[vmem],
        out_specs=vmem,
        scratch_shapes=[pltpu.SemaphoreType.DMA((N_DEV - 1,)), pltpu.SemaphoreType.DMA((N_DEV - 1,)),
                        pltpu.SemaphoreType.DMA],
        name=name,
    )(block)


def _sum8(stack, *, name):
    _, rows, cols = stack.shape

    def body(s_ref, o_ref):
        acc = s_ref[0]
        for i in range(1, N_DEV):
            acc = acc + s_ref[i]
        o_ref[...] = acc

    return pl.pallas_call(body, out_shape=jax.ShapeDtypeStruct((rows, cols), F32), name=name)(stack)


def _swap_halves_in(grads, kinds):
    n = len(grads)

    def body(*refs):
        ins, outs = refs[:n], refs[n:2 * n]
        send_sems, recv_sems = refs[2 * n:]
        x, y, c, _ = _position()
        sibling = (x, y, 1 - c)
        sends = [_remote(_half_of(ins[t], kinds[t], 1 - c), outs[t], send_sems, recv_sems, t, sibling) for t in range(n)]
        for cp in sends:
            cp.start()
        for t in range(n):
            _remote(_half_of(ins[t], kinds[t], c), outs[t], send_sems, recv_sems, t, sibling).wait_recv()
        for cp in sends:
            cp.wait_send()

    def half(g, kind):
        rows, cols = g.shape
        return (rows // 2, cols) if kind == "col" else (rows, cols // 2)

    return pl.pallas_call(
        body,
        out_shape=[jax.ShapeDtypeStruct(half(g, k), F32) for g, k in zip(grads, kinds)],
        in_specs=[ANY] * n,
        out_specs=[ANY] * n,
        scratch_shapes=[pltpu.SemaphoreType.DMA((n,)), pltpu.SemaphoreType.DMA((n,))],
        name="reduce_core_pair",
    )(*grads)


def _half_add(grad, theirs, kind, c_arr, *, name):
    rows, cols = theirs.shape
    tr = _pick(rows, (256, 128))
    nb = rows // tr
    if kind == "col":
        g_spec = pl.BlockSpec((tr, cols), lambda i, c_ref: (c_ref[0] * nb + i, 0))
    else:
        g_spec = pl.BlockSpec((tr, cols), lambda i, c_ref: (i, c_ref[0]))
    t_spec = pl.BlockSpec((tr, cols), lambda i, c_ref: (i, 0))

    def body(c_ref, g_ref, t_ref, o32_ref, o16_ref):
        tot = g_ref[...] + t_ref[...]
        o32_ref[...] = tot
        o16_ref[...] = tot.astype(BF16)

    return pl.pallas_call(
        body,
        out_shape=[jax.ShapeDtypeStruct((rows, cols), F32), jax.ShapeDtypeStruct((rows, cols), BF16)],
        grid_spec=pltpu.PrefetchScalarGridSpec(num_scalar_prefetch=1, grid=(nb,), in_specs=[g_spec, t_spec],
                                               out_specs=[t_spec, t_spec]),
        compiler_params=_params("parallel"),
        name=name,
    )(c_arr, grad, theirs)


def _exchange_chips(parts, kinds):
    n = len(parts)

    def body(*refs):
        ins, outs = refs[:n], refs[n:2 * n]
        send_sems, recv_sems = refs[2 * n:]
        x, y, c, chips = _position()
        sends = []
        for t in range(n):
            for k, (cx, cy) in enumerate(chips):
                src = _shard_of(ins[t], kinds[t], 2 * cx + cy, N_CHIPS)
                sends.append(_remote(src, outs[t].at[k], send_sems, recv_sems, 3 * t + k, (cx, cy, c)))
        for cp in sends:
            cp.start()
        for t in range(n):
            for k, (cx, cy) in enumerate(chips):
                src = _shard_of(ins[t], kinds[t], 2 * cx + cy, N_CHIPS)
                _remote(src, outs[t].at[k], send_sems, recv_sems, 3 * t + k, (cx, cy, c)).wait_recv()
        for cp in sends:
            cp.wait_send()

    def piece(p, kind):
        rows, cols = p.shape
        return (3, rows, cols // N_CHIPS) if kind == "col" else (3, rows // N_CHIPS, cols)

    return pl.pallas_call(
        body,
        out_shape=[jax.ShapeDtypeStruct(piece(p, k), BF16) for p, k in zip(parts, kinds)],
        in_specs=[ANY] * n,
        out_specs=[ANY] * n,
        scratch_shapes=[pltpu.SemaphoreType.DMA((3 * n,)), pltpu.SemaphoreType.DMA((3 * n,))],
        name="reduce_chips",
    )(*parts)


def _shard_sum(part32, recv, kind, s_arr, *, name):
    _, rows, cols = recv.shape
    tr = _pick(rows, (256, 128))
    nb = rows // tr
    if kind == "col":
        p_spec = pl.BlockSpec((tr, cols), lambda i, s_ref: (i, s_ref[0]))
    else:
        p_spec = pl.BlockSpec((tr, cols), lambda i, s_ref: (s_ref[0] * nb + i, 0))
    r_spec = pl.BlockSpec((3, tr, cols), lambda i, s_ref: (0, i, 0))
    o_spec = pl.BlockSpec((tr, cols), lambda i, s_ref: (i, 0))

    def body(s_ref, p_ref, r_ref, o_ref):
        acc = p_ref[...]
        for k in range(3):
            acc = acc + r_ref[k].astype(F32)
        o_ref[...] = acc

    return pl.pallas_call(
        body,
        out_shape=jax.ShapeDtypeStruct((rows, cols), F32),
        grid_spec=pltpu.PrefetchScalarGridSpec(num_scalar_prefetch=1, grid=(nb,), in_specs=[p_spec, r_spec],
                                               out_specs=o_spec),
        compiler_params=_params("parallel"),
        name=name,
    )(s_arr, part32, recv)


def _swap_halves_out(halves, kinds):
    n = len(halves)

    def body(*refs):
        ins, outs = refs[:n], refs[n:2 * n]
        send_sems, recv_sems, local_sems = refs[2 * n:]
        x, y, c, _ = _position()
        sibling = (x, y, 1 - c)
        own = [pltpu.make_async_copy(ins[t], _half_of(outs[t], kinds[t], c), local_sems.at[t]) for t in range(n)]
        sends = [_remote(ins[t], _half_of(outs[t], kinds[t], c), send_sems, recv_sems, t, sibling) for t in range(n)]
        for cp in own + sends:
            cp.start()
        for t in range(n):
            _remote(ins[t], _half_of(outs[t], kinds[t], 1 - c), send_sems, recv_sems, t, sibling).wait_recv()
        for cp in sends:
            cp.wait_send()
        for cp in own:
            cp.wait()

    def whole(h, kind):
        rows, cols = h.shape
        return (rows * 2, cols) if kind == "col" else (rows, cols * 2)

    return pl.pallas_call(
        body,
        out_shape=[jax.ShapeDtypeStruct(whole(h, k), F32) for h, k in zip(halves, kinds)],
        in_specs=[ANY] * n,
        out_specs=[ANY] * n,
        scratch_shapes=[pltpu.SemaphoreType.DMA((n,)), pltpu.SemaphoreType.DMA((n,)), pltpu.SemaphoreType.DMA((n,))],
        name="gather_core_pair",
    )(*halves)


def _adamw(w, g, m, v, *, name):
    rows, cols = w.shape
    tr = _pick(rows, (256, 128)) if rows * cols > 64 * 1024 else rows

    def body(w_ref, g_ref, m_ref, v_ref, d_ref, nm_ref, nv_ref):
        gv = g_ref[...]
        nm = ADAM_B1 * m_ref[...] + (1.0 - ADAM_B1) * gv
        nv = ADAM_B2 * v_ref[...] + (1.0 - ADAM_B2) * jnp.square(gv)
        m_hat = nm / (1.0 - ADAM_B1 ** ADAM_STEP)
        v_hat = nv / (1.0 - ADAM_B2 ** ADAM_STEP)
        d_ref[...] = -ADAM_LR * (m_hat / (jnp.sqrt(v_hat) + ADAM_EPS) + ADAM_WD * w_ref[...])
        nm_ref[...] = nm
        nv_ref[...] = nv

    spec = pl.BlockSpec((tr, cols), lambda i: (i, 0))
    return pl.pallas_call(
        body,
        out_shape=[jax.ShapeDtypeStruct((rows, cols), F32)] * 3,
        grid=(rows // tr,),
        in_specs=[spec] * 4,
        out_specs=[spec] * 3,
        compiler_params=_params("parallel"),
        name=name,
    )(w, g, m, v)


BIG = ("even_w_in", "odd_w_qkv", "mlp_w1_0", "mlp_w1_1", "even_w_out", "odd_w_o", "mlp_w2_0", "mlp_w2_1")
BIG_KIND = ("col", "col", "col", "col", "row", "row", "row", "row")
SUBLANES = 8


def _pack_rows(parts, width):
    padded, offsets, r0 = [], [], 0
    for t in parts:
        rows = -(-t.shape[0] // SUBLANES) * SUBLANES
        padded.append(jnp.pad(t, ((0, rows - t.shape[0]), (0, width - t.shape[1]))))
        offsets.append(r0)
        r0 += rows
    return jnp.concatenate(padded, axis=0), offsets


def kernel(x, meta, norm_mix_g, norm_mlp_g, even_w_in, even_ret_gn_g, even_conv_w, even_conv_b, even_conv_ln_g, even_conv_ln_b, even_w_out, odd_w_qkv, odd_q_norm_g, odd_k_norm_g, odd_w_o, mlp_w1, mlp_w2, loss_target, m_meta, m_norm_mix_g, m_norm_mlp_g, m_even_w_in, m_even_ret_gn_g, m_even_conv_w, m_even_conv_b, m_even_conv_ln_g, m_even_conv_ln_b, m_even_w_out, m_odd_w_qkv, m_odd_q_norm_g, m_odd_k_norm_g, m_odd_w_o, m_mlp_w1, m_mlp_w2, v_meta, v_norm_mix_g, v_norm_mlp_g, v_even_w_in, v_even_ret_gn_g, v_even_conv_w, v_even_conv_b, v_even_conv_ln_g, v_even_conv_ln_b, v_even_w_out, v_odd_w_qkv, v_odd_q_norm_g, v_odd_k_norm_g, v_odd_w_o, v_mlp_w1, v_mlp_w2):
    d = D_MODEL
    xi, yi, ci = lax.axis_index("x"), lax.axis_index("y"), lax.axis_index("c")
    chip = 2 * xi + yi
    c_arr = jnp.reshape(ci, (1,)).astype(jnp.int32)
    s_arr = jnp.reshape(chip, (1,)).astype(jnp.int32)

    def split_big(w_in, w_qkv, w1, w_out, w_o, w2):
        return dict(zip(BIG, (w_in[0], w_qkv[0], w1[0], w1[1], w_out[0], w_o[0], w2[0], w2[1])))

    w_big = split_big(even_w_in, odd_w_qkv, mlp_w1, even_w_out, odd_w_o, mlp_w2)
    m_big = split_big(m_even_w_in, m_odd_w_qkv, m_mlp_w1, m_even_w_out, m_odd_w_o, m_mlp_w2)
    v_big = split_big(v_even_w_in, v_odd_w_qkv, v_mlp_w1, v_even_w_out, v_odd_w_o, v_mlp_w2)

    shards16 = [_cast_bf16(w_big[n], name="cast_" + n) for n in BIG]
    full = dict(zip(BIG, _allgather_weights(shards16, BIG_KIND)))

    packed, (r_meta, r_conv, r_gn) = _pack_rows([meta, even_conv_w[0], even_ret_gn_g[0]], d // N_CHIPS)
    gathered = _allgather8(packed, name="allgather_small_params")[0::2]
    across = lambda r0, rows, width: jnp.concatenate([gathered[s, r0:r0 + rows, 0:width] for s in range(N_CHIPS)], axis=1)
    meta_full = across(r_meta, N_META, d // N_CHIPS)
    conv_w_full = across(r_conv, CONV_WIDTH, d // N_CHIPS)
    gn_full = across(r_gn, RET_HEADS, RET_V_DIM // N_CHIPS)

    sq, g = _local_step(
        x[0], loss_target[0], meta_full, norm_mix_g, norm_mlp_g, full["even_w_in"], gn_full, conv_w_full,
        even_conv_b[0], even_conv_ln_g[0], even_conv_ln_b[0], full["even_w_out"], full["odd_w_qkv"],
        odd_q_norm_g[0], odd_k_norm_g[0], full["odd_w_o"], (full["mlp_w1_0"], full["mlp_w1_1"]),
        (full["mlp_w2_0"], full["mlp_w2_1"]))
    loss = lax.psum(0.5 * sq / d, ("x", "y", "c"))

    small_names = ("norm_mix_g", "norm_mlp_g", "even_conv_b", "even_conv_ln_g", "even_conv_ln_b", "odd_q_norm_g",
                   "odd_k_norm_g", "meta", "even_conv_w", "even_ret_gn_g")
    pack, offsets = _pack_rows([g[n] for n in small_names], d)
    summed = _sum8(_allgather8(pack, name="allgather_small_grads"), name="sum_small_grads")
    small = {n: summed[r0:r0 + g[n].shape[0], 0:g[n].shape[1]] for n, r0 in zip(small_names, offsets)}
    for n in ("meta", "even_conv_w", "even_ret_gn_g"):
        width = small[n].shape[1] // N_CHIPS
        small[n] = lax.dynamic_slice_in_dim(small[n], chip * width, width, axis=1)

    g_big = [g["even_w_in"], g["odd_w_qkv"], g["mlp_w1"][0], g["mlp_w1"][1], g["even_w_out"], g["odd_w_o"],
             g["mlp_w2"][0], g["mlp_w2"][1]]
    theirs = _swap_halves_in(g_big, BIG_KIND)
    sums = [_half_add(gb, th, k, c_arr, name="pair_sum_" + n) for gb, th, k, n in zip(g_big, theirs, BIG_KIND, BIG)]
    recv = _exchange_chips([s16 for _, s16 in sums], BIG_KIND)
    halves = [_shard_sum(s32, r, k, s_arr, name="chip_sum_" + n) for (s32, _), r, k, n in zip(sums, recv, BIG_KIND, BIG)]
    grad_big = dict(zip(BIG, _swap_halves_out(halves, BIG_KIND)))

    upd = {n: _adamw(w_big[n], grad_big[n], m_big[n], v_big[n], name="adamw_" + n) for n in BIG}

    def join(name, idx, lead):
        if name in ("mlp_w1", "mlp_w2"):
            return jnp.stack([upd[name + "_0"][idx], upd[name + "_1"][idx]]) if idx >= 0 else jnp.stack(
                [grad_big[name + "_0"], grad_big[name + "_1"]])
        t = upd[name][idx] if idx >= 0 else grad_big[name]
        return t[None] if lead else t

    small_w = dict(meta=meta, norm_mix_g=norm_mix_g, norm_mlp_g=norm_mlp_g, even_ret_gn_g=even_ret_gn_g[0],
                   even_conv_w=even_conv_w[0], even_conv_b=even_conv_b, even_conv_ln_g=even_conv_ln_g,
                   even_conv_ln_b=even_conv_ln_b, odd_q_norm_g=odd_q_norm_g, odd_k_norm_g=odd_k_norm_g)
    small_m = dict(meta=m_meta, norm_mix_g=m_norm_mix_g, norm_mlp_g=m_norm_mlp_g, even_ret_gn_g=m_even_ret_gn_g[0],
                   even_conv_w=m_even_conv_w[0], even_conv_b=m_even_conv_b, even_conv_ln_g=m_even_conv_ln_g,
                   even_conv_ln_b=m_even_conv_ln_b, odd_q_norm_g=m_odd_q_norm_g, odd_k_norm_g=m_odd_k_norm_g)
    small_v = dict(meta=v_meta, norm_mix_g=v_norm_mix_g, norm_mlp_g=v_norm_mlp_g, even_ret_gn_g=v_even_ret_gn_g[0],
                   even_conv_w=v_even_conv_w[0], even_conv_b=v_even_conv_b, even_conv_ln_g=v_even_conv_ln_g,
                   even_conv_ln_b=v_even_conv_ln_b, odd_q_norm_g=v_odd_q_norm_g, odd_k_norm_g=v_odd_k_norm_g)
    small_upd = {n: _adamw(small_w[n], small[n], small_m[n], small_v[n], name="adamw_" + n) for n in small_w}
    leading = ("even_ret_gn_g", "even_conv_w")

    order = ("meta", "norm_mix_g", "norm_mlp_g", "even_w_in", "even_ret_gn_g", "even_conv_w", "even_conv_b",
             "even_conv_ln_g", "even_conv_ln_b", "even_w_out", "odd_w_qkv", "odd_q_norm_g", "odd_k_norm_g", "odd_w_o",
             "mlp_w1", "mlp_w2")
    big_lead = ("even_w_in", "even_w_out", "odd_w_qkv", "odd_w_o")

    def leaf(name, idx):
        if name in small_w:
            t = small_upd[name][idx] if idx >= 0 else small[name]
            return t[None] if name in leading else t
        return join(name, idx, name in big_lead)

    outs = [loss, g["x"][None]]
    for idx in (-1, 0, 1, 2):
        outs += [leaf(n, idx) for n in order]
    return tuple(outs)
```

```python
import functools

import jax
import jax.numpy as jnp
from jax import lax
from jax.experimental import pallas as pl
from jax.experimental.pallas import tpu as pltpu

F32 = jnp.float32
BF16 = jnp.bfloat16

D_MODEL = 1024
N_META = 16
CHUNK = 128
PAD_FRONT = (-N_META) % CHUNK
RET_HEADS = 4
RET_QK_DIM = 128
RET_V_DIM = 256
RET_QK_W = RET_HEADS * RET_QK_DIM
RET_V_W = RET_HEADS * RET_V_DIM
CONV_WIDTH = 31
CONV_HALO = 32
RET_DECAY_OFFSET = 5.0
ROPE_BASE = 10000.0
SB_HEADS = 16
SB_HEAD_DIM = 64
D_FF = 4 * D_MODEL
EPS = 1e-6
ADAM_LR = 0.001
ADAM_B1 = 0.9
ADAM_B2 = 0.999
ADAM_EPS = 1e-08
ADAM_WD = 0.01
ADAM_STEP = 10

N_CHIPS = 4
N_DEV = 8
VMEM_LIMIT = 56 * 1024 * 1024
MESH = pl.DeviceIdType.MESH
ANY = pl.BlockSpec(memory_space=pl.ANY)


def _params(*sem):
    return pltpu.CompilerParams(dimension_semantics=sem, vmem_limit_bytes=VMEM_LIMIT)


def _pick(n, cands):
    for c in cands:
        if n % c == 0:
            return c
    return n


def _sigmoid(x):
    return 1.0 / (1.0 + jnp.exp(-x))


def _dot(a, b):
    return lax.dot_general(a, b, (((1,), (0,)), ((), ())), preferred_element_type=F32)


def _dot_nt(a, b):
    return lax.dot_general(a, b, (((1,), (1,)), ((), ())), preferred_element_type=F32)


def _dot_tn(a, b):
    return lax.dot_general(a, b, (((0,), (0,)), ((), ())), preferred_element_type=F32)


def _split_dot(x, m):
    hi = x.astype(BF16)
    lo = (x - hi.astype(F32)).astype(BF16)
    return _dot(hi, m) + _dot(lo, m)


def _matmul(a, b, *, mode, out_dtypes, epilogue=None, extras=(), name):
    if mode == "nn":
        (m, k), (k2, n) = a.shape, b.shape
    elif mode == "nt":
        (m, k), (n, k2) = a.shape, b.shape
    else:
        (k, m), (k2, n) = a.shape, b.shape
    assert k == k2, (a.shape, b.shape, mode)
    tm = _pick(m, (1056, 1024, 768, 512, 384, 256, 128, 96))
    tn = _pick(n, (1024, 768, 512, 256, 128))
    tk = _pick(k, (1056, 1024, 768, 512, 384, 256, 128, 96))
    nk = k // tk
    dot = {"nn": _dot, "nt": _dot_nt, "tn": _dot_tn}[mode]
    n_extra, n_out = len(extras), len(out_dtypes)
    if epilogue is None:
        epilogue = lambda acc: (acc,)

    def body(a_ref, b_ref, *rest):
        extra_refs = rest[:n_extra]
        out_refs = rest[n_extra:n_extra + n_out]
        part = dot(a_ref[...].astype(BF16), b_ref[...].astype(BF16))

        def finish(acc):
            res = epilogue(acc, *[r[...] for r in extra_refs])
            for o_ref, r in zip(out_refs, res):
                o_ref[...] = r.astype(o_ref.dtype)

        if nk == 1:
            finish(part)
        else:
            acc_ref = rest[-1]
            kk = pl.program_id(2)

            @pl.when(kk == 0)
            def _():
                acc_ref[...] = part

            @pl.when(kk > 0)
            def _():
                acc_ref[...] += part

            @pl.when(kk == nk - 1)
            def _():
                finish(acc_ref[...])

    if mode == "nn":
        a_spec = pl.BlockSpec((tm, tk), lambda i, j, kk: (i, kk))
        b_spec = pl.BlockSpec((tk, tn), lambda i, j, kk: (kk, j))
    elif mode == "nt":
        a_spec = pl.BlockSpec((tm, tk), lambda i, j, kk: (i, kk))
        b_spec = pl.BlockSpec((tn, tk), lambda i, j, kk: (j, kk))
    else:
        a_spec = pl.BlockSpec((tk, tm), lambda i, j, kk: (kk, i))
        b_spec = pl.BlockSpec((tk, tn), lambda i, j, kk: (kk, j))
    o_spec = pl.BlockSpec((tm, tn), lambda i, j, kk: (i, j))
    outs = pl.pallas_call(
        body,
        out_shape=[jax.ShapeDtypeStruct((m, n), dt) for dt in out_dtypes],
        grid=(m // tm, n // tn, nk),
        in_specs=[a_spec, b_spec] + [o_spec] * n_extra,
        out_specs=[o_spec] * n_out,
        scratch_shapes=[pltpu.VMEM((tm, tn), F32)] if nk > 1 else [],
        compiler_params=_params("parallel", "parallel", "arbitrary"),
        name=name,
    )(a, b, *extras)
    return outs[0] if n_out == 1 else outs


def _add_epilogue(acc, res):
    return (res + acc,)


def _rmsnorm_fwd(x, g, *, name):
    p, d = x.shape
    rows = _pick(p, (384, 128, 96))

    def body(x_ref, g_ref, o_ref):
        xv = x_ref[...]
        r = lax.rsqrt(jnp.mean(xv * xv, axis=-1, keepdims=True) + EPS)
        o_ref[...] = (xv * r * g_ref[...]).astype(o_ref.dtype)

    return pl.pallas_call(
        body,
        out_shape=jax.ShapeDtypeStruct((p, d), BF16),
        grid=(p // rows,),
        in_specs=[pl.BlockSpec((rows, d), lambda i: (i, 0)), pl.BlockSpec((1, d), lambda i: (0, 0))],
        out_specs=pl.BlockSpec((rows, d), lambda i: (i, 0)),
        compiler_params=_params("parallel"),
        name=name,
    )(x, g)


def _rmsnorm_bwd(x, g, dy, dres, *, name):
    p, d = x.shape
    rows = _pick(p, (384, 128, 96))

    def body(x_ref, g_ref, dy_ref, dres_ref, dx_ref, dg_ref):
        xv = x_ref[...]
        r = lax.rsqrt(jnp.mean(xv * xv, axis=-1, keepdims=True) + EPS)
        dyv = dy_ref[...]
        gdy = dyv * g_ref[...]
        proj = jnp.mean(xv * gdy, axis=-1, keepdims=True)
        dx_ref[...] = dres_ref[...] + r * gdy - xv * (r * r * r) * proj
        part = jnp.sum(dyv * xv * r, axis=0, keepdims=True)

        @pl.when(pl.program_id(0) == 0)
        def _():
            dg_ref[...] = part

        @pl.when(pl.program_id(0) > 0)
        def _():
            dg_ref[...] += part

    row_spec = pl.BlockSpec((rows, d), lambda i: (i, 0))
    vec_spec = pl.BlockSpec((1, d), lambda i: (0, 0))
    return pl.pallas_call(
        body,
        out_shape=[jax.ShapeDtypeStruct((p, d), F32), jax.ShapeDtypeStruct((1, d), F32)],
        grid=(p // rows,),
        in_specs=[row_spec, vec_spec, row_spec, row_spec],
        out_specs=[row_spec, vec_spec],
        compiler_params=_params("arbitrary"),
        name=name,
    )(x, g, dy, dres)


def _mlp_fwd(h, g, w1, w2, *, name):
    hn = _rmsnorm_fwd(h, g, name=name + "_norm")

    def act(acc):
        r = jnp.maximum(acc, 0.0)
        return acc, r * r

    z, a2 = _matmul(hn, w1, mode="nn", out_dtypes=(F32, BF16), epilogue=act, name=name + "_up")
    out = _matmul(a2, w2, mode="nn", out_dtypes=(F32,), epilogue=_add_epilogue, extras=(h,), name=name + "_down")
    return out, (hn, z, a2)


def _mlp_bwd(h, g, w1, w2, saved, dout, *, name):
    hn, z, a2 = saved

    def dact(acc, zt):
        return (acc * (2.0 * jnp.maximum(zt, 0.0)),)

    dz = _matmul(dout, w2, mode="nt", out_dtypes=(BF16,), epilogue=dact, extras=(z,), name=name + "_dz")
    dw2 = _matmul(a2, dout, mode="tn", out_dtypes=(F32,), name=name + "_dw2")
    dw1 = _matmul(hn, dz, mode="tn", out_dtypes=(F32,), name=name + "_dw1")
    dhn = _matmul(dz, w1, mode="nt", out_dtypes=(F32,), name=name + "_dhn")
    dh, dg = _rmsnorm_bwd(h, g, dhn, dout, name=name + "_dnorm")
    return dh, dg, dw1, dw2


def _retention_tables(p):
    half = RET_QK_DIM // 2
    inv_freq = ROPE_BASE ** (-jnp.arange(half, dtype=F32) / half)
    ang = jnp.arange(p, dtype=F32)[:, None] * inv_freq[None, :]
    cos, sin = jnp.cos(ang), jnp.sin(ang)
    cosf = jnp.concatenate([cos, cos], axis=1)
    sins = jnp.concatenate([-sin, sin], axis=1)
    log_g = jnp.log1p(-jnp.exp2(-RET_DECAY_OFFSET - jnp.arange(RET_HEADS, dtype=F32)))
    idx = jnp.arange(CHUNK, dtype=F32)
    diff = idx[:, None] - idx[None, :]
    inner = jnp.where(diff[None] >= 0, jnp.exp(jnp.maximum(diff, 0.0)[None] * log_g[:, None, None]), 0.0)
    kdec = jnp.exp((CHUNK - 1 - idx)[None, :] * log_g[:, None])
    qdec = jnp.exp((idx + 1.0)[None, :] * log_g[:, None])
    cdec = jnp.exp(CHUNK * log_g)
    kdec = jnp.broadcast_to(kdec[:, :, None], (RET_HEADS, CHUNK, RET_QK_DIM))
    qdec = jnp.broadcast_to(qdec[:, :, None], (RET_HEADS, CHUNK, RET_QK_DIM))
    cdec = jnp.broadcast_to(cdec[:, None, None], (RET_HEADS, RET_QK_DIM, RET_V_DIM))
    return cosf, sins, inner, kdec, qdec, cdec


def _rot(x, cosf, sins):
    return x * cosf + pltpu.roll(x, RET_QK_DIM // 2, 1) * sins


def _rot_bwd(dy, cosf, sins):
    return dy * cosf + pltpu.roll(dy * sins, RET_QK_DIM // 2, 1)


def _ret_in_specs(chunk_of):
    nh = RET_HEADS
    q_spec = pl.BlockSpec((CHUNK, RET_QK_DIM), lambda h, s: (chunk_of(s), h))
    k_spec = pl.BlockSpec((CHUNK, RET_QK_DIM), lambda h, s: (chunk_of(s), nh + h))
    v_spec = pl.BlockSpec((CHUNK, RET_V_DIM), lambda h, s: (chunk_of(s), nh + h))
    g_spec = pl.BlockSpec((CHUNK, RET_V_DIM), lambda h, s: (chunk_of(s), 2 * nh + h))
    rope_spec = pl.BlockSpec((CHUNK, RET_QK_DIM), lambda h, s: (chunk_of(s), 0))
    head_sq = pl.BlockSpec((None, CHUNK, CHUNK), lambda h, s: (h, 0, 0))
    head_qk = pl.BlockSpec((None, CHUNK, RET_QK_DIM), lambda h, s: (h, 0, 0))
    head_st = pl.BlockSpec((None, RET_QK_DIM, RET_V_DIM), lambda h, s: (h, 0, 0))
    gam_spec = pl.BlockSpec((None, 1, RET_V_DIM), lambda h, s: (h, 0, 0))
    return [q_spec, k_spec, v_spec, g_spec, rope_spec, rope_spec, head_sq, head_qk, head_qk, head_st, gam_spec]


def _retention_fwd(proj, gn_g, tables, *, name):
    p = proj.shape[0]
    n_chunks = p // CHUNK
    scale = RET_QK_DIM ** -0.5

    def body(q_ref, k_ref, v_ref, g_ref, cos_ref, sin_ref, inner_ref, kdec_ref, qdec_ref, cdec_ref, gam_ref,
             og_ref, opre_ref, sprev_ref, s_scr):
        @pl.when(pl.program_id(1) == 0)
        def _():
            s_scr[...] = jnp.zeros_like(s_scr)

        cosf, sins = cos_ref[...], sin_ref[...]
        qr = _rot(q_ref[...], cosf, sins)
        kr = _rot(k_ref[...], cosf, sins) * scale
        vb = v_ref[...].astype(BF16)
        scores = _dot_nt(qr.astype(BF16), kr.astype(BF16)) * inner_ref[...]
        state = s_scr[...]
        sprev_ref[...] = state
        o = _dot(scores.astype(BF16), vb) + _dot((qr * qdec_ref[...]).astype(BF16), state.astype(BF16))
        kd = kr * kdec_ref[...]
        s_scr[...] = cdec_ref[...] * state + _dot(kd.T.astype(BF16), vb)
        opre_ref[...] = o
        mu = jnp.mean(o, axis=-1, keepdims=True)
        oc = o - mu
        var = jnp.mean(oc * oc, axis=-1, keepdims=True)
        on = oc * lax.rsqrt(var + EPS) * gam_ref[...]
        gv = g_ref[...]
        og_ref[...] = (gv * _sigmoid(gv) * on).astype(og_ref.dtype)

    chunk_of = lambda s: s
    out_v = pl.BlockSpec((CHUNK, RET_V_DIM), lambda h, s: (s, h))
    return pl.pallas_call(
        body,
        out_shape=[
            jax.ShapeDtypeStruct((p, RET_V_W), BF16),
            jax.ShapeDtypeStruct((p, RET_V_W), F32),
            jax.ShapeDtypeStruct((RET_HEADS, n_chunks, RET_QK_DIM, RET_V_DIM), F32),
        ],
        grid=(RET_HEADS, n_chunks),
        in_specs=_ret_in_specs(chunk_of),
        out_specs=[out_v, out_v, pl.BlockSpec((None, None, RET_QK_DIM, RET_V_DIM), lambda h, s: (h, s, 0, 0))],
        scratch_shapes=[pltpu.VMEM((RET_QK_DIM, RET_V_DIM), F32)],
        compiler_params=_params("parallel", "arbitrary"),
        name=name,
    )(proj, proj, proj, proj, *tables, gn_g.reshape(RET_HEADS, 1, RET_V_DIM))


def _retention_bwd(proj, gn_g, tables, opre, sprev, dog, *, name):
    p = proj.shape[0]
    n_chunks = p // CHUNK
    scale = RET_QK_DIM ** -0.5

    def body(q_ref, k_ref, v_ref, g_ref, cos_ref, sin_ref, inner_ref, kdec_ref, qdec_ref, cdec_ref, gam_ref,
             opre_ref, sprev_ref, dog_ref, dq_ref, dk_ref, dv_ref, dg_ref, dgam_ref, ds_scr):
        first = pl.program_id(1) == 0

        @pl.when(first)
        def _():
            ds_scr[...] = jnp.zeros_like(ds_scr)

        cosf, sins = cos_ref[...], sin_ref[...]
        qr = _rot(q_ref[...], cosf, sins)
        kr = _rot(k_ref[...], cosf, sins) * scale
        qb, kb = qr.astype(BF16), kr.astype(BF16)
        vb = v_ref[...].astype(BF16)
        inner = inner_ref[...]
        qdec, kdec = qdec_ref[...], kdec_ref[...]
        state_b = sprev_ref[...].astype(BF16)
        o = opre_ref[...]
        mu = jnp.mean(o, axis=-1, keepdims=True)
        oc = o - mu
        rstd = lax.rsqrt(jnp.mean(oc * oc, axis=-1, keepdims=True) + EPS)
        xhat = oc * rstd
        gam = gam_ref[...]
        on = xhat * gam
        gv = g_ref[...]
        sig = _sigmoid(gv)
        dogv = dog_ref[...]
        dg_ref[...] = (dogv * on * sig * (1.0 + gv * (1.0 - sig))).astype(dg_ref.dtype)
        don = dogv * gv * sig
        dgam_part = jnp.sum(don * xhat, axis=0, keepdims=True)

        @pl.when(first)
        def _():
            dgam_ref[...] = dgam_part

        @pl.when(jnp.logical_not(first))
        def _():
            dgam_ref[...] += dgam_part

        dxhat = don * gam
        do = rstd * (dxhat - jnp.mean(dxhat, axis=-1, keepdims=True)
                     - xhat * jnp.mean(dxhat * xhat, axis=-1, keepdims=True))
        dob = do.astype(BF16)
        scores_b = (_dot_nt(qb, kb) * inner).astype(BF16)
        da = (_dot_nt(dob, vb) * inner).astype(BF16)
        dv = _dot(scores_b.astype(F32).T.astype(BF16), dob)
        dqr = _dot(da, kb)
        dkr = _dot(da.astype(F32).T.astype(BF16), qb)
        dqr += _dot_nt(dob, state_b) * qdec
        ds_local = _dot((qr * qdec).T.astype(BF16), dob)
        gstate = ds_scr[...]
        gb = gstate.astype(BF16)
        kd_b = (kr * kdec).astype(BF16)
        dkr += _dot_nt(vb, gb) * kdec
        dv += _dot(kd_b, gb)
        ds_scr[...] = cdec_ref[...] * gstate + ds_local
        dq_ref[...] = _rot_bwd(dqr, cosf, sins).astype(dq_ref.dtype)
        dk_ref[...] = _rot_bwd(dkr * scale, cosf, sins).astype(dk_ref.dtype)
        dv_ref[...] = dv.astype(dv_ref.dtype)

    chunk_of = lambda s: n_chunks - 1 - s
    blk_v = pl.BlockSpec((CHUNK, RET_V_DIM), lambda h, s: (chunk_of(s), h))
    blk_qk = pl.BlockSpec((CHUNK, RET_QK_DIM), lambda h, s: (chunk_of(s), h))
    st_spec = pl.BlockSpec((None, None, RET_QK_DIM, RET_V_DIM), lambda h, s: (h, chunk_of(s), 0, 0))
    return pl.pallas_call(
        body,
        out_shape=[
            jax.ShapeDtypeStruct((p, RET_QK_W), BF16),
            jax.ShapeDtypeStruct((p, RET_QK_W), BF16),
            jax.ShapeDtypeStruct((p, RET_V_W), BF16),
            jax.ShapeDtypeStruct((p, RET_V_W), BF16),
            jax.ShapeDtypeStruct((RET_HEADS, 1, RET_V_DIM), F32),
        ],
        grid=(RET_HEADS, n_chunks),
        in_specs=_ret_in_specs(chunk_of) + [blk_v, st_spec, blk_v],
        out_specs=[blk_qk, blk_qk, blk_v, blk_v, pl.BlockSpec((None, 1, RET_V_DIM), lambda h, s: (h, 0, 0))],
        scratch_shapes=[pltpu.VMEM((RET_QK_DIM, RET_V_DIM), F32)],
        compiler_params=_params("parallel", "arbitrary"),
        name=name,
    )(proj, proj, proj, proj, *tables, gn_g.reshape(RET_HEADS, 1, RET_V_DIM), opre, sprev, dog)


def _conv_rows(p):
    return _pick(p, (384, 128))


def _ln_stats(y):
    mu = jnp.mean(y, axis=-1, keepdims=True)
    yc = y - mu
    rstd = lax.rsqrt(jnp.mean(yc * yc, axis=-1, keepdims=True) + EPS)
    return yc * rstd, rstd


def _conv_fwd(proj, conv_w, conv_b, ln_g, ln_b, *, name):
    p = proj.shape[0]
    c = D_MODEL
    rows = _conv_rows(p)
    hpb = rows // CONV_HALO
    a_col, gate_col = (2 * RET_QK_W + 2 * RET_V_W) // c, (2 * RET_QK_W + 2 * RET_V_W) // c + 1

    def body(a_ref, gate_ref, ah_ref, gateh_ref, w_ref, b_ref, lg_ref, lb_ref, c_ref, y_ref, hdn_scr):
        i = pl.program_id(0)
        hdn_scr[0:CONV_HALO, :] = ah_ref[...] * _sigmoid(gateh_ref[...])
        hdn_scr[CONV_HALO:, :] = a_ref[...] * _sigmoid(gate_ref[...])
        acc = jnp.zeros((rows, c), F32)
        for w in range(CONV_WIDTH):
            off = CONV_HALO - (CONV_WIDTH - 1) + w
            acc += hdn_scr[off:off + rows, :] * w_ref[w:w + 1, :]
        y = acc + b_ref[...]
        y_ref[...] = y
        yhat, _ = _ln_stats(y)
        ln = yhat * lg_ref[...] + lb_ref[...]
        row = i * rows + lax.broadcasted_iota(jnp.int32, (rows, 1), 0)
        c_ref[...] = jnp.where(row >= PAD_FRONT, ln * _sigmoid(ln), 0.0).astype(c_ref.dtype)

    halo_idx = lambda i: jnp.maximum(i * hpb - 1, 0)
    vec = pl.BlockSpec((1, c), lambda i: (0, 0))
    return pl.pallas_call(
        body,
        out_shape=[jax.ShapeDtypeStruct((p, c), BF16), jax.ShapeDtypeStruct((p, c), F32)],
        grid=(p // rows,),
        in_specs=[
            pl.BlockSpec((rows, c), lambda i: (i, a_col)),
            pl.BlockSpec((rows, c), lambda i: (i, gate_col)),
            pl.BlockSpec((CONV_HALO, c), lambda i: (halo_idx(i), a_col)),
            pl.BlockSpec((CONV_HALO, c), lambda i: (halo_idx(i), gate_col)),
            pl.BlockSpec((CONV_WIDTH, c), lambda i: (0, 0)),
            vec, vec, vec,
        ],
        out_specs=[pl.BlockSpec((rows, c), lambda i: (i, 0)), pl.BlockSpec((rows, c), lambda i: (i, 0))],
        scratch_shapes=[pltpu.VMEM((CONV_HALO + rows, c), F32)],
        compiler_params=_params("parallel"),
        name=name,
    )(proj, proj, proj, proj, conv_w, conv_b, ln_g, ln_b)


def _conv_bwd(proj, conv_w, ln_g, ln_b, y, dcat, *, name):
    p = proj.shape[0]
    c = D_MODEL
    rows = _conv_rows(p)
    hpb = rows // CONV_HALO
    n_blocks = p // rows
    a_col, gate_col = (2 * RET_QK_W + 2 * RET_V_W) // c, (2 * RET_QK_W + 2 * RET_V_W) // c + 1

    def body(a_ref, gate_ref, ah_ref, gateh_ref, w_ref, lg_ref, lb_ref, y_ref, yh_ref, dc_ref, dch_ref,
             da_ref, dgate_ref, dw_ref, db_ref, dlg_ref, dlb_ref, hdn_scr, dy_scr):
        i = pl.program_id(0)
        lg, lb = lg_ref[...], lb_ref[...]

        def ln_bwd(yv, dcv):
            yhat, rstd = _ln_stats(yv)
            ln = yhat * lg + lb
            sig = _sigmoid(ln)
            dln = dcv * sig * (1.0 + ln * (1.0 - sig))
            dyhat = dln * lg
            dyv = rstd * (dyhat - jnp.mean(dyhat, axis=-1, keepdims=True)
                          - yhat * jnp.mean(dyhat * yhat, axis=-1, keepdims=True))
            return dyv, dln, yhat

        row = i * rows + lax.broadcasted_iota(jnp.int32, (rows, 1), 0)
        dy, dln, yhat = ln_bwd(y_ref[...], jnp.where(row >= PAD_FRONT, dc_ref[...], 0.0))
        dy_halo, _, _ = ln_bwd(yh_ref[...], dch_ref[...])
        dy_scr[0:rows, :] = dy
        dy_scr[rows:, :] = jnp.where(i == n_blocks - 1, 0.0, dy_halo)
        sig_gate = _sigmoid(gate_ref[...])
        av = a_ref[...]
        hdn_scr[0:CONV_HALO, :] = ah_ref[...] * _sigmoid(gateh_ref[...])
        hdn_scr[CONV_HALO:, :] = av * sig_gate
        @pl.when(i == 0)
        def _():
            dw_ref[...] = jnp.zeros_like(dw_ref)
            db_ref[...] = jnp.zeros_like(db_ref)
            dlg_ref[...] = jnp.zeros_like(dlg_ref)
            dlb_ref[...] = jnp.zeros_like(dlb_ref)

        dhdn = jnp.zeros((rows, c), F32)
        for w in range(CONV_WIDTH):
            back = CONV_WIDTH - 1 - w
            dhdn += dy_scr[back:back + rows, :] * w_ref[w:w + 1, :]
            off = CONV_HALO - (CONV_WIDTH - 1) + w
            dw_ref[w:w + 1, :] += jnp.sum(dy * hdn_scr[off:off + rows, :], axis=0, keepdims=True)
        da_ref[...] = (dhdn * sig_gate).astype(da_ref.dtype)
        dgate_ref[...] = (dhdn * av * sig_gate * (1.0 - sig_gate)).astype(dgate_ref.dtype)
        db_ref[...] += jnp.sum(dy, axis=0, keepdims=True)
        dlg_ref[...] += jnp.sum(dln * yhat, axis=0, keepdims=True)
        dlb_ref[...] += jnp.sum(dln, axis=0, keepdims=True)

    prev_halo = lambda i: jnp.maximum(i * hpb - 1, 0)
    next_halo = lambda i: jnp.minimum((i + 1) * hpb, p // CONV_HALO - 1)
    vec = pl.BlockSpec((1, c), lambda i: (0, 0))
    blk = lambda col: pl.BlockSpec((rows, c), lambda i: (i, col))
    outs = pl.pallas_call(
        body,
        out_shape=[
            jax.ShapeDtypeStruct((p, c), BF16),
            jax.ShapeDtypeStruct((p, c), BF16),
            jax.ShapeDtypeStruct((CONV_WIDTH + 1, c), F32),
            jax.ShapeDtypeStruct((1, c), F32),
            jax.ShapeDtypeStruct((1, c), F32),
            jax.ShapeDtypeStruct((1, c), F32),
        ],
        grid=(n_blocks,),
        in_specs=[
            blk(a_col), blk(gate_col),
            pl.BlockSpec((CONV_HALO, c), lambda i: (prev_halo(i), a_col)),
            pl.BlockSpec((CONV_HALO, c), lambda i: (prev_halo(i), gate_col)),
            pl.BlockSpec((CONV_WIDTH, c), lambda i: (0, 0)),
            vec, vec,
            blk(0),
            pl.BlockSpec((CONV_HALO, c), lambda i: (next_halo(i), 0)),
            blk(1),
            pl.BlockSpec((CONV_HALO, c), lambda i: (next_halo(i), 1)),
        ],
        out_specs=[blk(0), blk(0), pl.BlockSpec((CONV_WIDTH + 1, c), lambda i: (0, 0)), vec, vec, vec],
        scratch_shapes=[pltpu.VMEM((CONV_HALO + rows, c), F32), pltpu.VMEM((rows + CONV_HALO, c), F32)],
        compiler_params=_params("arbitrary"),
        name=name,
    )(proj, proj, proj, proj, conv_w, ln_g, ln_b, y, y, dcat, dcat)
    da, dgate, dw, db, dlg, dlb = outs
    return da, dgate, dw[:CONV_WIDTH], db, dlg, dlb


def _group_matrix():
    r = jnp.arange(D_MODEL)[:, None] // SB_HEAD_DIM
    c = jnp.arange(D_MODEL)[None, :] // SB_HEAD_DIM
    return (r == c).astype(BF16)


def _qknorm_fwd(qkv, qg, kg, *, name):
    p = qkv.shape[0]
    d = D_MODEL
    rows = _pick(p, (384, 128, 96))

    def body(q_ref, k_ref, v_ref, qg_ref, kg_ref, gm_ref, qn_ref, kn_ref, vb_ref):
        gm = gm_ref[...]

        def norm(x, g):
            ms = _split_dot(x * x, gm) * (1.0 / SB_HEAD_DIM)
            return x * lax.rsqrt(ms + EPS) * g

        qn_ref[...] = norm(q_ref[...], qg_ref[...]).astype(BF16)
        kn_ref[...] = norm(k_ref[...], kg_ref[...]).astype(BF16)
        vb_ref[...] = v_ref[...].astype(BF16)

    blk = lambda col: pl.BlockSpec((rows, d), lambda i: (i, col))
    vec = pl.BlockSpec((1, d), lambda i: (0, 0))
    return pl.pallas_call(
        body,
        out_shape=[jax.ShapeDtypeStruct((p, d), BF16)] * 3,
        grid=(p // rows,),
        in_specs=[blk(0), blk(1), blk(2), vec, vec, pl.BlockSpec((d, d), lambda i: (0, 0))],
        out_specs=[blk(0)] * 3,
        compiler_params=_params("parallel"),
        name=name,
    )(qkv, qkv, qkv, qg, kg, _group_matrix())


def _qknorm_bwd(qkv, qg, kg, dqn, dkn, dv, *, name):
    p = qkv.shape[0]
    d = D_MODEL
    rows = _pick(p, (384, 128, 96))

    def body(q_ref, k_ref, qg_ref, kg_ref, gm_ref, dqn_ref, dkn_ref, dv_ref, dqkv_ref, dqg_ref, dkg_ref):
        gm = gm_ref[...]

        def bwd(x, g, dy):
            ms = _split_dot(x * x, gm) * (1.0 / SB_HEAD_DIM)
            r = lax.rsqrt(ms + EPS)
            gdy = dy * g
            proj = _split_dot(x * gdy, gm) * (1.0 / SB_HEAD_DIM)
            return r * gdy - x * (r * r * r) * proj, jnp.sum(dy * x * r, axis=0, keepdims=True)

        dq, dqg = bwd(q_ref[...], qg_ref[...], dqn_ref[...])
        dk, dkg = bwd(k_ref[...], kg_ref[...], dkn_ref[...])
        dqkv_ref[:, 0:d] = dq.astype(BF16)
        dqkv_ref[:, d:2 * d] = dk.astype(BF16)
        dqkv_ref[:, 2 * d:3 * d] = dv_ref[...].astype(BF16)

        @pl.when(pl.program_id(0) == 0)
        def _():
            dqg_ref[...] = dqg
            dkg_ref[...] = dkg

        @pl.when(pl.program_id(0) > 0)
        def _():
            dqg_ref[...] += dqg
            dkg_ref[...] += dkg

    blk = lambda col: pl.BlockSpec((rows, d), lambda i: (i, col))
    vec = pl.BlockSpec((1, d), lambda i: (0, 0))
    return pl.pallas_call(
        body,
        out_shape=[jax.ShapeDtypeStruct((p, 3 * d), BF16), jax.ShapeDtypeStruct((1, d), F32),
                   jax.ShapeDtypeStruct((1, d), F32)],
        grid=(p // rows,),
        in_specs=[blk(0), blk(1), vec, vec, pl.BlockSpec((d, d), lambda i: (0, 0)), blk(0), blk(0), blk(0)],
        out_specs=[pl.BlockSpec((rows, 3 * d), lambda i: (i, 0)), vec, vec],
        compiler_params=_params("arbitrary"),
        name=name,
    )(qkv, qkv, qg, kg, _group_matrix(), dqn, dkn, dv)


SB_PAIR = 2 * SB_HEAD_DIM
SB_GROUP = 4
SB_MASKED = -1e30


def _sb_consts():
    lane = lax.broadcasted_iota(jnp.int32, (CHUNK, SB_PAIR), 1)
    r = lax.broadcasted_iota(jnp.int32, (CHUNK, CHUNK), 0)
    c = lax.broadcasted_iota(jnp.int32, (CHUNK, CHUNK), 1)
    lo = (lane < SB_HEAD_DIM).astype(F32).astype(BF16)
    ones = jnp.ones((CHUNK, CHUNK), BF16)
    later = jnp.concatenate([(r > c).astype(BF16), ones], axis=1)
    earlier = jnp.concatenate([(r < c).astype(BF16), ones], axis=1)
    not_before = (c >= r).astype(F32) * SB_MASKED
    padding = (c < PAD_FRONT).astype(F32) * SB_MASKED
    return (lo, 1.0 - lo), c, later, earlier, not_before, padding


def _sb_halves(t, head_lanes):
    return t * head_lanes[0], t * head_lanes[1]


def _sb_logits(q2, kg, bias, n_b):
    z = _dot_nt(q2, kg)
    tiles = []
    for b in range(n_b):
        for h in range(2):
            zt = z[h * CHUNK:(h + 1) * CHUNK, b * CHUNK:(b + 1) * CHUNK]
            if bias is not None:
                zt = zt + bias
            ls_pos = jnp.minimum(zt, 0.0) - jnp.log(1.0 + jnp.exp(-jnp.abs(zt)))
            tiles.append((ls_pos, ls_pos - zt))
    return tiles


def _sb_block_sums(tiles, m):
    n = len(tiles)
    st = jnp.concatenate(tiles, axis=0)
    hi = st.astype(BF16)
    lo = (st - hi.astype(F32)).astype(BF16)
    res = _dot(jnp.concatenate([hi, lo], axis=0), m)
    tot = res[0:n * CHUNK] + res[n * CHUNK:2 * n * CHUNK]
    return [(tot[i * CHUNK:(i + 1) * CHUNK, 0:CHUNK], tot[i * CHUNK:(i + 1) * CHUNK, CHUNK:2 * CHUNK]) for i in range(n)]


def _sb_stack(tg, head_lanes, n_b):
    return jnp.concatenate([tg[b * CHUNK:(b + 1) * CHUNK] * head_lanes[h] for b in range(n_b) for h in range(2)], axis=0)


def _sb_fwd(qn, kn, vb, *, name):
    p = qn.shape[0]
    n_blocks = p // CHUNK
    n_pairs = SB_HEADS // 2
    scale = SB_HEAD_DIM ** -0.5

    def body(q_ref, k_ref, v_ref, o_ref, car_ref):
        head_lanes, c, later, _, not_before, padding = _sb_consts()

        def q_block(qi, _):
            rows = pl.ds(pl.multiple_of(qi * CHUNK, CHUNK), CHUNK)
            qh = _sb_halves(q_ref[rows, :], head_lanes)
            q2 = jnp.concatenate(qh, axis=0) * scale

            def blocks(kb0, n_b, bias, carry):
                acc, run0, run1, sav0, sav1 = carry
                krows = pl.ds(pl.multiple_of(kb0 * CHUNK, CHUNK), n_b * CHUNK)
                tiles = _sb_logits(q2, k_ref[krows, :], bias, n_b)
                sums = _sb_block_sums([log_keep for _, log_keep in tiles], later)
                runs, savs = [run0, run1], [sav0, sav1]
                ws = [None] * (2 * n_b)
                for b in reversed(range(n_b)):
                    this_col = (c == kb0 + b).astype(F32)
                    for h in range(2):
                        after, row_sum = sums[2 * b + h]
                        ws[2 * b + h] = jnp.exp(tiles[2 * b + h][0] + after + runs[h]).astype(BF16)
                        savs[h] = savs[h] + this_col * runs[h]
                        runs[h] = runs[h] + row_sum
                acc = acc + _dot(jnp.concatenate(ws, axis=1), _sb_stack(v_ref[krows, :], head_lanes, n_b))
                return acc, runs[0], runs[1], savs[0], savs[1]

            zt = qh[0].astype(F32) * 0.0
            diag_bias = not_before + padding * (qi == 0).astype(F32)
            carry = blocks(qi, 1, diag_bias, (zt, zt, zt, zt, zt))
            inner = jnp.maximum(qi - 1, 0)
            n_groups = lax.div(inner, SB_GROUP)
            carry = lax.fori_loop(0, n_groups, lambda it, cr: blocks(qi - SB_GROUP * (it + 1), SB_GROUP, None, cr), carry)
            left = inner - n_groups * SB_GROUP
            carry = lax.fori_loop(0, left, lambda s, cr: blocks(left - s, 1, None, cr), carry)
            carry = lax.fori_loop(0, jnp.minimum(qi, 1), lambda s, cr: blocks(0, 1, padding, cr), carry)
            acc, _, _, sav0, sav1 = carry
            o_ref[rows, :] = acc.astype(o_ref.dtype)
            car_ref[rows, 0:CHUNK] = sav0
            car_ref[rows, CHUNK:2 * CHUNK] = sav1
            return 0

        lax.fori_loop(0, n_blocks, q_block, 0)

    col = pl.BlockSpec((p, SB_PAIR), lambda g: (0, g))
    return pl.pallas_call(
        body,
        out_shape=[jax.ShapeDtypeStruct((p, D_MODEL), BF16), jax.ShapeDtypeStruct((p, n_pairs * 2 * CHUNK), F32)],
        grid=(n_pairs,),
        in_specs=[col, col, col],
        out_specs=[col, pl.BlockSpec((p, 2 * CHUNK), lambda g: (0, g))],
        compiler_params=_params("parallel"),
        name=name,
    )(qn, kn, vb)


def _sb_bwd(qn, kn, vb, carries, do, *, name):
    p = qn.shape[0]
    n_blocks = p // CHUNK
    n_pairs = SB_HEADS // 2
    scale = SB_HEAD_DIM ** -0.5

    def body(q_ref, k_ref, v_ref, car_ref, do_ref, dq_ref, dk_ref, dv_ref):
        head_lanes, c, later, earlier, not_before, padding = _sb_consts()
        dk_ref[...] = jnp.zeros_like(dk_ref)
        dv_ref[...] = jnp.zeros_like(dv_ref)

        def q_block(qi, _):
            rows = pl.ds(pl.multiple_of(qi * CHUNK, CHUNK), CHUNK)
            qh = _sb_halves(q_ref[rows, :], head_lanes)
            q2 = jnp.concatenate(qh, axis=0) * scale
            do2 = jnp.concatenate(_sb_halves(do_ref[rows, :].astype(BF16), head_lanes), axis=0)
            sav = (car_ref[rows, 0:CHUNK], car_ref[rows, CHUNK:2 * CHUNK])

            def blocks(kb0, n_b, bias, carry):
                dq_acc, pre0, pre1 = carry
                krows = pl.ds(pl.multiple_of(kb0 * CHUNK, CHUNK), n_b * CHUNK)
                kg, vg = k_ref[krows, :], v_ref[krows, :]
                tiles = _sb_logits(q2, kg, bias, n_b)
                afters = _sb_block_sums([log_keep for _, log_keep in tiles], later)
                dw = _dot_nt(do2, vg)
                ws, es = [], []
                for b in range(n_b):
                    this_col = (c == kb0 + b).astype(F32)
                    for h in range(2):
                        run = jnp.sum(this_col * sav[h], axis=-1, keepdims=True)
                        w = jnp.exp(tiles[2 * b + h][0] + afters[2 * b + h][0] + run)
                        ws.append(w)
                        es.append(w * dw[h * CHUNK:(h + 1) * CHUNK, b * CHUNK:(b + 1) * CHUNK])
                befores = _sb_block_sums(es, earlier)
                pres = [pre0, pre1]
                dzs = []
                for b in range(n_b):
                    for h in range(2):
                        before, row_sum = befores[2 * b + h]
                        sig = jnp.exp(tiles[2 * b + h][0])
                        e = es[2 * b + h]
                        dzs.append((e - (e + before + pres[h]) * sig).astype(BF16))
                        pres[h] = pres[h] + row_sum
                by_head = lambda ts: jnp.concatenate(
                    [jnp.concatenate([ts[2 * b + h] for b in range(n_b)], axis=1) for h in range(2)], axis=0)
                dq_acc = dq_acc + _dot(jnp.concatenate(dzs, axis=1), _sb_stack(kg, head_lanes, n_b))
                dv_ref[krows, :] += _dot_tn(by_head([w.astype(BF16) for w in ws]), do2)
                dk_ref[krows, :] += _dot_tn(by_head(dzs), q2)
                return dq_acc, pres[0], pres[1]

            zt = qh[0].astype(F32) * 0.0
            carry = lax.fori_loop(0, jnp.minimum(qi, 1), lambda s, cr: blocks(0, 1, padding, cr), (zt, zt, zt))
            inner = jnp.maximum(qi - 1, 0)
            n_groups = lax.div(inner, SB_GROUP)
            carry = lax.fori_loop(0, n_groups, lambda it, cr: blocks(1 + SB_GROUP * it, SB_GROUP, None, cr), carry)
            carry = lax.fori_loop(1 + n_groups * SB_GROUP, qi, lambda kj, cr: blocks(kj, 1, None, cr), carry)
            diag_bias = not_before + padding * (qi == 0).astype(F32)
            dq_acc, _, _ = blocks(qi, 1, diag_bias, carry)
            dq_ref[rows, :] = dq_acc * scale
            return 0

        lax.fori_loop(0, n_blocks, q_block, 0)

    col = pl.BlockSpec((p, SB_PAIR), lambda g: (0, g))
    return pl.pallas_call(
        body,
        out_shape=[jax.ShapeDtypeStruct((p, D_MODEL), F32)] * 3,
        grid=(n_pairs,),
        in_specs=[col, col, col, pl.BlockSpec((p, 2 * CHUNK), lambda g: (0, g)), col],
        out_specs=[col, col, col],
        compiler_params=_params("parallel"),
        name=name,
    )(qn, kn, vb, carries, do)


def _loss_head(h, target, *, name):
    p, d = h.shape
    n_blocks = p // CHUNK

    def body(h_ref, t_ref, sq_ref, dh_ref):
        i = pl.program_id(0)

        @pl.when(i == 0)
        def _():
            sq_ref[...] = jnp.zeros_like(sq_ref)
            dh_ref[...] = jnp.zeros_like(dh_ref)

        @pl.when(i > 0)
        def _():
            err = h_ref[...] - t_ref[...]
            sq_ref[...] += jnp.sum(err * err)
            dh_ref[...] = err * (1.0 / d)

    return pl.pallas_call(
        body,
        out_shape=[jax.ShapeDtypeStruct((8, 128), F32), jax.ShapeDtypeStruct((p, d), F32)],
        grid=(n_blocks,),
        in_specs=[pl.BlockSpec((CHUNK, d), lambda i: (i, 0)),
                  pl.BlockSpec((CHUNK, d), lambda i: (jnp.maximum(i - 1, 0), 0))],
        out_specs=[pl.BlockSpec((8, 128), lambda i: (0, 0)), pl.BlockSpec((CHUNK, d), lambda i: (i, 0))],
        compiler_params=_params("arbitrary"),
        name=name,
    )(h, target)


def _local_step(x, target, meta, norm_mix_g, norm_mlp_g, w_in, gn_g, conv_w, conv_b, ln_g, ln_b, w_out,
                w_qkv, qn_g, kn_g, w_o, w1, w2):
    seq = x.shape[0]
    p = PAD_FRONT + N_META + seq
    d = D_MODEL
    tables = _retention_tables(p)
    row = lambda v: v.reshape(1, -1)
    h0 = jnp.concatenate([jnp.zeros((PAD_FRONT, d), F32), meta, x], axis=0)

    hn0 = _rmsnorm_fwd(h0, row(norm_mix_g[0]), name="l0_mix_norm")
    proj = _matmul(hn0, w_in, mode="nn", out_dtypes=(F32,), name="l0_proj")
    og, opre, sprev = _retention_fwd(proj, gn_g, tables, name="l0_retention")
    cb, y_conv = _conv_fwd(proj, conv_w, row(conv_b), row(ln_g), row(ln_b), name="l0_conv")
    cat = jnp.concatenate([og, cb], axis=1)
    h1 = _matmul(cat, w_out, mode="nn", out_dtypes=(F32,), epilogue=_add_epilogue, extras=(h0,), name="l0_mix_out")
    h2, mlp0 = _mlp_fwd(h1, row(norm_mlp_g[0]), w1[0], w2[0], name="l0_mlp")

    hn1 = _rmsnorm_fwd(h2, row(norm_mix_g[1]), name="l1_mix_norm")
    qkv = _matmul(hn1, w_qkv, mode="nn", out_dtypes=(F32,), name="l1_qkv")
    qg_t, kg_t = jnp.tile(row(qn_g), (1, SB_HEADS)), jnp.tile(row(kn_g), (1, SB_HEADS))
    qn, kn, vb = _qknorm_fwd(qkv, qg_t, kg_t, name="l1_qknorm")
    o_sb, carries = _sb_fwd(qn, kn, vb, name="l1_stickbreak")
    h3 = _matmul(o_sb, w_o, mode="nn", out_dtypes=(F32,), epilogue=_add_epilogue, extras=(h2,), name="l1_mix_out")
    h4, mlp1 = _mlp_fwd(h3, row(norm_mlp_g[1]), w1[1], w2[1], name="l1_mlp")

    sq, dh4 = _loss_head(h4, target, name="loss_head")

    dh3, dg_mlp1, dw1_1, dw2_1 = _mlp_bwd(h3, row(norm_mlp_g[1]), w1[1], w2[1], mlp1, dh4, name="l1_mlp_bwd")
    do_sb = _matmul(dh3, w_o, mode="nt", out_dtypes=(F32,), name="l1_do")
    dw_o = _matmul(o_sb, dh3, mode="tn", out_dtypes=(F32,), name="l1_dwo")
    dqn, dkn, dv = _sb_bwd(qn, kn, vb, carries, do_sb, name="l1_stickbreak_bwd")
    dqkv, dqg_t, dkg_t = _qknorm_bwd(qkv, qg_t, kg_t, dqn, dkn, dv, name="l1_qknorm_bwd")
    dw_qkv = _matmul(hn1, dqkv, mode="tn", out_dtypes=(F32,), name="l1_dwqkv")
    dhn1 = _matmul(dqkv, w_qkv, mode="nt", out_dtypes=(F32,), name="l1_dhn")
    dh2, dg_mix1 = _rmsnorm_bwd(h2, row(norm_mix_g[1]), dhn1, dh3, name="l1_mix_dnorm")

    dh1, dg_mlp0, dw1_0, dw2_0 = _mlp_bwd(h1, row(norm_mlp_g[0]), w1[0], w2[0], mlp0, dh2, name="l0_mlp_bwd")
    dcat = _matmul(dh1, w_out, mode="nt", out_dtypes=(F32,), name="l0_dcat")
    dw_out = _matmul(cat, dh1, mode="tn", out_dtypes=(F32,), name="l0_dwout")
    dq, dk, dvr, dgate_r, dgn = _retention_bwd(proj, gn_g, tables, opre, sprev, dcat, name="l0_retention_bwd")
    da, dgate_c, dconv_w, dconv_b, dln_g, dln_b = _conv_bwd(proj, conv_w, row(ln_g), row(ln_b), y_conv, dcat,
                                                            name="l0_conv_bwd")
    dproj = jnp.concatenate([dq, dk, dvr, dgate_r, da, dgate_c], axis=1)
    dw_in = _matmul(hn0, dproj, mode="tn", out_dtypes=(F32,), name="l0_dwin")
    dhn0 = _matmul(dproj, w_in, mode="nt", out_dtypes=(F32,), name="l0_dhn")
    dh0, dg_mix0 = _rmsnorm_bwd(h0, row(norm_mix_g[0]), dhn0, dh1, name="l0_mix_dnorm")

    fold = lambda t: t.reshape(SB_HEADS, SB_HEAD_DIM).sum(axis=0)
    grads = dict(
        x=dh0[PAD_FRONT + N_META:],
        meta=dh0[PAD_FRONT:PAD_FRONT + N_META],
        norm_mix_g=jnp.concatenate([dg_mix0, dg_mix1], axis=0),
        norm_mlp_g=jnp.concatenate([dg_mlp0, dg_mlp1], axis=0),
        even_w_in=dw_in,
        even_ret_gn_g=dgn.reshape(RET_HEADS, RET_V_DIM),
        even_conv_w=dconv_w,
        even_conv_b=dconv_b,
        even_conv_ln_g=dln_g,
        even_conv_ln_b=dln_b,
        even_w_out=dw_out,
        odd_w_qkv=dw_qkv,
        odd_q_norm_g=fold(dqg_t)[None],
        odd_k_norm_g=fold(dkg_t)[None],
        odd_w_o=dw_o,
        mlp_w1=(dw1_0, dw1_1),
        mlp_w2=(dw2_0, dw2_1),
    )
    return sq[0, 0], grads


def _position():
    x, y, c = lax.axis_index("x"), lax.axis_index("y"), lax.axis_index("c")
    other_chips = [(1 - x, y), (x, 1 - y), (1 - x, 1 - y)]
    return x, y, c, other_chips


def _shard_of(ref, kind, s, n):
    rows, cols = ref.shape
    if kind == "col":
        return ref.at[:, pl.ds(s * (cols // n), cols // n)]
    return ref.at[pl.ds(s * (rows // n), rows // n), :]


def _half_of(ref, kind, c):
    rows, cols = ref.shape
    if kind == "col":
        return ref.at[pl.ds(c * (rows // 2), rows // 2), :]
    return ref.at[:, pl.ds(c * (cols // 2), cols // 2)]


def _remote(src, dst, send_sems, recv_sems, idx, device):
    return pltpu.make_async_remote_copy(src_ref=src, dst_ref=dst, send_sem=send_sems.at[idx], recv_sem=recv_sems.at[idx],
                                        device_id=device, device_id_type=MESH)


def _cast_into_whole(w, kind, s_arr, *, name):
    rows, cols = w.shape
    tr = _pick(rows, (256, 128))
    nb = rows // tr
    if kind == "col":
        whole, o_spec = (rows, cols * N_CHIPS), pl.BlockSpec((tr, cols), lambda i, s_ref: (i, s_ref[0]))
    else:
        whole, o_spec = (rows * N_CHIPS, cols), pl.BlockSpec((tr, cols), lambda i, s_ref: (s_ref[0] * nb + i, 0))

    def body(s_ref, w_ref, o_ref):
        o_ref[...] = w_ref[...].astype(BF16)

    return pl.pallas_call(
        body,
        out_shape=jax.ShapeDtypeStruct(whole, BF16),
        grid_spec=pltpu.PrefetchScalarGridSpec(num_scalar_prefetch=1, grid=(nb,),
                                               in_specs=[pl.BlockSpec((tr, cols), lambda i, s_ref: (i, 0))],
                                               out_specs=o_spec),
        compiler_params=_params("parallel"),
        name=name,
    )(s_arr, w)


def _allgather_weights(wholes, kinds):
    n = len(wholes)

    def body(*refs):
        ins, outs = refs[:n], refs[n:2 * n]
        send_sems, recv_sems = refs[2 * n:]
        x, y, c, chips = _position()
        me_chip = 2 * x + y
        sibling = (x, y, 1 - c)
        sends = []
        for t in range(n):
            for k, (cx, cy) in enumerate(chips):
                src = _half_of(_shard_of(ins[t], kinds[t], me_chip, N_CHIPS), kinds[t], c)
                dst = _half_of(_shard_of(outs[t], kinds[t], me_chip, N_CHIPS), kinds[t], c)
                sends.append(_remote(src, dst, send_sems, recv_sems, 6 * t + k, (cx, cy, c)))
        for cp in sends:
            cp.start()
        passed = []
        for t in range(n):
            for k, (cx, cy) in enumerate(chips):
                landed = _half_of(_shard_of(outs[t], kinds[t], 2 * cx + cy, N_CHIPS), kinds[t], c)
                _remote(landed, landed, send_sems, recv_sems, 6 * t + k, (cx, cy, c)).wait_recv()
                fwd = _remote(landed, landed, send_sems, recv_sems, 6 * t + 3 + k, sibling)
                fwd.start()
                passed.append(fwd)
        for t in range(n):
            for k, (cx, cy) in enumerate(chips):
                theirs = _half_of(_shard_of(outs[t], kinds[t], 2 * cx + cy, N_CHIPS), kinds[t], 1 - c)
                _remote(theirs, theirs, send_sems, recv_sems, 6 * t + 3 + k, sibling).wait_recv()
        for cp in sends + passed:
            cp.wait_send()

    return pl.pallas_call(
        body,
        out_shape=[jax.ShapeDtypeStruct(w.shape, BF16) for w in wholes],
        in_specs=[ANY] * n,
        out_specs=[ANY] * n,
        input_output_aliases={t: t for t in range(n)},
        scratch_shapes=[pltpu.SemaphoreType.DMA((6 * n,)), pltpu.SemaphoreType.DMA((6 * n,))],
        name="allgather_weights",
    )(*wholes)


def _allgather8(block, *, name):
    rows, cols = block.shape

    def body(in_ref, out_ref, send_sems, recv_sems, local_sem):
        x, y, c, _ = _position()
        me = 4 * x + 2 * y + c
        mine = pltpu.make_async_copy(in_ref, out_ref.at[me], local_sem)
        mine.start()
        peers = []
        for flip in range(1, N_DEV):
            fx, fy, fc = (flip >> 2) & 1, (flip >> 1) & 1, flip & 1
            peers.append(((1 - x if fx else x), (1 - y if fy else y), (1 - c if fc else c)))
        sends = [_remote(in_ref, out_ref.at[me], send_sems, recv_sems, j, peer) for j, peer in enumerate(peers)]
        for cp in sends:
            cp.start()
        for j, (px, py, pc) in enumerate(peers):
            slot = out_ref.at[4 * px + 2 * py + pc]
            _remote(slot, slot, send_sems, recv_sems, j, (px, py, pc)).wait_recv()
        for cp in sends:
            cp.wait_send()
        mine.wait()

    vmem = pl.BlockSpec(memory_space=pltpu.VMEM)
    return pl.pallas_call(
        body,
        out_shape=jax.ShapeDtypeStruct((N_DEV, rows, cols), F32),
        in_specs=[vmem],
        out_specs=vmem,
        scratch_shapes=[pltpu.SemaphoreType.DMA((N_DEV - 1,)), pltpu.SemaphoreType.DMA((N_DEV - 1,)),
                        pltpu.SemaphoreType.DMA],
        name=name,
    )(block)


def _sum8(stack, *, name):
    _, rows, cols = stack.shape

    def body(s_ref, o_ref):
        acc = s_ref[0]
        for i in range(1, N_DEV):
            acc = acc + s_ref[i]
        o_ref[...] = acc

    return pl.pallas_call(body, out_shape=jax.ShapeDtypeStruct((rows, cols), F32), name=name)(stack)


def _swap_halves_in(grads, kinds):
    n = len(grads)

    def body(*refs):
        ins, outs = refs[:n], refs[n:2 * n]
        send_sems, recv_sems = refs[2 * n:]
        x, y, c, _ = _position()
        sibling = (x, y, 1 - c)
        sends = [_remote(_half_of(ins[t], kinds[t], 1 - c), outs[t], send_sems, recv_sems, t, sibling) for t in range(n)]
        for cp in sends:
            cp.start()
        for t in range(n):
            _remote(_half_of(ins[t], kinds[t], c), outs[t], send_sems, recv_sems, t, sibling).wait_recv()
        for cp in sends:
            cp.wait_send()

    def half(g, kind):
        rows, cols = g.shape
        return (rows // 2, cols) if kind == "col" else (rows, cols // 2)

    return pl.pallas_call(
        body,
        out_shape=[jax.ShapeDtypeStruct(half(g, k), F32) for g, k in zip(grads, kinds)],
        in_specs=[ANY] * n,
        out_specs=[ANY] * n,
        scratch_shapes=[pltpu.SemaphoreType.DMA((n,)), pltpu.SemaphoreType.DMA((n,))],
        name="reduce_core_pair",
    )(*grads)


def _half_add(grad, theirs, kind, c_arr, *, name):
    rows, cols = theirs.shape
    tr = _pick(rows, (256, 128))
    nb = rows // tr
    if kind == "col":
        g_spec = pl.BlockSpec((tr, cols), lambda i, c_ref: (c_ref[0] * nb + i, 0))
    else:
        g_spec = pl.BlockSpec((tr, cols), lambda i, c_ref: (i, c_ref[0]))
    t_spec = pl.BlockSpec((tr, cols), lambda i, c_ref: (i, 0))

    def body(c_ref, g_ref, t_ref, o32_ref, o16_ref):
        tot = g_ref[...] + t_ref[...]
        o32_ref[...] = tot
        o16_ref[...] = tot.astype(BF16)

    return pl.pallas_call(
        body,
        out_shape=[jax.ShapeDtypeStruct((rows, cols), F32), jax.ShapeDtypeStruct((rows, cols), BF16)],
        grid_spec=pltpu.PrefetchScalarGridSpec(num_scalar_prefetch=1, grid=(nb,), in_specs=[g_spec, t_spec],
                                               out_specs=[t_spec, t_spec]),
        compiler_params=_params("parallel"),
        name=name,
    )(c_arr, grad, theirs)


def _exchange_chips(parts, kinds):
    n = len(parts)

    def body(*refs):
        ins, outs = refs[:n], refs[n:2 * n]
        send_sems, recv_sems = refs[2 * n:]
        x, y, c, chips = _position()
        sends = []
        for t in range(n):
            for k, (cx, cy) in enumerate(chips):
                src = _shard_of(ins[t], kinds[t], 2 * cx + cy, N_CHIPS)
                sends.append(_remote(src, outs[t].at[k], send_sems, recv_sems, 3 * t + k, (cx, cy, c)))
        for cp in sends:
            cp.start()
        for t in range(n):
            for k, (cx, cy) in enumerate(chips):
                src = _shard_of(ins[t], kinds[t], 2 * cx + cy, N_CHIPS)
                _remote(src, outs[t].at[k], send_sems, recv_sems, 3 * t + k, (cx, cy, c)).wait_recv()
        for cp in sends:
            cp.wait_send()

    def piece(p, kind):
        rows, cols = p.shape
        return (3, rows, cols // N_CHIPS) if kind == "col" else (3, rows // N_CHIPS, cols)

    return pl.pallas_call(
        body,
        out_shape=[jax.ShapeDtypeStruct(piece(p, k), BF16) for p, k in zip(parts, kinds)],
        in_specs=[ANY] * n,
        out_specs=[ANY] * n,
        scratch_shapes=[pltpu.SemaphoreType.DMA((3 * n,)), pltpu.SemaphoreType.DMA((3 * n,))],
        name="reduce_chips",
    )(*parts)


def _shard_sum(part32, recv, kind, sc_arr, *, name):
    _, rows, cols = recv.shape
    tr = _pick(rows, (256, 128))
    nb = rows // tr
    if kind == "col":
        whole = (2 * rows, cols)
        p_spec = pl.BlockSpec((tr, cols), lambda i, sc: (i, sc[0]))
        o_spec = pl.BlockSpec((tr, cols), lambda i, sc: (sc[1] * nb + i, 0))
    else:
        whole = (rows, 2 * cols)
        p_spec = pl.BlockSpec((tr, cols), lambda i, sc: (sc[0] * nb + i, 0))
        o_spec = pl.BlockSpec((tr, cols), lambda i, sc: (i, sc[1]))
    r_spec = pl.BlockSpec((3, tr, cols), lambda i, sc: (0, i, 0))

    def body(sc_ref, p_ref, r_ref, o_ref):
        acc = p_ref[...]
        for k in range(3):
            acc = acc + r_ref[k].astype(F32)
        o_ref[...] = acc

    return pl.pallas_call(
        body,
        out_shape=jax.ShapeDtypeStruct(whole, F32),
        grid_spec=pltpu.PrefetchScalarGridSpec(num_scalar_prefetch=1, grid=(nb,), in_specs=[p_spec, r_spec],
                                               out_specs=o_spec),
        compiler_params=_params("parallel"),
        name=name,
    )(sc_arr, part32, recv)


def _swap_halves_out(shards, kinds):
    n = len(shards)

    def body(*refs):
        ins, outs = refs[:n], refs[n:2 * n]
        send_sems, recv_sems = refs[2 * n:]
        x, y, c, _ = _position()
        sibling = (x, y, 1 - c)
        sends = [_remote(_half_of(ins[t], kinds[t], c), _half_of(outs[t], kinds[t], c), send_sems, recv_sems, t, sibling)
                 for t in range(n)]
        for cp in sends:
            cp.start()
        for t in range(n):
            theirs = _half_of(outs[t], kinds[t], 1 - c)
            _remote(theirs, theirs, send_sems, recv_sems, t, sibling).wait_recv()
        for cp in sends:
            cp.wait_send()

    return pl.pallas_call(
        body,
        out_shape=[jax.ShapeDtypeStruct(s.shape, F32) for s in shards],
        in_specs=[ANY] * n,
        out_specs=[ANY] * n,
        input_output_aliases={t: t for t in range(n)},
        scratch_shapes=[pltpu.SemaphoreType.DMA((n,)), pltpu.SemaphoreType.DMA((n,))],
        name="gather_core_pair",
    )(*shards)


def _adamw(w, g, m, v, *, name):
    rows, cols = w.shape
    tr = _pick(rows, (256, 128)) if rows * cols > 64 * 1024 else rows

    def body(w_ref, g_ref, m_ref, v_ref, d_ref, nm_ref, nv_ref):
        gv = g_ref[...]
        nm = ADAM_B1 * m_ref[...] + (1.0 - ADAM_B1) * gv
        nv = ADAM_B2 * v_ref[...] + (1.0 - ADAM_B2) * jnp.square(gv)
        m_hat = nm / (1.0 - ADAM_B1 ** ADAM_STEP)
        v_hat = nv / (1.0 - ADAM_B2 ** ADAM_STEP)
        d_ref[...] = -ADAM_LR * (m_hat / (jnp.sqrt(v_hat) + ADAM_EPS) + ADAM_WD * w_ref[...])
        nm_ref[...] = nm
        nv_ref[...] = nv

    spec = pl.BlockSpec((tr, cols), lambda i: (i, 0))
    return pl.pallas_call(
        body,
        out_shape=[jax.ShapeDtypeStruct((rows, cols), F32)] * 3,
        grid=(rows // tr,),
        in_specs=[spec] * 4,
        out_specs=[spec] * 3,
        compiler_params=_params("parallel"),
        name=name,
    )(w, g, m, v)


BIG = ("even_w_in", "odd_w_qkv", "mlp_w1_0", "mlp_w1_1", "even_w_out", "odd_w_o", "mlp_w2_0", "mlp_w2_1")
BIG_KIND = ("col", "col", "col", "col", "row", "row", "row", "row")
SUBLANES = 8


def _pack_rows(parts, width):
    padded, offsets, r0 = [], [], 0
    for t in parts:
        rows = -(-t.shape[0] // SUBLANES) * SUBLANES
        padded.append(jnp.pad(t, ((0, rows - t.shape[0]), (0, width - t.shape[1]))))
        offsets.append(r0)
        r0 += rows
    return jnp.concatenate(padded, axis=0), offsets


def kernel(x, meta, norm_mix_g, norm_mlp_g, even_w_in, even_ret_gn_g, even_conv_w, even_conv_b, even_conv_ln_g, even_conv_ln_b, even_w_out, odd_w_qkv, odd_q_norm_g, odd_k_norm_g, odd_w_o, mlp_w1, mlp_w2, loss_target, m_meta, m_norm_mix_g, m_norm_mlp_g, m_even_w_in, m_even_ret_gn_g, m_even_conv_w, m_even_conv_b, m_even_conv_ln_g, m_even_conv_ln_b, m_even_w_out, m_odd_w_qkv, m_odd_q_norm_g, m_odd_k_norm_g, m_odd_w_o, m_mlp_w1, m_mlp_w2, v_meta, v_norm_mix_g, v_norm_mlp_g, v_even_w_in, v_even_ret_gn_g, v_even_conv_w, v_even_conv_b, v_even_conv_ln_g, v_even_conv_ln_b, v_even_w_out, v_odd_w_qkv, v_odd_q_norm_g, v_odd_k_norm_g, v_odd_w_o, v_mlp_w1, v_mlp_w2):
    d = D_MODEL
    xi, yi, ci = lax.axis_index("x"), lax.axis_index("y"), lax.axis_index("c")
    chip = 2 * xi + yi
    c_arr = jnp.reshape(ci, (1,)).astype(jnp.int32)
    s_arr = jnp.reshape(chip, (1,)).astype(jnp.int32)

    def split_big(w_in, w_qkv, w1, w_out, w_o, w2):
        return dict(zip(BIG, (w_in[0], w_qkv[0], w1[0], w1[1], w_out[0], w_o[0], w2[0], w2[1])))

    w_big = split_big(even_w_in, odd_w_qkv, mlp_w1, even_w_out, odd_w_o, mlp_w2)
    m_big = split_big(m_even_w_in, m_odd_w_qkv, m_mlp_w1, m_even_w_out, m_odd_w_o, m_mlp_w2)
    v_big = split_big(v_even_w_in, v_odd_w_qkv, v_mlp_w1, v_even_w_out, v_odd_w_o, v_mlp_w2)

    placed = [_cast_into_whole(w_big[n], k, s_arr, name="cast_" + n) for n, k in zip(BIG, BIG_KIND)]
    full = dict(zip(BIG, _allgather_weights(placed, BIG_KIND)))

    packed, (r_meta, r_conv, r_gn) = _pack_rows([meta, even_conv_w[0], even_ret_gn_g[0]], d // N_CHIPS)
    gathered = _allgather8(packed, name="allgather_small_params")[0::2]
    across = lambda r0, rows, width: jnp.concatenate([gathered[s, r0:r0 + rows, 0:width] for s in range(N_CHIPS)], axis=1)
    meta_full = across(r_meta, N_META, d // N_CHIPS)
    conv_w_full = across(r_conv, CONV_WIDTH, d // N_CHIPS)
    gn_full = across(r_gn, RET_HEADS, RET_V_DIM // N_CHIPS)

    sq, g = _local_step(
        x[0], loss_target[0], meta_full, norm_mix_g, norm_mlp_g, full["even_w_in"], gn_full, conv_w_full,
        even_conv_b[0], even_conv_ln_g[0], even_conv_ln_b[0], full["even_w_out"], full["odd_w_qkv"],
        odd_q_norm_g[0], odd_k_norm_g[0], full["odd_w_o"], (full["mlp_w1_0"], full["mlp_w1_1"]),
        (full["mlp_w2_0"], full["mlp_w2_1"]))
    loss = lax.psum(0.5 * sq / d, ("x", "y", "c"))

    small_names = ("norm_mix_g", "norm_mlp_g", "even_conv_b", "even_conv_ln_g", "even_conv_ln_b", "odd_q_norm_g",
                   "odd_k_norm_g", "meta", "even_conv_w", "even_ret_gn_g")
    pack, offsets = _pack_rows([g[n] for n in small_names], d)
    summed = _sum8(_allgather8(pack, name="allgather_small_grads"), name="sum_small_grads")
    small = {n: summed[r0:r0 + g[n].shape[0], 0:g[n].shape[1]] for n, r0 in zip(small_names, offsets)}
    for n in ("meta", "even_conv_w", "even_ret_gn_g"):
        width = small[n].shape[1] // N_CHIPS
        small[n] = lax.dynamic_slice_in_dim(small[n], chip * width, width, axis=1)

    g_big = [g["even_w_in"], g["odd_w_qkv"], g["mlp_w1"][0], g["mlp_w1"][1], g["even_w_out"], g["odd_w_o"],
             g["mlp_w2"][0], g["mlp_w2"][1]]
    theirs = _swap_halves_in(g_big, BIG_KIND)
    sums = [_half_add(gb, th, k, c_arr, name="pair_sum_" + n) for gb, th, k, n in zip(g_big, theirs, BIG_KIND, BIG)]
    recv = _exchange_chips([s16 for _, s16 in sums], BIG_KIND)
    sc_arr = jnp.concatenate([s_arr, c_arr])
    halves = [_shard_sum(s32, r, k, sc_arr, name="chip_sum_" + n) for (s32, _), r, k, n in zip(sums, recv, BIG_KIND, BIG)]
    grad_big = dict(zip(BIG, _swap_halves_out(halves, BIG_KIND)))

    upd = {n: _adamw(w_big[n], grad_big[n], m_big[n], v_big[n], name="adamw_" + n) for n in BIG}

    def join(name, idx, lead):
        if name in ("mlp_w1", "mlp_w2"):
            return jnp.stack([upd[name + "_0"][idx], upd[name + "_1"][idx]]) if idx >= 0 else jnp.stack(
                [grad_big[name + "_0"], grad_big[name + "_1"]])
        t = upd[name][idx] if idx >= 0 else grad_big[name]
        return t[None] if lead else t

    small_w = dict(meta=meta, norm_mix_g=norm_mix_g, norm_mlp_g=norm_mlp_g, even_ret_gn_g=even_ret_gn_g[0],
                   even_conv_w=even_conv_w[0], even_conv_b=even_conv_b, even_conv_ln_g=even_conv_ln_g,
                   even_conv_ln_b=even_conv_ln_b, odd_q_norm_g=odd_q_norm_g, odd_k_norm_g=odd_k_norm_g)
    small_m = dict(meta=m_meta, norm_mix_g=m_norm_mix_g, norm_mlp_g=m_norm_mlp_g, even_ret_gn_g=m_even_ret_gn_g[0],
                   even_conv_w=m_even_conv_w[0], even_conv_b=m_even_conv_b, even_conv_ln_g=m_even_conv_ln_g,
                   even_conv_ln_b=m_even_conv_ln_b, odd_q_norm_g=m_odd_q_norm_g, odd_k_norm_g=m_odd_k_norm_g)
    small_v = dict(meta=v_meta, norm_mix_g=v_norm_mix_g, norm_mlp_g=v_norm_mlp_g, even_ret_gn_g=v_even_ret_gn_g[0],
                   even_conv_w=v_even_conv_w[0], even_conv_b=v_even_conv_b, even_conv_ln_g=v_even_conv_ln_g,
                   even_conv_ln_b=v_even_conv_ln_b, odd_q_norm_g=v_odd_q_norm_g, odd_k_norm_g=v_odd_k_norm_g)
    small_upd = {n: _adamw(small_w[n], small[n], small_m[n], small_v[n], name="adamw_" + n) for n in small_w}
    leading = ("even_ret_gn_g", "even_conv_w")

    order = ("meta", "norm_mix_g", "norm_mlp_g", "even_w_in", "even_ret_gn_g", "even_conv_w", "even_conv_b",
             "even_conv_ln_g", "even_conv_ln_b", "even_w_out", "odd_w_qkv", "odd_q_norm_g", "odd_k_norm_g", "odd_w_o",
             "mlp_w1", "mlp_w2")
    big_lead = ("even_w_in", "even_w_out", "odd_w_qkv", "odd_w_o")

    def leaf(name, idx):
        if name in small_w:
            t = small_upd[name][idx] if idx >= 0 else small[name]
            return t[None] if name in leading else t
        return join(name, idx, name in big_lead)

    outs = [loss, g["x"][None]]
    for idx in (-1, 0, 1, 2):
        outs += [leaf(n, idx) for n in order]
    return tuple(outs)
```

```python
import functools

import jax
import jax.numpy as jnp
from jax import lax
from jax.experimental import pallas as pl
from jax.experimental.pallas import tpu as pltpu

F32 = jnp.float32
BF16 = jnp.bfloat16

D_MODEL = 1024
N_META = 16
CHUNK = 128
PAD_FRONT = (-N_META) % CHUNK
RET_HEADS = 4
RET_QK_DIM = 128
RET_V_DIM = 256
RET_QK_W = RET_HEADS * RET_QK_DIM
RET_V_W = RET_HEADS * RET_V_DIM
CONV_WIDTH = 31
CONV_HALO = 32
RET_DECAY_OFFSET = 5.0
ROPE_BASE = 10000.0
SB_HEADS = 16
SB_HEAD_DIM = 64
D_FF = 4 * D_MODEL
EPS = 1e-6
ADAM_LR = 0.001
ADAM_B1 = 0.9
ADAM_B2 = 0.999
ADAM_EPS = 1e-08
ADAM_WD = 0.01
ADAM_STEP = 10

N_CHIPS = 4
N_DEV = 8
VMEM_LIMIT = 56 * 1024 * 1024
MESH = pl.DeviceIdType.MESH
ANY = pl.BlockSpec(memory_space=pl.ANY)


def _params(*sem):
    return pltpu.CompilerParams(dimension_semantics=sem, vmem_limit_bytes=VMEM_LIMIT)


def _pick(n, cands):
    for c in cands:
        if n % c == 0:
            return c
    return n


def _sigmoid(x):
    return 1.0 / (1.0 + jnp.exp(-x))


def _dot(a, b):
    return lax.dot_general(a, b, (((1,), (0,)), ((), ())), preferred_element_type=F32)


def _dot_nt(a, b):
    return lax.dot_general(a, b, (((1,), (1,)), ((), ())), preferred_element_type=F32)


def _dot_tn(a, b):
    return lax.dot_general(a, b, (((0,), (0,)), ((), ())), preferred_element_type=F32)


def _split_dot(x, m):
    hi = x.astype(BF16)
    lo = (x - hi.astype(F32)).astype(BF16)
    return _dot(hi, m) + _dot(lo, m)


def _matmul(a, b, *, mode, out_dtypes, epilogue=None, extras=(), name):
    if mode == "nn":
        (m, k), (k2, n) = a.shape, b.shape
    elif mode == "nt":
        (m, k), (n, k2) = a.shape, b.shape
    else:
        (k, m), (k2, n) = a.shape, b.shape
    assert k == k2, (a.shape, b.shape, mode)
    tm = _pick(m, (1056, 1024, 768, 512, 384, 256, 128, 96))
    tn = _pick(n, (1024, 768, 512, 256, 128))
    tk = _pick(k, (1056, 1024, 768, 512, 384, 256, 128, 96))
    nk = k // tk
    dot = {"nn": _dot, "nt": _dot_nt, "tn": _dot_tn}[mode]
    n_extra, n_out = len(extras), len(out_dtypes)
    if epilogue is None:
        epilogue = lambda acc: (acc,)

    def body(a_ref, b_ref, *rest):
        extra_refs = rest[:n_extra]
        out_refs = rest[n_extra:n_extra + n_out]
        part = dot(a_ref[...].astype(BF16), b_ref[...].astype(BF16))

        def finish(acc):
            res = epilogue(acc, *[r[...] for r in extra_refs])
            for o_ref, r in zip(out_refs, res):
                o_ref[...] = r.astype(o_ref.dtype)

        if nk == 1:
            finish(part)
        else:
            acc_ref = rest[-1]
            kk = pl.program_id(2)

            @pl.when(kk == 0)
            def _():
                acc_ref[...] = part

            @pl.when(kk > 0)
            def _():
                acc_ref[...] += part

            @pl.when(kk == nk - 1)
            def _():
                finish(acc_ref[...])

    if mode == "nn":
        a_spec = pl.BlockSpec((tm, tk), lambda i, j, kk: (i, kk))
        b_spec = pl.BlockSpec((tk, tn), lambda i, j, kk: (kk, j))
    elif mode == "nt":
        a_spec = pl.BlockSpec((tm, tk), lambda i, j, kk: (i, kk))
        b_spec = pl.BlockSpec((tn, tk), lambda i, j, kk: (j, kk))
    else:
        a_spec = pl.BlockSpec((tk, tm), lambda i, j, kk: (kk, i))
        b_spec = pl.BlockSpec((tk, tn), lambda i, j, kk: (kk, j))
    o_spec = pl.BlockSpec((tm, tn), lambda i, j, kk: (i, j))
    outs = pl.pallas_call(
        body,
        out_shape=[jax.ShapeDtypeStruct((m, n), dt) for dt in out_dtypes],
        grid=(m // tm, n // tn, nk),
        in_specs=[a_spec, b_spec] + [o_spec] * n_extra,
        out_specs=[o_spec] * n_out,
        scratch_shapes=[pltpu.VMEM((tm, tn), F32)] if nk > 1 else [],
        compiler_params=_params("parallel", "parallel", "arbitrary"),
        name=name,
    )(a, b, *extras)
    return outs[0] if n_out == 1 else outs


def _add_epilogue(acc, res):
    return (res + acc,)


def _rmsnorm_fwd(x, g, *, name):
    p, d = x.shape
    rows = _pick(p, (384, 128, 96))

    def body(x_ref, g_ref, o_ref):
        xv = x_ref[...]
        r = lax.rsqrt(jnp.mean(xv * xv, axis=-1, keepdims=True) + EPS)
        o_ref[...] = (xv * r * g_ref[...]).astype(o_ref.dtype)

    return pl.pallas_call(
        body,
        out_shape=jax.ShapeDtypeStruct((p, d), BF16),
        grid=(p // rows,),
        in_specs=[pl.BlockSpec((rows, d), lambda i: (i, 0)), pl.BlockSpec((1, d), lambda i: (0, 0))],
        out_specs=pl.BlockSpec((rows, d), lambda i: (i, 0)),
        compiler_params=_params("parallel"),
        name=name,
    )(x, g)


def _rmsnorm_bwd(x, g, dy, dres, *, name):
    p, d = x.shape
    rows = _pick(p, (384, 128, 96))

    def body(x_ref, g_ref, dy_ref, dres_ref, dx_ref, dg_ref):
        xv = x_ref[...]
        r = lax.rsqrt(jnp.mean(xv * xv, axis=-1, keepdims=True) + EPS)
        dyv = dy_ref[...]
        gdy = dyv * g_ref[...]
        proj = jnp.mean(xv * gdy, axis=-1, keepdims=True)
        dx_ref[...] = dres_ref[...] + r * gdy - xv * (r * r * r) * proj
        part = jnp.sum(dyv * xv * r, axis=0, keepdims=True)

        @pl.when(pl.program_id(0) == 0)
        def _():
            dg_ref[...] = part

        @pl.when(pl.program_id(0) > 0)
        def _():
            dg_ref[...] += part

    row_spec = pl.BlockSpec((rows, d), lambda i: (i, 0))
    vec_spec = pl.BlockSpec((1, d), lambda i: (0, 0))
    return pl.pallas_call(
        body,
        out_shape=[jax.ShapeDtypeStruct((p, d), F32), jax.ShapeDtypeStruct((1, d), F32)],
        grid=(p // rows,),
        in_specs=[row_spec, vec_spec, row_spec, row_spec],
        out_specs=[row_spec, vec_spec],
        compiler_params=_params("arbitrary"),
        name=name,
    )(x, g, dy, dres)


def _mlp_fwd(h, g, w1, w2, *, name):
    hn = _rmsnorm_fwd(h, g, name=name + "_norm")

    def act(acc):
        r = jnp.maximum(acc, 0.0)
        return acc, r * r

    z, a2 = _matmul(hn, w1, mode="nn", out_dtypes=(F32, BF16), epilogue=act, name=name + "_up")
    out = _matmul(a2, w2, mode="nn", out_dtypes=(F32,), epilogue=_add_epilogue, extras=(h,), name=name + "_down")
    return out, (hn, z, a2)


def _mlp_bwd(h, g, w1, w2, saved, dout, *, name):
    hn, z, a2 = saved

    def dact(acc, zt):
        return (acc * (2.0 * jnp.maximum(zt, 0.0)),)

    dz = _matmul(dout, w2, mode="nt", out_dtypes=(BF16,), epilogue=dact, extras=(z,), name=name + "_dz")
    dw2 = _matmul(a2, dout, mode="tn", out_dtypes=(F32,), name=name + "_dw2")
    dw1 = _matmul(hn, dz, mode="tn", out_dtypes=(F32,), name=name + "_dw1")
    dhn = _matmul(dz, w1, mode="nt", out_dtypes=(F32,), name=name + "_dhn")
    dh, dg = _rmsnorm_bwd(h, g, dhn, dout, name=name + "_dnorm")
    return dh, dg, dw1, dw2


def _retention_tables(p):
    half = RET_QK_DIM // 2
    inv_freq = ROPE_BASE ** (-jnp.arange(half, dtype=F32) / half)
    ang = jnp.arange(p, dtype=F32)[:, None] * inv_freq[None, :]
    cos, sin = jnp.cos(ang), jnp.sin(ang)
    cosf = jnp.concatenate([cos, cos], axis=1)
    sins = jnp.concatenate([-sin, sin], axis=1)
    log_g = jnp.log1p(-jnp.exp2(-RET_DECAY_OFFSET - jnp.arange(RET_HEADS, dtype=F32)))
    idx = jnp.arange(CHUNK, dtype=F32)
    diff = idx[:, None] - idx[None, :]
    inner = jnp.where(diff[None] >= 0, jnp.exp(jnp.maximum(diff, 0.0)[None] * log_g[:, None, None]), 0.0)
    kdec = jnp.exp((CHUNK - 1 - idx)[None, :] * log_g[:, None])
    qdec = jnp.exp((idx + 1.0)[None, :] * log_g[:, None])
    cdec = jnp.exp(CHUNK * log_g)
    kdec = jnp.broadcast_to(kdec[:, :, None], (RET_HEADS, CHUNK, RET_QK_DIM))
    qdec = jnp.broadcast_to(qdec[:, :, None], (RET_HEADS, CHUNK, RET_QK_DIM))
    cdec = jnp.broadcast_to(cdec[:, None, None], (RET_HEADS, RET_QK_DIM, RET_V_DIM))
    return cosf, sins, inner, kdec, qdec, cdec


def _rot(x, cosf, sins):
    return x * cosf + pltpu.roll(x, RET_QK_DIM // 2, 1) * sins


def _rot_bwd(dy, cosf, sins):
    return dy * cosf + pltpu.roll(dy * sins, RET_QK_DIM // 2, 1)


def _ret_in_specs(chunk_of):
    nh = RET_HEADS
    q_spec = pl.BlockSpec((CHUNK, RET_QK_DIM), lambda h, s: (chunk_of(s), h))
    k_spec = pl.BlockSpec((CHUNK, RET_QK_DIM), lambda h, s: (chunk_of(s), nh + h))
    v_spec = pl.BlockSpec((CHUNK, RET_V_DIM), lambda h, s: (chunk_of(s), nh + h))
    g_spec = pl.BlockSpec((CHUNK, RET_V_DIM), lambda h, s: (chunk_of(s), 2 * nh + h))
    rope_spec = pl.BlockSpec((CHUNK, RET_QK_DIM), lambda h, s: (chunk_of(s), 0))
    head_sq = pl.BlockSpec((None, CHUNK, CHUNK), lambda h, s: (h, 0, 0))
    head_qk = pl.BlockSpec((None, CHUNK, RET_QK_DIM), lambda h, s: (h, 0, 0))
    head_st = pl.BlockSpec((None, RET_QK_DIM, RET_V_DIM), lambda h, s: (h, 0, 0))
    gam_spec = pl.BlockSpec((None, 1, RET_V_DIM), lambda h, s: (h, 0, 0))
    return [q_spec, k_spec, v_spec, g_spec, rope_spec, rope_spec, head_sq, head_qk, head_qk, head_st, gam_spec]


def _retention_fwd(proj, gn_g, tables, *, name):
    p = proj.shape[0]
    n_chunks = p // CHUNK
    scale = RET_QK_DIM ** -0.5

    def body(q_ref, k_ref, v_ref, g_ref, cos_ref, sin_ref, inner_ref, kdec_ref, qdec_ref, cdec_ref, gam_ref,
             og_ref, opre_ref, sprev_ref, s_scr):
        @pl.when(pl.program_id(1) == 0)
        def _():
            s_scr[...] = jnp.zeros_like(s_scr)

        cosf, sins = cos_ref[...], sin_ref[...]
        qr = _rot(q_ref[...], cosf, sins)
        kr = _rot(k_ref[...], cosf, sins) * scale
        vb = v_ref[...].astype(BF16)
        scores = _dot_nt(qr.astype(BF16), kr.astype(BF16)) * inner_ref[...]
        state = s_scr[...]
        sprev_ref[...] = state
        o = _dot(scores.astype(BF16), vb) + _dot((qr * qdec_ref[...]).astype(BF16), state.astype(BF16))
        kd = kr * kdec_ref[...]
        s_scr[...] = cdec_ref[...] * state + _dot(kd.T.astype(BF16), vb)
        opre_ref[...] = o
        mu = jnp.mean(o, axis=-1, keepdims=True)
        oc = o - mu
        var = jnp.mean(oc * oc, axis=-1, keepdims=True)
        on = oc * lax.rsqrt(var + EPS) * gam_ref[...]
        gv = g_ref[...]
        og_ref[...] = (gv * _sigmoid(gv) * on).astype(og_ref.dtype)

    chunk_of = lambda s: s
    out_v = pl.BlockSpec((CHUNK, RET_V_DIM), lambda h, s: (s, h))
    return pl.pallas_call(
        body,
        out_shape=[
            jax.ShapeDtypeStruct((p, RET_V_W), BF16),
            jax.ShapeDtypeStruct((p, RET_V_W), F32),
            jax.ShapeDtypeStruct((RET_HEADS, n_chunks, RET_QK_DIM, RET_V_DIM), F32),
        ],
        grid=(RET_HEADS, n_chunks),
        in_specs=_ret_in_specs(chunk_of),
        out_specs=[out_v, out_v, pl.BlockSpec((None, None, RET_QK_DIM, RET_V_DIM), lambda h, s: (h, s, 0, 0))],
        scratch_shapes=[pltpu.VMEM((RET_QK_DIM, RET_V_DIM), F32)],
        compiler_params=_params("parallel", "arbitrary"),
        name=name,
    )(proj, proj, proj, proj, *tables, gn_g.reshape(RET_HEADS, 1, RET_V_DIM))


def _retention_bwd(proj, gn_g, tables, opre, sprev, dog, *, name):
    p = proj.shape[0]
    n_chunks = p // CHUNK
    scale = RET_QK_DIM ** -0.5

    def body(q_ref, k_ref, v_ref, g_ref, cos_ref, sin_ref, inner_ref, kdec_ref, qdec_ref, cdec_ref, gam_ref,
             opre_ref, sprev_ref, dog_ref, dq_ref, dk_ref, dv_ref, dg_ref, dgam_ref, ds_scr):
        first = pl.program_id(1) == 0

        @pl.when(first)
        def _():
            ds_scr[...] = jnp.zeros_like(ds_scr)

        cosf, sins = cos_ref[...], sin_ref[...]
        qr = _rot(q_ref[...], cosf, sins)
        kr = _rot(k_ref[...], cosf, sins) * scale
        qb, kb = qr.astype(BF16), kr.astype(BF16)
        vb = v_ref[...].astype(BF16)
        inner = inner_ref[...]
        qdec, kdec = qdec_ref[...], kdec_ref[...]
        state_b = sprev_ref[...].astype(BF16)
        o = opre_ref[...]
        mu = jnp.mean(o, axis=-1, keepdims=True)
        oc = o - mu
        rstd = lax.rsqrt(jnp.mean(oc * oc, axis=-1, keepdims=True) + EPS)
        xhat = oc * rstd
        gam = gam_ref[...]
        on = xhat * gam
        gv = g_ref[...]
        sig = _sigmoid(gv)
        dogv = dog_ref[...]
        dg_ref[...] = (dogv * on * sig * (1.0 + gv * (1.0 - sig))).astype(dg_ref.dtype)
        don = dogv * gv * sig
        dgam_part = jnp.sum(don * xhat, axis=0, keepdims=True)

        @pl.when(first)
        def _():
            dgam_ref[...] = dgam_part

        @pl.when(jnp.logical_not(first))
        def _():
            dgam_ref[...] += dgam_part

        dxhat = don * gam
        do = rstd * (dxhat - jnp.mean(dxhat, axis=-1, keepdims=True)
                     - xhat * jnp.mean(dxhat * xhat, axis=-1, keepdims=True))
        dob = do.astype(BF16)
        scores_b = (_dot_nt(qb, kb) * inner).astype(BF16)
        da = (_dot_nt(dob, vb) * inner).astype(BF16)
        dv = _dot(scores_b.astype(F32).T.astype(BF16), dob)
        dqr = _dot(da, kb)
        dkr = _dot(da.astype(F32).T.astype(BF16), qb)
        dqr += _dot_nt(dob, state_b) * qdec
        ds_local = _dot((qr * qdec).T.astype(BF16), dob)
        gstate = ds_scr[...]
        gb = gstate.astype(BF16)
        kd_b = (kr * kdec).astype(BF16)
        dkr += _dot_nt(vb, gb) * kdec
        dv += _dot(kd_b, gb)
        ds_scr[...] = cdec_ref[...] * gstate + ds_local
        dq_ref[...] = _rot_bwd(dqr, cosf, sins).astype(dq_ref.dtype)
        dk_ref[...] = _rot_bwd(dkr * scale, cosf, sins).astype(dk_ref.dtype)
        dv_ref[...] = dv.astype(dv_ref.dtype)

    chunk_of = lambda s: n_chunks - 1 - s
    blk_v = pl.BlockSpec((CHUNK, RET_V_DIM), lambda h, s: (chunk_of(s), h))
    blk_qk = pl.BlockSpec((CHUNK, RET_QK_DIM), lambda h, s: (chunk_of(s), h))
    st_spec = pl.BlockSpec((None, None, RET_QK_DIM, RET_V_DIM), lambda h, s: (h, chunk_of(s), 0, 0))
    return pl.pallas_call(
        body,
        out_shape=[
            jax.ShapeDtypeStruct((p, RET_QK_W), BF16),
            jax.ShapeDtypeStruct((p, RET_QK_W), BF16),
            jax.ShapeDtypeStruct((p, RET_V_W), BF16),
            jax.ShapeDtypeStruct((p, RET_V_W), BF16),
            jax.ShapeDtypeStruct((RET_HEADS, 1, RET_V_DIM), F32),
        ],
        grid=(RET_HEADS, n_chunks),
        in_specs=_ret_in_specs(chunk_of) + [blk_v, st_spec, blk_v],
        out_specs=[blk_qk, blk_qk, blk_v, blk_v, pl.BlockSpec((None, 1, RET_V_DIM), lambda h, s: (h, 0, 0))],
        scratch_shapes=[pltpu.VMEM((RET_QK_DIM, RET_V_DIM), F32)],
        compiler_params=_params("parallel", "arbitrary"),
        name=name,
    )(proj, proj, proj, proj, *tables, gn_g.reshape(RET_HEADS, 1, RET_V_DIM), opre, sprev, dog)


def _conv_rows(p):
    return _pick(p, (384, 128))


def _ln_stats(y):
    mu = jnp.mean(y, axis=-1, keepdims=True)
    yc = y - mu
    rstd = lax.rsqrt(jnp.mean(yc * yc, axis=-1, keepdims=True) + EPS)
    return yc * rstd, rstd


def _conv_fwd(proj, conv_w, conv_b, ln_g, ln_b, *, name):
    p = proj.shape[0]
    c = D_MODEL
    rows = _conv_rows(p)
    hpb = rows // CONV_HALO
    a_col, gate_col = (2 * RET_QK_W + 2 * RET_V_W) // c, (2 * RET_QK_W + 2 * RET_V_W) // c + 1

    def body(a_ref, gate_ref, ah_ref, gateh_ref, w_ref, b_ref, lg_ref, lb_ref, c_ref, y_ref, hdn_scr):
        i = pl.program_id(0)
        hdn_scr[0:CONV_HALO, :] = ah_ref[...] * _sigmoid(gateh_ref[...])
        hdn_scr[CONV_HALO:, :] = a_ref[...] * _sigmoid(gate_ref[...])
        acc = jnp.zeros((rows, c), F32)
        for w in range(CONV_WIDTH):
            off = CONV_HALO - (CONV_WIDTH - 1) + w
            acc += hdn_scr[off:off + rows, :] * w_ref[w:w + 1, :]
        y = acc + b_ref[...]
        y_ref[...] = y
        yhat, _ = _ln_stats(y)
        ln = yhat * lg_ref[...] + lb_ref[...]
        row = i * rows + lax.broadcasted_iota(jnp.int32, (rows, 1), 0)
        c_ref[...] = jnp.where(row >= PAD_FRONT, ln * _sigmoid(ln), 0.0).astype(c_ref.dtype)

    halo_idx = lambda i: jnp.maximum(i * hpb - 1, 0)
    vec = pl.BlockSpec((1, c), lambda i: (0, 0))
    return pl.pallas_call(
        body,
        out_shape=[jax.ShapeDtypeStruct((p, c), BF16), jax.ShapeDtypeStruct((p, c), F32)],
        grid=(p // rows,),
        in_specs=[
            pl.BlockSpec((rows, c), lambda i: (i, a_col)),
            pl.BlockSpec((rows, c), lambda i: (i, gate_col)),
            pl.BlockSpec((CONV_HALO, c), lambda i: (halo_idx(i), a_col)),
            pl.BlockSpec((CONV_HALO, c), lambda i: (halo_idx(i), gate_col)),
            pl.BlockSpec((CONV_WIDTH, c), lambda i: (0, 0)),
            vec, vec, vec,
        ],
        out_specs=[pl.BlockSpec((rows, c), lambda i: (i, 0)), pl.BlockSpec((rows, c), lambda i: (i, 0))],
        scratch_shapes=[pltpu.VMEM((CONV_HALO + rows, c), F32)],
        compiler_params=_params("parallel"),
        name=name,
    )(proj, proj, proj, proj, conv_w, conv_b, ln_g, ln_b)


def _conv_bwd(proj, conv_w, ln_g, ln_b, y, dcat, *, name):
    p = proj.shape[0]
    c = D_MODEL
    rows = _conv_rows(p)
    hpb = rows // CONV_HALO
    n_blocks = p // rows
    a_col, gate_col = (2 * RET_QK_W + 2 * RET_V_W) // c, (2 * RET_QK_W + 2 * RET_V_W) // c + 1

    def body(a_ref, gate_ref, ah_ref, gateh_ref, w_ref, lg_ref, lb_ref, y_ref, yh_ref, dc_ref, dch_ref,
             da_ref, dgate_ref, dw_ref, db_ref, dlg_ref, dlb_ref, hdn_scr, dy_scr):
        i = pl.program_id(0)
        lg, lb = lg_ref[...], lb_ref[...]

        def ln_bwd(yv, dcv):
            yhat, rstd = _ln_stats(yv)
            ln = yhat * lg + lb
            sig = _sigmoid(ln)
            dln = dcv * sig * (1.0 + ln * (1.0 - sig))
            dyhat = dln * lg
            dyv = rstd * (dyhat - jnp.mean(dyhat, axis=-1, keepdims=True)
                          - yhat * jnp.mean(dyhat * yhat, axis=-1, keepdims=True))
            return dyv, dln, yhat

        row = i * rows + lax.broadcasted_iota(jnp.int32, (rows, 1), 0)
        dy, dln, yhat = ln_bwd(y_ref[...], jnp.where(row >= PAD_FRONT, dc_ref[...], 0.0))
        dy_halo, _, _ = ln_bwd(yh_ref[...], dch_ref[...])
        dy_scr[0:rows, :] = dy
        dy_scr[rows:, :] = jnp.where(i == n_blocks - 1, 0.0, dy_halo)
        sig_gate = _sigmoid(gate_ref[...])
        av = a_ref[...]
        hdn_scr[0:CONV_HALO, :] = ah_ref[...] * _sigmoid(gateh_ref[...])
        hdn_scr[CONV_HALO:, :] = av * sig_gate
        @pl.when(i == 0)
        def _():
            dw_ref[...] = jnp.zeros_like(dw_ref)
            db_ref[...] = jnp.zeros_like(db_ref)
            dlg_ref[...] = jnp.zeros_like(dlg_ref)
            dlb_ref[...] = jnp.zeros_like(dlb_ref)

        dhdn = jnp.zeros((rows, c), F32)
        for w in range(CONV_WIDTH):
            back = CONV_WIDTH - 1 - w
            dhdn += dy_scr[back:back + rows, :] * w_ref[w:w + 1, :]
            off = CONV_HALO - (CONV_WIDTH - 1) + w
            dw_ref[w:w + 1, :] += jnp.sum(dy * hdn_scr[off:off + rows, :], axis=0, keepdims=True)
        da_ref[...] = (dhdn * sig_gate).astype(da_ref.dtype)
        dgate_ref[...] = (dhdn * av * sig_gate * (1.0 - sig_gate)).astype(dgate_ref.dtype)
        db_ref[...] += jnp.sum(dy, axis=0, keepdims=True)
        dlg_ref[...] += jnp.sum(dln * yhat, axis=0, keepdims=True)
        dlb_ref[...] += jnp.sum(dln, axis=0, keepdims=True)

    prev_halo = lambda i: jnp.maximum(i * hpb - 1, 0)
    next_halo = lambda i: jnp.minimum((i + 1) * hpb, p // CONV_HALO - 1)
    vec = pl.BlockSpec((1, c), lambda i: (0, 0))
    blk = lambda col: pl.BlockSpec((rows, c), lambda i: (i, col))
    outs = pl.pallas_call(
        body,
        out_shape=[
            jax.ShapeDtypeStruct((p, c), BF16),
            jax.ShapeDtypeStruct((p, c), BF16),
            jax.ShapeDtypeStruct((CONV_WIDTH + 1, c), F32),
            jax.ShapeDtypeStruct((1, c), F32),
            jax.ShapeDtypeStruct((1, c), F32),
            jax.ShapeDtypeStruct((1, c), F32),
        ],
        grid=(n_blocks,),
        in_specs=[
            blk(a_col), blk(gate_col),
            pl.BlockSpec((CONV_HALO, c), lambda i: (prev_halo(i), a_col)),
            pl.BlockSpec((CONV_HALO, c), lambda i: (prev_halo(i), gate_col)),
            pl.BlockSpec((CONV_WIDTH, c), lambda i: (0, 0)),
            vec, vec,
            blk(0),
            pl.BlockSpec((CONV_HALO, c), lambda i: (next_halo(i), 0)),
            blk(1),
            pl.BlockSpec((CONV_HALO, c), lambda i: (next_halo(i), 1)),
        ],
        out_specs=[blk(0), blk(0), pl.BlockSpec((CONV_WIDTH + 1, c), lambda i: (0, 0)), vec, vec, vec],
        scratch_shapes=[pltpu.VMEM((CONV_HALO + rows, c), F32), pltpu.VMEM((rows + CONV_HALO, c), F32)],
        compiler_params=_params("arbitrary"),
        name=name,
    )(proj, proj, proj, proj, conv_w, ln_g, ln_b, y, y, dcat, dcat)
    da, dgate, dw, db, dlg, dlb = outs
    return da, dgate, dw[:CONV_WIDTH], db, dlg, dlb


def _group_matrix():
    r = jnp.arange(D_MODEL)[:, None] // SB_HEAD_DIM
    c = jnp.arange(D_MODEL)[None, :] // SB_HEAD_DIM
    return (r == c).astype(BF16)


def _qknorm_fwd(qkv, qg, kg, *, name):
    p = qkv.shape[0]
    d = D_MODEL
    rows = _pick(p, (384, 128, 96))

    def body(q_ref, k_ref, v_ref, qg_ref, kg_ref, gm_ref, qn_ref, kn_ref, vb_ref):
        gm = gm_ref[...]

        def norm(x, g):
            ms = _split_dot(x * x, gm) * (1.0 / SB_HEAD_DIM)
            return x * lax.rsqrt(ms + EPS) * g

        qn_ref[...] = norm(q_ref[...], qg_ref[...]).astype(BF16)
        kn_ref[...] = norm(k_ref[...], kg_ref[...]).astype(BF16)
        vb_ref[...] = v_ref[...].astype(BF16)

    blk = lambda col: pl.BlockSpec((rows, d), lambda i: (i, col))
    vec = pl.BlockSpec((1, d), lambda i: (0, 0))
    return pl.pallas_call(
        body,
        out_shape=[jax.ShapeDtypeStruct((p, d), BF16)] * 3,
        grid=(p // rows,),
        in_specs=[blk(0), blk(1), blk(2), vec, vec, pl.BlockSpec((d, d), lambda i: (0, 0))],
        out_specs=[blk(0)] * 3,
        compiler_params=_params("parallel"),
        name=name,
    )(qkv, qkv, qkv, qg, kg, _group_matrix())


def _qknorm_bwd(qkv, qg, kg, dqn, dkn, dv, *, name):
    p = qkv.shape[0]
    d = D_MODEL
    rows = _pick(p, (384, 128, 96))

    def body(q_ref, k_ref, qg_ref, kg_ref, gm_ref, dqn_ref, dkn_ref, dv_ref, dqkv_ref, dqg_ref, dkg_ref):
        gm = gm_ref[...]

        def bwd(x, g, dy):
            ms = _split_dot(x * x, gm) * (1.0 / SB_HEAD_DIM)
            r = lax.rsqrt(ms + EPS)
            gdy = dy * g
            proj = _split_dot(x * gdy, gm) * (1.0 / SB_HEAD_DIM)
            return r * gdy - x * (r * r * r) * proj, jnp.sum(dy * x * r, axis=0, keepdims=True)

        dq, dqg = bwd(q_ref[...], qg_ref[...], dqn_ref[...])
        dk, dkg = bwd(k_ref[...], kg_ref[...], dkn_ref[...])
        dqkv_ref[:, 0:d] = dq.astype(BF16)
        dqkv_ref[:, d:2 * d] = dk.astype(BF16)
        dqkv_ref[:, 2 * d:3 * d] = dv_ref[...].astype(BF16)

        @pl.when(pl.program_id(0) == 0)
        def _():
            dqg_ref[...] = dqg
            dkg_ref[...] = dkg

        @pl.when(pl.program_id(0) > 0)
        def _():
            dqg_ref[...] += dqg
            dkg_ref[...] += dkg

    blk = lambda col: pl.BlockSpec((rows, d), lambda i: (i, col))
    vec = pl.BlockSpec((1, d), lambda i: (0, 0))
    return pl.pallas_call(
        body,
        out_shape=[jax.ShapeDtypeStruct((p, 3 * d), BF16), jax.ShapeDtypeStruct((1, d), F32),
                   jax.ShapeDtypeStruct((1, d), F32)],
        grid=(p // rows,),
        in_specs=[blk(0), blk(1), vec, vec, pl.BlockSpec((d, d), lambda i: (0, 0)), blk(0), blk(0), blk(0)],
        out_specs=[pl.BlockSpec((rows, 3 * d), lambda i: (i, 0)), vec, vec],
        compiler_params=_params("arbitrary"),
        name=name,
    )(qkv, qkv, qg, kg, _group_matrix(), dqn, dkn, dv)


SB_PAIR = 2 * SB_HEAD_DIM
SB_GROUP = 4
SB_MASKED = -1e30


def _sb_consts():
    lane = lax.broadcasted_iota(jnp.int32, (CHUNK, SB_PAIR), 1)
    r = lax.broadcasted_iota(jnp.int32, (CHUNK, CHUNK), 0)
    c = lax.broadcasted_iota(jnp.int32, (CHUNK, CHUNK), 1)
    lo = (lane < SB_HEAD_DIM).astype(F32).astype(BF16)
    ones = jnp.ones((CHUNK, CHUNK), BF16)
    twice = lambda m: jnp.concatenate([jnp.concatenate([m, ones], axis=1)] * 2, axis=0)
    later, earlier = twice((r > c).astype(BF16)), twice((r < c).astype(BF16))
    not_before = (c >= r).astype(F32) * SB_MASKED
    padding = (c < PAD_FRONT).astype(F32) * SB_MASKED
    return (lo, 1.0 - lo), c, later, earlier, not_before, padding


def _sb_halves(t, head_lanes):
    return t * head_lanes[0], t * head_lanes[1]


def _sb_logits(qh, kg, biases):
    z = _dot_nt(qh, kg)
    tiles = []
    for b, bias in enumerate(biases):
        zt = z[:, b * CHUNK:(b + 1) * CHUNK]
        if bias is not None:
            zt = zt + bias
        ls_pos = jnp.minimum(zt, 0.0) - jnp.log(1.0 + jnp.exp(-jnp.abs(zt)))
        tiles.append((ls_pos, ls_pos - zt))
    return tiles


def _sb_block_sums(tiles, m):
    st = jnp.concatenate(tiles, axis=0)
    hi = st.astype(BF16)
    lo = (st - hi.astype(F32)).astype(BF16)
    tot = _dot(jnp.concatenate([hi, lo], axis=1), m)
    return [(tot[i * CHUNK:(i + 1) * CHUNK, 0:CHUNK], tot[i * CHUNK:(i + 1) * CHUNK, CHUNK:2 * CHUNK])
            for i in range(len(tiles))]


def _sb_plan(qi, padding, not_before):
    top = lax.div(qi, SB_GROUP)
    size = qi - SB_GROUP * top + 1
    pad_if_first = padding * (top == 0).astype(F32)
    masks = []
    for n_b in range(1, SB_GROUP + 1):
        m = [None] * n_b
        m[n_b - 1] = not_before
        m[0] = pad_if_first if m[0] is None else m[0] + pad_if_first
        masks.append(m)
    return top, size, masks


def _once_if(cond, fn, carry):
    return lax.fori_loop(0, jnp.where(cond, 1, 0), lambda s, cr: fn(cr), carry)


def _sb_head_rows(tg, lanes, n_b):
    return jnp.concatenate([tg[b * CHUNK:(b + 1) * CHUNK] * lanes for b in range(n_b)], axis=0)


def _sb_fwd(qn, kn, vb, *, name):
    p = qn.shape[0]
    n_blocks = p // CHUNK
    n_pairs = SB_HEADS // 2
    scale = SB_HEAD_DIM ** -0.5

    def body(q_ref, k_ref, v_ref, o_ref, car_ref):
        head_lanes, c, later, _, not_before, padding = _sb_consts()

        def q_block(qi, _):
            rows = pl.ds(pl.multiple_of(qi * CHUNK, CHUNK), CHUNK)
            qh = _sb_halves(q_ref[rows, :], head_lanes)
            qs = (qh[0] * scale, qh[1] * scale)

            def blocks(kb0, biases, carry):
                n_b = len(biases)
                acc, run0, run1, sav0, sav1 = carry
                krows = pl.ds(pl.multiple_of(kb0 * CHUNK, CHUNK), n_b * CHUNK)
                kg, vg = k_ref[krows, :], v_ref[krows, :]
                tiles = [_sb_logits(qs[h], kg, biases) for h in range(2)]
                sums = [_sb_block_sums([log_keep for _, log_keep in tiles[h]], later) for h in range(2)]
                cols = [(c == kb0 + b).astype(F32) for b in range(n_b)]
                runs, savs = [run0, run1], [sav0, sav1]
                for h in range(2):
                    ws = [None] * n_b
                    for b in reversed(range(n_b)):
                        after, row_sum = sums[h][b]
                        ws[b] = jnp.exp(tiles[h][b][0] + after + runs[h]).astype(BF16)
                        savs[h] = savs[h] + cols[b] * runs[h]
                        runs[h] = runs[h] + row_sum
                    acc = acc + _dot(jnp.concatenate(ws, axis=1), _sb_head_rows(vg, head_lanes[h], n_b))
                return acc, runs[0], runs[1], savs[0], savs[1]

            zt = qh[0].astype(F32) * 0.0
            top, size, masks = _sb_plan(qi, padding, not_before)
            carry = (zt, zt, zt, zt, zt)
            for m in masks:
                carry = _once_if(size == len(m), functools.partial(blocks, SB_GROUP * top, m), carry)
            carry = lax.fori_loop(0, jnp.maximum(top - 1, 0),
                                  lambda it, cr: blocks(SB_GROUP * (top - 1 - it), [None] * SB_GROUP, cr), carry)
            carry = _once_if(top > 0, functools.partial(blocks, 0, [padding] + [None] * (SB_GROUP - 1)), carry)
            acc, _, _, sav0, sav1 = carry
            o_ref[rows, :] = acc.astype(o_ref.dtype)
            car_ref[rows, 0:CHUNK] = sav0
            car_ref[rows, CHUNK:2 * CHUNK] = sav1
            return 0

        lax.fori_loop(0, n_blocks, q_block, 0)

    col = pl.BlockSpec((p, SB_PAIR), lambda g: (0, g))
    return pl.pallas_call(
        body,
        out_shape=[jax.ShapeDtypeStruct((p, D_MODEL), BF16), jax.ShapeDtypeStruct((p, n_pairs * 2 * CHUNK), F32)],
        grid=(n_pairs,),
        in_specs=[col, col, col],
        out_specs=[col, pl.BlockSpec((p, 2 * CHUNK), lambda g: (0, g))],
        compiler_params=_params("parallel"),
        name=name,
    )(qn, kn, vb)


def _sb_bwd(qn, kn, vb, carries, do, *, name):
    p = qn.shape[0]
    n_blocks = p // CHUNK
    n_pairs = SB_HEADS // 2
    scale = SB_HEAD_DIM ** -0.5

    def body(q_ref, k_ref, v_ref, car_ref, do_ref, dq_ref, dk_ref, dv_ref):
        head_lanes, c, later, earlier, not_before, padding = _sb_consts()
        dk_ref[...] = jnp.zeros_like(dk_ref)
        dv_ref[...] = jnp.zeros_like(dv_ref)

        def q_block(qi, _):
            rows = pl.ds(pl.multiple_of(qi * CHUNK, CHUNK), CHUNK)
            qh = _sb_halves(q_ref[rows, :], head_lanes)
            qs = (qh[0] * scale, qh[1] * scale)
            doh = _sb_halves(do_ref[rows, :].astype(BF16), head_lanes)
            sav = (car_ref[rows, 0:CHUNK], car_ref[rows, CHUNK:2 * CHUNK])

            def blocks(kb0, biases, carry):
                n_b = len(biases)
                dq_acc, pre0, pre1 = carry
                krows = pl.ds(pl.multiple_of(kb0 * CHUNK, CHUNK), n_b * CHUNK)
                kg, vg = k_ref[krows, :], v_ref[krows, :]
                cols = [(c == kb0 + b).astype(F32) for b in range(n_b)]
                block = lambda t, b: t[:, b * CHUNK:(b + 1) * CHUNK]
                tiles = [_sb_logits(qs[h], kg, biases) for h in range(2)]
                afters = [_sb_block_sums([log_keep for _, log_keep in tiles[h]], later) for h in range(2)]
                dws = [_dot_nt(doh[h], vg) for h in range(2)]
                ws, es, befores = [], [], []
                for h in range(2):
                    runs = [jnp.sum(cols[b] * sav[h], axis=-1, keepdims=True) for b in range(n_b)]
                    ws.append([jnp.exp(tiles[h][b][0] + afters[h][b][0] + runs[b]) for b in range(n_b)])
                    es.append([ws[h][b] * block(dws[h], b) for b in range(n_b)])
                    befores.append(_sb_block_sums(es[h], earlier))
                pres = [pre0, pre1]
                dk_add = dv_add = None
                for h in range(2):
                    dzs = []
                    for b in range(n_b):
                        before, row_sum = befores[h][b]
                        sig = jnp.exp(tiles[h][b][0])
                        e = es[h][b]
                        dzs.append((e - (e + before + pres[h]) * sig).astype(BF16))
                        pres[h] = pres[h] + row_sum
                    dz = jnp.concatenate(dzs, axis=1)
                    w = jnp.concatenate([t.astype(BF16) for t in ws[h]], axis=1)
                    dq_acc = dq_acc + _dot(dz, _sb_head_rows(kg, head_lanes[h], n_b))
                    dv_h, dk_h = _dot_tn(w, doh[h]), _dot_tn(dz, qs[h])
                    dv_add = dv_h if dv_add is None else dv_add + dv_h
                    dk_add = dk_h if dk_add is None else dk_add + dk_h
                dv_ref[krows, :] += dv_add
                dk_ref[krows, :] += dk_add
                return dq_acc, pres[0], pres[1]

            zt = qh[0].astype(F32) * 0.0
            top, size, masks = _sb_plan(qi, padding, not_before)
            carry = _once_if(top > 0, functools.partial(blocks, 0, [padding] + [None] * (SB_GROUP - 1)), (zt, zt, zt))
            carry = lax.fori_loop(1, top, lambda g, cr: blocks(SB_GROUP * g, [None] * SB_GROUP, cr), carry)
            for m in masks:
                carry = _once_if(size == len(m), functools.partial(blocks, SB_GROUP * top, m), carry)
            dq_acc, _, _ = carry
            dq_ref[rows, :] = dq_acc * scale
            return 0

        lax.fori_loop(0, n_blocks, q_block, 0)

    col = pl.BlockSpec((p, SB_PAIR), lambda g: (0, g))
    return pl.pallas_call(
        body,
        out_shape=[jax.ShapeDtypeStruct((p, D_MODEL), F32)] * 3,
        grid=(n_pairs,),
        in_specs=[col, col, col, pl.BlockSpec((p, 2 * CHUNK), lambda g: (0, g)), col],
        out_specs=[col, col, col],
        compiler_params=_params("parallel"),
        name=name,
    )(qn, kn, vb, carries, do)


def _loss_head(h, target, *, name):
    p, d = h.shape
    n_blocks = p // CHUNK

    def body(h_ref, t_ref, sq_ref, dh_ref):
        i = pl.program_id(0)

        @pl.when(i == 0)
        def _():
            sq_ref[...] = jnp.zeros_like(sq_ref)
            dh_ref[...] = jnp.zeros_like(dh_ref)

        @pl.when(i > 0)
        def _():
            err = h_ref[...] - t_ref[...]
            sq_ref[...] += jnp.sum(err * err)
            dh_ref[...] = err * (1.0 / d)

    return pl.pallas_call(
        body,
        out_shape=[jax.ShapeDtypeStruct((8, 128), F32), jax.ShapeDtypeStruct((p, d), F32)],
        grid=(n_blocks,),
        in_specs=[pl.BlockSpec((CHUNK, d), lambda i: (i, 0)),
                  pl.BlockSpec((CHUNK, d), lambda i: (jnp.maximum(i - 1, 0), 0))],
        out_specs=[pl.BlockSpec((8, 128), lambda i: (0, 0)), pl.BlockSpec((CHUNK, d), lambda i: (i, 0))],
        compiler_params=_params("arbitrary"),
        name=name,
    )(h, target)


def _local_step(x, target, meta, norm_mix_g, norm_mlp_g, w_in, gn_g, conv_w, conv_b, ln_g, ln_b, w_out,
                w_qkv, qn_g, kn_g, w_o, w1, w2):
    seq = x.shape[0]
    p = PAD_FRONT + N_META + seq
    d = D_MODEL
    tables = _retention_tables(p)
    row = lambda v: v.reshape(1, -1)
    h0 = jnp.concatenate([jnp.zeros((PAD_FRONT, d), F32), meta, x], axis=0)

    hn0 = _rmsnorm_fwd(h0, row(norm_mix_g[0]), name="l0_mix_norm")
    proj = _matmul(hn0, w_in, mode="nn", out_dtypes=(F32,), name="l0_proj")
    og, opre, sprev = _retention_fwd(proj, gn_g, tables, name="l0_retention")
    cb, y_conv = _conv_fwd(proj, conv_w, row(conv_b), row(ln_g), row(ln_b), name="l0_conv")
    cat = jnp.concatenate([og, cb], axis=1)
    h1 = _matmul(cat, w_out, mode="nn", out_dtypes=(F32,), epilogue=_add_epilogue, extras=(h0,), name="l0_mix_out")
    h2, mlp0 = _mlp_fwd(h1, row(norm_mlp_g[0]), w1[0], w2[0], name="l0_mlp")

    hn1 = _rmsnorm_fwd(h2, row(norm_mix_g[1]), name="l1_mix_norm")
    qkv = _matmul(hn1, w_qkv, mode="nn", out_dtypes=(F32,), name="l1_qkv")
    qg_t, kg_t = jnp.tile(row(qn_g), (1, SB_HEADS)), jnp.tile(row(kn_g), (1, SB_HEADS))
    qn, kn, vb = _qknorm_fwd(qkv, qg_t, kg_t, name="l1_qknorm")
    o_sb, carries = _sb_fwd(qn, kn, vb, name="l1_stickbreak")
    h3 = _matmul(o_sb, w_o, mode="nn", out_dtypes=(F32,), epilogue=_add_epilogue, extras=(h2,), name="l1_mix_out")
    h4, mlp1 = _mlp_fwd(h3, row(norm_mlp_g[1]), w1[1], w2[1], name="l1_mlp")

    sq, dh4 = _loss_head(h4, target, name="loss_head")

    dh3, dg_mlp1, dw1_1, dw2_1 = _mlp_bwd(h3, row(norm_mlp_g[1]), w1[1], w2[1], mlp1, dh4, name="l1_mlp_bwd")
    do_sb = _matmul(dh3, w_o, mode="nt", out_dtypes=(F32,), name="l1_do")
    dw_o = _matmul(o_sb, dh3, mode="tn", out_dtypes=(F32,), name="l1_dwo")
    dqn, dkn, dv = _sb_bwd(qn, kn, vb, carries, do_sb, name="l1_stickbreak_bwd")
    dqkv, dqg_t, dkg_t = _qknorm_bwd(qkv, qg_t, kg_t, dqn, dkn, dv, name="l1_qknorm_bwd")
    dw_qkv = _matmul(hn1, dqkv, mode="tn", out_dtypes=(F32,), name="l1_dwqkv")
    dhn1 = _matmul(dqkv, w_qkv, mode="nt", out_dtypes=(F32,), name="l1_dhn")
    dh2, dg_mix1 = _rmsnorm_bwd(h2, row(norm_mix_g[1]), dhn1, dh3, name="l1_mix_dnorm")

    dh1, dg_mlp0, dw1_0, dw2_0 = _mlp_bwd(h1, row(norm_mlp_g[0]), w1[0], w2[0], mlp0, dh2, name="l0_mlp_bwd")
    dcat = _matmul(dh1, w_out, mode="nt", out_dtypes=(F32,), name="l0_dcat")
    dw_out = _matmul(cat, dh1, mode="tn", out_dtypes=(F32,), name="l0_dwout")
    dq, dk, dvr, dgate_r, dgn = _retention_bwd(proj, gn_g, tables, opre, sprev, dcat, name="l0_retention_bwd")
    da, dgate_c, dconv_w, dconv_b, dln_g, dln_b = _conv_bwd(proj, conv_w, row(ln_g), row(ln_b), y_conv, dcat,
                                                            name="l0_conv_bwd")
    dproj = jnp.concatenate([dq, dk, dvr, dgate_r, da, dgate_c], axis=1)
    dw_in = _matmul(hn0, dproj, mode="tn", out_dtypes=(F32,), name="l0_dwin")
    dhn0 = _matmul(dproj, w_in, mode="nt", out_dtypes=(F32,), name="l0_dhn")
    dh0, dg_mix0 = _rmsnorm_bwd(h0, row(norm_mix_g[0]), dhn0, dh1, name="l0_mix_dnorm")

    fold = lambda t: t.reshape(SB_HEADS, SB_HEAD_DIM).sum(axis=0)
    grads = dict(
        x=dh0[PAD_FRONT + N_META:],
        meta=dh0[PAD_FRONT:PAD_FRONT + N_META],
        norm_mix_g=jnp.concatenate([dg_mix0, dg_mix1], axis=0),
        norm_mlp_g=jnp.concatenate([dg_mlp0, dg_mlp1], axis=0),
        even_w_in=dw_in,
        even_ret_gn_g=dgn.reshape(RET_HEADS, RET_V_DIM),
        even_conv_w=dconv_w,
        even_conv_b=dconv_b,
        even_conv_ln_g=dln_g,
        even_conv_ln_b=dln_b,
        even_w_out=dw_out,
        odd_w_qkv=dw_qkv,
        odd_q_norm_g=fold(dqg_t)[None],
        odd_k_norm_g=fold(dkg_t)[None],
        odd_w_o=dw_o,
        mlp_w1=(dw1_0, dw1_1),
        mlp_w2=(dw2_0, dw2_1),
    )
    return sq[0, 0], grads


def _position():
    x, y, c = lax.axis_index("x"), lax.axis_index("y"), lax.axis_index("c")
    other_chips = [(1 - x, y), (x, 1 - y), (1 - x, 1 - y)]
    return x, y, c, other_chips


def _shard_of(ref, kind, s, n):
    rows, cols = ref.shape
    if kind == "col":
        return ref.at[:, pl.ds(s * (cols // n), cols // n)]
    return ref.at[pl.ds(s * (rows // n), rows // n), :]


def _half_of(ref, kind, c):
    rows, cols = ref.shape
    if kind == "col":
        return ref.at[pl.ds(c * (rows // 2), rows // 2), :]
    return ref.at[:, pl.ds(c * (cols // 2), cols // 2)]


def _remote(src, dst, send_sems, recv_sems, idx, device):
    return pltpu.make_async_remote_copy(src_ref=src, dst_ref=dst, send_sem=send_sems.at[idx], recv_sem=recv_sems.at[idx],
                                        device_id=device, device_id_type=MESH)


def _cast_into_whole(w, kind, s_arr, *, name):
    rows, cols = w.shape
    tr = _pick(rows, (256, 128))
    nb = rows // tr
    if kind == "col":
        whole, o_spec = (rows, cols * N_CHIPS), pl.BlockSpec((tr, cols), lambda i, s_ref: (i, s_ref[0]))
    else:
        whole, o_spec = (rows * N_CHIPS, cols), pl.BlockSpec((tr, cols), lambda i, s_ref: (s_ref[0] * nb + i, 0))

    def body(s_ref, w_ref, o_ref):
        o_ref[...] = w_ref[...].astype(BF16)

    return pl.pallas_call(
        body,
        out_shape=jax.ShapeDtypeStruct(whole, BF16),
        grid_spec=pltpu.PrefetchScalarGridSpec(num_scalar_prefetch=1, grid=(nb,),
                                               in_specs=[pl.BlockSpec((tr, cols), lambda i, s_ref: (i, 0))],
                                               out_specs=o_spec),
        compiler_params=_params("parallel"),
        name=name,
    )(s_arr, w)


def _allgather_weights(wholes, kinds):
    n = len(wholes)

    def body(*refs):
        ins, outs = refs[:n], refs[n:2 * n]
        send_sems, recv_sems = refs[2 * n:]
        x, y, c, chips = _position()
        me_chip = 2 * x + y
        sibling = (x, y, 1 - c)
        sends = []
        for t in range(n):
            for k, (cx, cy) in enumerate(chips):
                src = _half_of(_shard_of(ins[t], kinds[t], me_chip, N_CHIPS), kinds[t], c)
                dst = _half_of(_shard_of(outs[t], kinds[t], me_chip, N_CHIPS), kinds[t], c)
                sends.append(_remote(src, dst, send_sems, recv_sems, 6 * t + k, (cx, cy, c)))
        for cp in sends:
            cp.start()
        passed = []
        for t in range(n):
            for k, (cx, cy) in enumerate(chips):
                landed = _half_of(_shard_of(outs[t], kinds[t], 2 * cx + cy, N_CHIPS), kinds[t], c)
                _remote(landed, landed, send_sems, recv_sems, 6 * t + k, (cx, cy, c)).wait_recv()
                fwd = _remote(landed, landed, send_sems, recv_sems, 6 * t + 3 + k, sibling)
                fwd.start()
                passed.append(fwd)
        for t in range(n):
            for k, (cx, cy) in enumerate(chips):
                theirs = _half_of(_shard_of(outs[t], kinds[t], 2 * cx + cy, N_CHIPS), kinds[t], 1 - c)
                _remote(theirs, theirs, send_sems, recv_sems, 6 * t + 3 + k, sibling).wait_recv()
        for cp in sends + passed:
            cp.wait_send()

    return pl.pallas_call(
        body,
        out_shape=[jax.ShapeDtypeStruct(w.shape, BF16) for w in wholes],
        in_specs=[ANY] * n,
        out_specs=[ANY] * n,
        input_output_aliases={t: t for t in range(n)},
        scratch_shapes=[pltpu.SemaphoreType.DMA((6 * n,)), pltpu.SemaphoreType.DMA((6 * n,))],
        name="allgather_weights",
    )(*wholes)


def _allgather8(block, *, name):
    rows, cols = block.shape

    def body(in_ref, out_ref, send_sems, recv_sems, local_sem):
        x, y, c, _ = _position()
        me = 4 * x + 2 * y + c
        mine = pltpu.make_async_copy(in_ref, out_ref.at[me], local_sem)
        mine.start()
        peers = []
        for flip in range(1, N_DEV):
            fx, fy, fc = (flip >> 2) & 1, (flip >> 1) & 1, flip & 1
            peers.append(((1 - x if fx else x), (1 - y if fy else y), (1 - c if fc else c)))
        sends = [_remote(in_ref, out_ref.at[me], send_sems, recv_sems, j, peer) for j, peer in enumerate(peers)]
        for cp in sends:
            cp.start()
        for j, (px, py, pc) in enumerate(peers):
            slot = out_ref.at[4 * px + 2 * py + pc]
            _remote(slot, slot, send_sems, recv_sems, j, (px, py, pc)).wait_recv()
        for cp in sends:
            cp.wait_send()
        mine.wait()

    vmem = pl.BlockSpec(memory_space=pltpu.VMEM)
    return pl.pallas_call(
        body,
        out_shape=jax.ShapeDtypeStruct((N_DEV, rows, cols), F32),
        in_specs=[vmem],
        out_specs=vmem,
        scratch_shapes=[pltpu.SemaphoreType.DMA((N_DEV - 1,)), pltpu.SemaphoreType.DMA((N_DEV - 1,)),
                        pltpu.SemaphoreType.DMA],
        name=name,
    )(block)


def _sum8(stack, *, name):
    _, rows, cols = stack.shape

    def body(s_ref, o_ref):
        acc = s_ref[0]
        for i in range(1, N_DEV):
            acc = acc + s_ref[i]
        o_ref[...] = acc

    return pl.pallas_call(body, out_shape=jax.ShapeDtypeStruct((rows, cols), F32), name=name)(stack)


def _swap_halves_in(grads, kinds):
    n = len(grads)

    def body(*refs):
        ins, outs = refs[:n], refs[n:2 * n]
        send_sems, recv_sems = refs[2 * n:]
        x, y, c, _ = _position()
        sibling = (x, y, 1 - c)
        sends = [_remote(_half_of(ins[t], kinds[t], 1 - c), outs[t], send_sems, recv_sems, t, sibling) for t in range(n)]
        for cp in sends:
            cp.start()
        for t in range(n):
            _remote(_half_of(ins[t], kinds[t], c), outs[t], send_sems, recv_sems, t, sibling).wait_recv()
        for cp in sends:
            cp.wait_send()

    def half(g, kind):
        rows, cols = g.shape
        return (rows // 2, cols) if kind == "col" else (rows, cols // 2)

    return pl.pallas_call(
        body,
        out_shape=[jax.ShapeDtypeStruct(half(g, k), F32) for g, k in zip(grads, kinds)],
        in_specs=[ANY] * n,
        out_specs=[ANY] * n,
        scratch_shapes=[pltpu.SemaphoreType.DMA((n,)), pltpu.SemaphoreType.DMA((n,))],
        name="reduce_core_pair",
    )(*grads)


def _half_add(grad, theirs, kind, c_arr, *, name):
    rows, cols = theirs.shape
    tr = _pick(rows, (256, 128))
    nb = rows // tr
    if kind == "col":
        g_spec = pl.BlockSpec((tr, cols), lambda i, c_ref: (c_ref[0] * nb + i, 0))
    else:
        g_spec = pl.BlockSpec((tr, cols), lambda i, c_ref: (i, c_ref[0]))
    t_spec = pl.BlockSpec((tr, cols), lambda i, c_ref: (i, 0))

    def body(c_ref, g_ref, t_ref, o32_ref, o16_ref):
        tot = g_ref[...] + t_ref[...]
        o32_ref[...] = tot
        o16_ref[...] = tot.astype(BF16)

    return pl.pallas_call(
        body,
        out_shape=[jax.ShapeDtypeStruct((rows, cols), F32), jax.ShapeDtypeStruct((rows, cols), BF16)],
        grid_spec=pltpu.PrefetchScalarGridSpec(num_scalar_prefetch=1, grid=(nb,), in_specs=[g_spec, t_spec],
                                               out_specs=[t_spec, t_spec]),
        compiler_params=_params("parallel"),
        name=name,
    )(c_arr, grad, theirs)


def _exchange_chips(parts, kinds):
    n = len(parts)

    def body(*refs):
        ins, outs = refs[:n], refs[n:2 * n]
        send_sems, recv_sems = refs[2 * n:]
        x, y, c, chips = _position()
        sends = []
        for t in range(n):
            for k, (cx, cy) in enumerate(chips):
                src = _shard_of(ins[t], kinds[t], 2 * cx + cy, N_CHIPS)
                sends.append(_remote(src, outs[t].at[k], send_sems, recv_sems, 3 * t + k, (cx, cy, c)))
        for cp in sends:
            cp.start()
        for t in range(n):
            for k, (cx, cy) in enumerate(chips):
                src = _shard_of(ins[t], kinds[t], 2 * cx + cy, N_CHIPS)
                _remote(src, outs[t].at[k], send_sems, recv_sems, 3 * t + k, (cx, cy, c)).wait_recv()
        for cp in sends:
            cp.wait_send()

    def piece(p, kind):
        rows, cols = p.shape
        return (3, rows, cols // N_CHIPS) if kind == "col" else (3, rows // N_CHIPS, cols)

    return pl.pallas_call(
        body,
        out_shape=[jax.ShapeDtypeStruct(piece(p, k), BF16) for p, k in zip(parts, kinds)],
        in_specs=[ANY] * n,
        out_specs=[ANY] * n,
        scratch_shapes=[pltpu.SemaphoreType.DMA((3 * n,)), pltpu.SemaphoreType.DMA((3 * n,))],
        name="reduce_chips",
    )(*parts)


def _shard_sum(part32, recv, kind, sc_arr, *, name):
    _, rows, cols = recv.shape
    tr = _pick(rows, (256, 128))
    nb = rows // tr
    if kind == "col":
        whole = (2 * rows, cols)
        p_spec = pl.BlockSpec((tr, cols), lambda i, sc: (i, sc[0]))
        o_spec = pl.BlockSpec((tr, cols), lambda i, sc: (sc[1] * nb + i, 0))
    else:
        whole = (rows, 2 * cols)
        p_spec = pl.BlockSpec((tr, cols), lambda i, sc: (sc[0] * nb + i, 0))
        o_spec = pl.BlockSpec((tr, cols), lambda i, sc: (i, sc[1]))
    r_spec = pl.BlockSpec((3, tr, cols), lambda i, sc: (0, i, 0))

    def body(sc_ref, p_ref, r_ref, o_ref):
        acc = p_ref[...]
        for k in range(3):
            acc = acc + r_ref[k].astype(F32)
        o_ref[...] = acc

    return pl.pallas_call(
        body,
        out_shape=jax.ShapeDtypeStruct(whole, F32),
        grid_spec=pltpu.PrefetchScalarGridSpec(num_scalar_prefetch=1, grid=(nb,), in_specs=[p_spec, r_spec],
                                               out_specs=o_spec),
        compiler_params=_params("parallel"),
        name=name,
    )(sc_arr, part32, recv)


def _swap_halves_out(shards, kinds):
    n = len(shards)

    def body(*refs):
        ins, outs = refs[:n], refs[n:2 * n]
        send_sems, recv_sems = refs[2 * n:]
        x, y, c, _ = _position()
        sibling = (x, y, 1 - c)
        sends = [_remote(_half_of(ins[t], kinds[t], c), _half_of(outs[t], kinds[t], c), send_sems, recv_sems, t, sibling)
                 for t in range(n)]
        for cp in sends:
            cp.start()
        for t in range(n):
            theirs = _half_of(outs[t], kinds[t], 1 - c)
            _remote(theirs, theirs, send_sems, recv_sems, t, sibling).wait_recv()
        for cp in sends:
            cp.wait_send()

    return pl.pallas_call(
        body,
        out_shape=[jax.ShapeDtypeStruct(s.shape, F32) for s in shards],
        in_specs=[ANY] * n,
        out_specs=[ANY] * n,
        input_output_aliases={t: t for t in range(n)},
        scratch_shapes=[pltpu.SemaphoreType.DMA((n,)), pltpu.SemaphoreType.DMA((n,))],
        name="gather_core_pair",
    )(*shards)


def _adamw(w, g, m, v, *, name):
    rows, cols = w.shape
    tr = _pick(rows, (256, 128)) if rows * cols > 64 * 1024 else rows

    def body(w_ref, g_ref, m_ref, v_ref, d_ref, nm_ref, nv_ref):
        gv = g_ref[...]
        nm = ADAM_B1 * m_ref[...] + (1.0 - ADAM_B1) * gv
        nv = ADAM_B2 * v_ref[...] + (1.0 - ADAM_B2) * jnp.square(gv)
        m_hat = nm / (1.0 - ADAM_B1 ** ADAM_STEP)
        v_hat = nv / (1.0 - ADAM_B2 ** ADAM_STEP)
        d_ref[...] = -ADAM_LR * (m_hat / (jnp.sqrt(v_hat) + ADAM_EPS) + ADAM_WD * w_ref[...])
        nm_ref[...] = nm
        nv_ref[...] = nv

    spec = pl.BlockSpec((tr, cols), lambda i: (i, 0))
    return pl.pallas_call(
        body,
        out_shape=[jax.ShapeDtypeStruct((rows, cols), F32)] * 3,
        grid=(rows // tr,),
        in_specs=[spec] * 4,
        out_specs=[spec] * 3,
        compiler_params=_params("parallel"),
        name=name,
    )(w, g, m, v)


BIG = ("even_w_in", "odd_w_qkv", "mlp_w1_0", "mlp_w1_1", "even_w_out", "odd_w_o", "mlp_w2_0", "mlp_w2_1")
BIG_KIND = ("col", "col", "col", "col", "row", "row", "row", "row")
SUBLANES = 8


def _pack_rows(parts, width):
    padded, offsets, r0 = [], [], 0
    for t in parts:
        rows = -(-t.shape[0] // SUBLANES) * SUBLANES
        padded.append(jnp.pad(t, ((0, rows - t.shape[0]), (0, width - t.shape[1]))))
        offsets.append(r0)
        r0 += rows
    return jnp.concatenate(padded, axis=0), offsets


def kernel(x, meta, norm_mix_g, norm_mlp_g, even_w_in, even_ret_gn_g, even_conv_w, even_conv_b, even_conv_ln_g, even_conv_ln_b, even_w_out, odd_w_qkv, odd_q_norm_g, odd_k_norm_g, odd_w_o, mlp_w1, mlp_w2, loss_target, m_meta, m_norm_mix_g, m_norm_mlp_g, m_even_w_in, m_even_ret_gn_g, m_even_conv_w, m_even_conv_b, m_even_conv_ln_g, m_even_conv_ln_b, m_even_w_out, m_odd_w_qkv, m_odd_q_norm_g, m_odd_k_norm_g, m_odd_w_o, m_mlp_w1, m_mlp_w2, v_meta, v_norm_mix_g, v_norm_mlp_g, v_even_w_in, v_even_ret_gn_g, v_even_conv_w, v_even_conv_b, v_even_conv_ln_g, v_even_conv_ln_b, v_even_w_out, v_odd_w_qkv, v_odd_q_norm_g, v_odd_k_norm_g, v_odd_w_o, v_mlp_w1, v_mlp_w2):
    d = D_MODEL
    xi, yi, ci = lax.axis_index("x"), lax.axis_index("y"), lax.axis_index("c")
    chip = 2 * xi + yi
    c_arr = jnp.reshape(ci, (1,)).astype(jnp.int32)
    s_arr = jnp.reshape(chip, (1,)).astype(jnp.int32)

    def split_big(w_in, w_qkv, w1, w_out, w_o, w2):
        return dict(zip(BIG, (w_in[0], w_qkv[0], w1[0], w1[1], w_out[0], w_o[0], w2[0], w2[1])))

    w_big = split_big(even_w_in, odd_w_qkv, mlp_w1, even_w_out, odd_w_o, mlp_w2)
    m_big = split_big(m_even_w_in, m_odd_w_qkv, m_mlp_w1, m_even_w_out, m_odd_w_o, m_mlp_w2)
    v_big = split_big(v_even_w_in, v_odd_w_qkv, v_mlp_w1, v_even_w_out, v_odd_w_o, v_mlp_w2)

    placed = [_cast_into_whole(w_big[n], k, s_arr, name="cast_" + n) for n, k in zip(BIG, BIG_KIND)]
    full = dict(zip(BIG, _allgather_weights(placed, BIG_KIND)))

    packed, (r_meta, r_conv, r_gn) = _pack_rows([meta, even_conv_w[0], even_ret_gn_g[0]], d // N_CHIPS)
    gathered = _allgather8(packed, name="allgather_small_params")[0::2]
    across = lambda r0, rows, width: jnp.concatenate([gathered[s, r0:r0 + rows, 0:width] for s in range(N_CHIPS)], axis=1)
    meta_full = across(r_meta, N_META, d // N_CHIPS)
    conv_w_full = across(r_conv, CONV_WIDTH, d // N_CHIPS)
    gn_full = across(r_gn, RET_HEADS, RET_V_DIM // N_CHIPS)

    sq, g = _local_step(
        x[0], loss_target[0], meta_full, norm_mix_g, norm_mlp_g, full["even_w_in"], gn_full, conv_w_full,
        even_conv_b[0], even_conv_ln_g[0], even_conv_ln_b[0], full["even_w_out"], full["odd_w_qkv"],
        odd_q_norm_g[0], odd_k_norm_g[0], full["odd_w_o"], (full["mlp_w1_0"], full["mlp_w1_1"]),
        (full["mlp_w2_0"], full["mlp_w2_1"]))
    loss = lax.psum(0.5 * sq / d, ("x", "y", "c"))

    small_names = ("norm_mix_g", "norm_mlp_g", "even_conv_b", "even_conv_ln_g", "even_conv_ln_b", "odd_q_norm_g",
                   "odd_k_norm_g", "meta", "even_conv_w", "even_ret_gn_g")
    pack, offsets = _pack_rows([g[n] for n in small_names], d)
    summed = _sum8(_allgather8(pack, name="allgather_small_grads"), name="sum_small_grads")
    small = {n: summed[r0:r0 + g[n].shape[0], 0:g[n].shape[1]] for n, r0 in zip(small_names, offsets)}
    for n in ("meta", "even_conv_w", "even_ret_gn_g"):
        width = small[n].shape[1] // N_CHIPS
        small[n] = lax.dynamic_slice_in_dim(small[n], chip * width, width, axis=1)

    g_big = [g["even_w_in"], g["odd_w_qkv"], g["mlp_w1"][0], g["mlp_w1"][1], g["even_w_out"], g["odd_w_o"],
             g["mlp_w2"][0], g["mlp_w2"][1]]
    theirs = _swap_halves_in(g_big, BIG_KIND)
    sums = [_half_add(gb, th, k, c_arr, name="pair_sum_" + n) for gb, th, k, n in zip(g_big, theirs, BIG_KIND, BIG)]
    recv = _exchange_chips([s16 for _, s16 in sums], BIG_KIND)
    sc_arr = jnp.concatenate([s_arr, c_arr])
    halves = [_shard_sum(s32, r, k, sc_arr, name="chip_sum_" + n) for (s32, _), r, k, n in zip(sums, recv, BIG_KIND, BIG)]
    grad_big = dict(zip(BIG, _swap_halves_out(halves, BIG_KIND)))

    upd = {n: _adamw(w_big[n], grad_big[n], m_big[n], v_big[n], name="adamw_" + n) for n in BIG}

    def join(name, idx, lead):
        if name in ("mlp_w1", "mlp_w2"):
            return jnp.stack([upd[name + "_0"][idx], upd[name + "_1"][idx]]) if idx >= 0 else jnp.stack(
                [grad_big[name + "_0"], grad_big[name + "_1"]])
        t = upd[name][idx] if idx >= 0 else grad_big[name]
        return t[None] if lead else t

    small_w = dict(meta=meta, norm_mix_g=norm_mix_g, norm_mlp_g=norm_mlp_g, even_ret_gn_g=even_ret_gn_g[0],
                   even_conv_w=even_conv_w[0], even_conv_b=even_conv_b, even_conv_ln_g=even_conv_ln_g,
                   even_conv_ln_b=even_conv_ln_b, odd_q_norm_g=odd_q_norm_g, odd_k_norm_g=odd_k_norm_g)
    small_m = dict(meta=m_meta, norm_mix_g=m_norm_mix_g, norm_mlp_g=m_norm_mlp_g, even_ret_gn_g=m_even_ret_gn_g[0],
                   even_conv_w=m_even_conv_w[0], even_conv_b=m_even_conv_b, even_conv_ln_g=m_even_conv_ln_g,
                   even_conv_ln_b=m_even_conv_ln_b, odd_q_norm_g=m_odd_q_norm_g, odd_k_norm_g=m_odd_k_norm_g)
    small_v = dict(meta=v_meta, norm_mix_g=v_norm_mix_g, norm_mlp_g=v_norm_mlp_g, even_ret_gn_g=v_even_ret_gn_g[0],
                   even_conv_w=v_even_conv_w[0], even_conv_b=v_even_conv_b, even_conv_ln_g=v_even_conv_ln_g,
                   even_conv_ln_b=v_even_conv_ln_b, odd_q_norm_g=v_odd_q_norm_g, odd_k_norm_g=v_odd_k_norm_g)
    small_upd = {n: _adamw(small_w[n], small[n], small_m[n], small_v[n], name="adamw_" + n) for n in small_w}
    leading = ("even_ret_gn_g", "even_conv_w")

    order = ("meta", "norm_mix_g", "norm_mlp_g", "even_w_in", "even_ret_gn_g", "even_conv_w", "even_conv_b",
             "even_conv_ln_g", "even_conv_ln_b", "even_w_out", "odd_w_qkv", "odd_q_norm_g", "odd_k_norm_g", "odd_w_o",
             "mlp_w1", "mlp_w2")
    big_lead = ("even_w_in", "even_w_out", "odd_w_qkv", "odd_w_o")

    def leaf(name, idx):
        if name in small_w:
            t = small_upd[name][idx] if idx >= 0 else small[name]
            return t[None] if name in leading else t
        return join(name, idx, name in big_lead)

    outs = [loss, g["x"][None]]
    for idx in (-1, 0, 1, 2):
        outs += [leaf(n, idx) for n in order]
    return tuple(outs)
```

```python
import functools

import jax
import jax.numpy as jnp
from jax import lax
from jax.experimental import pallas as pl
from jax.experimental.pallas import tpu as pltpu

F32 = jnp.float32
BF16 = jnp.bfloat16

D_MODEL = 1024
N_META = 16
CHUNK = 128
PAD_FRONT = (-N_META) % CHUNK
RET_HEADS = 4
RET_QK_DIM = 128
RET_V_DIM = 256
RET_QK_W = RET_HEADS * RET_QK_DIM
RET_V_W = RET_HEADS * RET_V_DIM
CONV_WIDTH = 31
CONV_HALO = 32
RET_DECAY_OFFSET = 5.0
ROPE_BASE = 10000.0
SB_HEADS = 16
SB_HEAD_DIM = 64
D_FF = 4 * D_MODEL
EPS = 1e-6
ADAM_LR = 0.001
ADAM_B1 = 0.9
ADAM_B2 = 0.999
ADAM_EPS = 1e-08
ADAM_WD = 0.01
ADAM_STEP = 10

N_CHIPS = 4
N_DEV = 8
VMEM_LIMIT = 56 * 1024 * 1024
MESH = pl.DeviceIdType.MESH
ANY = pl.BlockSpec(memory_space=pl.ANY)


def _params(*sem):
    return pltpu.CompilerParams(dimension_semantics=sem, vmem_limit_bytes=VMEM_LIMIT)


def _pick(n, cands):
    for c in cands:
        if n % c == 0:
            return c
    return n


def _sigmoid(x):
    return 1.0 / (1.0 + jnp.exp(-x))


def _dot(a, b):
    return lax.dot_general(a, b, (((1,), (0,)), ((), ())), preferred_element_type=F32)


def _dot_nt(a, b):
    return lax.dot_general(a, b, (((1,), (1,)), ((), ())), preferred_element_type=F32)


def _dot_tn(a, b):
    return lax.dot_general(a, b, (((0,), (0,)), ((), ())), preferred_element_type=F32)


def _split_dot(x, m):
    hi = x.astype(BF16)
    lo = (x - hi.astype(F32)).astype(BF16)
    return _dot(hi, m) + _dot(lo, m)


def _matmul(a, b, *, mode, out_dtypes, epilogue=None, extras=(), name):
    if mode == "nn":
        (m, k), (k2, n) = a.shape, b.shape
    elif mode == "nt":
        (m, k), (n, k2) = a.shape, b.shape
    else:
        (k, m), (k2, n) = a.shape, b.shape
    assert k == k2, (a.shape, b.shape, mode)
    tm = _pick(m, (1056, 1024, 768, 512, 384, 256, 128, 96))
    tn = _pick(n, (1024, 768, 512, 256, 128))
    tk = _pick(k, (1056, 1024, 768, 512, 384, 256, 128, 96))
    nk = k // tk
    dot = {"nn": _dot, "nt": _dot_nt, "tn": _dot_tn}[mode]
    n_extra, n_out = len(extras), len(out_dtypes)
    if epilogue is None:
        epilogue = lambda acc: (acc,)

    def body(a_ref, b_ref, *rest):
        extra_refs = rest[:n_extra]
        out_refs = rest[n_extra:n_extra + n_out]
        part = dot(a_ref[...].astype(BF16), b_ref[...].astype(BF16))

        def finish(acc):
            res = epilogue(acc, *[r[...] for r in extra_refs])
            for o_ref, r in zip(out_refs, res):
                o_ref[...] = r.astype(o_ref.dtype)

        if nk == 1:
            finish(part)
        else:
            acc_ref = rest[-1]
            kk = pl.program_id(2)

            @pl.when(kk == 0)
            def _():
                acc_ref[...] = part

            @pl.when(kk > 0)
            def _():
                acc_ref[...] += part

            @pl.when(kk == nk - 1)
            def _():
                finish(acc_ref[...])

    if mode == "nn":
        a_spec = pl.BlockSpec((tm, tk), lambda i, j, kk: (i, kk))
        b_spec = pl.BlockSpec((tk, tn), lambda i, j, kk: (kk, j))
    elif mode == "nt":
        a_spec = pl.BlockSpec((tm, tk), lambda i, j, kk: (i, kk))
        b_spec = pl.BlockSpec((tn, tk), lambda i, j, kk: (j, kk))
    else:
        a_spec = pl.BlockSpec((tk, tm), lambda i, j, kk: (kk, i))
        b_spec = pl.BlockSpec((tk, tn), lambda i, j, kk: (kk, j))
    o_spec = pl.BlockSpec((tm, tn), lambda i, j, kk: (i, j))
    outs = pl.pallas_call(
        body,
        out_shape=[jax.ShapeDtypeStruct((m, n), dt) for dt in out_dtypes],
        grid=(m // tm, n // tn, nk),
        in_specs=[a_spec, b_spec] + [o_spec] * n_extra,
        out_specs=[o_spec] * n_out,
        scratch_shapes=[pltpu.VMEM((tm, tn), F32)] if nk > 1 else [],
        compiler_params=_params("parallel", "parallel", "arbitrary"),
        name=name,
    )(a, b, *extras)
    return outs[0] if n_out == 1 else outs


def _add_epilogue(acc, res):
    return (res + acc,)


def _rmsnorm_fwd(x, g, *, name):
    p, d = x.shape
    rows = _pick(p, (384, 128, 96))

    def body(x_ref, g_ref, o_ref):
        xv = x_ref[...]
        r = lax.rsqrt(jnp.mean(xv * xv, axis=-1, keepdims=True) + EPS)
        o_ref[...] = (xv * r * g_ref[...]).astype(o_ref.dtype)

    return pl.pallas_call(
        body,
        out_shape=jax.ShapeDtypeStruct((p, d), BF16),
        grid=(p // rows,),
        in_specs=[pl.BlockSpec((rows, d), lambda i: (i, 0)), pl.BlockSpec((1, d), lambda i: (0, 0))],
        out_specs=pl.BlockSpec((rows, d), lambda i: (i, 0)),
        compiler_params=_params("parallel"),
        name=name,
    )(x, g)


def _rmsnorm_bwd(x, g, dy, dres, *, name):
    p, d = x.shape
    rows = _pick(p, (384, 128, 96))

    def body(x_ref, g_ref, dy_ref, dres_ref, dx_ref, dg_ref):
        xv = x_ref[...]
        r = lax.rsqrt(jnp.mean(xv * xv, axis=-1, keepdims=True) + EPS)
        dyv = dy_ref[...]
        gdy = dyv * g_ref[...]
        proj = jnp.mean(xv * gdy, axis=-1, keepdims=True)
        dx_ref[...] = dres_ref[...] + r * gdy - xv * (r * r * r) * proj
        part = jnp.sum(dyv * xv * r, axis=0, keepdims=True)

        @pl.when(pl.program_id(0) == 0)
        def _():
            dg_ref[...] = part

        @pl.when(pl.program_id(0) > 0)
        def _():
            dg_ref[...] += part

    row_spec = pl.BlockSpec((rows, d), lambda i: (i, 0))
    vec_spec = pl.BlockSpec((1, d), lambda i: (0, 0))
    return pl.pallas_call(
        body,
        out_shape=[jax.ShapeDtypeStruct((p, d), F32), jax.ShapeDtypeStruct((1, d), F32)],
        grid=(p // rows,),
        in_specs=[row_spec, vec_spec, row_spec, row_spec],
        out_specs=[row_spec, vec_spec],
        compiler_params=_params("arbitrary"),
        name=name,
    )(x, g, dy, dres)


def _mlp_fwd(h, g, w1, w2, *, name):
    hn = _rmsnorm_fwd(h, g, name=name + "_norm")

    def act(acc):
        r = jnp.maximum(acc, 0.0)
        return acc, r * r

    z, a2 = _matmul(hn, w1, mode="nn", out_dtypes=(F32, BF16), epilogue=act, name=name + "_up")
    out = _matmul(a2, w2, mode="nn", out_dtypes=(F32,), epilogue=_add_epilogue, extras=(h,), name=name + "_down")
    return out, (hn, z, a2)


def _mlp_bwd(h, g, w1, w2, saved, dout, *, name):
    hn, z, a2 = saved

    def dact(acc, zt):
        return (acc * (2.0 * jnp.maximum(zt, 0.0)),)

    dz = _matmul(dout, w2, mode="nt", out_dtypes=(BF16,), epilogue=dact, extras=(z,), name=name + "_dz")
    dw2 = _matmul(a2, dout, mode="tn", out_dtypes=(F32,), name=name + "_dw2")
    dw1 = _matmul(hn, dz, mode="tn", out_dtypes=(F32,), name=name + "_dw1")
    dhn = _matmul(dz, w1, mode="nt", out_dtypes=(F32,), name=name + "_dhn")
    dh, dg = _rmsnorm_bwd(h, g, dhn, dout, name=name + "_dnorm")
    return dh, dg, dw1, dw2


def _retention_tables(p):
    half = RET_QK_DIM // 2
    inv_freq = ROPE_BASE ** (-jnp.arange(half, dtype=F32) / half)
    ang = jnp.arange(p, dtype=F32)[:, None] * inv_freq[None, :]
    cos, sin = jnp.cos(ang), jnp.sin(ang)
    cosf = jnp.concatenate([cos, cos], axis=1)
    sins = jnp.concatenate([-sin, sin], axis=1)
    log_g = jnp.log1p(-jnp.exp2(-RET_DECAY_OFFSET - jnp.arange(RET_HEADS, dtype=F32)))
    idx = jnp.arange(CHUNK, dtype=F32)
    diff = idx[:, None] - idx[None, :]
    inner = jnp.where(diff[None] >= 0, jnp.exp(jnp.maximum(diff, 0.0)[None] * log_g[:, None, None]), 0.0)
    kdec = jnp.exp((CHUNK - 1 - idx)[None, :] * log_g[:, None])
    qdec = jnp.exp((idx + 1.0)[None, :] * log_g[:, None])
    cdec = jnp.exp(CHUNK * log_g)
    kdec = jnp.broadcast_to(kdec[:, :, None], (RET_HEADS, CHUNK, RET_QK_DIM))
    qdec = jnp.broadcast_to(qdec[:, :, None], (RET_HEADS, CHUNK, RET_QK_DIM))
    cdec = jnp.broadcast_to(cdec[:, None, None], (RET_HEADS, RET_QK_DIM, RET_V_DIM))
    return cosf, sins, inner, kdec, qdec, cdec


def _rot(x, cosf, sins):
    return x * cosf + pltpu.roll(x, RET_QK_DIM // 2, 1) * sins


def _rot_bwd(dy, cosf, sins):
    return dy * cosf + pltpu.roll(dy * sins, RET_QK_DIM // 2, 1)


def _ret_in_specs(chunk_of):
    nh = RET_HEADS
    q_spec = pl.BlockSpec((CHUNK, RET_QK_DIM), lambda h, s: (chunk_of(s), h))
    k_spec = pl.BlockSpec((CHUNK, RET_QK_DIM), lambda h, s: (chunk_of(s), nh + h))
    v_spec = pl.BlockSpec((CHUNK, RET_V_DIM), lambda h, s: (chunk_of(s), nh + h))
    g_spec = pl.BlockSpec((CHUNK, RET_V_DIM), lambda h, s: (chunk_of(s), 2 * nh + h))
    rope_spec = pl.BlockSpec((CHUNK, RET_QK_DIM), lambda h, s: (chunk_of(s), 0))
    head_sq = pl.BlockSpec((None, CHUNK, CHUNK), lambda h, s: (h, 0, 0))
    head_qk = pl.BlockSpec((None, CHUNK, RET_QK_DIM), lambda h, s: (h, 0, 0))
    head_st = pl.BlockSpec((None, RET_QK_DIM, RET_V_DIM), lambda h, s: (h, 0, 0))
    gam_spec = pl.BlockSpec((None, 1, RET_V_DIM), lambda h, s: (h, 0, 0))
    return [q_spec, k_spec, v_spec, g_spec, rope_spec, rope_spec, head_sq, head_qk, head_qk, head_st, gam_spec]


def _retention_fwd(proj, gn_g, tables, *, name):
    p = proj.shape[0]
    n_chunks = p // CHUNK
    scale = RET_QK_DIM ** -0.5

    def body(q_ref, k_ref, v_ref, g_ref, cos_ref, sin_ref, inner_ref, kdec_ref, qdec_ref, cdec_ref, gam_ref,
             og_ref, opre_ref, sprev_ref, s_scr):
        @pl.when(pl.program_id(1) == 0)
        def _():
            s_scr[...] = jnp.zeros_like(s_scr)

        cosf, sins = cos_ref[...], sin_ref[...]
        qr = _rot(q_ref[...], cosf, sins)
        kr = _rot(k_ref[...], cosf, sins) * scale
        vb = v_ref[...].astype(BF16)
        scores = _dot_nt(qr.astype(BF16), kr.astype(BF16)) * inner_ref[...]
        state = s_scr[...]
        sprev_ref[...] = state
        o = _dot(scores.astype(BF16), vb) + _dot((qr * qdec_ref[...]).astype(BF16), state.astype(BF16))
        kd = kr * kdec_ref[...]
        s_scr[...] = cdec_ref[...] * state + _dot(kd.T.astype(BF16), vb)
        opre_ref[...] = o
        mu = jnp.mean(o, axis=-1, keepdims=True)
        oc = o - mu
        var = jnp.mean(oc * oc, axis=-1, keepdims=True)
        on = oc * lax.rsqrt(var + EPS) * gam_ref[...]
        gv = g_ref[...]
        og_ref[...] = (gv * _sigmoid(gv) * on).astype(og_ref.dtype)

    chunk_of = lambda s: s
    out_v = pl.BlockSpec((CHUNK, RET_V_DIM), lambda h, s: (s, h))
    return pl.pallas_call(
        body,
        out_shape=[
            jax.ShapeDtypeStruct((p, RET_V_W), BF16),
            jax.ShapeDtypeStruct((p, RET_V_W), F32),
            jax.ShapeDtypeStruct((RET_HEADS, n_chunks, RET_QK_DIM, RET_V_DIM), F32),
        ],
        grid=(RET_HEADS, n_chunks),
        in_specs=_ret_in_specs(chunk_of),
        out_specs=[out_v, out_v, pl.BlockSpec((None, None, RET_QK_DIM, RET_V_DIM), lambda h, s: (h, s, 0, 0))],
        scratch_shapes=[pltpu.VMEM((RET_QK_DIM, RET_V_DIM), F32)],
        compiler_params=_params("parallel", "arbitrary"),
        name=name,
    )(proj, proj, proj, proj, *tables, gn_g.reshape(RET_HEADS, 1, RET_V_DIM))


def _retention_bwd(proj, gn_g, tables, opre, sprev, dog, *, name):
    p = proj.shape[0]
    n_chunks = p // CHUNK
    scale = RET_QK_DIM ** -0.5

    def body(q_ref, k_ref, v_ref, g_ref, cos_ref, sin_ref, inner_ref, kdec_ref, qdec_ref, cdec_ref, gam_ref,
             opre_ref, sprev_ref, dog_ref, dq_ref, dk_ref, dv_ref, dg_ref, dgam_ref, ds_scr):
        first = pl.program_id(1) == 0

        @pl.when(first)
        def _():
            ds_scr[...] = jnp.zeros_like(ds_scr)

        cosf, sins = cos_ref[...], sin_ref[...]
        qr = _rot(q_ref[...], cosf, sins)
        kr = _rot(k_ref[...], cosf, sins) * scale
        qb, kb = qr.astype(BF16), kr.astype(BF16)
        vb = v_ref[...].astype(BF16)
        inner = inner_ref[...]
        qdec, kdec = qdec_ref[...], kdec_ref[...]
        state_b = sprev_ref[...].astype(BF16)
        o = opre_ref[...]
        mu = jnp.mean(o, axis=-1, keepdims=True)
        oc = o - mu
        rstd = lax.rsqrt(jnp.mean(oc * oc, axis=-1, keepdims=True) + EPS)
        xhat = oc * rstd
        gam = gam_ref[...]
        on = xhat * gam
        gv = g_ref[...]
        sig = _sigmoid(gv)
        dogv = dog_ref[...]
        dg_ref[...] = (dogv * on * sig * (1.0 + gv * (1.0 - sig))).astype(dg_ref.dtype)
        don = dogv * gv * sig
        dgam_part = jnp.sum(don * xhat, axis=0, keepdims=True)

        @pl.when(first)
        def _():
            dgam_ref[...] = dgam_part

        @pl.when(jnp.logical_not(first))
        def _():
            dgam_ref[...] += dgam_part

        dxhat = don * gam
        do = rstd * (dxhat - jnp.mean(dxhat, axis=-1, keepdims=True)
                     - xhat * jnp.mean(dxhat * xhat, axis=-1, keepdims=True))
        dob = do.astype(BF16)
        scores_b = (_dot_nt(qb, kb) * inner).astype(BF16)
        da = (_dot_nt(dob, vb) * inner).astype(BF16)
        dv = _dot(scores_b.astype(F32).T.astype(BF16), dob)
        dqr = _dot(da, kb)
        dkr = _dot(da.astype(F32).T.astype(BF16), qb)
        dqr += _dot_nt(dob, state_b) * qdec
        ds_local = _dot((qr * qdec).T.astype(BF16), dob)
        gstate = ds_scr[...]
        gb = gstate.astype(BF16)
        kd_b = (kr * kdec).astype(BF16)
        dkr += _dot_nt(vb, gb) * kdec
        dv += _dot(kd_b, gb)
        ds_scr[...] = cdec_ref[...] * gstate + ds_local
        dq_ref[...] = _rot_bwd(dqr, cosf, sins).astype(dq_ref.dtype)
        dk_ref[...] = _rot_bwd(dkr * scale, cosf, sins).astype(dk_ref.dtype)
        dv_ref[...] = dv.astype(dv_ref.dtype)

    chunk_of = lambda s: n_chunks - 1 - s
    blk_v = pl.BlockSpec((CHUNK, RET_V_DIM), lambda h, s: (chunk_of(s), h))
    blk_qk = pl.BlockSpec((CHUNK, RET_QK_DIM), lambda h, s: (chunk_of(s), h))
    st_spec = pl.BlockSpec((None, None, RET_QK_DIM, RET_V_DIM), lambda h, s: (h, chunk_of(s), 0, 0))
    return pl.pallas_call(
        body,
        out_shape=[
            jax.ShapeDtypeStruct((p, RET_QK_W), BF16),
            jax.ShapeDtypeStruct((p, RET_QK_W), BF16),
            jax.ShapeDtypeStruct((p, RET_V_W), BF16),
            jax.ShapeDtypeStruct((p, RET_V_W), BF16),
            jax.ShapeDtypeStruct((RET_HEADS, 1, RET_V_DIM), F32),
        ],
        grid=(RET_HEADS, n_chunks),
        in_specs=_ret_in_specs(chunk_of) + [blk_v, st_spec, blk_v],
        out_specs=[blk_qk, blk_qk, blk_v, blk_v, pl.BlockSpec((None, 1, RET_V_DIM), lambda h, s: (h, 0, 0))],
        scratch_shapes=[pltpu.VMEM((RET_QK_DIM, RET_V_DIM), F32)],
        compiler_params=_params("parallel", "arbitrary"),
        name=name,
    )(proj, proj, proj, proj, *tables, gn_g.reshape(RET_HEADS, 1, RET_V_DIM), opre, sprev, dog)


def _conv_rows(p):
    return _pick(p, (384, 128))


def _ln_stats(y):
    mu = jnp.mean(y, axis=-1, keepdims=True)
    yc = y - mu
    rstd = lax.rsqrt(jnp.mean(yc * yc, axis=-1, keepdims=True) + EPS)
    return yc * rstd, rstd


def _conv_fwd(proj, conv_w, conv_b, ln_g, ln_b, *, name):
    p = proj.shape[0]
    c = D_MODEL
    rows = _conv_rows(p)
    hpb = rows // CONV_HALO
    a_col, gate_col = (2 * RET_QK_W + 2 * RET_V_W) // c, (2 * RET_QK_W + 2 * RET_V_W) // c + 1

    def body(a_ref, gate_ref, ah_ref, gateh_ref, w_ref, b_ref, lg_ref, lb_ref, c_ref, y_ref, hdn_scr):
        i = pl.program_id(0)
        hdn_scr[0:CONV_HALO, :] = ah_ref[...] * _sigmoid(gateh_ref[...])
        hdn_scr[CONV_HALO:, :] = a_ref[...] * _sigmoid(gate_ref[...])
        acc = jnp.zeros((rows, c), F32)
        for w in range(CONV_WIDTH):
            off = CONV_HALO - (CONV_WIDTH - 1) + w
            acc += hdn_scr[off:off + rows, :] * w_ref[w:w + 1, :]
        y = acc + b_ref[...]
        y_ref[...] = y
        yhat, _ = _ln_stats(y)
        ln = yhat * lg_ref[...] + lb_ref[...]
        row = i * rows + lax.broadcasted_iota(jnp.int32, (rows, 1), 0)
        c_ref[...] = jnp.where(row >= PAD_FRONT, ln * _sigmoid(ln), 0.0).astype(c_ref.dtype)

    halo_idx = lambda i: jnp.maximum(i * hpb - 1, 0)
    vec = pl.BlockSpec((1, c), lambda i: (0, 0))
    return pl.pallas_call(
        body,
        out_shape=[jax.ShapeDtypeStruct((p, c), BF16), jax.ShapeDtypeStruct((p, c), F32)],
        grid=(p // rows,),
        in_specs=[
            pl.BlockSpec((rows, c), lambda i: (i, a_col)),
            pl.BlockSpec((rows, c), lambda i: (i, gate_col)),
            pl.BlockSpec((CONV_HALO, c), lambda i: (halo_idx(i), a_col)),
            pl.BlockSpec((CONV_HALO, c), lambda i: (halo_idx(i), gate_col)),
            pl.BlockSpec((CONV_WIDTH, c), lambda i: (0, 0)),
            vec, vec, vec,
        ],
        out_specs=[pl.BlockSpec((rows, c), lambda i: (i, 0)), pl.BlockSpec((rows, c), lambda i: (i, 0))],
        scratch_shapes=[pltpu.VMEM((CONV_HALO + rows, c), F32)],
        compiler_params=_params("parallel"),
        name=name,
    )(proj, proj, proj, proj, conv_w, conv_b, ln_g, ln_b)


def _conv_bwd(proj, conv_w, ln_g, ln_b, y, dcat, *, name):
    p = proj.shape[0]
    c = D_MODEL
    rows = _conv_rows(p)
    hpb = rows // CONV_HALO
    n_blocks = p // rows
    a_col, gate_col = (2 * RET_QK_W + 2 * RET_V_W) // c, (2 * RET_QK_W + 2 * RET_V_W) // c + 1

    def body(a_ref, gate_ref, ah_ref, gateh_ref, w_ref, lg_ref, lb_ref, y_ref, yh_ref, dc_ref, dch_ref,
             da_ref, dgate_ref, dw_ref, db_ref, dlg_ref, dlb_ref, hdn_scr, dy_scr):
        i = pl.program_id(0)
        lg, lb = lg_ref[...], lb_ref[...]

        def ln_bwd(yv, dcv):
            yhat, rstd = _ln_stats(yv)
            ln = yhat * lg + lb
            sig = _sigmoid(ln)
            dln = dcv * sig * (1.0 + ln * (1.0 - sig))
            dyhat = dln * lg
            dyv = rstd * (dyhat - jnp.mean(dyhat, axis=-1, keepdims=True)
                          - yhat * jnp.mean(dyhat * yhat, axis=-1, keepdims=True))
            return dyv, dln, yhat

        row = i * rows + lax.broadcasted_iota(jnp.int32, (rows, 1), 0)
        dy, dln, yhat = ln_bwd(y_ref[...], jnp.where(row >= PAD_FRONT, dc_ref[...], 0.0))
        dy_halo, _, _ = ln_bwd(yh_ref[...], dch_ref[...])
        dy_scr[0:rows, :] = dy
        dy_scr[rows:, :] = jnp.where(i == n_blocks - 1, 0.0, dy_halo)
        sig_gate = _sigmoid(gate_ref[...])
        av = a_ref[...]
        hdn_scr[0:CONV_HALO, :] = ah_ref[...] * _sigmoid(gateh_ref[...])
        hdn_scr[CONV_HALO:, :] = av * sig_gate
        @pl.when(i == 0)
        def _():
            dw_ref[...] = jnp.zeros_like(dw_ref)
            db_ref[...] = jnp.zeros_like(db_ref)
            dlg_ref[...] = jnp.zeros_like(dlg_ref)
            dlb_ref[...] = jnp.zeros_like(dlb_ref)

        dhdn = jnp.zeros((rows, c), F32)
        for w in range(CONV_WIDTH):
            back = CONV_WIDTH - 1 - w
            dhdn += dy_scr[back:back + rows, :] * w_ref[w:w + 1, :]
            off = CONV_HALO - (CONV_WIDTH - 1) + w
            dw_ref[w:w + 1, :] += jnp.sum(dy * hdn_scr[off:off + rows, :], axis=0, keepdims=True)
        da_ref[...] = (dhdn * sig_gate).astype(da_ref.dtype)
        dgate_ref[...] = (dhdn * av * sig_gate * (1.0 - sig_gate)).astype(dgate_ref.dtype)
        db_ref[...] += jnp.sum(dy, axis=0, keepdims=True)
        dlg_ref[...] += jnp.sum(dln * yhat, axis=0, keepdims=True)
        dlb_ref[...] += jnp.sum(dln, axis=0, keepdims=True)

    prev_halo = lambda i: jnp.maximum(i * hpb - 1, 0)
    next_halo = lambda i: jnp.minimum((i + 1) * hpb, p // CONV_HALO - 1)
    vec = pl.BlockSpec((1, c), lambda i: (0, 0))
    blk = lambda col: pl.BlockSpec((rows, c), lambda i: (i, col))
    outs = pl.pallas_call(
        body,
        out_shape=[
            jax.ShapeDtypeStruct((p, c), BF16),
            jax.ShapeDtypeStruct((p, c), BF16),
            jax.ShapeDtypeStruct((CONV_WIDTH + 1, c), F32),
            jax.ShapeDtypeStruct((1, c), F32),
            jax.ShapeDtypeStruct((1, c), F32),
            jax.ShapeDtypeStruct((1, c), F32),
        ],
        grid=(n_blocks,),
        in_specs=[
            blk(a_col), blk(gate_col),
            pl.BlockSpec((CONV_HALO, c), lambda i: (prev_halo(i), a_col)),
            pl.BlockSpec((CONV_HALO, c), lambda i: (prev_halo(i), gate_col)),
            pl.BlockSpec((CONV_WIDTH, c), lambda i: (0, 0)),
            vec, vec,
            blk(0),
            pl.BlockSpec((CONV_HALO, c), lambda i: (next_halo(i), 0)),
            blk(1),
            pl.BlockSpec((CONV_HALO, c), lambda i: (next_halo(i), 1)),
        ],
        out_specs=[blk(0), blk(0), pl.BlockSpec((CONV_WIDTH + 1, c), lambda i: (0, 0)), vec, vec, vec],
        scratch_shapes=[pltpu.VMEM((CONV_HALO + rows, c), F32), pltpu.VMEM((rows + CONV_HALO, c), F32)],
        compiler_params=_params("arbitrary"),
        name=name,
    )(proj, proj, proj, proj, conv_w, ln_g, ln_b, y, y, dcat, dcat)
    da, dgate, dw, db, dlg, dlb = outs
    return da, dgate, dw[:CONV_WIDTH], db, dlg, dlb


def _group_matrix():
    r = jnp.arange(D_MODEL)[:, None] // SB_HEAD_DIM
    c = jnp.arange(D_MODEL)[None, :] // SB_HEAD_DIM
    return (r == c).astype(BF16)


def _qknorm_fwd(qkv, qg, kg, *, name):
    p = qkv.shape[0]
    d = D_MODEL
    rows = _pick(p, (384, 128, 96))

    def body(q_ref, k_ref, v_ref, qg_ref, kg_ref, gm_ref, qn_ref, kn_ref, vb_ref):
        gm = gm_ref[...]

        def norm(x, g):
            ms = _split_dot(x * x, gm) * (1.0 / SB_HEAD_DIM)
            return x * lax.rsqrt(ms + EPS) * g

        qn_ref[...] = norm(q_ref[...], qg_ref[...]).astype(BF16)
        kn_ref[...] = norm(k_ref[...], kg_ref[...]).astype(BF16)
        vb_ref[...] = v_ref[...].astype(BF16)

    blk = lambda col: pl.BlockSpec((rows, d), lambda i: (i, col))
    vec = pl.BlockSpec((1, d), lambda i: (0, 0))
    return pl.pallas_call(
        body,
        out_shape=[jax.ShapeDtypeStruct((p, d), BF16)] * 3,
        grid=(p // rows,),
        in_specs=[blk(0), blk(1), blk(2), vec, vec, pl.BlockSpec((d, d), lambda i: (0, 0))],
        out_specs=[blk(0)] * 3,
        compiler_params=_params("parallel"),
        name=name,
    )(qkv, qkv, qkv, qg, kg, _group_matrix())


def _qknorm_bwd(qkv, qg, kg, dqn, dkn, dv, *, name):
    p = qkv.shape[0]
    d = D_MODEL
    rows = _pick(p, (384, 128, 96))

    def body(q_ref, k_ref, qg_ref, kg_ref, gm_ref, dqn_ref, dkn_ref, dv_ref, dqkv_ref, dqg_ref, dkg_ref):
        gm = gm_ref[...]

        def bwd(x, g, dy):
            ms = _split_dot(x * x, gm) * (1.0 / SB_HEAD_DIM)
            r = lax.rsqrt(ms + EPS)
            gdy = dy * g
            proj = _split_dot(x * gdy, gm) * (1.0 / SB_HEAD_DIM)
            return r * gdy - x * (r * r * r) * proj, jnp.sum(dy * x * r, axis=0, keepdims=True)

        dq, dqg = bwd(q_ref[...], qg_ref[...], dqn_ref[...])
        dk, dkg = bwd(k_ref[...], kg_ref[...], dkn_ref[...])
        dqkv_ref[:, 0:d] = dq.astype(BF16)
        dqkv_ref[:, d:2 * d] = dk.astype(BF16)
        dqkv_ref[:, 2 * d:3 * d] = dv_ref[...].astype(BF16)

        @pl.when(pl.program_id(0) == 0)
        def _():
            dqg_ref[...] = dqg
            dkg_ref[...] = dkg

        @pl.when(pl.program_id(0) > 0)
        def _():
            dqg_ref[...] += dqg
            dkg_ref[...] += dkg

    blk = lambda col: pl.BlockSpec((rows, d), lambda i: (i, col))
    vec = pl.BlockSpec((1, d), lambda i: (0, 0))
    return pl.pallas_call(
        body,
        out_shape=[jax.ShapeDtypeStruct((p, 3 * d), BF16), jax.ShapeDtypeStruct((1, d), F32),
                   jax.ShapeDtypeStruct((1, d), F32)],
        grid=(p // rows,),
        in_specs=[blk(0), blk(1), vec, vec, pl.BlockSpec((d, d), lambda i: (0, 0)), blk(0), blk(0), blk(0)],
        out_specs=[pl.BlockSpec((rows, 3 * d), lambda i: (i, 0)), vec, vec],
        compiler_params=_params("arbitrary"),
        name=name,
    )(qkv, qkv, qg, kg, _group_matrix(), dqn, dkn, dv)


SB_PAIR = 2 * SB_HEAD_DIM
SB_GROUP = 8
SB_MASKED = -1e30


def _sb_consts():
    lane = lax.broadcasted_iota(jnp.int32, (CHUNK, SB_PAIR), 1)
    r = lax.broadcasted_iota(jnp.int32, (CHUNK, CHUNK), 0)
    c = lax.broadcasted_iota(jnp.int32, (CHUNK, CHUNK), 1)
    lo = (lane < SB_HEAD_DIM).astype(F32).astype(BF16)
    ones = jnp.ones((CHUNK, CHUNK), BF16)
    twice = lambda m: jnp.concatenate([jnp.concatenate([m, ones], axis=1)] * 2, axis=0)
    later, earlier = twice((r > c).astype(BF16)), twice((r < c).astype(BF16))
    not_before = (c >= r).astype(F32) * SB_MASKED
    padding = (c < PAD_FRONT).astype(F32) * SB_MASKED
    return (lo, 1.0 - lo), c, later, earlier, not_before, padding


def _sb_halves(t, head_lanes):
    return t * head_lanes[0], t * head_lanes[1]


def _sb_logits(qh, kg, biases):
    z = _dot_nt(qh, kg)
    tiles = []
    for b, bias in enumerate(biases):
        zt = z[:, b * CHUNK:(b + 1) * CHUNK]
        if bias is not None:
            zt = zt + bias
        ls_pos = jnp.minimum(zt, 0.0) - jnp.log(1.0 + jnp.exp(-jnp.abs(zt)))
        tiles.append((ls_pos, ls_pos - zt))
    return tiles


def _sb_block_sums(tiles, m):
    st = jnp.concatenate(tiles, axis=0)
    hi = st.astype(BF16)
    lo = (st - hi.astype(F32)).astype(BF16)
    tot = _dot(jnp.concatenate([hi, lo], axis=1), m)
    return [(tot[i * CHUNK:(i + 1) * CHUNK, 0:CHUNK], tot[i * CHUNK:(i + 1) * CHUNK, CHUNK:2 * CHUNK])
            for i in range(len(tiles))]


def _sb_plan(qi, padding, not_before):
    top = lax.div(qi, SB_GROUP)
    size = qi - SB_GROUP * top + 1
    pad_if_first = padding * (top == 0).astype(F32)
    masks = []
    for n_b in range(1, SB_GROUP + 1):
        m = [None] * n_b
        m[n_b - 1] = not_before
        m[0] = pad_if_first if m[0] is None else m[0] + pad_if_first
        masks.append(m)
    return top, size, masks


def _once_if(cond, fn, carry):
    return lax.fori_loop(0, jnp.where(cond, 1, 0), lambda s, cr: fn(cr), carry)


def _sb_head_rows(tg, lanes, n_b):
    return jnp.concatenate([tg[b * CHUNK:(b + 1) * CHUNK] * lanes for b in range(n_b)], axis=0)


def _sb_fwd(qn, kn, vb, *, name):
    p = qn.shape[0]
    n_blocks = p // CHUNK
    n_pairs = SB_HEADS // 2
    scale = SB_HEAD_DIM ** -0.5

    def body(q_ref, k_ref, v_ref, o_ref, car_ref):
        head_lanes, c, later, _, not_before, padding = _sb_consts()

        def q_block(qi, _):
            rows = pl.ds(pl.multiple_of(qi * CHUNK, CHUNK), CHUNK)
            qh = _sb_halves(q_ref[rows, :], head_lanes)
            qs = (qh[0] * scale, qh[1] * scale)

            def blocks(kb0, biases, carry):
                n_b = len(biases)
                acc, run0, run1, sav0, sav1 = carry
                krows = pl.ds(pl.multiple_of(kb0 * CHUNK, CHUNK), n_b * CHUNK)
                kg, vg = k_ref[krows, :], v_ref[krows, :]
                tiles = [_sb_logits(qs[h], kg, biases) for h in range(2)]
                sums = [_sb_block_sums([log_keep for _, log_keep in tiles[h]], later) for h in range(2)]
                cols = [(c == kb0 + b).astype(F32) for b in range(n_b)]
                runs, savs = [run0, run1], [sav0, sav1]
                for h in range(2):
                    ws = [None] * n_b
                    for b in reversed(range(n_b)):
                        after, row_sum = sums[h][b]
                        ws[b] = jnp.exp(tiles[h][b][0] + after + runs[h]).astype(BF16)
                        savs[h] = savs[h] + cols[b] * runs[h]
                        runs[h] = runs[h] + row_sum
                    acc = acc + _dot(jnp.concatenate(ws, axis=1), _sb_head_rows(vg, head_lanes[h], n_b))
                return acc, runs[0], runs[1], savs[0], savs[1]

            zt = qh[0].astype(F32) * 0.0
            top, size, masks = _sb_plan(qi, padding, not_before)
            carry = (zt, zt, zt, zt, zt)
            for m in masks:
                carry = _once_if(size == len(m), functools.partial(blocks, SB_GROUP * top, m), carry)
            carry = lax.fori_loop(0, jnp.maximum(top - 1, 0),
                                  lambda it, cr: blocks(SB_GROUP * (top - 1 - it), [None] * SB_GROUP, cr), carry)
            carry = _once_if(top > 0, functools.partial(blocks, 0, [padding] + [None] * (SB_GROUP - 1)), carry)
            acc, _, _, sav0, sav1 = carry
            o_ref[rows, :] = acc.astype(o_ref.dtype)
            car_ref[rows, 0:CHUNK] = sav0
            car_ref[rows, CHUNK:2 * CHUNK] = sav1
            return 0

        lax.fori_loop(0, n_blocks, q_block, 0)

    col = pl.BlockSpec((p, SB_PAIR), lambda g: (0, g))
    return pl.pallas_call(
        body,
        out_shape=[jax.ShapeDtypeStruct((p, D_MODEL), BF16), jax.ShapeDtypeStruct((p, n_pairs * 2 * CHUNK), F32)],
        grid=(n_pairs,),
        in_specs=[col, col, col],
        out_specs=[col, pl.BlockSpec((p, 2 * CHUNK), lambda g: (0, g))],
        compiler_params=_params("parallel"),
        name=name,
    )(qn, kn, vb)


def _sb_bwd(qn, kn, vb, carries, do, *, name):
    p = qn.shape[0]
    n_blocks = p // CHUNK
    n_pairs = SB_HEADS // 2
    scale = SB_HEAD_DIM ** -0.5

    def body(q_ref, k_ref, v_ref, car_ref, do_ref, dq_ref, dk_ref, dv_ref):
        head_lanes, c, later, earlier, not_before, padding = _sb_consts()
        dk_ref[...] = jnp.zeros_like(dk_ref)
        dv_ref[...] = jnp.zeros_like(dv_ref)

        def q_block(qi, _):
            rows = pl.ds(pl.multiple_of(qi * CHUNK, CHUNK), CHUNK)
            qh = _sb_halves(q_ref[rows, :], head_lanes)
            qs = (qh[0] * scale, qh[1] * scale)
            doh = _sb_halves(do_ref[rows, :].astype(BF16), head_lanes)
            sav = (car_ref[rows, 0:CHUNK], car_ref[rows, CHUNK:2 * CHUNK])

            def blocks(kb0, biases, carry):
                n_b = len(biases)
                dq_acc, pre0, pre1 = carry
                krows = pl.ds(pl.multiple_of(kb0 * CHUNK, CHUNK), n_b * CHUNK)
                kg, vg = k_ref[krows, :], v_ref[krows, :]
                cols = [(c == kb0 + b).astype(F32) for b in range(n_b)]
                block = lambda t, b: t[:, b * CHUNK:(b + 1) * CHUNK]
                tiles = [_sb_logits(qs[h], kg, biases) for h in range(2)]
                afters = [_sb_block_sums([log_keep for _, log_keep in tiles[h]], later) for h in range(2)]
                dws = [_dot_nt(doh[h], vg) for h in range(2)]
                ws, es, befores = [], [], []
                for h in range(2):
                    runs = [jnp.sum(cols[b] * sav[h], axis=-1, keepdims=True) for b in range(n_b)]
                    ws.append([jnp.exp(tiles[h][b][0] + afters[h][b][0] + runs[b]) for b in range(n_b)])
                    es.append([ws[h][b] * block(dws[h], b) for b in range(n_b)])
                    befores.append(_sb_block_sums(es[h], earlier))
                pres = [pre0, pre1]
                dk_add = dv_add = None
                for h in range(2):
                    dzs = []
                    for b in range(n_b):
                        before, row_sum = befores[h][b]
                        sig = jnp.exp(tiles[h][b][0])
                        e = es[h][b]
                        dzs.append((e - (e + before + pres[h]) * sig).astype(BF16))
                        pres[h] = pres[h] + row_sum
                    dz = jnp.concatenate(dzs, axis=1)
                    w = jnp.concatenate([t.astype(BF16) for t in ws[h]], axis=1)
                    dq_acc = dq_acc + _dot(dz, _sb_head_rows(kg, head_lanes[h], n_b))
                    dv_h, dk_h = _dot_tn(w, doh[h]), _dot_tn(dz, qs[h])
                    dv_add = dv_h if dv_add is None else dv_add + dv_h
                    dk_add = dk_h if dk_add is None else dk_add + dk_h
                dv_ref[krows, :] += dv_add
                dk_ref[krows, :] += dk_add
                return dq_acc, pres[0], pres[1]

            zt = qh[0].astype(F32) * 0.0
            top, size, masks = _sb_plan(qi, padding, not_before)
            carry = _once_if(top > 0, functools.partial(blocks, 0, [padding] + [None] * (SB_GROUP - 1)), (zt, zt, zt))
            carry = lax.fori_loop(1, top, lambda g, cr: blocks(SB_GROUP * g, [None] * SB_GROUP, cr), carry)
            for m in masks:
                carry = _once_if(size == len(m), functools.partial(blocks, SB_GROUP * top, m), carry)
            dq_acc, _, _ = carry
            dq_ref[rows, :] = dq_acc * scale
            return 0

        lax.fori_loop(0, n_blocks, q_block, 0)

    col = pl.BlockSpec((p, SB_PAIR), lambda g: (0, g))
    return pl.pallas_call(
        body,
        out_shape=[jax.ShapeDtypeStruct((p, D_MODEL), F32)] * 3,
        grid=(n_pairs,),
        in_specs=[col, col, col, pl.BlockSpec((p, 2 * CHUNK), lambda g: (0, g)), col],
        out_specs=[col, col, col],
        compiler_params=_params("parallel"),
        name=name,
    )(qn, kn, vb, carries, do)


def _loss_head(h, target, *, name):
    p, d = h.shape
    n_blocks = p // CHUNK

    def body(h_ref, t_ref, sq_ref, dh_ref):
        i = pl.program_id(0)

        @pl.when(i == 0)
        def _():
            sq_ref[...] = jnp.zeros_like(sq_ref)
            dh_ref[...] = jnp.zeros_like(dh_ref)

        @pl.when(i > 0)
        def _():
            err = h_ref[...] - t_ref[...]
            sq_ref[...] += jnp.sum(err * err)
            dh_ref[...] = err * (1.0 / d)

    return pl.pallas_call(
        body,
        out_shape=[jax.ShapeDtypeStruct((8, 128), F32), jax.ShapeDtypeStruct((p, d), F32)],
        grid=(n_blocks,),
        in_specs=[pl.BlockSpec((CHUNK, d), lambda i: (i, 0)),
                  pl.BlockSpec((CHUNK, d), lambda i: (jnp.maximum(i - 1, 0), 0))],
        out_specs=[pl.BlockSpec((8, 128), lambda i: (0, 0)), pl.BlockSpec((CHUNK, d), lambda i: (i, 0))],
        compiler_params=_params("arbitrary"),
        name=name,
    )(h, target)


def _local_step(x, target, meta, norm_mix_g, norm_mlp_g, w_in, gn_g, conv_w, conv_b, ln_g, ln_b, w_out,
                w_qkv, qn_g, kn_g, w_o, w1, w2):
    seq = x.shape[0]
    p = PAD_FRONT + N_META + seq
    d = D_MODEL
    tables = _retention_tables(p)
    row = lambda v: v.reshape(1, -1)
    h0 = jnp.concatenate([jnp.zeros((PAD_FRONT, d), F32), meta, x], axis=0)

    hn0 = _rmsnorm_fwd(h0, row(norm_mix_g[0]), name="l0_mix_norm")
    proj = _matmul(hn0, w_in, mode="nn", out_dtypes=(F32,), name="l0_proj")
    og, opre, sprev = _retention_fwd(proj, gn_g, tables, name="l0_retention")
    cb, y_conv = _conv_fwd(proj, conv_w, row(conv_b), row(ln_g), row(ln_b), name="l0_conv")
    cat = jnp.concatenate([og, cb], axis=1)
    h1 = _matmul(cat, w_out, mode="nn", out_dtypes=(F32,), epilogue=_add_epilogue, extras=(h0,), name="l0_mix_out")
    h2, mlp0 = _mlp_fwd(h1, row(norm_mlp_g[0]), w1[0], w2[0], name="l0_mlp")

    hn1 = _rmsnorm_fwd(h2, row(norm_mix_g[1]), name="l1_mix_norm")
    qkv = _matmul(hn1, w_qkv, mode="nn", out_dtypes=(F32,), name="l1_qkv")
    qg_t, kg_t = jnp.tile(row(qn_g), (1, SB_HEADS)), jnp.tile(row(kn_g), (1, SB_HEADS))
    qn, kn, vb = _qknorm_fwd(qkv, qg_t, kg_t, name="l1_qknorm")
    o_sb, carries = _sb_fwd(qn, kn, vb, name="l1_stickbreak")
    h3 = _matmul(o_sb, w_o, mode="nn", out_dtypes=(F32,), epilogue=_add_epilogue, extras=(h2,), name="l1_mix_out")
    h4, mlp1 = _mlp_fwd(h3, row(norm_mlp_g[1]), w1[1], w2[1], name="l1_mlp")

    sq, dh4 = _loss_head(h4, target, name="loss_head")

    dh3, dg_mlp1, dw1_1, dw2_1 = _mlp_bwd(h3, row(norm_mlp_g[1]), w1[1], w2[1], mlp1, dh4, name="l1_mlp_bwd")
    do_sb = _matmul(dh3, w_o, mode="nt", out_dtypes=(F32,), name="l1_do")
    dw_o = _matmul(o_sb, dh3, mode="tn", out_dtypes=(F32,), name="l1_dwo")
    dqn, dkn, dv = _sb_bwd(qn, kn, vb, carries, do_sb, name="l1_stickbreak_bwd")
    dqkv, dqg_t, dkg_t = _qknorm_bwd(qkv, qg_t, kg_t, dqn, dkn, dv, name="l1_qknorm_bwd")
    dw_qkv = _matmul(hn1, dqkv, mode="tn", out_dtypes=(F32,), name="l1_dwqkv")
    dhn1 = _matmul(dqkv, w_qkv, mode="nt", out_dtypes=(F32,), name="l1_dhn")
    dh2, dg_mix1 = _rmsnorm_bwd(h2, row(norm_mix_g[1]), dhn1, dh3, name="l1_mix_dnorm")

    dh1, dg_mlp0, dw1_0, dw2_0 = _mlp_bwd(h1, row(norm_mlp_g[0]), w1[0], w2[0], mlp0, dh2, name="l0_mlp_bwd")
    dcat = _matmul(dh1, w_out, mode="nt", out_dtypes=(F32,), name="l0_dcat")
    dw_out = _matmul(cat, dh1, mode="tn", out_dtypes=(F32,), name="l0_dwout")
    dq, dk, dvr, dgate_r, dgn = _retention_bwd(proj, gn_g, tables, opre, sprev, dcat, name="l0_retention_bwd")
    da, dgate_c, dconv_w, dconv_b, dln_g, dln_b = _conv_bwd(proj, conv_w, row(ln_g), row(ln_b), y_conv, dcat,
                                                            name="l0_conv_bwd")
    dproj = jnp.concatenate([dq, dk, dvr, dgate_r, da, dgate_c], axis=1)
    dw_in = _matmul(hn0, dproj, mode="tn", out_dtypes=(F32,), name="l0_dwin")
    dhn0 = _matmul(dproj, w_in, mode="nt", out_dtypes=(F32,), name="l0_dhn")
    dh0, dg_mix0 = _rmsnorm_bwd(h0, row(norm_mix_g[0]), dhn0, dh1, name="l0_mix_dnorm")

    fold = lambda t: t.reshape(SB_HEADS, SB_HEAD_DIM).sum(axis=0)
    grads = dict(
        x=dh0[PAD_FRONT + N_META:],
        meta=dh0[PAD_FRONT:PAD_FRONT + N_META],
        norm_mix_g=jnp.concatenate([dg_mix0, dg_mix1], axis=0),
        norm_mlp_g=jnp.concatenate([dg_mlp0, dg_mlp1], axis=0),
        even_w_in=dw_in,
        even_ret_gn_g=dgn.reshape(RET_HEADS, RET_V_DIM),
        even_conv_w=dconv_w,
        even_conv_b=dconv_b,
        even_conv_ln_g=dln_g,
        even_conv_ln_b=dln_b,
        even_w_out=dw_out,
        odd_w_qkv=dw_qkv,
        odd_q_norm_g=fold(dqg_t)[None],
        odd_k_norm_g=fold(dkg_t)[None],
        odd_w_o=dw_o,
        mlp_w1=(dw1_0, dw1_1),
        mlp_w2=(dw2_0, dw2_1),
    )
    return sq[0, 0], grads


def _position():
    x, y, c = lax.axis_index("x"), lax.axis_index("y"), lax.axis_index("c")
    other_chips = [(1 - x, y), (x, 1 - y), (1 - x, 1 - y)]
    return x, y, c, other_chips


def _shard_of(ref, kind, s, n):
    rows, cols = ref.shape
    if kind == "col":
        return ref.at[:, pl.ds(s * (cols // n), cols // n)]
    return ref.at[pl.ds(s * (rows // n), rows // n), :]


def _half_of(ref, kind, c):
    rows, cols = ref.shape
    if kind == "col":
        return ref.at[pl.ds(c * (rows // 2), rows // 2), :]
    return ref.at[:, pl.ds(c * (cols // 2), cols // 2)]


def _remote(src, dst, send_sems, recv_sems, idx, device):
    return pltpu.make_async_remote_copy(src_ref=src, dst_ref=dst, send_sem=send_sems.at[idx], recv_sem=recv_sems.at[idx],
                                        device_id=device, device_id_type=MESH)


def _cast_into_whole(w, kind, s_arr, *, name):
    rows, cols = w.shape
    tr = _pick(rows, (256, 128))
    nb = rows // tr
    if kind == "col":
        whole, o_spec = (rows, cols * N_CHIPS), pl.BlockSpec((tr, cols), lambda i, s_ref: (i, s_ref[0]))
    else:
        whole, o_spec = (rows * N_CHIPS, cols), pl.BlockSpec((tr, cols), lambda i, s_ref: (s_ref[0] * nb + i, 0))

    def body(s_ref, w_ref, o_ref):
        o_ref[...] = w_ref[...].astype(BF16)

    return pl.pallas_call(
        body,
        out_shape=jax.ShapeDtypeStruct(whole, BF16),
        grid_spec=pltpu.PrefetchScalarGridSpec(num_scalar_prefetch=1, grid=(nb,),
                                               in_specs=[pl.BlockSpec((tr, cols), lambda i, s_ref: (i, 0))],
                                               out_specs=o_spec),
        compiler_params=_params("parallel"),
        name=name,
    )(s_arr, w)


def _allgather_weights(wholes, kinds):
    n = len(wholes)

    def body(*refs):
        ins, outs = refs[:n], refs[n:2 * n]
        send_sems, recv_sems = refs[2 * n:]
        x, y, c, chips = _position()
        me_chip = 2 * x + y
        sibling = (x, y, 1 - c)
        sends = []
        for t in range(n):
            for k, (cx, cy) in enumerate(chips):
                src = _half_of(_shard_of(ins[t], kinds[t], me_chip, N_CHIPS), kinds[t], c)
                dst = _half_of(_shard_of(outs[t], kinds[t], me_chip, N_CHIPS), kinds[t], c)
                sends.append(_remote(src, dst, send_sems, recv_sems, 6 * t + k, (cx, cy, c)))
        for cp in sends:
            cp.start()
        passed = []
        for t in range(n):
            for k, (cx, cy) in enumerate(chips):
                landed = _half_of(_shard_of(outs[t], kinds[t], 2 * cx + cy, N_CHIPS), kinds[t], c)
                _remote(landed, landed, send_sems, recv_sems, 6 * t + k, (cx, cy, c)).wait_recv()
                fwd = _remote(landed, landed, send_sems, recv_sems, 6 * t + 3 + k, sibling)
                fwd.start()
                passed.append(fwd)
        for t in range(n):
            for k, (cx, cy) in enumerate(chips):
                theirs = _half_of(_shard_of(outs[t], kinds[t], 2 * cx + cy, N_CHIPS), kinds[t], 1 - c)
                _remote(theirs, theirs, send_sems, recv_sems, 6 * t + 3 + k, sibling).wait_recv()
        for cp in sends + passed:
            cp.wait_send()

    return pl.pallas_call(
        body,
        out_shape=[jax.ShapeDtypeStruct(w.shape, BF16) for w in wholes],
        in_specs=[ANY] * n,
        out_specs=[ANY] * n,
        input_output_aliases={t: t for t in range(n)},
        scratch_shapes=[pltpu.SemaphoreType.DMA((6 * n,)), pltpu.SemaphoreType.DMA((6 * n,))],
        name="allgather_weights",
    )(*wholes)


def _allgather8(block, *, name):
    rows, cols = block.shape

    def body(in_ref, out_ref, send_sems, recv_sems, local_sem):
        x, y, c, _ = _position()
        me = 4 * x + 2 * y + c
        mine = pltpu.make_async_copy(in_ref, out_ref.at[me], local_sem)
        mine.start()
        peers = []
        for flip in range(1, N_DEV):
            fx, fy, fc = (flip >> 2) & 1, (flip >> 1) & 1, flip & 1
            peers.append(((1 - x if fx else x), (1 - y if fy else y), (1 - c if fc else c)))
        sends = [_remote(in_ref, out_ref.at[me], send_sems, recv_sems, j, peer) for j, peer in enumerate(peers)]
        for cp in sends:
            cp.start()
        for j, (px, py, pc) in enumerate(peers):
            slot = out_ref.at[4 * px + 2 * py + pc]
            _remote(slot, slot, send_sems, recv_sems, j, (px, py, pc)).wait_recv()
        for cp in sends:
            cp.wait_send()
        mine.wait()

    vmem = pl.BlockSpec(memory_space=pltpu.VMEM)
    return pl.pallas_call(
        body,
        out_shape=jax.ShapeDtypeStruct((N_DEV, rows, cols), F32),
        in_specs=[vmem],
        out_specs=vmem,
        scratch_shapes=[pltpu.SemaphoreType.DMA((N_DEV - 1,)), pltpu.SemaphoreType.DMA((N_DEV - 1,)),
                        pltpu.SemaphoreType.DMA],
        name=name,
    )(block)


def _sum8(stack, *, name):
    _, rows, cols = stack.shape

    def body(s_ref, o_ref):
        acc = s_ref[0]
        for i in range(1, N_DEV):
            acc = acc + s_ref[i]
        o_ref[...] = acc

    return pl.pallas_call(body, out_shape=jax.ShapeDtypeStruct((rows, cols), F32), name=name)(stack)


def _swap_halves_in(grads, kinds):
    n = len(grads)

    def body(*refs):
        ins, outs = refs[:n], refs[n:2 * n]
        send_sems, recv_sems = refs[2 * n:]
        x, y, c, _ = _position()
        sibling = (x, y, 1 - c)
        sends = [_remote(_half_of(ins[t], kinds[t], 1 - c), outs[t], send_sems, recv_sems, t, sibling) for t in range(n)]
        for cp in sends:
            cp.start()
        for t in range(n):
            _remote(_half_of(ins[t], kinds[t], c), outs[t], send_sems, recv_sems, t, sibling).wait_recv()
        for cp in sends:
            cp.wait_send()

    def half(g, kind):
        rows, cols = g.shape
        return (rows // 2, cols) if kind == "col" else (rows, cols // 2)

    return pl.pallas_call(
        body,
        out_shape=[jax.ShapeDtypeStruct(half(g, k), F32) for g, k in zip(grads, kinds)],
        in_specs=[ANY] * n,
        out_specs=[ANY] * n,
        scratch_shapes=[pltpu.SemaphoreType.DMA((n,)), pltpu.SemaphoreType.DMA((n,))],
        name="reduce_core_pair",
    )(*grads)


def _half_add(grad, theirs, kind, c_arr, *, name):
    rows, cols = theirs.shape
    tr = _pick(rows, (256, 128))
    nb = rows // tr
    if kind == "col":
        g_spec = pl.BlockSpec((tr, cols), lambda i, c_ref: (c_ref[0] * nb + i, 0))
    else:
        g_spec = pl.BlockSpec((tr, cols), lambda i, c_ref: (i, c_ref[0]))
    t_spec = pl.BlockSpec((tr, cols), lambda i, c_ref: (i, 0))

    def body(c_ref, g_ref, t_ref, o32_ref, o16_ref):
        tot = g_ref[...] + t_ref[...]
        o32_ref[...] = tot
        o16_ref[...] = tot.astype(BF16)

    return pl.pallas_call(
        body,
        out_shape=[jax.ShapeDtypeStruct((rows, cols), F32), jax.ShapeDtypeStruct((rows, cols), BF16)],
        grid_spec=pltpu.PrefetchScalarGridSpec(num_scalar_prefetch=1, grid=(nb,), in_specs=[g_spec, t_spec],
                                               out_specs=[t_spec, t_spec]),
        compiler_params=_params("parallel"),
        name=name,
    )(c_arr, grad, theirs)


def _exchange_chips(parts, kinds):
    n = len(parts)

    def body(*refs):
        ins, outs = refs[:n], refs[n:2 * n]
        send_sems, recv_sems = refs[2 * n:]
        x, y, c, chips = _position()
        sends = []
        for t in range(n):
            for k, (cx, cy) in enumerate(chips):
                src = _shard_of(ins[t], kinds[t], 2 * cx + cy, N_CHIPS)
                sends.append(_remote(src, outs[t].at[k], send_sems, recv_sems, 3 * t + k, (cx, cy, c)))
        for cp in sends:
            cp.start()
        for t in range(n):
            for k, (cx, cy) in enumerate(chips):
                src = _shard_of(ins[t], kinds[t], 2 * cx + cy, N_CHIPS)
                _remote(src, outs[t].at[k], send_sems, recv_sems, 3 * t + k, (cx, cy, c)).wait_recv()
        for cp in sends:
            cp.wait_send()

    def piece(p, kind):
        rows, cols = p.shape
        return (3, rows, cols // N_CHIPS) if kind == "col" else (3, rows // N_CHIPS, cols)

    return pl.pallas_call(
        body,
        out_shape=[jax.ShapeDtypeStruct(piece(p, k), BF16) for p, k in zip(parts, kinds)],
        in_specs=[ANY] * n,
        out_specs=[ANY] * n,
        scratch_shapes=[pltpu.SemaphoreType.DMA((3 * n,)), pltpu.SemaphoreType.DMA((3 * n,))],
        name="reduce_chips",
    )(*parts)


def _shard_sum(part32, recv, kind, sc_arr, *, name):
    _, rows, cols = recv.shape
    tr = _pick(rows, (256, 128))
    nb = rows // tr
    if kind == "col":
        whole = (2 * rows, cols)
        p_spec = pl.BlockSpec((tr, cols), lambda i, sc: (i, sc[0]))
        o_spec = pl.BlockSpec((tr, cols), lambda i, sc: (sc[1] * nb + i, 0))
    else:
        whole = (rows, 2 * cols)
        p_spec = pl.BlockSpec((tr, cols), lambda i, sc: (sc[0] * nb + i, 0))
        o_spec = pl.BlockSpec((tr, cols), lambda i, sc: (i, sc[1]))
    r_spec = pl.BlockSpec((3, tr, cols), lambda i, sc: (0, i, 0))

    def body(sc_ref, p_ref, r_ref, o_ref):
        acc = p_ref[...]
        for k in range(3):
            acc = acc + r_ref[k].astype(F32)
        o_ref[...] = acc

    return pl.pallas_call(
        body,
        out_shape=jax.ShapeDtypeStruct(whole, F32),
        grid_spec=pltpu.PrefetchScalarGridSpec(num_scalar_prefetch=1, grid=(nb,), in_specs=[p_spec, r_spec],
                                               out_specs=o_spec),
        compiler_params=_params("parallel"),
        name=name,
    )(sc_arr, part32, recv)


def _swap_halves_out(shards, kinds):
    n = len(shards)

    def body(*refs):
        ins, outs = refs[:n], refs[n:2 * n]
        send_sems, recv_sems = refs[2 * n:]
        x, y, c, _ = _position()
        sibling = (x, y, 1 - c)
        sends = [_remote(_half_of(ins[t], kinds[t], c), _half_of(outs[t], kinds[t], c), send_sems, recv_sems, t, sibling)
                 for t in range(n)]
        for cp in sends:
            cp.start()
        for t in range(n):
            theirs = _half_of(outs[t], kinds[t], 1 - c)
            _remote(theirs, theirs, send_sems, recv_sems, t, sibling).wait_recv()
        for cp in sends:
            cp.wait_send()

    return pl.pallas_call(
        body,
        out_shape=[jax.ShapeDtypeStruct(s.shape, F32) for s in shards],
        in_specs=[ANY] * n,
        out_specs=[ANY] * n,
        input_output_aliases={t: t for t in range(n)},
        scratch_shapes=[pltpu.SemaphoreType.DMA((n,)), pltpu.SemaphoreType.DMA((n,))],
        name="gather_core_pair",
    )(*shards)


def _adamw(w, g, m, v, *, name):
    rows, cols = w.shape
    tr = _pick(rows, (256, 128)) if rows * cols > 64 * 1024 else rows

    def body(w_ref, g_ref, m_ref, v_ref, d_ref, nm_ref, nv_ref):
        gv = g_ref[...]
        nm = ADAM_B1 * m_ref[...] + (1.0 - ADAM_B1) * gv
        nv = ADAM_B2 * v_ref[...] + (1.0 - ADAM_B2) * jnp.square(gv)
        m_hat = nm / (1.0 - ADAM_B1 ** ADAM_STEP)
        v_hat = nv / (1.0 - ADAM_B2 ** ADAM_STEP)
        d_ref[...] = -ADAM_LR * (m_hat / (jnp.sqrt(v_hat) + ADAM_EPS) + ADAM_WD * w_ref[...])
        nm_ref[...] = nm
        nv_ref[...] = nv

    spec = pl.BlockSpec((tr, cols), lambda i: (i, 0))
    return pl.pallas_call(
        body,
        out_shape=[jax.ShapeDtypeStruct((rows, cols), F32)] * 3,
        grid=(rows // tr,),
        in_specs=[spec] * 4,
        out_specs=[spec] * 3,
        compiler_params=_params("parallel"),
        name=name,
    )(w, g, m, v)


BIG = ("even_w_in", "odd_w_qkv", "mlp_w1_0", "mlp_w1_1", "even_w_out", "odd_w_o", "mlp_w2_0", "mlp_w2_1")
BIG_KIND = ("col", "col", "col", "col", "row", "row", "row", "row")
SUBLANES = 8


def _pack_rows(parts, width):
    padded, offsets, r0 = [], [], 0
    for t in parts:
        rows = -(-t.shape[0] // SUBLANES) * SUBLANES
        padded.append(jnp.pad(t, ((0, rows - t.shape[0]), (0, width - t.shape[1]))))
        offsets.append(r0)
        r0 += rows
    return jnp.concatenate(padded, axis=0), offsets


def kernel(x, meta, norm_mix_g, norm_mlp_g, even_w_in, even_ret_gn_g, even_conv_w, even_conv_b, even_conv_ln_g, even_conv_ln_b, even_w_out, odd_w_qkv, odd_q_norm_g, odd_k_norm_g, odd_w_o, mlp_w1, mlp_w2, loss_target, m_meta, m_norm_mix_g, m_norm_mlp_g, m_even_w_in, m_even_ret_gn_g, m_even_conv_w, m_even_conv_b, m_even_conv_ln_g, m_even_conv_ln_b, m_even_w_out, m_odd_w_qkv, m_odd_q_norm_g, m_odd_k_norm_g, m_odd_w_o, m_mlp_w1, m_mlp_w2, v_meta, v_norm_mix_g, v_norm_mlp_g, v_even_w_in, v_even_ret_gn_g, v_even_conv_w, v_even_conv_b, v_even_conv_ln_g, v_even_conv_ln_b, v_even_w_out, v_odd_w_qkv, v_odd_q_norm_g, v_odd_k_norm_g, v_odd_w_o, v_mlp_w1, v_mlp_w2):
    d = D_MODEL
    xi, yi, ci = lax.axis_index("x"), lax.axis_index("y"), lax.axis_index("c")
    chip = 2 * xi + yi
    c_arr = jnp.reshape(ci, (1,)).astype(jnp.int32)
    s_arr = jnp.reshape(chip, (1,)).astype(jnp.int32)

    def split_big(w_in, w_qkv, w1, w_out, w_o, w2):
        return dict(zip(BIG, (w_in[0], w_qkv[0], w1[0], w1[1], w_out[0], w_o[0], w2[0], w2[1])))

    w_big = split_big(even_w_in, odd_w_qkv, mlp_w1, even_w_out, odd_w_o, mlp_w2)
    m_big = split_big(m_even_w_in, m_odd_w_qkv, m_mlp_w1, m_even_w_out, m_odd_w_o, m_mlp_w2)
    v_big = split_big(v_even_w_in, v_odd_w_qkv, v_mlp_w1, v_even_w_out, v_odd_w_o, v_mlp_w2)

    placed = [_cast_into_whole(w_big[n], k, s_arr, name="cast_" + n) for n, k in zip(BIG, BIG_KIND)]
    full = dict(zip(BIG, _allgather_weights(placed, BIG_KIND)))

    packed, (r_meta, r_conv, r_gn) = _pack_rows([meta, even_conv_w[0], even_ret_gn_g[0]], d // N_CHIPS)
    gathered = _allgather8(packed, name="allgather_small_params")[0::2]
    across = lambda r0, rows, width: jnp.concatenate([gathered[s, r0:r0 + rows, 0:width] for s in range(N_CHIPS)], axis=1)
    meta_full = across(r_meta, N_META, d // N_CHIPS)
    conv_w_full = across(r_conv, CONV_WIDTH, d // N_CHIPS)
    gn_full = across(r_gn, RET_HEADS, RET_V_DIM // N_CHIPS)

    sq, g = _local_step(
        x[0], loss_target[0], meta_full, norm_mix_g, norm_mlp_g, full["even_w_in"], gn_full, conv_w_full,
        even_conv_b[0], even_conv_ln_g[0], even_conv_ln_b[0], full["even_w_out"], full["odd_w_qkv"],
        odd_q_norm_g[0], odd_k_norm_g[0], full["odd_w_o"], (full["mlp_w1_0"], full["mlp_w1_1"]),
        (full["mlp_w2_0"], full["mlp_w2_1"]))
    loss = lax.psum(0.5 * sq / d, ("x", "y", "c"))

    small_names = ("norm_mix_g", "norm_mlp_g", "even_conv_b", "even_conv_ln_g", "even_conv_ln_b", "odd_q_norm_g",
                   "odd_k_norm_g", "meta", "even_conv_w", "even_ret_gn_g")
    pack, offsets = _pack_rows([g[n] for n in small_names], d)
    summed = _sum8(_allgather8(pack, name="allgather_small_grads"), name="sum_small_grads")
    small = {n: summed[r0:r0 + g[n].shape[0], 0:g[n].shape[1]] for n, r0 in zip(small_names, offsets)}
    for n in ("meta", "even_conv_w", "even_ret_gn_g"):
        width = small[n].shape[1] // N_CHIPS
        small[n] = lax.dynamic_slice_in_dim(small[n], chip * width, width, axis=1)

    g_big = [g["even_w_in"], g["odd_w_qkv"], g["mlp_w1"][0], g["mlp_w1"][1], g["even_w_out"], g["odd_w_o"],
             g["mlp_w2"][0], g["mlp_w2"][1]]
    theirs = _swap_halves_in(g_big, BIG_KIND)
    sums = [_half_add(gb, th, k, c_arr, name="pair_sum_" + n) for gb, th, k, n in zip(g_big, theirs, BIG_KIND, BIG)]
    recv = _exchange_chips([s16 for _, s16 in sums], BIG_KIND)
    sc_arr = jnp.concatenate([s_arr, c_arr])
    halves = [_shard_sum(s32, r, k, sc_arr, name="chip_sum_" + n) for (s32, _), r, k, n in zip(sums, recv, BIG_KIND, BIG)]
    grad_big = dict(zip(BIG, _swap_halves_out(halves, BIG_KIND)))

    upd = {n: _adamw(w_big[n], grad_big[n], m_big[n], v_big[n], name="adamw_" + n) for n in BIG}

    def join(name, idx, lead):
        if name in ("mlp_w1", "mlp_w2"):
            return jnp.stack([upd[name + "_0"][idx], upd[name + "_1"][idx]]) if idx >= 0 else jnp.stack(
                [grad_big[name + "_0"], grad_big[name + "_1"]])
        t = upd[name][idx] if idx >= 0 else grad_big[name]
        return t[None] if lead else t

    small_w = dict(meta=meta, norm_mix_g=norm_mix_g, norm_mlp_g=norm_mlp_g, even_ret_gn_g=even_ret_gn_g[0],
                   even_conv_w=even_conv_w[0], even_conv_b=even_conv_b, even_conv_ln_g=even_conv_ln_g,
                   even_conv_ln_b=even_conv_ln_b, odd_q_norm_g=odd_q_norm_g, odd_k_norm_g=odd_k_norm_g)
    small_m = dict(meta=m_meta, norm_mix_g=m_norm_mix_g, norm_mlp_g=m_norm_mlp_g, even_ret_gn_g=m_even_ret_gn_g[0],
                   even_conv_w=m_even_conv_w[0], even_conv_b=m_even_conv_b, even_conv_ln_g=m_even_conv_ln_g,
                   even_conv_ln_b=m_even_conv_ln_b, odd_q_norm_g=m_odd_q_norm_g, odd_k_norm_g=m_odd_k_norm_g)
    small_v = dict(meta=v_meta, norm_mix_g=v_norm_mix_g, norm_mlp_g=v_norm_mlp_g, even_ret_gn_g=v_even_ret_gn_g[0],
                   even_conv_w=v_even_conv_w[0], even_conv_b=v_even_conv_b, even_conv_ln_g=v_even_conv_ln_g,
                   even_conv_ln_b=v_even_conv_ln_b, odd_q_norm_g=v_odd_q_norm_g, odd_k_norm_g=v_odd_k_norm_g)
    small_upd = {n: _adamw(small_w[n], small[n], small_m[n], small_v[n], name="adamw_" + n) for n in small_w}
    leading = ("even_ret_gn_g", "even_conv_w")

    order = ("meta", "norm_mix_g", "norm_mlp_g", "even_w_in", "even_ret_gn_g", "even_conv_w", "even_conv_b",
             "even_conv_ln_g", "even_conv_ln_b", "even_w_out", "odd_w_qkv", "odd_q_norm_g", "odd_k_norm_g", "odd_w_o",
             "mlp_w1", "mlp_w2")
    big_lead = ("even_w_in", "even_w_out", "odd_w_qkv", "odd_w_o")

    def leaf(name, idx):
        if name in small_w:
            t = small_upd[name][idx] if idx >= 0 else small[name]
            return t[None] if name in leading else t
        return join(name, idx, name in big_lead)

    outs = [loss, g["x"][None]]
    for idx in (-1, 0, 1, 2):
        outs += [leaf(n, idx) for n in order]
    return tuple(outs)
```

```python
import functools

import jax
import jax.numpy as jnp
from jax import lax
from jax.experimental import pallas as pl
from jax.experimental.pallas import tpu as pltpu

F32 = jnp.float32
BF16 = jnp.bfloat16

D_MODEL = 1024
N_META = 16
CHUNK = 128
PAD_FRONT = (-N_META) % CHUNK
RET_HEADS = 4
RET_QK_DIM = 128
RET_V_DIM = 256
RET_QK_W = RET_HEADS * RET_QK_DIM
RET_V_W = RET_HEADS * RET_V_DIM
CONV_WIDTH = 31
CONV_HALO = 32
RET_DECAY_OFFSET = 5.0
ROPE_BASE = 10000.0
SB_HEADS = 16
SB_HEAD_DIM = 64
D_FF = 4 * D_MODEL
EPS = 1e-6
ADAM_LR = 0.001
ADAM_B1 = 0.9
ADAM_B2 = 0.999
ADAM_EPS = 1e-08
ADAM_WD = 0.01
ADAM_STEP = 10

N_CHIPS = 4
N_DEV = 8
VMEM_LIMIT = 56 * 1024 * 1024
MESH = pl.DeviceIdType.MESH
ANY = pl.BlockSpec(memory_space=pl.ANY)


def _params(*sem):
    return pltpu.CompilerParams(dimension_semantics=sem, vmem_limit_bytes=VMEM_LIMIT)


def _pick(n, cands):
    for c in cands:
        if n % c == 0:
            return c
    return n


def _sigmoid(x):
    return 1.0 / (1.0 + jnp.exp(-x))


def _dot(a, b):
    return lax.dot_general(a, b, (((1,), (0,)), ((), ())), preferred_element_type=F32)


def _dot_nt(a, b):
    return lax.dot_general(a, b, (((1,), (1,)), ((), ())), preferred_element_type=F32)


def _dot_tn(a, b):
    return lax.dot_general(a, b, (((0,), (0,)), ((), ())), preferred_element_type=F32)


def _split_dot(x, m):
    hi = x.astype(BF16)
    lo = (x - hi.astype(F32)).astype(BF16)
    return _dot(hi, m) + _dot(lo, m)


def _matmul(a, b, *, mode, out_dtypes, epilogue=None, extras=(), name):
    if mode == "nn":
        (m, k), (k2, n) = a.shape, b.shape
    elif mode == "nt":
        (m, k), (n, k2) = a.shape, b.shape
    else:
        (k, m), (k2, n) = a.shape, b.shape
    assert k == k2, (a.shape, b.shape, mode)
    tm = _pick(m, (1056, 1024, 768, 512, 384, 256, 128, 96))
    tn = _pick(n, (1024, 768, 512, 256, 128))
    tk = _pick(k, (1056, 1024, 768, 512, 384, 256, 128, 96))
    nk = k // tk
    dot = {"nn": _dot, "nt": _dot_nt, "tn": _dot_tn}[mode]
    n_extra, n_out = len(extras), len(out_dtypes)
    if epilogue is None:
        epilogue = lambda acc: (acc,)

    def body(a_ref, b_ref, *rest):
        extra_refs = rest[:n_extra]
        out_refs = rest[n_extra:n_extra + n_out]
        part = dot(a_ref[...].astype(BF16), b_ref[...].astype(BF16))

        def finish(acc):
            res = epilogue(acc, *[r[...] for r in extra_refs])
            for o_ref, r in zip(out_refs, res):
                o_ref[...] = r.astype(o_ref.dtype)

        if nk == 1:
            finish(part)
        else:
            acc_ref = rest[-1]
            kk = pl.program_id(2)

            @pl.when(kk == 0)
            def _():
                acc_ref[...] = part

            @pl.when(kk > 0)
            def _():
                acc_ref[...] += part

            @pl.when(kk == nk - 1)
            def _():
                finish(acc_ref[...])

    if mode == "nn":
        a_spec = pl.BlockSpec((tm, tk), lambda i, j, kk: (i, kk))
        b_spec = pl.BlockSpec((tk, tn), lambda i, j, kk: (kk, j))
    elif mode == "nt":
        a_spec = pl.BlockSpec((tm, tk), lambda i, j, kk: (i, kk))
        b_spec = pl.BlockSpec((tn, tk), lambda i, j, kk: (j, kk))
    else:
        a_spec = pl.BlockSpec((tk, tm), lambda i, j, kk: (kk, i))
        b_spec = pl.BlockSpec((tk, tn), lambda i, j, kk: (kk, j))
    o_spec = pl.BlockSpec((tm, tn), lambda i, j, kk: (i, j))
    outs = pl.pallas_call(
        body,
        out_shape=[jax.ShapeDtypeStruct((m, n), dt) for dt in out_dtypes],
        grid=(m // tm, n // tn, nk),
        in_specs=[a_spec, b_spec] + [o_spec] * n_extra,
        out_specs=[o_spec] * n_out,
        scratch_shapes=[pltpu.VMEM((tm, tn), F32)] if nk > 1 else [],
        compiler_params=_params("parallel", "parallel", "arbitrary"),
        name=name,
    )(a, b, *extras)
    return outs[0] if n_out == 1 else outs


def _add_epilogue(acc, res):
    return (res + acc,)


def _rmsnorm_fwd(x, g, *, name):
    p, d = x.shape
    rows = _pick(p, (384, 128, 96))

    def body(x_ref, g_ref, o_ref):
        xv = x_ref[...]
        r = lax.rsqrt(jnp.mean(xv * xv, axis=-1, keepdims=True) + EPS)
        o_ref[...] = (xv * r * g_ref[...]).astype(o_ref.dtype)

    return pl.pallas_call(
        body,
        out_shape=jax.ShapeDtypeStruct((p, d), BF16),
        grid=(p // rows,),
        in_specs=[pl.BlockSpec((rows, d), lambda i: (i, 0)), pl.BlockSpec((1, d), lambda i: (0, 0))],
        out_specs=pl.BlockSpec((rows, d), lambda i: (i, 0)),
        compiler_params=_params("parallel"),
        name=name,
    )(x, g)


def _rmsnorm_bwd(x, g, dy, dres, *, name):
    p, d = x.shape
    rows = _pick(p, (384, 128, 96))

    def body(x_ref, g_ref, dy_ref, dres_ref, dx_ref, dg_ref):
        xv = x_ref[...]
        r = lax.rsqrt(jnp.mean(xv * xv, axis=-1, keepdims=True) + EPS)
        dyv = dy_ref[...]
        gdy = dyv * g_ref[...]
        proj = jnp.mean(xv * gdy, axis=-1, keepdims=True)
        dx_ref[...] = dres_ref[...] + r * gdy - xv * (r * r * r) * proj
        part = jnp.sum(dyv * xv * r, axis=0, keepdims=True)

        @pl.when(pl.program_id(0) == 0)
        def _():
            dg_ref[...] = part

        @pl.when(pl.program_id(0) > 0)
        def _():
            dg_ref[...] += part

    row_spec = pl.BlockSpec((rows, d), lambda i: (i, 0))
    vec_spec = pl.BlockSpec((1, d), lambda i: (0, 0))
    return pl.pallas_call(
        body,
        out_shape=[jax.ShapeDtypeStruct((p, d), F32), jax.ShapeDtypeStruct((1, d), F32)],
        grid=(p // rows,),
        in_specs=[row_spec, vec_spec, row_spec, row_spec],
        out_specs=[row_spec, vec_spec],
        compiler_params=_params("arbitrary"),
        name=name,
    )(x, g, dy, dres)


def _mlp_fwd(h, g, w1, w2, *, name):
    hn = _rmsnorm_fwd(h, g, name=name + "_norm")

    def act(acc):
        r = jnp.maximum(acc, 0.0)
        return acc, r * r

    z, a2 = _matmul(hn, w1, mode="nn", out_dtypes=(F32, BF16), epilogue=act, name=name + "_up")
    out = _matmul(a2, w2, mode="nn", out_dtypes=(F32,), epilogue=_add_epilogue, extras=(h,), name=name + "_down")
    return out, (hn, z, a2)


def _mlp_bwd(h, g, w1, w2, saved, dout, *, name):
    hn, z, a2 = saved

    def dact(acc, zt):
        return (acc * (2.0 * jnp.maximum(zt, 0.0)),)

    dz = _matmul(dout, w2, mode="nt", out_dtypes=(BF16,), epilogue=dact, extras=(z,), name=name + "_dz")
    dw2 = _matmul(a2, dout, mode="tn", out_dtypes=(F32,), name=name + "_dw2")
    dw1 = _matmul(hn, dz, mode="tn", out_dtypes=(F32,), name=name + "_dw1")
    dhn = _matmul(dz, w1, mode="nt", out_dtypes=(F32,), name=name + "_dhn")
    dh, dg = _rmsnorm_bwd(h, g, dhn, dout, name=name + "_dnorm")
    return dh, dg, dw1, dw2


def _retention_tables(p):
    half = RET_QK_DIM // 2
    inv_freq = ROPE_BASE ** (-jnp.arange(half, dtype=F32) / half)
    ang = jnp.arange(p, dtype=F32)[:, None] * inv_freq[None, :]
    cos, sin = jnp.cos(ang), jnp.sin(ang)
    cosf = jnp.concatenate([cos, cos], axis=1)
    sins = jnp.concatenate([-sin, sin], axis=1)
    log_g = jnp.log1p(-jnp.exp2(-RET_DECAY_OFFSET - jnp.arange(RET_HEADS, dtype=F32)))
    idx = jnp.arange(CHUNK, dtype=F32)
    diff = idx[:, None] - idx[None, :]
    inner = jnp.where(diff[None] >= 0, jnp.exp(jnp.maximum(diff, 0.0)[None] * log_g[:, None, None]), 0.0)
    kdec = jnp.exp((CHUNK - 1 - idx)[None, :] * log_g[:, None])
    qdec = jnp.exp((idx + 1.0)[None, :] * log_g[:, None])
    cdec = jnp.exp(CHUNK * log_g)
    kdec = jnp.broadcast_to(kdec[:, :, None], (RET_HEADS, CHUNK, RET_QK_DIM))
    qdec = jnp.broadcast_to(qdec[:, :, None], (RET_HEADS, CHUNK, RET_QK_DIM))
    cdec = jnp.broadcast_to(cdec[:, None, None], (RET_HEADS, RET_QK_DIM, RET_V_DIM))
    return cosf, sins, inner, kdec, qdec, cdec


def _rot(x, cosf, sins):
    return x * cosf + pltpu.roll(x, RET_QK_DIM // 2, 1) * sins


def _rot_bwd(dy, cosf, sins):
    return dy * cosf + pltpu.roll(dy * sins, RET_QK_DIM // 2, 1)


def _ret_in_specs(chunk_of):
    nh = RET_HEADS
    q_spec = pl.BlockSpec((CHUNK, RET_QK_DIM), lambda h, s: (chunk_of(s), h))
    k_spec = pl.BlockSpec((CHUNK, RET_QK_DIM), lambda h, s: (chunk_of(s), nh + h))
    v_spec = pl.BlockSpec((CHUNK, RET_V_DIM), lambda h, s: (chunk_of(s), nh + h))
    g_spec = pl.BlockSpec((CHUNK, RET_V_DIM), lambda h, s: (chunk_of(s), 2 * nh + h))
    rope_spec = pl.BlockSpec((CHUNK, RET_QK_DIM), lambda h, s: (chunk_of(s), 0))
    head_sq = pl.BlockSpec((None, CHUNK, CHUNK), lambda h, s: (h, 0, 0))
    head_qk = pl.BlockSpec((None, CHUNK, RET_QK_DIM), lambda h, s: (h, 0, 0))
    head_st = pl.BlockSpec((None, RET_QK_DIM, RET_V_DIM), lambda h, s: (h, 0, 0))
    gam_spec = pl.BlockSpec((None, 1, RET_V_DIM), lambda h, s: (h, 0, 0))
    return [q_spec, k_spec, v_spec, g_spec, rope_spec, rope_spec, head_sq, head_qk, head_qk, head_st, gam_spec]


def _retention_fwd(proj, gn_g, tables, *, name):
    p = proj.shape[0]
    n_chunks = p // CHUNK
    scale = RET_QK_DIM ** -0.5

    def body(q_ref, k_ref, v_ref, g_ref, cos_ref, sin_ref, inner_ref, kdec_ref, qdec_ref, cdec_ref, gam_ref,
             og_ref, opre_ref, sprev_ref, s_scr):
        @pl.when(pl.program_id(1) == 0)
        def _():
            s_scr[...] = jnp.zeros_like(s_scr)

        cosf, sins = cos_ref[...], sin_ref[...]
        qr = _rot(q_ref[...], cosf, sins)
        kr = _rot(k_ref[...], cosf, sins) * scale
        vb = v_ref[...].astype(BF16)
        scores = _dot_nt(qr.astype(BF16), kr.astype(BF16)) * inner_ref[...]
        state = s_scr[...]
        sprev_ref[...] = state
        o = _dot(scores.astype(BF16), vb) + _dot((qr * qdec_ref[...]).astype(BF16), state.astype(BF16))
        kd = kr * kdec_ref[...]
        s_scr[...] = cdec_ref[...] * state + _dot(kd.T.astype(BF16), vb)
        opre_ref[...] = o
        mu = jnp.mean(o, axis=-1, keepdims=True)
        oc = o - mu
        var = jnp.mean(oc * oc, axis=-1, keepdims=True)
        on = oc * lax.rsqrt(var + EPS) * gam_ref[...]
        gv = g_ref[...]
        og_ref[...] = (gv * _sigmoid(gv) * on).astype(og_ref.dtype)

    chunk_of = lambda s: s
    out_v = pl.BlockSpec((CHUNK, RET_V_DIM), lambda h, s: (s, h))
    return pl.pallas_call(
        body,
        out_shape=[
            jax.ShapeDtypeStruct((p, RET_V_W), BF16),
            jax.ShapeDtypeStruct((p, RET_V_W), F32),
            jax.ShapeDtypeStruct((RET_HEADS, n_chunks, RET_QK_DIM, RET_V_DIM), F32),
        ],
        grid=(RET_HEADS, n_chunks),
        in_specs=_ret_in_specs(chunk_of),
        out_specs=[out_v, out_v, pl.BlockSpec((None, None, RET_QK_DIM, RET_V_DIM), lambda h, s: (h, s, 0, 0))],
        scratch_shapes=[pltpu.VMEM((RET_QK_DIM, RET_V_DIM), F32)],
        compiler_params=_params("parallel", "arbitrary"),
        name=name,
    )(proj, proj, proj, proj, *tables, gn_g.reshape(RET_HEADS, 1, RET_V_DIM))


def _retention_bwd(proj, gn_g, tables, opre, sprev, dog, *, name):
    p = proj.shape[0]
    n_chunks = p // CHUNK
    scale = RET_QK_DIM ** -0.5

    def body(q_ref, k_ref, v_ref, g_ref, cos_ref, sin_ref, inner_ref, kdec_ref, qdec_ref, cdec_ref, gam_ref,
             opre_ref, sprev_ref, dog_ref, dq_ref, dk_ref, dv_ref, dg_ref, dgam_ref, ds_scr):
        first = pl.program_id(1) == 0

        @pl.when(first)
        def _():
            ds_scr[...] = jnp.zeros_like(ds_scr)

        cosf, sins = cos_ref[...], sin_ref[...]
        qr = _rot(q_ref[...], cosf, sins)
        kr = _rot(k_ref[...], cosf, sins) * scale
        qb, kb = qr.astype(BF16), kr.astype(BF16)
        vb = v_ref[...].astype(BF16)
        inner = inner_ref[...]
        qdec, kdec = qdec_ref[...], kdec_ref[...]
        state_b = sprev_ref[...].astype(BF16)
        o = opre_ref[...]
        mu = jnp.mean(o, axis=-1, keepdims=True)
        oc = o - mu
        rstd = lax.rsqrt(jnp.mean(oc * oc, axis=-1, keepdims=True) + EPS)
        xhat = oc * rstd
        gam = gam_ref[...]
        on = xhat * gam
        gv = g_ref[...]
        sig = _sigmoid(gv)
        dogv = dog_ref[...]
        dg_ref[...] = (dogv * on * sig * (1.0 + gv * (1.0 - sig))).astype(dg_ref.dtype)
        don = dogv * gv * sig
        dgam_part = jnp.sum(don * xhat, axis=0, keepdims=True)

        @pl.when(first)
        def _():
            dgam_ref[...] = dgam_part

        @pl.when(jnp.logical_not(first))
        def _():
            dgam_ref[...] += dgam_part

        dxhat = don * gam
        do = rstd * (dxhat - jnp.mean(dxhat, axis=-1, keepdims=True)
                     - xhat * jnp.mean(dxhat * xhat, axis=-1, keepdims=True))
        dob = do.astype(BF16)
        scores_b = (_dot_nt(qb, kb) * inner).astype(BF16)
        da = (_dot_nt(dob, vb) * inner).astype(BF16)
        dv = _dot(scores_b.astype(F32).T.astype(BF16), dob)
        dqr = _dot(da, kb)
        dkr = _dot(da.astype(F32).T.astype(BF16), qb)
        dqr += _dot_nt(dob, state_b) * qdec
        ds_local = _dot((qr * qdec).T.astype(BF16), dob)
        gstate = ds_scr[...]
        gb = gstate.astype(BF16)
        kd_b = (kr * kdec).astype(BF16)
        dkr += _dot_nt(vb, gb) * kdec
        dv += _dot(kd_b, gb)
        ds_scr[...] = cdec_ref[...] * gstate + ds_local
        dq_ref[...] = _rot_bwd(dqr, cosf, sins).astype(dq_ref.dtype)
        dk_ref[...] = _rot_bwd(dkr * scale, cosf, sins).astype(dk_ref.dtype)
        dv_ref[...] = dv.astype(dv_ref.dtype)

    chunk_of = lambda s: n_chunks - 1 - s
    blk_v = pl.BlockSpec((CHUNK, RET_V_DIM), lambda h, s: (chunk_of(s), h))
    blk_qk = pl.BlockSpec((CHUNK, RET_QK_DIM), lambda h, s: (chunk_of(s), h))
    st_spec = pl.BlockSpec((None, None, RET_QK_DIM, RET_V_DIM), lambda h, s: (h, chunk_of(s), 0, 0))
    return pl.pallas_call(
        body,
        out_shape=[
            jax.ShapeDtypeStruct((p, RET_QK_W), BF16),
            jax.ShapeDtypeStruct((p, RET_QK_W), BF16),
            jax.ShapeDtypeStruct((p, RET_V_W), BF16),
            jax.ShapeDtypeStruct((p, RET_V_W), BF16),
            jax.ShapeDtypeStruct((RET_HEADS, 1, RET_V_DIM), F32),
        ],
        grid=(RET_HEADS, n_chunks),
        in_specs=_ret_in_specs(chunk_of) + [blk_v, st_spec, blk_v],
        out_specs=[blk_qk, blk_qk, blk_v, blk_v, pl.BlockSpec((None, 1, RET_V_DIM), lambda h, s: (h, 0, 0))],
        scratch_shapes=[pltpu.VMEM((RET_QK_DIM, RET_V_DIM), F32)],
        compiler_params=_params("parallel", "arbitrary"),
        name=name,
    )(proj, proj, proj, proj, *tables, gn_g.reshape(RET_HEADS, 1, RET_V_DIM), opre, sprev, dog)


def _conv_rows(p):
    return _pick(p, (384, 128))


def _ln_stats(y):
    mu = jnp.mean(y, axis=-1, keepdims=True)
    yc = y - mu
    rstd = lax.rsqrt(jnp.mean(yc * yc, axis=-1, keepdims=True) + EPS)
    return yc * rstd, rstd


def _conv_fwd(proj, conv_w, conv_b, ln_g, ln_b, *, name):
    p = proj.shape[0]
    c = D_MODEL
    rows = _conv_rows(p)
    hpb = rows // CONV_HALO
    a_col, gate_col = (2 * RET_QK_W + 2 * RET_V_W) // c, (2 * RET_QK_W + 2 * RET_V_W) // c + 1

    def body(a_ref, gate_ref, ah_ref, gateh_ref, w_ref, b_ref, lg_ref, lb_ref, c_ref, y_ref, hdn_scr):
        i = pl.program_id(0)
        hdn_scr[0:CONV_HALO, :] = ah_ref[...] * _sigmoid(gateh_ref[...])
        hdn_scr[CONV_HALO:, :] = a_ref[...] * _sigmoid(gate_ref[...])
        acc = jnp.zeros((rows, c), F32)
        for w in range(CONV_WIDTH):
            off = CONV_HALO - (CONV_WIDTH - 1) + w
            acc += hdn_scr[off:off + rows, :] * w_ref[w:w + 1, :]
        y = acc + b_ref[...]
        y_ref[...] = y
        yhat, _ = _ln_stats(y)
        ln = yhat * lg_ref[...] + lb_ref[...]
        row = i * rows + lax.broadcasted_iota(jnp.int32, (rows, 1), 0)
        c_ref[...] = jnp.where(row >= PAD_FRONT, ln * _sigmoid(ln), 0.0).astype(c_ref.dtype)

    halo_idx = lambda i: jnp.maximum(i * hpb - 1, 0)
    vec = pl.BlockSpec((1, c), lambda i: (0, 0))
    return pl.pallas_call(
        body,
        out_shape=[jax.ShapeDtypeStruct((p, c), BF16), jax.ShapeDtypeStruct((p, c), F32)],
        grid=(p // rows,),
        in_specs=[
            pl.BlockSpec((rows, c), lambda i: (i, a_col)),
            pl.BlockSpec((rows, c), lambda i: (i, gate_col)),
            pl.BlockSpec((CONV_HALO, c), lambda i: (halo_idx(i), a_col)),
            pl.BlockSpec((CONV_HALO, c), lambda i: (halo_idx(i), gate_col)),
            pl.BlockSpec((CONV_WIDTH, c), lambda i: (0, 0)),
            vec, vec, vec,
        ],
        out_specs=[pl.BlockSpec((rows, c), lambda i: (i, 0)), pl.BlockSpec((rows, c), lambda i: (i, 0))],
        scratch_shapes=[pltpu.VMEM((CONV_HALO + rows, c), F32)],
        compiler_params=_params("parallel"),
        name=name,
    )(proj, proj, proj, proj, conv_w, conv_b, ln_g, ln_b)


def _conv_bwd(proj, conv_w, ln_g, ln_b, y, dcat, *, name):
    p = proj.shape[0]
    c = D_MODEL
    rows = _conv_rows(p)
    hpb = rows // CONV_HALO
    n_blocks = p // rows
    a_col, gate_col = (2 * RET_QK_W + 2 * RET_V_W) // c, (2 * RET_QK_W + 2 * RET_V_W) // c + 1

    def body(a_ref, gate_ref, ah_ref, gateh_ref, w_ref, lg_ref, lb_ref, y_ref, yh_ref, dc_ref, dch_ref,
             da_ref, dgate_ref, dw_ref, db_ref, dlg_ref, dlb_ref, hdn_scr, dy_scr):
        i = pl.program_id(0)
        lg, lb = lg_ref[...], lb_ref[...]

        def ln_bwd(yv, dcv):
            yhat, rstd = _ln_stats(yv)
            ln = yhat * lg + lb
            sig = _sigmoid(ln)
            dln = dcv * sig * (1.0 + ln * (1.0 - sig))
            dyhat = dln * lg
            dyv = rstd * (dyhat - jnp.mean(dyhat, axis=-1, keepdims=True)
                          - yhat * jnp.mean(dyhat * yhat, axis=-1, keepdims=True))
            return dyv, dln, yhat

        row = i * rows + lax.broadcasted_iota(jnp.int32, (rows, 1), 0)
        dy, dln, yhat = ln_bwd(y_ref[...], jnp.where(row >= PAD_FRONT, dc_ref[...], 0.0))
        dy_halo, _, _ = ln_bwd(yh_ref[...], dch_ref[...])
        dy_scr[0:rows, :] = dy
        dy_scr[rows:, :] = jnp.where(i == n_blocks - 1, 0.0, dy_halo)
        sig_gate = _sigmoid(gate_ref[...])
        av = a_ref[...]
        hdn_scr[0:CONV_HALO, :] = ah_ref[...] * _sigmoid(gateh_ref[...])
        hdn_scr[CONV_HALO:, :] = av * sig_gate
        @pl.when(i == 0)
        def _():
            dw_ref[...] = jnp.zeros_like(dw_ref)
            db_ref[...] = jnp.zeros_like(db_ref)
            dlg_ref[...] = jnp.zeros_like(dlg_ref)
            dlb_ref[...] = jnp.zeros_like(dlb_ref)

        dhdn = jnp.zeros((rows, c), F32)
        for w in range(CONV_WIDTH):
            back = CONV_WIDTH - 1 - w
            dhdn += dy_scr[back:back + rows, :] * w_ref[w:w + 1, :]
            off = CONV_HALO - (CONV_WIDTH - 1) + w
            dw_ref[w:w + 1, :] += jnp.sum(dy * hdn_scr[off:off + rows, :], axis=0, keepdims=True)
        da_ref[...] = (dhdn * sig_gate).astype(da_ref.dtype)
        dgate_ref[...] = (dhdn * av * sig_gate * (1.0 - sig_gate)).astype(dgate_ref.dtype)
        db_ref[...] += jnp.sum(dy, axis=0, keepdims=True)
        dlg_ref[...] += jnp.sum(dln * yhat, axis=0, keepdims=True)
        dlb_ref[...] += jnp.sum(dln, axis=0, keepdims=True)

    prev_halo = lambda i: jnp.maximum(i * hpb - 1, 0)
    next_halo = lambda i: jnp.minimum((i + 1) * hpb, p // CONV_HALO - 1)
    vec = pl.BlockSpec((1, c), lambda i: (0, 0))
    blk = lambda col: pl.BlockSpec((rows, c), lambda i: (i, col))
    outs = pl.pallas_call(
        body,
        out_shape=[
            jax.ShapeDtypeStruct((p, c), BF16),
            jax.ShapeDtypeStruct((p, c), BF16),
            jax.ShapeDtypeStruct((CONV_WIDTH + 1, c), F32),
            jax.ShapeDtypeStruct((1, c), F32),
            jax.ShapeDtypeStruct((1, c), F32),
            jax.ShapeDtypeStruct((1, c), F32),
        ],
        grid=(n_blocks,),
        in_specs=[
            blk(a_col), blk(gate_col),
            pl.BlockSpec((CONV_HALO, c), lambda i: (prev_halo(i), a_col)),
            pl.BlockSpec((CONV_HALO, c), lambda i: (prev_halo(i), gate_col)),
            pl.BlockSpec((CONV_WIDTH, c), lambda i: (0, 0)),
            vec, vec,
            blk(0),
            pl.BlockSpec((CONV_HALO, c), lambda i: (next_halo(i), 0)),
            blk(1),
            pl.BlockSpec((CONV_HALO, c), lambda i: (next_halo(i), 1)),
        ],
        out_specs=[blk(0), blk(0), pl.BlockSpec((CONV_WIDTH + 1, c), lambda i: (0, 0)), vec, vec, vec],
        scratch_shapes=[pltpu.VMEM((CONV_HALO + rows, c), F32), pltpu.VMEM((rows + CONV_HALO, c), F32)],
        compiler_params=_params("arbitrary"),
        name=name,
    )(proj, proj, proj, proj, conv_w, ln_g, ln_b, y, y, dcat, dcat)
    da, dgate, dw, db, dlg, dlb = outs
    return da, dgate, dw[:CONV_WIDTH], db, dlg, dlb


def _group_matrix():
    r = jnp.arange(D_MODEL)[:, None] // SB_HEAD_DIM
    c = jnp.arange(D_MODEL)[None, :] // SB_HEAD_DIM
    return (r == c).astype(BF16)


def _qknorm_fwd(qkv, qg, kg, *, name):
    p = qkv.shape[0]
    d = D_MODEL
    rows = _pick(p, (384, 128, 96))

    def body(q_ref, k_ref, v_ref, qg_ref, kg_ref, gm_ref, qn_ref, kn_ref, vb_ref):
        gm = gm_ref[...]

        def norm(x, g):
            ms = _split_dot(x * x, gm) * (1.0 / SB_HEAD_DIM)
            return x * lax.rsqrt(ms + EPS) * g

        qn_ref[...] = norm(q_ref[...], qg_ref[...]).astype(BF16)
        kn_ref[...] = norm(k_ref[...], kg_ref[...]).astype(BF16)
        vb_ref[...] = v_ref[...].astype(BF16)

    blk = lambda col: pl.BlockSpec((rows, d), lambda i: (i, col))
    vec = pl.BlockSpec((1, d), lambda i: (0, 0))
    return pl.pallas_call(
        body,
        out_shape=[jax.ShapeDtypeStruct((p, d), BF16)] * 3,
        grid=(p // rows,),
        in_specs=[blk(0), blk(1), blk(2), vec, vec, pl.BlockSpec((d, d), lambda i: (0, 0))],
        out_specs=[blk(0)] * 3,
        compiler_params=_params("parallel"),
        name=name,
    )(qkv, qkv, qkv, qg, kg, _group_matrix())


def _qknorm_bwd(qkv, qg, kg, dqn, dkn, dv, *, name):
    p = qkv.shape[0]
    d = D_MODEL
    rows = _pick(p, (384, 128, 96))

    def body(q_ref, k_ref, qg_ref, kg_ref, gm_ref, dqn_ref, dkn_ref, dv_ref, dqkv_ref, dqg_ref, dkg_ref):
        gm = gm_ref[...]

        def bwd(x, g, dy):
            ms = _split_dot(x * x, gm) * (1.0 / SB_HEAD_DIM)
            r = lax.rsqrt(ms + EPS)
            gdy = dy * g
            proj = _split_dot(x * gdy, gm) * (1.0 / SB_HEAD_DIM)
            return r * gdy - x * (r * r * r) * proj, jnp.sum(dy * x * r, axis=0, keepdims=True)

        dq, dqg = bwd(q_ref[...], qg_ref[...], dqn_ref[...])
        dk, dkg = bwd(k_ref[...], kg_ref[...], dkn_ref[...])
        dqkv_ref[:, 0:d] = dq.astype(BF16)
        dqkv_ref[:, d:2 * d] = dk.astype(BF16)
        dqkv_ref[:, 2 * d:3 * d] = dv_ref[...].astype(BF16)

        @pl.when(pl.program_id(0) == 0)
        def _():
            dqg_ref[...] = dqg
            dkg_ref[...] = dkg

        @pl.when(pl.program_id(0) > 0)
        def _():
            dqg_ref[...] += dqg
            dkg_ref[...] += dkg

    blk = lambda col: pl.BlockSpec((rows, d), lambda i: (i, col))
    vec = pl.BlockSpec((1, d), lambda i: (0, 0))
    return pl.pallas_call(
        body,
        out_shape=[jax.ShapeDtypeStruct((p, 3 * d), BF16), jax.ShapeDtypeStruct((1, d), F32),
                   jax.ShapeDtypeStruct((1, d), F32)],
        grid=(p // rows,),
        in_specs=[blk(0), blk(1), vec, vec, pl.BlockSpec((d, d), lambda i: (0, 0)), blk(0), blk(0), blk(0)],
        out_specs=[pl.BlockSpec((rows, 3 * d), lambda i: (i, 0)), vec, vec],
        compiler_params=_params("arbitrary"),
        name=name,
    )(qkv, qkv, qg, kg, _group_matrix(), dqn, dkn, dv)


SB_PAIR = 2 * SB_HEAD_DIM
SB_GROUP = 8
SB_MASKED = -1e30


def _sb_consts():
    lane = lax.broadcasted_iota(jnp.int32, (CHUNK, SB_PAIR), 1)
    r = lax.broadcasted_iota(jnp.int32, (CHUNK, CHUNK), 0)
    c = lax.broadcasted_iota(jnp.int32, (CHUNK, CHUNK), 1)
    lo = (lane < SB_HEAD_DIM).astype(F32).astype(BF16)
    ones = jnp.ones((CHUNK, CHUNK), BF16)
    twice = lambda m: jnp.concatenate([jnp.concatenate([m, ones], axis=1)] * 2, axis=0)
    later, earlier = twice((r > c).astype(BF16)), twice((r < c).astype(BF16))
    not_before = (c >= r).astype(F32) * SB_MASKED
    padding = (c < PAD_FRONT).astype(F32) * SB_MASKED
    return (lo, 1.0 - lo), c, later, earlier, not_before, padding


def _sb_halves(t, head_lanes):
    return t * head_lanes[0], t * head_lanes[1]


def _sb_logits(qh, kg, biases):
    z = _dot_nt(qh, kg)
    tiles = []
    for b, bias in enumerate(biases):
        zt = z[:, b * CHUNK:(b + 1) * CHUNK]
        if bias is not None:
            zt = zt + bias
        ls_pos = jnp.minimum(zt, 0.0) - jnp.log(1.0 + jnp.exp(-jnp.abs(zt)))
        tiles.append((ls_pos, ls_pos - zt))
    return tiles


def _sb_block_sums(tiles, m):
    st = jnp.concatenate(tiles, axis=0)
    hi = st.astype(BF16)
    lo = (st - hi.astype(F32)).astype(BF16)
    tot = _dot(jnp.concatenate([hi, lo], axis=1), m)
    return [(tot[i * CHUNK:(i + 1) * CHUNK, 0:CHUNK], tot[i * CHUNK:(i + 1) * CHUNK, CHUNK:2 * CHUNK])
            for i in range(len(tiles))]


def _sb_plan(qi, padding, not_before):
    top = lax.div(qi, SB_GROUP)
    size = qi - SB_GROUP * top + 1
    pad_if_first = padding * (top == 0).astype(F32)
    masks = []
    for n_b in range(1, SB_GROUP + 1):
        m = [None] * n_b
        m[n_b - 1] = not_before
        m[0] = pad_if_first if m[0] is None else m[0] + pad_if_first
        masks.append(m)
    return top, size, masks


def _once_if(cond, fn, carry):
    return lax.fori_loop(0, jnp.where(cond, 1, 0), lambda s, cr: fn(cr), carry)


def _sb_head_rows(tg, lanes, n_b):
    return jnp.concatenate([tg[b * CHUNK:(b + 1) * CHUNK] * lanes for b in range(n_b)], axis=0)


def _sb_fwd(qn, kn, vb, *, name):
    p = qn.shape[0]
    n_blocks = p // CHUNK
    n_pairs = SB_HEADS // 2
    scale = SB_HEAD_DIM ** -0.5

    def body(q_ref, k_ref, v_ref, o_ref, car_ref):
        head_lanes, c, later, _, not_before, padding = _sb_consts()

        def q_block(qi, _):
            rows = pl.ds(pl.multiple_of(qi * CHUNK, CHUNK), CHUNK)
            qh = _sb_halves(q_ref[rows, :], head_lanes)
            qs = (qh[0] * scale, qh[1] * scale)

            def blocks(kb0, biases, carry):
                n_b = len(biases)
                acc, run0, run1, sav0, sav1 = carry
                krows = pl.ds(pl.multiple_of(kb0 * CHUNK, CHUNK), n_b * CHUNK)
                kg, vg = k_ref[krows, :], v_ref[krows, :]
                tiles = [_sb_logits(qs[h], kg, biases) for h in range(2)]
                sums = [_sb_block_sums([log_keep for _, log_keep in tiles[h]], later) for h in range(2)]
                cols = [(c == kb0 + b).astype(F32) for b in range(n_b)]
                runs, savs = [run0, run1], [sav0, sav1]
                for h in range(2):
                    ws = [None] * n_b
                    for b in reversed(range(n_b)):
                        after, row_sum = sums[h][b]
                        ws[b] = jnp.exp(tiles[h][b][0] + after + runs[h]).astype(BF16)
                        savs[h] = savs[h] + cols[b] * runs[h]
                        runs[h] = runs[h] + row_sum
                    acc = acc + _dot(jnp.concatenate(ws, axis=1), _sb_head_rows(vg, head_lanes[h], n_b))
                return acc, runs[0], runs[1], savs[0], savs[1]

            zt = qh[0].astype(F32) * 0.0
            top, size, masks = _sb_plan(qi, padding, not_before)
            carry = (zt, zt, zt, zt, zt)
            for m in masks:
                carry = _once_if(size == len(m), functools.partial(blocks, SB_GROUP * top, m), carry)
            carry = lax.fori_loop(0, jnp.maximum(top - 1, 0),
                                  lambda it, cr: blocks(SB_GROUP * (top - 1 - it), [None] * SB_GROUP, cr), carry)
            carry = _once_if(top > 0, functools.partial(blocks, 0, [padding] + [None] * (SB_GROUP - 1)), carry)
            acc, _, _, sav0, sav1 = carry
            o_ref[rows, :] = acc.astype(o_ref.dtype)
            car_ref[rows, 0:CHUNK] = sav0
            car_ref[rows, CHUNK:2 * CHUNK] = sav1
            return 0

        lax.fori_loop(0, n_blocks, q_block, 0)

    col = pl.BlockSpec((p, SB_PAIR), lambda g: (0, g))
    return pl.pallas_call(
        body,
        out_shape=[jax.ShapeDtypeStruct((p, D_MODEL), BF16), jax.ShapeDtypeStruct((p, n_pairs * 2 * CHUNK), F32)],
        grid=(n_pairs,),
        in_specs=[col, col, col],
        out_specs=[col, pl.BlockSpec((p, 2 * CHUNK), lambda g: (0, g))],
        compiler_params=_params("parallel"),
        name=name,
    )(qn, kn, vb)


def _sb_bwd(qn, kn, vb, carries, do, *, name):
    p = qn.shape[0]
    n_blocks = p // CHUNK
    n_pairs = SB_HEADS // 2
    scale = SB_HEAD_DIM ** -0.5

    def body(q_ref, k_ref, v_ref, car_ref, do_ref, dq_ref, dk_ref, dv_ref):
        head_lanes, c, later, earlier, not_before, padding = _sb_consts()
        dk_ref[...] = jnp.zeros_like(dk_ref)
        dv_ref[...] = jnp.zeros_like(dv_ref)

        def q_block(qi, _):
            rows = pl.ds(pl.multiple_of(qi * CHUNK, CHUNK), CHUNK)
            qh = _sb_halves(q_ref[rows, :], head_lanes)
            qs = (qh[0] * scale, qh[1] * scale)
            doh = _sb_halves(do_ref[rows, :].astype(BF16), head_lanes)
            sav = (car_ref[rows, 0:CHUNK], car_ref[rows, CHUNK:2 * CHUNK])

            def blocks(kb0, biases, carry):
                n_b = len(biases)
                dq_acc, pre0, pre1 = carry
                krows = pl.ds(pl.multiple_of(kb0 * CHUNK, CHUNK), n_b * CHUNK)
                kg, vg = k_ref[krows, :], v_ref[krows, :]
                cols = [(c == kb0 + b).astype(F32) for b in range(n_b)]
                block = lambda t, b: t[:, b * CHUNK:(b + 1) * CHUNK]
                tiles = [_sb_logits(qs[h], kg, biases) for h in range(2)]
                afters = [_sb_block_sums([log_keep for _, log_keep in tiles[h]], later) for h in range(2)]
                dws = [_dot_nt(doh[h], vg) for h in range(2)]
                ws, es, befores = [], [], []
                for h in range(2):
                    runs = [jnp.sum(cols[b] * sav[h], axis=-1, keepdims=True) for b in range(n_b)]
                    ws.append([jnp.exp(tiles[h][b][0] + afters[h][b][0] + runs[b]) for b in range(n_b)])
                    es.append([ws[h][b] * block(dws[h], b) for b in range(n_b)])
                    befores.append(_sb_block_sums(es[h], earlier))
                pres = [pre0, pre1]
                dk_add = dv_add = None
                for h in range(2):
                    dzs = []
                    for b in range(n_b):
                        before, row_sum = befores[h][b]
                        sig = jnp.exp(tiles[h][b][0])
                        e = es[h][b]
                        dzs.append((e - (e + before + pres[h]) * sig).astype(BF16))
                        pres[h] = pres[h] + row_sum
                    dz = jnp.concatenate(dzs, axis=1)
                    w = jnp.concatenate([t.astype(BF16) for t in ws[h]], axis=1)
                    dq_acc = dq_acc + _dot(dz, _sb_head_rows(kg, head_lanes[h], n_b))
                    dv_h, dk_h = _dot_tn(w, doh[h]), _dot_tn(dz, qs[h])
                    dv_add = dv_h if dv_add is None else dv_add + dv_h
                    dk_add = dk_h if dk_add is None else dk_add + dk_h
                dv_ref[krows, :] += dv_add
                dk_ref[krows, :] += dk_add
                return dq_acc, pres[0], pres[1]

            zt = qh[0].astype(F32) * 0.0
            top, size, masks = _sb_plan(qi, padding, not_before)
            carry = _once_if(top > 0, functools.partial(blocks, 0, [padding] + [None] * (SB_GROUP - 1)), (zt, zt, zt))
            carry = lax.fori_loop(1, top, lambda g, cr: blocks(SB_GROUP * g, [None] * SB_GROUP, cr), carry)
            for m in masks:
                carry = _once_if(size == len(m), functools.partial(blocks, SB_GROUP * top, m), carry)
            dq_acc, _, _ = carry
            dq_ref[rows, :] = dq_acc * scale
            return 0

        lax.fori_loop(0, n_blocks, q_block, 0)

    col = pl.BlockSpec((p, SB_PAIR), lambda g: (0, g))
    return pl.pallas_call(
        body,
        out_shape=[jax.ShapeDtypeStruct((p, D_MODEL), F32)] * 3,
        grid=(n_pairs,),
        in_specs=[col, col, col, pl.BlockSpec((p, 2 * CHUNK), lambda g: (0, g)), col],
        out_specs=[col, col, col],
        compiler_params=_params("parallel"),
        name=name,
    )(qn, kn, vb, carries, do)


def _loss_head(h, target, *, name):
    p, d = h.shape
    n_blocks = p // CHUNK

    def body(h_ref, t_ref, sq_ref, dh_ref):
        i = pl.program_id(0)

        @pl.when(i == 0)
        def _():
            sq_ref[...] = jnp.zeros_like(sq_ref)
            dh_ref[...] = jnp.zeros_like(dh_ref)

        @pl.when(i > 0)
        def _():
            err = h_ref[...] - t_ref[...]
            sq_ref[...] += jnp.sum(err * err)
            dh_ref[...] = err * (1.0 / d)

    return pl.pallas_call(
        body,
        out_shape=[jax.ShapeDtypeStruct((8, 128), F32), jax.ShapeDtypeStruct((p, d), F32)],
        grid=(n_blocks,),
        in_specs=[pl.BlockSpec((CHUNK, d), lambda i: (i, 0)),
                  pl.BlockSpec((CHUNK, d), lambda i: (jnp.maximum(i - 1, 0), 0))],
        out_specs=[pl.BlockSpec((8, 128), lambda i: (0, 0)), pl.BlockSpec((CHUNK, d), lambda i: (i, 0))],
        compiler_params=_params("arbitrary"),
        name=name,
    )(h, target)


def _local_step(x, target, meta, norm_mix_g, norm_mlp_g, w_in, gn_g, conv_w, conv_b, ln_g, ln_b, qn_g, kn_g, later):
    seq = x.shape[0]
    p = PAD_FRONT + N_META + seq
    d = D_MODEL
    tables = _retention_tables(p)
    row = lambda v: v.reshape(1, -1)
    h0 = jnp.concatenate([jnp.zeros((PAD_FRONT, d), F32), meta, x], axis=0)

    hn0 = _rmsnorm_fwd(h0, row(norm_mix_g[0]), name="l0_mix_norm")
    proj = _matmul(hn0, w_in, mode="nn", out_dtypes=(F32,), name="l0_proj")
    og, opre, sprev = _retention_fwd(proj, gn_g, tables, name="l0_retention")
    cb, y_conv = _conv_fwd(proj, conv_w, row(conv_b), row(ln_g), row(ln_b), name="l0_conv")
    cat = jnp.concatenate([og, cb], axis=1)
    w_out, w1_0, w2_0 = later("l0", cat)
    w1, w2 = [w1_0, None], [w2_0, None]
    h1 = _matmul(cat, w_out, mode="nn", out_dtypes=(F32,), epilogue=_add_epilogue, extras=(h0,), name="l0_mix_out")
    h2, mlp0 = _mlp_fwd(h1, row(norm_mlp_g[0]), w1[0], w2[0], name="l0_mlp")

    hn1 = _rmsnorm_fwd(h2, row(norm_mix_g[1]), name="l1_mix_norm")
    (w_qkv,) = later("qkv", hn1)
    qkv = _matmul(hn1, w_qkv, mode="nn", out_dtypes=(F32,), name="l1_qkv")
    qg_t, kg_t = jnp.tile(row(qn_g), (1, SB_HEADS)), jnp.tile(row(kn_g), (1, SB_HEADS))
    qn, kn, vb = _qknorm_fwd(qkv, qg_t, kg_t, name="l1_qknorm")
    o_sb, carries = _sb_fwd(qn, kn, vb, name="l1_stickbreak")
    w_o, w1[1], w2[1] = later("l1", o_sb)
    h3 = _matmul(o_sb, w_o, mode="nn", out_dtypes=(F32,), epilogue=_add_epilogue, extras=(h2,), name="l1_mix_out")
    h4, mlp1 = _mlp_fwd(h3, row(norm_mlp_g[1]), w1[1], w2[1], name="l1_mlp")

    sq, dh4 = _loss_head(h4, target, name="loss_head")

    dh3, dg_mlp1, dw1_1, dw2_1 = _mlp_bwd(h3, row(norm_mlp_g[1]), w1[1], w2[1], mlp1, dh4, name="l1_mlp_bwd")
    do_sb = _matmul(dh3, w_o, mode="nt", out_dtypes=(F32,), name="l1_do")
    dw_o = _matmul(o_sb, dh3, mode="tn", out_dtypes=(F32,), name="l1_dwo")
    dqn, dkn, dv = _sb_bwd(qn, kn, vb, carries, do_sb, name="l1_stickbreak_bwd")
    dqkv, dqg_t, dkg_t = _qknorm_bwd(qkv, qg_t, kg_t, dqn, dkn, dv, name="l1_qknorm_bwd")
    dw_qkv = _matmul(hn1, dqkv, mode="tn", out_dtypes=(F32,), name="l1_dwqkv")
    dhn1 = _matmul(dqkv, w_qkv, mode="nt", out_dtypes=(F32,), name="l1_dhn")
    dh2, dg_mix1 = _rmsnorm_bwd(h2, row(norm_mix_g[1]), dhn1, dh3, name="l1_mix_dnorm")

    dh1, dg_mlp0, dw1_0, dw2_0 = _mlp_bwd(h1, row(norm_mlp_g[0]), w1[0], w2[0], mlp0, dh2, name="l0_mlp_bwd")
    dcat = _matmul(dh1, w_out, mode="nt", out_dtypes=(F32,), name="l0_dcat")
    dw_out = _matmul(cat, dh1, mode="tn", out_dtypes=(F32,), name="l0_dwout")
    dq, dk, dvr, dgate_r, dgn = _retention_bwd(proj, gn_g, tables, opre, sprev, dcat, name="l0_retention_bwd")
    da, dgate_c, dconv_w, dconv_b, dln_g, dln_b = _conv_bwd(proj, conv_w, row(ln_g), row(ln_b), y_conv, dcat,
                                                            name="l0_conv_bwd")
    dproj = jnp.concatenate([dq, dk, dvr, dgate_r, da, dgate_c], axis=1)
    dw_in = _matmul(hn0, dproj, mode="tn", out_dtypes=(F32,), name="l0_dwin")
    dhn0 = _matmul(dproj, w_in, mode="nt", out_dtypes=(F32,), name="l0_dhn")
    dh0, dg_mix0 = _rmsnorm_bwd(h0, row(norm_mix_g[0]), dhn0, dh1, name="l0_mix_dnorm")

    fold = lambda t: t.reshape(SB_HEADS, SB_HEAD_DIM).sum(axis=0)
    grads = dict(
        x=dh0[PAD_FRONT + N_META:],
        meta=dh0[PAD_FRONT:PAD_FRONT + N_META],
        norm_mix_g=jnp.concatenate([dg_mix0, dg_mix1], axis=0),
        norm_mlp_g=jnp.concatenate([dg_mlp0, dg_mlp1], axis=0),
        even_w_in=dw_in,
        even_ret_gn_g=dgn.reshape(RET_HEADS, RET_V_DIM),
        even_conv_w=dconv_w,
        even_conv_b=dconv_b,
        even_conv_ln_g=dln_g,
        even_conv_ln_b=dln_b,
        even_w_out=dw_out,
        odd_w_qkv=dw_qkv,
        odd_q_norm_g=fold(dqg_t)[None],
        odd_k_norm_g=fold(dkg_t)[None],
        odd_w_o=dw_o,
        mlp_w1=(dw1_0, dw1_1),
        mlp_w2=(dw2_0, dw2_1),
    )
    return sq[0, 0], grads


def _position():
    x, y, c = lax.axis_index("x"), lax.axis_index("y"), lax.axis_index("c")
    other_chips = [(1 - x, y), (x, 1 - y), (1 - x, 1 - y)]
    return x, y, c, other_chips


def _shard_of(ref, kind, s, n):
    rows, cols = ref.shape
    if kind == "col":
        return ref.at[:, pl.ds(s * (cols // n), cols // n)]
    return ref.at[pl.ds(s * (rows // n), rows // n), :]


def _half_of(ref, kind, c):
    rows, cols = ref.shape
    if kind == "col":
        return ref.at[pl.ds(c * (rows // 2), rows // 2), :]
    return ref.at[:, pl.ds(c * (cols // 2), cols // 2)]


def _remote(src, dst, send_sems, recv_sems, idx, device):
    return pltpu.make_async_remote_copy(src_ref=src, dst_ref=dst, send_sem=send_sems.at[idx], recv_sem=recv_sems.at[idx],
                                        device_id=device, device_id_type=MESH)


def _cast_into_whole(w, kind, s_arr, *, name):
    rows, cols = w.shape
    tr = _pick(rows, (256, 128))
    nb = rows // tr
    if kind == "col":
        whole, o_spec = (rows, cols * N_CHIPS), pl.BlockSpec((tr, cols), lambda i, s_ref: (i, s_ref[0]))
    else:
        whole, o_spec = (rows * N_CHIPS, cols), pl.BlockSpec((tr, cols), lambda i, s_ref: (s_ref[0] * nb + i, 0))

    def body(s_ref, w_ref, o_ref):
        o_ref[...] = w_ref[...].astype(BF16)

    return pl.pallas_call(
        body,
        out_shape=jax.ShapeDtypeStruct(whole, BF16),
        grid_spec=pltpu.PrefetchScalarGridSpec(num_scalar_prefetch=1, grid=(nb,),
                                               in_specs=[pl.BlockSpec((tr, cols), lambda i, s_ref: (i, 0))],
                                               out_specs=o_spec),
        compiler_params=_params("parallel"),
        name=name,
    )(s_arr, w)


def _allgather_weights(wholes, kinds):
    n = len(wholes)

    def body(*refs):
        ins, outs = refs[:n], refs[n:2 * n]
        send_sems, recv_sems = refs[2 * n:]
        x, y, c, chips = _position()
        me_chip = 2 * x + y
        sibling = (x, y, 1 - c)
        sends = []
        for t in range(n):
            for k, (cx, cy) in enumerate(chips):
                src = _half_of(_shard_of(ins[t], kinds[t], me_chip, N_CHIPS), kinds[t], c)
                dst = _half_of(_shard_of(outs[t], kinds[t], me_chip, N_CHIPS), kinds[t], c)
                sends.append(_remote(src, dst, send_sems, recv_sems, 6 * t + k, (cx, cy, c)))
        for cp in sends:
            cp.start()
        passed = []
        for t in range(n):
            for k, (cx, cy) in enumerate(chips):
                landed = _half_of(_shard_of(outs[t], kinds[t], 2 * cx + cy, N_CHIPS), kinds[t], c)
                _remote(landed, landed, send_sems, recv_sems, 6 * t + k, (cx, cy, c)).wait_recv()
                fwd = _remote(landed, landed, send_sems, recv_sems, 6 * t + 3 + k, sibling)
                fwd.start()
                passed.append(fwd)
        for t in range(n):
            for k, (cx, cy) in enumerate(chips):
                theirs = _half_of(_shard_of(outs[t], kinds[t], 2 * cx + cy, N_CHIPS), kinds[t], 1 - c)
                _remote(theirs, theirs, send_sems, recv_sems, 6 * t + 3 + k, sibling).wait_recv()
        for cp in sends + passed:
            cp.wait_send()

    return pl.pallas_call(
        body,
        out_shape=[jax.ShapeDtypeStruct(w.shape, BF16) for w in wholes],
        in_specs=[ANY] * n,
        out_specs=[ANY] * n,
        input_output_aliases={t: t for t in range(n)},
        scratch_shapes=[pltpu.SemaphoreType.DMA((6 * n,)), pltpu.SemaphoreType.DMA((6 * n,))],
        name="allgather_weights",
    )(*wholes)


HBM = pl.BlockSpec(memory_space=pltpu.HBM)
SEM = pl.BlockSpec(memory_space=pltpu.SEMAPHORE)
DATAFLOW = pltpu.SideEffectType.DATAFLOW_SIDE_EFFECTING
TARGETS = 6


def _gather_copies(refs, kinds, send_sems, recv_sems):
    x, y, c, chips = _position()
    me_chip = 2 * x + y
    sends, lands = [], []
    for t, (ref, kind) in enumerate(zip(refs, kinds)):
        mine = _half_of(_shard_of(ref, kind, me_chip, N_CHIPS), kind, c)
        for k, (cx, cy) in enumerate(chips):
            for other_core in range(2):
                j = TARGETS * t + 2 * k + other_core
                peer_c = 1 - c if other_core else c
                sends.append(_remote(mine, mine, send_sems, recv_sems, j, (cx, cy, peer_c)))
                theirs = _half_of(_shard_of(ref, kind, 2 * cx + cy, N_CHIPS), kind, peer_c)
                lands.append(_remote(theirs, theirs, send_sems, recv_sems, j, (cx, cy, peer_c)))
    return sends, lands


def _gather_start(wholes, kinds, follows, *, name):
    n = len(wholes)

    def body(*refs):
        send_sems, recv_sems = refs[n + 1], refs[n + 2]
        thru, token = refs[n + 3:2 * n + 3], refs[2 * n + 3]
        sends, _ = _gather_copies(thru, kinds, send_sems, recv_sems)
        for cp in sends:
            cp.start()
        token[...] = jnp.zeros_like(token)

    outs = pl.pallas_call(
        body,
        name=name,
        out_shape=(pltpu.SemaphoreType.DMA((TARGETS * n,)), pltpu.SemaphoreType.DMA((TARGETS * n,)),
                   *[pltpu.HBM(w.shape, w.dtype) for w in wholes], jax.ShapeDtypeStruct((8, 128), F32)),
        in_specs=(*[HBM] * n, ANY),
        out_specs=(SEM, SEM, *[HBM] * n, pl.BlockSpec(memory_space=pltpu.VMEM)),
        input_output_aliases={t: 2 + t for t in range(n)},
        compiler_params=pltpu.CompilerParams(has_side_effects=DATAFLOW),
    )(*[pltpu.with_memory_space_constraint(w, pltpu.HBM) for w in wholes], follows)
    return outs[0], outs[1], outs[2:2 + n], outs[2 + n]


def _gather_wait(send_sems, recv_sems, thru, kinds, follows, *, name):
    n = len(thru)

    def body(*refs):
        ins, s_sems, r_sems = refs[:n], refs[n], refs[n + 1]
        sends, lands = _gather_copies(ins, kinds, s_sems, r_sems)
        for cp in sends:
            cp.wait_send()
        for cp in lands:
            cp.wait_recv()

    return pl.pallas_call(
        body,
        name=name,
        out_shape=tuple(pltpu.HBM(w.shape, w.dtype) for w in thru),
        in_specs=(*[HBM] * n, SEM, SEM, ANY),
        out_specs=tuple([HBM] * n),
        input_output_aliases={t: t for t in range(n)},
        compiler_params=pltpu.CompilerParams(has_side_effects=DATAFLOW),
    )(*thru, send_sems, recv_sems, follows)


def _allgather8(block, *, name):
    rows, cols = block.shape

    def body(in_ref, out_ref, send_sems, recv_sems, local_sem):
        x, y, c, _ = _position()
        me = 4 * x + 2 * y + c
        mine = pltpu.make_async_copy(in_ref, out_ref.at[me], local_sem)
        mine.start()
        peers = []
        for flip in range(1, N_DEV):
            fx, fy, fc = (flip >> 2) & 1, (flip >> 1) & 1, flip & 1
            peers.append(((1 - x if fx else x), (1 - y if fy else y), (1 - c if fc else c)))
        sends = [_remote(in_ref, out_ref.at[me], send_sems, recv_sems, j, peer) for j, peer in enumerate(peers)]
        for cp in sends:
            cp.start()
        for j, (px, py, pc) in enumerate(peers):
            slot = out_ref.at[4 * px + 2 * py + pc]
            _remote(slot, slot, send_sems, recv_sems, j, (px, py, pc)).wait_recv()
        for cp in sends:
            cp.wait_send()
        mine.wait()

    vmem = pl.BlockSpec(memory_space=pltpu.VMEM)
    return pl.pallas_call(
        body,
        out_shape=jax.ShapeDtypeStruct((N_DEV, rows, cols), F32),
        in_specs=[vmem],
        out_specs=vmem,
        scratch_shapes=[pltpu.SemaphoreType.DMA((N_DEV - 1,)), pltpu.SemaphoreType.DMA((N_DEV - 1,)),
                        pltpu.SemaphoreType.DMA],
        name=name,
    )(block)


def _sum8(stack, *, name):
    _, rows, cols = stack.shape

    def body(s_ref, o_ref):
        acc = s_ref[0]
        for i in range(1, N_DEV):
            acc = acc + s_ref[i]
        o_ref[...] = acc

    return pl.pallas_call(body, out_shape=jax.ShapeDtypeStruct((rows, cols), F32), name=name)(stack)


def _swap_halves_in(grads, kinds):
    n = len(grads)

    def body(*refs):
        ins, outs = refs[:n], refs[n:2 * n]
        send_sems, recv_sems = refs[2 * n:]
        x, y, c, _ = _position()
        sibling = (x, y, 1 - c)
        sends = [_remote(_half_of(ins[t], kinds[t], 1 - c), outs[t], send_sems, recv_sems, t, sibling) for t in range(n)]
        for cp in sends:
            cp.start()
        for t in range(n):
            _remote(_half_of(ins[t], kinds[t], c), outs[t], send_sems, recv_sems, t, sibling).wait_recv()
        for cp in sends:
            cp.wait_send()

    def half(g, kind):
        rows, cols = g.shape
        return (rows // 2, cols) if kind == "col" else (rows, cols // 2)

    return pl.pallas_call(
        body,
        out_shape=[jax.ShapeDtypeStruct(half(g, k), F32) for g, k in zip(grads, kinds)],
        in_specs=[ANY] * n,
        out_specs=[ANY] * n,
        scratch_shapes=[pltpu.SemaphoreType.DMA((n,)), pltpu.SemaphoreType.DMA((n,))],
        name="reduce_core_pair",
    )(*grads)


def _half_add(grad, theirs, kind, c_arr, *, name):
    rows, cols = theirs.shape
    tr = _pick(rows, (256, 128))
    nb = rows // tr
    if kind == "col":
        g_spec = pl.BlockSpec((tr, cols), lambda i, c_ref: (c_ref[0] * nb + i, 0))
    else:
        g_spec = pl.BlockSpec((tr, cols), lambda i, c_ref: (i, c_ref[0]))
    t_spec = pl.BlockSpec((tr, cols), lambda i, c_ref: (i, 0))

    def body(c_ref, g_ref, t_ref, o32_ref, o16_ref):
        tot = g_ref[...] + t_ref[...]
        o32_ref[...] = tot
        o16_ref[...] = tot.astype(BF16)

    return pl.pallas_call(
        body,
        out_shape=[jax.ShapeDtypeStruct((rows, cols), F32), jax.ShapeDtypeStruct((rows, cols), BF16)],
        grid_spec=pltpu.PrefetchScalarGridSpec(num_scalar_prefetch=1, grid=(nb,), in_specs=[g_spec, t_spec],
                                               out_specs=[t_spec, t_spec]),
        compiler_params=_params("parallel"),
        name=name,
    )(c_arr, grad, theirs)


def _exchange_chips(parts, kinds):
    n = len(parts)

    def body(*refs):
        ins, outs = refs[:n], refs[n:2 * n]
        send_sems, recv_sems = refs[2 * n:]
        x, y, c, chips = _position()
        sends = []
        for t in range(n):
            for k, (cx, cy) in enumerate(chips):
                src = _shard_of(ins[t], kinds[t], 2 * cx + cy, N_CHIPS)
                sends.append(_remote(src, outs[t].at[k], send_sems, recv_sems, 3 * t + k, (cx, cy, c)))
        for cp in sends:
            cp.start()
        for t in range(n):
            for k, (cx, cy) in enumerate(chips):
                src = _shard_of(ins[t], kinds[t], 2 * cx + cy, N_CHIPS)
                _remote(src, outs[t].at[k], send_sems, recv_sems, 3 * t + k, (cx, cy, c)).wait_recv()
        for cp in sends:
            cp.wait_send()

    def piece(p, kind):
        rows, cols = p.shape
        return (3, rows, cols // N_CHIPS) if kind == "col" else (3, rows // N_CHIPS, cols)

    return pl.pallas_call(
        body,
        out_shape=[jax.ShapeDtypeStruct(piece(p, k), BF16) for p, k in zip(parts, kinds)],
        in_specs=[ANY] * n,
        out_specs=[ANY] * n,
        scratch_shapes=[pltpu.SemaphoreType.DMA((3 * n,)), pltpu.SemaphoreType.DMA((3 * n,))],
        name="reduce_chips",
    )(*parts)


def _shard_sum(part32, recv, kind, sc_arr, *, name):
    _, rows, cols = recv.shape
    tr = _pick(rows, (256, 128))
    nb = rows // tr
    if kind == "col":
        whole = (2 * rows, cols)
        p_spec = pl.BlockSpec((tr, cols), lambda i, sc: (i, sc[0]))
        o_spec = pl.BlockSpec((tr, cols), lambda i, sc: (sc[1] * nb + i, 0))
    else:
        whole = (rows, 2 * cols)
        p_spec = pl.BlockSpec((tr, cols), lambda i, sc: (sc[0] * nb + i, 0))
        o_spec = pl.BlockSpec((tr, cols), lambda i, sc: (i, sc[1]))
    r_spec = pl.BlockSpec((3, tr, cols), lambda i, sc: (0, i, 0))

    def body(sc_ref, p_ref, r_ref, o_ref):
        acc = p_ref[...]
        for k in range(3):
            acc = acc + r_ref[k].astype(F32)
        o_ref[...] = acc

    return pl.pallas_call(
        body,
        out_shape=jax.ShapeDtypeStruct(whole, F32),
        grid_spec=pltpu.PrefetchScalarGridSpec(num_scalar_prefetch=1, grid=(nb,), in_specs=[p_spec, r_spec],
                                               out_specs=o_spec),
        compiler_params=_params("parallel"),
        name=name,
    )(sc_arr, part32, recv)


def _swap_halves_out(shards, kinds):
    n = len(shards)

    def body(*refs):
        ins, outs = refs[:n], refs[n:2 * n]
        send_sems, recv_sems = refs[2 * n:]
        x, y, c, _ = _position()
        sibling = (x, y, 1 - c)
        sends = [_remote(_half_of(ins[t], kinds[t], c), _half_of(outs[t], kinds[t], c), send_sems, recv_sems, t, sibling)
                 for t in range(n)]
        for cp in sends:
            cp.start()
        for t in range(n):
            theirs = _half_of(outs[t], kinds[t], 1 - c)
            _remote(theirs, theirs, send_sems, recv_sems, t, sibling).wait_recv()
        for cp in sends:
            cp.wait_send()

    return pl.pallas_call(
        body,
        out_shape=[jax.ShapeDtypeStruct(s.shape, F32) for s in shards],
        in_specs=[ANY] * n,
        out_specs=[ANY] * n,
        input_output_aliases={t: t for t in range(n)},
        scratch_shapes=[pltpu.SemaphoreType.DMA((n,)), pltpu.SemaphoreType.DMA((n,))],
        name="gather_core_pair",
    )(*shards)


def _adamw(w, g, m, v, *, name):
    rows, cols = w.shape
    tr = _pick(rows, (256, 128)) if rows * cols > 64 * 1024 else rows

    def body(w_ref, g_ref, m_ref, v_ref, d_ref, nm_ref, nv_ref):
        gv = g_ref[...]
        nm = ADAM_B1 * m_ref[...] + (1.0 - ADAM_B1) * gv
        nv = ADAM_B2 * v_ref[...] + (1.0 - ADAM_B2) * jnp.square(gv)
        m_hat = nm / (1.0 - ADAM_B1 ** ADAM_STEP)
        v_hat = nv / (1.0 - ADAM_B2 ** ADAM_STEP)
        d_ref[...] = -ADAM_LR * (m_hat / (jnp.sqrt(v_hat) + ADAM_EPS) + ADAM_WD * w_ref[...])
        nm_ref[...] = nm
        nv_ref[...] = nv

    spec = pl.BlockSpec((tr, cols), lambda i: (i, 0))
    return pl.pallas_call(
        body,
        out_shape=[jax.ShapeDtypeStruct((rows, cols), F32)] * 3,
        grid=(rows // tr,),
        in_specs=[spec] * 4,
        out_specs=[spec] * 3,
        compiler_params=_params("parallel"),
        name=name,
    )(w, g, m, v)


BIG = ("even_w_in", "odd_w_qkv", "mlp_w1_0", "mlp_w1_1", "even_w_out", "odd_w_o", "mlp_w2_0", "mlp_w2_1")
BIG_KIND = ("col", "col", "col", "col", "row", "row", "row", "row")
SUBLANES = 8


def _pack_rows(parts, width):
    padded, offsets, r0 = [], [], 0
    for t in parts:
        rows = -(-t.shape[0] // SUBLANES) * SUBLANES
        padded.append(jnp.pad(t, ((0, rows - t.shape[0]), (0, width - t.shape[1]))))
        offsets.append(r0)
        r0 += rows
    return jnp.concatenate(padded, axis=0), offsets


def kernel(x, meta, norm_mix_g, norm_mlp_g, even_w_in, even_ret_gn_g, even_conv_w, even_conv_b, even_conv_ln_g, even_conv_ln_b, even_w_out, odd_w_qkv, odd_q_norm_g, odd_k_norm_g, odd_w_o, mlp_w1, mlp_w2, loss_target, m_meta, m_norm_mix_g, m_norm_mlp_g, m_even_w_in, m_even_ret_gn_g, m_even_conv_w, m_even_conv_b, m_even_conv_ln_g, m_even_conv_ln_b, m_even_w_out, m_odd_w_qkv, m_odd_q_norm_g, m_odd_k_norm_g, m_odd_w_o, m_mlp_w1, m_mlp_w2, v_meta, v_norm_mix_g, v_norm_mlp_g, v_even_w_in, v_even_ret_gn_g, v_even_conv_w, v_even_conv_b, v_even_conv_ln_g, v_even_conv_ln_b, v_even_w_out, v_odd_w_qkv, v_odd_q_norm_g, v_odd_k_norm_g, v_odd_w_o, v_mlp_w1, v_mlp_w2):
    d = D_MODEL
    xi, yi, ci = lax.axis_index("x"), lax.axis_index("y"), lax.axis_index("c")
    chip = 2 * xi + yi
    c_arr = jnp.reshape(ci, (1,)).astype(jnp.int32)
    s_arr = jnp.reshape(chip, (1,)).astype(jnp.int32)

    def split_big(w_in, w_qkv, w1, w_out, w_o, w2):
        return dict(zip(BIG, (w_in[0], w_qkv[0], w1[0], w1[1], w_out[0], w_o[0], w2[0], w2[1])))

    w_big = split_big(even_w_in, odd_w_qkv, mlp_w1, even_w_out, odd_w_o, mlp_w2)
    m_big = split_big(m_even_w_in, m_odd_w_qkv, m_mlp_w1, m_even_w_out, m_odd_w_o, m_mlp_w2)
    v_big = split_big(v_even_w_in, v_odd_w_qkv, v_mlp_w1, v_even_w_out, v_odd_w_o, v_mlp_w2)

    placed = {n: _cast_into_whole(w_big[n], k, s_arr, name="cast_" + n) for n, k in zip(BIG, BIG_KIND)}
    kind_of = dict(zip(BIG, BIG_KIND))
    (w_in_full,) = _allgather_weights([placed["even_w_in"]], [kind_of["even_w_in"]])
    groups = dict(l0=("even_w_out", "mlp_w1_0", "mlp_w2_0"), qkv=("odd_w_qkv",), l1=("odd_w_o", "mlp_w1_1", "mlp_w2_1"))
    in_flight, follows = {}, w_in_full
    for group, names in groups.items():
        kinds = [kind_of[n] for n in names]
        send_sems, recv_sems, thru, follows = _gather_start([placed[n] for n in names], kinds, follows,
                                                            name="gather_" + group + "_start")
        in_flight[group] = (send_sems, recv_sems, thru, kinds)
    started = follows[0:1, 0:1]

    def later(group, after):
        send_sems, recv_sems, thru, kinds = in_flight[group]
        return _gather_wait(send_sems, recv_sems, thru, kinds, after, name="gather_" + group + "_wait")

    packed, (r_meta, r_conv, r_gn) = _pack_rows([meta, even_conv_w[0], even_ret_gn_g[0]], d // N_CHIPS)
    gathered = _allgather8(packed, name="allgather_small_params")[0::2]
    across = lambda r0, rows, width: jnp.concatenate([gathered[s, r0:r0 + rows, 0:width] for s in range(N_CHIPS)], axis=1)
    meta_full = across(r_meta, N_META, d // N_CHIPS) + started
    conv_w_full = across(r_conv, CONV_WIDTH, d // N_CHIPS)
    gn_full = across(r_gn, RET_HEADS, RET_V_DIM // N_CHIPS)

    sq, g = _local_step(
        x[0], loss_target[0], meta_full, norm_mix_g, norm_mlp_g, w_in_full, gn_full, conv_w_full,
        even_conv_b[0], even_conv_ln_g[0], even_conv_ln_b[0], odd_q_norm_g[0], odd_k_norm_g[0], later)
    loss = lax.psum(0.5 * sq / d, ("x", "y", "c"))

    small_names = ("norm_mix_g", "norm_mlp_g", "even_conv_b", "even_conv_ln_g", "even_conv_ln_b", "odd_q_norm_g",
                   "odd_k_norm_g", "meta", "even_conv_w", "even_ret_gn_g")
    pack, offsets = _pack_rows([g[n] for n in small_names], d)
    summed = _sum8(_allgather8(pack, name="allgather_small_grads"), name="sum_small_grads")
    small = {n: summed[r0:r0 + g[n].shape[0], 0:g[n].shape[1]] for n, r0 in zip(small_names, offsets)}
    for n in ("meta", "even_conv_w", "even_ret_gn_g"):
        width = small[n].shape[1] // N_CHIPS
        small[n] = lax.dynamic_slice_in_dim(small[n], chip * width, width, axis=1)

    g_big = [g["even_w_in"], g["odd_w_qkv"], g["mlp_w1"][0], g["mlp_w1"][1], g["even_w_out"], g["odd_w_o"],
             g["mlp_w2"][0], g["mlp_w2"][1]]
    theirs = _swap_halves_in(g_big, BIG_KIND)
    sums = [_half_add(gb, th, k, c_arr, name="pair_sum_" + n) for gb, th, k, n in zip(g_big, theirs, BIG_KIND, BIG)]
    recv = _exchange_chips([s16 for _, s16 in sums], BIG_KIND)
    sc_arr = jnp.concatenate([s_arr, c_arr])
    halves = [_shard_sum(s32, r, k, sc_arr, name="chip_sum_" + n) for (s32, _), r, k, n in zip(sums, recv, BIG_KIND, BIG)]
    grad_big = dict(zip(BIG, _swap_halves_out(halves, BIG_KIND)))

    upd = {n: _adamw(w_big[n], grad_big[n], m_big[n], v_big[n], name="adamw_" + n) for n in BIG}

    def join(name, idx, lead):
        if name in ("mlp_w1", "mlp_w2"):
            return jnp.stack([upd[name + "_0"][idx], upd[name + "_1"][idx]]) if idx >= 0 else jnp.stack(
                [grad_big[name + "_0"], grad_big[name + "_1"]])
        t = upd[name][idx] if idx >= 0 else grad_big[name]
        return t[None] if lead else t

    small_w = dict(meta=meta, norm_mix_g=norm_mix_g, norm_mlp_g=norm_mlp_g, even_ret_gn_g=even_ret_gn_g[0],
                   even_conv_w=even_conv_w[0], even_conv_b=even_conv_b, even_conv_ln_g=even_conv_ln_g,
                   even_conv_ln_b=even_conv_ln_b, odd_q_norm_g=odd_q_norm_g, odd_k_norm_g=odd_k_norm_g)
    small_m = dict(meta=m_meta, norm_mix_g=m_norm_mix_g, norm_mlp_g=m_norm_mlp_g, even_ret_gn_g=m_even_ret_gn_g[0],
                   even_conv_w=m_even_conv_w[0], even_conv_b=m_even_conv_b, even_conv_ln_g=m_even_conv_ln_g,
                   even_conv_ln_b=m_even_conv_ln_b, odd_q_norm_g=m_odd_q_norm_g, odd_k_norm_g=m_odd_k_norm_g)
    small_v = dict(meta=v_meta, norm_mix_g=v_norm_mix_g, norm_mlp_g=v_norm_mlp_g, even_ret_gn_g=v_even_ret_gn_g[0],
                   even_conv_w=v_even_conv_w[0], even_conv_b=v_even_conv_b, even_conv_ln_g=v_even_conv_ln_g,
                   even_conv_ln_b=v_even_conv_ln_b, odd_q_norm_g=v_odd_q_norm_g, odd_k_norm_g=v_odd_k_norm_g)
    small_upd = {n: _adamw(small_w[n], small[n], small_m[n], small_v[n], name="adamw_" + n) for n in small_w}
    leading = ("even_ret_gn_g", "even_conv_w")

    order = ("meta", "norm_mix_g", "norm_mlp_g", "even_w_in", "even_ret_gn_g", "even_conv_w", "even_conv_b",
             "even_conv_ln_g", "even_conv_ln_b", "even_w_out", "odd_w_qkv", "odd_q_norm_g", "odd_k_norm_g", "odd_w_o",
             "mlp_w1", "mlp_w2")
    big_lead = ("even_w_in", "even_w_out", "odd_w_qkv", "odd_w_o")

    def leaf(name, idx):
        if name in small_w:
            t = small_upd[name][idx] if idx >= 0 else small[name]
            return t[None] if name in leading else t
        return join(name, idx, name in big_lead)

    outs = [loss, g["x"][None]]
    for idx in (-1, 0, 1, 2):
        outs += [leaf(n, idx) for n in order]
    return tuple(outs)
```

```python
import functools

import jax
import jax.numpy as jnp
from jax import lax
from jax.experimental import pallas as pl
from jax.experimental.pallas import tpu as pltpu

F32 = jnp.float32
BF16 = jnp.bfloat16

D_MODEL = 1024
N_META = 16
CHUNK = 128
PAD_FRONT = (-N_META) % CHUNK
RET_HEADS = 4
RET_QK_DIM = 128
RET_V_DIM = 256
RET_QK_W = RET_HEADS * RET_QK_DIM
RET_V_W = RET_HEADS * RET_V_DIM
CONV_WIDTH = 31
CONV_HALO = 32
RET_DECAY_OFFSET = 5.0
ROPE_BASE = 10000.0
SB_HEADS = 16
SB_HEAD_DIM = 64
D_FF = 4 * D_MODEL
EPS = 1e-6
ADAM_LR = 0.001
ADAM_B1 = 0.9
ADAM_B2 = 0.999
ADAM_EPS = 1e-08
ADAM_WD = 0.01
ADAM_STEP = 10

N_CHIPS = 4
N_DEV = 8
VMEM_LIMIT = 56 * 1024 * 1024
MESH = pl.DeviceIdType.MESH
ANY = pl.BlockSpec(memory_space=pl.ANY)


def _params(*sem):
    return pltpu.CompilerParams(dimension_semantics=sem, vmem_limit_bytes=VMEM_LIMIT)


def _pick(n, cands):
    for c in cands:
        if n % c == 0:
            return c
    return n


def _sigmoid(x):
    return 1.0 / (1.0 + jnp.exp(-x))


def _dot(a, b):
    return lax.dot_general(a, b, (((1,), (0,)), ((), ())), preferred_element_type=F32)


def _dot_nt(a, b):
    return lax.dot_general(a, b, (((1,), (1,)), ((), ())), preferred_element_type=F32)


def _dot_tn(a, b):
    return lax.dot_general(a, b, (((0,), (0,)), ((), ())), preferred_element_type=F32)


def _split_dot(x, m):
    hi = x.astype(BF16)
    lo = (x - hi.astype(F32)).astype(BF16)
    return _dot(hi, m) + _dot(lo, m)


def _matmul(a, b, *, mode, out_dtypes, epilogue=None, extras=(), name):
    if mode == "nn":
        (m, k), (k2, n) = a.shape, b.shape
    elif mode == "nt":
        (m, k), (n, k2) = a.shape, b.shape
    else:
        (k, m), (k2, n) = a.shape, b.shape
    assert k == k2, (a.shape, b.shape, mode)
    tm = _pick(m, (1056, 1024, 768, 512, 384, 256, 128, 96))
    tn = _pick(n, (1024, 768, 512, 256, 128))
    tk = _pick(k, (1056, 1024, 768, 512, 384, 256, 128, 96))
    nk = k // tk
    dot = {"nn": _dot, "nt": _dot_nt, "tn": _dot_tn}[mode]
    n_extra, n_out = len(extras), len(out_dtypes)
    if epilogue is None:
        epilogue = lambda acc: (acc,)

    def body(a_ref, b_ref, *rest):
        extra_refs = rest[:n_extra]
        out_refs = rest[n_extra:n_extra + n_out]
        part = dot(a_ref[...].astype(BF16), b_ref[...].astype(BF16))

        def finish(acc):
            res = epilogue(acc, *[r[...] for r in extra_refs])
            for o_ref, r in zip(out_refs, res):
                o_ref[...] = r.astype(o_ref.dtype)

        if nk == 1:
            finish(part)
        else:
            acc_ref = rest[-1]
            kk = pl.program_id(2)

            @pl.when(kk == 0)
            def _():
                acc_ref[...] = part

            @pl.when(kk > 0)
            def _():
                acc_ref[...] += part

            @pl.when(kk == nk - 1)
            def _():
                finish(acc_ref[...])

    if mode == "nn":
        a_spec = pl.BlockSpec((tm, tk), lambda i, j, kk: (i, kk))
        b_spec = pl.BlockSpec((tk, tn), lambda i, j, kk: (kk, j))
    elif mode == "nt":
        a_spec = pl.BlockSpec((tm, tk), lambda i, j, kk: (i, kk))
        b_spec = pl.BlockSpec((tn, tk), lambda i, j, kk: (j, kk))
    else:
        a_spec = pl.BlockSpec((tk, tm), lambda i, j, kk: (kk, i))
        b_spec = pl.BlockSpec((tk, tn), lambda i, j, kk: (kk, j))
    o_spec = pl.BlockSpec((tm, tn), lambda i, j, kk: (i, j))
    outs = pl.pallas_call(
        body,
        out_shape=[jax.ShapeDtypeStruct((m, n), dt) for dt in out_dtypes],
        grid=(m // tm, n // tn, nk),
        in_specs=[a_spec, b_spec] + [o_spec] * n_extra,
        out_specs=[o_spec] * n_out,
        scratch_shapes=[pltpu.VMEM((tm, tn), F32)] if nk > 1 else [],
        compiler_params=_params("parallel", "parallel", "arbitrary"),
        name=name,
    )(a, b, *extras)
    return outs[0] if n_out == 1 else outs


def _add_epilogue(acc, res):
    return (res + acc,)


def _rmsnorm_fwd(x, g, *, name):
    p, d = x.shape
    rows = _pick(p, (384, 128, 96))

    def body(x_ref, g_ref, o_ref):
        xv = x_ref[...]
        r = lax.rsqrt(jnp.mean(xv * xv, axis=-1, keepdims=True) + EPS)
        o_ref[...] = (xv * r * g_ref[...]).astype(o_ref.dtype)

    return pl.pallas_call(
        body,
        out_shape=jax.ShapeDtypeStruct((p, d), BF16),
        grid=(p // rows,),
        in_specs=[pl.BlockSpec((rows, d), lambda i: (i, 0)), pl.BlockSpec((1, d), lambda i: (0, 0))],
        out_specs=pl.BlockSpec((rows, d), lambda i: (i, 0)),
        compiler_params=_params("parallel"),
        name=name,
    )(x, g)


def _rmsnorm_bwd(x, g, dy, dres, *, name):
    p, d = x.shape
    rows = _pick(p, (384, 128, 96))

    def body(x_ref, g_ref, dy_ref, dres_ref, dx_ref, dg_ref):
        xv = x_ref[...]
        r = lax.rsqrt(jnp.mean(xv * xv, axis=-1, keepdims=True) + EPS)
        dyv = dy_ref[...]
        gdy = dyv * g_ref[...]
        proj = jnp.mean(xv * gdy, axis=-1, keepdims=True)
        dx_ref[...] = dres_ref[...] + r * gdy - xv * (r * r * r) * proj
        part = jnp.sum(dyv * xv * r, axis=0, keepdims=True)

        @pl.when(pl.program_id(0) == 0)
        def _():
            dg_ref[...] = part

        @pl.when(pl.program_id(0) > 0)
        def _():
            dg_ref[...] += part

    row_spec = pl.BlockSpec((rows, d), lambda i: (i, 0))
    vec_spec = pl.BlockSpec((1, d), lambda i: (0, 0))
    return pl.pallas_call(
        body,
        out_shape=[jax.ShapeDtypeStruct((p, d), F32), jax.ShapeDtypeStruct((1, d), F32)],
        grid=(p // rows,),
        in_specs=[row_spec, vec_spec, row_spec, row_spec],
        out_specs=[row_spec, vec_spec],
        compiler_params=_params("arbitrary"),
        name=name,
    )(x, g, dy, dres)


def _mlp_fwd(h, g, w1, w2, *, name):
    hn = _rmsnorm_fwd(h, g, name=name + "_norm")

    def act(acc):
        r = jnp.maximum(acc, 0.0)
        return acc, r * r

    z, a2 = _matmul(hn, w1, mode="nn", out_dtypes=(F32, BF16), epilogue=act, name=name + "_up")
    out = _matmul(a2, w2, mode="nn", out_dtypes=(F32,), epilogue=_add_epilogue, extras=(h,), name=name + "_down")
    return out, (hn, z, a2)


def _mlp_bwd(h, g, w1, w2, saved, dout, *, name):
    hn, z, a2 = saved

    def dact(acc, zt):
        return (acc * (2.0 * jnp.maximum(zt, 0.0)),)

    dz = _matmul(dout, w2, mode="nt", out_dtypes=(BF16,), epilogue=dact, extras=(z,), name=name + "_dz")
    dw2 = _matmul(a2, dout, mode="tn", out_dtypes=(F32,), name=name + "_dw2")
    dw1 = _matmul(hn, dz, mode="tn", out_dtypes=(F32,), name=name + "_dw1")
    dhn = _matmul(dz, w1, mode="nt", out_dtypes=(F32,), name=name + "_dhn")
    dh, dg = _rmsnorm_bwd(h, g, dhn, dout, name=name + "_dnorm")
    return dh, dg, dw1, dw2


def _retention_tables(p):
    half = RET_QK_DIM // 2
    inv_freq = ROPE_BASE ** (-jnp.arange(half, dtype=F32) / half)
    ang = jnp.arange(p, dtype=F32)[:, None] * inv_freq[None, :]
    cos, sin = jnp.cos(ang), jnp.sin(ang)
    cosf = jnp.concatenate([cos, cos], axis=1)
    sins = jnp.concatenate([-sin, sin], axis=1)
    log_g = jnp.log1p(-jnp.exp2(-RET_DECAY_OFFSET - jnp.arange(RET_HEADS, dtype=F32)))
    idx = jnp.arange(CHUNK, dtype=F32)
    diff = idx[:, None] - idx[None, :]
    inner = jnp.where(diff[None] >= 0, jnp.exp(jnp.maximum(diff, 0.0)[None] * log_g[:, None, None]), 0.0)
    kdec = jnp.exp((CHUNK - 1 - idx)[None, :] * log_g[:, None])
    qdec = jnp.exp((idx + 1.0)[None, :] * log_g[:, None])
    cdec = jnp.exp(CHUNK * log_g)
    kdec = jnp.broadcast_to(kdec[:, :, None], (RET_HEADS, CHUNK, RET_QK_DIM))
    qdec = jnp.broadcast_to(qdec[:, :, None], (RET_HEADS, CHUNK, RET_QK_DIM))
    cdec = jnp.broadcast_to(cdec[:, None, None], (RET_HEADS, RET_QK_DIM, RET_V_DIM))
    return cosf, sins, inner, kdec, qdec, cdec


def _rot(x, cosf, sins):
    return x * cosf + pltpu.roll(x, RET_QK_DIM // 2, 1) * sins


def _rot_bwd(dy, cosf, sins):
    return dy * cosf + pltpu.roll(dy * sins, RET_QK_DIM // 2, 1)


def _ret_in_specs(chunk_of):
    nh = RET_HEADS
    q_spec = pl.BlockSpec((CHUNK, RET_QK_DIM), lambda h, s: (chunk_of(s), h))
    k_spec = pl.BlockSpec((CHUNK, RET_QK_DIM), lambda h, s: (chunk_of(s), nh + h))
    v_spec = pl.BlockSpec((CHUNK, RET_V_DIM), lambda h, s: (chunk_of(s), nh + h))
    g_spec = pl.BlockSpec((CHUNK, RET_V_DIM), lambda h, s: (chunk_of(s), 2 * nh + h))
    rope_spec = pl.BlockSpec((CHUNK, RET_QK_DIM), lambda h, s: (chunk_of(s), 0))
    head_sq = pl.BlockSpec((None, CHUNK, CHUNK), lambda h, s: (h, 0, 0))
    head_qk = pl.BlockSpec((None, CHUNK, RET_QK_DIM), lambda h, s: (h, 0, 0))
    head_st = pl.BlockSpec((None, RET_QK_DIM, RET_V_DIM), lambda h, s: (h, 0, 0))
    gam_spec = pl.BlockSpec((None, 1, RET_V_DIM), lambda h, s: (h, 0, 0))
    return [q_spec, k_spec, v_spec, g_spec, rope_spec, rope_spec, head_sq, head_qk, head_qk, head_st, gam_spec]


def _retention_fwd(proj, gn_g, tables, *, name):
    p = proj.shape[0]
    n_chunks = p // CHUNK
    scale = RET_QK_DIM ** -0.5

    def body(q_ref, k_ref, v_ref, g_ref, cos_ref, sin_ref, inner_ref, kdec_ref, qdec_ref, cdec_ref, gam_ref,
             og_ref, opre_ref, sprev_ref, s_scr):
        @pl.when(pl.program_id(1) == 0)
        def _():
            s_scr[...] = jnp.zeros_like(s_scr)

        cosf, sins = cos_ref[...], sin_ref[...]
        qr = _rot(q_ref[...], cosf, sins)
        kr = _rot(k_ref[...], cosf, sins) * scale
        vb = v_ref[...].astype(BF16)
        scores = _dot_nt(qr.astype(BF16), kr.astype(BF16)) * inner_ref[...]
        state = s_scr[...]
        sprev_ref[...] = state
        o = _dot(scores.astype(BF16), vb) + _dot((qr * qdec_ref[...]).astype(BF16), state.astype(BF16))
        kd = kr * kdec_ref[...]
        s_scr[...] = cdec_ref[...] * state + _dot(kd.T.astype(BF16), vb)
        opre_ref[...] = o
        mu = jnp.mean(o, axis=-1, keepdims=True)
        oc = o - mu
        var = jnp.mean(oc * oc, axis=-1, keepdims=True)
        on = oc * lax.rsqrt(var + EPS) * gam_ref[...]
        gv = g_ref[...]
        og_ref[...] = (gv * _sigmoid(gv) * on).astype(og_ref.dtype)

    chunk_of = lambda s: s
    out_v = pl.BlockSpec((CHUNK, RET_V_DIM), lambda h, s: (s, h))
    return pl.pallas_call(
        body,
        out_shape=[
            jax.ShapeDtypeStruct((p, RET_V_W), BF16),
            jax.ShapeDtypeStruct((p, RET_V_W), F32),
            jax.ShapeDtypeStruct((RET_HEADS, n_chunks, RET_QK_DIM, RET_V_DIM), F32),
        ],
        grid=(RET_HEADS, n_chunks),
        in_specs=_ret_in_specs(chunk_of),
        out_specs=[out_v, out_v, pl.BlockSpec((None, None, RET_QK_DIM, RET_V_DIM), lambda h, s: (h, s, 0, 0))],
        scratch_shapes=[pltpu.VMEM((RET_QK_DIM, RET_V_DIM), F32)],
        compiler_params=_params("parallel", "arbitrary"),
        name=name,
    )(proj, proj, proj, proj, *tables, gn_g.reshape(RET_HEADS, 1, RET_V_DIM))


def _retention_bwd(proj, gn_g, tables, opre, sprev, dog, *, name):
    p = proj.shape[0]
    n_chunks = p // CHUNK
    scale = RET_QK_DIM ** -0.5

    def body(q_ref, k_ref, v_ref, g_ref, cos_ref, sin_ref, inner_ref, kdec_ref, qdec_ref, cdec_ref, gam_ref,
             opre_ref, sprev_ref, dog_ref, dq_ref, dk_ref, dv_ref, dg_ref, dgam_ref, ds_scr):
        first = pl.program_id(1) == 0

        @pl.when(first)
        def _():
            ds_scr[...] = jnp.zeros_like(ds_scr)

        cosf, sins = cos_ref[...], sin_ref[...]
        qr = _rot(q_ref[...], cosf, sins)
        kr = _rot(k_ref[...], cosf, sins) * scale
        qb, kb = qr.astype(BF16), kr.astype(BF16)
        vb = v_ref[...].astype(BF16)
        inner = inner_ref[...]
        qdec, kdec = qdec_ref[...], kdec_ref[...]
        state_b = sprev_ref[...].astype(BF16)
        o = opre_ref[...]
        mu = jnp.mean(o, axis=-1, keepdims=True)
        oc = o - mu
        rstd = lax.rsqrt(jnp.mean(oc * oc, axis=-1, keepdims=True) + EPS)
        xhat = oc * rstd
        gam = gam_ref[...]
        on = xhat * gam
        gv = g_ref[...]
        sig = _sigmoid(gv)
        dogv = dog_ref[...]
        dg_ref[...] = (dogv * on * sig * (1.0 + gv * (1.0 - sig))).astype(dg_ref.dtype)
        don = dogv * gv * sig
        dgam_part = jnp.sum(don * xhat, axis=0, keepdims=True)

        @pl.when(first)
        def _():
            dgam_ref[...] = dgam_part

        @pl.when(jnp.logical_not(first))
        def _():
            dgam_ref[...] += dgam_part

        dxhat = don * gam
        do = rstd * (dxhat - jnp.mean(dxhat, axis=-1, keepdims=True)
                     - xhat * jnp.mean(dxhat * xhat, axis=-1, keepdims=True))
        dob = do.astype(BF16)
        scores_b = (_dot_nt(qb, kb) * inner).astype(BF16)
        da = (_dot_nt(dob, vb) * inner).astype(BF16)
        dv = _dot(scores_b.astype(F32).T.astype(BF16), dob)
        dqr = _dot(da, kb)
        dkr = _dot(da.astype(F32).T.astype(BF16), qb)
        dqr += _dot_nt(dob, state_b) * qdec
        ds_local = _dot((qr * qdec).T.astype(BF16), dob)
        gstate = ds_scr[...]
        gb = gstate.astype(BF16)
        kd_b = (kr * kdec).astype(BF16)
        dkr += _dot_nt(vb, gb) * kdec
        dv += _dot(kd_b, gb)
        ds_scr[...] = cdec_ref[...] * gstate + ds_local
        dq_ref[...] = _rot_bwd(dqr, cosf, sins).astype(dq_ref.dtype)
        dk_ref[...] = _rot_bwd(dkr * scale, cosf, sins).astype(dk_ref.dtype)
        dv_ref[...] = dv.astype(dv_ref.dtype)

    chunk_of = lambda s: n_chunks - 1 - s
    blk_v = pl.BlockSpec((CHUNK, RET_V_DIM), lambda h, s: (chunk_of(s), h))
    blk_qk = pl.BlockSpec((CHUNK, RET_QK_DIM), lambda h, s: (chunk_of(s), h))
    st_spec = pl.BlockSpec((None, None, RET_QK_DIM, RET_V_DIM), lambda h, s: (h, chunk_of(s), 0, 0))
    return pl.pallas_call(
        body,
        out_shape=[
            jax.ShapeDtypeStruct((p, RET_QK_W), BF16),
            jax.ShapeDtypeStruct((p, RET_QK_W), BF16),
            jax.ShapeDtypeStruct((p, RET_V_W), BF16),
            jax.ShapeDtypeStruct((p, RET_V_W), BF16),
            jax.ShapeDtypeStruct((RET_HEADS, 1, RET_V_DIM), F32),
        ],
        grid=(RET_HEADS, n_chunks),
        in_specs=_ret_in_specs(chunk_of) + [blk_v, st_spec, blk_v],
        out_specs=[blk_qk, blk_qk, blk_v, blk_v, pl.BlockSpec((None, 1, RET_V_DIM), lambda h, s: (h, 0, 0))],
        scratch_shapes=[pltpu.VMEM((RET_QK_DIM, RET_V_DIM), F32)],
        compiler_params=_params("parallel", "arbitrary"),
        name=name,
    )(proj, proj, proj, proj, *tables, gn_g.reshape(RET_HEADS, 1, RET_V_DIM), opre, sprev, dog)


def _conv_rows(p):
    return _pick(p, (384, 128))


def _ln_stats(y):
    mu = jnp.mean(y, axis=-1, keepdims=True)
    yc = y - mu
    rstd = lax.rsqrt(jnp.mean(yc * yc, axis=-1, keepdims=True) + EPS)
    return yc * rstd, rstd


def _conv_fwd(proj, conv_w, conv_b, ln_g, ln_b, *, name):
    p = proj.shape[0]
    c = D_MODEL
    rows = _conv_rows(p)
    hpb = rows // CONV_HALO
    a_col, gate_col = (2 * RET_QK_W + 2 * RET_V_W) // c, (2 * RET_QK_W + 2 * RET_V_W) // c + 1

    def body(a_ref, gate_ref, ah_ref, gateh_ref, w_ref, b_ref, lg_ref, lb_ref, c_ref, y_ref, hdn_scr):
        i = pl.program_id(0)
        hdn_scr[0:CONV_HALO, :] = ah_ref[...] * _sigmoid(gateh_ref[...])
        hdn_scr[CONV_HALO:, :] = a_ref[...] * _sigmoid(gate_ref[...])
        acc = jnp.zeros((rows, c), F32)
        for w in range(CONV_WIDTH):
            off = CONV_HALO - (CONV_WIDTH - 1) + w
            acc += hdn_scr[off:off + rows, :] * w_ref[w:w + 1, :]
        y = acc + b_ref[...]
        y_ref[...] = y
        yhat, _ = _ln_stats(y)
        ln = yhat * lg_ref[...] + lb_ref[...]
        row = i * rows + lax.broadcasted_iota(jnp.int32, (rows, 1), 0)
        c_ref[...] = jnp.where(row >= PAD_FRONT, ln * _sigmoid(ln), 0.0).astype(c_ref.dtype)

    halo_idx = lambda i: jnp.maximum(i * hpb - 1, 0)
    vec = pl.BlockSpec((1, c), lambda i: (0, 0))
    return pl.pallas_call(
        body,
        out_shape=[jax.ShapeDtypeStruct((p, c), BF16), jax.ShapeDtypeStruct((p, c), F32)],
        grid=(p // rows,),
        in_specs=[
            pl.BlockSpec((rows, c), lambda i: (i, a_col)),
            pl.BlockSpec((rows, c), lambda i: (i, gate_col)),
            pl.BlockSpec((CONV_HALO, c), lambda i: (halo_idx(i), a_col)),
            pl.BlockSpec((CONV_HALO, c), lambda i: (halo_idx(i), gate_col)),
            pl.BlockSpec((CONV_WIDTH, c), lambda i: (0, 0)),
            vec, vec, vec,
        ],
        out_specs=[pl.BlockSpec((rows, c), lambda i: (i, 0)), pl.BlockSpec((rows, c), lambda i: (i, 0))],
        scratch_shapes=[pltpu.VMEM((CONV_HALO + rows, c), F32)],
        compiler_params=_params("parallel"),
        name=name,
    )(proj, proj, proj, proj, conv_w, conv_b, ln_g, ln_b)


def _conv_bwd(proj, conv_w, ln_g, ln_b, y, dcat, *, name):
    p = proj.shape[0]
    c = D_MODEL
    rows = _conv_rows(p)
    hpb = rows // CONV_HALO
    n_blocks = p // rows
    a_col, gate_col = (2 * RET_QK_W + 2 * RET_V_W) // c, (2 * RET_QK_W + 2 * RET_V_W) // c + 1

    def body(a_ref, gate_ref, ah_ref, gateh_ref, w_ref, lg_ref, lb_ref, y_ref, yh_ref, dc_ref, dch_ref,
             da_ref, dgate_ref, dw_ref, db_ref, dlg_ref, dlb_ref, hdn_scr, dy_scr):
        i = pl.program_id(0)
        lg, lb = lg_ref[...], lb_ref[...]

        def ln_bwd(yv, dcv):
            yhat, rstd = _ln_stats(yv)
            ln = yhat * lg + lb
            sig = _sigmoid(ln)
            dln = dcv * sig * (1.0 + ln * (1.0 - sig))
            dyhat = dln * lg
            dyv = rstd * (dyhat - jnp.mean(dyhat, axis=-1, keepdims=True)
                          - yhat * jnp.mean(dyhat * yhat, axis=-1, keepdims=True))
            return dyv, dln, yhat

        row = i * rows + lax.broadcasted_iota(jnp.int32, (rows, 1), 0)
        dy, dln, yhat = ln_bwd(y_ref[...], jnp.where(row >= PAD_FRONT, dc_ref[...], 0.0))
        dy_halo, _, _ = ln_bwd(yh_ref[...], dch_ref[...])
        dy_scr[0:rows, :] = dy
        dy_scr[rows:, :] = jnp.where(i == n_blocks - 1, 0.0, dy_halo)
        sig_gate = _sigmoid(gate_ref[...])
        av = a_ref[...]
        hdn_scr[0:CONV_HALO, :] = ah_ref[...] * _sigmoid(gateh_ref[...])
        hdn_scr[CONV_HALO:, :] = av * sig_gate
        @pl.when(i == 0)
        def _():
            dw_ref[...] = jnp.zeros_like(dw_ref)
            db_ref[...] = jnp.zeros_like(db_ref)
            dlg_ref[...] = jnp.zeros_like(dlg_ref)
            dlb_ref[...] = jnp.zeros_like(dlb_ref)

        dhdn = jnp.zeros((rows, c), F32)
        for w in range(CONV_WIDTH):
            back = CONV_WIDTH - 1 - w
            dhdn += dy_scr[back:back + rows, :] * w_ref[w:w + 1, :]
            off = CONV_HALO - (CONV_WIDTH - 1) + w
            dw_ref[w:w + 1, :] += jnp.sum(dy * hdn_scr[off:off + rows, :], axis=0, keepdims=True)
        da_ref[...] = (dhdn * sig_gate).astype(da_ref.dtype)
        dgate_ref[...] = (dhdn * av * sig_gate * (1.0 - sig_gate)).astype(dgate_ref.dtype)
        db_ref[...] += jnp.sum(dy, axis=0, keepdims=True)
        dlg_ref[...] += jnp.sum(dln * yhat, axis=0, keepdims=True)
        dlb_ref[...] += jnp.sum(dln, axis=0, keepdims=True)

    prev_halo = lambda i: jnp.maximum(i * hpb - 1, 0)
    next_halo = lambda i: jnp.minimum((i + 1) * hpb, p // CONV_HALO - 1)
    vec = pl.BlockSpec((1, c), lambda i: (0, 0))
    blk = lambda col: pl.BlockSpec((rows, c), lambda i: (i, col))
    outs = pl.pallas_call(
        body,
        out_shape=[
            jax.ShapeDtypeStruct((p, c), BF16),
            jax.ShapeDtypeStruct((p, c), BF16),
            jax.ShapeDtypeStruct((CONV_WIDTH + 1, c), F32),
            jax.ShapeDtypeStruct((1, c), F32),
            jax.ShapeDtypeStruct((1, c), F32),
            jax.ShapeDtypeStruct((1, c), F32),
        ],
        grid=(n_blocks,),
        in_specs=[
            blk(a_col), blk(gate_col),
            pl.BlockSpec((CONV_HALO, c), lambda i: (prev_halo(i), a_col)),
            pl.BlockSpec((CONV_HALO, c), lambda i: (prev_halo(i), gate_col)),
            pl.BlockSpec((CONV_WIDTH, c), lambda i: (0, 0)),
            vec, vec,
            blk(0),
            pl.BlockSpec((CONV_HALO, c), lambda i: (next_halo(i), 0)),
            blk(1),
            pl.BlockSpec((CONV_HALO, c), lambda i: (next_halo(i), 1)),
        ],
        out_specs=[blk(0), blk(0), pl.BlockSpec((CONV_WIDTH + 1, c), lambda i: (0, 0)), vec, vec, vec],
        scratch_shapes=[pltpu.VMEM((CONV_HALO + rows, c), F32), pltpu.VMEM((rows + CONV_HALO, c), F32)],
        compiler_params=_params("arbitrary"),
        name=name,
    )(proj, proj, proj, proj, conv_w, ln_g, ln_b, y, y, dcat, dcat)
    da, dgate, dw, db, dlg, dlb = outs
    return da, dgate, dw[:CONV_WIDTH], db, dlg, dlb


def _group_matrix():
    r = jnp.arange(D_MODEL)[:, None] // SB_HEAD_DIM
    c = jnp.arange(D_MODEL)[None, :] // SB_HEAD_DIM
    return (r == c).astype(BF16)


def _qknorm_fwd(qkv, qg, kg, *, name):
    p = qkv.shape[0]
    d = D_MODEL
    rows = _pick(p, (384, 128, 96))

    def body(q_ref, k_ref, v_ref, qg_ref, kg_ref, gm_ref, qn_ref, kn_ref, vb_ref):
        gm = gm_ref[...]

        def norm(x, g):
            ms = _split_dot(x * x, gm) * (1.0 / SB_HEAD_DIM)
            return x * lax.rsqrt(ms + EPS) * g

        qn_ref[...] = norm(q_ref[...], qg_ref[...]).astype(BF16)
        kn_ref[...] = norm(k_ref[...], kg_ref[...]).astype(BF16)
        vb_ref[...] = v_ref[...].astype(BF16)

    blk = lambda col: pl.BlockSpec((rows, d), lambda i: (i, col))
    vec = pl.BlockSpec((1, d), lambda i: (0, 0))
    return pl.pallas_call(
        body,
        out_shape=[jax.ShapeDtypeStruct((p, d), BF16)] * 3,
        grid=(p // rows,),
        in_specs=[blk(0), blk(1), blk(2), vec, vec, pl.BlockSpec((d, d), lambda i: (0, 0))],
        out_specs=[blk(0)] * 3,
        compiler_params=_params("parallel"),
        name=name,
    )(qkv, qkv, qkv, qg, kg, _group_matrix())


def _qknorm_bwd(qkv, qg, kg, dqn, dkn, dv, *, name):
    p = qkv.shape[0]
    d = D_MODEL
    rows = _pick(p, (384, 128, 96))

    def body(q_ref, k_ref, qg_ref, kg_ref, gm_ref, dqn_ref, dkn_ref, dv_ref, dqkv_ref, dqg_ref, dkg_ref):
        gm = gm_ref[...]

        def bwd(x, g, dy):
            ms = _split_dot(x * x, gm) * (1.0 / SB_HEAD_DIM)
            r = lax.rsqrt(ms + EPS)
            gdy = dy * g
            proj = _split_dot(x * gdy, gm) * (1.0 / SB_HEAD_DIM)
            return r * gdy - x * (r * r * r) * proj, jnp.sum(dy * x * r, axis=0, keepdims=True)

        dq, dqg = bwd(q_ref[...], qg_ref[...], dqn_ref[...])
        dk, dkg = bwd(k_ref[...], kg_ref[...], dkn_ref[...])
        dqkv_ref[:, 0:d] = dq.astype(BF16)
        dqkv_ref[:, d:2 * d] = dk.astype(BF16)
        dqkv_ref[:, 2 * d:3 * d] = dv_ref[...].astype(BF16)

        @pl.when(pl.program_id(0) == 0)
        def _():
            dqg_ref[...] = dqg
            dkg_ref[...] = dkg

        @pl.when(pl.program_id(0) > 0)
        def _():
            dqg_ref[...] += dqg
            dkg_ref[...] += dkg

    blk = lambda col: pl.BlockSpec((rows, d), lambda i: (i, col))
    vec = pl.BlockSpec((1, d), lambda i: (0, 0))
    return pl.pallas_call(
        body,
        out_shape=[jax.ShapeDtypeStruct((p, 3 * d), BF16), jax.ShapeDtypeStruct((1, d), F32),
                   jax.ShapeDtypeStruct((1, d), F32)],
        grid=(p // rows,),
        in_specs=[blk(0), blk(1), vec, vec, pl.BlockSpec((d, d), lambda i: (0, 0)), blk(0), blk(0), blk(0)],
        out_specs=[pl.BlockSpec((rows, 3 * d), lambda i: (i, 0)), vec, vec],
        compiler_params=_params("arbitrary"),
        name=name,
    )(qkv, qkv, qg, kg, _group_matrix(), dqn, dkn, dv)


SB_PAIR = 2 * SB_HEAD_DIM
SB_GROUP = 8
SB_MASKED = -1e30


def _sb_consts():
    lane = lax.broadcasted_iota(jnp.int32, (CHUNK, SB_PAIR), 1)
    r = lax.broadcasted_iota(jnp.int32, (CHUNK, CHUNK), 0)
    c = lax.broadcasted_iota(jnp.int32, (CHUNK, CHUNK), 1)
    lo = (lane < SB_HEAD_DIM).astype(F32).astype(BF16)
    ones = jnp.ones((CHUNK, CHUNK), BF16)
    twice = lambda m: jnp.concatenate([jnp.concatenate([m, ones], axis=1)] * 2, axis=0)
    later, earlier = twice((r > c).astype(BF16)), twice((r < c).astype(BF16))
    not_before = (c >= r).astype(F32) * SB_MASKED
    padding = (c < PAD_FRONT).astype(F32) * SB_MASKED
    return (lo, 1.0 - lo), c, later, earlier, not_before, padding


def _sb_halves(t, head_lanes):
    return t * head_lanes[0], t * head_lanes[1]


def _sb_logits(qh, kg, biases):
    z = _dot_nt(qh, kg)
    tiles = []
    for b, bias in enumerate(biases):
        zt = z[:, b * CHUNK:(b + 1) * CHUNK]
        if bias is not None:
            zt = zt + bias
        ls_pos = jnp.minimum(zt, 0.0) - jnp.log(1.0 + jnp.exp(-jnp.abs(zt)))
        tiles.append((ls_pos, ls_pos - zt))
    return tiles


def _sb_block_sums(tiles, m):
    st = jnp.concatenate(tiles, axis=0)
    hi = st.astype(BF16)
    lo = (st - hi.astype(F32)).astype(BF16)
    tot = _dot(jnp.concatenate([hi, lo], axis=1), m)
    return [(tot[i * CHUNK:(i + 1) * CHUNK, 0:CHUNK], tot[i * CHUNK:(i + 1) * CHUNK, CHUNK:2 * CHUNK])
            for i in range(len(tiles))]


def _sb_plan(qi, padding, not_before):
    top = lax.div(qi, SB_GROUP)
    size = qi - SB_GROUP * top + 1
    pad_if_first = padding * (top == 0).astype(F32)
    masks = []
    for n_b in range(1, SB_GROUP + 1):
        m = [None] * n_b
        m[n_b - 1] = not_before
        m[0] = pad_if_first if m[0] is None else m[0] + pad_if_first
        masks.append(m)
    return top, size, masks


def _once_if(cond, fn, carry):
    return lax.fori_loop(0, jnp.where(cond, 1, 0), lambda s, cr: fn(cr), carry)


def _sb_head_rows(tg, lanes, n_b):
    return jnp.concatenate([tg[b * CHUNK:(b + 1) * CHUNK] * lanes for b in range(n_b)], axis=0)


def _sb_fwd(qn, kn, vb, *, name):
    p = qn.shape[0]
    n_blocks = p // CHUNK
    n_pairs = SB_HEADS // 2
    scale = SB_HEAD_DIM ** -0.5

    def body(q_ref, k_ref, v_ref, o_ref, car_ref):
        head_lanes, c, later, _, not_before, padding = _sb_consts()

        def q_block(qi, _):
            rows = pl.ds(pl.multiple_of(qi * CHUNK, CHUNK), CHUNK)
            qh = _sb_halves(q_ref[rows, :], head_lanes)
            qs = (qh[0] * scale, qh[1] * scale)

            def blocks(kb0, biases, carry):
                n_b = len(biases)
                acc, run0, run1, sav0, sav1 = carry
                krows = pl.ds(pl.multiple_of(kb0 * CHUNK, CHUNK), n_b * CHUNK)
                kg, vg = k_ref[krows, :], v_ref[krows, :]
                tiles = [_sb_logits(qs[h], kg, biases) for h in range(2)]
                sums = [_sb_block_sums([log_keep for _, log_keep in tiles[h]], later) for h in range(2)]
                cols = [(c == kb0 + b).astype(F32) for b in range(n_b)]
                runs, savs = [run0, run1], [sav0, sav1]
                for h in range(2):
                    ws = [None] * n_b
                    for b in reversed(range(n_b)):
                        after, row_sum = sums[h][b]
                        ws[b] = jnp.exp(tiles[h][b][0] + after + runs[h]).astype(BF16)
                        savs[h] = savs[h] + cols[b] * runs[h]
                        runs[h] = runs[h] + row_sum
                    acc = acc + _dot(jnp.concatenate(ws, axis=1), _sb_head_rows(vg, head_lanes[h], n_b))
                return acc, runs[0], runs[1], savs[0], savs[1]

            zt = qh[0].astype(F32) * 0.0
            top, size, masks = _sb_plan(qi, padding, not_before)
            carry = (zt, zt, zt, zt, zt)
            for m in masks:
                carry = _once_if(size == len(m), functools.partial(blocks, SB_GROUP * top, m), carry)
            carry = lax.fori_loop(0, jnp.maximum(top - 1, 0),
                                  lambda it, cr: blocks(SB_GROUP * (top - 1 - it), [None] * SB_GROUP, cr), carry)
            carry = _once_if(top > 0, functools.partial(blocks, 0, [padding] + [None] * (SB_GROUP - 1)), carry)
            acc, _, _, sav0, sav1 = carry
            o_ref[rows, :] = acc.astype(o_ref.dtype)
            car_ref[rows, 0:CHUNK] = sav0
            car_ref[rows, CHUNK:2 * CHUNK] = sav1
            return 0

        lax.fori_loop(0, n_blocks, q_block, 0)

    col = pl.BlockSpec((p, SB_PAIR), lambda g: (0, g))
    return pl.pallas_call(
        body,
        out_shape=[jax.ShapeDtypeStruct((p, D_MODEL), BF16), jax.ShapeDtypeStruct((p, n_pairs * 2 * CHUNK), F32)],
        grid=(n_pairs,),
        in_specs=[col, col, col],
        out_specs=[col, pl.BlockSpec((p, 2 * CHUNK), lambda g: (0, g))],
        compiler_params=_params("parallel"),
        name=name,
    )(qn, kn, vb)


def _sb_bwd(qn, kn, vb, carries, do, *, name):
    p = qn.shape[0]
    n_blocks = p // CHUNK
    n_pairs = SB_HEADS // 2
    scale = SB_HEAD_DIM ** -0.5

    def body(q_ref, k_ref, v_ref, car_ref, do_ref, dq_ref, dk_ref, dv_ref):
        head_lanes, c, later, earlier, not_before, padding = _sb_consts()
        dk_ref[...] = jnp.zeros_like(dk_ref)
        dv_ref[...] = jnp.zeros_like(dv_ref)

        def q_block(qi, _):
            rows = pl.ds(pl.multiple_of(qi * CHUNK, CHUNK), CHUNK)
            qh = _sb_halves(q_ref[rows, :], head_lanes)
            qs = (qh[0] * scale, qh[1] * scale)
            doh = _sb_halves(do_ref[rows, :].astype(BF16), head_lanes)
            sav = (car_ref[rows, 0:CHUNK], car_ref[rows, CHUNK:2 * CHUNK])

            def blocks(kb0, biases, carry):
                n_b = len(biases)
                dq_acc, pre0, pre1 = carry
                krows = pl.ds(pl.multiple_of(kb0 * CHUNK, CHUNK), n_b * CHUNK)
                kg, vg = k_ref[krows, :], v_ref[krows, :]
                cols = [(c == kb0 + b).astype(F32) for b in range(n_b)]
                block = lambda t, b: t[:, b * CHUNK:(b + 1) * CHUNK]
                tiles = [_sb_logits(qs[h], kg, biases) for h in range(2)]
                afters = [_sb_block_sums([log_keep for _, log_keep in tiles[h]], later) for h in range(2)]
                dws = [_dot_nt(doh[h], vg) for h in range(2)]
                ws, es, befores = [], [], []
                for h in range(2):
                    runs = [jnp.sum(cols[b] * sav[h], axis=-1, keepdims=True) for b in range(n_b)]
                    ws.append([jnp.exp(tiles[h][b][0] + afters[h][b][0] + runs[b]) for b in range(n_b)])
                    es.append([ws[h][b] * block(dws[h], b) for b in range(n_b)])
                    befores.append(_sb_block_sums(es[h], earlier))
                pres = [pre0, pre1]
                dk_add = dv_add = None
                for h in range(2):
                    dzs = []
                    for b in range(n_b):
                        before, row_sum = befores[h][b]
                        sig = jnp.exp(tiles[h][b][0])
                        e = es[h][b]
                        dzs.append((e - (e + before + pres[h]) * sig).astype(BF16))
                        pres[h] = pres[h] + row_sum
                    dz = jnp.concatenate(dzs, axis=1)
                    w = jnp.concatenate([t.astype(BF16) for t in ws[h]], axis=1)
                    dq_acc = dq_acc + _dot(dz, _sb_head_rows(kg, head_lanes[h], n_b))
                    dv_h, dk_h = _dot_tn(w, doh[h]), _dot_tn(dz, qs[h])
                    dv_add = dv_h if dv_add is None else dv_add + dv_h
                    dk_add = dk_h if dk_add is None else dk_add + dk_h
                dv_ref[krows, :] += dv_add
                dk_ref[krows, :] += dk_add
                return dq_acc, pres[0], pres[1]

            zt = qh[0].astype(F32) * 0.0
            top, size, masks = _sb_plan(qi, padding, not_before)
            carry = _once_if(top > 0, functools.partial(blocks, 0, [padding] + [None] * (SB_GROUP - 1)), (zt, zt, zt))
            carry = lax.fori_loop(1, top, lambda g, cr: blocks(SB_GROUP * g, [None] * SB_GROUP, cr), carry)
            for m in masks:
                carry = _once_if(size == len(m), functools.partial(blocks, SB_GROUP * top, m), carry)
            dq_acc, _, _ = carry
            dq_ref[rows, :] = dq_acc * scale
            return 0

        lax.fori_loop(0, n_blocks, q_block, 0)

    col = pl.BlockSpec((p, SB_PAIR), lambda g: (0, g))
    return pl.pallas_call(
        body,
        out_shape=[jax.ShapeDtypeStruct((p, D_MODEL), F32)] * 3,
        grid=(n_pairs,),
        in_specs=[col, col, col, pl.BlockSpec((p, 2 * CHUNK), lambda g: (0, g)), col],
        out_specs=[col, col, col],
        compiler_params=_params("parallel"),
        name=name,
    )(qn, kn, vb, carries, do)


def _loss_head(h, target, *, name):
    p, d = h.shape
    n_blocks = p // CHUNK

    def body(h_ref, t_ref, sq_ref, dh_ref):
        i = pl.program_id(0)

        @pl.when(i == 0)
        def _():
            sq_ref[...] = jnp.zeros_like(sq_ref)
            dh_ref[...] = jnp.zeros_like(dh_ref)

        @pl.when(i > 0)
        def _():
            err = h_ref[...] - t_ref[...]
            sq_ref[...] += jnp.sum(err * err)
            dh_ref[...] = err * (1.0 / d)

    return pl.pallas_call(
        body,
        out_shape=[jax.ShapeDtypeStruct((8, 128), F32), jax.ShapeDtypeStruct((p, d), F32)],
        grid=(n_blocks,),
        in_specs=[pl.BlockSpec((CHUNK, d), lambda i: (i, 0)),
                  pl.BlockSpec((CHUNK, d), lambda i: (jnp.maximum(i - 1, 0), 0))],
        out_specs=[pl.BlockSpec((8, 128), lambda i: (0, 0)), pl.BlockSpec((CHUNK, d), lambda i: (i, 0))],
        compiler_params=_params("arbitrary"),
        name=name,
    )(h, target)


def _local_step(x, target, meta, norm_mix_g, norm_mlp_g, w_in, gn_g, conv_w, conv_b, ln_g, ln_b, qn_g, kn_g, later,
                reached=lambda point, after, grads=None: None):
    seq = x.shape[0]
    p = PAD_FRONT + N_META + seq
    d = D_MODEL
    tables = _retention_tables(p)
    row = lambda v: v.reshape(1, -1)
    h0 = jnp.concatenate([jnp.zeros((PAD_FRONT, d), F32), meta, x], axis=0)

    hn0 = _rmsnorm_fwd(h0, row(norm_mix_g[0]), name="l0_mix_norm")
    proj = _matmul(hn0, w_in, mode="nn", out_dtypes=(F32,), name="l0_proj")
    og, opre, sprev = _retention_fwd(proj, gn_g, tables, name="l0_retention")
    cb, y_conv = _conv_fwd(proj, conv_w, row(conv_b), row(ln_g), row(ln_b), name="l0_conv")
    cat = jnp.concatenate([og, cb], axis=1)
    w_out, w1_0, w2_0 = later("l0", cat)
    w1, w2 = [w1_0, None], [w2_0, None]
    h1 = _matmul(cat, w_out, mode="nn", out_dtypes=(F32,), epilogue=_add_epilogue, extras=(h0,), name="l0_mix_out")
    h2, mlp0 = _mlp_fwd(h1, row(norm_mlp_g[0]), w1[0], w2[0], name="l0_mlp")

    hn1 = _rmsnorm_fwd(h2, row(norm_mix_g[1]), name="l1_mix_norm")
    (w_qkv,) = later("qkv", hn1)
    qkv = _matmul(hn1, w_qkv, mode="nn", out_dtypes=(F32,), name="l1_qkv")
    qg_t, kg_t = jnp.tile(row(qn_g), (1, SB_HEADS)), jnp.tile(row(kn_g), (1, SB_HEADS))
    qn, kn, vb = _qknorm_fwd(qkv, qg_t, kg_t, name="l1_qknorm")
    o_sb, carries = _sb_fwd(qn, kn, vb, name="l1_stickbreak")
    w_o, w1[1], w2[1] = later("l1", o_sb)
    h3 = _matmul(o_sb, w_o, mode="nn", out_dtypes=(F32,), epilogue=_add_epilogue, extras=(h2,), name="l1_mix_out")
    h4, mlp1 = _mlp_fwd(h3, row(norm_mlp_g[1]), w1[1], w2[1], name="l1_mlp")

    sq, dh4 = _loss_head(h4, target, name="loss_head")

    dh3, dg_mlp1, dw1_1, dw2_1 = _mlp_bwd(h3, row(norm_mlp_g[1]), w1[1], w2[1], mlp1, dh4, name="l1_mlp_bwd")
    do_sb = _matmul(dh3, w_o, mode="nt", out_dtypes=(F32,), name="l1_do")
    dw_o = _matmul(o_sb, dh3, mode="tn", out_dtypes=(F32,), name="l1_dwo")
    dqn, dkn, dv = _sb_bwd(qn, kn, vb, carries, do_sb, name="l1_stickbreak_bwd")
    dqkv, dqg_t, dkg_t = _qknorm_bwd(qkv, qg_t, kg_t, dqn, dkn, dv, name="l1_qknorm_bwd")
    dw_qkv = _matmul(hn1, dqkv, mode="tn", out_dtypes=(F32,), name="l1_dwqkv")
    reached("l1_grads", dw_qkv, dict(odd_w_qkv=dw_qkv, odd_w_o=dw_o, mlp_w1_1=dw1_1, mlp_w2_1=dw2_1))
    dhn1 = _matmul(dqkv, w_qkv, mode="nt", out_dtypes=(F32,), name="l1_dhn")
    dh2, dg_mix1 = _rmsnorm_bwd(h2, row(norm_mix_g[1]), dhn1, dh3, name="l1_mix_dnorm")
    reached("l1_done", dh2)

    dh1, dg_mlp0, dw1_0, dw2_0 = _mlp_bwd(h1, row(norm_mlp_g[0]), w1[0], w2[0], mlp0, dh2, name="l0_mlp_bwd")
    reached("l0_mlp_grads", dh1, dict(mlp_w1_0=dw1_0, mlp_w2_0=dw2_0))
    dcat = _matmul(dh1, w_out, mode="nt", out_dtypes=(F32,), name="l0_dcat")
    dw_out = _matmul(cat, dh1, mode="tn", out_dtypes=(F32,), name="l0_dwout")
    reached("l0_dwout", dw_out)
    dq, dk, dvr, dgate_r, dgn = _retention_bwd(proj, gn_g, tables, opre, sprev, dcat, name="l0_retention_bwd")
    reached("l0_retention_bwd", dq)
    da, dgate_c, dconv_w, dconv_b, dln_g, dln_b = _conv_bwd(proj, conv_w, row(ln_g), row(ln_b), y_conv, dcat,
                                                            name="l0_conv_bwd")
    reached("l0_conv_bwd", da)
    dproj = jnp.concatenate([dq, dk, dvr, dgate_r, da, dgate_c], axis=1)
    dw_in = _matmul(hn0, dproj, mode="tn", out_dtypes=(F32,), name="l0_dwin")
    dhn0 = _matmul(dproj, w_in, mode="nt", out_dtypes=(F32,), name="l0_dhn")
    dh0, dg_mix0 = _rmsnorm_bwd(h0, row(norm_mix_g[0]), dhn0, dh1, name="l0_mix_dnorm")

    fold = lambda t: t.reshape(SB_HEADS, SB_HEAD_DIM).sum(axis=0)
    grads = dict(
        x=dh0[PAD_FRONT + N_META:],
        meta=dh0[PAD_FRONT:PAD_FRONT + N_META],
        norm_mix_g=jnp.concatenate([dg_mix0, dg_mix1], axis=0),
        norm_mlp_g=jnp.concatenate([dg_mlp0, dg_mlp1], axis=0),
        even_w_in=dw_in,
        even_ret_gn_g=dgn.reshape(RET_HEADS, RET_V_DIM),
        even_conv_w=dconv_w,
        even_conv_b=dconv_b,
        even_conv_ln_g=dln_g,
        even_conv_ln_b=dln_b,
        even_w_out=dw_out,
        odd_w_qkv=dw_qkv,
        odd_q_norm_g=fold(dqg_t)[None],
        odd_k_norm_g=fold(dkg_t)[None],
        odd_w_o=dw_o,
        mlp_w1=(dw1_0, dw1_1),
        mlp_w2=(dw2_0, dw2_1),
    )
    return sq[0, 0], grads


def _position():
    x, y, c = lax.axis_index("x"), lax.axis_index("y"), lax.axis_index("c")
    other_chips = [(1 - x, y), (x, 1 - y), (1 - x, 1 - y)]
    return x, y, c, other_chips


def _shard_of(ref, kind, s, n):
    rows, cols = ref.shape
    if kind == "col":
        return ref.at[:, pl.ds(s * (cols // n), cols // n)]
    return ref.at[pl.ds(s * (rows // n), rows // n), :]


def _half_of(ref, kind, c):
    rows, cols = ref.shape
    if kind == "col":
        return ref.at[pl.ds(c * (rows // 2), rows // 2), :]
    return ref.at[:, pl.ds(c * (cols // 2), cols // 2)]


def _remote(src, dst, send_sems, recv_sems, idx, device):
    return pltpu.make_async_remote_copy(src_ref=src, dst_ref=dst, send_sem=send_sems.at[idx], recv_sem=recv_sems.at[idx],
                                        device_id=device, device_id_type=MESH)


def _cast_into_whole(w, kind, s_arr, *, name):
    rows, cols = w.shape
    tr = _pick(rows, (256, 128))
    nb = rows // tr
    if kind == "col":
        whole, o_spec = (rows, cols * N_CHIPS), pl.BlockSpec((tr, cols), lambda i, s_ref: (i, s_ref[0]))
    else:
        whole, o_spec = (rows * N_CHIPS, cols), pl.BlockSpec((tr, cols), lambda i, s_ref: (s_ref[0] * nb + i, 0))

    def body(s_ref, w_ref, o_ref):
        o_ref[...] = w_ref[...].astype(BF16)

    return pl.pallas_call(
        body,
        out_shape=jax.ShapeDtypeStruct(whole, BF16),
        grid_spec=pltpu.PrefetchScalarGridSpec(num_scalar_prefetch=1, grid=(nb,),
                                               in_specs=[pl.BlockSpec((tr, cols), lambda i, s_ref: (i, 0))],
                                               out_specs=o_spec),
        compiler_params=_params("parallel"),
        name=name,
    )(s_arr, w)


def _allgather_weights(wholes, kinds):
    n = len(wholes)

    def body(*refs):
        ins, outs = refs[:n], refs[n:2 * n]
        send_sems, recv_sems = refs[2 * n:]
        x, y, c, chips = _position()
        me_chip = 2 * x + y
        sibling = (x, y, 1 - c)
        sends = []
        for t in range(n):
            for k, (cx, cy) in enumerate(chips):
                src = _half_of(_shard_of(ins[t], kinds[t], me_chip, N_CHIPS), kinds[t], c)
                dst = _half_of(_shard_of(outs[t], kinds[t], me_chip, N_CHIPS), kinds[t], c)
                sends.append(_remote(src, dst, send_sems, recv_sems, 6 * t + k, (cx, cy, c)))
        for cp in sends:
            cp.start()
        passed = []
        for t in range(n):
            for k, (cx, cy) in enumerate(chips):
                landed = _half_of(_shard_of(outs[t], kinds[t], 2 * cx + cy, N_CHIPS), kinds[t], c)
                _remote(landed, landed, send_sems, recv_sems, 6 * t + k, (cx, cy, c)).wait_recv()
                fwd = _remote(landed, landed, send_sems, recv_sems, 6 * t + 3 + k, sibling)
                fwd.start()
                passed.append(fwd)
        for t in range(n):
            for k, (cx, cy) in enumerate(chips):
                theirs = _half_of(_shard_of(outs[t], kinds[t], 2 * cx + cy, N_CHIPS), kinds[t], 1 - c)
                _remote(theirs, theirs, send_sems, recv_sems, 6 * t + 3 + k, sibling).wait_recv()
        for cp in sends + passed:
            cp.wait_send()

    return pl.pallas_call(
        body,
        out_shape=[jax.ShapeDtypeStruct(w.shape, BF16) for w in wholes],
        in_specs=[ANY] * n,
        out_specs=[ANY] * n,
        input_output_aliases={t: t for t in range(n)},
        scratch_shapes=[pltpu.SemaphoreType.DMA((6 * n,)), pltpu.SemaphoreType.DMA((6 * n,))],
        name="allgather_weights",
    )(*wholes)


HBM = pl.BlockSpec(memory_space=pltpu.HBM)
SEM = pl.BlockSpec(memory_space=pltpu.SEMAPHORE)
DATAFLOW = pltpu.SideEffectType.DATAFLOW_SIDE_EFFECTING
TARGETS = 6


def _gather_copies(kinds, refs, _, send_sems, recv_sems):
    x, y, c, chips = _position()
    me_chip = 2 * x + y
    sends, lands = [], []
    for t, (ref, kind) in enumerate(zip(refs, kinds)):
        mine = _half_of(_shard_of(ref, kind, me_chip, N_CHIPS), kind, c)
        for k, (cx, cy) in enumerate(chips):
            for other_core in range(2):
                j = TARGETS * t + 2 * k + other_core
                peer_c = 1 - c if other_core else c
                sends.append(_remote(mine, mine, send_sems, recv_sems, j, (cx, cy, peer_c)))
                theirs = _half_of(_shard_of(ref, kind, 2 * cx + cy, N_CHIPS), kind, peer_c)
                lands.append(_remote(theirs, theirs, send_sems, recv_sems, j, (cx, cy, peer_c)))
    return sends, lands


def _pair_swap_copies(kinds, srcs, lands, send_sems, recv_sems):
    x, y, c, _ = _position()
    sibling = (x, y, 1 - c)
    sends = [_remote(_half_of(srcs[t], kinds[t], 1 - c), lands[t], send_sems, recv_sems, t, sibling) for t in range(len(srcs))]
    arrivals = [_remote(_half_of(srcs[t], kinds[t], c), lands[t], send_sems, recv_sems, t, sibling) for t in range(len(srcs))]
    return sends, arrivals


def _chip_exchange_copies(kinds, srcs, lands, send_sems, recv_sems):
    x, y, c, chips = _position()
    sends, arrivals = [], []
    for t in range(len(srcs)):
        for k, (cx, cy) in enumerate(chips):
            src = _shard_of(srcs[t], kinds[t], 2 * cx + cy, N_CHIPS)
            sends.append(_remote(src, lands[t].at[k], send_sems, recv_sems, 3 * t + k, (cx, cy, c)))
            arrivals.append(_remote(src, lands[t].at[k], send_sems, recv_sems, 3 * t + k, (cx, cy, c)))
    return sends, arrivals


def _pair_gather_copies(kinds, srcs, lands, send_sems, recv_sems):
    x, y, c, _ = _position()
    sibling = (x, y, 1 - c)
    sends, arrivals = [], []
    for t in range(len(srcs)):
        mine, theirs = _half_of(srcs[t], kinds[t], c), _half_of(srcs[t], kinds[t], 1 - c)
        sends.append(_remote(mine, mine, send_sems, recv_sems, t, sibling))
        arrivals.append(_remote(theirs, theirs, send_sems, recv_sems, t, sibling))
    return sends, arrivals


def _copies_start(plan, n_sems, srcs, lands, follows, *, name):
    ns, n = len(srcs), len(srcs) + len(lands)

    def body(*refs):
        send_sems, recv_sems = refs[n + 1], refs[n + 2]
        thru, token = refs[n + 3:2 * n + 3], refs[2 * n + 3]
        sends, _ = plan(thru[:ns], thru[ns:], send_sems, recv_sems)
        for cp in sends:
            cp.start()
        token[...] = jnp.zeros_like(token)

    arrays = [pltpu.with_memory_space_constraint(a, pltpu.HBM) for a in list(srcs) + list(lands)]
    outs = pl.pallas_call(
        body,
        name=name,
        out_shape=(pltpu.SemaphoreType.DMA((n_sems,)), pltpu.SemaphoreType.DMA((n_sems,)),
                   *[pltpu.HBM(a.shape, a.dtype) for a in arrays], jax.ShapeDtypeStruct((8, 128), F32)),
        in_specs=(*[HBM] * n, ANY),
        out_specs=(SEM, SEM, *[HBM] * n, pl.BlockSpec(memory_space=pltpu.VMEM)),
        input_output_aliases={t: 2 + t for t in range(n)},
        compiler_params=pltpu.CompilerParams(has_side_effects=DATAFLOW),
    )(*arrays, follows)
    return outs[0], outs[1], list(outs[2:2 + ns]), list(outs[2 + ns:2 + n]), outs[2 + n]


def _copies_wait(plan, started, follows, *, name):
    send_sems, recv_sems, srcs, lands, _ = started
    ns, n = len(srcs), len(srcs) + len(lands)

    def body(*refs):
        ins, s_sems, r_sems = refs[:n], refs[n], refs[n + 1]
        sends, arrivals = plan(ins[:ns], ins[ns:], s_sems, r_sems)
        for cp in sends:
            cp.wait_send()
        for cp in arrivals:
            cp.wait_recv()

    outs = pl.pallas_call(
        body,
        name=name,
        out_shape=tuple(pltpu.HBM(a.shape, a.dtype) for a in srcs + lands),
        in_specs=(*[HBM] * n, SEM, SEM, ANY),
        out_specs=tuple([HBM] * n),
        input_output_aliases={t: t for t in range(n)},
        compiler_params=pltpu.CompilerParams(has_side_effects=DATAFLOW),
    )(*srcs, *lands, send_sems, recv_sems, follows)
    return list(outs[:ns]), list(outs[ns:])


def _allgather8(block, *, name):
    rows, cols = block.shape

    def body(in_ref, out_ref, send_sems, recv_sems, local_sem):
        x, y, c, _ = _position()
        me = 4 * x + 2 * y + c
        mine = pltpu.make_async_copy(in_ref, out_ref.at[me], local_sem)
        mine.start()
        peers = []
        for flip in range(1, N_DEV):
            fx, fy, fc = (flip >> 2) & 1, (flip >> 1) & 1, flip & 1
            peers.append(((1 - x if fx else x), (1 - y if fy else y), (1 - c if fc else c)))
        sends = [_remote(in_ref, out_ref.at[me], send_sems, recv_sems, j, peer) for j, peer in enumerate(peers)]
        for cp in sends:
            cp.start()
        for j, (px, py, pc) in enumerate(peers):
            slot = out_ref.at[4 * px + 2 * py + pc]
            _remote(slot, slot, send_sems, recv_sems, j, (px, py, pc)).wait_recv()
        for cp in sends:
            cp.wait_send()
        mine.wait()

    vmem = pl.BlockSpec(memory_space=pltpu.VMEM)
    return pl.pallas_call(
        body,
        out_shape=jax.ShapeDtypeStruct((N_DEV, rows, cols), F32),
        in_specs=[vmem],
        out_specs=vmem,
        scratch_shapes=[pltpu.SemaphoreType.DMA((N_DEV - 1,)), pltpu.SemaphoreType.DMA((N_DEV - 1,)),
                        pltpu.SemaphoreType.DMA],
        name=name,
    )(block)


def _sum8(stack, *, name):
    _, rows, cols = stack.shape

    def body(s_ref, o_ref):
        acc = s_ref[0]
        for i in range(1, N_DEV):
            acc = acc + s_ref[i]
        o_ref[...] = acc

    return pl.pallas_call(body, out_shape=jax.ShapeDtypeStruct((rows, cols), F32), name=name)(stack)


def _swap_halves_in(grads, kinds):
    n = len(grads)

    def body(*refs):
        ins, outs = refs[:n], refs[n:2 * n]
        send_sems, recv_sems = refs[2 * n:]
        x, y, c, _ = _position()
        sibling = (x, y, 1 - c)
        sends = [_remote(_half_of(ins[t], kinds[t], 1 - c), outs[t], send_sems, recv_sems, t, sibling) for t in range(n)]
        for cp in sends:
            cp.start()
        for t in range(n):
            _remote(_half_of(ins[t], kinds[t], c), outs[t], send_sems, recv_sems, t, sibling).wait_recv()
        for cp in sends:
            cp.wait_send()

    def half(g, kind):
        rows, cols = g.shape
        return (rows // 2, cols) if kind == "col" else (rows, cols // 2)

    return pl.pallas_call(
        body,
        out_shape=[jax.ShapeDtypeStruct(half(g, k), F32) for g, k in zip(grads, kinds)],
        in_specs=[ANY] * n,
        out_specs=[ANY] * n,
        scratch_shapes=[pltpu.SemaphoreType.DMA((n,)), pltpu.SemaphoreType.DMA((n,))],
        name="reduce_core_pair",
    )(*grads)


def _half_add(grad, theirs, kind, c_arr, *, name):
    rows, cols = theirs.shape
    tr = _pick(rows, (256, 128))
    nb = rows // tr
    if kind == "col":
        g_spec = pl.BlockSpec((tr, cols), lambda i, c_ref: (c_ref[0] * nb + i, 0))
    else:
        g_spec = pl.BlockSpec((tr, cols), lambda i, c_ref: (i, c_ref[0]))
    t_spec = pl.BlockSpec((tr, cols), lambda i, c_ref: (i, 0))

    def body(c_ref, g_ref, t_ref, o32_ref, o16_ref):
        tot = g_ref[...] + t_ref[...]
        o32_ref[...] = tot
        o16_ref[...] = tot.astype(BF16)

    return pl.pallas_call(
        body,
        out_shape=[jax.ShapeDtypeStruct((rows, cols), F32), jax.ShapeDtypeStruct((rows, cols), BF16)],
        grid_spec=pltpu.PrefetchScalarGridSpec(num_scalar_prefetch=1, grid=(nb,), in_specs=[g_spec, t_spec],
                                               out_specs=[t_spec, t_spec]),
        compiler_params=_params("parallel"),
        name=name,
    )(c_arr, grad, theirs)


def _exchange_chips(parts, kinds):
    n = len(parts)

    def body(*refs):
        ins, outs = refs[:n], refs[n:2 * n]
        send_sems, recv_sems = refs[2 * n:]
        x, y, c, chips = _position()
        sends = []
        for t in range(n):
            for k, (cx, cy) in enumerate(chips):
                src = _shard_of(ins[t], kinds[t], 2 * cx + cy, N_CHIPS)
                sends.append(_remote(src, outs[t].at[k], send_sems, recv_sems, 3 * t + k, (cx, cy, c)))
        for cp in sends:
            cp.start()
        for t in range(n):
            for k, (cx, cy) in enumerate(chips):
                src = _shard_of(ins[t], kinds[t], 2 * cx + cy, N_CHIPS)
                _remote(src, outs[t].at[k], send_sems, recv_sems, 3 * t + k, (cx, cy, c)).wait_recv()
        for cp in sends:
            cp.wait_send()

    def piece(p, kind):
        rows, cols = p.shape
        return (3, rows, cols // N_CHIPS) if kind == "col" else (3, rows // N_CHIPS, cols)

    return pl.pallas_call(
        body,
        out_shape=[jax.ShapeDtypeStruct(piece(p, k), BF16) for p, k in zip(parts, kinds)],
        in_specs=[ANY] * n,
        out_specs=[ANY] * n,
        scratch_shapes=[pltpu.SemaphoreType.DMA((3 * n,)), pltpu.SemaphoreType.DMA((3 * n,))],
        name="reduce_chips",
    )(*parts)


def _shard_sum(part32, recv, kind, sc_arr, *, name):
    _, rows, cols = recv.shape
    tr = _pick(rows, (256, 128))
    nb = rows // tr
    if kind == "col":
        whole = (2 * rows, cols)
        p_spec = pl.BlockSpec((tr, cols), lambda i, sc: (i, sc[0]))
        o_spec = pl.BlockSpec((tr, cols), lambda i, sc: (sc[1] * nb + i, 0))
    else:
        whole = (rows, 2 * cols)
        p_spec = pl.BlockSpec((tr, cols), lambda i, sc: (sc[0] * nb + i, 0))
        o_spec = pl.BlockSpec((tr, cols), lambda i, sc: (i, sc[1]))
    r_spec = pl.BlockSpec((3, tr, cols), lambda i, sc: (0, i, 0))

    def body(sc_ref, p_ref, r_ref, o_ref):
        acc = p_ref[...]
        for k in range(3):
            acc = acc + r_ref[k].astype(F32)
        o_ref[...] = acc

    return pl.pallas_call(
        body,
        out_shape=jax.ShapeDtypeStruct(whole, F32),
        grid_spec=pltpu.PrefetchScalarGridSpec(num_scalar_prefetch=1, grid=(nb,), in_specs=[p_spec, r_spec],
                                               out_specs=o_spec),
        compiler_params=_params("parallel"),
        name=name,
    )(sc_arr, part32, recv)


def _swap_halves_out(shards, kinds):
    n = len(shards)

    def body(*refs):
        ins, outs = refs[:n], refs[n:2 * n]
        send_sems, recv_sems = refs[2 * n:]
        x, y, c, _ = _position()
        sibling = (x, y, 1 - c)
        sends = [_remote(_half_of(ins[t], kinds[t], c), _half_of(outs[t], kinds[t], c), send_sems, recv_sems, t, sibling)
                 for t in range(n)]
        for cp in sends:
            cp.start()
        for t in range(n):
            theirs = _half_of(outs[t], kinds[t], 1 - c)
            _remote(theirs, theirs, send_sems, recv_sems, t, sibling).wait_recv()
        for cp in sends:
            cp.wait_send()

    return pl.pallas_call(
        body,
        out_shape=[jax.ShapeDtypeStruct(s.shape, F32) for s in shards],
        in_specs=[ANY] * n,
        out_specs=[ANY] * n,
        input_output_aliases={t: t for t in range(n)},
        scratch_shapes=[pltpu.SemaphoreType.DMA((n,)), pltpu.SemaphoreType.DMA((n,))],
        name="gather_core_pair",
    )(*shards)


def _adamw(w, g, m, v, *, name):
    rows, cols = w.shape
    tr = _pick(rows, (256, 128)) if rows * cols > 64 * 1024 else rows

    def body(w_ref, g_ref, m_ref, v_ref, d_ref, nm_ref, nv_ref):
        gv = g_ref[...]
        nm = ADAM_B1 * m_ref[...] + (1.0 - ADAM_B1) * gv
        nv = ADAM_B2 * v_ref[...] + (1.0 - ADAM_B2) * jnp.square(gv)
        m_hat = nm / (1.0 - ADAM_B1 ** ADAM_STEP)
        v_hat = nv / (1.0 - ADAM_B2 ** ADAM_STEP)
        d_ref[...] = -ADAM_LR * (m_hat / (jnp.sqrt(v_hat) + ADAM_EPS) + ADAM_WD * w_ref[...])
        nm_ref[...] = nm
        nv_ref[...] = nv

    spec = pl.BlockSpec((tr, cols), lambda i: (i, 0))
    return pl.pallas_call(
        body,
        out_shape=[jax.ShapeDtypeStruct((rows, cols), F32)] * 3,
        grid=(rows // tr,),
        in_specs=[spec] * 4,
        out_specs=[spec] * 3,
        compiler_params=_params("parallel"),
        name=name,
    )(w, g, m, v)


BIG = ("even_w_in", "odd_w_qkv", "mlp_w1_0", "mlp_w1_1", "even_w_out", "odd_w_o", "mlp_w2_0", "mlp_w2_1")
BIG_KIND = ("col", "col", "col", "col", "row", "row", "row", "row")


class _TravellingReduction:
    def __init__(self, tag, names, kinds, c_arr, sc_arr):
        self.tag, self.names, self.kinds, self.c_arr, self.sc_arr = tag, names, kinds, c_arr, sc_arr
        self.swap = functools.partial(_pair_swap_copies, kinds)
        self.exchange = functools.partial(_chip_exchange_copies, kinds)
        self.gather = functools.partial(_pair_gather_copies, kinds)

    def pair_swap_start(self, grads, follows):
        half = lambda g, kind: (g.shape[0] // 2, g.shape[1]) if kind == "col" else (g.shape[0], g.shape[1] // 2)
        lands = [lax.empty(half(g, k), F32) for g, k in zip(grads, self.kinds)]
        self.started = _copies_start(self.swap, len(grads), grads, lands, follows, name=f"reduce_{self.tag}_pair_start")

    def pair_swap_finish(self, after):
        grads, theirs = _copies_wait(self.swap, self.started, after, name=f"reduce_{self.tag}_pair_wait")
        self.sums = [_half_add(g, th, k, self.c_arr, name="pair_sum_" + n)
                     for g, th, k, n in zip(grads, theirs, self.kinds, self.names)]

    def chips_start(self, follows):
        parts = [s16 for _, s16 in self.sums]
        piece = lambda p, kind: (3, p.shape[0], p.shape[1] // N_CHIPS) if kind == "col" else (3, p.shape[0] // N_CHIPS, p.shape[1])
        lands = [lax.empty(piece(p, k), BF16) for p, k in zip(parts, self.kinds)]
        self.started = _copies_start(self.exchange, 3 * len(parts), parts, lands, follows,
                                     name=f"reduce_{self.tag}_chips_start")

    def chips_finish(self, after):
        _, recv = _copies_wait(self.exchange, self.started, after, name=f"reduce_{self.tag}_chips_wait")
        self.halves = [_shard_sum(s32, r, k, self.sc_arr, name="chip_sum_" + n)
                       for (s32, _), r, k, n in zip(self.sums, recv, self.kinds, self.names)]

    def pair_gather_start(self, follows):
        self.started = _copies_start(self.gather, len(self.halves), self.halves, [], follows,
                                     name=f"reduce_{self.tag}_gather_start")

    def pair_gather_finish(self, after):
        shards, _ = _copies_wait(self.gather, self.started, after, name=f"reduce_{self.tag}_gather_wait")
        return dict(zip(self.names, shards))
SUBLANES = 8


def _pack_rows(parts, width):
    padded, offsets, r0 = [], [], 0
    for t in parts:
        rows = -(-t.shape[0] // SUBLANES) * SUBLANES
        padded.append(jnp.pad(t, ((0, rows - t.shape[0]), (0, width - t.shape[1]))))
        offsets.append(r0)
        r0 += rows
    return jnp.concatenate(padded, axis=0), offsets


def kernel(x, meta, norm_mix_g, norm_mlp_g, even_w_in, even_ret_gn_g, even_conv_w, even_conv_b, even_conv_ln_g, even_conv_ln_b, even_w_out, odd_w_qkv, odd_q_norm_g, odd_k_norm_g, odd_w_o, mlp_w1, mlp_w2, loss_target, m_meta, m_norm_mix_g, m_norm_mlp_g, m_even_w_in, m_even_ret_gn_g, m_even_conv_w, m_even_conv_b, m_even_conv_ln_g, m_even_conv_ln_b, m_even_w_out, m_odd_w_qkv, m_odd_q_norm_g, m_odd_k_norm_g, m_odd_w_o, m_mlp_w1, m_mlp_w2, v_meta, v_norm_mix_g, v_norm_mlp_g, v_even_w_in, v_even_ret_gn_g, v_even_conv_w, v_even_conv_b, v_even_conv_ln_g, v_even_conv_ln_b, v_even_w_out, v_odd_w_qkv, v_odd_q_norm_g, v_odd_k_norm_g, v_odd_w_o, v_mlp_w1, v_mlp_w2):
    d = D_MODEL
    xi, yi, ci = lax.axis_index("x"), lax.axis_index("y"), lax.axis_index("c")
    chip = 2 * xi + yi
    c_arr = jnp.reshape(ci, (1,)).astype(jnp.int32)
    s_arr = jnp.reshape(chip, (1,)).astype(jnp.int32)

    def split_big(w_in, w_qkv, w1, w_out, w_o, w2):
        return dict(zip(BIG, (w_in[0], w_qkv[0], w1[0], w1[1], w_out[0], w_o[0], w2[0], w2[1])))

    w_big = split_big(even_w_in, odd_w_qkv, mlp_w1, even_w_out, odd_w_o, mlp_w2)
    m_big = split_big(m_even_w_in, m_odd_w_qkv, m_mlp_w1, m_even_w_out, m_odd_w_o, m_mlp_w2)
    v_big = split_big(v_even_w_in, v_odd_w_qkv, v_mlp_w1, v_even_w_out, v_odd_w_o, v_mlp_w2)

    placed = {n: _cast_into_whole(w_big[n], k, s_arr, name="cast_" + n) for n, k in zip(BIG, BIG_KIND)}
    kind_of = dict(zip(BIG, BIG_KIND))
    (w_in_full,) = _allgather_weights([placed["even_w_in"]], [kind_of["even_w_in"]])
    packed, (r_meta, r_conv, r_gn) = _pack_rows([meta, even_conv_w[0], even_ret_gn_g[0]], d // N_CHIPS)
    gathered = _allgather8(packed, name="allgather_small_params")[0::2]
    groups = dict(l0=("even_w_out", "mlp_w1_0", "mlp_w2_0"), qkv=("odd_w_qkv",), l1=("odd_w_o", "mlp_w1_1", "mlp_w2_1"))
    in_flight, follows = {}, gathered[0, 0:1, 0:1] + w_in_full[0:1, 0:1].astype(F32)
    for group, names in groups.items():
        plan = functools.partial(_gather_copies, [kind_of[n] for n in names])
        in_flight[group] = (plan, _copies_start(plan, TARGETS * len(names), [placed[n] for n in names], [], follows,
                                                name="gather_" + group + "_start"))
        follows = in_flight[group][1][-1]
    started = follows[0:1, 0:1]

    def later(group, after):
        plan, state = in_flight[group]
        return _copies_wait(plan, state, after, name="gather_" + group + "_wait")[0]

    sc_arr = jnp.concatenate([s_arr, c_arr])
    early = ("odd_w_qkv", "odd_w_o", "mlp_w1_1", "mlp_w2_1"), ("mlp_w1_0", "mlp_w2_0")
    red_l1, red_m0 = (_TravellingReduction(tag, names, [kind_of[n] for n in names], c_arr, sc_arr)
                      for tag, names in zip(("l1", "m0"), early))
    grad_big = {}

    def reached(point, after, grads=None):
        if point == "l1_grads":
            red_l1.pair_swap_start([grads[n] for n in red_l1.names], after)
        elif point == "l1_done":
            red_l1.pair_swap_finish(after)
            red_l1.chips_start(after)
        elif point == "l0_mlp_grads":
            red_m0.pair_swap_start([grads[n] for n in red_m0.names], after)
        elif point == "l0_dwout":
            red_m0.pair_swap_finish(after)
            red_m0.chips_start(after)
        elif point == "l0_retention_bwd":
            red_l1.chips_finish(after)
            red_l1.pair_gather_start(after)
        elif point == "l0_conv_bwd":
            red_m0.chips_finish(after)
            red_m0.pair_gather_start(after)
            grad_big.update(red_l1.pair_gather_finish(after))

    across = lambda r0, rows, width: jnp.concatenate([gathered[s, r0:r0 + rows, 0:width] for s in range(N_CHIPS)], axis=1)
    meta_full = across(r_meta, N_META, d // N_CHIPS) + started
    conv_w_full = across(r_conv, CONV_WIDTH, d // N_CHIPS)
    gn_full = across(r_gn, RET_HEADS, RET_V_DIM // N_CHIPS)

    sq, g = _local_step(
        x[0], loss_target[0], meta_full, norm_mix_g, norm_mlp_g, w_in_full, gn_full, conv_w_full,
        even_conv_b[0], even_conv_ln_g[0], even_conv_ln_b[0], odd_q_norm_g[0], odd_k_norm_g[0], later, reached)
    grad_big.update(red_m0.pair_gather_finish(g["even_w_in"]))
    loss = lax.psum(0.5 * sq / d, ("x", "y", "c"))

    small_names = ("norm_mix_g", "norm_mlp_g", "even_conv_b", "even_conv_ln_g", "even_conv_ln_b", "odd_q_norm_g",
                   "odd_k_norm_g", "meta", "even_conv_w", "even_ret_gn_g")
    pack, offsets = _pack_rows([g[n] for n in small_names], d)
    summed = _sum8(_allgather8(pack, name="allgather_small_grads"), name="sum_small_grads")
    small = {n: summed[r0:r0 + g[n].shape[0], 0:g[n].shape[1]] for n, r0 in zip(small_names, offsets)}
    for n in ("meta", "even_conv_w", "even_ret_gn_g"):
        width = small[n].shape[1] // N_CHIPS
        small[n] = lax.dynamic_slice_in_dim(small[n], chip * width, width, axis=1)

    last = ("even_w_in", "even_w_out")
    last_kinds = [kind_of[n] for n in last]
    g_last = [g[n] for n in last]
    theirs = _swap_halves_in(g_last, last_kinds)
    sums = [_half_add(gb, th, k, c_arr, name="pair_sum_" + n) for gb, th, k, n in zip(g_last, theirs, last_kinds, last)]
    recv = _exchange_chips([s16 for _, s16 in sums], last_kinds)
    halves = [_shard_sum(s32, r, k, sc_arr, name="chip_sum_" + n) for (s32, _), r, k, n in zip(sums, recv, last_kinds, last)]
    grad_big.update(zip(last, _swap_halves_out(halves, last_kinds)))

    upd = {n: _adamw(w_big[n], grad_big[n], m_big[n], v_big[n], name="adamw_" + n) for n in BIG}

    def join(name, idx, lead):
        if name in ("mlp_w1", "mlp_w2"):
            return jnp.stack([upd[name + "_0"][idx], upd[name + "_1"][idx]]) if idx >= 0 else jnp.stack(
                [grad_big[name + "_0"], grad_big[name + "_1"]])
        t = upd[name][idx] if idx >= 0 else grad_big[name]
        return t[None] if lead else t

    small_w = dict(meta=meta, norm_mix_g=norm_mix_g, norm_mlp_g=norm_mlp_g, even_ret_gn_g=even_ret_gn_g[0],
                   even_conv_w=even_conv_w[0], even_conv_b=even_conv_b, even_conv_ln_g=even_conv_ln_g,
                   even_conv_ln_b=even_conv_ln_b, odd_q_norm_g=odd_q_norm_g, odd_k_norm_g=odd_k_norm_g)
    small_m = dict(meta=m_meta, norm_mix_g=m_norm_mix_g, norm_mlp_g=m_norm_mlp_g, even_ret_gn_g=m_even_ret_gn_g[0],
                   even_conv_w=m_even_conv_w[0], even_conv_b=m_even_conv_b, even_conv_ln_g=m_even_conv_ln_g,
                   even_conv_ln_b=m_even_conv_ln_b, odd_q_norm_g=m_odd_q_norm_g, odd_k_norm_g=m_odd_k_norm_g)
    small_v = dict(meta=v_meta, norm_mix_g=v_norm_mix_g, norm_mlp_g=v_norm_mlp_g, even_ret_gn_g=v_even_ret_gn_g[0],
                   even_conv_w=v_even_conv_w[0], even_conv_b=v_even_conv_b, even_conv_ln_g=v_even_conv_ln_g,
                   even_conv_ln_b=v_even_conv_ln_b, odd_q_norm_g=v_odd_q_norm_g, odd_k_norm_g=v_odd_k_norm_g)
    small_upd = {n: _adamw(small_w[n], small[n], small_m[n], small_v[n], name="adamw_" + n) for n in small_w}
    leading = ("even_ret_gn_g", "even_conv_w")

    order = ("meta", "norm_mix_g", "norm_mlp_g", "even_w_in", "even_ret_gn_g", "even_conv_w", "even_conv_b",
             "even_conv_ln_g", "even_conv_ln_b", "even_w_out", "odd_w_qkv", "odd_q_norm_g", "odd_k_norm_g", "odd_w_o",
             "mlp_w1", "mlp_w2")
    big_lead = ("even_w_in", "even_w_out", "odd_w_qkv", "odd_w_o")

    def leaf(name, idx):
        if name in small_w:
            t = small_upd[name][idx] if idx >= 0 else small[name]
            return t[None] if name in leading else t
        return join(name, idx, name in big_lead)

    outs = [loss, g["x"][None]]
    for idx in (-1, 0, 1, 2):
        outs += [leaf(n, idx) for n in order]
    return tuple(outs)
```

```python
import functools

import jax
import jax.numpy as jnp
from jax import lax
from jax.experimental import pallas as pl
from jax.experimental.pallas import tpu as pltpu

F32 = jnp.float32
BF16 = jnp.bfloat16

D_MODEL = 1024
N_META = 16
CHUNK = 128
PAD_FRONT = (-N_META) % CHUNK
RET_HEADS = 4
RET_QK_DIM = 128
RET_V_DIM = 256
RET_QK_W = RET_HEADS * RET_QK_DIM
RET_V_W = RET_HEADS * RET_V_DIM
CONV_WIDTH = 31
CONV_HALO = 32
RET_DECAY_OFFSET = 5.0
ROPE_BASE = 10000.0
SB_HEADS = 16
SB_HEAD_DIM = 64
D_FF = 4 * D_MODEL
EPS = 1e-6
ADAM_LR = 0.001
ADAM_B1 = 0.9
ADAM_B2 = 0.999
ADAM_EPS = 1e-08
ADAM_WD = 0.01
ADAM_STEP = 10

N_CHIPS = 4
N_DEV = 8
VMEM_LIMIT = 56 * 1024 * 1024
MESH = pl.DeviceIdType.MESH
ANY = pl.BlockSpec(memory_space=pl.ANY)


def _params(*sem):
    return pltpu.CompilerParams(dimension_semantics=sem, vmem_limit_bytes=VMEM_LIMIT)


def _pick(n, cands):
    for c in cands:
        if n % c == 0:
            return c
    return n


def _sigmoid(x):
    return 1.0 / (1.0 + jnp.exp(-x))


def _dot(a, b):
    return lax.dot_general(a, b, (((1,), (0,)), ((), ())), preferred_element_type=F32)


def _dot_nt(a, b):
    return lax.dot_general(a, b, (((1,), (1,)), ((), ())), preferred_element_type=F32)


def _dot_tn(a, b):
    return lax.dot_general(a, b, (((0,), (0,)), ((), ())), preferred_element_type=F32)


def _split_dot(x, m):
    hi = x.astype(BF16)
    lo = (x - hi.astype(F32)).astype(BF16)
    return _dot(hi, m) + _dot(lo, m)


def _matmul(a, b, *, mode, out_dtypes, epilogue=None, extras=(), name, after=None):
    if mode == "nn":
        (m, k), (k2, n) = a.shape, b.shape
    elif mode == "nt":
        (m, k), (n, k2) = a.shape, b.shape
    else:
        (k, m), (k2, n) = a.shape, b.shape
    assert k == k2, (a.shape, b.shape, mode)
    tm = _pick(m, (1056, 1024, 768, 512, 384, 256, 128, 96))
    tn = _pick(n, (1024, 768, 512, 256, 128))
    tk = _pick(k, (1056, 1024, 768, 512, 384, 256, 128, 96))
    nk = k // tk
    dot = {"nn": _dot, "nt": _dot_nt, "tn": _dot_tn}[mode]
    n_extra, n_out = len(extras), len(out_dtypes)
    n_after = 0 if after is None else 1
    if epilogue is None:
        epilogue = lambda acc: (acc,)

    def body(a_ref, b_ref, *rest):
        extra_refs = rest[:n_extra]
        out_refs = rest[n_extra + n_after:n_extra + n_after + n_out]
        part = dot(a_ref[...].astype(BF16), b_ref[...].astype(BF16))

        def finish(acc):
            res = epilogue(acc, *[r[...] for r in extra_refs])
            for o_ref, r in zip(out_refs, res):
                o_ref[...] = r.astype(o_ref.dtype)

        if nk == 1:
            finish(part)
        else:
            acc_ref = rest[-1]
            kk = pl.program_id(2)

            @pl.when(kk == 0)
            def _():
                acc_ref[...] = part

            @pl.when(kk > 0)
            def _():
                acc_ref[...] += part

            @pl.when(kk == nk - 1)
            def _():
                finish(acc_ref[...])

    if mode == "nn":
        a_spec = pl.BlockSpec((tm, tk), lambda i, j, kk: (i, kk))
        b_spec = pl.BlockSpec((tk, tn), lambda i, j, kk: (kk, j))
    elif mode == "nt":
        a_spec = pl.BlockSpec((tm, tk), lambda i, j, kk: (i, kk))
        b_spec = pl.BlockSpec((tn, tk), lambda i, j, kk: (j, kk))
    else:
        a_spec = pl.BlockSpec((tk, tm), lambda i, j, kk: (kk, i))
        b_spec = pl.BlockSpec((tk, tn), lambda i, j, kk: (kk, j))
    o_spec = pl.BlockSpec((tm, tn), lambda i, j, kk: (i, j))
    outs = pl.pallas_call(
        body,
        out_shape=[jax.ShapeDtypeStruct((m, n), dt) for dt in out_dtypes],
        grid=(m // tm, n // tn, nk),
        in_specs=[a_spec, b_spec] + [o_spec] * n_extra + [ANY] * n_after,
        out_specs=[o_spec] * n_out,
        scratch_shapes=[pltpu.VMEM((tm, tn), F32)] if nk > 1 else [],
        compiler_params=_params("parallel", "parallel", "arbitrary"),
        name=name,
    )(a, b, *extras, *([] if after is None else [after]))
    return outs[0] if n_out == 1 else outs


def _add_epilogue(acc, res):
    return (res + acc,)


def _rmsnorm_fwd(x, g, *, name):
    p, d = x.shape
    rows = _pick(p, (384, 128, 96))

    def body(x_ref, g_ref, o_ref):
        xv = x_ref[...]
        r = lax.rsqrt(jnp.mean(xv * xv, axis=-1, keepdims=True) + EPS)
        o_ref[...] = (xv * r * g_ref[...]).astype(o_ref.dtype)

    return pl.pallas_call(
        body,
        out_shape=jax.ShapeDtypeStruct((p, d), BF16),
        grid=(p // rows,),
        in_specs=[pl.BlockSpec((rows, d), lambda i: (i, 0)), pl.BlockSpec((1, d), lambda i: (0, 0))],
        out_specs=pl.BlockSpec((rows, d), lambda i: (i, 0)),
        compiler_params=_params("parallel"),
        name=name,
    )(x, g)


def _rmsnorm_bwd(x, g, dy, dres, *, name):
    p, d = x.shape
    rows = _pick(p, (384, 128, 96))

    def body(x_ref, g_ref, dy_ref, dres_ref, dx_ref, dg_ref):
        xv = x_ref[...]
        r = lax.rsqrt(jnp.mean(xv * xv, axis=-1, keepdims=True) + EPS)
        dyv = dy_ref[...]
        gdy = dyv * g_ref[...]
        proj = jnp.mean(xv * gdy, axis=-1, keepdims=True)
        dx_ref[...] = dres_ref[...] + r * gdy - xv * (r * r * r) * proj
        part = jnp.sum(dyv * xv * r, axis=0, keepdims=True)

        @pl.when(pl.program_id(0) == 0)
        def _():
            dg_ref[...] = part

        @pl.when(pl.program_id(0) > 0)
        def _():
            dg_ref[...] += part

    row_spec = pl.BlockSpec((rows, d), lambda i: (i, 0))
    vec_spec = pl.BlockSpec((1, d), lambda i: (0, 0))
    return pl.pallas_call(
        body,
        out_shape=[jax.ShapeDtypeStruct((p, d), F32), jax.ShapeDtypeStruct((1, d), F32)],
        grid=(p // rows,),
        in_specs=[row_spec, vec_spec, row_spec, row_spec],
        out_specs=[row_spec, vec_spec],
        compiler_params=_params("arbitrary"),
        name=name,
    )(x, g, dy, dres)


def _mlp_fwd(h, g, w1, w2, *, name):
    hn = _rmsnorm_fwd(h, g, name=name + "_norm")

    def act(acc):
        r = jnp.maximum(acc, 0.0)
        return acc, r * r

    z, a2 = _matmul(hn, w1, mode="nn", out_dtypes=(F32, BF16), epilogue=act, name=name + "_up")
    out = _matmul(a2, w2, mode="nn", out_dtypes=(F32,), epilogue=_add_epilogue, extras=(h,), name=name + "_down")
    return out, (hn, z, a2)


def _mlp_bwd(h, g, w1, w2, saved, dout, *, name, after=None):
    hn, z, a2 = saved

    def dact(acc, zt):
        return (acc * (2.0 * jnp.maximum(zt, 0.0)),)

    dz = _matmul(dout, w2, mode="nt", out_dtypes=(BF16,), epilogue=dact, extras=(z,), name=name + "_dz", after=after)
    dw2 = _matmul(a2, dout, mode="tn", out_dtypes=(F32,), name=name + "_dw2")
    dw1 = _matmul(hn, dz, mode="tn", out_dtypes=(F32,), name=name + "_dw1")
    dhn = _matmul(dz, w1, mode="nt", out_dtypes=(F32,), name=name + "_dhn")
    dh, dg = _rmsnorm_bwd(h, g, dhn, dout, name=name + "_dnorm")
    return dh, dg, dw1, dw2


def _retention_tables(p):
    half = RET_QK_DIM // 2
    inv_freq = ROPE_BASE ** (-jnp.arange(half, dtype=F32) / half)
    ang = jnp.arange(p, dtype=F32)[:, None] * inv_freq[None, :]
    cos, sin = jnp.cos(ang), jnp.sin(ang)
    cosf = jnp.concatenate([cos, cos], axis=1)
    sins = jnp.concatenate([-sin, sin], axis=1)
    log_g = jnp.log1p(-jnp.exp2(-RET_DECAY_OFFSET - jnp.arange(RET_HEADS, dtype=F32)))
    idx = jnp.arange(CHUNK, dtype=F32)
    diff = idx[:, None] - idx[None, :]
    inner = jnp.where(diff[None] >= 0, jnp.exp(jnp.maximum(diff, 0.0)[None] * log_g[:, None, None]), 0.0)
    kdec = jnp.exp((CHUNK - 1 - idx)[None, :] * log_g[:, None])
    qdec = jnp.exp((idx + 1.0)[None, :] * log_g[:, None])
    cdec = jnp.exp(CHUNK * log_g)
    kdec = jnp.broadcast_to(kdec[:, :, None], (RET_HEADS, CHUNK, RET_QK_DIM))
    qdec = jnp.broadcast_to(qdec[:, :, None], (RET_HEADS, CHUNK, RET_QK_DIM))
    cdec = jnp.broadcast_to(cdec[:, None, None], (RET_HEADS, RET_QK_DIM, RET_V_DIM))
    return cosf, sins, inner, kdec, qdec, cdec


def _rot(x, cosf, sins):
    return x * cosf + pltpu.roll(x, RET_QK_DIM // 2, 1) * sins


def _rot_bwd(dy, cosf, sins):
    return dy * cosf + pltpu.roll(dy * sins, RET_QK_DIM // 2, 1)


def _ret_in_specs(chunk_of):
    nh = RET_HEADS
    q_spec = pl.BlockSpec((CHUNK, RET_QK_DIM), lambda h, s: (chunk_of(s), h))
    k_spec = pl.BlockSpec((CHUNK, RET_QK_DIM), lambda h, s: (chunk_of(s), nh + h))
    v_spec = pl.BlockSpec((CHUNK, RET_V_DIM), lambda h, s: (chunk_of(s), nh + h))
    g_spec = pl.BlockSpec((CHUNK, RET_V_DIM), lambda h, s: (chunk_of(s), 2 * nh + h))
    rope_spec = pl.BlockSpec((CHUNK, RET_QK_DIM), lambda h, s: (chunk_of(s), 0))
    head_sq = pl.BlockSpec((None, CHUNK, CHUNK), lambda h, s: (h, 0, 0))
    head_qk = pl.BlockSpec((None, CHUNK, RET_QK_DIM), lambda h, s: (h, 0, 0))
    head_st = pl.BlockSpec((None, RET_QK_DIM, RET_V_DIM), lambda h, s: (h, 0, 0))
    gam_spec = pl.BlockSpec((None, 1, RET_V_DIM), lambda h, s: (h, 0, 0))
    return [q_spec, k_spec, v_spec, g_spec, rope_spec, rope_spec, head_sq, head_qk, head_qk, head_st, gam_spec]


def _retention_fwd(proj, gn_g, tables, *, name):
    p = proj.shape[0]
    n_chunks = p // CHUNK
    scale = RET_QK_DIM ** -0.5

    def body(q_ref, k_ref, v_ref, g_ref, cos_ref, sin_ref, inner_ref, kdec_ref, qdec_ref, cdec_ref, gam_ref,
             og_ref, opre_ref, sprev_ref, s_scr):
        @pl.when(pl.program_id(1) == 0)
        def _():
            s_scr[...] = jnp.zeros_like(s_scr)

        cosf, sins = cos_ref[...], sin_ref[...]
        qr = _rot(q_ref[...], cosf, sins)
        kr = _rot(k_ref[...], cosf, sins) * scale
        vb = v_ref[...].astype(BF16)
        scores = _dot_nt(qr.astype(BF16), kr.astype(BF16)) * inner_ref[...]
        state = s_scr[...]
        sprev_ref[...] = state
        o = _dot(scores.astype(BF16), vb) + _dot((qr * qdec_ref[...]).astype(BF16), state.astype(BF16))
        kd = kr * kdec_ref[...]
        s_scr[...] = cdec_ref[...] * state + _dot(kd.T.astype(BF16), vb)
        opre_ref[...] = o
        mu = jnp.mean(o, axis=-1, keepdims=True)
        oc = o - mu
        var = jnp.mean(oc * oc, axis=-1, keepdims=True)
        on = oc * lax.rsqrt(var + EPS) * gam_ref[...]
        gv = g_ref[...]
        og_ref[...] = (gv * _sigmoid(gv) * on).astype(og_ref.dtype)

    chunk_of = lambda s: s
    out_v = pl.BlockSpec((CHUNK, RET_V_DIM), lambda h, s: (s, h))
    return pl.pallas_call(
        body,
        out_shape=[
            jax.ShapeDtypeStruct((p, RET_V_W), BF16),
            jax.ShapeDtypeStruct((p, RET_V_W), F32),
            jax.ShapeDtypeStruct((RET_HEADS, n_chunks, RET_QK_DIM, RET_V_DIM), F32),
        ],
        grid=(RET_HEADS, n_chunks),
        in_specs=_ret_in_specs(chunk_of),
        out_specs=[out_v, out_v, pl.BlockSpec((None, None, RET_QK_DIM, RET_V_DIM), lambda h, s: (h, s, 0, 0))],
        scratch_shapes=[pltpu.VMEM((RET_QK_DIM, RET_V_DIM), F32)],
        compiler_params=_params("parallel", "arbitrary"),
        name=name,
    )(proj, proj, proj, proj, *tables, gn_g.reshape(RET_HEADS, 1, RET_V_DIM))


def _retention_bwd(proj, gn_g, tables, opre, sprev, dog, *, name):
    p = proj.shape[0]
    n_chunks = p // CHUNK
    scale = RET_QK_DIM ** -0.5

    def body(q_ref, k_ref, v_ref, g_ref, cos_ref, sin_ref, inner_ref, kdec_ref, qdec_ref, cdec_ref, gam_ref,
             opre_ref, sprev_ref, dog_ref, dq_ref, dk_ref, dv_ref, dg_ref, dgam_ref, ds_scr):
        first = pl.program_id(1) == 0

        @pl.when(first)
        def _():
            ds_scr[...] = jnp.zeros_like(ds_scr)

        cosf, sins = cos_ref[...], sin_ref[...]
        qr = _rot(q_ref[...], cosf, sins)
        kr = _rot(k_ref[...], cosf, sins) * scale
        qb, kb = qr.astype(BF16), kr.astype(BF16)
        vb = v_ref[...].astype(BF16)
        inner = inner_ref[...]
        qdec, kdec = qdec_ref[...], kdec_ref[...]
        state_b = sprev_ref[...].astype(BF16)
        o = opre_ref[...]
        mu = jnp.mean(o, axis=-1, keepdims=True)
        oc = o - mu
        rstd = lax.rsqrt(jnp.mean(oc * oc, axis=-1, keepdims=True) + EPS)
        xhat = oc * rstd
        gam = gam_ref[...]
        on = xhat * gam
        gv = g_ref[...]
        sig = _sigmoid(gv)
        dogv = dog_ref[...]
        dg_ref[...] = (dogv * on * sig * (1.0 + gv * (1.0 - sig))).astype(dg_ref.dtype)
        don = dogv * gv * sig
        dgam_part = jnp.sum(don * xhat, axis=0, keepdims=True)

        @pl.when(first)
        def _():
            dgam_ref[...] = dgam_part

        @pl.when(jnp.logical_not(first))
        def _():
            dgam_ref[...] += dgam_part

        dxhat = don * gam
        do = rstd * (dxhat - jnp.mean(dxhat, axis=-1, keepdims=True)
                     - xhat * jnp.mean(dxhat * xhat, axis=-1, keepdims=True))
        dob = do.astype(BF16)
        scores_b = (_dot_nt(qb, kb) * inner).astype(BF16)
        da = (_dot_nt(dob, vb) * inner).astype(BF16)
        dv = _dot(scores_b.astype(F32).T.astype(BF16), dob)
        dqr = _dot(da, kb)
        dkr = _dot(da.astype(F32).T.astype(BF16), qb)
        dqr += _dot_nt(dob, state_b) * qdec
        ds_local = _dot((qr * qdec).T.astype(BF16), dob)
        gstate = ds_scr[...]
        gb = gstate.astype(BF16)
        kd_b = (kr * kdec).astype(BF16)
        dkr += _dot_nt(vb, gb) * kdec
        dv += _dot(kd_b, gb)
        ds_scr[...] = cdec_ref[...] * gstate + ds_local
        dq_ref[...] = _rot_bwd(dqr, cosf, sins).astype(dq_ref.dtype)
        dk_ref[...] = _rot_bwd(dkr * scale, cosf, sins).astype(dk_ref.dtype)
        dv_ref[...] = dv.astype(dv_ref.dtype)

    chunk_of = lambda s: n_chunks - 1 - s
    blk_v = pl.BlockSpec((CHUNK, RET_V_DIM), lambda h, s: (chunk_of(s), h))
    blk_qk = pl.BlockSpec((CHUNK, RET_QK_DIM), lambda h, s: (chunk_of(s), h))
    st_spec = pl.BlockSpec((None, None, RET_QK_DIM, RET_V_DIM), lambda h, s: (h, chunk_of(s), 0, 0))
    return pl.pallas_call(
        body,
        out_shape=[
            jax.ShapeDtypeStruct((p, RET_QK_W), BF16),
            jax.ShapeDtypeStruct((p, RET_QK_W), BF16),
            jax.ShapeDtypeStruct((p, RET_V_W), BF16),
            jax.ShapeDtypeStruct((p, RET_V_W), BF16),
            jax.ShapeDtypeStruct((RET_HEADS, 1, RET_V_DIM), F32),
        ],
        grid=(RET_HEADS, n_chunks),
        in_specs=_ret_in_specs(chunk_of) + [blk_v, st_spec, blk_v],
        out_specs=[blk_qk, blk_qk, blk_v, blk_v, pl.BlockSpec((None, 1, RET_V_DIM), lambda h, s: (h, 0, 0))],
        scratch_shapes=[pltpu.VMEM((RET_QK_DIM, RET_V_DIM), F32)],
        compiler_params=_params("parallel", "arbitrary"),
        name=name,
    )(proj, proj, proj, proj, *tables, gn_g.reshape(RET_HEADS, 1, RET_V_DIM), opre, sprev, dog)


def _conv_rows(p):
    return _pick(p, (384, 128))


def _ln_stats(y):
    mu = jnp.mean(y, axis=-1, keepdims=True)
    yc = y - mu
    rstd = lax.rsqrt(jnp.mean(yc * yc, axis=-1, keepdims=True) + EPS)
    return yc * rstd, rstd


def _conv_fwd(proj, conv_w, conv_b, ln_g, ln_b, *, name):
    p = proj.shape[0]
    c = D_MODEL
    rows = _conv_rows(p)
    hpb = rows // CONV_HALO
    a_col, gate_col = (2 * RET_QK_W + 2 * RET_V_W) // c, (2 * RET_QK_W + 2 * RET_V_W) // c + 1

    def body(a_ref, gate_ref, ah_ref, gateh_ref, w_ref, b_ref, lg_ref, lb_ref, c_ref, y_ref, hdn_scr):
        i = pl.program_id(0)
        hdn_scr[0:CONV_HALO, :] = ah_ref[...] * _sigmoid(gateh_ref[...])
        hdn_scr[CONV_HALO:, :] = a_ref[...] * _sigmoid(gate_ref[...])
        acc = jnp.zeros((rows, c), F32)
        for w in range(CONV_WIDTH):
            off = CONV_HALO - (CONV_WIDTH - 1) + w
            acc += hdn_scr[off:off + rows, :] * w_ref[w:w + 1, :]
        y = acc + b_ref[...]
        y_ref[...] = y
        yhat, _ = _ln_stats(y)
        ln = yhat * lg_ref[...] + lb_ref[...]
        row = i * rows + lax.broadcasted_iota(jnp.int32, (rows, 1), 0)
        c_ref[...] = jnp.where(row >= PAD_FRONT, ln * _sigmoid(ln), 0.0).astype(c_ref.dtype)

    halo_idx = lambda i: jnp.maximum(i * hpb - 1, 0)
    vec = pl.BlockSpec((1, c), lambda i: (0, 0))
    return pl.pallas_call(
        body,
        out_shape=[jax.ShapeDtypeStruct((p, c), BF16), jax.ShapeDtypeStruct((p, c), F32)],
        grid=(p // rows,),
        in_specs=[
            pl.BlockSpec((rows, c), lambda i: (i, a_col)),
            pl.BlockSpec((rows, c), lambda i: (i, gate_col)),
            pl.BlockSpec((CONV_HALO, c), lambda i: (halo_idx(i), a_col)),
            pl.BlockSpec((CONV_HALO, c), lambda i: (halo_idx(i), gate_col)),
            pl.BlockSpec((CONV_WIDTH, c), lambda i: (0, 0)),
            vec, vec, vec,
        ],
        out_specs=[pl.BlockSpec((rows, c), lambda i: (i, 0)), pl.BlockSpec((rows, c), lambda i: (i, 0))],
        scratch_shapes=[pltpu.VMEM((CONV_HALO + rows, c), F32)],
        compiler_params=_params("parallel"),
        name=name,
    )(proj, proj, proj, proj, conv_w, conv_b, ln_g, ln_b)


def _conv_bwd(proj, conv_w, ln_g, ln_b, y, dcat, *, name):
    p = proj.shape[0]
    c = D_MODEL
    rows = _conv_rows(p)
    hpb = rows // CONV_HALO
    n_blocks = p // rows
    a_col, gate_col = (2 * RET_QK_W + 2 * RET_V_W) // c, (2 * RET_QK_W + 2 * RET_V_W) // c + 1

    def body(a_ref, gate_ref, ah_ref, gateh_ref, w_ref, lg_ref, lb_ref, y_ref, yh_ref, dc_ref, dch_ref,
             da_ref, dgate_ref, dw_ref, db_ref, dlg_ref, dlb_ref, hdn_scr, dy_scr):
        i = pl.program_id(0)
        lg, lb = lg_ref[...], lb_ref[...]

        def ln_bwd(yv, dcv):
            yhat, rstd = _ln_stats(yv)
            ln = yhat * lg + lb
            sig = _sigmoid(ln)
            dln = dcv * sig * (1.0 + ln * (1.0 - sig))
            dyhat = dln * lg
            dyv = rstd * (dyhat - jnp.mean(dyhat, axis=-1, keepdims=True)
                          - yhat * jnp.mean(dyhat * yhat, axis=-1, keepdims=True))
            return dyv, dln, yhat

        row = i * rows + lax.broadcasted_iota(jnp.int32, (rows, 1), 0)
        dy, dln, yhat = ln_bwd(y_ref[...], jnp.where(row >= PAD_FRONT, dc_ref[...], 0.0))
        dy_halo, _, _ = ln_bwd(yh_ref[...], dch_ref[...])
        dy_scr[0:rows, :] = dy
        dy_scr[rows:, :] = jnp.where(i == n_blocks - 1, 0.0, dy_halo)
        sig_gate = _sigmoid(gate_ref[...])
        av = a_ref[...]
        hdn_scr[0:CONV_HALO, :] = ah_ref[...] * _sigmoid(gateh_ref[...])
        hdn_scr[CONV_HALO:, :] = av * sig_gate
        @pl.when(i == 0)
        def _():
            dw_ref[...] = jnp.zeros_like(dw_ref)
            db_ref[...] = jnp.zeros_like(db_ref)
            dlg_ref[...] = jnp.zeros_like(dlg_ref)
            dlb_ref[...] = jnp.zeros_like(dlb_ref)

        dhdn = jnp.zeros((rows, c), F32)
        for w in range(CONV_WIDTH):
            back = CONV_WIDTH - 1 - w
            dhdn += dy_scr[back:back + rows, :] * w_ref[w:w + 1, :]
            off = CONV_HALO - (CONV_WIDTH - 1) + w
            dw_ref[w:w + 1, :] += jnp.sum(dy * hdn_scr[off:off + rows, :], axis=0, keepdims=True)
        da_ref[...] = (dhdn * sig_gate).astype(da_ref.dtype)
        dgate_ref[...] = (dhdn * av * sig_gate * (1.0 - sig_gate)).astype(dgate_ref.dtype)
        db_ref[...] += jnp.sum(dy, axis=0, keepdims=True)
        dlg_ref[...] += jnp.sum(dln * yhat, axis=0, keepdims=True)
        dlb_ref[...] += jnp.sum(dln, axis=0, keepdims=True)

    prev_halo = lambda i: jnp.maximum(i * hpb - 1, 0)
    next_halo = lambda i: jnp.minimum((i + 1) * hpb, p // CONV_HALO - 1)
    vec = pl.BlockSpec((1, c), lambda i: (0, 0))
    blk = lambda col: pl.BlockSpec((rows, c), lambda i: (i, col))
    outs = pl.pallas_call(
        body,
        out_shape=[
            jax.ShapeDtypeStruct((p, c), BF16),
            jax.ShapeDtypeStruct((p, c), BF16),
            jax.ShapeDtypeStruct((CONV_WIDTH + 1, c), F32),
            jax.ShapeDtypeStruct((1, c), F32),
            jax.ShapeDtypeStruct((1, c), F32),
            jax.ShapeDtypeStruct((1, c), F32),
        ],
        grid=(n_blocks,),
        in_specs=[
            blk(a_col), blk(gate_col),
            pl.BlockSpec((CONV_HALO, c), lambda i: (prev_halo(i), a_col)),
            pl.BlockSpec((CONV_HALO, c), lambda i: (prev_halo(i), gate_col)),
            pl.BlockSpec((CONV_WIDTH, c), lambda i: (0, 0)),
            vec, vec,
            blk(0),
            pl.BlockSpec((CONV_HALO, c), lambda i: (next_halo(i), 0)),
            blk(1),
            pl.BlockSpec((CONV_HALO, c), lambda i: (next_halo(i), 1)),
        ],
        out_specs=[blk(0), blk(0), pl.BlockSpec((CONV_WIDTH + 1, c), lambda i: (0, 0)), vec, vec, vec],
        scratch_shapes=[pltpu.VMEM((CONV_HALO + rows, c), F32), pltpu.VMEM((rows + CONV_HALO, c), F32)],
        compiler_params=_params("arbitrary"),
        name=name,
    )(proj, proj, proj, proj, conv_w, ln_g, ln_b, y, y, dcat, dcat)
    da, dgate, dw, db, dlg, dlb = outs
    return da, dgate, dw[:CONV_WIDTH], db, dlg, dlb


def _group_matrix():
    r = jnp.arange(D_MODEL)[:, None] // SB_HEAD_DIM
    c = jnp.arange(D_MODEL)[None, :] // SB_HEAD_DIM
    return (r == c).astype(BF16)


def _qknorm_fwd(qkv, qg, kg, *, name):
    p = qkv.shape[0]
    d = D_MODEL
    rows = _pick(p, (384, 128, 96))

    def body(q_ref, k_ref, v_ref, qg_ref, kg_ref, gm_ref, qn_ref, kn_ref, vb_ref):
        gm = gm_ref[...]

        def norm(x, g):
            ms = _split_dot(x * x, gm) * (1.0 / SB_HEAD_DIM)
            return x * lax.rsqrt(ms + EPS) * g

        qn_ref[...] = norm(q_ref[...], qg_ref[...]).astype(BF16)
        kn_ref[...] = norm(k_ref[...], kg_ref[...]).astype(BF16)
        vb_ref[...] = v_ref[...].astype(BF16)

    blk = lambda col: pl.BlockSpec((rows, d), lambda i: (i, col))
    vec = pl.BlockSpec((1, d), lambda i: (0, 0))
    return pl.pallas_call(
        body,
        out_shape=[jax.ShapeDtypeStruct((p, d), BF16)] * 3,
        grid=(p // rows,),
        in_specs=[blk(0), blk(1), blk(2), vec, vec, pl.BlockSpec((d, d), lambda i: (0, 0))],
        out_specs=[blk(0)] * 3,
        compiler_params=_params("parallel"),
        name=name,
    )(qkv, qkv, qkv, qg, kg, _group_matrix())


def _qknorm_bwd(qkv, qg, kg, dqn, dkn, dv, *, name):
    p = qkv.shape[0]
    d = D_MODEL
    rows = _pick(p, (384, 128, 96))

    def body(q_ref, k_ref, qg_ref, kg_ref, gm_ref, dqn_ref, dkn_ref, dv_ref, dqkv_ref, dqg_ref, dkg_ref):
        gm = gm_ref[...]

        def bwd(x, g, dy):
            ms = _split_dot(x * x, gm) * (1.0 / SB_HEAD_DIM)
            r = lax.rsqrt(ms + EPS)
            gdy = dy * g
            proj = _split_dot(x * gdy, gm) * (1.0 / SB_HEAD_DIM)
            return r * gdy - x * (r * r * r) * proj, jnp.sum(dy * x * r, axis=0, keepdims=True)

        dq, dqg = bwd(q_ref[...], qg_ref[...], dqn_ref[...])
        dk, dkg = bwd(k_ref[...], kg_ref[...], dkn_ref[...])
        dqkv_ref[:, 0:d] = dq.astype(BF16)
        dqkv_ref[:, d:2 * d] = dk.astype(BF16)
        dqkv_ref[:, 2 * d:3 * d] = dv_ref[...].astype(BF16)

        @pl.when(pl.program_id(0) == 0)
        def _():
            dqg_ref[...] = dqg
            dkg_ref[...] = dkg

        @pl.when(pl.program_id(0) > 0)
        def _():
            dqg_ref[...] += dqg
            dkg_ref[...] += dkg

    blk = lambda col: pl.BlockSpec((rows, d), lambda i: (i, col))
    vec = pl.BlockSpec((1, d), lambda i: (0, 0))
    return pl.pallas_call(
        body,
        out_shape=[jax.ShapeDtypeStruct((p, 3 * d), BF16), jax.ShapeDtypeStruct((1, d), F32),
                   jax.ShapeDtypeStruct((1, d), F32)],
        grid=(p // rows,),
        in_specs=[blk(0), blk(1), vec, vec, pl.BlockSpec((d, d), lambda i: (0, 0)), blk(0), blk(0), blk(0)],
        out_specs=[pl.BlockSpec((rows, 3 * d), lambda i: (i, 0)), vec, vec],
        compiler_params=_params("arbitrary"),
        name=name,
    )(qkv, qkv, qg, kg, _group_matrix(), dqn, dkn, dv)


SB_PAIR = 2 * SB_HEAD_DIM
SB_GROUP = 8
SB_MASKED = -1e30


def _sb_consts():
    lane = lax.broadcasted_iota(jnp.int32, (CHUNK, SB_PAIR), 1)
    r = lax.broadcasted_iota(jnp.int32, (CHUNK, CHUNK), 0)
    c = lax.broadcasted_iota(jnp.int32, (CHUNK, CHUNK), 1)
    lo = (lane < SB_HEAD_DIM).astype(F32).astype(BF16)
    ones = jnp.ones((CHUNK, CHUNK), BF16)
    twice = lambda m: jnp.concatenate([jnp.concatenate([m, ones], axis=1)] * 2, axis=0)
    later, earlier = twice((r > c).astype(BF16)), twice((r < c).astype(BF16))
    not_before = (c >= r).astype(F32) * SB_MASKED
    padding = (c < PAD_FRONT).astype(F32) * SB_MASKED
    return (lo, 1.0 - lo), c, later, earlier, not_before, padding


def _sb_halves(t, head_lanes):
    return t * head_lanes[0], t * head_lanes[1]


def _sb_logits(qh, kg, biases):
    z = _dot_nt(qh, kg)
    tiles = []
    for b, bias in enumerate(biases):
        zt = z[:, b * CHUNK:(b + 1) * CHUNK]
        if bias is not None:
            zt = zt + bias
        ls_pos = jnp.minimum(zt, 0.0) - jnp.log(1.0 + jnp.exp(-jnp.abs(zt)))
        tiles.append((ls_pos, ls_pos - zt))
    return tiles


def _sb_block_sums(tiles, m):
    st = jnp.concatenate(tiles, axis=0)
    hi = st.astype(BF16)
    lo = (st - hi.astype(F32)).astype(BF16)
    tot = _dot(jnp.concatenate([hi, lo], axis=1), m)
    return [(tot[i * CHUNK:(i + 1) * CHUNK, 0:CHUNK], tot[i * CHUNK:(i + 1) * CHUNK, CHUNK:2 * CHUNK])
            for i in range(len(tiles))]


def _sb_plan(qi, padding, not_before):
    top = lax.div(qi, SB_GROUP)
    size = qi - SB_GROUP * top + 1
    pad_if_first = padding * (top == 0).astype(F32)
    masks = []
    for n_b in range(1, SB_GROUP + 1):
        m = [None] * n_b
        m[n_b - 1] = not_before
        m[0] = pad_if_first if m[0] is None else m[0] + pad_if_first
        masks.append(m)
    return top, size, masks


def _once_if(cond, fn, carry):
    return lax.fori_loop(0, jnp.where(cond, 1, 0), lambda s, cr: fn(cr), carry)


def _sb_head_rows(tg, lanes, n_b):
    return jnp.concatenate([tg[b * CHUNK:(b + 1) * CHUNK] * lanes for b in range(n_b)], axis=0)


def _sb_fwd(qn, kn, vb, *, name):
    p = qn.shape[0]
    n_blocks = p // CHUNK
    n_pairs = SB_HEADS // 2
    scale = SB_HEAD_DIM ** -0.5

    def body(q_ref, k_ref, v_ref, o_ref, car_ref):
        head_lanes, c, later, _, not_before, padding = _sb_consts()

        def q_block(qi, _):
            rows = pl.ds(pl.multiple_of(qi * CHUNK, CHUNK), CHUNK)
            qh = _sb_halves(q_ref[rows, :], head_lanes)
            qs = (qh[0] * scale, qh[1] * scale)

            def blocks(kb0, biases, carry):
                n_b = len(biases)
                acc, run0, run1, sav0, sav1 = carry
                krows = pl.ds(pl.multiple_of(kb0 * CHUNK, CHUNK), n_b * CHUNK)
                kg, vg = k_ref[krows, :], v_ref[krows, :]
                tiles = [_sb_logits(qs[h], kg, biases) for h in range(2)]
                sums = [_sb_block_sums([log_keep for _, log_keep in tiles[h]], later) for h in range(2)]
                cols = [(c == kb0 + b).astype(F32) for b in range(n_b)]
                runs, savs = [run0, run1], [sav0, sav1]
                for h in range(2):
                    ws = [None] * n_b
                    for b in reversed(range(n_b)):
                        after, row_sum = sums[h][b]
                        ws[b] = jnp.exp(tiles[h][b][0] + after + runs[h]).astype(BF16)
                        savs[h] = savs[h] + cols[b] * runs[h]
                        runs[h] = runs[h] + row_sum
                    acc = acc + _dot(jnp.concatenate(ws, axis=1), _sb_head_rows(vg, head_lanes[h], n_b))
                return acc, runs[0], runs[1], savs[0], savs[1]

            zt = qh[0].astype(F32) * 0.0
            top, size, masks = _sb_plan(qi, padding, not_before)
            carry = (zt, zt, zt, zt, zt)
            for m in masks:
                carry = _once_if(size == len(m), functools.partial(blocks, SB_GROUP * top, m), carry)
            carry = lax.fori_loop(0, jnp.maximum(top - 1, 0),
                                  lambda it, cr: blocks(SB_GROUP * (top - 1 - it), [None] * SB_GROUP, cr), carry)
            carry = _once_if(top > 0, functools.partial(blocks, 0, [padding] + [None] * (SB_GROUP - 1)), carry)
            acc, _, _, sav0, sav1 = carry
            o_ref[rows, :] = acc.astype(o_ref.dtype)
            car_ref[rows, 0:CHUNK] = sav0
            car_ref[rows, CHUNK:2 * CHUNK] = sav1
            return 0

        lax.fori_loop(0, n_blocks, q_block, 0)

    col = pl.BlockSpec((p, SB_PAIR), lambda g: (0, g))
    return pl.pallas_call(
        body,
        out_shape=[jax.ShapeDtypeStruct((p, D_MODEL), BF16), jax.ShapeDtypeStruct((p, n_pairs * 2 * CHUNK), F32)],
        grid=(n_pairs,),
        in_specs=[col, col, col],
        out_specs=[col, pl.BlockSpec((p, 2 * CHUNK), lambda g: (0, g))],
        compiler_params=_params("parallel"),
        name=name,
    )(qn, kn, vb)


def _sb_bwd(qn, kn, vb, carries, do, *, name):
    p = qn.shape[0]
    n_blocks = p // CHUNK
    n_pairs = SB_HEADS // 2
    scale = SB_HEAD_DIM ** -0.5

    def body(q_ref, k_ref, v_ref, car_ref, do_ref, dq_ref, dk_ref, dv_ref):
        head_lanes, c, later, earlier, not_before, padding = _sb_consts()
        dk_ref[...] = jnp.zeros_like(dk_ref)
        dv_ref[...] = jnp.zeros_like(dv_ref)

        def q_block(qi, _):
            rows = pl.ds(pl.multiple_of(qi * CHUNK, CHUNK), CHUNK)
            qh = _sb_halves(q_ref[rows, :], head_lanes)
            qs = (qh[0] * scale, qh[1] * scale)
            doh = _sb_halves(do_ref[rows, :].astype(BF16), head_lanes)
            sav = (car_ref[rows, 0:CHUNK], car_ref[rows, CHUNK:2 * CHUNK])

            def blocks(kb0, biases, carry):
                n_b = len(biases)
                dq_acc, pre0, pre1 = carry
                krows = pl.ds(pl.multiple_of(kb0 * CHUNK, CHUNK), n_b * CHUNK)
                kg, vg = k_ref[krows, :], v_ref[krows, :]
                cols = [(c == kb0 + b).astype(F32) for b in range(n_b)]
                block = lambda t, b: t[:, b * CHUNK:(b + 1) * CHUNK]
                tiles = [_sb_logits(qs[h], kg, biases) for h in range(2)]
                afters = [_sb_block_sums([log_keep for _, log_keep in tiles[h]], later) for h in range(2)]
                dws = [_dot_nt(doh[h], vg) for h in range(2)]
                ws, es, befores = [], [], []
                for h in range(2):
                    runs = [jnp.sum(cols[b] * sav[h], axis=-1, keepdims=True) for b in range(n_b)]
                    ws.append([jnp.exp(tiles[h][b][0] + afters[h][b][0] + runs[b]) for b in range(n_b)])
                    es.append([ws[h][b] * block(dws[h], b) for b in range(n_b)])
                    befores.append(_sb_block_sums(es[h], earlier))
                pres = [pre0, pre1]
                dk_add = dv_add = None
                for h in range(2):
                    dzs = []
                    for b in range(n_b):
                        before, row_sum = befores[h][b]
                        sig = jnp.exp(tiles[h][b][0])
                        e = es[h][b]
                        dzs.append((e - (e + before + pres[h]) * sig).astype(BF16))
                        pres[h] = pres[h] + row_sum
                    dz = jnp.concatenate(dzs, axis=1)
                    w = jnp.concatenate([t.astype(BF16) for t in ws[h]], axis=1)
                    dq_acc = dq_acc + _dot(dz, _sb_head_rows(kg, head_lanes[h], n_b))
                    dv_h, dk_h = _dot_tn(w, doh[h]), _dot_tn(dz, qs[h])
                    dv_add = dv_h if dv_add is None else dv_add + dv_h
                    dk_add = dk_h if dk_add is None else dk_add + dk_h
                dv_ref[krows, :] += dv_add
                dk_ref[krows, :] += dk_add
                return dq_acc, pres[0], pres[1]

            zt = qh[0].astype(F32) * 0.0
            top, size, masks = _sb_plan(qi, padding, not_before)
            carry = _once_if(top > 0, functools.partial(blocks, 0, [padding] + [None] * (SB_GROUP - 1)), (zt, zt, zt))
            carry = lax.fori_loop(1, top, lambda g, cr: blocks(SB_GROUP * g, [None] * SB_GROUP, cr), carry)
            for m in masks:
                carry = _once_if(size == len(m), functools.partial(blocks, SB_GROUP * top, m), carry)
            dq_acc, _, _ = carry
            dq_ref[rows, :] = dq_acc * scale
            return 0

        lax.fori_loop(0, n_blocks, q_block, 0)

    col = pl.BlockSpec((p, SB_PAIR), lambda g: (0, g))
    return pl.pallas_call(
        body,
        out_shape=[jax.ShapeDtypeStruct((p, D_MODEL), F32)] * 3,
        grid=(n_pairs,),
        in_specs=[col, col, col, pl.BlockSpec((p, 2 * CHUNK), lambda g: (0, g)), col],
        out_specs=[col, col, col],
        compiler_params=_params("parallel"),
        name=name,
    )(qn, kn, vb, carries, do)


def _loss_head(h, target, *, name):
    p, d = h.shape
    n_blocks = p // CHUNK

    def body(h_ref, t_ref, sq_ref, dh_ref):
        i = pl.program_id(0)

        @pl.when(i == 0)
        def _():
            sq_ref[...] = jnp.zeros_like(sq_ref)
            dh_ref[...] = jnp.zeros_like(dh_ref)

        @pl.when(i > 0)
        def _():
            err = h_ref[...] - t_ref[...]
            sq_ref[...] += jnp.sum(err * err)
            dh_ref[...] = err * (1.0 / d)

    return pl.pallas_call(
        body,
        out_shape=[jax.ShapeDtypeStruct((8, 128), F32), jax.ShapeDtypeStruct((p, d), F32)],
        grid=(n_blocks,),
        in_specs=[pl.BlockSpec((CHUNK, d), lambda i: (i, 0)),
                  pl.BlockSpec((CHUNK, d), lambda i: (jnp.maximum(i - 1, 0), 0))],
        out_specs=[pl.BlockSpec((8, 128), lambda i: (0, 0)), pl.BlockSpec((CHUNK, d), lambda i: (i, 0))],
        compiler_params=_params("arbitrary"),
        name=name,
    )(h, target)


def _local_step(x, target, meta, norm_mix_g, norm_mlp_g, w_in, gn_g, conv_w, conv_b, ln_g, ln_b, qn_g, kn_g, later,
                reached=lambda point, after, grads=None: None):
    seq = x.shape[0]
    p = PAD_FRONT + N_META + seq
    d = D_MODEL
    tables = _retention_tables(p)
    row = lambda v: v.reshape(1, -1)
    h0 = jnp.concatenate([jnp.zeros((PAD_FRONT, d), F32), meta, x], axis=0)

    hn0 = _rmsnorm_fwd(h0, row(norm_mix_g[0]), name="l0_mix_norm")
    proj = _matmul(hn0, w_in, mode="nn", out_dtypes=(F32,), name="l0_proj")
    og, opre, sprev = _retention_fwd(proj, gn_g, tables, name="l0_retention")
    cb, y_conv = _conv_fwd(proj, conv_w, row(conv_b), row(ln_g), row(ln_b), name="l0_conv")
    cat = jnp.concatenate([og, cb], axis=1)
    w_out, w1_0, w2_0 = later("l0", cat)
    w1, w2 = [w1_0, None], [w2_0, None]
    h1 = _matmul(cat, w_out, mode="nn", out_dtypes=(F32,), epilogue=_add_epilogue, extras=(h0,), name="l0_mix_out")
    h2, mlp0 = _mlp_fwd(h1, row(norm_mlp_g[0]), w1[0], w2[0], name="l0_mlp")

    hn1 = _rmsnorm_fwd(h2, row(norm_mix_g[1]), name="l1_mix_norm")
    (w_qkv,) = later("qkv", hn1)
    qkv = _matmul(hn1, w_qkv, mode="nn", out_dtypes=(F32,), name="l1_qkv")
    qg_t, kg_t = jnp.tile(row(qn_g), (1, SB_HEADS)), jnp.tile(row(kn_g), (1, SB_HEADS))
    qn, kn, vb = _qknorm_fwd(qkv, qg_t, kg_t, name="l1_qknorm")
    o_sb, carries = _sb_fwd(qn, kn, vb, name="l1_stickbreak")
    w_o, w1[1], w2[1] = later("l1", o_sb)
    h3 = _matmul(o_sb, w_o, mode="nn", out_dtypes=(F32,), epilogue=_add_epilogue, extras=(h2,), name="l1_mix_out")
    h4, mlp1 = _mlp_fwd(h3, row(norm_mlp_g[1]), w1[1], w2[1], name="l1_mlp")

    sq, dh4 = _loss_head(h4, target, name="loss_head")

    dh3, dg_mlp1, dw1_1, dw2_1 = _mlp_bwd(h3, row(norm_mlp_g[1]), w1[1], w2[1], mlp1, dh4, name="l1_mlp_bwd")
    do_sb = _matmul(dh3, w_o, mode="nt", out_dtypes=(F32,), name="l1_do")
    dw_o = _matmul(o_sb, dh3, mode="tn", out_dtypes=(F32,), name="l1_dwo")
    dqn, dkn, dv = _sb_bwd(qn, kn, vb, carries, do_sb, name="l1_stickbreak_bwd")
    dqkv, dqg_t, dkg_t = _qknorm_bwd(qkv, qg_t, kg_t, dqn, dkn, dv, name="l1_qknorm_bwd")
    dw_qkv = _matmul(hn1, dqkv, mode="tn", out_dtypes=(F32,), name="l1_dwqkv")
    pin = lambda arr, tok: arr if tok is None else arr + tok[0:1, 0:1]
    tok = reached("l1_grads", dw_qkv, dict(odd_w_qkv=dw_qkv, odd_w_o=dw_o, mlp_w1_1=dw1_1, mlp_w2_1=dw2_1))
    dhn1 = _matmul(dqkv, w_qkv, mode="nt", out_dtypes=(F32,), name="l1_dhn", after=tok)
    dh2, dg_mix1 = _rmsnorm_bwd(h2, row(norm_mix_g[1]), dhn1, dh3, name="l1_mix_dnorm")
    tok = reached("l1_done", dh2)

    dh1, dg_mlp0, dw1_0, dw2_0 = _mlp_bwd(h1, row(norm_mlp_g[0]), w1[0], w2[0], mlp0, dh2, name="l0_mlp_bwd", after=tok)
    tok = reached("l0_mlp_grads", dh1, dict(mlp_w1_0=dw1_0, mlp_w2_0=dw2_0))
    dcat = _matmul(dh1, w_out, mode="nt", out_dtypes=(F32,), name="l0_dcat", after=tok)
    dw_out = _matmul(cat, dh1, mode="tn", out_dtypes=(F32,), name="l0_dwout")
    tok = reached("l0_dwout", dw_out)
    dq, dk, dvr, dgate_r, dgn = _retention_bwd(proj, pin(gn_g, tok), tables, opre, sprev, dcat, name="l0_retention_bwd")
    tok = reached("l0_retention_bwd", dq)
    da, dgate_c, dconv_w, dconv_b, dln_g, dln_b = _conv_bwd(proj, conv_w, pin(row(ln_g), tok), row(ln_b), y_conv, dcat,
                                                            name="l0_conv_bwd")
    tok = reached("l0_conv_bwd", da)
    dproj = jnp.concatenate([dq, dk, dvr, dgate_r, da, dgate_c], axis=1)
    dw_in = _matmul(hn0, dproj, mode="tn", out_dtypes=(F32,), name="l0_dwin", after=tok)
    dhn0 = _matmul(dproj, w_in, mode="nt", out_dtypes=(F32,), name="l0_dhn")
    dh0, dg_mix0 = _rmsnorm_bwd(h0, row(norm_mix_g[0]), dhn0, dh1, name="l0_mix_dnorm")

    fold = lambda t: t.reshape(SB_HEADS, SB_HEAD_DIM).sum(axis=0)
    grads = dict(
        x=dh0[PAD_FRONT + N_META:],
        meta=dh0[PAD_FRONT:PAD_FRONT + N_META],
        norm_mix_g=jnp.concatenate([dg_mix0, dg_mix1], axis=0),
        norm_mlp_g=jnp.concatenate([dg_mlp0, dg_mlp1], axis=0),
        even_w_in=dw_in,
        even_ret_gn_g=dgn.reshape(RET_HEADS, RET_V_DIM),
        even_conv_w=dconv_w,
        even_conv_b=dconv_b,
        even_conv_ln_g=dln_g,
        even_conv_ln_b=dln_b,
        even_w_out=dw_out,
        odd_w_qkv=dw_qkv,
        odd_q_norm_g=fold(dqg_t)[None],
        odd_k_norm_g=fold(dkg_t)[None],
        odd_w_o=dw_o,
        mlp_w1=(dw1_0, dw1_1),
        mlp_w2=(dw2_0, dw2_1),
    )
    return sq[0, 0], grads


def _position():
    x, y, c = lax.axis_index("x"), lax.axis_index("y"), lax.axis_index("c")
    other_chips = [(1 - x, y), (x, 1 - y), (1 - x, 1 - y)]
    return x, y, c, other_chips


def _shard_of(ref, kind, s, n):
    rows, cols = ref.shape
    if kind == "col":
        return ref.at[:, pl.ds(s * (cols // n), cols // n)]
    return ref.at[pl.ds(s * (rows // n), rows // n), :]


def _half_of(ref, kind, c):
    rows, cols = ref.shape
    if kind == "col":
        return ref.at[pl.ds(c * (rows // 2), rows // 2), :]
    return ref.at[:, pl.ds(c * (cols // 2), cols // 2)]


def _remote(src, dst, send_sems, recv_sems, idx, device):
    return pltpu.make_async_remote_copy(src_ref=src, dst_ref=dst, send_sem=send_sems.at[idx], recv_sem=recv_sems.at[idx],
                                        device_id=device, device_id_type=MESH)


def _cast_into_whole(w, kind, s_arr, *, name):
    rows, cols = w.shape
    tr = _pick(rows, (256, 128))
    nb = rows // tr
    if kind == "col":
        whole, o_spec = (rows, cols * N_CHIPS), pl.BlockSpec((tr, cols), lambda i, s_ref: (i, s_ref[0]))
    else:
        whole, o_spec = (rows * N_CHIPS, cols), pl.BlockSpec((tr, cols), lambda i, s_ref: (s_ref[0] * nb + i, 0))

    def body(s_ref, w_ref, o_ref):
        o_ref[...] = w_ref[...].astype(BF16)

    return pl.pallas_call(
        body,
        out_shape=jax.ShapeDtypeStruct(whole, BF16),
        grid_spec=pltpu.PrefetchScalarGridSpec(num_scalar_prefetch=1, grid=(nb,),
                                               in_specs=[pl.BlockSpec((tr, cols), lambda i, s_ref: (i, 0))],
                                               out_specs=o_spec),
        compiler_params=_params("parallel"),
        name=name,
    )(s_arr, w)


def _allgather_weights(wholes, kinds):
    n = len(wholes)

    def body(*refs):
        ins, outs = refs[:n], refs[n:2 * n]
        send_sems, recv_sems = refs[2 * n:]
        x, y, c, chips = _position()
        me_chip = 2 * x + y
        sibling = (x, y, 1 - c)
        sends = []
        for t in range(n):
            for k, (cx, cy) in enumerate(chips):
                src = _half_of(_shard_of(ins[t], kinds[t], me_chip, N_CHIPS), kinds[t], c)
                dst = _half_of(_shard_of(outs[t], kinds[t], me_chip, N_CHIPS), kinds[t], c)
                sends.append(_remote(src, dst, send_sems, recv_sems, 6 * t + k, (cx, cy, c)))
        for cp in sends:
            cp.start()
        passed = []
        for t in range(n):
            for k, (cx, cy) in enumerate(chips):
                landed = _half_of(_shard_of(outs[t], kinds[t], 2 * cx + cy, N_CHIPS), kinds[t], c)
                _remote(landed, landed, send_sems, recv_sems, 6 * t + k, (cx, cy, c)).wait_recv()
                fwd = _remote(landed, landed, send_sems, recv_sems, 6 * t + 3 + k, sibling)
                fwd.start()
                passed.append(fwd)
        for t in range(n):
            for k, (cx, cy) in enumerate(chips):
                theirs = _half_of(_shard_of(outs[t], kinds[t], 2 * cx + cy, N_CHIPS), kinds[t], 1 - c)
                _remote(theirs, theirs, send_sems, recv_sems, 6 * t + 3 + k, sibling).wait_recv()
        for cp in sends + passed:
            cp.wait_send()

    return pl.pallas_call(
        body,
        out_shape=[jax.ShapeDtypeStruct(w.shape, BF16) for w in wholes],
        in_specs=[ANY] * n,
        out_specs=[ANY] * n,
        input_output_aliases={t: t for t in range(n)},
        scratch_shapes=[pltpu.SemaphoreType.DMA((6 * n,)), pltpu.SemaphoreType.DMA((6 * n,))],
        name="allgather_weights",
    )(*wholes)


HBM = pl.BlockSpec(memory_space=pltpu.HBM)
SEM = pl.BlockSpec(memory_space=pltpu.SEMAPHORE)
DATAFLOW = pltpu.SideEffectType.DATAFLOW_SIDE_EFFECTING
TARGETS = 6


def _gather_copies(kinds, refs, _, send_sems, recv_sems):
    x, y, c, chips = _position()
    me_chip = 2 * x + y
    sends, lands = [], []
    for t, (ref, kind) in enumerate(zip(refs, kinds)):
        mine = _half_of(_shard_of(ref, kind, me_chip, N_CHIPS), kind, c)
        for k, (cx, cy) in enumerate(chips):
            for other_core in range(2):
                j = TARGETS * t + 2 * k + other_core
                peer_c = 1 - c if other_core else c
                sends.append(_remote(mine, mine, send_sems, recv_sems, j, (cx, cy, peer_c)))
                theirs = _half_of(_shard_of(ref, kind, 2 * cx + cy, N_CHIPS), kind, peer_c)
                lands.append(_remote(theirs, theirs, send_sems, recv_sems, j, (cx, cy, peer_c)))
    return sends, lands


def _pair_swap_copies(kinds, srcs, lands, send_sems, recv_sems):
    x, y, c, _ = _position()
    sibling = (x, y, 1 - c)
    sends = [_remote(_half_of(srcs[t], kinds[t], 1 - c), lands[t], send_sems, recv_sems, t, sibling) for t in range(len(srcs))]
    arrivals = [_remote(_half_of(srcs[t], kinds[t], c), lands[t], send_sems, recv_sems, t, sibling) for t in range(len(srcs))]
    return sends, arrivals


def _chip_exchange_copies(kinds, srcs, lands, send_sems, recv_sems):
    x, y, c, chips = _position()
    sends, arrivals = [], []
    for t in range(len(srcs)):
        for k, (cx, cy) in enumerate(chips):
            src = _shard_of(srcs[t], kinds[t], 2 * cx + cy, N_CHIPS)
            sends.append(_remote(src, lands[t].at[k], send_sems, recv_sems, 3 * t + k, (cx, cy, c)))
            arrivals.append(_remote(src, lands[t].at[k], send_sems, recv_sems, 3 * t + k, (cx, cy, c)))
    return sends, arrivals


def _pair_gather_copies(kinds, srcs, lands, send_sems, recv_sems):
    x, y, c, _ = _position()
    sibling = (x, y, 1 - c)
    sends, arrivals = [], []
    for t in range(len(srcs)):
        mine, theirs = _half_of(srcs[t], kinds[t], c), _half_of(srcs[t], kinds[t], 1 - c)
        sends.append(_remote(mine, mine, send_sems, recv_sems, t, sibling))
        arrivals.append(_remote(theirs, theirs, send_sems, recv_sems, t, sibling))
    return sends, arrivals


def _copies_start(plan, n_sems, srcs, lands, follows, *, name):
    ns, n = len(srcs), len(srcs) + len(lands)

    def body(*refs):
        send_sems, recv_sems = refs[n + 1], refs[n + 2]
        thru, token = refs[n + 3:2 * n + 3], refs[2 * n + 3]
        sends, _ = plan(thru[:ns], thru[ns:], send_sems, recv_sems)
        for cp in sends:
            cp.start()
        token[...] = jnp.zeros_like(token)

    arrays = [pltpu.with_memory_space_constraint(a, pltpu.HBM) for a in list(srcs) + list(lands)]
    outs = pl.pallas_call(
        body,
        name=name,
        out_shape=(pltpu.SemaphoreType.DMA((n_sems,)), pltpu.SemaphoreType.DMA((n_sems,)),
                   *[pltpu.HBM(a.shape, a.dtype) for a in arrays], jax.ShapeDtypeStruct((8, 128), F32)),
        in_specs=(*[HBM] * n, ANY),
        out_specs=(SEM, SEM, *[HBM] * n, pl.BlockSpec(memory_space=pltpu.VMEM)),
        input_output_aliases={t: 2 + t for t in range(n)},
        compiler_params=pltpu.CompilerParams(has_side_effects=DATAFLOW),
    )(*arrays, follows)
    return outs[0], outs[1], list(outs[2:2 + ns]), list(outs[2 + ns:2 + n]), outs[2 + n]


def _copies_wait(plan, started, follows, *, name):
    send_sems, recv_sems, srcs, lands, _ = started
    ns, n = len(srcs), len(srcs) + len(lands)

    def body(*refs):
        ins, s_sems, r_sems = refs[:n], refs[n], refs[n + 1]
        sends, arrivals = plan(ins[:ns], ins[ns:], s_sems, r_sems)
        for cp in sends:
            cp.wait_send()
        for cp in arrivals:
            cp.wait_recv()

    outs = pl.pallas_call(
        body,
        name=name,
        out_shape=tuple(pltpu.HBM(a.shape, a.dtype) for a in srcs + lands),
        in_specs=(*[HBM] * n, SEM, SEM, ANY),
        out_specs=tuple([HBM] * n),
        input_output_aliases={t: t for t in range(n)},
        compiler_params=pltpu.CompilerParams(has_side_effects=DATAFLOW),
    )(*srcs, *lands, send_sems, recv_sems, follows)
    return list(outs[:ns]), list(outs[ns:])


def _allgather8(block, *, name):
    rows, cols = block.shape

    def body(in_ref, out_ref, send_sems, recv_sems, local_sem):
        x, y, c, _ = _position()
        me = 4 * x + 2 * y + c
        mine = pltpu.make_async_copy(in_ref, out_ref.at[me], local_sem)
        mine.start()
        peers = []
        for flip in range(1, N_DEV):
            fx, fy, fc = (flip >> 2) & 1, (flip >> 1) & 1, flip & 1
            peers.append(((1 - x if fx else x), (1 - y if fy else y), (1 - c if fc else c)))
        sends = [_remote(in_ref, out_ref.at[me], send_sems, recv_sems, j, peer) for j, peer in enumerate(peers)]
        for cp in sends:
            cp.start()
        for j, (px, py, pc) in enumerate(peers):
            slot = out_ref.at[4 * px + 2 * py + pc]
            _remote(slot, slot, send_sems, recv_sems, j, (px, py, pc)).wait_recv()
        for cp in sends:
            cp.wait_send()
        mine.wait()

    vmem = pl.BlockSpec(memory_space=pltpu.VMEM)
    return pl.pallas_call(
        body,
        out_shape=jax.ShapeDtypeStruct((N_DEV, rows, cols), F32),
        in_specs=[vmem],
        out_specs=vmem,
        scratch_shapes=[pltpu.SemaphoreType.DMA((N_DEV - 1,)), pltpu.SemaphoreType.DMA((N_DEV - 1,)),
                        pltpu.SemaphoreType.DMA],
        name=name,
    )(block)


def _sum8(stack, *, name):
    _, rows, cols = stack.shape

    def body(s_ref, o_ref):
        acc = s_ref[0]
        for i in range(1, N_DEV):
            acc = acc + s_ref[i]
        o_ref[...] = acc

    return pl.pallas_call(body, out_shape=jax.ShapeDtypeStruct((rows, cols), F32), name=name)(stack)


def _swap_halves_in(grads, kinds):
    n = len(grads)

    def body(*refs):
        ins, outs = refs[:n], refs[n:2 * n]
        send_sems, recv_sems = refs[2 * n:]
        x, y, c, _ = _position()
        sibling = (x, y, 1 - c)
        sends = [_remote(_half_of(ins[t], kinds[t], 1 - c), outs[t], send_sems, recv_sems, t, sibling) for t in range(n)]
        for cp in sends:
            cp.start()
        for t in range(n):
            _remote(_half_of(ins[t], kinds[t], c), outs[t], send_sems, recv_sems, t, sibling).wait_recv()
        for cp in sends:
            cp.wait_send()

    def half(g, kind):
        rows, cols = g.shape
        return (rows // 2, cols) if kind == "col" else (rows, cols // 2)

    return pl.pallas_call(
        body,
        out_shape=[jax.ShapeDtypeStruct(half(g, k), F32) for g, k in zip(grads, kinds)],
        in_specs=[ANY] * n,
        out_specs=[ANY] * n,
        scratch_shapes=[pltpu.SemaphoreType.DMA((n,)), pltpu.SemaphoreType.DMA((n,))],
        name="reduce_core_pair",
    )(*grads)


def _half_add(grad, theirs, kind, c_arr, *, name):
    rows, cols = theirs.shape
    tr = _pick(rows, (256, 128))
    nb = rows // tr
    if kind == "col":
        g_spec = pl.BlockSpec((tr, cols), lambda i, c_ref: (c_ref[0] * nb + i, 0))
    else:
        g_spec = pl.BlockSpec((tr, cols), lambda i, c_ref: (i, c_ref[0]))
    t_spec = pl.BlockSpec((tr, cols), lambda i, c_ref: (i, 0))

    def body(c_ref, g_ref, t_ref, o32_ref, o16_ref):
        tot = g_ref[...] + t_ref[...]
        o32_ref[...] = tot
        o16_ref[...] = tot.astype(BF16)

    return pl.pallas_call(
        body,
        out_shape=[jax.ShapeDtypeStruct((rows, cols), F32), jax.ShapeDtypeStruct((rows, cols), BF16)],
        grid_spec=pltpu.PrefetchScalarGridSpec(num_scalar_prefetch=1, grid=(nb,), in_specs=[g_spec, t_spec],
                                               out_specs=[t_spec, t_spec]),
        compiler_params=_params("parallel"),
        name=name,
    )(c_arr, grad, theirs)


def _exchange_chips(parts, kinds):
    n = len(parts)

    def body(*refs):
        ins, outs = refs[:n], refs[n:2 * n]
        send_sems, recv_sems = refs[2 * n:]
        x, y, c, chips = _position()
        sends = []
        for t in range(n):
            for k, (cx, cy) in enumerate(chips):
                src = _shard_of(ins[t], kinds[t], 2 * cx + cy, N_CHIPS)
                sends.append(_remote(src, outs[t].at[k], send_sems, recv_sems, 3 * t + k, (cx, cy, c)))
        for cp in sends:
            cp.start()
        for t in range(n):
            for k, (cx, cy) in enumerate(chips):
                src = _shard_of(ins[t], kinds[t], 2 * cx + cy, N_CHIPS)
                _remote(src, outs[t].at[k], send_sems, recv_sems, 3 * t + k, (cx, cy, c)).wait_recv()
        for cp in sends:
            cp.wait_send()

    def piece(p, kind):
        rows, cols = p.shape
        return (3, rows, cols // N_CHIPS) if kind == "col" else (3, rows // N_CHIPS, cols)

    return pl.pallas_call(
        body,
        out_shape=[jax.ShapeDtypeStruct(piece(p, k), BF16) for p, k in zip(parts, kinds)],
        in_specs=[ANY] * n,
        out_specs=[ANY] * n,
        scratch_shapes=[pltpu.SemaphoreType.DMA((3 * n,)), pltpu.SemaphoreType.DMA((3 * n,))],
        name="reduce_chips",
    )(*parts)


def _shard_sum(part32, recv, kind, sc_arr, *, name):
    _, rows, cols = recv.shape
    tr = _pick(rows, (256, 128))
    nb = rows // tr
    if kind == "col":
        whole = (2 * rows, cols)
        p_spec = pl.BlockSpec((tr, cols), lambda i, sc: (i, sc[0]))
        o_spec = pl.BlockSpec((tr, cols), lambda i, sc: (sc[1] * nb + i, 0))
    else:
        whole = (rows, 2 * cols)
        p_spec = pl.BlockSpec((tr, cols), lambda i, sc: (sc[0] * nb + i, 0))
        o_spec = pl.BlockSpec((tr, cols), lambda i, sc: (i, sc[1]))
    r_spec = pl.BlockSpec((3, tr, cols), lambda i, sc: (0, i, 0))

    def body(sc_ref, p_ref, r_ref, o_ref):
        acc = p_ref[...]
        for k in range(3):
            acc = acc + r_ref[k].astype(F32)
        o_ref[...] = acc

    return pl.pallas_call(
        body,
        out_shape=jax.ShapeDtypeStruct(whole, F32),
        grid_spec=pltpu.PrefetchScalarGridSpec(num_scalar_prefetch=1, grid=(nb,), in_specs=[p_spec, r_spec],
                                               out_specs=o_spec),
        compiler_params=_params("parallel"),
        name=name,
    )(sc_arr, part32, recv)


def _swap_halves_out(shards, kinds):
    n = len(shards)

    def body(*refs):
        ins, outs = refs[:n], refs[n:2 * n]
        send_sems, recv_sems = refs[2 * n:]
        x, y, c, _ = _position()
        sibling = (x, y, 1 - c)
        sends = [_remote(_half_of(ins[t], kinds[t], c), _half_of(outs[t], kinds[t], c), send_sems, recv_sems, t, sibling)
                 for t in range(n)]
        for cp in sends:
            cp.start()
        for t in range(n):
            theirs = _half_of(outs[t], kinds[t], 1 - c)
            _remote(theirs, theirs, send_sems, recv_sems, t, sibling).wait_recv()
        for cp in sends:
            cp.wait_send()

    return pl.pallas_call(
        body,
        out_shape=[jax.ShapeDtypeStruct(s.shape, F32) for s in shards],
        in_specs=[ANY] * n,
        out_specs=[ANY] * n,
        input_output_aliases={t: t for t in range(n)},
        scratch_shapes=[pltpu.SemaphoreType.DMA((n,)), pltpu.SemaphoreType.DMA((n,))],
        name="gather_core_pair",
    )(*shards)


def _adamw(w, g, m, v, *, name):
    rows, cols = w.shape
    tr = _pick(rows, (256, 128)) if rows * cols > 64 * 1024 else rows

    def body(w_ref, g_ref, m_ref, v_ref, d_ref, nm_ref, nv_ref):
        gv = g_ref[...]
        nm = ADAM_B1 * m_ref[...] + (1.0 - ADAM_B1) * gv
        nv = ADAM_B2 * v_ref[...] + (1.0 - ADAM_B2) * jnp.square(gv)
        m_hat = nm / (1.0 - ADAM_B1 ** ADAM_STEP)
        v_hat = nv / (1.0 - ADAM_B2 ** ADAM_STEP)
        d_ref[...] = -ADAM_LR * (m_hat / (jnp.sqrt(v_hat) + ADAM_EPS) + ADAM_WD * w_ref[...])
        nm_ref[...] = nm
        nv_ref[...] = nv

    spec = pl.BlockSpec((tr, cols), lambda i: (i, 0))
    return pl.pallas_call(
        body,
        out_shape=[jax.ShapeDtypeStruct((rows, cols), F32)] * 3,
        grid=(rows // tr,),
        in_specs=[spec] * 4,
        out_specs=[spec] * 3,
        compiler_params=_params("parallel"),
        name=name,
    )(w, g, m, v)


BIG = ("even_w_in", "odd_w_qkv", "mlp_w1_0", "mlp_w1_1", "even_w_out", "odd_w_o", "mlp_w2_0", "mlp_w2_1")
BIG_KIND = ("col", "col", "col", "col", "row", "row", "row", "row")


class _TravellingReduction:
    def __init__(self, tag, names, kinds, c_arr, sc_arr):
        self.tag, self.names, self.kinds, self.c_arr, self.sc_arr = tag, names, kinds, c_arr, sc_arr
        self.swap = functools.partial(_pair_swap_copies, kinds)
        self.exchange = functools.partial(_chip_exchange_copies, kinds)
        self.gather = functools.partial(_pair_gather_copies, kinds)

    def pair_swap_start(self, grads, follows):
        half = lambda g, kind: (g.shape[0] // 2, g.shape[1]) if kind == "col" else (g.shape[0], g.shape[1] // 2)
        lands = [lax.empty(half(g, k), F32) for g, k in zip(grads, self.kinds)]
        self.started = _copies_start(self.swap, len(grads), grads, lands, follows, name=f"reduce_{self.tag}_pair_start")

    def pair_swap_finish(self, after):
        grads, theirs = _copies_wait(self.swap, self.started, after, name=f"reduce_{self.tag}_pair_wait")
        self.sums = [_half_add(g, th, k, self.c_arr, name="pair_sum_" + n)
                     for g, th, k, n in zip(grads, theirs, self.kinds, self.names)]

    def chips_start(self, follows):
        parts = [s16 for _, s16 in self.sums]
        piece = lambda p, kind: (3, p.shape[0], p.shape[1] // N_CHIPS) if kind == "col" else (3, p.shape[0] // N_CHIPS, p.shape[1])
        lands = [lax.empty(piece(p, k), BF16) for p, k in zip(parts, self.kinds)]
        self.started = _copies_start(self.exchange, 3 * len(parts), parts, lands, follows,
                                     name=f"reduce_{self.tag}_chips_start")

    def chips_finish(self, after):
        _, recv = _copies_wait(self.exchange, self.started, after, name=f"reduce_{self.tag}_chips_wait")
        self.halves = [_shard_sum(s32, r, k, self.sc_arr, name="chip_sum_" + n)
                       for (s32, _), r, k, n in zip(self.sums, recv, self.kinds, self.names)]

    def pair_gather_start(self, follows):
        self.started = _copies_start(self.gather, len(self.halves), self.halves, [], follows,
                                     name=f"reduce_{self.tag}_gather_start")

    def pair_gather_finish(self, after):
        shards, _ = _copies_wait(self.gather, self.started, after, name=f"reduce_{self.tag}_gather_wait")
        return dict(zip(self.names, shards))
SUBLANES = 8


def _pack_rows(parts, width):
    padded, offsets, r0 = [], [], 0
    for t in parts:
        rows = -(-t.shape[0] // SUBLANES) * SUBLANES
        padded.append(jnp.pad(t, ((0, rows - t.shape[0]), (0, width - t.shape[1]))))
        offsets.append(r0)
        r0 += rows
    return jnp.concatenate(padded, axis=0), offsets


def kernel(x, meta, norm_mix_g, norm_mlp_g, even_w_in, even_ret_gn_g, even_conv_w, even_conv_b, even_conv_ln_g, even_conv_ln_b, even_w_out, odd_w_qkv, odd_q_norm_g, odd_k_norm_g, odd_w_o, mlp_w1, mlp_w2, loss_target, m_meta, m_norm_mix_g, m_norm_mlp_g, m_even_w_in, m_even_ret_gn_g, m_even_conv_w, m_even_conv_b, m_even_conv_ln_g, m_even_conv_ln_b, m_even_w_out, m_odd_w_qkv, m_odd_q_norm_g, m_odd_k_norm_g, m_odd_w_o, m_mlp_w1, m_mlp_w2, v_meta, v_norm_mix_g, v_norm_mlp_g, v_even_w_in, v_even_ret_gn_g, v_even_conv_w, v_even_conv_b, v_even_conv_ln_g, v_even_conv_ln_b, v_even_w_out, v_odd_w_qkv, v_odd_q_norm_g, v_odd_k_norm_g, v_odd_w_o, v_mlp_w1, v_mlp_w2):
    d = D_MODEL
    xi, yi, ci = lax.axis_index("x"), lax.axis_index("y"), lax.axis_index("c")
    chip = 2 * xi + yi
    c_arr = jnp.reshape(ci, (1,)).astype(jnp.int32)
    s_arr = jnp.reshape(chip, (1,)).astype(jnp.int32)

    def split_big(w_in, w_qkv, w1, w_out, w_o, w2):
        return dict(zip(BIG, (w_in[0], w_qkv[0], w1[0], w1[1], w_out[0], w_o[0], w2[0], w2[1])))

    w_big = split_big(even_w_in, odd_w_qkv, mlp_w1, even_w_out, odd_w_o, mlp_w2)
    m_big = split_big(m_even_w_in, m_odd_w_qkv, m_mlp_w1, m_even_w_out, m_odd_w_o, m_mlp_w2)
    v_big = split_big(v_even_w_in, v_odd_w_qkv, v_mlp_w1, v_even_w_out, v_odd_w_o, v_mlp_w2)

    placed = {n: _cast_into_whole(w_big[n], k, s_arr, name="cast_" + n) for n, k in zip(BIG, BIG_KIND)}
    kind_of = dict(zip(BIG, BIG_KIND))
    (w_in_full,) = _allgather_weights([placed["even_w_in"]], [kind_of["even_w_in"]])
    packed, (r_meta, r_conv, r_gn) = _pack_rows([meta, even_conv_w[0], even_ret_gn_g[0]], d // N_CHIPS)
    gathered = _allgather8(packed, name="allgather_small_params")[0::2]
    groups = dict(l0=("even_w_out", "mlp_w1_0", "mlp_w2_0"), qkv=("odd_w_qkv",), l1=("odd_w_o", "mlp_w1_1", "mlp_w2_1"))
    in_flight, follows = {}, gathered[0, 0:1, 0:1] + w_in_full[0:1, 0:1].astype(F32)
    for group, names in groups.items():
        plan = functools.partial(_gather_copies, [kind_of[n] for n in names])
        in_flight[group] = (plan, _copies_start(plan, TARGETS * len(names), [placed[n] for n in names], [], follows,
                                                name="gather_" + group + "_start"))
        follows = in_flight[group][1][-1]
    started = follows[0:1, 0:1]

    def later(group, after):
        plan, state = in_flight[group]
        return _copies_wait(plan, state, after, name="gather_" + group + "_wait")[0]

    sc_arr = jnp.concatenate([s_arr, c_arr])
    early = ("odd_w_qkv", "odd_w_o", "mlp_w1_1", "mlp_w2_1"), ("mlp_w1_0", "mlp_w2_0")
    red_l1, red_m0 = (_TravellingReduction(tag, names, [kind_of[n] for n in names], c_arr, sc_arr)
                      for tag, names in zip(("l1", "m0"), early))
    grad_big = {}

    def reached(point, after, grads=None):
        if point == "l1_grads":
            red_l1.pair_swap_start([grads[n] for n in red_l1.names], after)
            return red_l1.started[-1]
        if point == "l1_done":
            red_l1.pair_swap_finish(after)
            red_l1.chips_start(after)
            return red_l1.started[-1]
        if point == "l0_mlp_grads":
            red_m0.pair_swap_start([grads[n] for n in red_m0.names], after)
            return red_m0.started[-1]
        if point == "l0_dwout":
            red_m0.pair_swap_finish(after)
            red_m0.chips_start(after)
            return red_m0.started[-1]
        if point == "l0_retention_bwd":
            red_l1.chips_finish(after)
            red_l1.pair_gather_start(after)
            return red_l1.started[-1]
        if point == "l0_conv_bwd":
            red_m0.chips_finish(after)
            red_m0.pair_gather_start(after)
            grad_big.update(red_l1.pair_gather_finish(after))
            return red_m0.started[-1]
        return None

    across = lambda r0, rows, width: jnp.concatenate([gathered[s, r0:r0 + rows, 0:width] for s in range(N_CHIPS)], axis=1)
    meta_full = across(r_meta, N_META, d // N_CHIPS) + started
    conv_w_full = across(r_conv, CONV_WIDTH, d // N_CHIPS)
    gn_full = across(r_gn, RET_HEADS, RET_V_DIM // N_CHIPS)

    sq, g = _local_step(
        x[0], loss_target[0], meta_full, norm_mix_g, norm_mlp_g, w_in_full, gn_full, conv_w_full,
        even_conv_b[0], even_conv_ln_g[0], even_conv_ln_b[0], odd_q_norm_g[0], odd_k_norm_g[0], later, reached)
    grad_big.update(red_m0.pair_gather_finish(g["even_w_in"]))
    loss = lax.psum(0.5 * sq / d, ("x", "y", "c"))

    small_names = ("norm_mix_g", "norm_mlp_g", "even_conv_b", "even_conv_ln_g", "even_conv_ln_b", "odd_q_norm_g",
                   "odd_k_norm_g", "meta", "even_conv_w", "even_ret_gn_g")
    pack, offsets = _pack_rows([g[n] for n in small_names], d)
    summed = _sum8(_allgather8(pack, name="allgather_small_grads"), name="sum_small_grads")
    small = {n: summed[r0:r0 + g[n].shape[0], 0:g[n].shape[1]] for n, r0 in zip(small_names, offsets)}
    for n in ("meta", "even_conv_w", "even_ret_gn_g"):
        width = small[n].shape[1] // N_CHIPS
        small[n] = lax.dynamic_slice_in_dim(small[n], chip * width, width, axis=1)

    last = ("even_w_in", "even_w_out")
    last_kinds = [kind_of[n] for n in last]
    g_last = [g[n] for n in last]
    theirs = _swap_halves_in(g_last, last_kinds)
    sums = [_half_add(gb, th, k, c_arr, name="pair_sum_" + n) for gb, th, k, n in zip(g_last, theirs, last_kinds, last)]
    recv = _exchange_chips([s16 for _, s16 in sums], last_kinds)
    halves = [_shard_sum(s32, r, k, sc_arr, name="chip_sum_" + n) for (s32, _), r, k, n in zip(sums, recv, last_kinds, last)]
    grad_big.update(zip(last, _swap_halves_out(halves, last_kinds)))

    upd = {n: _adamw(w_big[n], grad_big[n], m_big[n], v_big[n], name="adamw_" + n) for n in BIG}

    def join(name, idx, lead):
        if name in ("mlp_w1", "mlp_w2"):
            return jnp.stack([upd[name + "_0"][idx], upd[name + "_1"][idx]]) if idx >= 0 else jnp.stack(
                [grad_big[name + "_0"], grad_big[name + "_1"]])
        t = upd[name][idx] if idx >= 0 else grad_big[name]
        return t[None] if lead else t

    small_w = dict(meta=meta, norm_mix_g=norm_mix_g, norm_mlp_g=norm_mlp_g, even_ret_gn_g=even_ret_gn_g[0],
                   even_conv_w=even_conv_w[0], even_conv_b=even_conv_b, even_conv_ln_g=even_conv_ln_g,
                   even_conv_ln_b=even_conv_ln_b, odd_q_norm_g=odd_q_norm_g, odd_k_norm_g=odd_k_norm_g)
    small_m = dict(meta=m_meta, norm_mix_g=m_norm_mix_g, norm_mlp_g=m_norm_mlp_g, even_ret_gn_g=m_even_ret_gn_g[0],
                   even_conv_w=m_even_conv_w[0], even_conv_b=m_even_conv_b, even_conv_ln_g=m_even_conv_ln_g,
                   even_conv_ln_b=m_even_conv_ln_b, odd_q_norm_g=m_odd_q_norm_g, odd_k_norm_g=m_odd_k_norm_g)
    small_v = dict(meta=v_meta, norm_mix_g=v_norm_mix_g, norm_mlp_g=v_norm_mlp_g, even_ret_gn_g=v_even_ret_gn_g[0],
                   even_conv_w=v_even_conv_w[0], even_conv_b=v_even_conv_b, even_conv_ln_g=v_even_conv_ln_g,
                   even_conv_ln_b=v_even_conv_ln_b, odd_q_norm_g=v_odd_q_norm_g, odd_k_norm_g=v_odd_k_norm_g)
    small_upd = {n: _adamw(small_w[n], small[n], small_m[n], small_v[n], name="adamw_" + n) for n in small_w}
    leading = ("even_ret_gn_g", "even_conv_w")

    order = ("meta", "norm_mix_g", "norm_mlp_g", "even_w_in", "even_ret_gn_g", "even_conv_w", "even_conv_b",
             "even_conv_ln_g", "even_conv_ln_b", "even_w_out", "odd_w_qkv", "odd_q_norm_g", "odd_k_norm_g", "odd_w_o",
             "mlp_w1", "mlp_w2")
    big_lead = ("even_w_in", "even_w_out", "odd_w_qkv", "odd_w_o")

    def leaf(name, idx):
        if name in small_w:
            t = small_upd[name][idx] if idx >= 0 else small[name]
            return t[None] if name in leading else t
        return join(name, idx, name in big_lead)

    outs = [loss, g["x"][None]]
    for idx in (-1, 0, 1, 2):
        outs += [leaf(n, idx) for n in order]
    return tuple(outs)
```

```python
import functools

import jax
import jax.numpy as jnp
from jax import lax
from jax.experimental import pallas as pl
from jax.experimental.pallas import tpu as pltpu

F32 = jnp.float32
BF16 = jnp.bfloat16

D_MODEL = 1024
N_META = 16
CHUNK = 128
PAD_FRONT = (-N_META) % CHUNK
RET_HEADS = 4
RET_QK_DIM = 128
RET_V_DIM = 256
RET_QK_W = RET_HEADS * RET_QK_DIM
RET_V_W = RET_HEADS * RET_V_DIM
CONV_WIDTH = 31
CONV_HALO = 32
RET_DECAY_OFFSET = 5.0
ROPE_BASE = 10000.0
SB_HEADS = 16
SB_HEAD_DIM = 64
D_FF = 4 * D_MODEL
EPS = 1e-6
ADAM_LR = 0.001
ADAM_B1 = 0.9
ADAM_B2 = 0.999
ADAM_EPS = 1e-08
ADAM_WD = 0.01
ADAM_STEP = 10

N_CHIPS = 4
N_DEV = 8
VMEM_LIMIT = 56 * 1024 * 1024
MESH = pl.DeviceIdType.MESH
ANY = pl.BlockSpec(memory_space=pl.ANY)


def _params(*sem):
    return pltpu.CompilerParams(dimension_semantics=sem, vmem_limit_bytes=VMEM_LIMIT)


def _pick(n, cands):
    for c in cands:
        if n % c == 0:
            return c
    return n


def _sigmoid(x):
    return 1.0 / (1.0 + jnp.exp(-x))


def _dot(a, b):
    return lax.dot_general(a, b, (((1,), (0,)), ((), ())), preferred_element_type=F32)


def _dot_nt(a, b):
    return lax.dot_general(a, b, (((1,), (1,)), ((), ())), preferred_element_type=F32)


def _dot_tn(a, b):
    return lax.dot_general(a, b, (((0,), (0,)), ((), ())), preferred_element_type=F32)


def _split_dot(x, m):
    hi = x.astype(BF16)
    lo = (x - hi.astype(F32)).astype(BF16)
    return _dot(hi, m) + _dot(lo, m)


def _matmul(a, b, *, mode, out_dtypes, epilogue=None, extras=(), name, after=None):
    if mode == "nn":
        (m, k), (k2, n) = a.shape, b.shape
    elif mode == "nt":
        (m, k), (n, k2) = a.shape, b.shape
    else:
        (k, m), (k2, n) = a.shape, b.shape
    assert k == k2, (a.shape, b.shape, mode)
    tm = _pick(m, (1056, 1024, 768, 512, 384, 256, 128, 96))
    tn = _pick(n, (1024, 768, 512, 256, 128))
    tk = _pick(k, (1056, 1024, 768, 512, 384, 256, 128, 96))
    nk = k // tk
    dot = {"nn": _dot, "nt": _dot_nt, "tn": _dot_tn}[mode]
    n_extra, n_out = len(extras), len(out_dtypes)
    n_after = 0 if after is None else 1
    if epilogue is None:
        epilogue = lambda acc: (acc,)

    def body(a_ref, b_ref, *rest):
        extra_refs = rest[:n_extra]
        out_refs = rest[n_extra + n_after:n_extra + n_after + n_out]
        part = dot(a_ref[...].astype(BF16), b_ref[...].astype(BF16))

        def finish(acc):
            res = epilogue(acc, *[r[...] for r in extra_refs])
            for o_ref, r in zip(out_refs, res):
                o_ref[...] = r.astype(o_ref.dtype)

        if nk == 1:
            finish(part)
        else:
            acc_ref = rest[-1]
            kk = pl.program_id(2)

            @pl.when(kk == 0)
            def _():
                acc_ref[...] = part

            @pl.when(kk > 0)
            def _():
                acc_ref[...] += part

            @pl.when(kk == nk - 1)
            def _():
                finish(acc_ref[...])

    if mode == "nn":
        a_spec = pl.BlockSpec((tm, tk), lambda i, j, kk: (i, kk))
        b_spec = pl.BlockSpec((tk, tn), lambda i, j, kk: (kk, j))
    elif mode == "nt":
        a_spec = pl.BlockSpec((tm, tk), lambda i, j, kk: (i, kk))
        b_spec = pl.BlockSpec((tn, tk), lambda i, j, kk: (j, kk))
    else:
        a_spec = pl.BlockSpec((tk, tm), lambda i, j, kk: (kk, i))
        b_spec = pl.BlockSpec((tk, tn), lambda i, j, kk: (kk, j))
    o_spec = pl.BlockSpec((tm, tn), lambda i, j, kk: (i, j))
    outs = pl.pallas_call(
        body,
        out_shape=[jax.ShapeDtypeStruct((m, n), dt) for dt in out_dtypes],
        grid=(m // tm, n // tn, nk),
        in_specs=[a_spec, b_spec] + [o_spec] * n_extra + [ANY] * n_after,
        out_specs=[o_spec] * n_out,
        scratch_shapes=[pltpu.VMEM((tm, tn), F32)] if nk > 1 else [],
        compiler_params=_params("parallel", "parallel", "arbitrary"),
        name=name,
    )(a, b, *extras, *([] if after is None else [after]))
    return outs[0] if n_out == 1 else outs


def _add_epilogue(acc, res):
    return (res + acc,)


def _rmsnorm_fwd(x, g, *, name):
    p, d = x.shape
    rows = _pick(p, (384, 128, 96))

    def body(x_ref, g_ref, o_ref):
        xv = x_ref[...]
        r = lax.rsqrt(jnp.mean(xv * xv, axis=-1, keepdims=True) + EPS)
        o_ref[...] = (xv * r * g_ref[...]).astype(o_ref.dtype)

    return pl.pallas_call(
        body,
        out_shape=jax.ShapeDtypeStruct((p, d), BF16),
        grid=(p // rows,),
        in_specs=[pl.BlockSpec((rows, d), lambda i: (i, 0)), pl.BlockSpec((1, d), lambda i: (0, 0))],
        out_specs=pl.BlockSpec((rows, d), lambda i: (i, 0)),
        compiler_params=_params("parallel"),
        name=name,
    )(x, g)


def _rmsnorm_bwd(x, g, dy, dres, *, name):
    p, d = x.shape
    rows = _pick(p, (384, 128, 96))

    def body(x_ref, g_ref, dy_ref, dres_ref, dx_ref, dg_ref):
        xv = x_ref[...]
        r = lax.rsqrt(jnp.mean(xv * xv, axis=-1, keepdims=True) + EPS)
        dyv = dy_ref[...]
        gdy = dyv * g_ref[...]
        proj = jnp.mean(xv * gdy, axis=-1, keepdims=True)
        dx_ref[...] = dres_ref[...] + r * gdy - xv * (r * r * r) * proj
        part = jnp.sum(dyv * xv * r, axis=0, keepdims=True)

        @pl.when(pl.program_id(0) == 0)
        def _():
            dg_ref[...] = part

        @pl.when(pl.program_id(0) > 0)
        def _():
            dg_ref[...] += part

    row_spec = pl.BlockSpec((rows, d), lambda i: (i, 0))
    vec_spec = pl.BlockSpec((1, d), lambda i: (0, 0))
    return pl.pallas_call(
        body,
        out_shape=[jax.ShapeDtypeStruct((p, d), F32), jax.ShapeDtypeStruct((1, d), F32)],
        grid=(p // rows,),
        in_specs=[row_spec, vec_spec, row_spec, row_spec],
        out_specs=[row_spec, vec_spec],
        compiler_params=_params("arbitrary"),
        name=name,
    )(x, g, dy, dres)


def _mlp_fwd(h, g, w1, w2, *, name):
    hn = _rmsnorm_fwd(h, g, name=name + "_norm")

    def act(acc):
        r = jnp.maximum(acc, 0.0)
        return acc, r * r

    z, a2 = _matmul(hn, w1, mode="nn", out_dtypes=(F32, BF16), epilogue=act, name=name + "_up")
    out = _matmul(a2, w2, mode="nn", out_dtypes=(F32,), epilogue=_add_epilogue, extras=(h,), name=name + "_down")
    return out, (hn, z, a2)


def _mlp_bwd(h, g, w1, w2, saved, dout, *, name, after=None):
    hn, z, a2 = saved

    def dact(acc, zt):
        return (acc * (2.0 * jnp.maximum(zt, 0.0)),)

    dz = _matmul(dout, w2, mode="nt", out_dtypes=(BF16,), epilogue=dact, extras=(z,), name=name + "_dz", after=after)
    dw2 = _matmul(a2, dout, mode="tn", out_dtypes=(F32,), name=name + "_dw2")
    dw1 = _matmul(hn, dz, mode="tn", out_dtypes=(F32,), name=name + "_dw1")
    dhn = _matmul(dz, w1, mode="nt", out_dtypes=(F32,), name=name + "_dhn")
    dh, dg = _rmsnorm_bwd(h, g, dhn, dout, name=name + "_dnorm")
    return dh, dg, dw1, dw2


def _retention_tables(p):
    half = RET_QK_DIM // 2
    inv_freq = ROPE_BASE ** (-jnp.arange(half, dtype=F32) / half)
    ang = jnp.arange(p, dtype=F32)[:, None] * inv_freq[None, :]
    cos, sin = jnp.cos(ang), jnp.sin(ang)
    cosf = jnp.concatenate([cos, cos], axis=1)
    sins = jnp.concatenate([-sin, sin], axis=1)
    log_g = jnp.log1p(-jnp.exp2(-RET_DECAY_OFFSET - jnp.arange(RET_HEADS, dtype=F32)))
    idx = jnp.arange(CHUNK, dtype=F32)
    diff = idx[:, None] - idx[None, :]
    inner = jnp.where(diff[None] >= 0, jnp.exp(jnp.maximum(diff, 0.0)[None] * log_g[:, None, None]), 0.0)
    kdec = jnp.exp((CHUNK - 1 - idx)[None, :] * log_g[:, None])
    qdec = jnp.exp((idx + 1.0)[None, :] * log_g[:, None])
    cdec = jnp.exp(CHUNK * log_g)
    kdec = jnp.broadcast_to(kdec[:, :, None], (RET_HEADS, CHUNK, RET_QK_DIM))
    qdec = jnp.broadcast_to(qdec[:, :, None], (RET_HEADS, CHUNK, RET_QK_DIM))
    cdec = jnp.broadcast_to(cdec[:, None, None], (RET_HEADS, RET_QK_DIM, RET_V_DIM))
    return cosf, sins, inner, kdec, qdec, cdec


def _rot(x, cosf, sins):
    return x * cosf + pltpu.roll(x, RET_QK_DIM // 2, 1) * sins


def _rot_bwd(dy, cosf, sins):
    return dy * cosf + pltpu.roll(dy * sins, RET_QK_DIM // 2, 1)


def _ret_in_specs(chunk_of):
    nh = RET_HEADS
    q_spec = pl.BlockSpec((CHUNK, RET_QK_DIM), lambda h, s: (chunk_of(s), h))
    k_spec = pl.BlockSpec((CHUNK, RET_QK_DIM), lambda h, s: (chunk_of(s), nh + h))
    v_spec = pl.BlockSpec((CHUNK, RET_V_DIM), lambda h, s: (chunk_of(s), nh + h))
    g_spec = pl.BlockSpec((CHUNK, RET_V_DIM), lambda h, s: (chunk_of(s), 2 * nh + h))
    rope_spec = pl.BlockSpec((CHUNK, RET_QK_DIM), lambda h, s: (chunk_of(s), 0))
    head_sq = pl.BlockSpec((None, CHUNK, CHUNK), lambda h, s: (h, 0, 0))
    head_qk = pl.BlockSpec((None, CHUNK, RET_QK_DIM), lambda h, s: (h, 0, 0))
    head_st = pl.BlockSpec((None, RET_QK_DIM, RET_V_DIM), lambda h, s: (h, 0, 0))
    gam_spec = pl.BlockSpec((None, 1, RET_V_DIM), lambda h, s: (h, 0, 0))
    return [q_spec, k_spec, v_spec, g_spec, rope_spec, rope_spec, head_sq, head_qk, head_qk, head_st, gam_spec]


def _retention_fwd(proj, gn_g, tables, *, name):
    p = proj.shape[0]
    n_chunks = p // CHUNK
    scale = RET_QK_DIM ** -0.5

    def body(q_ref, k_ref, v_ref, g_ref, cos_ref, sin_ref, inner_ref, kdec_ref, qdec_ref, cdec_ref, gam_ref,
             og_ref, opre_ref, sprev_ref, s_scr):
        @pl.when(pl.program_id(1) == 0)
        def _():
            s_scr[...] = jnp.zeros_like(s_scr)

        cosf, sins = cos_ref[...], sin_ref[...]
        qr = _rot(q_ref[...], cosf, sins)
        kr = _rot(k_ref[...], cosf, sins) * scale
        vb = v_ref[...].astype(BF16)
        scores = _dot_nt(qr.astype(BF16), kr.astype(BF16)) * inner_ref[...]
        state = s_scr[...]
        sprev_ref[...] = state
        o = _dot(scores.astype(BF16), vb) + _dot((qr * qdec_ref[...]).astype(BF16), state.astype(BF16))
        kd = kr * kdec_ref[...]
        s_scr[...] = cdec_ref[...] * state + _dot(kd.T.astype(BF16), vb)
        opre_ref[...] = o
        mu = jnp.mean(o, axis=-1, keepdims=True)
        oc = o - mu
        var = jnp.mean(oc * oc, axis=-1, keepdims=True)
        on = oc * lax.rsqrt(var + EPS) * gam_ref[...]
        gv = g_ref[...]
        og_ref[...] = (gv * _sigmoid(gv) * on).astype(og_ref.dtype)

    chunk_of = lambda s: s
    out_v = pl.BlockSpec((CHUNK, RET_V_DIM), lambda h, s: (s, h))
    return pl.pallas_call(
        body,
        out_shape=[
            jax.ShapeDtypeStruct((p, RET_V_W), BF16),
            jax.ShapeDtypeStruct((p, RET_V_W), F32),
            jax.ShapeDtypeStruct((RET_HEADS, n_chunks, RET_QK_DIM, RET_V_DIM), F32),
        ],
        grid=(RET_HEADS, n_chunks),
        in_specs=_ret_in_specs(chunk_of),
        out_specs=[out_v, out_v, pl.BlockSpec((None, None, RET_QK_DIM, RET_V_DIM), lambda h, s: (h, s, 0, 0))],
        scratch_shapes=[pltpu.VMEM((RET_QK_DIM, RET_V_DIM), F32)],
        compiler_params=_params("parallel", "arbitrary"),
        name=name,
    )(proj, proj, proj, proj, *tables, gn_g.reshape(RET_HEADS, 1, RET_V_DIM))


def _retention_bwd(proj, gn_g, tables, opre, sprev, dog, *, name):
    p = proj.shape[0]
    n_chunks = p // CHUNK
    scale = RET_QK_DIM ** -0.5

    def body(q_ref, k_ref, v_ref, g_ref, cos_ref, sin_ref, inner_ref, kdec_ref, qdec_ref, cdec_ref, gam_ref,
             opre_ref, sprev_ref, dog_ref, dq_ref, dk_ref, dv_ref, dg_ref, dgam_ref, ds_scr):
        first = pl.program_id(1) == 0

        @pl.when(first)
        def _():
            ds_scr[...] = jnp.zeros_like(ds_scr)

        cosf, sins = cos_ref[...], sin_ref[...]
        qr = _rot(q_ref[...], cosf, sins)
        kr = _rot(k_ref[...], cosf, sins) * scale
        qb, kb = qr.astype(BF16), kr.astype(BF16)
        vb = v_ref[...].astype(BF16)
        inner = inner_ref[...]
        qdec, kdec = qdec_ref[...], kdec_ref[...]
        state_b = sprev_ref[...].astype(BF16)
        o = opre_ref[...]
        mu = jnp.mean(o, axis=-1, keepdims=True)
        oc = o - mu
        rstd = lax.rsqrt(jnp.mean(oc * oc, axis=-1, keepdims=True) + EPS)
        xhat = oc * rstd
        gam = gam_ref[...]
        on = xhat * gam
        gv = g_ref[...]
        sig = _sigmoid(gv)
        dogv = dog_ref[...]
        dg_ref[...] = (dogv * on * sig * (1.0 + gv * (1.0 - sig))).astype(dg_ref.dtype)
        don = dogv * gv * sig
        dgam_part = jnp.sum(don * xhat, axis=0, keepdims=True)

        @pl.when(first)
        def _():
            dgam_ref[...] = dgam_part

        @pl.when(jnp.logical_not(first))
        def _():
            dgam_ref[...] += dgam_part

        dxhat = don * gam
        do = rstd * (dxhat - jnp.mean(dxhat, axis=-1, keepdims=True)
                     - xhat * jnp.mean(dxhat * xhat, axis=-1, keepdims=True))
        dob = do.astype(BF16)
        scores_b = (_dot_nt(qb, kb) * inner).astype(BF16)
        da = (_dot_nt(dob, vb) * inner).astype(BF16)
        dv = _dot(scores_b.astype(F32).T.astype(BF16), dob)
        dqr = _dot(da, kb)
        dkr = _dot(da.astype(F32).T.astype(BF16), qb)
        dqr += _dot_nt(dob, state_b) * qdec
        ds_local = _dot((qr * qdec).T.astype(BF16), dob)
        gstate = ds_scr[...]
        gb = gstate.astype(BF16)
        kd_b = (kr * kdec).astype(BF16)
        dkr += _dot_nt(vb, gb) * kdec
        dv += _dot(kd_b, gb)
        ds_scr[...] = cdec_ref[...] * gstate + ds_local
        dq_ref[...] = _rot_bwd(dqr, cosf, sins).astype(dq_ref.dtype)
        dk_ref[...] = _rot_bwd(dkr * scale, cosf, sins).astype(dk_ref.dtype)
        dv_ref[...] = dv.astype(dv_ref.dtype)

    chunk_of = lambda s: n_chunks - 1 - s
    blk_v = pl.BlockSpec((CHUNK, RET_V_DIM), lambda h, s: (chunk_of(s), h))
    blk_qk = pl.BlockSpec((CHUNK, RET_QK_DIM), lambda h, s: (chunk_of(s), h))
    st_spec = pl.BlockSpec((None, None, RET_QK_DIM, RET_V_DIM), lambda h, s: (h, chunk_of(s), 0, 0))
    return pl.pallas_call(
        body,
        out_shape=[
            jax.ShapeDtypeStruct((p, RET_QK_W), BF16),
            jax.ShapeDtypeStruct((p, RET_QK_W), BF16),
            jax.ShapeDtypeStruct((p, RET_V_W), BF16),
            jax.ShapeDtypeStruct((p, RET_V_W), BF16),
            jax.ShapeDtypeStruct((RET_HEADS, 1, RET_V_DIM), F32),
        ],
        grid=(RET_HEADS, n_chunks),
        in_specs=_ret_in_specs(chunk_of) + [blk_v, st_spec, blk_v],
        out_specs=[blk_qk, blk_qk, blk_v, blk_v, pl.BlockSpec((None, 1, RET_V_DIM), lambda h, s: (h, 0, 0))],
        scratch_shapes=[pltpu.VMEM((RET_QK_DIM, RET_V_DIM), F32)],
        compiler_params=_params("parallel", "arbitrary"),
        name=name,
    )(proj, proj, proj, proj, *tables, gn_g.reshape(RET_HEADS, 1, RET_V_DIM), opre, sprev, dog)


def _conv_rows(p):
    return _pick(p, (384, 128))


CONV_CHUNK = 32
F32_SUBLANES = 8


def _shifted_rows(rows):
    return rows + CONV_HALO - F32_SUBLANES


def _shifted_copies(src_scr, sh_scr, n_rows):
    for s in range(1, F32_SUBLANES):
        sh_scr[s - 1] = src_scr[s:s + n_rows, :]


def _tap_rows(src_scr, sh_scr, off, r0, n):
    q, s = divmod(off, F32_SUBLANES)
    ref = src_scr if s == 0 else sh_scr.at[s - 1]
    return ref[pl.ds(pl.multiple_of(r0 + F32_SUBLANES * q, F32_SUBLANES), n), :]


def _ln_stats(y):
    mu = jnp.mean(y, axis=-1, keepdims=True)
    yc = y - mu
    rstd = lax.rsqrt(jnp.mean(yc * yc, axis=-1, keepdims=True) + EPS)
    return yc * rstd, rstd


def _conv_fwd(proj, conv_w, conv_b, ln_g, ln_b, *, name):
    p = proj.shape[0]
    c = D_MODEL
    rows = _conv_rows(p)
    hpb = rows // CONV_HALO
    a_col, gate_col = (2 * RET_QK_W + 2 * RET_V_W) // c, (2 * RET_QK_W + 2 * RET_V_W) // c + 1

    def body(a_ref, gate_ref, ah_ref, gateh_ref, w_ref, b_ref, lg_ref, lb_ref, c_ref, y_ref, hdn_scr, sh_scr):
        i = pl.program_id(0)
        hdn_scr[0:CONV_HALO, :] = ah_ref[...] * _sigmoid(gateh_ref[...])
        hdn_scr[CONV_HALO:, :] = a_ref[...] * _sigmoid(gate_ref[...])
        _shifted_copies(hdn_scr, sh_scr, _shifted_rows(rows))

        def chunk(j, _):
            r0 = pl.multiple_of(j * CONV_CHUNK, CONV_CHUNK)
            acc = jnp.zeros((CONV_CHUNK, c), F32)
            for w in range(CONV_WIDTH):
                off = CONV_HALO - (CONV_WIDTH - 1) + w
                acc += _tap_rows(hdn_scr, sh_scr, off, r0, CONV_CHUNK) * w_ref[w:w + 1, :]
            y_ref[pl.ds(r0, CONV_CHUNK), :] = acc + b_ref[...]
            return 0

        lax.fori_loop(0, rows // CONV_CHUNK, chunk, 0)
        y = y_ref[...]
        yhat, _ = _ln_stats(y)
        ln = yhat * lg_ref[...] + lb_ref[...]
        row = i * rows + lax.broadcasted_iota(jnp.int32, (rows, 1), 0)
        c_ref[...] = jnp.where(row >= PAD_FRONT, ln * _sigmoid(ln), 0.0).astype(c_ref.dtype)

    halo_idx = lambda i: jnp.maximum(i * hpb - 1, 0)
    vec = pl.BlockSpec((1, c), lambda i: (0, 0))
    return pl.pallas_call(
        body,
        out_shape=[jax.ShapeDtypeStruct((p, c), BF16), jax.ShapeDtypeStruct((p, c), F32)],
        grid=(p // rows,),
        in_specs=[
            pl.BlockSpec((rows, c), lambda i: (i, a_col)),
            pl.BlockSpec((rows, c), lambda i: (i, gate_col)),
            pl.BlockSpec((CONV_HALO, c), lambda i: (halo_idx(i), a_col)),
            pl.BlockSpec((CONV_HALO, c), lambda i: (halo_idx(i), gate_col)),
            pl.BlockSpec((CONV_WIDTH, c), lambda i: (0, 0)),
            vec, vec, vec,
        ],
        out_specs=[pl.BlockSpec((rows, c), lambda i: (i, 0)), pl.BlockSpec((rows, c), lambda i: (i, 0))],
        scratch_shapes=[pltpu.VMEM((CONV_HALO + rows, c), F32),
                        pltpu.VMEM((F32_SUBLANES - 1, _shifted_rows(rows), c), F32)],
        compiler_params=_params("parallel"),
        name=name,
    )(proj, proj, proj, proj, conv_w, conv_b, ln_g, ln_b)


def _conv_bwd(proj, conv_w, ln_g, ln_b, y, dcat, *, name):
    p = proj.shape[0]
    c = D_MODEL
    rows = _conv_rows(p)
    hpb = rows // CONV_HALO
    n_blocks = p // rows
    a_col, gate_col = (2 * RET_QK_W + 2 * RET_V_W) // c, (2 * RET_QK_W + 2 * RET_V_W) // c + 1

    def body(a_ref, gate_ref, ah_ref, gateh_ref, w_ref, lg_ref, lb_ref, y_ref, yh_ref, dc_ref, dch_ref,
             da_ref, dgate_ref, dw_ref, db_ref, dlg_ref, dlb_ref, hdn_scr, dy_scr, hdn_sh, dy_sh):
        i = pl.program_id(0)
        lg, lb = lg_ref[...], lb_ref[...]

        def ln_bwd(yv, dcv):
            yhat, rstd = _ln_stats(yv)
            ln = yhat * lg + lb
            sig = _sigmoid(ln)
            dln = dcv * sig * (1.0 + ln * (1.0 - sig))
            dyhat = dln * lg
            dyv = rstd * (dyhat - jnp.mean(dyhat, axis=-1, keepdims=True)
                          - yhat * jnp.mean(dyhat * yhat, axis=-1, keepdims=True))
            return dyv, dln, yhat

        row = i * rows + lax.broadcasted_iota(jnp.int32, (rows, 1), 0)
        dy, dln, yhat = ln_bwd(y_ref[...], jnp.where(row >= PAD_FRONT, dc_ref[...], 0.0))
        dy_halo, _, _ = ln_bwd(yh_ref[...], dch_ref[...])
        dy_scr[0:rows, :] = dy
        dy_scr[rows:, :] = jnp.where(i == n_blocks - 1, 0.0, dy_halo)
        hdn_scr[0:CONV_HALO, :] = ah_ref[...] * _sigmoid(gateh_ref[...])
        hdn_scr[CONV_HALO:, :] = a_ref[...] * _sigmoid(gate_ref[...])
        _shifted_copies(hdn_scr, hdn_sh, _shifted_rows(rows))
        _shifted_copies(dy_scr, dy_sh, _shifted_rows(rows))

        @pl.when(i == 0)
        def _():
            dw_ref[...] = jnp.zeros_like(dw_ref)
            db_ref[...] = jnp.zeros_like(db_ref)
            dlg_ref[...] = jnp.zeros_like(dlg_ref)
            dlb_ref[...] = jnp.zeros_like(dlb_ref)

        n_chunks = rows // CONV_CHUNK

        def input_grad(j, _):
            r0 = pl.multiple_of(j * CONV_CHUNK, CONV_CHUNK)
            dhdn = jnp.zeros((CONV_CHUNK, c), F32)
            for w in range(CONV_WIDTH):
                dhdn += _tap_rows(dy_scr, dy_sh, CONV_WIDTH - 1 - w, r0, CONV_CHUNK) * w_ref[w:w + 1, :]
            here = pl.ds(r0, CONV_CHUNK)
            sig_gate = _sigmoid(gate_ref[here, :])
            da_ref[here, :] = (dhdn * sig_gate).astype(da_ref.dtype)
            dgate_ref[here, :] = (dhdn * a_ref[here, :] * sig_gate * (1.0 - sig_gate)).astype(dgate_ref.dtype)
            return 0

        lax.fori_loop(0, n_chunks, input_grad, 0)
        for w in range(CONV_WIDTH):
            off = CONV_HALO - (CONV_WIDTH - 1) + w

            def tap_grad(j, acc, off=off):
                r0 = pl.multiple_of(j * CONV_CHUNK, CONV_CHUNK)
                prod = dy_scr[pl.ds(r0, CONV_CHUNK), :] * _tap_rows(hdn_scr, hdn_sh, off, r0, CONV_CHUNK)
                for k in range(CONV_CHUNK // F32_SUBLANES):
                    acc = acc + prod[k * F32_SUBLANES:(k + 1) * F32_SUBLANES]
                return acc

            acc = lax.fori_loop(0, n_chunks, tap_grad, jnp.zeros((F32_SUBLANES, c), F32))
            dw_ref[w:w + 1, :] += jnp.sum(acc, axis=0, keepdims=True)
        db_ref[...] += jnp.sum(dy, axis=0, keepdims=True)
        dlg_ref[...] += jnp.sum(dln * yhat, axis=0, keepdims=True)
        dlb_ref[...] += jnp.sum(dln, axis=0, keepdims=True)

    prev_halo = lambda i: jnp.maximum(i * hpb - 1, 0)
    next_halo = lambda i: jnp.minimum((i + 1) * hpb, p // CONV_HALO - 1)
    vec = pl.BlockSpec((1, c), lambda i: (0, 0))
    blk = lambda col: pl.BlockSpec((rows, c), lambda i: (i, col))
    outs = pl.pallas_call(
        body,
        out_shape=[
            jax.ShapeDtypeStruct((p, c), BF16),
            jax.ShapeDtypeStruct((p, c), BF16),
            jax.ShapeDtypeStruct((CONV_WIDTH + 1, c), F32),
            jax.ShapeDtypeStruct((1, c), F32),
            jax.ShapeDtypeStruct((1, c), F32),
            jax.ShapeDtypeStruct((1, c), F32),
        ],
        grid=(n_blocks,),
        in_specs=[
            blk(a_col), blk(gate_col),
            pl.BlockSpec((CONV_HALO, c), lambda i: (prev_halo(i), a_col)),
            pl.BlockSpec((CONV_HALO, c), lambda i: (prev_halo(i), gate_col)),
            pl.BlockSpec((CONV_WIDTH, c), lambda i: (0, 0)),
            vec, vec,
            blk(0),
            pl.BlockSpec((CONV_HALO, c), lambda i: (next_halo(i), 0)),
            blk(1),
            pl.BlockSpec((CONV_HALO, c), lambda i: (next_halo(i), 1)),
        ],
        out_specs=[blk(0), blk(0), pl.BlockSpec((CONV_WIDTH + 1, c), lambda i: (0, 0)), vec, vec, vec],
        scratch_shapes=[pltpu.VMEM((CONV_HALO + rows, c), F32), pltpu.VMEM((rows + CONV_HALO, c), F32),
                        pltpu.VMEM((F32_SUBLANES - 1, _shifted_rows(rows), c), F32),
                        pltpu.VMEM((F32_SUBLANES - 1, _shifted_rows(rows), c), F32)],
        compiler_params=_params("arbitrary"),
        name=name,
    )(proj, proj, proj, proj, conv_w, ln_g, ln_b, y, y, dcat, dcat)
    da, dgate, dw, db, dlg, dlb = outs
    return da, dgate, dw[:CONV_WIDTH], db, dlg, dlb


def _group_matrix():
    r = jnp.arange(D_MODEL)[:, None] // SB_HEAD_DIM
    c = jnp.arange(D_MODEL)[None, :] // SB_HEAD_DIM
    return (r == c).astype(BF16)


def _qknorm_fwd(qkv, qg, kg, *, name):
    p = qkv.shape[0]
    d = D_MODEL
    rows = _pick(p, (384, 128, 96))

    def body(q_ref, k_ref, v_ref, qg_ref, kg_ref, gm_ref, qn_ref, kn_ref, vb_ref):
        gm = gm_ref[...]

        def norm(x, g):
            ms = _split_dot(x * x, gm) * (1.0 / SB_HEAD_DIM)
            return x * lax.rsqrt(ms + EPS) * g

        qn_ref[...] = norm(q_ref[...], qg_ref[...]).astype(BF16)
        kn_ref[...] = norm(k_ref[...], kg_ref[...]).astype(BF16)
        vb_ref[...] = v_ref[...].astype(BF16)

    blk = lambda col: pl.BlockSpec((rows, d), lambda i: (i, col))
    vec = pl.BlockSpec((1, d), lambda i: (0, 0))
    return pl.pallas_call(
        body,
        out_shape=[jax.ShapeDtypeStruct((p, d), BF16)] * 3,
        grid=(p // rows,),
        in_specs=[blk(0), blk(1), blk(2), vec, vec, pl.BlockSpec((d, d), lambda i: (0, 0))],
        out_specs=[blk(0)] * 3,
        compiler_params=_params("parallel"),
        name=name,
    )(qkv, qkv, qkv, qg, kg, _group_matrix())


def _qknorm_bwd(qkv, qg, kg, dqn, dkn, dv, *, name):
    p = qkv.shape[0]
    d = D_MODEL
    rows = _pick(p, (384, 128, 96))

    def body(q_ref, k_ref, qg_ref, kg_ref, gm_ref, dqn_ref, dkn_ref, dv_ref, dqkv_ref, dqg_ref, dkg_ref):
        gm = gm_ref[...]

        def bwd(x, g, dy):
            ms = _split_dot(x * x, gm) * (1.0 / SB_HEAD_DIM)
            r = lax.rsqrt(ms + EPS)
            gdy = dy * g
            proj = _split_dot(x * gdy, gm) * (1.0 / SB_HEAD_DIM)
            return r * gdy - x * (r * r * r) * proj, jnp.sum(dy * x * r, axis=0, keepdims=True)

        dq, dqg = bwd(q_ref[...], qg_ref[...], dqn_ref[...])
        dk, dkg = bwd(k_ref[...], kg_ref[...], dkn_ref[...])
        dqkv_ref[:, 0:d] = dq.astype(BF16)
        dqkv_ref[:, d:2 * d] = dk.astype(BF16)
        dqkv_ref[:, 2 * d:3 * d] = dv_ref[...].astype(BF16)

        @pl.when(pl.program_id(0) == 0)
        def _():
            dqg_ref[...] = dqg
            dkg_ref[...] = dkg

        @pl.when(pl.program_id(0) > 0)
        def _():
            dqg_ref[...] += dqg
            dkg_ref[...] += dkg

    blk = lambda col: pl.BlockSpec((rows, d), lambda i: (i, col))
    vec = pl.BlockSpec((1, d), lambda i: (0, 0))
    return pl.pallas_call(
        body,
        out_shape=[jax.ShapeDtypeStruct((p, 3 * d), BF16), jax.ShapeDtypeStruct((1, d), F32),
                   jax.ShapeDtypeStruct((1, d), F32)],
        grid=(p // rows,),
        in_specs=[blk(0), blk(1), vec, vec, pl.BlockSpec((d, d), lambda i: (0, 0)), blk(0), blk(0), blk(0)],
        out_specs=[pl.BlockSpec((rows, 3 * d), lambda i: (i, 0)), vec, vec],
        compiler_params=_params("arbitrary"),
        name=name,
    )(qkv, qkv, qg, kg, _group_matrix(), dqn, dkn, dv)


SB_PAIR = 2 * SB_HEAD_DIM
SB_GROUP = 8
SB_MASKED = -1e30


def _sb_consts():
    lane = lax.broadcasted_iota(jnp.int32, (CHUNK, SB_PAIR), 1)
    r = lax.broadcasted_iota(jnp.int32, (CHUNK, CHUNK), 0)
    c = lax.broadcasted_iota(jnp.int32, (CHUNK, CHUNK), 1)
    lo = (lane < SB_HEAD_DIM).astype(F32).astype(BF16)
    ones = jnp.ones((CHUNK, CHUNK), BF16)
    twice = lambda m: jnp.concatenate([jnp.concatenate([m, ones], axis=1)] * 2, axis=0)
    later, earlier = twice((r > c).astype(BF16)), twice((r < c).astype(BF16))
    not_before = (c >= r).astype(F32) * SB_MASKED
    padding = (c < PAD_FRONT).astype(F32) * SB_MASKED
    return (lo, 1.0 - lo), c, later, earlier, not_before, padding


def _sb_halves(t, head_lanes):
    return t * head_lanes[0], t * head_lanes[1]


def _sb_logits(qh, kg, biases):
    z = _dot_nt(qh, kg)
    tiles = []
    for b, bias in enumerate(biases):
        zt = z[:, b * CHUNK:(b + 1) * CHUNK]
        if bias is not None:
            zt = zt + bias
        ls_pos = jnp.minimum(zt, 0.0) - jnp.log(1.0 + jnp.exp(-jnp.abs(zt)))
        tiles.append((ls_pos, ls_pos - zt))
    return tiles


def _sb_block_sums(tiles, m):
    st = jnp.concatenate(tiles, axis=0)
    hi = st.astype(BF16)
    lo = (st - hi.astype(F32)).astype(BF16)
    tot = _dot(jnp.concatenate([hi, lo], axis=1), m)
    return [(tot[i * CHUNK:(i + 1) * CHUNK, 0:CHUNK], tot[i * CHUNK:(i + 1) * CHUNK, CHUNK:2 * CHUNK])
            for i in range(len(tiles))]


def _sb_plan(qi, padding, not_before):
    top = lax.div(qi, SB_GROUP)
    size = qi - SB_GROUP * top + 1
    pad_if_first = padding * (top == 0).astype(F32)
    masks = []
    for n_b in range(1, SB_GROUP + 1):
        m = [None] * n_b
        m[n_b - 1] = not_before
        m[0] = pad_if_first if m[0] is None else m[0] + pad_if_first
        masks.append(m)
    return top, size, masks


def _once_if(cond, fn, carry):
    return lax.fori_loop(0, jnp.where(cond, 1, 0), lambda s, cr: fn(cr), carry)


def _sb_head_rows(tg, lanes, n_b):
    return jnp.concatenate([tg[b * CHUNK:(b + 1) * CHUNK] * lanes for b in range(n_b)], axis=0)


def _sb_fwd(qn, kn, vb, *, name):
    p = qn.shape[0]
    n_blocks = p // CHUNK
    n_pairs = SB_HEADS // 2
    scale = SB_HEAD_DIM ** -0.5

    def body(q_ref, k_ref, v_ref, o_ref, car_ref):
        head_lanes, c, later, _, not_before, padding = _sb_consts()

        def q_block(qi, _):
            rows = pl.ds(pl.multiple_of(qi * CHUNK, CHUNK), CHUNK)
            qh = _sb_halves(q_ref[rows, :], head_lanes)
            qs = (qh[0] * scale, qh[1] * scale)

            def blocks(kb0, biases, carry):
                n_b = len(biases)
                acc, run0, run1, sav0, sav1 = carry
                krows = pl.ds(pl.multiple_of(kb0 * CHUNK, CHUNK), n_b * CHUNK)
                kg, vg = k_ref[krows, :], v_ref[krows, :]
                tiles = [_sb_logits(qs[h], kg, biases) for h in range(2)]
                sums = [_sb_block_sums([log_keep for _, log_keep in tiles[h]], later) for h in range(2)]
                cols = [(c == kb0 + b).astype(F32) for b in range(n_b)]
                runs, savs = [run0, run1], [sav0, sav1]
                for h in range(2):
                    ws = [None] * n_b
                    for b in reversed(range(n_b)):
                        after, row_sum = sums[h][b]
                        ws[b] = jnp.exp(tiles[h][b][0] + after + runs[h]).astype(BF16)
                        savs[h] = savs[h] + cols[b] * runs[h]
                        runs[h] = runs[h] + row_sum
                    acc = acc + _dot(jnp.concatenate(ws, axis=1), _sb_head_rows(vg, head_lanes[h], n_b))
                return acc, runs[0], runs[1], savs[0], savs[1]

            zt = qh[0].astype(F32) * 0.0
            top, size, masks = _sb_plan(qi, padding, not_before)
            carry = (zt, zt, zt, zt, zt)
            for m in masks:
                carry = _once_if(size == len(m), functools.partial(blocks, SB_GROUP * top, m), carry)
            carry = lax.fori_loop(0, jnp.maximum(top - 1, 0),
                                  lambda it, cr: blocks(SB_GROUP * (top - 1 - it), [None] * SB_GROUP, cr), carry)
            carry = _once_if(top > 0, functools.partial(blocks, 0, [padding] + [None] * (SB_GROUP - 1)), carry)
            acc, _, _, sav0, sav1 = carry
            o_ref[rows, :] = acc.astype(o_ref.dtype)
            car_ref[rows, 0:CHUNK] = sav0
            car_ref[rows, CHUNK:2 * CHUNK] = sav1
            return 0

        lax.fori_loop(0, n_blocks, q_block, 0)

    col = pl.BlockSpec((p, SB_PAIR), lambda g: (0, g))
    return pl.pallas_call(
        body,
        out_shape=[jax.ShapeDtypeStruct((p, D_MODEL), BF16), jax.ShapeDtypeStruct((p, n_pairs * 2 * CHUNK), F32)],
        grid=(n_pairs,),
        in_specs=[col, col, col],
        out_specs=[col, pl.BlockSpec((p, 2 * CHUNK), lambda g: (0, g))],
        compiler_params=_params("parallel"),
        name=name,
    )(qn, kn, vb)


def _sb_bwd(qn, kn, vb, carries, do, *, name):
    p = qn.shape[0]
    n_blocks = p // CHUNK
    n_pairs = SB_HEADS // 2
    scale = SB_HEAD_DIM ** -0.5

    def body(q_ref, k_ref, v_ref, car_ref, do_ref, dq_ref, dk_ref, dv_ref):
        head_lanes, c, later, earlier, not_before, padding = _sb_consts()
        dk_ref[...] = jnp.zeros_like(dk_ref)
        dv_ref[...] = jnp.zeros_like(dv_ref)

        def q_block(qi, _):
            rows = pl.ds(pl.multiple_of(qi * CHUNK, CHUNK), CHUNK)
            qh = _sb_halves(q_ref[rows, :], head_lanes)
            qs = (qh[0] * scale, qh[1] * scale)
            doh = _sb_halves(do_ref[rows, :].astype(BF16), head_lanes)
            do2, q2 = jnp.concatenate(doh, axis=0), jnp.concatenate(qs, axis=0)
            sav = (car_ref[rows, 0:CHUNK], car_ref[rows, CHUNK:2 * CHUNK])

            def blocks(kb0, biases, carry):
                n_b = len(biases)
                dq_acc, pre0, pre1 = carry
                krows = pl.ds(pl.multiple_of(kb0 * CHUNK, CHUNK), n_b * CHUNK)
                kg, vg = k_ref[krows, :], v_ref[krows, :]
                cols = [(c == kb0 + b).astype(F32) for b in range(n_b)]
                block = lambda t, b: t[:, b * CHUNK:(b + 1) * CHUNK]
                tiles = [_sb_logits(qs[h], kg, biases) for h in range(2)]
                afters = [_sb_block_sums([log_keep for _, log_keep in tiles[h]], later) for h in range(2)]
                dws = [_dot_nt(doh[h], vg) for h in range(2)]
                ws, es, befores = [], [], []
                for h in range(2):
                    runs = [jnp.sum(cols[b] * sav[h], axis=-1, keepdims=True) for b in range(n_b)]
                    ws.append([jnp.exp(tiles[h][b][0] + afters[h][b][0] + runs[b]) for b in range(n_b)])
                    es.append([ws[h][b] * block(dws[h], b) for b in range(n_b)])
                    befores.append(_sb_block_sums(es[h], earlier))
                pres = [pre0, pre1]
                dz2, w2 = [], []
                for h in range(2):
                    dzs = []
                    for b in range(n_b):
                        before, row_sum = befores[h][b]
                        sig = jnp.exp(tiles[h][b][0])
                        e = es[h][b]
                        dzs.append((e - (e + before + pres[h]) * sig).astype(BF16))
                        pres[h] = pres[h] + row_sum
                    dz2.append(jnp.concatenate(dzs, axis=1))
                    w2.append(jnp.concatenate([t.astype(BF16) for t in ws[h]], axis=1))
                    dq_acc = dq_acc + _dot(dz2[h], _sb_head_rows(kg, head_lanes[h], n_b))
                dv_ref[krows, :] += _dot_tn(jnp.concatenate(w2, axis=0), do2)
                dk_ref[krows, :] += _dot_tn(jnp.concatenate(dz2, axis=0), q2)
                return dq_acc, pres[0], pres[1]

            zt = qh[0].astype(F32) * 0.0
            top, size, masks = _sb_plan(qi, padding, not_before)
            carry = _once_if(top > 0, functools.partial(blocks, 0, [padding] + [None] * (SB_GROUP - 1)), (zt, zt, zt))
            carry = lax.fori_loop(1, top, lambda g, cr: blocks(SB_GROUP * g, [None] * SB_GROUP, cr), carry)
            for m in masks:
                carry = _once_if(size == len(m), functools.partial(blocks, SB_GROUP * top, m), carry)
            dq_acc, _, _ = carry
            dq_ref[rows, :] = dq_acc * scale
            return 0

        lax.fori_loop(0, n_blocks, q_block, 0)

    col = pl.BlockSpec((p, SB_PAIR), lambda g: (0, g))
    return pl.pallas_call(
        body,
        out_shape=[jax.ShapeDtypeStruct((p, D_MODEL), F32)] * 3,
        grid=(n_pairs,),
        in_specs=[col, col, col, pl.BlockSpec((p, 2 * CHUNK), lambda g: (0, g)), col],
        out_specs=[col, col, col],
        compiler_params=_params("parallel"),
        name=name,
    )(qn, kn, vb, carries, do)


def _loss_head(h, target, *, name):
    p, d = h.shape
    n_blocks = p // CHUNK

    def body(h_ref, t_ref, sq_ref, dh_ref):
        i = pl.program_id(0)

        @pl.when(i == 0)
        def _():
            sq_ref[...] = jnp.zeros_like(sq_ref)
            dh_ref[...] = jnp.zeros_like(dh_ref)

        @pl.when(i > 0)
        def _():
            err = h_ref[...] - t_ref[...]
            sq_ref[...] += jnp.sum(err * err)
            dh_ref[...] = err * (1.0 / d)

    return pl.pallas_call(
        body,
        out_shape=[jax.ShapeDtypeStruct((8, 128), F32), jax.ShapeDtypeStruct((p, d), F32)],
        grid=(n_blocks,),
        in_specs=[pl.BlockSpec((CHUNK, d), lambda i: (i, 0)),
                  pl.BlockSpec((CHUNK, d), lambda i: (jnp.maximum(i - 1, 0), 0))],
        out_specs=[pl.BlockSpec((8, 128), lambda i: (0, 0)), pl.BlockSpec((CHUNK, d), lambda i: (i, 0))],
        compiler_params=_params("arbitrary"),
        name=name,
    )(h, target)


def _local_step(x, target, meta, norm_mix_g, norm_mlp_g, w_in, gn_g, conv_w, conv_b, ln_g, ln_b, qn_g, kn_g, later,
                reached=lambda point, after, grads=None: None):
    seq = x.shape[0]
    p = PAD_FRONT + N_META + seq
    d = D_MODEL
    tables = _retention_tables(p)
    row = lambda v: v.reshape(1, -1)
    h0 = jnp.concatenate([jnp.zeros((PAD_FRONT, d), F32), meta, x], axis=0)

    hn0 = _rmsnorm_fwd(h0, row(norm_mix_g[0]), name="l0_mix_norm")
    proj = _matmul(hn0, w_in, mode="nn", out_dtypes=(F32,), name="l0_proj")
    og, opre, sprev = _retention_fwd(proj, gn_g, tables, name="l0_retention")
    cb, y_conv = _conv_fwd(proj, conv_w, row(conv_b), row(ln_g), row(ln_b), name="l0_conv")
    cat = jnp.concatenate([og, cb], axis=1)
    w_out, w1_0, w2_0 = later("l0", cat)
    w1, w2 = [w1_0, None], [w2_0, None]
    h1 = _matmul(cat, w_out, mode="nn", out_dtypes=(F32,), epilogue=_add_epilogue, extras=(h0,), name="l0_mix_out")
    h2, mlp0 = _mlp_fwd(h1, row(norm_mlp_g[0]), w1[0], w2[0], name="l0_mlp")

    hn1 = _rmsnorm_fwd(h2, row(norm_mix_g[1]), name="l1_mix_norm")
    (w_qkv,) = later("qkv", hn1)
    qkv = _matmul(hn1, w_qkv, mode="nn", out_dtypes=(F32,), name="l1_qkv")
    qg_t, kg_t = jnp.tile(row(qn_g), (1, SB_HEADS)), jnp.tile(row(kn_g), (1, SB_HEADS))
    qn, kn, vb = _qknorm_fwd(qkv, qg_t, kg_t, name="l1_qknorm")
    o_sb, carries = _sb_fwd(qn, kn, vb, name="l1_stickbreak")
    w_o, w1[1], w2[1] = later("l1", o_sb)
    h3 = _matmul(o_sb, w_o, mode="nn", out_dtypes=(F32,), epilogue=_add_epilogue, extras=(h2,), name="l1_mix_out")
    h4, mlp1 = _mlp_fwd(h3, row(norm_mlp_g[1]), w1[1], w2[1], name="l1_mlp")

    sq, dh4 = _loss_head(h4, target, name="loss_head")

    dh3, dg_mlp1, dw1_1, dw2_1 = _mlp_bwd(h3, row(norm_mlp_g[1]), w1[1], w2[1], mlp1, dh4, name="l1_mlp_bwd")
    do_sb = _matmul(dh3, w_o, mode="nt", out_dtypes=(F32,), name="l1_do")
    dw_o = _matmul(o_sb, dh3, mode="tn", out_dtypes=(F32,), name="l1_dwo")
    dqn, dkn, dv = _sb_bwd(qn, kn, vb, carries, do_sb, name="l1_stickbreak_bwd")
    dqkv, dqg_t, dkg_t = _qknorm_bwd(qkv, qg_t, kg_t, dqn, dkn, dv, name="l1_qknorm_bwd")
    dw_qkv = _matmul(hn1, dqkv, mode="tn", out_dtypes=(F32,), name="l1_dwqkv")
    pin = lambda arr, tok: arr if tok is None else arr + tok[0:1, 0:1]
    tok = reached("l1_grads", dw_qkv, dict(odd_w_qkv=dw_qkv, odd_w_o=dw_o, mlp_w1_1=dw1_1, mlp_w2_1=dw2_1))
    dhn1 = _matmul(dqkv, w_qkv, mode="nt", out_dtypes=(F32,), name="l1_dhn", after=tok)
    dh2, dg_mix1 = _rmsnorm_bwd(h2, row(norm_mix_g[1]), dhn1, dh3, name="l1_mix_dnorm")
    tok = reached("l1_done", dh2)

    dh1, dg_mlp0, dw1_0, dw2_0 = _mlp_bwd(h1, row(norm_mlp_g[0]), w1[0], w2[0], mlp0, dh2, name="l0_mlp_bwd", after=tok)
    tok = reached("l0_mlp_grads", dh1, dict(mlp_w1_0=dw1_0, mlp_w2_0=dw2_0))
    dcat = _matmul(dh1, w_out, mode="nt", out_dtypes=(F32,), name="l0_dcat", after=tok)
    dw_out = _matmul(cat, dh1, mode="tn", out_dtypes=(F32,), name="l0_dwout")
    tok = reached("l0_dwout", dw_out)
    dq, dk, dvr, dgate_r, dgn = _retention_bwd(proj, pin(gn_g, tok), tables, opre, sprev, dcat, name="l0_retention_bwd")
    tok = reached("l0_retention_bwd", dq)
    da, dgate_c, dconv_w, dconv_b, dln_g, dln_b = _conv_bwd(proj, conv_w, pin(row(ln_g), tok), row(ln_b), y_conv, dcat,
                                                            name="l0_conv_bwd")
    tok = reached("l0_conv_bwd", da)
    dproj = jnp.concatenate([dq, dk, dvr, dgate_r, da, dgate_c], axis=1)
    dw_in = _matmul(hn0, dproj, mode="tn", out_dtypes=(F32,), name="l0_dwin", after=tok)
    dhn0 = _matmul(dproj, w_in, mode="nt", out_dtypes=(F32,), name="l0_dhn")
    dh0, dg_mix0 = _rmsnorm_bwd(h0, row(norm_mix_g[0]), dhn0, dh1, name="l0_mix_dnorm")

    fold = lambda t: t.reshape(SB_HEADS, SB_HEAD_DIM).sum(axis=0)
    grads = dict(
        x=dh0[PAD_FRONT + N_META:],
        meta=dh0[PAD_FRONT:PAD_FRONT + N_META],
        norm_mix_g=jnp.concatenate([dg_mix0, dg_mix1], axis=0),
        norm_mlp_g=jnp.concatenate([dg_mlp0, dg_mlp1], axis=0),
        even_w_in=dw_in,
        even_ret_gn_g=dgn.reshape(RET_HEADS, RET_V_DIM),
        even_conv_w=dconv_w,
        even_conv_b=dconv_b,
        even_conv_ln_g=dln_g,
        even_conv_ln_b=dln_b,
        even_w_out=dw_out,
        odd_w_qkv=dw_qkv,
        odd_q_norm_g=fold(dqg_t)[None],
        odd_k_norm_g=fold(dkg_t)[None],
        odd_w_o=dw_o,
        mlp_w1=(dw1_0, dw1_1),
        mlp_w2=(dw2_0, dw2_1),
    )
    return sq[0, 0], grads


def _position():
    x, y, c = lax.axis_index("x"), lax.axis_index("y"), lax.axis_index("c")
    other_chips = [(1 - x, y), (x, 1 - y), (1 - x, 1 - y)]
    return x, y, c, other_chips


def _shard_of(ref, kind, s, n):
    rows, cols = ref.shape
    if kind == "col":
        return ref.at[:, pl.ds(s * (cols // n), cols // n)]
    return ref.at[pl.ds(s * (rows // n), rows // n), :]


def _half_of(ref, kind, c):
    rows, cols = ref.shape
    if kind == "col":
        return ref.at[pl.ds(c * (rows // 2), rows // 2), :]
    return ref.at[:, pl.ds(c * (cols // 2), cols // 2)]


def _remote(src, dst, send_sems, recv_sems, idx, device):
    return pltpu.make_async_remote_copy(src_ref=src, dst_ref=dst, send_sem=send_sems.at[idx], recv_sem=recv_sems.at[idx],
                                        device_id=device, device_id_type=MESH)


def _cast_into_whole(w, kind, s_arr, *, name):
    rows, cols = w.shape
    tr = _pick(rows, (256, 128))
    nb = rows // tr
    if kind == "col":
        whole, o_spec = (rows, cols * N_CHIPS), pl.BlockSpec((tr, cols), lambda i, s_ref: (i, s_ref[0]))
    else:
        whole, o_spec = (rows * N_CHIPS, cols), pl.BlockSpec((tr, cols), lambda i, s_ref: (s_ref[0] * nb + i, 0))

    def body(s_ref, w_ref, o_ref):
        o_ref[...] = w_ref[...].astype(BF16)

    return pl.pallas_call(
        body,
        out_shape=jax.ShapeDtypeStruct(whole, BF16),
        grid_spec=pltpu.PrefetchScalarGridSpec(num_scalar_prefetch=1, grid=(nb,),
                                               in_specs=[pl.BlockSpec((tr, cols), lambda i, s_ref: (i, 0))],
                                               out_specs=o_spec),
        compiler_params=_params("parallel"),
        name=name,
    )(s_arr, w)


def _allgather_weights(wholes, kinds):
    n = len(wholes)

    def body(*refs):
        ins, outs = refs[:n], refs[n:2 * n]
        send_sems, recv_sems = refs[2 * n:]
        x, y, c, chips = _position()
        me_chip = 2 * x + y
        sibling = (x, y, 1 - c)
        sends = []
        for t in range(n):
            for k, (cx, cy) in enumerate(chips):
                src = _half_of(_shard_of(ins[t], kinds[t], me_chip, N_CHIPS), kinds[t], c)
                dst = _half_of(_shard_of(outs[t], kinds[t], me_chip, N_CHIPS), kinds[t], c)
                sends.append(_remote(src, dst, send_sems, recv_sems, 6 * t + k, (cx, cy, c)))
        for cp in sends:
            cp.start()
        passed = []
        for t in range(n):
            for k, (cx, cy) in enumerate(chips):
                landed = _half_of(_shard_of(outs[t], kinds[t], 2 * cx + cy, N_CHIPS), kinds[t], c)
                _remote(landed, landed, send_sems, recv_sems, 6 * t + k, (cx, cy, c)).wait_recv()
                fwd = _remote(landed, landed, send_sems, recv_sems, 6 * t + 3 + k, sibling)
                fwd.start()
                passed.append(fwd)
        for t in range(n):
            for k, (cx, cy) in enumerate(chips):
                theirs = _half_of(_shard_of(outs[t], kinds[t], 2 * cx + cy, N_CHIPS), kinds[t], 1 - c)
                _remote(theirs, theirs, send_sems, recv_sems, 6 * t + 3 + k, sibling).wait_recv()
        for cp in sends + passed:
            cp.wait_send()

    return pl.pallas_call(
        body,
        out_shape=[jax.ShapeDtypeStruct(w.shape, BF16) for w in wholes],
        in_specs=[ANY] * n,
        out_specs=[ANY] * n,
        input_output_aliases={t: t for t in range(n)},
        scratch_shapes=[pltpu.SemaphoreType.DMA((6 * n,)), pltpu.SemaphoreType.DMA((6 * n,))],
        name="allgather_weights",
    )(*wholes)


HBM = pl.BlockSpec(memory_space=pltpu.HBM)
SEM = pl.BlockSpec(memory_space=pltpu.SEMAPHORE)
DATAFLOW = pltpu.SideEffectType.DATAFLOW_SIDE_EFFECTING
TARGETS = 6


def _gather_copies(kinds, refs, _, send_sems, recv_sems):
    x, y, c, chips = _position()
    me_chip = 2 * x + y
    sends, lands = [], []
    for t, (ref, kind) in enumerate(zip(refs, kinds)):
        mine = _half_of(_shard_of(ref, kind, me_chip, N_CHIPS), kind, c)
        for k, (cx, cy) in enumerate(chips):
            for other_core in range(2):
                j = TARGETS * t + 2 * k + other_core
                peer_c = 1 - c if other_core else c
                sends.append(_remote(mine, mine, send_sems, recv_sems, j, (cx, cy, peer_c)))
                theirs = _half_of(_shard_of(ref, kind, 2 * cx + cy, N_CHIPS), kind, peer_c)
                lands.append(_remote(theirs, theirs, send_sems, recv_sems, j, (cx, cy, peer_c)))
    return sends, lands


def _pair_swap_copies(kinds, srcs, lands, send_sems, recv_sems):
    x, y, c, _ = _position()
    sibling = (x, y, 1 - c)
    sends = [_remote(_half_of(srcs[t], kinds[t], 1 - c), lands[t], send_sems, recv_sems, t, sibling) for t in range(len(srcs))]
    arrivals = [_remote(_half_of(srcs[t], kinds[t], c), lands[t], send_sems, recv_sems, t, sibling) for t in range(len(srcs))]
    return sends, arrivals


def _chip_exchange_copies(kinds, srcs, lands, send_sems, recv_sems):
    x, y, c, chips = _position()
    sends, arrivals = [], []
    for t in range(len(srcs)):
        for k, (cx, cy) in enumerate(chips):
            src = _shard_of(srcs[t], kinds[t], 2 * cx + cy, N_CHIPS)
            sends.append(_remote(src, lands[t].at[k], send_sems, recv_sems, 3 * t + k, (cx, cy, c)))
            arrivals.append(_remote(src, lands[t].at[k], send_sems, recv_sems, 3 * t + k, (cx, cy, c)))
    return sends, arrivals


def _pair_gather_copies(kinds, srcs, lands, send_sems, recv_sems):
    x, y, c, _ = _position()
    sibling = (x, y, 1 - c)
    sends, arrivals = [], []
    for t in range(len(srcs)):
        mine, theirs = _half_of(srcs[t], kinds[t], c), _half_of(srcs[t], kinds[t], 1 - c)
        sends.append(_remote(mine, mine, send_sems, recv_sems, t, sibling))
        arrivals.append(_remote(theirs, theirs, send_sems, recv_sems, t, sibling))
    return sends, arrivals


def _copies_start(plan, n_sems, srcs, lands, follows, *, name):
    ns, n = len(srcs), len(srcs) + len(lands)

    def body(*refs):
        send_sems, recv_sems = refs[n + 1], refs[n + 2]
        thru, token = refs[n + 3:2 * n + 3], refs[2 * n + 3]
        sends, _ = plan(thru[:ns], thru[ns:], send_sems, recv_sems)
        for cp in sends:
            cp.start()
        token[...] = jnp.zeros_like(token)

    arrays = [pltpu.with_memory_space_constraint(a, pltpu.HBM) for a in list(srcs) + list(lands)]
    outs = pl.pallas_call(
        body,
        name=name,
        out_shape=(pltpu.SemaphoreType.DMA((n_sems,)), pltpu.SemaphoreType.DMA((n_sems,)),
                   *[pltpu.HBM(a.shape, a.dtype) for a in arrays], jax.ShapeDtypeStruct((8, 128), F32)),
        in_specs=(*[HBM] * n, ANY),
        out_specs=(SEM, SEM, *[HBM] * n, pl.BlockSpec(memory_space=pltpu.VMEM)),
        input_output_aliases={t: 2 + t for t in range(n)},
        compiler_params=pltpu.CompilerParams(has_side_effects=DATAFLOW),
    )(*arrays, follows)
    return outs[0], outs[1], list(outs[2:2 + ns]), list(outs[2 + ns:2 + n]), outs[2 + n]


def _copies_wait(plan, started, follows, *, name):
    send_sems, recv_sems, srcs, lands, _ = started
    ns, n = len(srcs), len(srcs) + len(lands)

    def body(*refs):
        ins, s_sems, r_sems = refs[:n], refs[n], refs[n + 1]
        sends, arrivals = plan(ins[:ns], ins[ns:], s_sems, r_sems)
        for cp in sends:
            cp.wait_send()
        for cp in arrivals:
            cp.wait_recv()

    outs = pl.pallas_call(
        body,
        name=name,
        out_shape=tuple(pltpu.HBM(a.shape, a.dtype) for a in srcs + lands),
        in_specs=(*[HBM] * n, SEM, SEM, ANY),
        out_specs=tuple([HBM] * n),
        input_output_aliases={t: t for t in range(n)},
        compiler_params=pltpu.CompilerParams(has_side_effects=DATAFLOW),
    )(*srcs, *lands, send_sems, recv_sems, follows)
    return list(outs[:ns]), list(outs[ns:])


def _allgather8(block, *, name):
    rows, cols = block.shape

    def body(in_ref, out_ref, send_sems, recv_sems, local_sem):
        x, y, c, _ = _position()
        me = 4 * x + 2 * y + c
        mine = pltpu.make_async_copy(in_ref, out_ref.at[me], local_sem)
        mine.start()
        peers = []
        for flip in range(1, N_DEV):
            fx, fy, fc = (flip >> 2) & 1, (flip >> 1) & 1, flip & 1
            peers.append(((1 - x if fx else x), (1 - y if fy else y), (1 - c if fc else c)))
        sends = [_remote(in_ref, out_ref.at[me], send_sems, recv_sems, j, peer) for j, peer in enumerate(peers)]
        for cp in sends:
            cp.start()
        for j, (px, py, pc) in enumerate(peers):
            slot = out_ref.at[4 * px + 2 * py + pc]
            _remote(slot, slot, send_sems, recv_sems, j, (px, py, pc)).wait_recv()
        for cp in sends:
            cp.wait_send()
        mine.wait()

    vmem = pl.BlockSpec(memory_space=pltpu.VMEM)
    return pl.pallas_call(
        body,
        out_shape=jax.ShapeDtypeStruct((N_DEV, rows, cols), F32),
        in_specs=[vmem],
        out_specs=vmem,
        scratch_shapes=[pltpu.SemaphoreType.DMA((N_DEV - 1,)), pltpu.SemaphoreType.DMA((N_DEV - 1,)),
                        pltpu.SemaphoreType.DMA],
        name=name,
    )(block)


def _sum8(stack, *, name):
    _, rows, cols = stack.shape

    def body(s_ref, o_ref):
        acc = s_ref[0]
        for i in range(1, N_DEV):
            acc = acc + s_ref[i]
        o_ref[...] = acc

    return pl.pallas_call(body, out_shape=jax.ShapeDtypeStruct((rows, cols), F32), name=name)(stack)


def _swap_halves_in(grads, kinds):
    n = len(grads)

    def body(*refs):
        ins, outs = refs[:n], refs[n:2 * n]
        send_sems, recv_sems = refs[2 * n:]
        x, y, c, _ = _position()
        sibling = (x, y, 1 - c)
        sends = [_remote(_half_of(ins[t], kinds[t], 1 - c), outs[t], send_sems, recv_sems, t, sibling) for t in range(n)]
        for cp in sends:
            cp.start()
        for t in range(n):
            _remote(_half_of(ins[t], kinds[t], c), outs[t], send_sems, recv_sems, t, sibling).wait_recv()
        for cp in sends:
            cp.wait_send()

    def half(g, kind):
        rows, cols = g.shape
        return (rows // 2, cols) if kind == "col" else (rows, cols // 2)

    return pl.pallas_call(
        body,
        out_shape=[jax.ShapeDtypeStruct(half(g, k), F32) for g, k in zip(grads, kinds)],
        in_specs=[ANY] * n,
        out_specs=[ANY] * n,
        scratch_shapes=[pltpu.SemaphoreType.DMA((n,)), pltpu.SemaphoreType.DMA((n,))],
        name="reduce_core_pair",
    )(*grads)


def _half_add(grad, theirs, kind, c_arr, *, name):
    rows, cols = theirs.shape
    tr = _pick(rows, (256, 128))
    nb = rows // tr
    if kind == "col":
        g_spec = pl.BlockSpec((tr, cols), lambda i, c_ref: (c_ref[0] * nb + i, 0))
    else:
        g_spec = pl.BlockSpec((tr, cols), lambda i, c_ref: (i, c_ref[0]))
    t_spec = pl.BlockSpec((tr, cols), lambda i, c_ref: (i, 0))

    def body(c_ref, g_ref, t_ref, o32_ref, o16_ref):
        tot = g_ref[...] + t_ref[...]
        o32_ref[...] = tot
        o16_ref[...] = tot.astype(BF16)

    return pl.pallas_call(
        body,
        out_shape=[jax.ShapeDtypeStruct((rows, cols), F32), jax.ShapeDtypeStruct((rows, cols), BF16)],
        grid_spec=pltpu.PrefetchScalarGridSpec(num_scalar_prefetch=1, grid=(nb,), in_specs=[g_spec, t_spec],
                                               out_specs=[t_spec, t_spec]),
        compiler_params=_params("parallel"),
        name=name,
    )(c_arr, grad, theirs)


def _exchange_chips(parts, kinds):
    n = len(parts)

    def body(*refs):
        ins, outs = refs[:n], refs[n:2 * n]
        send_sems, recv_sems = refs[2 * n:]
        x, y, c, chips = _position()
        sends = []
        for t in range(n):
            for k, (cx, cy) in enumerate(chips):
                src = _shard_of(ins[t], kinds[t], 2 * cx + cy, N_CHIPS)
                sends.append(_remote(src, outs[t].at[k], send_sems, recv_sems, 3 * t + k, (cx, cy, c)))
        for cp in sends:
            cp.start()
        for t in range(n):
            for k, (cx, cy) in enumerate(chips):
                src = _shard_of(ins[t], kinds[t], 2 * cx + cy, N_CHIPS)
                _remote(src, outs[t].at[k], send_sems, recv_sems, 3 * t + k, (cx, cy, c)).wait_recv()
        for cp in sends:
            cp.wait_send()

    def piece(p, kind):
        rows, cols = p.shape
        return (3, rows, cols // N_CHIPS) if kind == "col" else (3, rows // N_CHIPS, cols)

    return pl.pallas_call(
        body,
        out_shape=[jax.ShapeDtypeStruct(piece(p, k), BF16) for p, k in zip(parts, kinds)],
        in_specs=[ANY] * n,
        out_specs=[ANY] * n,
        scratch_shapes=[pltpu.SemaphoreType.DMA((3 * n,)), pltpu.SemaphoreType.DMA((3 * n,))],
        name="reduce_chips",
    )(*parts)


def _shard_sum(part32, recv, kind, sc_arr, *, name):
    _, rows, cols = recv.shape
    tr = _pick(rows, (256, 128))
    nb = rows // tr
    if kind == "col":
        whole = (2 * rows, cols)
        p_spec = pl.BlockSpec((tr, cols), lambda i, sc: (i, sc[0]))
        o_spec = pl.BlockSpec((tr, cols), lambda i, sc: (sc[1] * nb + i, 0))
    else:
        whole = (rows, 2 * cols)
        p_spec = pl.BlockSpec((tr, cols), lambda i, sc: (sc[0] * nb + i, 0))
        o_spec = pl.BlockSpec((tr, cols), lambda i, sc: (i, sc[1]))
    r_spec = pl.BlockSpec((3, tr, cols), lambda i, sc: (0, i, 0))

    def body(sc_ref, p_ref, r_ref, o_ref):
        acc = p_ref[...]
        for k in range(3):
            acc = acc + r_ref[k].astype(F32)
        o_ref[...] = acc

    return pl.pallas_call(
        body,
        out_shape=jax.ShapeDtypeStruct(whole, F32),
        grid_spec=pltpu.PrefetchScalarGridSpec(num_scalar_prefetch=1, grid=(nb,), in_specs=[p_spec, r_spec],
                                               out_specs=o_spec),
        compiler_params=_params("parallel"),
        name=name,
    )(sc_arr, part32, recv)


def _swap_halves_out(shards, kinds):
    n = len(shards)

    def body(*refs):
        ins, outs = refs[:n], refs[n:2 * n]
        send_sems, recv_sems = refs[2 * n:]
        x, y, c, _ = _position()
        sibling = (x, y, 1 - c)
        sends = [_remote(_half_of(ins[t], kinds[t], c), _half_of(outs[t], kinds[t], c), send_sems, recv_sems, t, sibling)
                 for t in range(n)]
        for cp in sends:
            cp.start()
        for t in range(n):
            theirs = _half_of(outs[t], kinds[t], 1 - c)
            _remote(theirs, theirs, send_sems, recv_sems, t, sibling).wait_recv()
        for cp in sends:
            cp.wait_send()

    return pl.pallas_call(
        body,
        out_shape=[jax.ShapeDtypeStruct(s.shape, F32) for s in shards],
        in_specs=[ANY] * n,
        out_specs=[ANY] * n,
        input_output_aliases={t: t for t in range(n)},
        scratch_shapes=[pltpu.SemaphoreType.DMA((n,)), pltpu.SemaphoreType.DMA((n,))],
        name="gather_core_pair",
    )(*shards)


def _adamw(w, g, m, v, *, name):
    rows, cols = w.shape
    tr = _pick(rows, (256, 128)) if rows * cols > 64 * 1024 else rows

    def body(w_ref, g_ref, m_ref, v_ref, d_ref, nm_ref, nv_ref):
        gv = g_ref[...]
        nm = ADAM_B1 * m_ref[...] + (1.0 - ADAM_B1) * gv
        nv = ADAM_B2 * v_ref[...] + (1.0 - ADAM_B2) * jnp.square(gv)
        m_hat = nm / (1.0 - ADAM_B1 ** ADAM_STEP)
        v_hat = nv / (1.0 - ADAM_B2 ** ADAM_STEP)
        d_ref[...] = -ADAM_LR * (m_hat / (jnp.sqrt(v_hat) + ADAM_EPS) + ADAM_WD * w_ref[...])
        nm_ref[...] = nm
        nv_ref[...] = nv

    spec = pl.BlockSpec((tr, cols), lambda i: (i, 0))
    return pl.pallas_call(
        body,
        out_shape=[jax.ShapeDtypeStruct((rows, cols), F32)] * 3,
        grid=(rows // tr,),
        in_specs=[spec] * 4,
        out_specs=[spec] * 3,
        compiler_params=_params("parallel"),
        name=name,
    )(w, g, m, v)


BIG = ("even_w_in", "odd_w_qkv", "mlp_w1_0", "mlp_w1_1", "even_w_out", "odd_w_o", "mlp_w2_0", "mlp_w2_1")
BIG_KIND = ("col", "col", "col", "col", "row", "row", "row", "row")


class _TravellingReduction:
    def __init__(self, tag, names, kinds, c_arr, sc_arr):
        self.tag, self.names, self.kinds, self.c_arr, self.sc_arr = tag, names, kinds, c_arr, sc_arr
        self.swap = functools.partial(_pair_swap_copies, kinds)
        self.exchange = functools.partial(_chip_exchange_copies, kinds)
        self.gather = functools.partial(_pair_gather_copies, kinds)

    def pair_swap_start(self, grads, follows):
        half = lambda g, kind: (g.shape[0] // 2, g.shape[1]) if kind == "col" else (g.shape[0], g.shape[1] // 2)
        lands = [lax.empty(half(g, k), F32) for g, k in zip(grads, self.kinds)]
        self.started = _copies_start(self.swap, len(grads), grads, lands, follows, name=f"reduce_{self.tag}_pair_start")

    def pair_swap_finish(self, after):
        grads, theirs = _copies_wait(self.swap, self.started, after, name=f"reduce_{self.tag}_pair_wait")
        self.sums = [_half_add(g, th, k, self.c_arr, name="pair_sum_" + n)
                     for g, th, k, n in zip(grads, theirs, self.kinds, self.names)]

    def chips_start(self, follows):
        parts = [s16 for _, s16 in self.sums]
        piece = lambda p, kind: (3, p.shape[0], p.shape[1] // N_CHIPS) if kind == "col" else (3, p.shape[0] // N_CHIPS, p.shape[1])
        lands = [lax.empty(piece(p, k), BF16) for p, k in zip(parts, self.kinds)]
        self.started = _copies_start(self.exchange, 3 * len(parts), parts, lands, follows,
                                     name=f"reduce_{self.tag}_chips_start")

    def chips_finish(self, after):
        _, recv = _copies_wait(self.exchange, self.started, after, name=f"reduce_{self.tag}_chips_wait")
        self.halves = [_shard_sum(s32, r, k, self.sc_arr, name="chip_sum_" + n)
                       for (s32, _), r, k, n in zip(self.sums, recv, self.kinds, self.names)]

    def pair_gather_start(self, follows):
        self.started = _copies_start(self.gather, len(self.halves), self.halves, [], follows,
                                     name=f"reduce_{self.tag}_gather_start")

    def pair_gather_finish(self, after):
        shards, _ = _copies_wait(self.gather, self.started, after, name=f"reduce_{self.tag}_gather_wait")
        return dict(zip(self.names, shards))
SUBLANES = 8


def _pack_rows(parts, width):
    padded, offsets, r0 = [], [], 0
    for t in parts:
        rows = -(-t.shape[0] // SUBLANES) * SUBLANES
        padded.append(jnp.pad(t, ((0, rows - t.shape[0]), (0, width - t.shape[1]))))
        offsets.append(r0)
        r0 += rows
    return jnp.concatenate(padded, axis=0), offsets


def kernel(x, meta, norm_mix_g, norm_mlp_g, even_w_in, even_ret_gn_g, even_conv_w, even_conv_b, even_conv_ln_g, even_conv_ln_b, even_w_out, odd_w_qkv, odd_q_norm_g, odd_k_norm_g, odd_w_o, mlp_w1, mlp_w2, loss_target, m_meta, m_norm_mix_g, m_norm_mlp_g, m_even_w_in, m_even_ret_gn_g, m_even_conv_w, m_even_conv_b, m_even_conv_ln_g, m_even_conv_ln_b, m_even_w_out, m_odd_w_qkv, m_odd_q_norm_g, m_odd_k_norm_g, m_odd_w_o, m_mlp_w1, m_mlp_w2, v_meta, v_norm_mix_g, v_norm_mlp_g, v_even_w_in, v_even_ret_gn_g, v_even_conv_w, v_even_conv_b, v_even_conv_ln_g, v_even_conv_ln_b, v_even_w_out, v_odd_w_qkv, v_odd_q_norm_g, v_odd_k_norm_g, v_odd_w_o, v_mlp_w1, v_mlp_w2):
    d = D_MODEL
    xi, yi, ci = lax.axis_index("x"), lax.axis_index("y"), lax.axis_index("c")
    chip = 2 * xi + yi
    c_arr = jnp.reshape(ci, (1,)).astype(jnp.int32)
    s_arr = jnp.reshape(chip, (1,)).astype(jnp.int32)

    def split_big(w_in, w_qkv, w1, w_out, w_o, w2):
        return dict(zip(BIG, (w_in[0], w_qkv[0], w1[0], w1[1], w_out[0], w_o[0], w2[0], w2[1])))

    w_big = split_big(even_w_in, odd_w_qkv, mlp_w1, even_w_out, odd_w_o, mlp_w2)
    m_big = split_big(m_even_w_in, m_odd_w_qkv, m_mlp_w1, m_even_w_out, m_odd_w_o, m_mlp_w2)
    v_big = split_big(v_even_w_in, v_odd_w_qkv, v_mlp_w1, v_even_w_out, v_odd_w_o, v_mlp_w2)

    placed = {n: _cast_into_whole(w_big[n], k, s_arr, name="cast_" + n) for n, k in zip(BIG, BIG_KIND)}
    kind_of = dict(zip(BIG, BIG_KIND))
    (w_in_full,) = _allgather_weights([placed["even_w_in"]], [kind_of["even_w_in"]])
    packed, (r_meta, r_conv, r_gn) = _pack_rows([meta, even_conv_w[0], even_ret_gn_g[0]], d // N_CHIPS)
    gathered = _allgather8(packed, name="allgather_small_params")[0::2]
    groups = dict(l0=("even_w_out", "mlp_w1_0", "mlp_w2_0"), qkv=("odd_w_qkv",), l1=("odd_w_o", "mlp_w1_1", "mlp_w2_1"))
    in_flight, follows = {}, gathered[0, 0:1, 0:1] + w_in_full[0:1, 0:1].astype(F32)
    for group, names in groups.items():
        plan = functools.partial(_gather_copies, [kind_of[n] for n in names])
        in_flight[group] = (plan, _copies_start(plan, TARGETS * len(names), [placed[n] for n in names], [], follows,
                                                name="gather_" + group + "_start"))
        follows = in_flight[group][1][-1]
    started = follows[0:1, 0:1]

    def later(group, after):
        plan, state = in_flight[group]
        return _copies_wait(plan, state, after, name="gather_" + group + "_wait")[0]

    sc_arr = jnp.concatenate([s_arr, c_arr])
    early = ("odd_w_qkv", "odd_w_o", "mlp_w1_1", "mlp_w2_1"), ("mlp_w1_0", "mlp_w2_0")
    red_l1, red_m0 = (_TravellingReduction(tag, names, [kind_of[n] for n in names], c_arr, sc_arr)
                      for tag, names in zip(("l1", "m0"), early))
    grad_big = {}

    def reached(point, after, grads=None):
        if point == "l1_grads":
            red_l1.pair_swap_start([grads[n] for n in red_l1.names], after)
            return red_l1.started[-1]
        if point == "l1_done":
            red_l1.pair_swap_finish(after)
            red_l1.chips_start(after)
            return red_l1.started[-1]
        if point == "l0_mlp_grads":
            red_m0.pair_swap_start([grads[n] for n in red_m0.names], after)
            return red_m0.started[-1]
        if point == "l0_dwout":
            red_m0.pair_swap_finish(after)
            red_m0.chips_start(after)
            return red_m0.started[-1]
        if point == "l0_retention_bwd":
            red_l1.chips_finish(after)
            red_l1.pair_gather_start(after)
            return red_l1.started[-1]
        if point == "l0_conv_bwd":
            red_m0.chips_finish(after)
            red_m0.pair_gather_start(after)
            grad_big.update(red_l1.pair_gather_finish(after))
            return red_m0.started[-1]
        return None

    across = lambda r0, rows, width: jnp.concatenate([gathered[s, r0:r0 + rows, 0:width] for s in range(N_CHIPS)], axis=1)
    meta_full = across(r_meta, N_META, d // N_CHIPS) + started
    conv_w_full = across(r_conv, CONV_WIDTH, d // N_CHIPS)
    gn_full = across(r_gn, RET_HEADS, RET_V_DIM // N_CHIPS)

    sq, g = _local_step(
        x[0], loss_target[0], meta_full, norm_mix_g, norm_mlp_g, w_in_full, gn_full, conv_w_full,
        even_conv_b[0], even_conv_ln_g[0], even_conv_ln_b[0], odd_q_norm_g[0], odd_k_norm_g[0], later, reached)
    grad_big.update(red_m0.pair_gather_finish(g["even_w_in"]))
    loss = lax.psum(0.5 * sq / d, ("x", "y", "c"))

    small_names = ("norm_mix_g", "norm_mlp_g", "even_conv_b", "even_conv_ln_g", "even_conv_ln_b", "odd_q_norm_g",
                   "odd_k_norm_g", "meta", "even_conv_w", "even_ret_gn_g")
    pack, offsets = _pack_rows([g[n] for n in small_names], d)
    summed = _sum8(_allgather8(pack, name="allgather_small_grads"), name="sum_small_grads")
    small = {n: summed[r0:r0 + g[n].shape[0], 0:g[n].shape[1]] for n, r0 in zip(small_names, offsets)}
    for n in ("meta", "even_conv_w", "even_ret_gn_g"):
        width = small[n].shape[1] // N_CHIPS
        small[n] = lax.dynamic_slice_in_dim(small[n], chip * width, width, axis=1)

    last = ("even_w_in", "even_w_out")
    last_kinds = [kind_of[n] for n in last]
    g_last = [g[n] for n in last]
    theirs = _swap_halves_in(g_last, last_kinds)
    sums = [_half_add(gb, th, k, c_arr, name="pair_sum_" + n) for gb, th, k, n in zip(g_last, theirs, last_kinds, last)]
    recv = _exchange_chips([s16 for _, s16 in sums], last_kinds)
    halves = [_shard_sum(s32, r, k, sc_arr, name="chip_sum_" + n) for (s32, _), r, k, n in zip(sums, recv, last_kinds, last)]
    grad_big.update(zip(last, _swap_halves_out(halves, last_kinds)))

    upd = {n: _adamw(w_big[n], grad_big[n], m_big[n], v_big[n], name="adamw_" + n) for n in BIG}

    def join(name, idx, lead):
        if name in ("mlp_w1", "mlp_w2"):
            return jnp.stack([upd[name + "_0"][idx], upd[name + "_1"][idx]]) if idx >= 0 else jnp.stack(
                [grad_big[name + "_0"], grad_big[name + "_1"]])
        t = upd[name][idx] if idx >= 0 else grad_big[name]
        return t[None] if lead else t

    small_w = dict(meta=meta, norm_mix_g=norm_mix_g, norm_mlp_g=norm_mlp_g, even_ret_gn_g=even_ret_gn_g[0],
                   even_conv_w=even_conv_w[0], even_conv_b=even_conv_b, even_conv_ln_g=even_conv_ln_g,
                   even_conv_ln_b=even_conv_ln_b, odd_q_norm_g=odd_q_norm_g, odd_k_norm_g=odd_k_norm_g)
    small_m = dict(meta=m_meta, norm_mix_g=m_norm_mix_g, norm_mlp_g=m_norm_mlp_g, even_ret_gn_g=m_even_ret_gn_g[0],
                   even_conv_w=m_even_conv_w[0], even_conv_b=m_even_conv_b, even_conv_ln_g=m_even_conv_ln_g,
                   even_conv_ln_b=m_even_conv_ln_b, odd_q_norm_g=m_odd_q_norm_g, odd_k_norm_g=m_odd_k_norm_g)
    small_v = dict(meta=v_meta, norm_mix_g=v_norm_mix_g, norm_mlp_g=v_norm_mlp_g, even_ret_gn_g=v_even_ret_gn_g[0],
                   even_conv_w=v_even_conv_w[0], even_conv_b=v_even_conv_b, even_conv_ln_g=v_even_conv_ln_g,
                   even_conv_ln_b=v_even_conv_ln_b, odd_q_norm_g=v_odd_q_norm_g, odd_k_norm_g=v_odd_k_norm_g)
    small_upd = {n: _adamw(small_w[n], small[n], small_m[n], small_v[n], name="adamw_" + n) for n in small_w}
    leading = ("even_ret_gn_g", "even_conv_w")

    order = ("meta", "norm_mix_g", "norm_mlp_g", "even_w_in", "even_ret_gn_g", "even_conv_w", "even_conv_b",
             "even_conv_ln_g", "even_conv_ln_b", "even_w_out", "odd_w_qkv", "odd_q_norm_g", "odd_k_norm_g", "odd_w_o",
             "mlp_w1", "mlp_w2")
    big_lead = ("even_w_in", "even_w_out", "odd_w_qkv", "odd_w_o")

    def leaf(name, idx):
        if name in small_w:
            t = small_upd[name][idx] if idx >= 0 else small[name]
            return t[None] if name in leading else t
        return join(name, idx, name in big_lead)

    outs = [loss, g["x"][None]]
    for idx in (-1, 0, 1, 2):
        outs += [leaf(n, idx) for n in order]
    return tuple(outs)
```

```python
import functools

import jax
import jax.numpy as jnp
from jax import lax
from jax.experimental import pallas as pl
from jax.experimental.pallas import tpu as pltpu

F32 = jnp.float32
BF16 = jnp.bfloat16

D_MODEL = 1024
N_META = 16
CHUNK = 128
PAD_FRONT = (-N_META) % CHUNK
RET_HEADS = 4
RET_QK_DIM = 128
RET_V_DIM = 256
RET_QK_W = RET_HEADS * RET_QK_DIM
RET_V_W = RET_HEADS * RET_V_DIM
CONV_WIDTH = 31
CONV_HALO = 32
RET_DECAY_OFFSET = 5.0
ROPE_BASE = 10000.0
SB_HEADS = 16
SB_HEAD_DIM = 64
D_FF = 4 * D_MODEL
EPS = 1e-6
ADAM_LR = 0.001
ADAM_B1 = 0.9
ADAM_B2 = 0.999
ADAM_EPS = 1e-08
ADAM_WD = 0.01
ADAM_STEP = 10

N_CHIPS = 4
N_DEV = 8
VMEM_LIMIT = 56 * 1024 * 1024
MESH = pl.DeviceIdType.MESH
ANY = pl.BlockSpec(memory_space=pl.ANY)


def _params(*sem):
    return pltpu.CompilerParams(dimension_semantics=sem, vmem_limit_bytes=VMEM_LIMIT)


def _pick(n, cands):
    for c in cands:
        if n % c == 0:
            return c
    return n


def _sigmoid(x):
    return 1.0 / (1.0 + jnp.exp(-x))


def _dot(a, b):
    return lax.dot_general(a, b, (((1,), (0,)), ((), ())), preferred_element_type=F32)


def _dot_nt(a, b):
    return lax.dot_general(a, b, (((1,), (1,)), ((), ())), preferred_element_type=F32)


def _dot_tn(a, b):
    return lax.dot_general(a, b, (((0,), (0,)), ((), ())), preferred_element_type=F32)


def _split_dot(x, m):
    hi = x.astype(BF16)
    lo = (x - hi.astype(F32)).astype(BF16)
    return _dot(hi, m) + _dot(lo, m)


def _matmul(a, b, *, mode, out_dtypes, epilogue=None, extras=(), name, after=None):
    if mode == "nn":
        (m, k), (k2, n) = a.shape, b.shape
    elif mode == "nt":
        (m, k), (n, k2) = a.shape, b.shape
    else:
        (k, m), (k2, n) = a.shape, b.shape
    assert k == k2, (a.shape, b.shape, mode)
    tm = _pick(m, (1056, 1024, 768, 512, 384, 256, 128, 96))
    tn = _pick(n, (1024, 768, 512, 256, 128))
    tk = _pick(k, (1056, 1024, 768, 512, 384, 256, 128, 96))
    nk = k // tk
    dot = {"nn": _dot, "nt": _dot_nt, "tn": _dot_tn}[mode]
    n_extra, n_out = len(extras), len(out_dtypes)
    n_after = 0 if after is None else 1
    if epilogue is None:
        epilogue = lambda acc: (acc,)

    def body(a_ref, b_ref, *rest):
        extra_refs = rest[:n_extra]
        out_refs = rest[n_extra + n_after:n_extra + n_after + n_out]
        part = dot(a_ref[...].astype(BF16), b_ref[...].astype(BF16))

        def finish(acc):
            res = epilogue(acc, *[r[...] for r in extra_refs])
            for o_ref, r in zip(out_refs, res):
                o_ref[...] = r.astype(o_ref.dtype)

        if nk == 1:
            finish(part)
        else:
            acc_ref = rest[-1]
            kk = pl.program_id(2)

            @pl.when(kk == 0)
            def _():
                acc_ref[...] = part

            @pl.when(kk > 0)
            def _():
                acc_ref[...] += part

            @pl.when(kk == nk - 1)
            def _():
                finish(acc_ref[...])

    if mode == "nn":
        a_spec = pl.BlockSpec((tm, tk), lambda i, j, kk: (i, kk))
        b_spec = pl.BlockSpec((tk, tn), lambda i, j, kk: (kk, j))
    elif mode == "nt":
        a_spec = pl.BlockSpec((tm, tk), lambda i, j, kk: (i, kk))
        b_spec = pl.BlockSpec((tn, tk), lambda i, j, kk: (j, kk))
    else:
        a_spec = pl.BlockSpec((tk, tm), lambda i, j, kk: (kk, i))
        b_spec = pl.BlockSpec((tk, tn), lambda i, j, kk: (kk, j))
    o_spec = pl.BlockSpec((tm, tn), lambda i, j, kk: (i, j))
    outs = pl.pallas_call(
        body,
        out_shape=[jax.ShapeDtypeStruct((m, n), dt) for dt in out_dtypes],
        grid=(m // tm, n // tn, nk),
        in_specs=[a_spec, b_spec] + [o_spec] * n_extra + [ANY] * n_after,
        out_specs=[o_spec] * n_out,
        scratch_shapes=[pltpu.VMEM((tm, tn), F32)] if nk > 1 else [],
        compiler_params=_params("parallel", "parallel", "arbitrary"),
        name=name,
    )(a, b, *extras, *([] if after is None else [after]))
    return outs[0] if n_out == 1 else outs


def _add_epilogue(acc, res):
    return (res + acc,)


def _rmsnorm_fwd(x, g, *, name):
    p, d = x.shape
    rows = _pick(p, (384, 128, 96))

    def body(x_ref, g_ref, o_ref):
        xv = x_ref[...]
        r = lax.rsqrt(jnp.mean(xv * xv, axis=-1, keepdims=True) + EPS)
        o_ref[...] = (xv * r * g_ref[...]).astype(o_ref.dtype)

    return pl.pallas_call(
        body,
        out_shape=jax.ShapeDtypeStruct((p, d), BF16),
        grid=(p // rows,),
        in_specs=[pl.BlockSpec((rows, d), lambda i: (i, 0)), pl.BlockSpec((1, d), lambda i: (0, 0))],
        out_specs=pl.BlockSpec((rows, d), lambda i: (i, 0)),
        compiler_params=_params("parallel"),
        name=name,
    )(x, g)


def _rmsnorm_bwd(x, g, dy, dres, *, name):
    p, d = x.shape
    rows = _pick(p, (384, 128, 96))

    def body(x_ref, g_ref, dy_ref, dres_ref, dx_ref, dg_ref):
        xv = x_ref[...]
        r = lax.rsqrt(jnp.mean(xv * xv, axis=-1, keepdims=True) + EPS)
        dyv = dy_ref[...]
        gdy = dyv * g_ref[...]
        proj = jnp.mean(xv * gdy, axis=-1, keepdims=True)
        dx_ref[...] = dres_ref[...] + r * gdy - xv * (r * r * r) * proj
        part = jnp.sum(dyv * xv * r, axis=0, keepdims=True)

        @pl.when(pl.program_id(0) == 0)
        def _():
            dg_ref[...] = part

        @pl.when(pl.program_id(0) > 0)
        def _():
            dg_ref[...] += part

    row_spec = pl.BlockSpec((rows, d), lambda i: (i, 0))
    vec_spec = pl.BlockSpec((1, d), lambda i: (0, 0))
    return pl.pallas_call(
        body,
        out_shape=[jax.ShapeDtypeStruct((p, d), F32), jax.ShapeDtypeStruct((1, d), F32)],
        grid=(p // rows,),
        in_specs=[row_spec, vec_spec, row_spec, row_spec],
        out_specs=[row_spec, vec_spec],
        compiler_params=_params("arbitrary"),
        name=name,
    )(x, g, dy, dres)


def _mlp_fwd(h, g, w1, w2, *, name):
    hn = _rmsnorm_fwd(h, g, name=name + "_norm")

    def act(acc):
        r = jnp.maximum(acc, 0.0)
        return acc, r * r

    z, a2 = _matmul(hn, w1, mode="nn", out_dtypes=(F32, BF16), epilogue=act, name=name + "_up")
    out = _matmul(a2, w2, mode="nn", out_dtypes=(F32,), epilogue=_add_epilogue, extras=(h,), name=name + "_down")
    return out, (hn, z, a2)


def _mlp_bwd(h, g, w1, w2, saved, dout, *, name, after=None):
    hn, z, a2 = saved

    def dact(acc, zt):
        return (acc * (2.0 * jnp.maximum(zt, 0.0)),)

    dz = _matmul(dout, w2, mode="nt", out_dtypes=(BF16,), epilogue=dact, extras=(z,), name=name + "_dz", after=after)
    dw2 = _matmul(a2, dout, mode="tn", out_dtypes=(F32,), name=name + "_dw2")
    dw1 = _matmul(hn, dz, mode="tn", out_dtypes=(F32,), name=name + "_dw1")
    dhn = _matmul(dz, w1, mode="nt", out_dtypes=(F32,), name=name + "_dhn")
    dh, dg = _rmsnorm_bwd(h, g, dhn, dout, name=name + "_dnorm")
    return dh, dg, dw1, dw2


def _retention_tables(p):
    half = RET_QK_DIM // 2
    inv_freq = ROPE_BASE ** (-jnp.arange(half, dtype=F32) / half)
    ang = jnp.arange(p, dtype=F32)[:, None] * inv_freq[None, :]
    cos, sin = jnp.cos(ang), jnp.sin(ang)
    cosf = jnp.concatenate([cos, cos], axis=1)
    sins = jnp.concatenate([-sin, sin], axis=1)
    log_g = jnp.log1p(-jnp.exp2(-RET_DECAY_OFFSET - jnp.arange(RET_HEADS, dtype=F32)))
    idx = jnp.arange(CHUNK, dtype=F32)
    diff = idx[:, None] - idx[None, :]
    inner = jnp.where(diff[None] >= 0, jnp.exp(jnp.maximum(diff, 0.0)[None] * log_g[:, None, None]), 0.0)
    kdec = jnp.exp((CHUNK - 1 - idx)[None, :] * log_g[:, None])
    qdec = jnp.exp((idx + 1.0)[None, :] * log_g[:, None])
    cdec = jnp.exp(CHUNK * log_g)
    kdec = jnp.broadcast_to(kdec[:, :, None], (RET_HEADS, CHUNK, RET_QK_DIM))
    qdec = jnp.broadcast_to(qdec[:, :, None], (RET_HEADS, CHUNK, RET_QK_DIM))
    cdec = jnp.broadcast_to(cdec[:, None, None], (RET_HEADS, RET_QK_DIM, RET_V_DIM))
    return cosf, sins, inner, kdec, qdec, cdec


def _rot(x, cosf, sins):
    return x * cosf + pltpu.roll(x, RET_QK_DIM // 2, 1) * sins


def _rot_bwd(dy, cosf, sins):
    return dy * cosf + pltpu.roll(dy * sins, RET_QK_DIM // 2, 1)


def _ret_in_specs(chunk_of):
    nh = RET_HEADS
    q_spec = pl.BlockSpec((CHUNK, RET_QK_DIM), lambda h, s: (chunk_of(s), h))
    k_spec = pl.BlockSpec((CHUNK, RET_QK_DIM), lambda h, s: (chunk_of(s), nh + h))
    v_spec = pl.BlockSpec((CHUNK, RET_V_DIM), lambda h, s: (chunk_of(s), nh + h))
    g_spec = pl.BlockSpec((CHUNK, RET_V_DIM), lambda h, s: (chunk_of(s), 2 * nh + h))
    rope_spec = pl.BlockSpec((CHUNK, RET_QK_DIM), lambda h, s: (chunk_of(s), 0))
    head_sq = pl.BlockSpec((None, CHUNK, CHUNK), lambda h, s: (h, 0, 0))
    head_qk = pl.BlockSpec((None, CHUNK, RET_QK_DIM), lambda h, s: (h, 0, 0))
    head_st = pl.BlockSpec((None, RET_QK_DIM, RET_V_DIM), lambda h, s: (h, 0, 0))
    gam_spec = pl.BlockSpec((None, 1, RET_V_DIM), lambda h, s: (h, 0, 0))
    return [q_spec, k_spec, v_spec, g_spec, rope_spec, rope_spec, head_sq, head_qk, head_qk, head_st, gam_spec]


def _retention_fwd(proj, gn_g, tables, *, name):
    p = proj.shape[0]
    n_chunks = p // CHUNK
    scale = RET_QK_DIM ** -0.5

    def body(q_ref, k_ref, v_ref, g_ref, cos_ref, sin_ref, inner_ref, kdec_ref, qdec_ref, cdec_ref, gam_ref,
             og_ref, opre_ref, sprev_ref, s_scr):
        @pl.when(pl.program_id(1) == 0)
        def _():
            s_scr[...] = jnp.zeros_like(s_scr)

        cosf, sins = cos_ref[...], sin_ref[...]
        qr = _rot(q_ref[...], cosf, sins)
        kr = _rot(k_ref[...], cosf, sins) * scale
        vb = v_ref[...].astype(BF16)
        scores = _dot_nt(qr.astype(BF16), kr.astype(BF16)) * inner_ref[...]
        state = s_scr[...]
        sprev_ref[...] = state
        o = _dot(scores.astype(BF16), vb) + _dot((qr * qdec_ref[...]).astype(BF16), state.astype(BF16))
        kd = kr * kdec_ref[...]
        s_scr[...] = cdec_ref[...] * state + _dot(kd.T.astype(BF16), vb)
        opre_ref[...] = o
        mu = jnp.mean(o, axis=-1, keepdims=True)
        oc = o - mu
        var = jnp.mean(oc * oc, axis=-1, keepdims=True)
        on = oc * lax.rsqrt(var + EPS) * gam_ref[...]
        gv = g_ref[...]
        og_ref[...] = (gv * _sigmoid(gv) * on).astype(og_ref.dtype)

    chunk_of = lambda s: s
    out_v = pl.BlockSpec((CHUNK, RET_V_DIM), lambda h, s: (s, h))
    return pl.pallas_call(
        body,
        out_shape=[
            jax.ShapeDtypeStruct((p, RET_V_W), BF16),
            jax.ShapeDtypeStruct((p, RET_V_W), F32),
            jax.ShapeDtypeStruct((RET_HEADS, n_chunks, RET_QK_DIM, RET_V_DIM), F32),
        ],
        grid=(RET_HEADS, n_chunks),
        in_specs=_ret_in_specs(chunk_of),
        out_specs=[out_v, out_v, pl.BlockSpec((None, None, RET_QK_DIM, RET_V_DIM), lambda h, s: (h, s, 0, 0))],
        scratch_shapes=[pltpu.VMEM((RET_QK_DIM, RET_V_DIM), F32)],
        compiler_params=_params("parallel", "arbitrary"),
        name=name,
    )(proj, proj, proj, proj, *tables, gn_g.reshape(RET_HEADS, 1, RET_V_DIM))


def _retention_bwd(proj, gn_g, tables, opre, sprev, dog, *, name):
    p = proj.shape[0]
    n_chunks = p // CHUNK
    scale = RET_QK_DIM ** -0.5

    def body(q_ref, k_ref, v_ref, g_ref, cos_ref, sin_ref, inner_ref, kdec_ref, qdec_ref, cdec_ref, gam_ref,
             opre_ref, sprev_ref, dog_ref, dq_ref, dk_ref, dv_ref, dg_ref, dgam_ref, ds_scr):
        first = pl.program_id(1) == 0

        @pl.when(first)
        def _():
            ds_scr[...] = jnp.zeros_like(ds_scr)

        cosf, sins = cos_ref[...], sin_ref[...]
        qr = _rot(q_ref[...], cosf, sins)
        kr = _rot(k_ref[...], cosf, sins) * scale
        qb, kb = qr.astype(BF16), kr.astype(BF16)
        vb = v_ref[...].astype(BF16)
        inner = inner_ref[...]
        qdec, kdec = qdec_ref[...], kdec_ref[...]
        state_b = sprev_ref[...].astype(BF16)
        o = opre_ref[...]
        mu = jnp.mean(o, axis=-1, keepdims=True)
        oc = o - mu
        rstd = lax.rsqrt(jnp.mean(oc * oc, axis=-1, keepdims=True) + EPS)
        xhat = oc * rstd
        gam = gam_ref[...]
        on = xhat * gam
        gv = g_ref[...]
        sig = _sigmoid(gv)
        dogv = dog_ref[...]
        dg_ref[...] = (dogv * on * sig * (1.0 + gv * (1.0 - sig))).astype(dg_ref.dtype)
        don = dogv * gv * sig
        dgam_part = jnp.sum(don * xhat, axis=0, keepdims=True)

        @pl.when(first)
        def _():
            dgam_ref[...] = dgam_part

        @pl.when(jnp.logical_not(first))
        def _():
            dgam_ref[...] += dgam_part

        dxhat = don * gam
        do = rstd * (dxhat - jnp.mean(dxhat, axis=-1, keepdims=True)
                     - xhat * jnp.mean(dxhat * xhat, axis=-1, keepdims=True))
        dob = do.astype(BF16)
        scores_b = (_dot_nt(qb, kb) * inner).astype(BF16)
        da = (_dot_nt(dob, vb) * inner).astype(BF16)
        dv = _dot(scores_b.astype(F32).T.astype(BF16), dob)
        dqr = _dot(da, kb)
        dkr = _dot(da.astype(F32).T.astype(BF16), qb)
        dqr += _dot_nt(dob, state_b) * qdec
        ds_local = _dot((qr * qdec).T.astype(BF16), dob)
        gstate = ds_scr[...]
        gb = gstate.astype(BF16)
        kd_b = (kr * kdec).astype(BF16)
        dkr += _dot_nt(vb, gb) * kdec
        dv += _dot(kd_b, gb)
        ds_scr[...] = cdec_ref[...] * gstate + ds_local
        dq_ref[...] = _rot_bwd(dqr, cosf, sins).astype(dq_ref.dtype)
        dk_ref[...] = _rot_bwd(dkr * scale, cosf, sins).astype(dk_ref.dtype)
        dv_ref[...] = dv.astype(dv_ref.dtype)

    chunk_of = lambda s: n_chunks - 1 - s
    blk_v = pl.BlockSpec((CHUNK, RET_V_DIM), lambda h, s: (chunk_of(s), h))
    blk_qk = pl.BlockSpec((CHUNK, RET_QK_DIM), lambda h, s: (chunk_of(s), h))
    st_spec = pl.BlockSpec((None, None, RET_QK_DIM, RET_V_DIM), lambda h, s: (h, chunk_of(s), 0, 0))
    return pl.pallas_call(
        body,
        out_shape=[
            jax.ShapeDtypeStruct((p, RET_QK_W), BF16),
            jax.ShapeDtypeStruct((p, RET_QK_W), BF16),
            jax.ShapeDtypeStruct((p, RET_V_W), BF16),
            jax.ShapeDtypeStruct((p, RET_V_W), BF16),
            jax.ShapeDtypeStruct((RET_HEADS, 1, RET_V_DIM), F32),
        ],
        grid=(RET_HEADS, n_chunks),
        in_specs=_ret_in_specs(chunk_of) + [blk_v, st_spec, blk_v],
        out_specs=[blk_qk, blk_qk, blk_v, blk_v, pl.BlockSpec((None, 1, RET_V_DIM), lambda h, s: (h, 0, 0))],
        scratch_shapes=[pltpu.VMEM((RET_QK_DIM, RET_V_DIM), F32)],
        compiler_params=_params("parallel", "arbitrary"),
        name=name,
    )(proj, proj, proj, proj, *tables, gn_g.reshape(RET_HEADS, 1, RET_V_DIM), opre, sprev, dog)


def _conv_rows(p):
    return _pick(p, (384, 128))


CONV_CHUNK = 32
F32_SUBLANES = 8


def _shifted_rows(rows):
    return rows + CONV_HALO - F32_SUBLANES


def _shifted_copies(src_scr, sh_scr, n_rows):
    for s in range(1, F32_SUBLANES):
        sh_scr[s - 1] = src_scr[s:s + n_rows, :]


def _tap_rows(src_scr, sh_scr, off, r0, n):
    q, s = divmod(off, F32_SUBLANES)
    ref = src_scr if s == 0 else sh_scr.at[s - 1]
    return ref[pl.ds(pl.multiple_of(r0 + F32_SUBLANES * q, F32_SUBLANES), n), :]


def _ln_stats(y):
    mu = jnp.mean(y, axis=-1, keepdims=True)
    yc = y - mu
    rstd = lax.rsqrt(jnp.mean(yc * yc, axis=-1, keepdims=True) + EPS)
    return yc * rstd, rstd


def _conv_fwd(proj, conv_w, conv_b, ln_g, ln_b, *, name):
    p = proj.shape[0]
    c = D_MODEL
    rows = _conv_rows(p)
    hpb = rows // CONV_HALO
    a_col, gate_col = (2 * RET_QK_W + 2 * RET_V_W) // c, (2 * RET_QK_W + 2 * RET_V_W) // c + 1

    def body(a_ref, gate_ref, ah_ref, gateh_ref, w_ref, b_ref, lg_ref, lb_ref, c_ref, y_ref, hdn_scr, sh_scr):
        i = pl.program_id(0)
        hdn_scr[0:CONV_HALO, :] = ah_ref[...] * _sigmoid(gateh_ref[...])
        hdn_scr[CONV_HALO:, :] = a_ref[...] * _sigmoid(gate_ref[...])
        _shifted_copies(hdn_scr, sh_scr, _shifted_rows(rows))

        def chunk(j, _):
            r0 = pl.multiple_of(j * CONV_CHUNK, CONV_CHUNK)
            acc = jnp.zeros((CONV_CHUNK, c), F32)
            for w in range(CONV_WIDTH):
                off = CONV_HALO - (CONV_WIDTH - 1) + w
                acc += _tap_rows(hdn_scr, sh_scr, off, r0, CONV_CHUNK) * w_ref[w:w + 1, :]
            y_ref[pl.ds(r0, CONV_CHUNK), :] = acc + b_ref[...]
            return 0

        lax.fori_loop(0, rows // CONV_CHUNK, chunk, 0)
        y = y_ref[...]
        yhat, _ = _ln_stats(y)
        ln = yhat * lg_ref[...] + lb_ref[...]
        row = i * rows + lax.broadcasted_iota(jnp.int32, (rows, 1), 0)
        c_ref[...] = jnp.where(row >= PAD_FRONT, ln * _sigmoid(ln), 0.0).astype(c_ref.dtype)

    halo_idx = lambda i: jnp.maximum(i * hpb - 1, 0)
    vec = pl.BlockSpec((1, c), lambda i: (0, 0))
    return pl.pallas_call(
        body,
        out_shape=[jax.ShapeDtypeStruct((p, c), BF16), jax.ShapeDtypeStruct((p, c), F32)],
        grid=(p // rows,),
        in_specs=[
            pl.BlockSpec((rows, c), lambda i: (i, a_col)),
            pl.BlockSpec((rows, c), lambda i: (i, gate_col)),
            pl.BlockSpec((CONV_HALO, c), lambda i: (halo_idx(i), a_col)),
            pl.BlockSpec((CONV_HALO, c), lambda i: (halo_idx(i), gate_col)),
            pl.BlockSpec((CONV_WIDTH, c), lambda i: (0, 0)),
            vec, vec, vec,
        ],
        out_specs=[pl.BlockSpec((rows, c), lambda i: (i, 0)), pl.BlockSpec((rows, c), lambda i: (i, 0))],
        scratch_shapes=[pltpu.VMEM((CONV_HALO + rows, c), F32),
                        pltpu.VMEM((F32_SUBLANES - 1, _shifted_rows(rows), c), F32)],
        compiler_params=_params("parallel"),
        name=name,
    )(proj, proj, proj, proj, conv_w, conv_b, ln_g, ln_b)


def _conv_bwd(proj, conv_w, ln_g, ln_b, y, dcat, *, name):
    p = proj.shape[0]
    c = D_MODEL
    rows = _conv_rows(p)
    hpb = rows // CONV_HALO
    n_blocks = p // rows
    a_col, gate_col = (2 * RET_QK_W + 2 * RET_V_W) // c, (2 * RET_QK_W + 2 * RET_V_W) // c + 1

    def body(a_ref, gate_ref, ah_ref, gateh_ref, w_ref, lg_ref, lb_ref, y_ref, yh_ref, dc_ref, dch_ref,
             da_ref, dgate_ref, dw_ref, db_ref, dlg_ref, dlb_ref, hdn_scr, dy_scr, hdn_sh, dy_sh):
        i = pl.program_id(0)
        lg, lb = lg_ref[...], lb_ref[...]

        def ln_bwd(yv, dcv):
            yhat, rstd = _ln_stats(yv)
            ln = yhat * lg + lb
            sig = _sigmoid(ln)
            dln = dcv * sig * (1.0 + ln * (1.0 - sig))
            dyhat = dln * lg
            dyv = rstd * (dyhat - jnp.mean(dyhat, axis=-1, keepdims=True)
                          - yhat * jnp.mean(dyhat * yhat, axis=-1, keepdims=True))
            return dyv, dln, yhat

        row = i * rows + lax.broadcasted_iota(jnp.int32, (rows, 1), 0)
        dy, dln, yhat = ln_bwd(y_ref[...], jnp.where(row >= PAD_FRONT, dc_ref[...], 0.0))
        dy_halo, _, _ = ln_bwd(yh_ref[...], dch_ref[...])
        dy_scr[0:rows, :] = dy
        dy_scr[rows:, :] = jnp.where(i == n_blocks - 1, 0.0, dy_halo)
        hdn_scr[0:CONV_HALO, :] = ah_ref[...] * _sigmoid(gateh_ref[...])
        hdn_scr[CONV_HALO:, :] = a_ref[...] * _sigmoid(gate_ref[...])
        _shifted_copies(hdn_scr, hdn_sh, _shifted_rows(rows))
        _shifted_copies(dy_scr, dy_sh, _shifted_rows(rows))

        @pl.when(i == 0)
        def _():
            dw_ref[...] = jnp.zeros_like(dw_ref)
            db_ref[...] = jnp.zeros_like(db_ref)
            dlg_ref[...] = jnp.zeros_like(dlg_ref)
            dlb_ref[...] = jnp.zeros_like(dlb_ref)

        n_chunks = rows // CONV_CHUNK

        def input_grad(j, _):
            r0 = pl.multiple_of(j * CONV_CHUNK, CONV_CHUNK)
            dhdn = jnp.zeros((CONV_CHUNK, c), F32)
            for w in range(CONV_WIDTH):
                dhdn += _tap_rows(dy_scr, dy_sh, CONV_WIDTH - 1 - w, r0, CONV_CHUNK) * w_ref[w:w + 1, :]
            here = pl.ds(r0, CONV_CHUNK)
            sig_gate = _sigmoid(gate_ref[here, :])
            da_ref[here, :] = (dhdn * sig_gate).astype(da_ref.dtype)
            dgate_ref[here, :] = (dhdn * a_ref[here, :] * sig_gate * (1.0 - sig_gate)).astype(dgate_ref.dtype)
            return 0

        lax.fori_loop(0, n_chunks, input_grad, 0)
        for w in range(CONV_WIDTH):
            off = CONV_HALO - (CONV_WIDTH - 1) + w

            def tap_grad(j, acc, off=off):
                r0 = pl.multiple_of(j * CONV_CHUNK, CONV_CHUNK)
                prod = dy_scr[pl.ds(r0, CONV_CHUNK), :] * _tap_rows(hdn_scr, hdn_sh, off, r0, CONV_CHUNK)
                for k in range(CONV_CHUNK // F32_SUBLANES):
                    acc = acc + prod[k * F32_SUBLANES:(k + 1) * F32_SUBLANES]
                return acc

            acc = lax.fori_loop(0, n_chunks, tap_grad, jnp.zeros((F32_SUBLANES, c), F32))
            dw_ref[w:w + 1, :] += jnp.sum(acc, axis=0, keepdims=True)
        db_ref[...] += jnp.sum(dy, axis=0, keepdims=True)
        dlg_ref[...] += jnp.sum(dln * yhat, axis=0, keepdims=True)
        dlb_ref[...] += jnp.sum(dln, axis=0, keepdims=True)

    prev_halo = lambda i: jnp.maximum(i * hpb - 1, 0)
    next_halo = lambda i: jnp.minimum((i + 1) * hpb, p // CONV_HALO - 1)
    vec = pl.BlockSpec((1, c), lambda i: (0, 0))
    blk = lambda col: pl.BlockSpec((rows, c), lambda i: (i, col))
    outs = pl.pallas_call(
        body,
        out_shape=[
            jax.ShapeDtypeStruct((p, c), BF16),
            jax.ShapeDtypeStruct((p, c), BF16),
            jax.ShapeDtypeStruct((CONV_WIDTH + 1, c), F32),
            jax.ShapeDtypeStruct((1, c), F32),
            jax.ShapeDtypeStruct((1, c), F32),
            jax.ShapeDtypeStruct((1, c), F32),
        ],
        grid=(n_blocks,),
        in_specs=[
            blk(a_col), blk(gate_col),
            pl.BlockSpec((CONV_HALO, c), lambda i: (prev_halo(i), a_col)),
            pl.BlockSpec((CONV_HALO, c), lambda i: (prev_halo(i), gate_col)),
            pl.BlockSpec((CONV_WIDTH, c), lambda i: (0, 0)),
            vec, vec,
            blk(0),
            pl.BlockSpec((CONV_HALO, c), lambda i: (next_halo(i), 0)),
            blk(1),
            pl.BlockSpec((CONV_HALO, c), lambda i: (next_halo(i), 1)),
        ],
        out_specs=[blk(0), blk(0), pl.BlockSpec((CONV_WIDTH + 1, c), lambda i: (0, 0)), vec, vec, vec],
        scratch_shapes=[pltpu.VMEM((CONV_HALO + rows, c), F32), pltpu.VMEM((rows + CONV_HALO, c), F32),
                        pltpu.VMEM((F32_SUBLANES - 1, _shifted_rows(rows), c), F32),
                        pltpu.VMEM((F32_SUBLANES - 1, _shifted_rows(rows), c), F32)],
        compiler_params=_params("arbitrary"),
        name=name,
    )(proj, proj, proj, proj, conv_w, ln_g, ln_b, y, y, dcat, dcat)
    da, dgate, dw, db, dlg, dlb = outs
    return da, dgate, dw[:CONV_WIDTH], db, dlg, dlb


LANES = 128


def _group_matrix():
    r = jnp.arange(LANES)[:, None] // SB_HEAD_DIM
    c = jnp.arange(LANES)[None, :] // SB_HEAD_DIM
    return (r == c).astype(BF16)


def _head_sums(v, gm):
    return jnp.concatenate([_split_dot(v[:, j * LANES:(j + 1) * LANES], gm) for j in range(v.shape[1] // LANES)], axis=1)


def _qknorm_fwd(qkv, qg, kg, *, name):
    p = qkv.shape[0]
    d = D_MODEL
    rows = _pick(p, (384, 128, 96))

    def body(q_ref, k_ref, v_ref, qg_ref, kg_ref, gm_ref, qn_ref, kn_ref, vb_ref):
        gm = gm_ref[...]

        def norm(x, g):
            ms = _head_sums(x * x, gm) * (1.0 / SB_HEAD_DIM)
            return x * lax.rsqrt(ms + EPS) * g

        qn_ref[...] = norm(q_ref[...], qg_ref[...]).astype(BF16)
        kn_ref[...] = norm(k_ref[...], kg_ref[...]).astype(BF16)
        vb_ref[...] = v_ref[...].astype(BF16)

    blk = lambda col: pl.BlockSpec((rows, d), lambda i: (i, col))
    vec = pl.BlockSpec((1, d), lambda i: (0, 0))
    return pl.pallas_call(
        body,
        out_shape=[jax.ShapeDtypeStruct((p, d), BF16)] * 3,
        grid=(p // rows,),
        in_specs=[blk(0), blk(1), blk(2), vec, vec, pl.BlockSpec((LANES, LANES), lambda i: (0, 0))],
        out_specs=[blk(0)] * 3,
        compiler_params=_params("parallel"),
        name=name,
    )(qkv, qkv, qkv, qg, kg, _group_matrix())


def _qknorm_bwd(qkv, qg, kg, dqn, dkn, dv, *, name):
    p = qkv.shape[0]
    d = D_MODEL
    rows = _pick(p, (384, 128, 96))

    def body(q_ref, k_ref, qg_ref, kg_ref, gm_ref, dqn_ref, dkn_ref, dv_ref, dqkv_ref, dqg_ref, dkg_ref):
        gm = gm_ref[...]

        def bwd(x, g, dy):
            ms = _head_sums(x * x, gm) * (1.0 / SB_HEAD_DIM)
            r = lax.rsqrt(ms + EPS)
            gdy = dy * g
            proj = _head_sums(x * gdy, gm) * (1.0 / SB_HEAD_DIM)
            return r * gdy - x * (r * r * r) * proj, jnp.sum(dy * x * r, axis=0, keepdims=True)

        dq, dqg = bwd(q_ref[...], qg_ref[...], dqn_ref[...])
        dk, dkg = bwd(k_ref[...], kg_ref[...], dkn_ref[...])
        dqkv_ref[:, 0:d] = dq.astype(BF16)
        dqkv_ref[:, d:2 * d] = dk.astype(BF16)
        dqkv_ref[:, 2 * d:3 * d] = dv_ref[...].astype(BF16)

        @pl.when(pl.program_id(0) == 0)
        def _():
            dqg_ref[...] = dqg
            dkg_ref[...] = dkg

        @pl.when(pl.program_id(0) > 0)
        def _():
            dqg_ref[...] += dqg
            dkg_ref[...] += dkg

    blk = lambda col: pl.BlockSpec((rows, d), lambda i: (i, col))
    vec = pl.BlockSpec((1, d), lambda i: (0, 0))
    return pl.pallas_call(
        body,
        out_shape=[jax.ShapeDtypeStruct((p, 3 * d), BF16), jax.ShapeDtypeStruct((1, d), F32),
                   jax.ShapeDtypeStruct((1, d), F32)],
        grid=(p // rows,),
        in_specs=[blk(0), blk(1), vec, vec, pl.BlockSpec((LANES, LANES), lambda i: (0, 0)), blk(0), blk(0), blk(0)],
        out_specs=[pl.BlockSpec((rows, 3 * d), lambda i: (i, 0)), vec, vec],
        compiler_params=_params("arbitrary"),
        name=name,
    )(qkv, qkv, qg, kg, _group_matrix(), dqn, dkn, dv)


SB_PAIR = 2 * SB_HEAD_DIM
SB_GROUP = 8
SB_MASKED = -1e30


def _sb_consts():
    lane = lax.broadcasted_iota(jnp.int32, (CHUNK, SB_PAIR), 1)
    r = lax.broadcasted_iota(jnp.int32, (CHUNK, CHUNK), 0)
    c = lax.broadcasted_iota(jnp.int32, (CHUNK, CHUNK), 1)
    lo = (lane < SB_HEAD_DIM).astype(F32).astype(BF16)
    ones = jnp.ones((CHUNK, CHUNK), BF16)
    twice = lambda m: jnp.concatenate([jnp.concatenate([m, ones], axis=1)] * 2, axis=0)
    later, earlier = twice((r > c).astype(BF16)), twice((r < c).astype(BF16))
    not_before = (c >= r).astype(F32) * SB_MASKED
    padding = (c < PAD_FRONT).astype(F32) * SB_MASKED
    return (lo, 1.0 - lo), c, later, earlier, not_before, padding


def _sb_halves(t, head_lanes):
    return t * head_lanes[0], t * head_lanes[1]


def _sb_logits(qh, kg, biases):
    z = _dot_nt(qh, kg)
    tiles = []
    for b, bias in enumerate(biases):
        zt = z[:, b * CHUNK:(b + 1) * CHUNK]
        if bias is not None:
            zt = zt + bias
        ls_pos = jnp.minimum(zt, 0.0) - jnp.log(1.0 + jnp.exp(-jnp.abs(zt)))
        tiles.append((ls_pos, ls_pos - zt))
    return tiles


def _sb_block_sums(tiles, m):
    st = jnp.concatenate(tiles, axis=0)
    hi = st.astype(BF16)
    lo = (st - hi.astype(F32)).astype(BF16)
    tot = _dot(jnp.concatenate([hi, lo], axis=1), m)
    return [(tot[i * CHUNK:(i + 1) * CHUNK, 0:CHUNK], tot[i * CHUNK:(i + 1) * CHUNK, CHUNK:2 * CHUNK])
            for i in range(len(tiles))]


def _sb_plan(qi, padding, not_before):
    top = lax.div(qi, SB_GROUP)
    size = qi - SB_GROUP * top + 1

    def masks(n_b):
        pad_if_first = padding * (top == 0).astype(F32)
        m = [None] * n_b
        m[n_b - 1] = not_before
        m[0] = pad_if_first if m[0] is None else m[0] + pad_if_first
        return m

    return top, size, masks


def _once_if(cond, fn, carry):
    return lax.fori_loop(0, jnp.where(cond, 1, 0), lambda s, cr: fn(cr), carry)


def _sb_head_rows(tg, lanes, n_b):
    return jnp.concatenate([tg[b * CHUNK:(b + 1) * CHUNK] * lanes for b in range(n_b)], axis=0)


def _sb_fwd(qn, kn, vb, *, name):
    p = qn.shape[0]
    n_blocks = p // CHUNK
    n_pairs = SB_HEADS // 2
    scale = SB_HEAD_DIM ** -0.5

    def body(q_ref, k_ref, v_ref, o_ref, car_ref):
        head_lanes, c, later, _, not_before, padding = _sb_consts()

        def q_block(qi, _):
            rows = pl.ds(pl.multiple_of(qi * CHUNK, CHUNK), CHUNK)
            qh = _sb_halves(q_ref[rows, :], head_lanes)
            qs = (qh[0] * scale, qh[1] * scale)

            def blocks(kb0, biases, carry):
                n_b = len(biases)
                acc, run0, run1, sav0, sav1 = carry
                krows = pl.ds(pl.multiple_of(kb0 * CHUNK, CHUNK), n_b * CHUNK)
                kg, vg = k_ref[krows, :], v_ref[krows, :]
                tiles = [_sb_logits(qs[h], kg, biases) for h in range(2)]
                sums = [_sb_block_sums([log_keep for _, log_keep in tiles[h]], later) for h in range(2)]
                cols = [(c == kb0 + b).astype(F32) for b in range(n_b)]
                runs, savs = [run0, run1], [sav0, sav1]
                for h in range(2):
                    ws = [None] * n_b
                    for b in reversed(range(n_b)):
                        after, row_sum = sums[h][b]
                        ws[b] = jnp.exp(tiles[h][b][0] + after + runs[h]).astype(BF16)
                        savs[h] = savs[h] + cols[b] * runs[h]
                        runs[h] = runs[h] + row_sum
                    acc = acc + _dot(jnp.concatenate(ws, axis=1), _sb_head_rows(vg, head_lanes[h], n_b))
                return acc, runs[0], runs[1], savs[0], savs[1]

            zt = qh[0].astype(F32) * 0.0
            top, size, masks = _sb_plan(qi, padding, not_before)
            carry = (zt, zt, zt, zt, zt)
            for n_b in range(1, SB_GROUP + 1):
                carry = _once_if(size == n_b, lambda cr, n_b=n_b: blocks(SB_GROUP * top, masks(n_b), cr), carry)
            carry = lax.fori_loop(0, jnp.maximum(top - 1, 0),
                                  lambda it, cr: blocks(SB_GROUP * (top - 1 - it), [None] * SB_GROUP, cr), carry)
            carry = _once_if(top > 0, functools.partial(blocks, 0, [padding] + [None] * (SB_GROUP - 1)), carry)
            acc, _, _, sav0, sav1 = carry
            o_ref[rows, :] = acc.astype(o_ref.dtype)
            car_ref[rows, 0:CHUNK] = sav0
            car_ref[rows, CHUNK:2 * CHUNK] = sav1
            return 0

        lax.fori_loop(0, n_blocks, q_block, 0)

    col = pl.BlockSpec((p, SB_PAIR), lambda g: (0, g))
    return pl.pallas_call(
        body,
        out_shape=[jax.ShapeDtypeStruct((p, D_MODEL), BF16), jax.ShapeDtypeStruct((p, n_pairs * 2 * CHUNK), F32)],
        grid=(n_pairs,),
        in_specs=[col, col, col],
        out_specs=[col, pl.BlockSpec((p, 2 * CHUNK), lambda g: (0, g))],
        compiler_params=_params("parallel"),
        name=name,
    )(qn, kn, vb)


def _sb_bwd(qn, kn, vb, carries, do, *, name):
    p = qn.shape[0]
    n_blocks = p // CHUNK
    n_pairs = SB_HEADS // 2
    scale = SB_HEAD_DIM ** -0.5

    def body(q_ref, k_ref, v_ref, car_ref, do_ref, dq_ref, dk_ref, dv_ref):
        head_lanes, c, later, earlier, not_before, padding = _sb_consts()
        dk_ref[...] = jnp.zeros_like(dk_ref)
        dv_ref[...] = jnp.zeros_like(dv_ref)

        def q_block(qi, _):
            rows = pl.ds(pl.multiple_of(qi * CHUNK, CHUNK), CHUNK)
            qh = _sb_halves(q_ref[rows, :], head_lanes)
            qs = (qh[0] * scale, qh[1] * scale)
            doh = _sb_halves(do_ref[rows, :].astype(BF16), head_lanes)
            do2, q2 = jnp.concatenate(doh, axis=0), jnp.concatenate(qs, axis=0)
            sav = (car_ref[rows, 0:CHUNK], car_ref[rows, CHUNK:2 * CHUNK])

            def blocks(kb0, biases, carry):
                n_b = len(biases)
                dq_acc, pre0, pre1 = carry
                krows = pl.ds(pl.multiple_of(kb0 * CHUNK, CHUNK), n_b * CHUNK)
                kg, vg = k_ref[krows, :], v_ref[krows, :]
                cols = [(c == kb0 + b).astype(F32) for b in range(n_b)]
                block = lambda t, b: t[:, b * CHUNK:(b + 1) * CHUNK]
                tiles = [_sb_logits(qs[h], kg, biases) for h in range(2)]
                afters = [_sb_block_sums([log_keep for _, log_keep in tiles[h]], later) for h in range(2)]
                dws = [_dot_nt(doh[h], vg) for h in range(2)]
                ws, es, befores = [], [], []
                for h in range(2):
                    runs = [jnp.sum(cols[b] * sav[h], axis=-1, keepdims=True) for b in range(n_b)]
                    ws.append([jnp.exp(tiles[h][b][0] + afters[h][b][0] + runs[b]) for b in range(n_b)])
                    es.append([ws[h][b] * block(dws[h], b) for b in range(n_b)])
                    befores.append(_sb_block_sums(es[h], earlier))
                pres = [pre0, pre1]
                dz2, w2 = [], []
                for h in range(2):
                    dzs = []
                    for b in range(n_b):
                        before, row_sum = befores[h][b]
                        sig = jnp.exp(tiles[h][b][0])
                        e = es[h][b]
                        dzs.append((e - (e + before + pres[h]) * sig).astype(BF16))
                        pres[h] = pres[h] + row_sum
                    dz2.append(jnp.concatenate(dzs, axis=1))
                    w2.append(jnp.concatenate([t.astype(BF16) for t in ws[h]], axis=1))
                    dq_acc = dq_acc + _dot(dz2[h], _sb_head_rows(kg, head_lanes[h], n_b))
                dv_ref[krows, :] += _dot_tn(jnp.concatenate(w2, axis=0), do2)
                dk_ref[krows, :] += _dot_tn(jnp.concatenate(dz2, axis=0), q2)
                return dq_acc, pres[0], pres[1]

            zt = qh[0].astype(F32) * 0.0
            top, size, masks = _sb_plan(qi, padding, not_before)
            carry = _once_if(top > 0, functools.partial(blocks, 0, [padding] + [None] * (SB_GROUP - 1)), (zt, zt, zt))
            carry = lax.fori_loop(1, top, lambda g, cr: blocks(SB_GROUP * g, [None] * SB_GROUP, cr), carry)
            for n_b in range(1, SB_GROUP + 1):
                carry = _once_if(size == n_b, lambda cr, n_b=n_b: blocks(SB_GROUP * top, masks(n_b), cr), carry)
            dq_acc, _, _ = carry
            dq_ref[rows, :] = dq_acc * scale
            return 0

        lax.fori_loop(0, n_blocks, q_block, 0)

    col = pl.BlockSpec((p, SB_PAIR), lambda g: (0, g))
    return pl.pallas_call(
        body,
        out_shape=[jax.ShapeDtypeStruct((p, D_MODEL), F32)] * 3,
        grid=(n_pairs,),
        in_specs=[col, col, col, pl.BlockSpec((p, 2 * CHUNK), lambda g: (0, g)), col],
        out_specs=[col, col, col],
        compiler_params=_params("parallel"),
        name=name,
    )(qn, kn, vb, carries, do)


def _loss_head(h, target, *, name):
    p, d = h.shape
    n_blocks = p // CHUNK

    def body(h_ref, t_ref, sq_ref, dh_ref):
        i = pl.program_id(0)

        @pl.when(i == 0)
        def _():
            sq_ref[...] = jnp.zeros_like(sq_ref)
            dh_ref[...] = jnp.zeros_like(dh_ref)

        @pl.when(i > 0)
        def _():
            err = h_ref[...] - t_ref[...]
            sq_ref[...] += jnp.sum(err * err)
            dh_ref[...] = err * (1.0 / d)

    return pl.pallas_call(
        body,
        out_shape=[jax.ShapeDtypeStruct((8, 128), F32), jax.ShapeDtypeStruct((p, d), F32)],
        grid=(n_blocks,),
        in_specs=[pl.BlockSpec((CHUNK, d), lambda i: (i, 0)),
                  pl.BlockSpec((CHUNK, d), lambda i: (jnp.maximum(i - 1, 0), 0))],
        out_specs=[pl.BlockSpec((8, 128), lambda i: (0, 0)), pl.BlockSpec((CHUNK, d), lambda i: (i, 0))],
        compiler_params=_params("arbitrary"),
        name=name,
    )(h, target)


def _local_step(x, target, meta, norm_mix_g, norm_mlp_g, w_in, gn_g, conv_w, conv_b, ln_g, ln_b, qn_g, kn_g, later,
                reached=lambda point, after, grads=None: None):
    seq = x.shape[0]
    p = PAD_FRONT + N_META + seq
    d = D_MODEL
    tables = _retention_tables(p)
    row = lambda v: v.reshape(1, -1)
    h0 = jnp.concatenate([jnp.zeros((PAD_FRONT, d), F32), meta, x], axis=0)

    hn0 = _rmsnorm_fwd(h0, row(norm_mix_g[0]), name="l0_mix_norm")
    proj = _matmul(hn0, w_in, mode="nn", out_dtypes=(F32,), name="l0_proj")
    og, opre, sprev = _retention_fwd(proj, gn_g, tables, name="l0_retention")
    cb, y_conv = _conv_fwd(proj, conv_w, row(conv_b), row(ln_g), row(ln_b), name="l0_conv")
    cat = jnp.concatenate([og, cb], axis=1)
    w_out, w1_0, w2_0 = later("l0", cat)
    w1, w2 = [w1_0, None], [w2_0, None]
    h1 = _matmul(cat, w_out, mode="nn", out_dtypes=(F32,), epilogue=_add_epilogue, extras=(h0,), name="l0_mix_out")
    h2, mlp0 = _mlp_fwd(h1, row(norm_mlp_g[0]), w1[0], w2[0], name="l0_mlp")

    hn1 = _rmsnorm_fwd(h2, row(norm_mix_g[1]), name="l1_mix_norm")
    (w_qkv,) = later("qkv", hn1)
    qkv = _matmul(hn1, w_qkv, mode="nn", out_dtypes=(F32,), name="l1_qkv")
    qg_t, kg_t = jnp.tile(row(qn_g), (1, SB_HEADS)), jnp.tile(row(kn_g), (1, SB_HEADS))
    qn, kn, vb = _qknorm_fwd(qkv, qg_t, kg_t, name="l1_qknorm")
    o_sb, carries = _sb_fwd(qn, kn, vb, name="l1_stickbreak")
    w_o, w1[1], w2[1] = later("l1", o_sb)
    h3 = _matmul(o_sb, w_o, mode="nn", out_dtypes=(F32,), epilogue=_add_epilogue, extras=(h2,), name="l1_mix_out")
    h4, mlp1 = _mlp_fwd(h3, row(norm_mlp_g[1]), w1[1], w2[1], name="l1_mlp")

    sq, dh4 = _loss_head(h4, target, name="loss_head")

    dh3, dg_mlp1, dw1_1, dw2_1 = _mlp_bwd(h3, row(norm_mlp_g[1]), w1[1], w2[1], mlp1, dh4, name="l1_mlp_bwd")
    do_sb = _matmul(dh3, w_o, mode="nt", out_dtypes=(F32,), name="l1_do")
    dw_o = _matmul(o_sb, dh3, mode="tn", out_dtypes=(F32,), name="l1_dwo")
    dqn, dkn, dv = _sb_bwd(qn, kn, vb, carries, do_sb, name="l1_stickbreak_bwd")
    dqkv, dqg_t, dkg_t = _qknorm_bwd(qkv, qg_t, kg_t, dqn, dkn, dv, name="l1_qknorm_bwd")
    dw_qkv = _matmul(hn1, dqkv, mode="tn", out_dtypes=(F32,), name="l1_dwqkv")
    pin = lambda arr, tok: arr if tok is None else arr + tok[0:1, 0:1]
    tok = reached("l1_grads", dw_qkv, dict(odd_w_qkv=dw_qkv, odd_w_o=dw_o, mlp_w1_1=dw1_1, mlp_w2_1=dw2_1))
    dhn1 = _matmul(dqkv, w_qkv, mode="nt", out_dtypes=(F32,), name="l1_dhn", after=tok)
    dh2, dg_mix1 = _rmsnorm_bwd(h2, row(norm_mix_g[1]), dhn1, dh3, name="l1_mix_dnorm")
    tok = reached("l1_done", dh2)

    dh1, dg_mlp0, dw1_0, dw2_0 = _mlp_bwd(h1, row(norm_mlp_g[0]), w1[0], w2[0], mlp0, dh2, name="l0_mlp_bwd", after=tok)
    tok = reached("l0_mlp_grads", dh1, dict(mlp_w1_0=dw1_0, mlp_w2_0=dw2_0))
    dcat = _matmul(dh1, w_out, mode="nt", out_dtypes=(F32,), name="l0_dcat", after=tok)
    dw_out = _matmul(cat, dh1, mode="tn", out_dtypes=(F32,), name="l0_dwout")
    tok = reached("l0_dwout", dw_out, dict(even_w_out=dw_out))
    dq, dk, dvr, dgate_r, dgn = _retention_bwd(proj, pin(gn_g, tok), tables, opre, sprev, dcat, name="l0_retention_bwd")
    tok = reached("l0_retention_bwd", dq)
    da, dgate_c, dconv_w, dconv_b, dln_g, dln_b = _conv_bwd(proj, conv_w, pin(row(ln_g), tok), row(ln_b), y_conv, dcat,
                                                            name="l0_conv_bwd")
    tok = reached("l0_conv_bwd", da)
    dproj = jnp.concatenate([dq, dk, dvr, dgate_r, da, dgate_c], axis=1)
    dw_in = _matmul(hn0, dproj, mode="tn", out_dtypes=(F32,), name="l0_dwin", after=tok)
    dhn0 = _matmul(dproj, w_in, mode="nt", out_dtypes=(F32,), name="l0_dhn")
    dh0, dg_mix0 = _rmsnorm_bwd(h0, row(norm_mix_g[0]), dhn0, dh1, name="l0_mix_dnorm")

    fold = lambda t: t.reshape(SB_HEADS, SB_HEAD_DIM).sum(axis=0)
    grads = dict(
        x=dh0[PAD_FRONT + N_META:],
        meta=dh0[PAD_FRONT:PAD_FRONT + N_META],
        norm_mix_g=jnp.concatenate([dg_mix0, dg_mix1], axis=0),
        norm_mlp_g=jnp.concatenate([dg_mlp0, dg_mlp1], axis=0),
        even_w_in=dw_in,
        even_ret_gn_g=dgn.reshape(RET_HEADS, RET_V_DIM),
        even_conv_w=dconv_w,
        even_conv_b=dconv_b,
        even_conv_ln_g=dln_g,
        even_conv_ln_b=dln_b,
        even_w_out=dw_out,
        odd_w_qkv=dw_qkv,
        odd_q_norm_g=fold(dqg_t)[None],
        odd_k_norm_g=fold(dkg_t)[None],
        odd_w_o=dw_o,
        mlp_w1=(dw1_0, dw1_1),
        mlp_w2=(dw2_0, dw2_1),
    )
    return sq[0, 0], grads


def _position():
    x, y, c = lax.axis_index("x"), lax.axis_index("y"), lax.axis_index("c")
    other_chips = [(1 - x, y), (x, 1 - y), (1 - x, 1 - y)]
    return x, y, c, other_chips


def _shard_of(ref, kind, s, n):
    rows, cols = ref.shape
    if kind == "col":
        return ref.at[:, pl.ds(s * (cols // n), cols // n)]
    return ref.at[pl.ds(s * (rows // n), rows // n), :]


def _half_of(ref, kind, c):
    rows, cols = ref.shape
    if kind == "col":
        return ref.at[pl.ds(c * (rows // 2), rows // 2), :]
    return ref.at[:, pl.ds(c * (cols // 2), cols // 2)]


def _remote(src, dst, send_sems, recv_sems, idx, device):
    return pltpu.make_async_remote_copy(src_ref=src, dst_ref=dst, send_sem=send_sems.at[idx], recv_sem=recv_sems.at[idx],
                                        device_id=device, device_id_type=MESH)


def _cast_into_whole(w, kind, s_arr, *, name):
    rows, cols = w.shape
    tr = _pick(rows, (256, 128))
    nb = rows // tr
    if kind == "col":
        whole, o_spec = (rows, cols * N_CHIPS), pl.BlockSpec((tr, cols), lambda i, s_ref: (i, s_ref[0]))
    else:
        whole, o_spec = (rows * N_CHIPS, cols), pl.BlockSpec((tr, cols), lambda i, s_ref: (s_ref[0] * nb + i, 0))

    def body(s_ref, w_ref, o_ref):
        o_ref[...] = w_ref[...].astype(BF16)

    return pl.pallas_call(
        body,
        out_shape=jax.ShapeDtypeStruct(whole, BF16),
        grid_spec=pltpu.PrefetchScalarGridSpec(num_scalar_prefetch=1, grid=(nb,),
                                               in_specs=[pl.BlockSpec((tr, cols), lambda i, s_ref: (i, 0))],
                                               out_specs=o_spec),
        compiler_params=_params("parallel"),
        name=name,
    )(s_arr, w)


def _allgather_weights(wholes, kinds):
    n = len(wholes)

    def body(*refs):
        ins, outs = refs[:n], refs[n:2 * n]
        send_sems, recv_sems = refs[2 * n:]
        x, y, c, chips = _position()
        me_chip = 2 * x + y
        sibling = (x, y, 1 - c)
        sends = []
        for t in range(n):
            for k, (cx, cy) in enumerate(chips):
                src = _half_of(_shard_of(ins[t], kinds[t], me_chip, N_CHIPS), kinds[t], c)
                dst = _half_of(_shard_of(outs[t], kinds[t], me_chip, N_CHIPS), kinds[t], c)
                sends.append(_remote(src, dst, send_sems, recv_sems, 6 * t + k, (cx, cy, c)))
        for cp in sends:
            cp.start()
        passed = []
        for t in range(n):
            for k, (cx, cy) in enumerate(chips):
                landed = _half_of(_shard_of(outs[t], kinds[t], 2 * cx + cy, N_CHIPS), kinds[t], c)
                _remote(landed, landed, send_sems, recv_sems, 6 * t + k, (cx, cy, c)).wait_recv()
                fwd = _remote(landed, landed, send_sems, recv_sems, 6 * t + 3 + k, sibling)
                fwd.start()
                passed.append(fwd)
        for t in range(n):
            for k, (cx, cy) in enumerate(chips):
                theirs = _half_of(_shard_of(outs[t], kinds[t], 2 * cx + cy, N_CHIPS), kinds[t], 1 - c)
                _remote(theirs, theirs, send_sems, recv_sems, 6 * t + 3 + k, sibling).wait_recv()
        for cp in sends + passed:
            cp.wait_send()

    return pl.pallas_call(
        body,
        out_shape=[jax.ShapeDtypeStruct(w.shape, BF16) for w in wholes],
        in_specs=[ANY] * n,
        out_specs=[ANY] * n,
        input_output_aliases={t: t for t in range(n)},
        scratch_shapes=[pltpu.SemaphoreType.DMA((6 * n,)), pltpu.SemaphoreType.DMA((6 * n,))],
        name="allgather_weights",
    )(*wholes)


HBM = pl.BlockSpec(memory_space=pltpu.HBM)
SEM = pl.BlockSpec(memory_space=pltpu.SEMAPHORE)
DATAFLOW = pltpu.SideEffectType.DATAFLOW_SIDE_EFFECTING
TARGETS = 6


def _gather_copies(kinds, refs, _, send_sems, recv_sems):
    x, y, c, chips = _position()
    me_chip = 2 * x + y
    sends, lands = [], []
    for t, (ref, kind) in enumerate(zip(refs, kinds)):
        mine = _half_of(_shard_of(ref, kind, me_chip, N_CHIPS), kind, c)
        for k, (cx, cy) in enumerate(chips):
            for other_core in range(2):
                j = TARGETS * t + 2 * k + other_core
                peer_c = 1 - c if other_core else c
                sends.append(_remote(mine, mine, send_sems, recv_sems, j, (cx, cy, peer_c)))
                theirs = _half_of(_shard_of(ref, kind, 2 * cx + cy, N_CHIPS), kind, peer_c)
                lands.append(_remote(theirs, theirs, send_sems, recv_sems, j, (cx, cy, peer_c)))
    return sends, lands


def _pair_swap_copies(kinds, srcs, lands, send_sems, recv_sems):
    x, y, c, _ = _position()
    sibling = (x, y, 1 - c)
    sends = [_remote(_half_of(srcs[t], kinds[t], 1 - c), lands[t], send_sems, recv_sems, t, sibling) for t in range(len(srcs))]
    arrivals = [_remote(_half_of(srcs[t], kinds[t], c), lands[t], send_sems, recv_sems, t, sibling) for t in range(len(srcs))]
    return sends, arrivals


def _chip_exchange_copies(kinds, srcs, lands, send_sems, recv_sems):
    x, y, c, chips = _position()
    sends, arrivals = [], []
    for t in range(len(srcs)):
        for k, (cx, cy) in enumerate(chips):
            src = _shard_of(srcs[t], kinds[t], 2 * cx + cy, N_CHIPS)
            sends.append(_remote(src, lands[t].at[k], send_sems, recv_sems, 3 * t + k, (cx, cy, c)))
            arrivals.append(_remote(src, lands[t].at[k], send_sems, recv_sems, 3 * t + k, (cx, cy, c)))
    return sends, arrivals


def _pair_gather_copies(kinds, srcs, lands, send_sems, recv_sems):
    x, y, c, _ = _position()
    sibling = (x, y, 1 - c)
    sends, arrivals = [], []
    for t in range(len(srcs)):
        mine, theirs = _half_of(srcs[t], kinds[t], c), _half_of(srcs[t], kinds[t], 1 - c)
        sends.append(_remote(mine, mine, send_sems, recv_sems, t, sibling))
        arrivals.append(_remote(theirs, theirs, send_sems, recv_sems, t, sibling))
    return sends, arrivals


def _copies_start(plan, n_sems, srcs, lands, follows, *, name):
    ns, n = len(srcs), len(srcs) + len(lands)

    def body(*refs):
        send_sems, recv_sems = refs[n + 1], refs[n + 2]
        thru, token = refs[n + 3:2 * n + 3], refs[2 * n + 3]
        sends, _ = plan(thru[:ns], thru[ns:], send_sems, recv_sems)
        for cp in sends:
            cp.start()
        token[...] = jnp.zeros_like(token)

    arrays = [pltpu.with_memory_space_constraint(a, pltpu.HBM) for a in list(srcs) + list(lands)]
    outs = pl.pallas_call(
        body,
        name=name,
        out_shape=(pltpu.SemaphoreType.DMA((n_sems,)), pltpu.SemaphoreType.DMA((n_sems,)),
                   *[pltpu.HBM(a.shape, a.dtype) for a in arrays], jax.ShapeDtypeStruct((8, 128), F32)),
        in_specs=(*[HBM] * n, ANY),
        out_specs=(SEM, SEM, *[HBM] * n, pl.BlockSpec(memory_space=pltpu.VMEM)),
        input_output_aliases={t: 2 + t for t in range(n)},
        compiler_params=pltpu.CompilerParams(has_side_effects=DATAFLOW),
    )(*arrays, follows)
    return outs[0], outs[1], list(outs[2:2 + ns]), list(outs[2 + ns:2 + n]), outs[2 + n]


def _copies_wait(plan, started, follows, *, name):
    send_sems, recv_sems, srcs, lands, _ = started
    ns, n = len(srcs), len(srcs) + len(lands)

    def body(*refs):
        ins, s_sems, r_sems = refs[:n], refs[n], refs[n + 1]
        sends, arrivals = plan(ins[:ns], ins[ns:], s_sems, r_sems)
        for cp in sends:
            cp.wait_send()
        for cp in arrivals:
            cp.wait_recv()

    outs = pl.pallas_call(
        body,
        name=name,
        out_shape=tuple(pltpu.HBM(a.shape, a.dtype) for a in srcs + lands),
        in_specs=(*[HBM] * n, SEM, SEM, ANY),
        out_specs=tuple([HBM] * n),
        input_output_aliases={t: t for t in range(n)},
        compiler_params=pltpu.CompilerParams(has_side_effects=DATAFLOW),
    )(*srcs, *lands, send_sems, recv_sems, follows)
    return list(outs[:ns]), list(outs[ns:])


def _allgather8(block, *, name):
    rows, cols = block.shape

    def body(in_ref, out_ref, send_sems, recv_sems, local_sem):
        x, y, c, _ = _position()
        me = 4 * x + 2 * y + c
        mine = pltpu.make_async_copy(in_ref, out_ref.at[me], local_sem)
        mine.start()
        peers = []
        for flip in range(1, N_DEV):
            fx, fy, fc = (flip >> 2) & 1, (flip >> 1) & 1, flip & 1
            peers.append(((1 - x if fx else x), (1 - y if fy else y), (1 - c if fc else c)))
        sends = [_remote(in_ref, out_ref.at[me], send_sems, recv_sems, j, peer) for j, peer in enumerate(peers)]
        for cp in sends:
            cp.start()
        for j, (px, py, pc) in enumerate(peers):
            slot = out_ref.at[4 * px + 2 * py + pc]
            _remote(slot, slot, send_sems, recv_sems, j, (px, py, pc)).wait_recv()
        for cp in sends:
            cp.wait_send()
        mine.wait()

    vmem = pl.BlockSpec(memory_space=pltpu.VMEM)
    return pl.pallas_call(
        body,
        out_shape=jax.ShapeDtypeStruct((N_DEV, rows, cols), F32),
        in_specs=[vmem],
        out_specs=vmem,
        scratch_shapes=[pltpu.SemaphoreType.DMA((N_DEV - 1,)), pltpu.SemaphoreType.DMA((N_DEV - 1,)),
                        pltpu.SemaphoreType.DMA],
        name=name,
    )(block)


def _sum8(stack, *, name):
    _, rows, cols = stack.shape

    def body(s_ref, o_ref):
        acc = s_ref[0]
        for i in range(1, N_DEV):
            acc = acc + s_ref[i]
        o_ref[...] = acc

    return pl.pallas_call(body, out_shape=jax.ShapeDtypeStruct((rows, cols), F32), name=name)(stack)


def _swap_halves_in(grads, kinds):
    n = len(grads)

    def body(*refs):
        ins, outs = refs[:n], refs[n:2 * n]
        send_sems, recv_sems = refs[2 * n:]
        x, y, c, _ = _position()
        sibling = (x, y, 1 - c)
        sends = [_remote(_half_of(ins[t], kinds[t], 1 - c), outs[t], send_sems, recv_sems, t, sibling) for t in range(n)]
        for cp in sends:
            cp.start()
        for t in range(n):
            _remote(_half_of(ins[t], kinds[t], c), outs[t], send_sems, recv_sems, t, sibling).wait_recv()
        for cp in sends:
            cp.wait_send()

    def half(g, kind):
        rows, cols = g.shape
        return (rows // 2, cols) if kind == "col" else (rows, cols // 2)

    return pl.pallas_call(
        body,
        out_shape=[jax.ShapeDtypeStruct(half(g, k), F32) for g, k in zip(grads, kinds)],
        in_specs=[ANY] * n,
        out_specs=[ANY] * n,
        scratch_shapes=[pltpu.SemaphoreType.DMA((n,)), pltpu.SemaphoreType.DMA((n,))],
        name="reduce_core_pair",
    )(*grads)


def _half_add(grad, theirs, kind, c_arr, *, name):
    rows, cols = theirs.shape
    tr = _pick(rows, (256, 128))
    nb = rows // tr
    if kind == "col":
        g_spec = pl.BlockSpec((tr, cols), lambda i, c_ref: (c_ref[0] * nb + i, 0))
    else:
        g_spec = pl.BlockSpec((tr, cols), lambda i, c_ref: (i, c_ref[0]))
    t_spec = pl.BlockSpec((tr, cols), lambda i, c_ref: (i, 0))

    def body(c_ref, g_ref, t_ref, o32_ref, o16_ref):
        tot = g_ref[...] + t_ref[...]
        o32_ref[...] = tot
        o16_ref[...] = tot.astype(BF16)

    return pl.pallas_call(
        body,
        out_shape=[jax.ShapeDtypeStruct((rows, cols), F32), jax.ShapeDtypeStruct((rows, cols), BF16)],
        grid_spec=pltpu.PrefetchScalarGridSpec(num_scalar_prefetch=1, grid=(nb,), in_specs=[g_spec, t_spec],
                                               out_specs=[t_spec, t_spec]),
        compiler_params=_params("parallel"),
        name=name,
    )(c_arr, grad, theirs)


def _exchange_chips(parts, kinds):
    n = len(parts)

    def body(*refs):
        ins, outs = refs[:n], refs[n:2 * n]
        send_sems, recv_sems = refs[2 * n:]
        x, y, c, chips = _position()
        sends = []
        for t in range(n):
            for k, (cx, cy) in enumerate(chips):
                src = _shard_of(ins[t], kinds[t], 2 * cx + cy, N_CHIPS)
                sends.append(_remote(src, outs[t].at[k], send_sems, recv_sems, 3 * t + k, (cx, cy, c)))
        for cp in sends:
            cp.start()
        for t in range(n):
            for k, (cx, cy) in enumerate(chips):
                src = _shard_of(ins[t], kinds[t], 2 * cx + cy, N_CHIPS)
                _remote(src, outs[t].at[k], send_sems, recv_sems, 3 * t + k, (cx, cy, c)).wait_recv()
        for cp in sends:
            cp.wait_send()

    def piece(p, kind):
        rows, cols = p.shape
        return (3, rows, cols // N_CHIPS) if kind == "col" else (3, rows // N_CHIPS, cols)

    return pl.pallas_call(
        body,
        out_shape=[jax.ShapeDtypeStruct(piece(p, k), BF16) for p, k in zip(parts, kinds)],
        in_specs=[ANY] * n,
        out_specs=[ANY] * n,
        scratch_shapes=[pltpu.SemaphoreType.DMA((3 * n,)), pltpu.SemaphoreType.DMA((3 * n,))],
        name="reduce_chips",
    )(*parts)


def _shard_sum(part32, recv, kind, sc_arr, *, name):
    _, rows, cols = recv.shape
    tr = _pick(rows, (256, 128))
    nb = rows // tr
    if kind == "col":
        whole = (2 * rows, cols)
        p_spec = pl.BlockSpec((tr, cols), lambda i, sc: (i, sc[0]))
        o_spec = pl.BlockSpec((tr, cols), lambda i, sc: (sc[1] * nb + i, 0))
    else:
        whole = (rows, 2 * cols)
        p_spec = pl.BlockSpec((tr, cols), lambda i, sc: (sc[0] * nb + i, 0))
        o_spec = pl.BlockSpec((tr, cols), lambda i, sc: (i, sc[1]))
    r_spec = pl.BlockSpec((3, tr, cols), lambda i, sc: (0, i, 0))

    def body(sc_ref, p_ref, r_ref, o_ref):
        acc = p_ref[...]
        for k in range(3):
            acc = acc + r_ref[k].astype(F32)
        o_ref[...] = acc

    return pl.pallas_call(
        body,
        out_shape=jax.ShapeDtypeStruct(whole, F32),
        grid_spec=pltpu.PrefetchScalarGridSpec(num_scalar_prefetch=1, grid=(nb,), in_specs=[p_spec, r_spec],
                                               out_specs=o_spec),
        compiler_params=_params("parallel"),
        name=name,
    )(sc_arr, part32, recv)


def _swap_halves_out(shards, kinds):
    n = len(shards)

    def body(*refs):
        ins, outs = refs[:n], refs[n:2 * n]
        send_sems, recv_sems = refs[2 * n:]
        x, y, c, _ = _position()
        sibling = (x, y, 1 - c)
        sends = [_remote(_half_of(ins[t], kinds[t], c), _half_of(outs[t], kinds[t], c), send_sems, recv_sems, t, sibling)
                 for t in range(n)]
        for cp in sends:
            cp.start()
        for t in range(n):
            theirs = _half_of(outs[t], kinds[t], 1 - c)
            _remote(theirs, theirs, send_sems, recv_sems, t, sibling).wait_recv()
        for cp in sends:
            cp.wait_send()

    return pl.pallas_call(
        body,
        out_shape=[jax.ShapeDtypeStruct(s.shape, F32) for s in shards],
        in_specs=[ANY] * n,
        out_specs=[ANY] * n,
        input_output_aliases={t: t for t in range(n)},
        scratch_shapes=[pltpu.SemaphoreType.DMA((n,)), pltpu.SemaphoreType.DMA((n,))],
        name="gather_core_pair",
    )(*shards)


def _adamw(w, g, m, v, *, name):
    rows, cols = w.shape
    tr = _pick(rows, (256, 128)) if rows * cols > 64 * 1024 else rows

    def body(w_ref, g_ref, m_ref, v_ref, d_ref, nm_ref, nv_ref):
        gv = g_ref[...]
        nm = ADAM_B1 * m_ref[...] + (1.0 - ADAM_B1) * gv
        nv = ADAM_B2 * v_ref[...] + (1.0 - ADAM_B2) * jnp.square(gv)
        m_hat = nm / (1.0 - ADAM_B1 ** ADAM_STEP)
        v_hat = nv / (1.0 - ADAM_B2 ** ADAM_STEP)
        d_ref[...] = -ADAM_LR * (m_hat / (jnp.sqrt(v_hat) + ADAM_EPS) + ADAM_WD * w_ref[...])
        nm_ref[...] = nm
        nv_ref[...] = nv

    spec = pl.BlockSpec((tr, cols), lambda i: (i, 0))
    return pl.pallas_call(
        body,
        out_shape=[jax.ShapeDtypeStruct((rows, cols), F32)] * 3,
        grid=(rows // tr,),
        in_specs=[spec] * 4,
        out_specs=[spec] * 3,
        compiler_params=_params("parallel"),
        name=name,
    )(w, g, m, v)


BIG = ("even_w_in", "odd_w_qkv", "mlp_w1_0", "mlp_w1_1", "even_w_out", "odd_w_o", "mlp_w2_0", "mlp_w2_1")
BIG_KIND = ("col", "col", "col", "col", "row", "row", "row", "row")


class _TravellingReduction:
    def __init__(self, tag, names, kinds, c_arr, sc_arr):
        self.tag, self.names, self.kinds, self.c_arr, self.sc_arr = tag, names, kinds, c_arr, sc_arr
        self.swap = functools.partial(_pair_swap_copies, kinds)
        self.exchange = functools.partial(_chip_exchange_copies, kinds)
        self.gather = functools.partial(_pair_gather_copies, kinds)

    def pair_swap_start(self, grads, follows):
        half = lambda g, kind: (g.shape[0] // 2, g.shape[1]) if kind == "col" else (g.shape[0], g.shape[1] // 2)
        lands = [lax.empty(half(g, k), F32) for g, k in zip(grads, self.kinds)]
        self.started = _copies_start(self.swap, len(grads), grads, lands, follows, name=f"reduce_{self.tag}_pair_start")

    def pair_swap_finish(self, after):
        grads, theirs = _copies_wait(self.swap, self.started, after, name=f"reduce_{self.tag}_pair_wait")
        self.sums = [_half_add(g, th, k, self.c_arr, name="pair_sum_" + n)
                     for g, th, k, n in zip(grads, theirs, self.kinds, self.names)]

    def chips_start(self, follows):
        parts = [s16 for _, s16 in self.sums]
        piece = lambda p, kind: (3, p.shape[0], p.shape[1] // N_CHIPS) if kind == "col" else (3, p.shape[0] // N_CHIPS, p.shape[1])
        lands = [lax.empty(piece(p, k), BF16) for p, k in zip(parts, self.kinds)]
        self.started = _copies_start(self.exchange, 3 * len(parts), parts, lands, follows,
                                     name=f"reduce_{self.tag}_chips_start")

    def chips_finish(self, after):
        _, recv = _copies_wait(self.exchange, self.started, after, name=f"reduce_{self.tag}_chips_wait")
        self.halves = [_shard_sum(s32, r, k, self.sc_arr, name="chip_sum_" + n)
                       for (s32, _), r, k, n in zip(self.sums, recv, self.kinds, self.names)]

    def pair_gather_start(self, follows):
        self.started = _copies_start(self.gather, len(self.halves), self.halves, [], follows,
                                     name=f"reduce_{self.tag}_gather_start")

    def pair_gather_finish(self, after):
        shards, _ = _copies_wait(self.gather, self.started, after, name=f"reduce_{self.tag}_gather_wait")
        return dict(zip(self.names, shards))
SUBLANES = 8


def _pack_rows(parts, width):
    padded, offsets, r0 = [], [], 0
    for t in parts:
        rows = -(-t.shape[0] // SUBLANES) * SUBLANES
        padded.append(jnp.pad(t, ((0, rows - t.shape[0]), (0, width - t.shape[1]))))
        offsets.append(r0)
        r0 += rows
    return jnp.concatenate(padded, axis=0), offsets


def kernel(x, meta, norm_mix_g, norm_mlp_g, even_w_in, even_ret_gn_g, even_conv_w, even_conv_b, even_conv_ln_g, even_conv_ln_b, even_w_out, odd_w_qkv, odd_q_norm_g, odd_k_norm_g, odd_w_o, mlp_w1, mlp_w2, loss_target, m_meta, m_norm_mix_g, m_norm_mlp_g, m_even_w_in, m_even_ret_gn_g, m_even_conv_w, m_even_conv_b, m_even_conv_ln_g, m_even_conv_ln_b, m_even_w_out, m_odd_w_qkv, m_odd_q_norm_g, m_odd_k_norm_g, m_odd_w_o, m_mlp_w1, m_mlp_w2, v_meta, v_norm_mix_g, v_norm_mlp_g, v_even_w_in, v_even_ret_gn_g, v_even_conv_w, v_even_conv_b, v_even_conv_ln_g, v_even_conv_ln_b, v_even_w_out, v_odd_w_qkv, v_odd_q_norm_g, v_odd_k_norm_g, v_odd_w_o, v_mlp_w1, v_mlp_w2):
    d = D_MODEL
    xi, yi, ci = lax.axis_index("x"), lax.axis_index("y"), lax.axis_index("c")
    chip = 2 * xi + yi
    c_arr = jnp.reshape(ci, (1,)).astype(jnp.int32)
    s_arr = jnp.reshape(chip, (1,)).astype(jnp.int32)

    def split_big(w_in, w_qkv, w1, w_out, w_o, w2):
        return dict(zip(BIG, (w_in[0], w_qkv[0], w1[0], w1[1], w_out[0], w_o[0], w2[0], w2[1])))

    w_big = split_big(even_w_in, odd_w_qkv, mlp_w1, even_w_out, odd_w_o, mlp_w2)
    m_big = split_big(m_even_w_in, m_odd_w_qkv, m_mlp_w1, m_even_w_out, m_odd_w_o, m_mlp_w2)
    v_big = split_big(v_even_w_in, v_odd_w_qkv, v_mlp_w1, v_even_w_out, v_odd_w_o, v_mlp_w2)

    placed = {n: _cast_into_whole(w_big[n], k, s_arr, name="cast_" + n) for n, k in zip(BIG, BIG_KIND)}
    kind_of = dict(zip(BIG, BIG_KIND))
    (w_in_full,) = _allgather_weights([placed["even_w_in"]], [kind_of["even_w_in"]])
    packed, (r_meta, r_conv, r_gn) = _pack_rows([meta, even_conv_w[0], even_ret_gn_g[0]], d // N_CHIPS)
    gathered = _allgather8(packed, name="allgather_small_params")[0::2]
    groups = dict(l0=("even_w_out", "mlp_w1_0", "mlp_w2_0"), qkv=("odd_w_qkv",), l1=("odd_w_o", "mlp_w1_1", "mlp_w2_1"))
    in_flight, follows = {}, gathered[0, 0:1, 0:1] + w_in_full[0:1, 0:1].astype(F32)
    for group, names in groups.items():
        plan = functools.partial(_gather_copies, [kind_of[n] for n in names])
        in_flight[group] = (plan, _copies_start(plan, TARGETS * len(names), [placed[n] for n in names], [], follows,
                                                name="gather_" + group + "_start"))
        follows = in_flight[group][1][-1]
    started = follows[0:1, 0:1]

    def later(group, after):
        plan, state = in_flight[group]
        return _copies_wait(plan, state, after, name="gather_" + group + "_wait")[0]

    sc_arr = jnp.concatenate([s_arr, c_arr])
    early = ("odd_w_qkv", "odd_w_o", "mlp_w1_1", "mlp_w2_1"), ("mlp_w1_0", "mlp_w2_0"), ("even_w_out",)
    red_l1, red_m0, red_o0 = (_TravellingReduction(tag, names, [kind_of[n] for n in names], c_arr, sc_arr)
                              for tag, names in zip(("l1", "m0", "o0"), early))
    grad_big = {}

    def reached(point, after, grads=None):
        if point == "l1_grads":
            red_l1.pair_swap_start([grads[n] for n in red_l1.names], after)
            return red_l1.started[-1]
        if point == "l1_done":
            red_l1.pair_swap_finish(after)
            red_l1.chips_start(after)
            return red_l1.started[-1]
        if point == "l0_mlp_grads":
            red_m0.pair_swap_start([grads[n] for n in red_m0.names], after)
            return red_m0.started[-1]
        if point == "l0_dwout":
            red_m0.pair_swap_finish(after)
            red_m0.chips_start(after)
            red_o0.pair_swap_start([grads[n] for n in red_o0.names], red_m0.started[-1])
            return red_o0.started[-1]
        if point == "l0_retention_bwd":
            red_l1.chips_finish(after)
            red_l1.pair_gather_start(after)
            red_o0.pair_swap_finish(after)
            red_o0.chips_start(red_l1.started[-1])
            return red_o0.started[-1]
        if point == "l0_conv_bwd":
            red_m0.chips_finish(after)
            red_m0.pair_gather_start(after)
            grad_big.update(red_l1.pair_gather_finish(after))
            red_o0.chips_finish(after)
            red_o0.pair_gather_start(red_m0.started[-1])
            return red_o0.started[-1]
        return None

    across = lambda r0, rows, width: jnp.concatenate([gathered[s, r0:r0 + rows, 0:width] for s in range(N_CHIPS)], axis=1)
    meta_full = across(r_meta, N_META, d // N_CHIPS) + started
    conv_w_full = across(r_conv, CONV_WIDTH, d // N_CHIPS)
    gn_full = across(r_gn, RET_HEADS, RET_V_DIM // N_CHIPS)

    sq, g = _local_step(
        x[0], loss_target[0], meta_full, norm_mix_g, norm_mlp_g, w_in_full, gn_full, conv_w_full,
        even_conv_b[0], even_conv_ln_g[0], even_conv_ln_b[0], odd_q_norm_g[0], odd_k_norm_g[0], later, reached)
    grad_big.update(red_m0.pair_gather_finish(g["even_w_in"]))
    grad_big.update(red_o0.pair_gather_finish(g["even_w_in"]))
    loss = lax.psum(0.5 * sq / d, ("x", "y", "c"))

    small_names = ("norm_mix_g", "norm_mlp_g", "even_conv_b", "even_conv_ln_g", "even_conv_ln_b", "odd_q_norm_g",
                   "odd_k_norm_g", "meta", "even_conv_w", "even_ret_gn_g")
    pack, offsets = _pack_rows([g[n] for n in small_names], d)
    summed = _sum8(_allgather8(pack, name="allgather_small_grads"), name="sum_small_grads")
    small = {n: summed[r0:r0 + g[n].shape[0], 0:g[n].shape[1]] for n, r0 in zip(small_names, offsets)}
    for n in ("meta", "even_conv_w", "even_ret_gn_g"):
        width = small[n].shape[1] // N_CHIPS
        small[n] = lax.dynamic_slice_in_dim(small[n], chip * width, width, axis=1)

    last = ("even_w_in",)
    last_kinds = [kind_of[n] for n in last]
    g_last = [g[n] for n in last]
    theirs = _swap_halves_in(g_last, last_kinds)
    sums = [_half_add(gb, th, k, c_arr, name="pair_sum_" + n) for gb, th, k, n in zip(g_last, theirs, last_kinds, last)]
    recv = _exchange_chips([s16 for _, s16 in sums], last_kinds)
    halves = [_shard_sum(s32, r, k, sc_arr, name="chip_sum_" + n) for (s32, _), r, k, n in zip(sums, recv, last_kinds, last)]
    grad_big.update(zip(last, _swap_halves_out(halves, last_kinds)))

    upd = {n: _adamw(w_big[n], grad_big[n], m_big[n], v_big[n], name="adamw_" + n) for n in BIG}

    def join(name, idx, lead):
        if name in ("mlp_w1", "mlp_w2"):
            return jnp.stack([upd[name + "_0"][idx], upd[name + "_1"][idx]]) if idx >= 0 else jnp.stack(
                [grad_big[name + "_0"], grad_big[name + "_1"]])
        t = upd[name][idx] if idx >= 0 else grad_big[name]
        return t[None] if lead else t

    small_w = dict(meta=meta, norm_mix_g=norm_mix_g, norm_mlp_g=norm_mlp_g, even_ret_gn_g=even_ret_gn_g[0],
                   even_conv_w=even_conv_w[0], even_conv_b=even_conv_b, even_conv_ln_g=even_conv_ln_g,
                   even_conv_ln_b=even_conv_ln_b, odd_q_norm_g=odd_q_norm_g, odd_k_norm_g=odd_k_norm_g)
    small_m = dict(meta=m_meta, norm_mix_g=m_norm_mix_g, norm_mlp_g=m_norm_mlp_g, even_ret_gn_g=m_even_ret_gn_g[0],
                   even_conv_w=m_even_conv_w[0], even_conv_b=m_even_conv_b, even_conv_ln_g=m_even_conv_ln_g,
                   even_conv_ln_b=m_even_conv_ln_b, odd_q_norm_g=m_odd_q_norm_g, odd_k_norm_g=m_odd_k_norm_g)
    small_v = dict(meta=v_meta, norm_mix_g=v_norm_mix_g, norm_mlp_g=v_norm_mlp_g, even_ret_gn_g=v_even_ret_gn_g[0],
                   even_conv_w=v_even_conv_w[0], even_conv_b=v_even_conv_b, even_conv_ln_g=v_even_conv_ln_g,
                   even_conv_ln_b=v_even_conv_ln_b, odd_q_norm_g=v_odd_q_norm_g, odd_k_norm_g=v_odd_k_norm_g)
    small_upd = {n: _adamw(small_w[n], small[n], small_m[n], small_v[n], name="adamw_" + n) for n in small_w}
    leading = ("even_ret_gn_g", "even_conv_w")

    order = ("meta", "norm_mix_g", "norm_mlp_g", "even_w_in", "even_ret_gn_g", "even_conv_w", "even_conv_b",
             "even_conv_ln_g", "even_conv_ln_b", "even_w_out", "odd_w_qkv", "odd_q_norm_g", "odd_k_norm_g", "odd_w_o",
             "mlp_w1", "mlp_w2")
    big_lead = ("even_w_in", "even_w_out", "odd_w_qkv", "odd_w_o")

    def leaf(name, idx):
        if name in small_w:
            t = small_upd[name][idx] if idx >= 0 else small[name]
            return t[None] if name in leading else t
        return join(name, idx, name in big_lead)

    outs = [loss, g["x"][None]]
    for idx in (-1, 0, 1, 2):
        outs += [leaf(n, idx) for n in order]
    return tuple(outs)
```

```python
import functools

import jax
import jax.numpy as jnp
from jax import lax
from jax.experimental import pallas as pl
from jax.experimental.pallas import tpu as pltpu

F32 = jnp.float32
BF16 = jnp.bfloat16

D_MODEL = 1024
N_META = 16
CHUNK = 128
PAD_FRONT = (-N_META) % CHUNK
RET_HEADS = 4
RET_QK_DIM = 128
RET_V_DIM = 256
RET_QK_W = RET_HEADS * RET_QK_DIM
RET_V_W = RET_HEADS * RET_V_DIM
CONV_WIDTH = 31
CONV_HALO = 32
RET_DECAY_OFFSET = 5.0
ROPE_BASE = 10000.0
SB_HEADS = 16
SB_HEAD_DIM = 64
D_FF = 4 * D_MODEL
EPS = 1e-6
ADAM_LR = 0.001
ADAM_B1 = 0.9
ADAM_B2 = 0.999
ADAM_EPS = 1e-08
ADAM_WD = 0.01
ADAM_STEP = 10

N_CHIPS = 4
N_DEV = 8
VMEM_LIMIT = 56 * 1024 * 1024
MESH = pl.DeviceIdType.MESH
ANY = pl.BlockSpec(memory_space=pl.ANY)


def _params(*sem):
    return pltpu.CompilerParams(dimension_semantics=sem, vmem_limit_bytes=VMEM_LIMIT)


def _pick(n, cands):
    for c in cands:
        if n % c == 0:
            return c
    return n


def _sigmoid(x):
    return 1.0 / (1.0 + jnp.exp(-x))


def _dot(a, b):
    return lax.dot_general(a, b, (((1,), (0,)), ((), ())), preferred_element_type=F32)


def _dot_nt(a, b):
    return lax.dot_general(a, b, (((1,), (1,)), ((), ())), preferred_element_type=F32)


def _dot_tn(a, b):
    return lax.dot_general(a, b, (((0,), (0,)), ((), ())), preferred_element_type=F32)


def _split_dot(x, m):
    hi = x.astype(BF16)
    lo = (x - hi.astype(F32)).astype(BF16)
    return _dot(hi, m) + _dot(lo, m)


def _matmul(a, b, *, mode, out_dtypes, epilogue=None, extras=(), name, after=None):
    if mode == "nn":
        (m, k), (k2, n) = a.shape, b.shape
    elif mode == "nt":
        (m, k), (n, k2) = a.shape, b.shape
    else:
        (k, m), (k2, n) = a.shape, b.shape
    assert k == k2, (a.shape, b.shape, mode)
    tm = _pick(m, (1056, 1024, 768, 512, 384, 256, 128, 96))
    tn = _pick(n, (1024, 768, 512, 256, 128))
    tk = _pick(k, (1056, 1024, 768, 512, 384, 256, 128, 96))
    nk = k // tk
    dot = {"nn": _dot, "nt": _dot_nt, "tn": _dot_tn}[mode]
    n_extra, n_out = len(extras), len(out_dtypes)
    n_after = 0 if after is None else 1
    if epilogue is None:
        epilogue = lambda acc: (acc,)

    def body(a_ref, b_ref, *rest):
        extra_refs = rest[:n_extra]
        out_refs = rest[n_extra + n_after:n_extra + n_after + n_out]
        part = dot(a_ref[...].astype(BF16), b_ref[...].astype(BF16))

        def finish(acc):
            res = epilogue(acc, *[r[...] for r in extra_refs])
            for o_ref, r in zip(out_refs, res):
                o_ref[...] = r.astype(o_ref.dtype)

        if nk == 1:
            finish(part)
        else:
            acc_ref = rest[-1]
            kk = pl.program_id(2)

            @pl.when(kk == 0)
            def _():
                acc_ref[...] = part

            @pl.when(kk > 0)
            def _():
                acc_ref[...] += part

            @pl.when(kk == nk - 1)
            def _():
                finish(acc_ref[...])

    if mode == "nn":
        a_spec = pl.BlockSpec((tm, tk), lambda i, j, kk: (i, kk))
        b_spec = pl.BlockSpec((tk, tn), lambda i, j, kk: (kk, j))
    elif mode == "nt":
        a_spec = pl.BlockSpec((tm, tk), lambda i, j, kk: (i, kk))
        b_spec = pl.BlockSpec((tn, tk), lambda i, j, kk: (j, kk))
    else:
        a_spec = pl.BlockSpec((tk, tm), lambda i, j, kk: (kk, i))
        b_spec = pl.BlockSpec((tk, tn), lambda i, j, kk: (kk, j))
    o_spec = pl.BlockSpec((tm, tn), lambda i, j, kk: (i, j))
    outs = pl.pallas_call(
        body,
        out_shape=[jax.ShapeDtypeStruct((m, n), dt) for dt in out_dtypes],
        grid=(m // tm, n // tn, nk),
        in_specs=[a_spec, b_spec] + [o_spec] * n_extra + [ANY] * n_after,
        out_specs=[o_spec] * n_out,
        scratch_shapes=[pltpu.VMEM((tm, tn), F32)] if nk > 1 else [],
        compiler_params=_params("parallel", "parallel", "arbitrary"),
        name=name,
    )(a, b, *extras, *([] if after is None else [after]))
    return outs[0] if n_out == 1 else outs


def _add_epilogue(acc, res):
    return (res + acc,)


def _rmsnorm_fwd(x, g, *, name):
    p, d = x.shape
    rows = _pick(p, (384, 128, 96))

    def body(x_ref, g_ref, o_ref):
        xv = x_ref[...]
        r = lax.rsqrt(jnp.mean(xv * xv, axis=-1, keepdims=True) + EPS)
        o_ref[...] = (xv * r * g_ref[...]).astype(o_ref.dtype)

    return pl.pallas_call(
        body,
        out_shape=jax.ShapeDtypeStruct((p, d), BF16),
        grid=(p // rows,),
        in_specs=[pl.BlockSpec((rows, d), lambda i: (i, 0)), pl.BlockSpec((1, d), lambda i: (0, 0))],
        out_specs=pl.BlockSpec((rows, d), lambda i: (i, 0)),
        compiler_params=_params("parallel"),
        name=name,
    )(x, g)


def _rmsnorm_bwd(x, g, dy, dres, *, name):
    p, d = x.shape
    rows = _pick(p, (384, 128, 96))

    def body(x_ref, g_ref, dy_ref, dres_ref, dx_ref, dg_ref):
        xv = x_ref[...]
        r = lax.rsqrt(jnp.mean(xv * xv, axis=-1, keepdims=True) + EPS)
        dyv = dy_ref[...]
        gdy = dyv * g_ref[...]
        proj = jnp.mean(xv * gdy, axis=-1, keepdims=True)
        dx_ref[...] = dres_ref[...] + r * gdy - xv * (r * r * r) * proj
        part = jnp.sum(dyv * xv * r, axis=0, keepdims=True)

        @pl.when(pl.program_id(0) == 0)
        def _():
            dg_ref[...] = part

        @pl.when(pl.program_id(0) > 0)
        def _():
            dg_ref[...] += part

    row_spec = pl.BlockSpec((rows, d), lambda i: (i, 0))
    vec_spec = pl.BlockSpec((1, d), lambda i: (0, 0))
    return pl.pallas_call(
        body,
        out_shape=[jax.ShapeDtypeStruct((p, d), F32), jax.ShapeDtypeStruct((1, d), F32)],
        grid=(p // rows,),
        in_specs=[row_spec, vec_spec, row_spec, row_spec],
        out_specs=[row_spec, vec_spec],
        compiler_params=_params("arbitrary"),
        name=name,
    )(x, g, dy, dres)


def _mlp_fwd(h, g, w1, w2, *, name):
    hn = _rmsnorm_fwd(h, g, name=name + "_norm")

    def act(acc):
        r = jnp.maximum(acc, 0.0)
        return acc, r * r

    z, a2 = _matmul(hn, w1, mode="nn", out_dtypes=(F32, BF16), epilogue=act, name=name + "_up")
    out = _matmul(a2, w2, mode="nn", out_dtypes=(F32,), epilogue=_add_epilogue, extras=(h,), name=name + "_down")
    return out, (hn, z, a2)


def _mlp_bwd(h, g, w1, w2, saved, dout, *, name, after=None):
    hn, z, a2 = saved

    def dact(acc, zt):
        return (acc * (2.0 * jnp.maximum(zt, 0.0)),)

    dz = _matmul(dout, w2, mode="nt", out_dtypes=(BF16,), epilogue=dact, extras=(z,), name=name + "_dz", after=after)
    dw2 = _matmul(a2, dout, mode="tn", out_dtypes=(F32,), name=name + "_dw2")
    dw1 = _matmul(hn, dz, mode="tn", out_dtypes=(F32,), name=name + "_dw1")
    dhn = _matmul(dz, w1, mode="nt", out_dtypes=(F32,), name=name + "_dhn")
    dh, dg = _rmsnorm_bwd(h, g, dhn, dout, name=name + "_dnorm")
    return dh, dg, dw1, dw2


def _retention_tables(p):
    half = RET_QK_DIM // 2
    inv_freq = ROPE_BASE ** (-jnp.arange(half, dtype=F32) / half)
    ang = jnp.arange(p, dtype=F32)[:, None] * inv_freq[None, :]
    cos, sin = jnp.cos(ang), jnp.sin(ang)
    cosf = jnp.concatenate([cos, cos], axis=1)
    sins = jnp.concatenate([-sin, sin], axis=1)
    log_g = jnp.log1p(-jnp.exp2(-RET_DECAY_OFFSET - jnp.arange(RET_HEADS, dtype=F32)))
    idx = jnp.arange(CHUNK, dtype=F32)
    diff = idx[:, None] - idx[None, :]
    inner = jnp.where(diff[None] >= 0, jnp.exp(jnp.maximum(diff, 0.0)[None] * log_g[:, None, None]), 0.0)
    kdec = jnp.exp((CHUNK - 1 - idx)[None, :] * log_g[:, None])
    qdec = jnp.exp((idx + 1.0)[None, :] * log_g[:, None])
    cdec = jnp.exp(CHUNK * log_g)
    kdec = jnp.broadcast_to(kdec[:, :, None], (RET_HEADS, CHUNK, RET_QK_DIM))
    qdec = jnp.broadcast_to(qdec[:, :, None], (RET_HEADS, CHUNK, RET_QK_DIM))
    cdec = jnp.broadcast_to(cdec[:, None, None], (RET_HEADS, RET_QK_DIM, RET_V_DIM))
    return cosf, sins, inner, kdec, qdec, cdec


def _rot(x, cosf, sins):
    return x * cosf + pltpu.roll(x, RET_QK_DIM // 2, 1) * sins


def _rot_bwd(dy, cosf, sins):
    return dy * cosf + pltpu.roll(dy * sins, RET_QK_DIM // 2, 1)


def _ret_in_specs(chunk_of):
    q_spec = pl.BlockSpec((CHUNK, RET_QK_W), lambda s: (chunk_of(s), 0))
    k_spec = pl.BlockSpec((CHUNK, RET_QK_W), lambda s: (chunk_of(s), 1))
    v_spec = pl.BlockSpec((CHUNK, RET_V_W), lambda s: (chunk_of(s), 1))
    g_spec = pl.BlockSpec((CHUNK, RET_V_W), lambda s: (chunk_of(s), 2))
    rope_spec = pl.BlockSpec((CHUNK, RET_QK_DIM), lambda s: (chunk_of(s), 0))
    whole = lambda *shape: pl.BlockSpec(shape, lambda s: (0,) * len(shape))
    head_sq = whole(RET_HEADS, CHUNK, CHUNK)
    head_qk = whole(RET_HEADS, CHUNK, RET_QK_DIM)
    head_st = whole(RET_HEADS, RET_QK_DIM, RET_V_DIM)
    gam_spec = whole(RET_HEADS, 1, RET_V_DIM)
    return [q_spec, k_spec, v_spec, g_spec, rope_spec, rope_spec, head_sq, head_qk, head_qk, head_st, gam_spec]


def _ret_head_views(h, qk_refs, v_refs, head_refs):
    qk = pl.ds(h * RET_QK_DIM, RET_QK_DIM)
    vv = pl.ds(h * RET_V_DIM, RET_V_DIM)
    return [r.at[:, qk] for r in qk_refs], [r.at[:, vv] for r in v_refs], [r.at[h] for r in head_refs]


def _retention_fwd(proj, gn_g, tables, *, name):
    p = proj.shape[0]
    n_chunks = p // CHUNK
    scale = RET_QK_DIM ** -0.5

    def body(q_ref, k_ref, v_ref, g_ref, cos_ref, sin_ref, inner_ref, kdec_ref, qdec_ref, cdec_ref, gam_ref,
             og_ref, opre_ref, sprev_ref, s_scr):
        @pl.when(pl.program_id(0) == 0)
        def _():
            s_scr[...] = jnp.zeros_like(s_scr)

        for h in range(RET_HEADS):
            (q_h, k_h), (v_h, g_h, og_h, opre_h), tabs = _ret_head_views(
                h, (q_ref, k_ref), (v_ref, g_ref, og_ref, opre_ref),
                (inner_ref, kdec_ref, qdec_ref, cdec_ref, gam_ref, sprev_ref, s_scr))
            one_head(q_h, k_h, v_h, g_h, cos_ref, sin_ref, *tabs[:5], og_h, opre_h, *tabs[5:])

    def one_head(q_ref, k_ref, v_ref, g_ref, cos_ref, sin_ref, inner_ref, kdec_ref, qdec_ref, cdec_ref, gam_ref,
                 og_ref, opre_ref, sprev_ref, s_scr):
        cosf, sins = cos_ref[...], sin_ref[...]
        qr = _rot(q_ref[...], cosf, sins)
        kr = _rot(k_ref[...], cosf, sins) * scale
        vb = v_ref[...].astype(BF16)
        scores = _dot_nt(qr.astype(BF16), kr.astype(BF16)) * inner_ref[...]
        state = s_scr[...]
        sprev_ref[...] = state
        o = _dot(scores.astype(BF16), vb) + _dot((qr * qdec_ref[...]).astype(BF16), state.astype(BF16))
        kd = kr * kdec_ref[...]
        s_scr[...] = cdec_ref[...] * state + _dot(kd.T.astype(BF16), vb)
        opre_ref[...] = o
        mu = jnp.mean(o, axis=-1, keepdims=True)
        oc = o - mu
        var = jnp.mean(oc * oc, axis=-1, keepdims=True)
        on = oc * lax.rsqrt(var + EPS) * gam_ref[...]
        gv = g_ref[...]
        og_ref[...] = (gv * _sigmoid(gv) * on).astype(og_ref.dtype)

    chunk_of = lambda s: s
    out_v = pl.BlockSpec((CHUNK, RET_V_W), lambda s: (s, 0))
    return pl.pallas_call(
        body,
        out_shape=[
            jax.ShapeDtypeStruct((p, RET_V_W), BF16),
            jax.ShapeDtypeStruct((p, RET_V_W), F32),
            jax.ShapeDtypeStruct((RET_HEADS, n_chunks, RET_QK_DIM, RET_V_DIM), F32),
        ],
        grid=(n_chunks,),
        in_specs=_ret_in_specs(chunk_of),
        out_specs=[out_v, out_v, pl.BlockSpec((RET_HEADS, None, RET_QK_DIM, RET_V_DIM), lambda s: (0, s, 0, 0))],
        scratch_shapes=[pltpu.VMEM((RET_HEADS, RET_QK_DIM, RET_V_DIM), F32)],
        compiler_params=_params("arbitrary"),
        name=name,
    )(proj, proj, proj, proj, *tables, gn_g.reshape(RET_HEADS, 1, RET_V_DIM))


def _retention_bwd(proj, gn_g, tables, opre, sprev, dog, *, name):
    p = proj.shape[0]
    n_chunks = p // CHUNK
    scale = RET_QK_DIM ** -0.5

    def body(q_ref, k_ref, v_ref, g_ref, cos_ref, sin_ref, inner_ref, kdec_ref, qdec_ref, cdec_ref, gam_ref,
             opre_ref, sprev_ref, dog_ref, dq_ref, dk_ref, dv_ref, dg_ref, dgam_ref, ds_scr):
        first = pl.program_id(0) == 0

        @pl.when(first)
        def _():
            ds_scr[...] = jnp.zeros_like(ds_scr)

        for h in range(RET_HEADS):
            (q_h, k_h, dq_h, dk_h), (v_h, g_h, opre_h, dog_h, dv_h, dg_h), tabs = _ret_head_views(
                h, (q_ref, k_ref, dq_ref, dk_ref), (v_ref, g_ref, opre_ref, dog_ref, dv_ref, dg_ref),
                (inner_ref, kdec_ref, qdec_ref, cdec_ref, gam_ref, sprev_ref, dgam_ref, ds_scr))
            one_head(first, q_h, k_h, v_h, g_h, cos_ref, sin_ref, *tabs[:5], opre_h, tabs[5], dog_h,
                     dq_h, dk_h, dv_h, dg_h, tabs[6], tabs[7])

    def one_head(first, q_ref, k_ref, v_ref, g_ref, cos_ref, sin_ref, inner_ref, kdec_ref, qdec_ref, cdec_ref, gam_ref,
                 opre_ref, sprev_ref, dog_ref, dq_ref, dk_ref, dv_ref, dg_ref, dgam_ref, ds_scr):
        cosf, sins = cos_ref[...], sin_ref[...]
        qr = _rot(q_ref[...], cosf, sins)
        kr = _rot(k_ref[...], cosf, sins) * scale
        qb, kb = qr.astype(BF16), kr.astype(BF16)
        vb = v_ref[...].astype(BF16)
        inner = inner_ref[...]
        qdec, kdec = qdec_ref[...], kdec_ref[...]
        state_b = sprev_ref[...].astype(BF16)
        o = opre_ref[...]
        mu = jnp.mean(o, axis=-1, keepdims=True)
        oc = o - mu
        rstd = lax.rsqrt(jnp.mean(oc * oc, axis=-1, keepdims=True) + EPS)
        xhat = oc * rstd
        gam = gam_ref[...]
        on = xhat * gam
        gv = g_ref[...]
        sig = _sigmoid(gv)
        dogv = dog_ref[...]
        dg_ref[...] = (dogv * on * sig * (1.0 + gv * (1.0 - sig))).astype(dg_ref.dtype)
        don = dogv * gv * sig
        dgam_part = jnp.sum(don * xhat, axis=0, keepdims=True)

        @pl.when(first)
        def _():
            dgam_ref[...] = dgam_part

        @pl.when(jnp.logical_not(first))
        def _():
            dgam_ref[...] += dgam_part

        dxhat = don * gam
        do = rstd * (dxhat - jnp.mean(dxhat, axis=-1, keepdims=True)
                     - xhat * jnp.mean(dxhat * xhat, axis=-1, keepdims=True))
        dob = do.astype(BF16)
        scores_b = (_dot_nt(qb, kb) * inner).astype(BF16)
        da = (_dot_nt(dob, vb) * inner).astype(BF16)
        dv = _dot(scores_b.astype(F32).T.astype(BF16), dob)
        dqr = _dot(da, kb)
        dkr = _dot(da.astype(F32).T.astype(BF16), qb)
        dqr += _dot_nt(dob, state_b) * qdec
        ds_local = _dot((qr * qdec).T.astype(BF16), dob)
        gstate = ds_scr[...]
        gb = gstate.astype(BF16)
        kd_b = (kr * kdec).astype(BF16)
        dkr += _dot_nt(vb, gb) * kdec
        dv += _dot(kd_b, gb)
        ds_scr[...] = cdec_ref[...] * gstate + ds_local
        dq_ref[...] = _rot_bwd(dqr, cosf, sins).astype(dq_ref.dtype)
        dk_ref[...] = _rot_bwd(dkr * scale, cosf, sins).astype(dk_ref.dtype)
        dv_ref[...] = dv.astype(dv_ref.dtype)

    chunk_of = lambda s: n_chunks - 1 - s
    blk_v = pl.BlockSpec((CHUNK, RET_V_W), lambda s: (chunk_of(s), 0))
    blk_qk = pl.BlockSpec((CHUNK, RET_QK_W), lambda s: (chunk_of(s), 0))
    st_spec = pl.BlockSpec((RET_HEADS, None, RET_QK_DIM, RET_V_DIM), lambda s: (0, chunk_of(s), 0, 0))
    return pl.pallas_call(
        body,
        out_shape=[
            jax.ShapeDtypeStruct((p, RET_QK_W), BF16),
            jax.ShapeDtypeStruct((p, RET_QK_W), BF16),
            jax.ShapeDtypeStruct((p, RET_V_W), BF16),
            jax.ShapeDtypeStruct((p, RET_V_W), BF16),
            jax.ShapeDtypeStruct((RET_HEADS, 1, RET_V_DIM), F32),
        ],
        grid=(n_chunks,),
        in_specs=_ret_in_specs(chunk_of) + [blk_v, st_spec, blk_v],
        out_specs=[blk_qk, blk_qk, blk_v, blk_v, pl.BlockSpec((RET_HEADS, 1, RET_V_DIM), lambda s: (0, 0, 0))],
        scratch_shapes=[pltpu.VMEM((RET_HEADS, RET_QK_DIM, RET_V_DIM), F32)],
        compiler_params=_params("arbitrary"),
        name=name,
    )(proj, proj, proj, proj, *tables, gn_g.reshape(RET_HEADS, 1, RET_V_DIM), opre, sprev, dog)


def _conv_rows(p):
    return _pick(p, (384, 128))


CONV_CHUNK = 32
F32_SUBLANES = 8


def _shifted_rows(rows):
    return rows + CONV_HALO - F32_SUBLANES


def _shifted_copies(src_scr, sh_scr, n_rows):
    for s in range(1, F32_SUBLANES):
        sh_scr[s - 1] = src_scr[s:s + n_rows, :]


def _tap_rows(src_scr, sh_scr, off, r0, n):
    q, s = divmod(off, F32_SUBLANES)
    ref = src_scr if s == 0 else sh_scr.at[s - 1]
    return ref[pl.ds(pl.multiple_of(r0 + F32_SUBLANES * q, F32_SUBLANES), n), :]


def _ln_stats(y):
    mu = jnp.mean(y, axis=-1, keepdims=True)
    yc = y - mu
    rstd = lax.rsqrt(jnp.mean(yc * yc, axis=-1, keepdims=True) + EPS)
    return yc * rstd, rstd


def _conv_fwd(proj, conv_w, conv_b, ln_g, ln_b, *, name):
    p = proj.shape[0]
    c = D_MODEL
    rows = _conv_rows(p)
    hpb = rows // CONV_HALO
    a_col, gate_col = (2 * RET_QK_W + 2 * RET_V_W) // c, (2 * RET_QK_W + 2 * RET_V_W) // c + 1

    def body(a_ref, gate_ref, ah_ref, gateh_ref, w_ref, b_ref, lg_ref, lb_ref, c_ref, y_ref, hdn_scr, sh_scr):
        i = pl.program_id(0)
        hdn_scr[0:CONV_HALO, :] = ah_ref[...] * _sigmoid(gateh_ref[...])
        hdn_scr[CONV_HALO:, :] = a_ref[...] * _sigmoid(gate_ref[...])
        _shifted_copies(hdn_scr, sh_scr, _shifted_rows(rows))

        def chunk(j, _):
            r0 = pl.multiple_of(j * CONV_CHUNK, CONV_CHUNK)
            acc = jnp.zeros((CONV_CHUNK, c), F32)
            for w in range(CONV_WIDTH):
                off = CONV_HALO - (CONV_WIDTH - 1) + w
                acc += _tap_rows(hdn_scr, sh_scr, off, r0, CONV_CHUNK) * w_ref[w:w + 1, :]
            y_ref[pl.ds(r0, CONV_CHUNK), :] = acc + b_ref[...]
            return 0

        lax.fori_loop(0, rows // CONV_CHUNK, chunk, 0)
        y = y_ref[...]
        yhat, _ = _ln_stats(y)
        ln = yhat * lg_ref[...] + lb_ref[...]
        row = i * rows + lax.broadcasted_iota(jnp.int32, (rows, 1), 0)
        c_ref[...] = jnp.where(row >= PAD_FRONT, ln * _sigmoid(ln), 0.0).astype(c_ref.dtype)

    halo_idx = lambda i: jnp.maximum(i * hpb - 1, 0)
    vec = pl.BlockSpec((1, c), lambda i: (0, 0))
    return pl.pallas_call(
        body,
        out_shape=[jax.ShapeDtypeStruct((p, c), BF16), jax.ShapeDtypeStruct((p, c), F32)],
        grid=(p // rows,),
        in_specs=[
            pl.BlockSpec((rows, c), lambda i: (i, a_col)),
            pl.BlockSpec((rows, c), lambda i: (i, gate_col)),
            pl.BlockSpec((CONV_HALO, c), lambda i: (halo_idx(i), a_col)),
            pl.BlockSpec((CONV_HALO, c), lambda i: (halo_idx(i), gate_col)),
            pl.BlockSpec((CONV_WIDTH, c), lambda i: (0, 0)),
            vec, vec, vec,
        ],
        out_specs=[pl.BlockSpec((rows, c), lambda i: (i, 0)), pl.BlockSpec((rows, c), lambda i: (i, 0))],
        scratch_shapes=[pltpu.VMEM((CONV_HALO + rows, c), F32),
                        pltpu.VMEM((F32_SUBLANES - 1, _shifted_rows(rows), c), F32)],
        compiler_params=_params("parallel"),
        name=name,
    )(proj, proj, proj, proj, conv_w, conv_b, ln_g, ln_b)


def _conv_bwd(proj, conv_w, ln_g, ln_b, y, dcat, *, name):
    p = proj.shape[0]
    c = D_MODEL
    rows = _conv_rows(p)
    hpb = rows // CONV_HALO
    n_blocks = p // rows
    a_col, gate_col = (2 * RET_QK_W + 2 * RET_V_W) // c, (2 * RET_QK_W + 2 * RET_V_W) // c + 1

    def body(a_ref, gate_ref, ah_ref, gateh_ref, w_ref, lg_ref, lb_ref, y_ref, yh_ref, dc_ref, dch_ref,
             da_ref, dgate_ref, dw_ref, db_ref, dlg_ref, dlb_ref, hdn_scr, dy_scr, hdn_sh, dy_sh):
        i = pl.program_id(0)
        lg, lb = lg_ref[...], lb_ref[...]

        def ln_bwd(yv, dcv):
            yhat, rstd = _ln_stats(yv)
            ln = yhat * lg + lb
            sig = _sigmoid(ln)
            dln = dcv * sig * (1.0 + ln * (1.0 - sig))
            dyhat = dln * lg
            dyv = rstd * (dyhat - jnp.mean(dyhat, axis=-1, keepdims=True)
                          - yhat * jnp.mean(dyhat * yhat, axis=-1, keepdims=True))
            return dyv, dln, yhat

        row = i * rows + lax.broadcasted_iota(jnp.int32, (rows, 1), 0)
        dy, dln, yhat = ln_bwd(y_ref[...], jnp.where(row >= PAD_FRONT, dc_ref[...], 0.0))
        dy_halo, _, _ = ln_bwd(yh_ref[...], dch_ref[...])
        dy_scr[0:rows, :] = dy
        dy_scr[rows:, :] = jnp.where(i == n_blocks - 1, 0.0, dy_halo)
        hdn_scr[0:CONV_HALO, :] = ah_ref[...] * _sigmoid(gateh_ref[...])
        hdn_scr[CONV_HALO:, :] = a_ref[...] * _sigmoid(gate_ref[...])
        _shifted_copies(hdn_scr, hdn_sh, _shifted_rows(rows))
        _shifted_copies(dy_scr, dy_sh, _shifted_rows(rows))

        @pl.when(i == 0)
        def _():
            dw_ref[...] = jnp.zeros_like(dw_ref)
            db_ref[...] = jnp.zeros_like(db_ref)
            dlg_ref[...] = jnp.zeros_like(dlg_ref)
            dlb_ref[...] = jnp.zeros_like(dlb_ref)

        n_chunks = rows // CONV_CHUNK

        def input_grad(j, _):
            r0 = pl.multiple_of(j * CONV_CHUNK, CONV_CHUNK)
            dhdn = jnp.zeros((CONV_CHUNK, c), F32)
            for w in range(CONV_WIDTH):
                dhdn += _tap_rows(dy_scr, dy_sh, CONV_WIDTH - 1 - w, r0, CONV_CHUNK) * w_ref[w:w + 1, :]
            here = pl.ds(r0, CONV_CHUNK)
            sig_gate = _sigmoid(gate_ref[here, :])
            da_ref[here, :] = (dhdn * sig_gate).astype(da_ref.dtype)
            dgate_ref[here, :] = (dhdn * a_ref[here, :] * sig_gate * (1.0 - sig_gate)).astype(dgate_ref.dtype)
            return 0

        lax.fori_loop(0, n_chunks, input_grad, 0)
        for w in range(CONV_WIDTH):
            off = CONV_HALO - (CONV_WIDTH - 1) + w

            def tap_grad(j, acc, off=off):
                r0 = pl.multiple_of(j * CONV_CHUNK, CONV_CHUNK)
                prod = dy_scr[pl.ds(r0, CONV_CHUNK), :] * _tap_rows(hdn_scr, hdn_sh, off, r0, CONV_CHUNK)
                for k in range(CONV_CHUNK // F32_SUBLANES):
                    acc = acc + prod[k * F32_SUBLANES:(k + 1) * F32_SUBLANES]
                return acc

            acc = lax.fori_loop(0, n_chunks, tap_grad, jnp.zeros((F32_SUBLANES, c), F32))
            dw_ref[w:w + 1, :] += jnp.sum(acc, axis=0, keepdims=True)
        db_ref[...] += jnp.sum(dy, axis=0, keepdims=True)
        dlg_ref[...] += jnp.sum(dln * yhat, axis=0, keepdims=True)
        dlb_ref[...] += jnp.sum(dln, axis=0, keepdims=True)

    prev_halo = lambda i: jnp.maximum(i * hpb - 1, 0)
    next_halo = lambda i: jnp.minimum((i + 1) * hpb, p // CONV_HALO - 1)
    vec = pl.BlockSpec((1, c), lambda i: (0, 0))
    blk = lambda col: pl.BlockSpec((rows, c), lambda i: (i, col))
    outs = pl.pallas_call(
        body,
        out_shape=[
            jax.ShapeDtypeStruct((p, c), BF16),
            jax.ShapeDtypeStruct((p, c), BF16),
            jax.ShapeDtypeStruct((CONV_WIDTH + 1, c), F32),
            jax.ShapeDtypeStruct((1, c), F32),
            jax.ShapeDtypeStruct((1, c), F32),
            jax.ShapeDtypeStruct((1, c), F32),
        ],
        grid=(n_blocks,),
        in_specs=[
            blk(a_col), blk(gate_col),
            pl.BlockSpec((CONV_HALO, c), lambda i: (prev_halo(i), a_col)),
            pl.BlockSpec((CONV_HALO, c), lambda i: (prev_halo(i), gate_col)),
            pl.BlockSpec((CONV_WIDTH, c), lambda i: (0, 0)),
            vec, vec,
            blk(0),
            pl.BlockSpec((CONV_HALO, c), lambda i: (next_halo(i), 0)),
            blk(1),
            pl.BlockSpec((CONV_HALO, c), lambda i: (next_halo(i), 1)),
        ],
        out_specs=[blk(0), blk(0), pl.BlockSpec((CONV_WIDTH + 1, c), lambda i: (0, 0)), vec, vec, vec],
        scratch_shapes=[pltpu.VMEM((CONV_HALO + rows, c), F32), pltpu.VMEM((rows + CONV_HALO, c), F32),
                        pltpu.VMEM((F32_SUBLANES - 1, _shifted_rows(rows), c), F32),
                        pltpu.VMEM((F32_SUBLANES - 1, _shifted_rows(rows), c), F32)],
        compiler_params=_params("arbitrary"),
        name=name,
    )(proj, proj, proj, proj, conv_w, ln_g, ln_b, y, y, dcat, dcat)
    da, dgate, dw, db, dlg, dlb = outs
    return da, dgate, dw[:CONV_WIDTH], db, dlg, dlb


LANES = 128


def _group_matrix():
    r = jnp.arange(LANES)[:, None] // SB_HEAD_DIM
    c = jnp.arange(LANES)[None, :] // SB_HEAD_DIM
    return (r == c).astype(BF16)


def _head_sums(v, gm):
    return jnp.concatenate([_split_dot(v[:, j * LANES:(j + 1) * LANES], gm) for j in range(v.shape[1] // LANES)], axis=1)


def _qknorm_fwd(qkv, qg, kg, *, name):
    p = qkv.shape[0]
    d = D_MODEL
    rows = _pick(p, (384, 128, 96))

    def body(q_ref, k_ref, v_ref, qg_ref, kg_ref, gm_ref, qn_ref, kn_ref, vb_ref):
        gm = gm_ref[...]

        def norm(x, g):
            ms = _head_sums(x * x, gm) * (1.0 / SB_HEAD_DIM)
            return x * lax.rsqrt(ms + EPS) * g

        qn_ref[...] = norm(q_ref[...], qg_ref[...]).astype(BF16)
        kn_ref[...] = norm(k_ref[...], kg_ref[...]).astype(BF16)
        vb_ref[...] = v_ref[...].astype(BF16)

    blk = lambda col: pl.BlockSpec((rows, d), lambda i: (i, col))
    vec = pl.BlockSpec((1, d), lambda i: (0, 0))
    return pl.pallas_call(
        body,
        out_shape=[jax.ShapeDtypeStruct((p, d), BF16)] * 3,
        grid=(p // rows,),
        in_specs=[blk(0), blk(1), blk(2), vec, vec, pl.BlockSpec((LANES, LANES), lambda i: (0, 0))],
        out_specs=[blk(0)] * 3,
        compiler_params=_params("parallel"),
        name=name,
    )(qkv, qkv, qkv, qg, kg, _group_matrix())


def _qknorm_bwd(qkv, qg, kg, dqn, dkn, dv, *, name):
    p = qkv.shape[0]
    d = D_MODEL
    rows = _pick(p, (384, 128, 96))

    def body(q_ref, k_ref, qg_ref, kg_ref, gm_ref, dqn_ref, dkn_ref, dv_ref, dqkv_ref, dqg_ref, dkg_ref):
        gm = gm_ref[...]

        def bwd(x, g, dy):
            ms = _head_sums(x * x, gm) * (1.0 / SB_HEAD_DIM)
            r = lax.rsqrt(ms + EPS)
            gdy = dy * g
            proj = _head_sums(x * gdy, gm) * (1.0 / SB_HEAD_DIM)
            return r * gdy - x * (r * r * r) * proj, jnp.sum(dy * x * r, axis=0, keepdims=True)

        dq, dqg = bwd(q_ref[...], qg_ref[...], dqn_ref[...])
        dk, dkg = bwd(k_ref[...], kg_ref[...], dkn_ref[...])
        dqkv_ref[:, 0:d] = dq.astype(BF16)
        dqkv_ref[:, d:2 * d] = dk.astype(BF16)
        dqkv_ref[:, 2 * d:3 * d] = dv_ref[...].astype(BF16)

        @pl.when(pl.program_id(0) == 0)
        def _():
            dqg_ref[...] = dqg
            dkg_ref[...] = dkg

        @pl.when(pl.program_id(0) > 0)
        def _():
            dqg_ref[...] += dqg
            dkg_ref[...] += dkg

    blk = lambda col: pl.BlockSpec((rows, d), lambda i: (i, col))
    vec = pl.BlockSpec((1, d), lambda i: (0, 0))
    return pl.pallas_call(
        body,
        out_shape=[jax.ShapeDtypeStruct((p, 3 * d), BF16), jax.ShapeDtypeStruct((1, d), F32),
                   jax.ShapeDtypeStruct((1, d), F32)],
        grid=(p // rows,),
        in_specs=[blk(0), blk(1), vec, vec, pl.BlockSpec((LANES, LANES), lambda i: (0, 0)), blk(0), blk(0), blk(0)],
        out_specs=[pl.BlockSpec((rows, 3 * d), lambda i: (i, 0)), vec, vec],
        compiler_params=_params("arbitrary"),
        name=name,
    )(qkv, qkv, qg, kg, _group_matrix(), dqn, dkn, dv)


SB_PAIR = 2 * SB_HEAD_DIM
SB_GROUP = 8
SB_MASKED = -1e30


def _sb_consts():
    lane = lax.broadcasted_iota(jnp.int32, (CHUNK, SB_PAIR), 1)
    r = lax.broadcasted_iota(jnp.int32, (CHUNK, CHUNK), 0)
    c = lax.broadcasted_iota(jnp.int32, (CHUNK, CHUNK), 1)
    lo = (lane < SB_HEAD_DIM).astype(F32).astype(BF16)
    ones = jnp.ones((CHUNK, CHUNK), BF16)
    twice = lambda m: jnp.concatenate([jnp.concatenate([m, ones], axis=1)] * 2, axis=0)
    later, earlier = twice((r > c).astype(BF16)), twice((r < c).astype(BF16))
    not_before = (c >= r).astype(F32) * SB_MASKED
    padding = (c < PAD_FRONT).astype(F32) * SB_MASKED
    return (lo, 1.0 - lo), c, later, earlier, not_before, padding


def _sb_halves(t, head_lanes):
    return t * head_lanes[0], t * head_lanes[1]


def _sb_logits(qh, kg, biases):
    z = _dot_nt(qh, kg)
    tiles = []
    for b, bias in enumerate(biases):
        zt = z[:, b * CHUNK:(b + 1) * CHUNK]
        if bias is not None:
            zt = zt + bias
        ls_pos = jnp.minimum(zt, 0.0) - jnp.log(1.0 + jnp.exp(-jnp.abs(zt)))
        tiles.append((ls_pos, ls_pos - zt))
    return tiles


def _sb_block_sums(tiles, m):
    st = jnp.concatenate(tiles, axis=0)
    hi = st.astype(BF16)
    lo = (st - hi.astype(F32)).astype(BF16)
    tot = _dot(jnp.concatenate([hi, lo], axis=1), m)
    return [(tot[i * CHUNK:(i + 1) * CHUNK, 0:CHUNK], tot[i * CHUNK:(i + 1) * CHUNK, CHUNK:2 * CHUNK])
            for i in range(len(tiles))]


def _sb_plan(qi, padding, not_before):
    top = lax.div(qi, SB_GROUP)
    size = qi - SB_GROUP * top + 1

    def masks(n_b):
        pad_if_first = padding * (top == 0).astype(F32)
        m = [None] * n_b
        m[n_b - 1] = not_before
        m[0] = pad_if_first if m[0] is None else m[0] + pad_if_first
        return m

    return top, size, masks


def _once_if(cond, fn, carry):
    return lax.fori_loop(0, jnp.where(cond, 1, 0), lambda s, cr: fn(cr), carry)


def _sb_head_rows(tg, lanes, n_b):
    return jnp.concatenate([tg[b * CHUNK:(b + 1) * CHUNK] * lanes for b in range(n_b)], axis=0)


def _sb_fwd(qn, kn, vb, *, name):
    p = qn.shape[0]
    n_blocks = p // CHUNK
    n_pairs = SB_HEADS // 2
    scale = SB_HEAD_DIM ** -0.5

    def body(q_ref, k_ref, v_ref, o_ref, car_ref):
        head_lanes, c, later, _, not_before, padding = _sb_consts()

        def q_block(qi, _):
            rows = pl.ds(pl.multiple_of(qi * CHUNK, CHUNK), CHUNK)
            qh = _sb_halves(q_ref[rows, :], head_lanes)
            qs = (qh[0] * scale, qh[1] * scale)

            def blocks(kb0, biases, carry):
                n_b = len(biases)
                acc, run0, run1, sav0, sav1 = carry
                krows = pl.ds(pl.multiple_of(kb0 * CHUNK, CHUNK), n_b * CHUNK)
                kg, vg = k_ref[krows, :], v_ref[krows, :]
                tiles = [_sb_logits(qs[h], kg, biases) for h in range(2)]
                sums = [_sb_block_sums([log_keep for _, log_keep in tiles[h]], later) for h in range(2)]
                cols = [(c == kb0 + b).astype(F32) for b in range(n_b)]
                runs, savs = [run0, run1], [sav0, sav1]
                for h in range(2):
                    ws = [None] * n_b
                    for b in reversed(range(n_b)):
                        after, row_sum = sums[h][b]
                        ws[b] = jnp.exp(tiles[h][b][0] + after + runs[h]).astype(BF16)
                        savs[h] = savs[h] + cols[b] * runs[h]
                        runs[h] = runs[h] + row_sum
                    acc = acc + _dot(jnp.concatenate(ws, axis=1), _sb_head_rows(vg, head_lanes[h], n_b))
                return acc, runs[0], runs[1], savs[0], savs[1]

            zt = qh[0].astype(F32) * 0.0
            top, size, masks = _sb_plan(qi, padding, not_before)
            carry = (zt, zt, zt, zt, zt)
            for n_b in range(1, SB_GROUP + 1):
                carry = _once_if(size == n_b, lambda cr, n_b=n_b: blocks(SB_GROUP * top, masks(n_b), cr), carry)
            carry = lax.fori_loop(0, jnp.maximum(top - 1, 0),
                                  lambda it, cr: blocks(SB_GROUP * (top - 1 - it), [None] * SB_GROUP, cr), carry)
            carry = _once_if(top > 0, functools.partial(blocks, 0, [padding] + [None] * (SB_GROUP - 1)), carry)
            acc, _, _, sav0, sav1 = carry
            o_ref[rows, :] = acc.astype(o_ref.dtype)
            car_ref[rows, 0:CHUNK] = sav0
            car_ref[rows, CHUNK:2 * CHUNK] = sav1
            return 0

        lax.fori_loop(0, n_blocks, q_block, 0)

    col = pl.BlockSpec((p, SB_PAIR), lambda g: (0, g))
    return pl.pallas_call(
        body,
        out_shape=[jax.ShapeDtypeStruct((p, D_MODEL), BF16), jax.ShapeDtypeStruct((p, n_pairs * 2 * CHUNK), F32)],
        grid=(n_pairs,),
        in_specs=[col, col, col],
        out_specs=[col, pl.BlockSpec((p, 2 * CHUNK), lambda g: (0, g))],
        compiler_params=_params("parallel"),
        name=name,
    )(qn, kn, vb)


def _sb_bwd(qn, kn, vb, carries, do, *, name):
    p = qn.shape[0]
    n_blocks = p // CHUNK
    n_pairs = SB_HEADS // 2
    scale = SB_HEAD_DIM ** -0.5

    def body(q_ref, k_ref, v_ref, car_ref, do_ref, dq_ref, dk_ref, dv_ref):
        head_lanes, c, later, earlier, not_before, padding = _sb_consts()
        dk_ref[...] = jnp.zeros_like(dk_ref)
        dv_ref[...] = jnp.zeros_like(dv_ref)

        def q_block(qi, _):
            rows = pl.ds(pl.multiple_of(qi * CHUNK, CHUNK), CHUNK)
            qh = _sb_halves(q_ref[rows, :], head_lanes)
            qs = (qh[0] * scale, qh[1] * scale)
            doh = _sb_halves(do_ref[rows, :].astype(BF16), head_lanes)
            do2, q2 = jnp.concatenate(doh, axis=0), jnp.concatenate(qs, axis=0)
            sav = (car_ref[rows, 0:CHUNK], car_ref[rows, CHUNK:2 * CHUNK])

            def blocks(kb0, biases, carry):
                n_b = len(biases)
                dq_acc, pre0, pre1 = carry
                krows = pl.ds(pl.multiple_of(kb0 * CHUNK, CHUNK), n_b * CHUNK)
                kg, vg = k_ref[krows, :], v_ref[krows, :]
                cols = [(c == kb0 + b).astype(F32) for b in range(n_b)]
                block = lambda t, b: t[:, b * CHUNK:(b + 1) * CHUNK]
                tiles = [_sb_logits(qs[h], kg, biases) for h in range(2)]
                afters = [_sb_block_sums([log_keep for _, log_keep in tiles[h]], later) for h in range(2)]
                dws = [_dot_nt(doh[h], vg) for h in range(2)]
                ws, es, befores = [], [], []
                for h in range(2):
                    runs = [jnp.sum(cols[b] * sav[h], axis=-1, keepdims=True) for b in range(n_b)]
                    ws.append([jnp.exp(tiles[h][b][0] + afters[h][b][0] + runs[b]) for b in range(n_b)])
                    es.append([ws[h][b] * block(dws[h], b) for b in range(n_b)])
                    befores.append(_sb_block_sums(es[h], earlier))
                pres = [pre0, pre1]
                dz2, w2 = [], []
                for h in range(2):
                    dzs = []
                    for b in range(n_b):
                        before, row_sum = befores[h][b]
                        sig = jnp.exp(tiles[h][b][0])
                        e = es[h][b]
                        dzs.append((e - (e + before + pres[h]) * sig).astype(BF16))
                        pres[h] = pres[h] + row_sum
                    dz2.append(jnp.concatenate(dzs, axis=1))
                    w2.append(jnp.concatenate([t.astype(BF16) for t in ws[h]], axis=1))
                    dq_acc = dq_acc + _dot(dz2[h], _sb_head_rows(kg, head_lanes[h], n_b))
                dv_ref[krows, :] += _dot_tn(jnp.concatenate(w2, axis=0), do2)
                dk_ref[krows, :] += _dot_tn(jnp.concatenate(dz2, axis=0), q2)
                return dq_acc, pres[0], pres[1]

            zt = qh[0].astype(F32) * 0.0
            top, size, masks = _sb_plan(qi, padding, not_before)
            carry = _once_if(top > 0, functools.partial(blocks, 0, [padding] + [None] * (SB_GROUP - 1)), (zt, zt, zt))
            carry = lax.fori_loop(1, top, lambda g, cr: blocks(SB_GROUP * g, [None] * SB_GROUP, cr), carry)
            for n_b in range(1, SB_GROUP + 1):
                carry = _once_if(size == n_b, lambda cr, n_b=n_b: blocks(SB_GROUP * top, masks(n_b), cr), carry)
            dq_acc, _, _ = carry
            dq_ref[rows, :] = dq_acc * scale
            return 0

        lax.fori_loop(0, n_blocks, q_block, 0)

    col = pl.BlockSpec((p, SB_PAIR), lambda g: (0, g))
    return pl.pallas_call(
        body,
        out_shape=[jax.ShapeDtypeStruct((p, D_MODEL), F32)] * 3,
        grid=(n_pairs,),
        in_specs=[col, col, col, pl.BlockSpec((p, 2 * CHUNK), lambda g: (0, g)), col],
        out_specs=[col, col, col],
        compiler_params=_params("parallel"),
        name=name,
    )(qn, kn, vb, carries, do)


def _loss_head(h, target, *, name):
    p, d = h.shape
    n_blocks = p // CHUNK

    def body(h_ref, t_ref, sq_ref, dh_ref):
        i = pl.program_id(0)

        @pl.when(i == 0)
        def _():
            sq_ref[...] = jnp.zeros_like(sq_ref)
            dh_ref[...] = jnp.zeros_like(dh_ref)

        @pl.when(i > 0)
        def _():
            err = h_ref[...] - t_ref[...]
            sq_ref[...] += jnp.sum(err * err)
            dh_ref[...] = err * (1.0 / d)

    return pl.pallas_call(
        body,
        out_shape=[jax.ShapeDtypeStruct((8, 128), F32), jax.ShapeDtypeStruct((p, d), F32)],
        grid=(n_blocks,),
        in_specs=[pl.BlockSpec((CHUNK, d), lambda i: (i, 0)),
                  pl.BlockSpec((CHUNK, d), lambda i: (jnp.maximum(i - 1, 0), 0))],
        out_specs=[pl.BlockSpec((8, 128), lambda i: (0, 0)), pl.BlockSpec((CHUNK, d), lambda i: (i, 0))],
        compiler_params=_params("arbitrary"),
        name=name,
    )(h, target)


def _local_step(x, target, meta, norm_mix_g, norm_mlp_g, w_in, gn_g, conv_w, conv_b, ln_g, ln_b, qn_g, kn_g, later,
                reached=lambda point, after, grads=None: None):
    seq = x.shape[0]
    p = PAD_FRONT + N_META + seq
    d = D_MODEL
    tables = _retention_tables(p)
    row = lambda v: v.reshape(1, -1)
    h0 = jnp.concatenate([jnp.zeros((PAD_FRONT, d), F32), meta, x], axis=0)

    hn0 = _rmsnorm_fwd(h0, row(norm_mix_g[0]), name="l0_mix_norm")
    proj = _matmul(hn0, w_in, mode="nn", out_dtypes=(F32,), name="l0_proj")
    og, opre, sprev = _retention_fwd(proj, gn_g, tables, name="l0_retention")
    cb, y_conv = _conv_fwd(proj, conv_w, row(conv_b), row(ln_g), row(ln_b), name="l0_conv")
    cat = jnp.concatenate([og, cb], axis=1)
    w_out, w1_0, w2_0 = later("l0", cat)
    w1, w2 = [w1_0, None], [w2_0, None]
    h1 = _matmul(cat, w_out, mode="nn", out_dtypes=(F32,), epilogue=_add_epilogue, extras=(h0,), name="l0_mix_out")
    h2, mlp0 = _mlp_fwd(h1, row(norm_mlp_g[0]), w1[0], w2[0], name="l0_mlp")

    hn1 = _rmsnorm_fwd(h2, row(norm_mix_g[1]), name="l1_mix_norm")
    (w_qkv,) = later("qkv", hn1)
    qkv = _matmul(hn1, w_qkv, mode="nn", out_dtypes=(F32,), name="l1_qkv")
    qg_t, kg_t = jnp.tile(row(qn_g), (1, SB_HEADS)), jnp.tile(row(kn_g), (1, SB_HEADS))
    qn, kn, vb = _qknorm_fwd(qkv, qg_t, kg_t, name="l1_qknorm")
    o_sb, carries = _sb_fwd(qn, kn, vb, name="l1_stickbreak")
    w_o, w1[1], w2[1] = later("l1", o_sb)
    h3 = _matmul(o_sb, w_o, mode="nn", out_dtypes=(F32,), epilogue=_add_epilogue, extras=(h2,), name="l1_mix_out")
    h4, mlp1 = _mlp_fwd(h3, row(norm_mlp_g[1]), w1[1], w2[1], name="l1_mlp")

    sq, dh4 = _loss_head(h4, target, name="loss_head")

    dh3, dg_mlp1, dw1_1, dw2_1 = _mlp_bwd(h3, row(norm_mlp_g[1]), w1[1], w2[1], mlp1, dh4, name="l1_mlp_bwd")
    do_sb = _matmul(dh3, w_o, mode="nt", out_dtypes=(F32,), name="l1_do")
    dw_o = _matmul(o_sb, dh3, mode="tn", out_dtypes=(F32,), name="l1_dwo")
    dqn, dkn, dv = _sb_bwd(qn, kn, vb, carries, do_sb, name="l1_stickbreak_bwd")
    dqkv, dqg_t, dkg_t = _qknorm_bwd(qkv, qg_t, kg_t, dqn, dkn, dv, name="l1_qknorm_bwd")
    dw_qkv = _matmul(hn1, dqkv, mode="tn", out_dtypes=(F32,), name="l1_dwqkv")
    pin = lambda arr, tok: arr if tok is None else arr + tok[0:1, 0:1]
    tok = reached("l1_grads", dw_qkv, dict(odd_w_qkv=dw_qkv, odd_w_o=dw_o, mlp_w1_1=dw1_1, mlp_w2_1=dw2_1))
    dhn1 = _matmul(dqkv, w_qkv, mode="nt", out_dtypes=(F32,), name="l1_dhn", after=tok)
    dh2, dg_mix1 = _rmsnorm_bwd(h2, row(norm_mix_g[1]), dhn1, dh3, name="l1_mix_dnorm")
    tok = reached("l1_done", dh2)

    dh1, dg_mlp0, dw1_0, dw2_0 = _mlp_bwd(h1, row(norm_mlp_g[0]), w1[0], w2[0], mlp0, dh2, name="l0_mlp_bwd", after=tok)
    tok = reached("l0_mlp_grads", dh1, dict(mlp_w1_0=dw1_0, mlp_w2_0=dw2_0))
    dcat = _matmul(dh1, w_out, mode="nt", out_dtypes=(F32,), name="l0_dcat", after=tok)
    dw_out = _matmul(cat, dh1, mode="tn", out_dtypes=(F32,), name="l0_dwout")
    tok = reached("l0_dwout", dw_out, dict(even_w_out=dw_out))
    dq, dk, dvr, dgate_r, dgn = _retention_bwd(proj, pin(gn_g, tok), tables, opre, sprev, dcat, name="l0_retention_bwd")
    tok = reached("l0_retention_bwd", dq)
    da, dgate_c, dconv_w, dconv_b, dln_g, dln_b = _conv_bwd(proj, conv_w, pin(row(ln_g), tok), row(ln_b), y_conv, dcat,
                                                            name="l0_conv_bwd")
    tok = reached("l0_conv_bwd", da)
    dproj = jnp.concatenate([dq, dk, dvr, dgate_r, da, dgate_c], axis=1)
    dw_in = _matmul(hn0, dproj, mode="tn", out_dtypes=(F32,), name="l0_dwin", after=tok)
    dhn0 = _matmul(dproj, w_in, mode="nt", out_dtypes=(F32,), name="l0_dhn")
    dh0, dg_mix0 = _rmsnorm_bwd(h0, row(norm_mix_g[0]), dhn0, dh1, name="l0_mix_dnorm")

    fold = lambda t: t.reshape(SB_HEADS, SB_HEAD_DIM).sum(axis=0)
    grads = dict(
        x=dh0[PAD_FRONT + N_META:],
        meta=dh0[PAD_FRONT:PAD_FRONT + N_META],
        norm_mix_g=jnp.concatenate([dg_mix0, dg_mix1], axis=0),
        norm_mlp_g=jnp.concatenate([dg_mlp0, dg_mlp1], axis=0),
        even_w_in=dw_in,
        even_ret_gn_g=dgn.reshape(RET_HEADS, RET_V_DIM),
        even_conv_w=dconv_w,
        even_conv_b=dconv_b,
        even_conv_ln_g=dln_g,
        even_conv_ln_b=dln_b,
        even_w_out=dw_out,
        odd_w_qkv=dw_qkv,
        odd_q_norm_g=fold(dqg_t)[None],
        odd_k_norm_g=fold(dkg_t)[None],
        odd_w_o=dw_o,
        mlp_w1=(dw1_0, dw1_1),
        mlp_w2=(dw2_0, dw2_1),
    )
    return sq[0, 0], grads


def _position():
    x, y, c = lax.axis_index("x"), lax.axis_index("y"), lax.axis_index("c")
    other_chips = [(1 - x, y), (x, 1 - y), (1 - x, 1 - y)]
    return x, y, c, other_chips


def _shard_of(ref, kind, s, n):
    rows, cols = ref.shape
    if kind == "col":
        return ref.at[:, pl.ds(s * (cols // n), cols // n)]
    return ref.at[pl.ds(s * (rows // n), rows // n), :]


def _half_of(ref, kind, c):
    rows, cols = ref.shape
    if kind == "col":
        return ref.at[pl.ds(c * (rows // 2), rows // 2), :]
    return ref.at[:, pl.ds(c * (cols // 2), cols // 2)]


def _remote(src, dst, send_sems, recv_sems, idx, device):
    return pltpu.make_async_remote_copy(src_ref=src, dst_ref=dst, send_sem=send_sems.at[idx], recv_sem=recv_sems.at[idx],
                                        device_id=device, device_id_type=MESH)


def _cast_into_whole(w, kind, s_arr, *, name):
    rows, cols = w.shape
    tr = _pick(rows, (256, 128))
    nb = rows // tr
    if kind == "col":
        whole, o_spec = (rows, cols * N_CHIPS), pl.BlockSpec((tr, cols), lambda i, s_ref: (i, s_ref[0]))
    else:
        whole, o_spec = (rows * N_CHIPS, cols), pl.BlockSpec((tr, cols), lambda i, s_ref: (s_ref[0] * nb + i, 0))

    def body(s_ref, w_ref, o_ref):
        o_ref[...] = w_ref[...].astype(BF16)

    return pl.pallas_call(
        body,
        out_shape=jax.ShapeDtypeStruct(whole, BF16),
        grid_spec=pltpu.PrefetchScalarGridSpec(num_scalar_prefetch=1, grid=(nb,),
                                               in_specs=[pl.BlockSpec((tr, cols), lambda i, s_ref: (i, 0))],
                                               out_specs=o_spec),
        compiler_params=_params("parallel"),
        name=name,
    )(s_arr, w)


def _allgather_weights(wholes, kinds):
    n = len(wholes)

    def body(*refs):
        ins, outs = refs[:n], refs[n:2 * n]
        send_sems, recv_sems = refs[2 * n:]
        x, y, c, chips = _position()
        me_chip = 2 * x + y
        sibling = (x, y, 1 - c)
        sends = []
        for t in range(n):
            for k, (cx, cy) in enumerate(chips):
                src = _half_of(_shard_of(ins[t], kinds[t], me_chip, N_CHIPS), kinds[t], c)
                dst = _half_of(_shard_of(outs[t], kinds[t], me_chip, N_CHIPS), kinds[t], c)
                sends.append(_remote(src, dst, send_sems, recv_sems, 6 * t + k, (cx, cy, c)))
        for cp in sends:
            cp.start()
        passed = []
        for t in range(n):
            for k, (cx, cy) in enumerate(chips):
                landed = _half_of(_shard_of(outs[t], kinds[t], 2 * cx + cy, N_CHIPS), kinds[t], c)
                _remote(landed, landed, send_sems, recv_sems, 6 * t + k, (cx, cy, c)).wait_recv()
                fwd = _remote(landed, landed, send_sems, recv_sems, 6 * t + 3 + k, sibling)
                fwd.start()
                passed.append(fwd)
        for t in range(n):
            for k, (cx, cy) in enumerate(chips):
                theirs = _half_of(_shard_of(outs[t], kinds[t], 2 * cx + cy, N_CHIPS), kinds[t], 1 - c)
                _remote(theirs, theirs, send_sems, recv_sems, 6 * t + 3 + k, sibling).wait_recv()
        for cp in sends + passed:
            cp.wait_send()

    return pl.pallas_call(
        body,
        out_shape=[jax.ShapeDtypeStruct(w.shape, BF16) for w in wholes],
        in_specs=[ANY] * n,
        out_specs=[ANY] * n,
        input_output_aliases={t: t for t in range(n)},
        scratch_shapes=[pltpu.SemaphoreType.DMA((6 * n,)), pltpu.SemaphoreType.DMA((6 * n,))],
        name="allgather_weights",
    )(*wholes)


HBM = pl.BlockSpec(memory_space=pltpu.HBM)
SEM = pl.BlockSpec(memory_space=pltpu.SEMAPHORE)
DATAFLOW = pltpu.SideEffectType.DATAFLOW_SIDE_EFFECTING
TARGETS = 6


def _gather_copies(kinds, refs, _, send_sems, recv_sems):
    x, y, c, chips = _position()
    me_chip = 2 * x + y
    sends, lands = [], []
    for t, (ref, kind) in enumerate(zip(refs, kinds)):
        mine = _half_of(_shard_of(ref, kind, me_chip, N_CHIPS), kind, c)
        for k, (cx, cy) in enumerate(chips):
            for other_core in range(2):
                j = TARGETS * t + 2 * k + other_core
                peer_c = 1 - c if other_core else c
                sends.append(_remote(mine, mine, send_sems, recv_sems, j, (cx, cy, peer_c)))
                theirs = _half_of(_shard_of(ref, kind, 2 * cx + cy, N_CHIPS), kind, peer_c)
                lands.append(_remote(theirs, theirs, send_sems, recv_sems, j, (cx, cy, peer_c)))
    return sends, lands


def _pair_swap_copies(kinds, srcs, lands, send_sems, recv_sems):
    x, y, c, _ = _position()
    sibling = (x, y, 1 - c)
    sends = [_remote(_half_of(srcs[t], kinds[t], 1 - c), lands[t], send_sems, recv_sems, t, sibling) for t in range(len(srcs))]
    arrivals = [_remote(_half_of(srcs[t], kinds[t], c), lands[t], send_sems, recv_sems, t, sibling) for t in range(len(srcs))]
    return sends, arrivals


def _chip_exchange_copies(kinds, srcs, lands, send_sems, recv_sems):
    x, y, c, chips = _position()
    sends, arrivals = [], []
    for t in range(len(srcs)):
        for k, (cx, cy) in enumerate(chips):
            src = _shard_of(srcs[t], kinds[t], 2 * cx + cy, N_CHIPS)
            sends.append(_remote(src, lands[t].at[k], send_sems, recv_sems, 3 * t + k, (cx, cy, c)))
            arrivals.append(_remote(src, lands[t].at[k], send_sems, recv_sems, 3 * t + k, (cx, cy, c)))
    return sends, arrivals


def _pair_gather_copies(kinds, srcs, lands, send_sems, recv_sems):
    x, y, c, _ = _position()
    sibling = (x, y, 1 - c)
    sends, arrivals = [], []
    for t in range(len(srcs)):
        mine, theirs = _half_of(srcs[t], kinds[t], c), _half_of(srcs[t], kinds[t], 1 - c)
        sends.append(_remote(mine, mine, send_sems, recv_sems, t, sibling))
        arrivals.append(_remote(theirs, theirs, send_sems, recv_sems, t, sibling))
    return sends, arrivals


def _copies_start(plan, n_sems, srcs, lands, follows, *, name):
    ns, n = len(srcs), len(srcs) + len(lands)

    def body(*refs):
        send_sems, recv_sems = refs[n + 1], refs[n + 2]
        thru, token = refs[n + 3:2 * n + 3], refs[2 * n + 3]
        sends, _ = plan(thru[:ns], thru[ns:], send_sems, recv_sems)
        for cp in sends:
            cp.start()
        token[...] = jnp.zeros_like(token)

    arrays = [pltpu.with_memory_space_constraint(a, pltpu.HBM) for a in list(srcs) + list(lands)]
    outs = pl.pallas_call(
        body,
        name=name,
        out_shape=(pltpu.SemaphoreType.DMA((n_sems,)), pltpu.SemaphoreType.DMA((n_sems,)),
                   *[pltpu.HBM(a.shape, a.dtype) for a in arrays], jax.ShapeDtypeStruct((8, 128), F32)),
        in_specs=(*[HBM] * n, ANY),
        out_specs=(SEM, SEM, *[HBM] * n, pl.BlockSpec(memory_space=pltpu.VMEM)),
        input_output_aliases={t: 2 + t for t in range(n)},
        compiler_params=pltpu.CompilerParams(has_side_effects=DATAFLOW),
    )(*arrays, follows)
    return outs[0], outs[1], list(outs[2:2 + ns]), list(outs[2 + ns:2 + n]), outs[2 + n]


def _copies_wait(plan, started, follows, *, name):
    send_sems, recv_sems, srcs, lands, _ = started
    ns, n = len(srcs), len(srcs) + len(lands)

    def body(*refs):
        ins, s_sems, r_sems = refs[:n], refs[n], refs[n + 1]
        sends, arrivals = plan(ins[:ns], ins[ns:], s_sems, r_sems)
        for cp in sends:
            cp.wait_send()
        for cp in arrivals:
            cp.wait_recv()

    outs = pl.pallas_call(
        body,
        name=name,
        out_shape=tuple(pltpu.HBM(a.shape, a.dtype) for a in srcs + lands),
        in_specs=(*[HBM] * n, SEM, SEM, ANY),
        out_specs=tuple([HBM] * n),
        input_output_aliases={t: t for t in range(n)},
        compiler_params=pltpu.CompilerParams(has_side_effects=DATAFLOW),
    )(*srcs, *lands, send_sems, recv_sems, follows)
    return list(outs[:ns]), list(outs[ns:])


def _allgather8(block, *, name):
    rows, cols = block.shape

    def body(in_ref, out_ref, send_sems, recv_sems, local_sem):
        x, y, c, _ = _position()
        me = 4 * x + 2 * y + c
        mine = pltpu.make_async_copy(in_ref, out_ref.at[me], local_sem)
        mine.start()
        peers = []
        for flip in range(1, N_DEV):
            fx, fy, fc = (flip >> 2) & 1, (flip >> 1) & 1, flip & 1
            peers.append(((1 - x if fx else x), (1 - y if fy else y), (1 - c if fc else c)))
        sends = [_remote(in_ref, out_ref.at[me], send_sems, recv_sems, j, peer) for j, peer in enumerate(peers)]
        for cp in sends:
            cp.start()
        for j, (px, py, pc) in enumerate(peers):
            slot = out_ref.at[4 * px + 2 * py + pc]
            _remote(slot, slot, send_sems, recv_sems, j, (px, py, pc)).wait_recv()
        for cp in sends:
            cp.wait_send()
        mine.wait()

    vmem = pl.BlockSpec(memory_space=pltpu.VMEM)
    return pl.pallas_call(
        body,
        out_shape=jax.ShapeDtypeStruct((N_DEV, rows, cols), F32),
        in_specs=[vmem],
        out_specs=vmem,
        scratch_shapes=[pltpu.SemaphoreType.DMA((N_DEV - 1,)), pltpu.SemaphoreType.DMA((N_DEV - 1,)),
                        pltpu.SemaphoreType.DMA],
        name=name,
    )(block)


def _sum8(stack, *, name):
    _, rows, cols = stack.shape

    def body(s_ref, o_ref):
        acc = s_ref[0]
        for i in range(1, N_DEV):
            acc = acc + s_ref[i]
        o_ref[...] = acc

    return pl.pallas_call(body, out_shape=jax.ShapeDtypeStruct((rows, cols), F32), name=name)(stack)


def _swap_halves_in(grads, kinds):
    n = len(grads)

    def body(*refs):
        ins, outs = refs[:n], refs[n:2 * n]
        send_sems, recv_sems = refs[2 * n:]
        x, y, c, _ = _position()
        sibling = (x, y, 1 - c)
        sends = [_remote(_half_of(ins[t], kinds[t], 1 - c), outs[t], send_sems, recv_sems, t, sibling) for t in range(n)]
        for cp in sends:
            cp.start()
        for t in range(n):
            _remote(_half_of(ins[t], kinds[t], c), outs[t], send_sems, recv_sems, t, sibling).wait_recv()
        for cp in sends:
            cp.wait_send()

    def half(g, kind):
        rows, cols = g.shape
        return (rows // 2, cols) if kind == "col" else (rows, cols // 2)

    return pl.pallas_call(
        body,
        out_shape=[jax.ShapeDtypeStruct(half(g, k), F32) for g, k in zip(grads, kinds)],
        in_specs=[ANY] * n,
        out_specs=[ANY] * n,
        scratch_shapes=[pltpu.SemaphoreType.DMA((n,)), pltpu.SemaphoreType.DMA((n,))],
        name="reduce_core_pair",
    )(*grads)


def _half_add(grad, theirs, kind, c_arr, *, name):
    rows, cols = theirs.shape
    tr = _pick(rows, (256, 128))
    nb = rows // tr
    if kind == "col":
        g_spec = pl.BlockSpec((tr, cols), lambda i, c_ref: (c_ref[0] * nb + i, 0))
    else:
        g_spec = pl.BlockSpec((tr, cols), lambda i, c_ref: (i, c_ref[0]))
    t_spec = pl.BlockSpec((tr, cols), lambda i, c_ref: (i, 0))

    def body(c_ref, g_ref, t_ref, o32_ref, o16_ref):
        tot = g_ref[...] + t_ref[...]
        o32_ref[...] = tot
        o16_ref[...] = tot.astype(BF16)

    return pl.pallas_call(
        body,
        out_shape=[jax.ShapeDtypeStruct((rows, cols), F32), jax.ShapeDtypeStruct((rows, cols), BF16)],
        grid_spec=pltpu.PrefetchScalarGridSpec(num_scalar_prefetch=1, grid=(nb,), in_specs=[g_spec, t_spec],
                                               out_specs=[t_spec, t_spec]),
        compiler_params=_params("parallel"),
        name=name,
    )(c_arr, grad, theirs)


def _exchange_chips(parts, kinds):
    n = len(parts)

    def body(*refs):
        ins, outs = refs[:n], refs[n:2 * n]
        send_sems, recv_sems = refs[2 * n:]
        x, y, c, chips = _position()
        sends = []
        for t in range(n):
            for k, (cx, cy) in enumerate(chips):
                src = _shard_of(ins[t], kinds[t], 2 * cx + cy, N_CHIPS)
                sends.append(_remote(src, outs[t].at[k], send_sems, recv_sems, 3 * t + k, (cx, cy, c)))
        for cp in sends:
            cp.start()
        for t in range(n):
            for k, (cx, cy) in enumerate(chips):
                src = _shard_of(ins[t], kinds[t], 2 * cx + cy, N_CHIPS)
                _remote(src, outs[t].at[k], send_sems, recv_sems, 3 * t + k, (cx, cy, c)).wait_recv()
        for cp in sends:
            cp.wait_send()

    def piece(p, kind):
        rows, cols = p.shape
        return (3, rows, cols // N_CHIPS) if kind == "col" else (3, rows // N_CHIPS, cols)

    return pl.pallas_call(
        body,
        out_shape=[jax.ShapeDtypeStruct(piece(p, k), BF16) for p, k in zip(parts, kinds)],
        in_specs=[ANY] * n,
        out_specs=[ANY] * n,
        scratch_shapes=[pltpu.SemaphoreType.DMA((3 * n,)), pltpu.SemaphoreType.DMA((3 * n,))],
        name="reduce_chips",
    )(*parts)


def _shard_sum(part32, recv, kind, sc_arr, *, name):
    _, rows, cols = recv.shape
    tr = _pick(rows, (256, 128))
    nb = rows // tr
    if kind == "col":
        whole = (2 * rows, cols)
        p_spec = pl.BlockSpec((tr, cols), lambda i, sc: (i, sc[0]))
        o_spec = pl.BlockSpec((tr, cols), lambda i, sc: (sc[1] * nb + i, 0))
    else:
        whole = (rows, 2 * cols)
        p_spec = pl.BlockSpec((tr, cols), lambda i, sc: (sc[0] * nb + i, 0))
        o_spec = pl.BlockSpec((tr, cols), lambda i, sc: (i, sc[1]))
    r_spec = pl.BlockSpec((3, tr, cols), lambda i, sc: (0, i, 0))

    def body(sc_ref, p_ref, r_ref, o_ref):
        acc = p_ref[...]
        for k in range(3):
            acc = acc + r_ref[k].astype(F32)
        o_ref[...] = acc

    return pl.pallas_call(
        body,
        out_shape=jax.ShapeDtypeStruct(whole, F32),
        grid_spec=pltpu.PrefetchScalarGridSpec(num_scalar_prefetch=1, grid=(nb,), in_specs=[p_spec, r_spec],
                                               out_specs=o_spec),
        compiler_params=_params("parallel"),
        name=name,
    )(sc_arr, part32, recv)


def _swap_halves_out(shards, kinds):
    n = len(shards)

    def body(*refs):
        ins, outs = refs[:n], refs[n:2 * n]
        send_sems, recv_sems = refs[2 * n:]
        x, y, c, _ = _position()
        sibling = (x, y, 1 - c)
        sends = [_remote(_half_of(ins[t], kinds[t], c), _half_of(outs[t], kinds[t], c), send_sems, recv_sems, t, sibling)
                 for t in range(n)]
        for cp in sends:
            cp.start()
        for t in range(n):
            theirs = _half_of(outs[t], kinds[t], 1 - c)
            _remote(theirs, theirs, send_sems, recv_sems, t, sibling).wait_recv()
        for cp in sends:
            cp.wait_send()

    return pl.pallas_call(
        body,
        out_shape=[jax.ShapeDtypeStruct(s.shape, F32) for s in shards],
        in_specs=[ANY] * n,
        out_specs=[ANY] * n,
        input_output_aliases={t: t for t in range(n)},
        scratch_shapes=[pltpu.SemaphoreType.DMA((n,)), pltpu.SemaphoreType.DMA((n,))],
        name="gather_core_pair",
    )(*shards)


def _adamw(w, g, m, v, *, name):
    rows, cols = w.shape
    tr = _pick(rows, (256, 128)) if rows * cols > 64 * 1024 else rows

    def body(w_ref, g_ref, m_ref, v_ref, d_ref, nm_ref, nv_ref):
        gv = g_ref[...]
        nm = ADAM_B1 * m_ref[...] + (1.0 - ADAM_B1) * gv
        nv = ADAM_B2 * v_ref[...] + (1.0 - ADAM_B2) * jnp.square(gv)
        m_hat = nm / (1.0 - ADAM_B1 ** ADAM_STEP)
        v_hat = nv / (1.0 - ADAM_B2 ** ADAM_STEP)
        d_ref[...] = -ADAM_LR * (m_hat / (jnp.sqrt(v_hat) + ADAM_EPS) + ADAM_WD * w_ref[...])
        nm_ref[...] = nm
        nv_ref[...] = nv

    spec = pl.BlockSpec((tr, cols), lambda i: (i, 0))
    return pl.pallas_call(
        body,
        out_shape=[jax.ShapeDtypeStruct((rows, cols), F32)] * 3,
        grid=(rows // tr,),
        in_specs=[spec] * 4,
        out_specs=[spec] * 3,
        compiler_params=_params("parallel"),
        name=name,
    )(w, g, m, v)


BIG = ("even_w_in", "odd_w_qkv", "mlp_w1_0", "mlp_w1_1", "even_w_out", "odd_w_o", "mlp_w2_0", "mlp_w2_1")
BIG_KIND = ("col", "col", "col", "col", "row", "row", "row", "row")


class _TravellingReduction:
    def __init__(self, tag, names, kinds, c_arr, sc_arr):
        self.tag, self.names, self.kinds, self.c_arr, self.sc_arr = tag, names, kinds, c_arr, sc_arr
        self.swap = functools.partial(_pair_swap_copies, kinds)
        self.exchange = functools.partial(_chip_exchange_copies, kinds)
        self.gather = functools.partial(_pair_gather_copies, kinds)

    def pair_swap_start(self, grads, follows):
        half = lambda g, kind: (g.shape[0] // 2, g.shape[1]) if kind == "col" else (g.shape[0], g.shape[1] // 2)
        lands = [lax.empty(half(g, k), F32) for g, k in zip(grads, self.kinds)]
        self.started = _copies_start(self.swap, len(grads), grads, lands, follows, name=f"reduce_{self.tag}_pair_start")

    def pair_swap_finish(self, after):
        grads, theirs = _copies_wait(self.swap, self.started, after, name=f"reduce_{self.tag}_pair_wait")
        self.sums = [_half_add(g, th, k, self.c_arr, name="pair_sum_" + n)
                     for g, th, k, n in zip(grads, theirs, self.kinds, self.names)]

    def chips_start(self, follows):
        parts = [s16 for _, s16 in self.sums]
        piece = lambda p, kind: (3, p.shape[0], p.shape[1] // N_CHIPS) if kind == "col" else (3, p.shape[0] // N_CHIPS, p.shape[1])
        lands = [lax.empty(piece(p, k), BF16) for p, k in zip(parts, self.kinds)]
        self.started = _copies_start(self.exchange, 3 * len(parts), parts, lands, follows,
                                     name=f"reduce_{self.tag}_chips_start")

    def chips_finish(self, after):
        _, recv = _copies_wait(self.exchange, self.started, after, name=f"reduce_{self.tag}_chips_wait")
        self.halves = [_shard_sum(s32, r, k, self.sc_arr, name="chip_sum_" + n)
                       for (s32, _), r, k, n in zip(self.sums, recv, self.kinds, self.names)]

    def pair_gather_start(self, follows):
        self.started = _copies_start(self.gather, len(self.halves), self.halves, [], follows,
                                     name=f"reduce_{self.tag}_gather_start")

    def pair_gather_finish(self, after):
        shards, _ = _copies_wait(self.gather, self.started, after, name=f"reduce_{self.tag}_gather_wait")
        return dict(zip(self.names, shards))
SUBLANES = 8


def _pack_rows(parts, width):
    padded, offsets, r0 = [], [], 0
    for t in parts:
        rows = -(-t.shape[0] // SUBLANES) * SUBLANES
        padded.append(jnp.pad(t, ((0, rows - t.shape[0]), (0, width - t.shape[1]))))
        offsets.append(r0)
        r0 += rows
    return jnp.concatenate(padded, axis=0), offsets


def kernel(x, meta, norm_mix_g, norm_mlp_g, even_w_in, even_ret_gn_g, even_conv_w, even_conv_b, even_conv_ln_g, even_conv_ln_b, even_w_out, odd_w_qkv, odd_q_norm_g, odd_k_norm_g, odd_w_o, mlp_w1, mlp_w2, loss_target, m_meta, m_norm_mix_g, m_norm_mlp_g, m_even_w_in, m_even_ret_gn_g, m_even_conv_w, m_even_conv_b, m_even_conv_ln_g, m_even_conv_ln_b, m_even_w_out, m_odd_w_qkv, m_odd_q_norm_g, m_odd_k_norm_g, m_odd_w_o, m_mlp_w1, m_mlp_w2, v_meta, v_norm_mix_g, v_norm_mlp_g, v_even_w_in, v_even_ret_gn_g, v_even_conv_w, v_even_conv_b, v_even_conv_ln_g, v_even_conv_ln_b, v_even_w_out, v_odd_w_qkv, v_odd_q_norm_g, v_odd_k_norm_g, v_odd_w_o, v_mlp_w1, v_mlp_w2):
    d = D_MODEL
    xi, yi, ci = lax.axis_index("x"), lax.axis_index("y"), lax.axis_index("c")
    chip = 2 * xi + yi
    c_arr = jnp.reshape(ci, (1,)).astype(jnp.int32)
    s_arr = jnp.reshape(chip, (1,)).astype(jnp.int32)

    def split_big(w_in, w_qkv, w1, w_out, w_o, w2):
        return dict(zip(BIG, (w_in[0], w_qkv[0], w1[0], w1[1], w_out[0], w_o[0], w2[0], w2[1])))

    w_big = split_big(even_w_in, odd_w_qkv, mlp_w1, even_w_out, odd_w_o, mlp_w2)
    m_big = split_big(m_even_w_in, m_odd_w_qkv, m_mlp_w1, m_even_w_out, m_odd_w_o, m_mlp_w2)
    v_big = split_big(v_even_w_in, v_odd_w_qkv, v_mlp_w1, v_even_w_out, v_odd_w_o, v_mlp_w2)

    placed = {n: _cast_into_whole(w_big[n], k, s_arr, name="cast_" + n) for n, k in zip(BIG, BIG_KIND)}
    kind_of = dict(zip(BIG, BIG_KIND))
    (w_in_full,) = _allgather_weights([placed["even_w_in"]], [kind_of["even_w_in"]])
    packed, (r_meta, r_conv, r_gn) = _pack_rows([meta, even_conv_w[0], even_ret_gn_g[0]], d // N_CHIPS)
    gathered = _allgather8(packed, name="allgather_small_params")[0::2]
    groups = dict(l0=("even_w_out", "mlp_w1_0", "mlp_w2_0"), qkv=("odd_w_qkv",), l1=("odd_w_o", "mlp_w1_1", "mlp_w2_1"))
    in_flight, follows = {}, gathered[0, 0:1, 0:1] + w_in_full[0:1, 0:1].astype(F32)
    for group, names in groups.items():
        plan = functools.partial(_gather_copies, [kind_of[n] for n in names])
        in_flight[group] = (plan, _copies_start(plan, TARGETS * len(names), [placed[n] for n in names], [], follows,
                                                name="gather_" + group + "_start"))
        follows = in_flight[group][1][-1]
    started = follows[0:1, 0:1]

    def later(group, after):
        plan, state = in_flight[group]
        return _copies_wait(plan, state, after, name="gather_" + group + "_wait")[0]

    sc_arr = jnp.concatenate([s_arr, c_arr])
    early = ("odd_w_qkv", "odd_w_o", "mlp_w1_1", "mlp_w2_1"), ("mlp_w1_0", "mlp_w2_0"), ("even_w_out",)
    red_l1, red_m0, red_o0 = (_TravellingReduction(tag, names, [kind_of[n] for n in names], c_arr, sc_arr)
                              for tag, names in zip(("l1", "m0", "o0"), early))
    grad_big = {}

    def reached(point, after, grads=None):
        if point == "l1_grads":
            red_l1.pair_swap_start([grads[n] for n in red_l1.names], after)
            return red_l1.started[-1]
        if point == "l1_done":
            red_l1.pair_swap_finish(after)
            red_l1.chips_start(after)
            return red_l1.started[-1]
        if point == "l0_mlp_grads":
            red_m0.pair_swap_start([grads[n] for n in red_m0.names], after)
            return red_m0.started[-1]
        if point == "l0_dwout":
            red_m0.pair_swap_finish(after)
            red_m0.chips_start(after)
            red_o0.pair_swap_start([grads[n] for n in red_o0.names], red_m0.started[-1])
            return red_o0.started[-1]
        if point == "l0_retention_bwd":
            red_l1.chips_finish(after)
            red_l1.pair_gather_start(after)
            red_o0.pair_swap_finish(after)
            red_o0.chips_start(red_l1.started[-1])
            return red_o0.started[-1]
        if point == "l0_conv_bwd":
            red_m0.chips_finish(after)
            red_m0.pair_gather_start(after)
            grad_big.update(red_l1.pair_gather_finish(after))
            red_o0.chips_finish(after)
            red_o0.pair_gather_start(red_m0.started[-1])
            return red_o0.started[-1]
        return None

    across = lambda r0, rows, width: jnp.concatenate([gathered[s, r0:r0 + rows, 0:width] for s in range(N_CHIPS)], axis=1)
    meta_full = across(r_meta, N_META, d // N_CHIPS) + started
    conv_w_full = across(r_conv, CONV_WIDTH, d // N_CHIPS)
    gn_full = across(r_gn, RET_HEADS, RET_V_DIM // N_CHIPS)

    sq, g = _local_step(
        x[0], loss_target[0], meta_full, norm_mix_g, norm_mlp_g, w_in_full, gn_full, conv_w_full,
        even_conv_b[0], even_conv_ln_g[0], even_conv_ln_b[0], odd_q_norm_g[0], odd_k_norm_g[0], later, reached)
    grad_big.update(red_m0.pair_gather_finish(g["even_w_in"]))
    grad_big.update(red_o0.pair_gather_finish(g["even_w_in"]))
    loss = lax.psum(0.5 * sq / d, ("x", "y", "c"))

    small_names = ("norm_mix_g", "norm_mlp_g", "even_conv_b", "even_conv_ln_g", "even_conv_ln_b", "odd_q_norm_g",
                   "odd_k_norm_g", "meta", "even_conv_w", "even_ret_gn_g")
    pack, offsets = _pack_rows([g[n] for n in small_names], d)
    summed = _sum8(_allgather8(pack, name="allgather_small_grads"), name="sum_small_grads")
    small = {n: summed[r0:r0 + g[n].shape[0], 0:g[n].shape[1]] for n, r0 in zip(small_names, offsets)}
    for n in ("meta", "even_conv_w", "even_ret_gn_g"):
        width = small[n].shape[1] // N_CHIPS
        small[n] = lax.dynamic_slice_in_dim(small[n], chip * width, width, axis=1)

    last = ("even_w_in",)
    last_kinds = [kind_of[n] for n in last]
    g_last = [g[n] for n in last]
    theirs = _swap_halves_in(g_last, last_kinds)
    sums = [_half_add(gb, th, k, c_arr, name="pair_sum_" + n) for gb, th, k, n in zip(g_last, theirs, last_kinds, last)]
    recv = _exchange_chips([s16 for _, s16 in sums], last_kinds)
    halves = [_shard_sum(s32, r, k, sc_arr, name="chip_sum_" + n) for (s32, _), r, k, n in zip(sums, recv, last_kinds, last)]
    grad_big.update(zip(last, _swap_halves_out(halves, last_kinds)))

    stacked = dict(mlp_w1=(mlp_w1, m_mlp_w1, v_mlp_w1), mlp_w2=(mlp_w2, m_mlp_w2, v_mlp_w2))
    flat = lambda t: t.reshape(-1, t.shape[-1])
    for n in stacked:
        grad_big[n] = jnp.concatenate([grad_big.pop(n + "_0"), grad_big.pop(n + "_1")], axis=0)
    upd = {n: _adamw(w_big[n], grad_big[n], m_big[n], v_big[n], name="adamw_" + n) for n in BIG if n in grad_big}
    upd.update({n: _adamw(flat(w), grad_big[n], flat(m), flat(v), name="adamw_" + n) for n, (w, m, v) in stacked.items()})

    def join(name, idx, lead):
        t = upd[name][idx] if idx >= 0 else grad_big[name]
        if name in stacked:
            return t.reshape(stacked[name][0].shape)
        return t[None] if lead else t

    small_w = dict(meta=meta, norm_mix_g=norm_mix_g, norm_mlp_g=norm_mlp_g, even_ret_gn_g=even_ret_gn_g[0],
                   even_conv_w=even_conv_w[0], even_conv_b=even_conv_b, even_conv_ln_g=even_conv_ln_g,
                   even_conv_ln_b=even_conv_ln_b, odd_q_norm_g=odd_q_norm_g, odd_k_norm_g=odd_k_norm_g)
    small_m = dict(meta=m_meta, norm_mix_g=m_norm_mix_g, norm_mlp_g=m_norm_mlp_g, even_ret_gn_g=m_even_ret_gn_g[0],
                   even_conv_w=m_even_conv_w[0], even_conv_b=m_even_conv_b, even_conv_ln_g=m_even_conv_ln_g,
                   even_conv_ln_b=m_even_conv_ln_b, odd_q_norm_g=m_odd_q_norm_g, odd_k_norm_g=m_odd_k_norm_g)
    small_v = dict(meta=v_meta, norm_mix_g=v_norm_mix_g, norm_mlp_g=v_norm_mlp_g, even_ret_gn_g=v_even_ret_gn_g[0],
                   even_conv_w=v_even_conv_w[0], even_conv_b=v_even_conv_b, even_conv_ln_g=v_even_conv_ln_g,
                   even_conv_ln_b=v_even_conv_ln_b, odd_q_norm_g=v_odd_q_norm_g, odd_k_norm_g=v_odd_k_norm_g)
    small_upd = {n: _adamw(small_w[n], small[n], small_m[n], small_v[n], name="adamw_" + n) for n in small_w}
    leading = ("even_ret_gn_g", "even_conv_w")

    order = ("meta", "norm_mix_g", "norm_mlp_g", "even_w_in", "even_ret_gn_g", "even_conv_w", "even_conv_b",
             "even_conv_ln_g", "even_conv_ln_b", "even_w_out", "odd_w_qkv", "odd_q_norm_g", "odd_k_norm_g", "odd_w_o",
             "mlp_w1", "mlp_w2")
    big_lead = ("even_w_in", "even_w_out", "odd_w_qkv", "odd_w_o")

    def leaf(name, idx):
        if name in small_w:
            t = small_upd[name][idx] if idx >= 0 else small[name]
            return t[None] if name in leading else t
        return join(name, idx, name in big_lead)

    outs = [loss, g["x"][None]]
    for idx in (-1, 0, 1, 2):
        outs += [leaf(n, idx) for n in order]
    return tuple(outs)
```

```python
import functools

import jax
import jax.numpy as jnp
from jax import lax
from jax.experimental import pallas as pl
from jax.experimental.pallas import tpu as pltpu

F32 = jnp.float32
BF16 = jnp.bfloat16

D_MODEL = 1024
N_META = 16
CHUNK = 128
PAD_FRONT = (-N_META) % CHUNK
RET_HEADS = 4
RET_QK_DIM = 128
RET_V_DIM = 256
RET_QK_W = RET_HEADS * RET_QK_DIM
RET_V_W = RET_HEADS * RET_V_DIM
CONV_WIDTH = 31
CONV_HALO = 32
RET_DECAY_OFFSET = 5.0
ROPE_BASE = 10000.0
SB_HEADS = 16
SB_HEAD_DIM = 64
D_FF = 4 * D_MODEL
EPS = 1e-6
ADAM_LR = 0.001
ADAM_B1 = 0.9
ADAM_B2 = 0.999
ADAM_EPS = 1e-08
ADAM_WD = 0.01
ADAM_STEP = 10

N_CHIPS = 4
N_DEV = 8
VMEM_LIMIT = 56 * 1024 * 1024
MESH = pl.DeviceIdType.MESH
ANY = pl.BlockSpec(memory_space=pl.ANY)


def _params(*sem):
    return pltpu.CompilerParams(dimension_semantics=sem, vmem_limit_bytes=VMEM_LIMIT)


def _pick(n, cands):
    for c in cands:
        if n % c == 0:
            return c
    return n


def _sigmoid(x):
    return 1.0 / (1.0 + jnp.exp(-x))


def _dot(a, b):
    return lax.dot_general(a, b, (((1,), (0,)), ((), ())), preferred_element_type=F32)


def _dot_nt(a, b):
    return lax.dot_general(a, b, (((1,), (1,)), ((), ())), preferred_element_type=F32)


def _dot_tn(a, b):
    return lax.dot_general(a, b, (((0,), (0,)), ((), ())), preferred_element_type=F32)


def _split_dot(x, m):
    hi = x.astype(BF16)
    lo = (x - hi.astype(F32)).astype(BF16)
    return _dot(hi, m) + _dot(lo, m)


def _matmul(a, b, *, mode, out_dtypes, epilogue=None, extras=(), name, after=None):
    if mode == "nn":
        (m, k), (k2, n) = a.shape, b.shape
    elif mode == "nt":
        (m, k), (n, k2) = a.shape, b.shape
    else:
        (k, m), (k2, n) = a.shape, b.shape
    assert k == k2, (a.shape, b.shape, mode)
    tm = _pick(m, (1056, 1024, 768, 512, 384, 256, 128, 96))
    tn = _pick(n, (1024, 768, 512, 256, 128))
    tk = _pick(k, (1056, 1024, 768, 512, 384, 256, 128, 96))
    nk = k // tk
    dot = {"nn": _dot, "nt": _dot_nt, "tn": _dot_tn}[mode]
    n_extra, n_out = len(extras), len(out_dtypes)
    n_after = 0 if after is None else 1
    if epilogue is None:
        epilogue = lambda acc: (acc,)

    def body(a_ref, b_ref, *rest):
        extra_refs = rest[:n_extra]
        out_refs = rest[n_extra + n_after:n_extra + n_after + n_out]
        part = dot(a_ref[...].astype(BF16), b_ref[...].astype(BF16))

        def finish(acc):
            res = epilogue(acc, *[r[...] for r in extra_refs])
            for o_ref, r in zip(out_refs, res):
                o_ref[...] = r.astype(o_ref.dtype)

        if nk == 1:
            finish(part)
        else:
            acc_ref = rest[-1]
            kk = pl.program_id(2)

            @pl.when(kk == 0)
            def _():
                acc_ref[...] = part

            @pl.when(kk > 0)
            def _():
                acc_ref[...] += part

            @pl.when(kk == nk - 1)
            def _():
                finish(acc_ref[...])

    if mode == "nn":
        a_spec = pl.BlockSpec((tm, tk), lambda i, j, kk: (i, kk))
        b_spec = pl.BlockSpec((tk, tn), lambda i, j, kk: (kk, j))
    elif mode == "nt":
        a_spec = pl.BlockSpec((tm, tk), lambda i, j, kk: (i, kk))
        b_spec = pl.BlockSpec((tn, tk), lambda i, j, kk: (j, kk))
    else:
        a_spec = pl.BlockSpec((tk, tm), lambda i, j, kk: (kk, i))
        b_spec = pl.BlockSpec((tk, tn), lambda i, j, kk: (kk, j))
    o_spec = pl.BlockSpec((tm, tn), lambda i, j, kk: (i, j))
    outs = pl.pallas_call(
        body,
        out_shape=[jax.ShapeDtypeStruct((m, n), dt) for dt in out_dtypes],
        grid=(m // tm, n // tn, nk),
        in_specs=[a_spec, b_spec] + [o_spec] * n_extra + [ANY] * n_after,
        out_specs=[o_spec] * n_out,
        scratch_shapes=[pltpu.VMEM((tm, tn), F32)] if nk > 1 else [],
        compiler_params=_params("parallel", "parallel", "arbitrary"),
        name=name,
    )(a, b, *extras, *([] if after is None else [after]))
    return outs[0] if n_out == 1 else outs


def _add_epilogue(acc, res):
    return (res + acc,)


def _rmsnorm_fwd(x, g, *, name):
    p, d = x.shape
    rows = _pick(p, (384, 128, 96))

    def body(x_ref, g_ref, o_ref):
        xv = x_ref[...]
        r = lax.rsqrt(jnp.mean(xv * xv, axis=-1, keepdims=True) + EPS)
        o_ref[...] = (xv * r * g_ref[...]).astype(o_ref.dtype)

    return pl.pallas_call(
        body,
        out_shape=jax.ShapeDtypeStruct((p, d), BF16),
        grid=(p // rows,),
        in_specs=[pl.BlockSpec((rows, d), lambda i: (i, 0)), pl.BlockSpec((1, d), lambda i: (0, 0))],
        out_specs=pl.BlockSpec((rows, d), lambda i: (i, 0)),
        compiler_params=_params("parallel"),
        name=name,
    )(x, g)


def _rmsnorm_bwd(x, g, dy, dres, *, name):
    p, d = x.shape
    rows = _pick(p, (384, 128, 96))

    def body(x_ref, g_ref, dy_ref, dres_ref, dx_ref, dg_ref):
        xv = x_ref[...]
        r = lax.rsqrt(jnp.mean(xv * xv, axis=-1, keepdims=True) + EPS)
        dyv = dy_ref[...]
        gdy = dyv * g_ref[...]
        proj = jnp.mean(xv * gdy, axis=-1, keepdims=True)
        dx_ref[...] = dres_ref[...] + r * gdy - xv * (r * r * r) * proj
        part = jnp.sum(dyv * xv * r, axis=0, keepdims=True)

        @pl.when(pl.program_id(0) == 0)
        def _():
            dg_ref[...] = part

        @pl.when(pl.program_id(0) > 0)
        def _():
            dg_ref[...] += part

    row_spec = pl.BlockSpec((rows, d), lambda i: (i, 0))
    vec_spec = pl.BlockSpec((1, d), lambda i: (0, 0))
    return pl.pallas_call(
        body,
        out_shape=[jax.ShapeDtypeStruct((p, d), F32), jax.ShapeDtypeStruct((1, d), F32)],
        grid=(p // rows,),
        in_specs=[row_spec, vec_spec, row_spec, row_spec],
        out_specs=[row_spec, vec_spec],
        compiler_params=_params("arbitrary"),
        name=name,
    )(x, g, dy, dres)


def _mlp_fwd(h, g, w1, w2, *, name):
    hn = _rmsnorm_fwd(h, g, name=name + "_norm")

    def act(acc):
        r = jnp.maximum(acc, 0.0)
        return acc, r * r

    z, a2 = _matmul(hn, w1, mode="nn", out_dtypes=(F32, BF16), epilogue=act, name=name + "_up")
    out = _matmul(a2, w2, mode="nn", out_dtypes=(F32,), epilogue=_add_epilogue, extras=(h,), name=name + "_down")
    return out, (hn, z, a2)


def _mlp_bwd(h, g, w1, w2, saved, dout, *, name, after=None):
    hn, z, a2 = saved

    def dact(acc, zt):
        return (acc * (2.0 * jnp.maximum(zt, 0.0)),)

    dz = _matmul(dout, w2, mode="nt", out_dtypes=(BF16,), epilogue=dact, extras=(z,), name=name + "_dz", after=after)
    dw2 = _matmul(a2, dout, mode="tn", out_dtypes=(F32,), name=name + "_dw2")
    dw1 = _matmul(hn, dz, mode="tn", out_dtypes=(F32,), name=name + "_dw1")
    dhn = _matmul(dz, w1, mode="nt", out_dtypes=(F32,), name=name + "_dhn")
    dh, dg = _rmsnorm_bwd(h, g, dhn, dout, name=name + "_dnorm")
    return dh, dg, dw1, dw2


def _retention_tables(p):
    half = RET_QK_DIM // 2
    inv_freq = ROPE_BASE ** (-jnp.arange(half, dtype=F32) / half)
    ang = jnp.arange(p, dtype=F32)[:, None] * inv_freq[None, :]
    cos, sin = jnp.cos(ang), jnp.sin(ang)
    cosf = jnp.concatenate([cos, cos], axis=1)
    sins = jnp.concatenate([-sin, sin], axis=1)
    log_g = jnp.log1p(-jnp.exp2(-RET_DECAY_OFFSET - jnp.arange(RET_HEADS, dtype=F32)))
    idx = jnp.arange(CHUNK, dtype=F32)
    diff = idx[:, None] - idx[None, :]
    inner = jnp.where(diff[None] >= 0, jnp.exp(jnp.maximum(diff, 0.0)[None] * log_g[:, None, None]), 0.0)
    kdec = jnp.exp((CHUNK - 1 - idx)[None, :] * log_g[:, None])
    qdec = jnp.exp((idx + 1.0)[None, :] * log_g[:, None])
    cdec = jnp.exp(CHUNK * log_g)
    kdec = jnp.broadcast_to(kdec[:, :, None], (RET_HEADS, CHUNK, RET_QK_DIM))
    qdec = jnp.broadcast_to(qdec[:, :, None], (RET_HEADS, CHUNK, RET_QK_DIM))
    cdec = jnp.broadcast_to(cdec[:, None, None], (RET_HEADS, RET_QK_DIM, RET_V_DIM))
    return cosf, sins, inner, kdec, qdec, cdec


def _rot(x, cosf, sins):
    return x * cosf + pltpu.roll(x, RET_QK_DIM // 2, 1) * sins


def _rot_bwd(dy, cosf, sins):
    return dy * cosf + pltpu.roll(dy * sins, RET_QK_DIM // 2, 1)


def _ret_in_specs(chunk_of):
    q_spec = pl.BlockSpec((CHUNK, RET_QK_W), lambda s: (chunk_of(s), 0))
    k_spec = pl.BlockSpec((CHUNK, RET_QK_W), lambda s: (chunk_of(s), 1))
    v_spec = pl.BlockSpec((CHUNK, RET_V_W), lambda s: (chunk_of(s), 1))
    g_spec = pl.BlockSpec((CHUNK, RET_V_W), lambda s: (chunk_of(s), 2))
    rope_spec = pl.BlockSpec((CHUNK, RET_QK_DIM), lambda s: (chunk_of(s), 0))
    whole = lambda *shape: pl.BlockSpec(shape, lambda s: (0,) * len(shape))
    head_sq = whole(RET_HEADS, CHUNK, CHUNK)
    head_qk = whole(RET_HEADS, CHUNK, RET_QK_DIM)
    head_st = whole(RET_HEADS, RET_QK_DIM, RET_V_DIM)
    gam_spec = whole(RET_HEADS, 1, RET_V_DIM)
    return [q_spec, k_spec, v_spec, g_spec, rope_spec, rope_spec, head_sq, head_qk, head_qk, head_st, gam_spec]


def _ret_head_views(h, qk_refs, v_refs, head_refs):
    qk = pl.ds(h * RET_QK_DIM, RET_QK_DIM)
    vv = pl.ds(h * RET_V_DIM, RET_V_DIM)
    return [r.at[:, qk] for r in qk_refs], [r.at[:, vv] for r in v_refs], [r.at[h] for r in head_refs]


def _retention_fwd(proj, gn_g, tables, *, name):
    p = proj.shape[0]
    n_chunks = p // CHUNK
    scale = RET_QK_DIM ** -0.5

    def body(q_ref, k_ref, v_ref, g_ref, cos_ref, sin_ref, inner_ref, kdec_ref, qdec_ref, cdec_ref, gam_ref,
             og_ref, opre_ref, sprev_ref, s_scr):
        @pl.when(pl.program_id(0) == 0)
        def _():
            s_scr[...] = jnp.zeros_like(s_scr)

        for h in range(RET_HEADS):
            (q_h, k_h), (v_h, g_h, og_h, opre_h), tabs = _ret_head_views(
                h, (q_ref, k_ref), (v_ref, g_ref, og_ref, opre_ref),
                (inner_ref, kdec_ref, qdec_ref, cdec_ref, gam_ref, sprev_ref, s_scr))
            one_head(q_h, k_h, v_h, g_h, cos_ref, sin_ref, *tabs[:5], og_h, opre_h, *tabs[5:])

    def one_head(q_ref, k_ref, v_ref, g_ref, cos_ref, sin_ref, inner_ref, kdec_ref, qdec_ref, cdec_ref, gam_ref,
                 og_ref, opre_ref, sprev_ref, s_scr):
        cosf, sins = cos_ref[...], sin_ref[...]
        qr = _rot(q_ref[...], cosf, sins)
        kr = _rot(k_ref[...], cosf, sins) * scale
        vb = v_ref[...].astype(BF16)
        scores = _dot_nt(qr.astype(BF16), kr.astype(BF16)) * inner_ref[...]
        state = s_scr[...]
        sprev_ref[...] = state
        o = _dot(scores.astype(BF16), vb) + _dot((qr * qdec_ref[...]).astype(BF16), state.astype(BF16))
        kd = kr * kdec_ref[...]
        s_scr[...] = cdec_ref[...] * state + _dot(kd.T.astype(BF16), vb)
        opre_ref[...] = o
        mu = jnp.mean(o, axis=-1, keepdims=True)
        oc = o - mu
        var = jnp.mean(oc * oc, axis=-1, keepdims=True)
        on = oc * lax.rsqrt(var + EPS) * gam_ref[...]
        gv = g_ref[...]
        og_ref[...] = (gv * _sigmoid(gv) * on).astype(og_ref.dtype)

    chunk_of = lambda s: s
    out_v = pl.BlockSpec((CHUNK, RET_V_W), lambda s: (s, 0))
    return pl.pallas_call(
        body,
        out_shape=[
            jax.ShapeDtypeStruct((p, RET_V_W), BF16),
            jax.ShapeDtypeStruct((p, RET_V_W), F32),
            jax.ShapeDtypeStruct((RET_HEADS, n_chunks, RET_QK_DIM, RET_V_DIM), F32),
        ],
        grid=(n_chunks,),
        in_specs=_ret_in_specs(chunk_of),
        out_specs=[out_v, out_v, pl.BlockSpec((RET_HEADS, None, RET_QK_DIM, RET_V_DIM), lambda s: (0, s, 0, 0))],
        scratch_shapes=[pltpu.VMEM((RET_HEADS, RET_QK_DIM, RET_V_DIM), F32)],
        compiler_params=_params("arbitrary"),
        name=name,
    )(proj, proj, proj, proj, *tables, gn_g.reshape(RET_HEADS, 1, RET_V_DIM))


def _retention_bwd(proj, gn_g, tables, opre, sprev, dog, *, name):
    p = proj.shape[0]
    n_chunks = p // CHUNK
    scale = RET_QK_DIM ** -0.5

    def body(q_ref, k_ref, v_ref, g_ref, cos_ref, sin_ref, inner_ref, kdec_ref, qdec_ref, cdec_ref, gam_ref,
             opre_ref, sprev_ref, dog_ref, dq_ref, dk_ref, dv_ref, dg_ref, dgam_ref, ds_scr):
        first = pl.program_id(0) == 0

        @pl.when(first)
        def _():
            ds_scr[...] = jnp.zeros_like(ds_scr)

        for h in range(RET_HEADS):
            (q_h, k_h, dq_h, dk_h), (v_h, g_h, opre_h, dog_h, dv_h, dg_h), tabs = _ret_head_views(
                h, (q_ref, k_ref, dq_ref, dk_ref), (v_ref, g_ref, opre_ref, dog_ref, dv_ref, dg_ref),
                (inner_ref, kdec_ref, qdec_ref, cdec_ref, gam_ref, sprev_ref, dgam_ref, ds_scr))
            one_head(first, q_h, k_h, v_h, g_h, cos_ref, sin_ref, *tabs[:5], opre_h, tabs[5], dog_h,
                     dq_h, dk_h, dv_h, dg_h, tabs[6], tabs[7])

    def one_head(first, q_ref, k_ref, v_ref, g_ref, cos_ref, sin_ref, inner_ref, kdec_ref, qdec_ref, cdec_ref, gam_ref,
                 opre_ref, sprev_ref, dog_ref, dq_ref, dk_ref, dv_ref, dg_ref, dgam_ref, ds_scr):
        cosf, sins = cos_ref[...], sin_ref[...]
        qr = _rot(q_ref[...], cosf, sins)
        kr = _rot(k_ref[...], cosf, sins) * scale
        qb, kb = qr.astype(BF16), kr.astype(BF16)
        vb = v_ref[...].astype(BF16)
        inner = inner_ref[...]
        qdec, kdec = qdec_ref[...], kdec_ref[...]
        state_b = sprev_ref[...].astype(BF16)
        o = opre_ref[...]
        mu = jnp.mean(o, axis=-1, keepdims=True)
        oc = o - mu
        rstd = lax.rsqrt(jnp.mean(oc * oc, axis=-1, keepdims=True) + EPS)
        xhat = oc * rstd
        gam = gam_ref[...]
        on = xhat * gam
        gv = g_ref[...]
        sig = _sigmoid(gv)
        dogv = dog_ref[...]
        dg_ref[...] = (dogv * on * sig * (1.0 + gv * (1.0 - sig))).astype(dg_ref.dtype)
        don = dogv * gv * sig
        dgam_part = jnp.sum(don * xhat, axis=0, keepdims=True)

        @pl.when(first)
        def _():
            dgam_ref[...] = dgam_part

        @pl.when(jnp.logical_not(first))
        def _():
            dgam_ref[...] += dgam_part

        dxhat = don * gam
        do = rstd * (dxhat - jnp.mean(dxhat, axis=-1, keepdims=True)
                     - xhat * jnp.mean(dxhat * xhat, axis=-1, keepdims=True))
        dob = do.astype(BF16)
        scores_b = (_dot_nt(qb, kb) * inner).astype(BF16)
        da = (_dot_nt(dob, vb) * inner).astype(BF16)
        dv = _dot(scores_b.astype(F32).T.astype(BF16), dob)
        dqr = _dot(da, kb)
        dkr = _dot(da.astype(F32).T.astype(BF16), qb)
        dqr += _dot_nt(dob, state_b) * qdec
        ds_local = _dot((qr * qdec).T.astype(BF16), dob)
        gstate = ds_scr[...]
        gb = gstate.astype(BF16)
        kd_b = (kr * kdec).astype(BF16)
        dkr += _dot_nt(vb, gb) * kdec
        dv += _dot(kd_b, gb)
        ds_scr[...] = cdec_ref[...] * gstate + ds_local
        dq_ref[...] = _rot_bwd(dqr, cosf, sins).astype(dq_ref.dtype)
        dk_ref[...] = _rot_bwd(dkr * scale, cosf, sins).astype(dk_ref.dtype)
        dv_ref[...] = dv.astype(dv_ref.dtype)

    chunk_of = lambda s: n_chunks - 1 - s
    blk_v = pl.BlockSpec((CHUNK, RET_V_W), lambda s: (chunk_of(s), 0))
    blk_qk = pl.BlockSpec((CHUNK, RET_QK_W), lambda s: (chunk_of(s), 0))
    st_spec = pl.BlockSpec((RET_HEADS, None, RET_QK_DIM, RET_V_DIM), lambda s: (0, chunk_of(s), 0, 0))
    return pl.pallas_call(
        body,
        out_shape=[
            jax.ShapeDtypeStruct((p, RET_QK_W), BF16),
            jax.ShapeDtypeStruct((p, RET_QK_W), BF16),
            jax.ShapeDtypeStruct((p, RET_V_W), BF16),
            jax.ShapeDtypeStruct((p, RET_V_W), BF16),
            jax.ShapeDtypeStruct((RET_HEADS, 1, RET_V_DIM), F32),
        ],
        grid=(n_chunks,),
        in_specs=_ret_in_specs(chunk_of) + [blk_v, st_spec, blk_v],
        out_specs=[blk_qk, blk_qk, blk_v, blk_v, pl.BlockSpec((RET_HEADS, 1, RET_V_DIM), lambda s: (0, 0, 0))],
        scratch_shapes=[pltpu.VMEM((RET_HEADS, RET_QK_DIM, RET_V_DIM), F32)],
        compiler_params=_params("arbitrary"),
        name=name,
    )(proj, proj, proj, proj, *tables, gn_g.reshape(RET_HEADS, 1, RET_V_DIM), opre, sprev, dog)


def _conv_rows(p):
    return _pick(p, (384, 128))


CONV_CHUNK = 32
F32_SUBLANES = 8


def _shifted_rows(rows):
    return rows + CONV_HALO - F32_SUBLANES


def _shifted_copies(src_scr, sh_scr, n_rows):
    for s in range(1, F32_SUBLANES):
        sh_scr[s - 1] = src_scr[s:s + n_rows, :]


def _tap_rows(src_scr, sh_scr, off, r0, n):
    q, s = divmod(off, F32_SUBLANES)
    ref = src_scr if s == 0 else sh_scr.at[s - 1]
    return ref[pl.ds(pl.multiple_of(r0 + F32_SUBLANES * q, F32_SUBLANES), n), :]


def _ln_stats(y):
    mu = jnp.mean(y, axis=-1, keepdims=True)
    yc = y - mu
    rstd = lax.rsqrt(jnp.mean(yc * yc, axis=-1, keepdims=True) + EPS)
    return yc * rstd, rstd


def _conv_fwd(proj, conv_w, conv_b, ln_g, ln_b, *, name):
    p = proj.shape[0]
    c = D_MODEL
    rows = _conv_rows(p)
    hpb = rows // CONV_HALO
    a_col, gate_col = (2 * RET_QK_W + 2 * RET_V_W) // c, (2 * RET_QK_W + 2 * RET_V_W) // c + 1

    def body(a_ref, gate_ref, ah_ref, gateh_ref, w_ref, b_ref, lg_ref, lb_ref, c_ref, y_ref, hdn_scr, sh_scr):
        i = pl.program_id(0)
        hdn_scr[0:CONV_HALO, :] = ah_ref[...] * _sigmoid(gateh_ref[...])
        hdn_scr[CONV_HALO:, :] = a_ref[...] * _sigmoid(gate_ref[...])
        _shifted_copies(hdn_scr, sh_scr, _shifted_rows(rows))

        def chunk(j, _):
            r0 = pl.multiple_of(j * CONV_CHUNK, CONV_CHUNK)
            acc = jnp.zeros((CONV_CHUNK, c), F32)
            for w in range(CONV_WIDTH):
                off = CONV_HALO - (CONV_WIDTH - 1) + w
                acc += _tap_rows(hdn_scr, sh_scr, off, r0, CONV_CHUNK) * w_ref[w:w + 1, :]
            y_ref[pl.ds(r0, CONV_CHUNK), :] = acc + b_ref[...]
            return 0

        lax.fori_loop(0, rows // CONV_CHUNK, chunk, 0)
        y = y_ref[...]
        yhat, _ = _ln_stats(y)
        ln = yhat * lg_ref[...] + lb_ref[...]
        row = i * rows + lax.broadcasted_iota(jnp.int32, (rows, 1), 0)
        c_ref[...] = jnp.where(row >= PAD_FRONT, ln * _sigmoid(ln), 0.0).astype(c_ref.dtype)

    halo_idx = lambda i: jnp.maximum(i * hpb - 1, 0)
    vec = pl.BlockSpec((1, c), lambda i: (0, 0))
    return pl.pallas_call(
        body,
        out_shape=[jax.ShapeDtypeStruct((p, c), BF16), jax.ShapeDtypeStruct((p, c), F32)],
        grid=(p // rows,),
        in_specs=[
            pl.BlockSpec((rows, c), lambda i: (i, a_col)),
            pl.BlockSpec((rows, c), lambda i: (i, gate_col)),
            pl.BlockSpec((CONV_HALO, c), lambda i: (halo_idx(i), a_col)),
            pl.BlockSpec((CONV_HALO, c), lambda i: (halo_idx(i), gate_col)),
            pl.BlockSpec((CONV_WIDTH, c), lambda i: (0, 0)),
            vec, vec, vec,
        ],
        out_specs=[pl.BlockSpec((rows, c), lambda i: (i, 0)), pl.BlockSpec((rows, c), lambda i: (i, 0))],
        scratch_shapes=[pltpu.VMEM((CONV_HALO + rows, c), F32),
                        pltpu.VMEM((F32_SUBLANES - 1, _shifted_rows(rows), c), F32)],
        compiler_params=_params("parallel"),
        name=name,
    )(proj, proj, proj, proj, conv_w, conv_b, ln_g, ln_b)


def _conv_bwd(proj, conv_w, ln_g, ln_b, y, dcat, *, name):
    p = proj.shape[0]
    c = D_MODEL
    rows = _conv_rows(p)
    hpb = rows // CONV_HALO
    n_blocks = p // rows
    a_col, gate_col = (2 * RET_QK_W + 2 * RET_V_W) // c, (2 * RET_QK_W + 2 * RET_V_W) // c + 1

    def body(a_ref, gate_ref, ah_ref, gateh_ref, w_ref, lg_ref, lb_ref, y_ref, yh_ref, dc_ref, dch_ref,
             da_ref, dgate_ref, dw_ref, db_ref, dlg_ref, dlb_ref, hdn_scr, dy_scr, hdn_sh, dy_sh):
        i = pl.program_id(0)
        lg, lb = lg_ref[...], lb_ref[...]

        def ln_bwd(yv, dcv):
            yhat, rstd = _ln_stats(yv)
            ln = yhat * lg + lb
            sig = _sigmoid(ln)
            dln = dcv * sig * (1.0 + ln * (1.0 - sig))
            dyhat = dln * lg
            dyv = rstd * (dyhat - jnp.mean(dyhat, axis=-1, keepdims=True)
                          - yhat * jnp.mean(dyhat * yhat, axis=-1, keepdims=True))
            return dyv, dln, yhat

        row = i * rows + lax.broadcasted_iota(jnp.int32, (rows, 1), 0)
        dy, dln, yhat = ln_bwd(y_ref[...], jnp.where(row >= PAD_FRONT, dc_ref[...], 0.0))
        dy_halo, _, _ = ln_bwd(yh_ref[...], dch_ref[...])
        dy_scr[0:rows, :] = dy
        dy_scr[rows:, :] = jnp.where(i == n_blocks - 1, 0.0, dy_halo)
        hdn_scr[0:CONV_HALO, :] = ah_ref[...] * _sigmoid(gateh_ref[...])
        hdn_scr[CONV_HALO:, :] = a_ref[...] * _sigmoid(gate_ref[...])
        _shifted_copies(hdn_scr, hdn_sh, _shifted_rows(rows))
        _shifted_copies(dy_scr, dy_sh, _shifted_rows(rows))

        @pl.when(i == 0)
        def _():
            dw_ref[...] = jnp.zeros_like(dw_ref)
            db_ref[...] = jnp.zeros_like(db_ref)
            dlg_ref[...] = jnp.zeros_like(dlg_ref)
            dlb_ref[...] = jnp.zeros_like(dlb_ref)

        n_chunks = rows // CONV_CHUNK

        def input_grad(j, _):
            r0 = pl.multiple_of(j * CONV_CHUNK, CONV_CHUNK)
            dhdn = jnp.zeros((CONV_CHUNK, c), F32)
            for w in range(CONV_WIDTH):
                dhdn += _tap_rows(dy_scr, dy_sh, CONV_WIDTH - 1 - w, r0, CONV_CHUNK) * w_ref[w:w + 1, :]
            here = pl.ds(r0, CONV_CHUNK)
            sig_gate = _sigmoid(gate_ref[here, :])
            da_ref[here, :] = (dhdn * sig_gate).astype(da_ref.dtype)
            dgate_ref[here, :] = (dhdn * a_ref[here, :] * sig_gate * (1.0 - sig_gate)).astype(dgate_ref.dtype)
            return 0

        lax.fori_loop(0, n_chunks, input_grad, 0)
        for w in range(CONV_WIDTH):
            off = CONV_HALO - (CONV_WIDTH - 1) + w

            def tap_grad(j, acc, off=off):
                r0 = pl.multiple_of(j * CONV_CHUNK, CONV_CHUNK)
                prod = dy_scr[pl.ds(r0, CONV_CHUNK), :] * _tap_rows(hdn_scr, hdn_sh, off, r0, CONV_CHUNK)
                for k in range(CONV_CHUNK // F32_SUBLANES):
                    acc = acc + prod[k * F32_SUBLANES:(k + 1) * F32_SUBLANES]
                return acc

            acc = lax.fori_loop(0, n_chunks, tap_grad, jnp.zeros((F32_SUBLANES, c), F32))
            dw_ref[w:w + 1, :] += jnp.sum(acc, axis=0, keepdims=True)
        db_ref[...] += jnp.sum(dy, axis=0, keepdims=True)
        dlg_ref[...] += jnp.sum(dln * yhat, axis=0, keepdims=True)
        dlb_ref[...] += jnp.sum(dln, axis=0, keepdims=True)

    prev_halo = lambda i: jnp.maximum(i * hpb - 1, 0)
    next_halo = lambda i: jnp.minimum((i + 1) * hpb, p // CONV_HALO - 1)
    vec = pl.BlockSpec((1, c), lambda i: (0, 0))
    blk = lambda col: pl.BlockSpec((rows, c), lambda i: (i, col))
    outs = pl.pallas_call(
        body,
        out_shape=[
            jax.ShapeDtypeStruct((p, c), BF16),
            jax.ShapeDtypeStruct((p, c), BF16),
            jax.ShapeDtypeStruct((CONV_WIDTH + 1, c), F32),
            jax.ShapeDtypeStruct((1, c), F32),
            jax.ShapeDtypeStruct((1, c), F32),
            jax.ShapeDtypeStruct((1, c), F32),
        ],
        grid=(n_blocks,),
        in_specs=[
            blk(a_col), blk(gate_col),
            pl.BlockSpec((CONV_HALO, c), lambda i: (prev_halo(i), a_col)),
            pl.BlockSpec((CONV_HALO, c), lambda i: (prev_halo(i), gate_col)),
            pl.BlockSpec((CONV_WIDTH, c), lambda i: (0, 0)),
            vec, vec,
            blk(0),
            pl.BlockSpec((CONV_HALO, c), lambda i: (next_halo(i), 0)),
            blk(1),
            pl.BlockSpec((CONV_HALO, c), lambda i: (next_halo(i), 1)),
        ],
        out_specs=[blk(0), blk(0), pl.BlockSpec((CONV_WIDTH + 1, c), lambda i: (0, 0)), vec, vec, vec],
        scratch_shapes=[pltpu.VMEM((CONV_HALO + rows, c), F32), pltpu.VMEM((rows + CONV_HALO, c), F32),
                        pltpu.VMEM((F32_SUBLANES - 1, _shifted_rows(rows), c), F32),
                        pltpu.VMEM((F32_SUBLANES - 1, _shifted_rows(rows), c), F32)],
        compiler_params=_params("arbitrary"),
        name=name,
    )(proj, proj, proj, proj, conv_w, ln_g, ln_b, y, y, dcat, dcat)
    da, dgate, dw, db, dlg, dlb = outs
    return da, dgate, dw[:CONV_WIDTH], db, dlg, dlb


LANES = 128


def _group_matrix():
    r = jnp.arange(LANES)[:, None] // SB_HEAD_DIM
    c = jnp.arange(LANES)[None, :] // SB_HEAD_DIM
    return (r == c).astype(BF16)


def _head_sums(v, gm):
    return jnp.concatenate([_split_dot(v[:, j * LANES:(j + 1) * LANES], gm) for j in range(v.shape[1] // LANES)], axis=1)


def _qknorm_fwd(qkv, qg, kg, *, name):
    p = qkv.shape[0]
    d = D_MODEL
    rows = _pick(p, (384, 128, 96))

    def body(q_ref, k_ref, v_ref, qg_ref, kg_ref, gm_ref, qn_ref, kn_ref, vb_ref):
        gm = gm_ref[...]

        def norm(x, g):
            ms = _head_sums(x * x, gm) * (1.0 / SB_HEAD_DIM)
            return x * lax.rsqrt(ms + EPS) * g

        qn_ref[...] = norm(q_ref[...], qg_ref[...]).astype(BF16)
        kn_ref[...] = norm(k_ref[...], kg_ref[...]).astype(BF16)
        vb_ref[...] = v_ref[...].astype(BF16)

    blk = lambda col: pl.BlockSpec((rows, d), lambda i: (i, col))
    vec = pl.BlockSpec((1, d), lambda i: (0, 0))
    return pl.pallas_call(
        body,
        out_shape=[jax.ShapeDtypeStruct((p, d), BF16)] * 3,
        grid=(p // rows,),
        in_specs=[blk(0), blk(1), blk(2), vec, vec, pl.BlockSpec((LANES, LANES), lambda i: (0, 0))],
        out_specs=[blk(0)] * 3,
        compiler_params=_params("parallel"),
        name=name,
    )(qkv, qkv, qkv, qg, kg, _group_matrix())


def _qknorm_bwd(qkv, qg, kg, dqn, dkn, dv, *, name):
    p = qkv.shape[0]
    d = D_MODEL
    rows = _pick(p, (384, 128, 96))

    def body(q_ref, k_ref, qg_ref, kg_ref, gm_ref, dqn_ref, dkn_ref, dv_ref, dqkv_ref, dqg_ref, dkg_ref):
        gm = gm_ref[...]

        def bwd(x, g, dy):
            ms = _head_sums(x * x, gm) * (1.0 / SB_HEAD_DIM)
            r = lax.rsqrt(ms + EPS)
            gdy = dy * g
            proj = _head_sums(x * gdy, gm) * (1.0 / SB_HEAD_DIM)
            return r * gdy - x * (r * r * r) * proj, jnp.sum(dy * x * r, axis=0, keepdims=True)

        dq, dqg = bwd(q_ref[...], qg_ref[...], dqn_ref[...])
        dk, dkg = bwd(k_ref[...], kg_ref[...], dkn_ref[...])
        dqkv_ref[:, 0:d] = dq.astype(BF16)
        dqkv_ref[:, d:2 * d] = dk.astype(BF16)
        dqkv_ref[:, 2 * d:3 * d] = dv_ref[...].astype(BF16)

        @pl.when(pl.program_id(0) == 0)
        def _():
            dqg_ref[...] = dqg
            dkg_ref[...] = dkg

        @pl.when(pl.program_id(0) > 0)
        def _():
            dqg_ref[...] += dqg
            dkg_ref[...] += dkg

    blk = lambda col: pl.BlockSpec((rows, d), lambda i: (i, col))
    vec = pl.BlockSpec((1, d), lambda i: (0, 0))
    return pl.pallas_call(
        body,
        out_shape=[jax.ShapeDtypeStruct((p, 3 * d), BF16), jax.ShapeDtypeStruct((1, d), F32),
                   jax.ShapeDtypeStruct((1, d), F32)],
        grid=(p // rows,),
        in_specs=[blk(0), blk(1), vec, vec, pl.BlockSpec((LANES, LANES), lambda i: (0, 0)), blk(0), blk(0), blk(0)],
        out_specs=[pl.BlockSpec((rows, 3 * d), lambda i: (i, 0)), vec, vec],
        compiler_params=_params("arbitrary"),
        name=name,
    )(qkv, qkv, qg, kg, _group_matrix(), dqn, dkn, dv)


SB_PAIR = 2 * SB_HEAD_DIM
SB_GROUP = 8
SB_PAIRS_PER_STEP = 2
SB_PAIRS_PER_STEP_BWD = 1
SB_MASKED = -1e30


def _sb_consts():
    lane = lax.broadcasted_iota(jnp.int32, (CHUNK, SB_PAIR), 1)
    r = lax.broadcasted_iota(jnp.int32, (CHUNK, CHUNK), 0)
    c = lax.broadcasted_iota(jnp.int32, (CHUNK, CHUNK), 1)
    lo = (lane < SB_HEAD_DIM).astype(F32).astype(BF16)
    ones = jnp.ones((CHUNK, CHUNK), BF16)
    twice = lambda m: jnp.concatenate([jnp.concatenate([m, ones], axis=1)] * 2, axis=0)
    later, earlier = twice((r > c).astype(BF16)), twice((r < c).astype(BF16))
    not_before = (c >= r).astype(F32) * SB_MASKED
    padding = (c < PAD_FRONT).astype(F32) * SB_MASKED
    return (lo, 1.0 - lo), c, later, earlier, not_before, padding


def _sb_halves(t, head_lanes):
    return t * head_lanes[0], t * head_lanes[1]


def _sb_logits(qh, kg, biases):
    z = _dot_nt(qh, kg)
    tiles = []
    for b, bias in enumerate(biases):
        zt = z[:, b * CHUNK:(b + 1) * CHUNK]
        if bias is not None:
            zt = zt + bias
        ls_pos = jnp.minimum(zt, 0.0) - jnp.log(1.0 + jnp.exp(-jnp.abs(zt)))
        tiles.append((ls_pos, ls_pos - zt))
    return tiles


def _sb_block_sums(tiles, m):
    st = jnp.concatenate(tiles, axis=0)
    hi = st.astype(BF16)
    lo = (st - hi.astype(F32)).astype(BF16)
    tot = _dot(jnp.concatenate([hi, lo], axis=1), m)
    return [(tot[i * CHUNK:(i + 1) * CHUNK, 0:CHUNK], tot[i * CHUNK:(i + 1) * CHUNK, CHUNK:2 * CHUNK])
            for i in range(len(tiles))]


def _sb_plan(qi, padding, not_before):
    top = lax.div(qi, SB_GROUP)
    size = qi - SB_GROUP * top + 1

    def masks(n_b):
        pad_if_first = padding * (top == 0).astype(F32)
        m = [None] * n_b
        m[n_b - 1] = not_before
        m[0] = pad_if_first if m[0] is None else m[0] + pad_if_first
        return m

    return top, size, masks


def _once_if(cond, fn, carry):
    return lax.fori_loop(0, jnp.where(cond, 1, 0), lambda s, cr: fn(cr), carry)


def _sb_head_rows(tg, lanes, n_b):
    return jnp.concatenate([tg[b * CHUNK:(b + 1) * CHUNK] * lanes for b in range(n_b)], axis=0)


def _sb_fwd(qn, kn, vb, *, name):
    p = qn.shape[0]
    n_blocks = p // CHUNK
    n_pairs = SB_HEADS // 2
    scale = SB_HEAD_DIM ** -0.5

    n_step = SB_PAIRS_PER_STEP
    n_chains = 2 * n_step
    lanes_of = lambda pair: slice(pair * SB_PAIR, (pair + 1) * SB_PAIR)

    def body(q_ref, k_ref, v_ref, o_ref, car_ref):
        head_lanes, c, later, _, not_before, padding = _sb_consts()

        def q_block(qi, _):
            rows = pl.ds(pl.multiple_of(qi * CHUNK, CHUNK), CHUNK)
            qs = []
            for pair in range(n_step):
                qh = _sb_halves(q_ref[rows, lanes_of(pair)], head_lanes)
                qs += [qh[0] * scale, qh[1] * scale]

            def blocks(kb0, biases, carry):
                n_b = len(biases)
                accs, runs, savs = list(carry[:n_step]), list(carry[n_step:n_step + n_chains]), list(carry[n_step + n_chains:])
                krows = pl.ds(pl.multiple_of(kb0 * CHUNK, CHUNK), n_b * CHUNK)
                kgs = [k_ref[krows, lanes_of(pair)] for pair in range(n_step)]
                vgs = [v_ref[krows, lanes_of(pair)] for pair in range(n_step)]
                tiles = [_sb_logits(qs[ch], kgs[ch // 2], biases) for ch in range(n_chains)]
                sums = [_sb_block_sums([log_keep for _, log_keep in tiles[ch]], later) for ch in range(n_chains)]
                cols = [(c == kb0 + b).astype(F32) for b in range(n_b)]
                for ch in range(n_chains):
                    ws = [None] * n_b
                    for b in reversed(range(n_b)):
                        after, row_sum = sums[ch][b]
                        ws[b] = jnp.exp(tiles[ch][b][0] + after + runs[ch]).astype(BF16)
                        savs[ch] = savs[ch] + cols[b] * runs[ch]
                        runs[ch] = runs[ch] + row_sum
                    accs[ch // 2] = accs[ch // 2] + _dot(jnp.concatenate(ws, axis=1),
                                                         _sb_head_rows(vgs[ch // 2], head_lanes[ch % 2], n_b))
                return (*accs, *runs, *savs)

            zt = qs[0].astype(F32) * 0.0
            top, size, masks = _sb_plan(qi, padding, not_before)
            carry = (zt,) * (n_step + 2 * n_chains)
            for n_b in range(1, SB_GROUP + 1):
                carry = _once_if(size == n_b, lambda cr, n_b=n_b: blocks(SB_GROUP * top, masks(n_b), cr), carry)
            carry = lax.fori_loop(0, jnp.maximum(top - 1, 0),
                                  lambda it, cr: blocks(SB_GROUP * (top - 1 - it), [None] * SB_GROUP, cr), carry)
            carry = _once_if(top > 0, functools.partial(blocks, 0, [padding] + [None] * (SB_GROUP - 1)), carry)
            for pair in range(n_step):
                o_ref[rows, lanes_of(pair)] = carry[pair].astype(o_ref.dtype)
            for ch in range(n_chains):
                car_ref[rows, ch * CHUNK:(ch + 1) * CHUNK] = carry[n_step + n_chains + ch]
            return 0

        lax.fori_loop(0, n_blocks, q_block, 0)

    col = pl.BlockSpec((p, n_step * SB_PAIR), lambda g: (0, g))
    return pl.pallas_call(
        body,
        out_shape=[jax.ShapeDtypeStruct((p, D_MODEL), BF16), jax.ShapeDtypeStruct((p, n_pairs * 2 * CHUNK), F32)],
        grid=(n_pairs // n_step,),
        in_specs=[col, col, col],
        out_specs=[col, pl.BlockSpec((p, n_chains * CHUNK), lambda g: (0, g))],
        compiler_params=_params("parallel"),
        name=name,
    )(qn, kn, vb)


def _sb_bwd(qn, kn, vb, carries, do, *, name):
    p = qn.shape[0]
    n_blocks = p // CHUNK
    n_pairs = SB_HEADS // 2
    scale = SB_HEAD_DIM ** -0.5

    n_step = SB_PAIRS_PER_STEP_BWD
    n_chains = 2 * n_step
    lanes_of = lambda pair: slice(pair * SB_PAIR, (pair + 1) * SB_PAIR)

    def body(q_ref, k_ref, v_ref, car_ref, do_ref, dq_ref, dk_ref, dv_ref):
        head_lanes, c, later, earlier, not_before, padding = _sb_consts()
        dk_ref[...] = jnp.zeros_like(dk_ref)
        dv_ref[...] = jnp.zeros_like(dv_ref)

        def q_block(qi, _):
            rows = pl.ds(pl.multiple_of(qi * CHUNK, CHUNK), CHUNK)
            qs, doh, do2, q2 = [], [], [], []
            for pair in range(n_step):
                qh = _sb_halves(q_ref[rows, lanes_of(pair)], head_lanes)
                qs += [qh[0] * scale, qh[1] * scale]
                doh += list(_sb_halves(do_ref[rows, lanes_of(pair)].astype(BF16), head_lanes))
                do2.append(jnp.concatenate(doh[-2:], axis=0))
                q2.append(jnp.concatenate(qs[-2:], axis=0))
            sav = [car_ref[rows, ch * CHUNK:(ch + 1) * CHUNK] for ch in range(n_chains)]

            def blocks(kb0, biases, carry):
                n_b = len(biases)
                dq_accs, pres = list(carry[:n_step]), list(carry[n_step:])
                krows = pl.ds(pl.multiple_of(kb0 * CHUNK, CHUNK), n_b * CHUNK)
                kgs = [k_ref[krows, lanes_of(pair)] for pair in range(n_step)]
                vgs = [v_ref[krows, lanes_of(pair)] for pair in range(n_step)]
                cols = [(c == kb0 + b).astype(F32) for b in range(n_b)]
                block = lambda t, b: t[:, b * CHUNK:(b + 1) * CHUNK]
                tiles = [_sb_logits(qs[ch], kgs[ch // 2], biases) for ch in range(n_chains)]
                afters = [_sb_block_sums([log_keep for _, log_keep in tiles[ch]], later) for ch in range(n_chains)]
                dws = [_dot_nt(doh[ch], vgs[ch // 2]) for ch in range(n_chains)]
                ws, es, befores = [], [], []
                for ch in range(n_chains):
                    runs = [jnp.sum(cols[b] * sav[ch], axis=-1, keepdims=True) for b in range(n_b)]
                    ws.append([jnp.exp(tiles[ch][b][0] + afters[ch][b][0] + runs[b]) for b in range(n_b)])
                    es.append([ws[ch][b] * block(dws[ch], b) for b in range(n_b)])
                    befores.append(_sb_block_sums(es[ch], earlier))
                dz2, w2 = [], []
                for ch in range(n_chains):
                    dzs = []
                    for b in range(n_b):
                        before, row_sum = befores[ch][b]
                        sig = jnp.exp(tiles[ch][b][0])
                        e = es[ch][b]
                        dzs.append((e - (e + before + pres[ch]) * sig).astype(BF16))
                        pres[ch] = pres[ch] + row_sum
                    dz2.append(jnp.concatenate(dzs, axis=1))
                    w2.append(jnp.concatenate([t.astype(BF16) for t in ws[ch]], axis=1))
                    dq_accs[ch // 2] = dq_accs[ch // 2] + _dot(dz2[ch], _sb_head_rows(kgs[ch // 2], head_lanes[ch % 2], n_b))
                for pair in range(n_step):
                    both = slice(2 * pair, 2 * pair + 2)
                    dv_ref[krows, lanes_of(pair)] += _dot_tn(jnp.concatenate(w2[both], axis=0), do2[pair])
                    dk_ref[krows, lanes_of(pair)] += _dot_tn(jnp.concatenate(dz2[both], axis=0), q2[pair])
                return (*dq_accs, *pres)

            zt = qs[0].astype(F32) * 0.0
            top, size, masks = _sb_plan(qi, padding, not_before)
            carry = _once_if(top > 0, functools.partial(blocks, 0, [padding] + [None] * (SB_GROUP - 1)),
                             (zt,) * (n_step + n_chains))
            carry = lax.fori_loop(1, top, lambda g, cr: blocks(SB_GROUP * g, [None] * SB_GROUP, cr), carry)
            for n_b in range(1, SB_GROUP + 1):
                carry = _once_if(size == n_b, lambda cr, n_b=n_b: blocks(SB_GROUP * top, masks(n_b), cr), carry)
            for pair in range(n_step):
                dq_ref[rows, lanes_of(pair)] = carry[pair] * scale
            return 0

        lax.fori_loop(0, n_blocks, q_block, 0)

    col = pl.BlockSpec((p, n_step * SB_PAIR), lambda g: (0, g))
    return pl.pallas_call(
        body,
        out_shape=[jax.ShapeDtypeStruct((p, D_MODEL), F32)] * 3,
        grid=(n_pairs // n_step,),
        in_specs=[col, col, col, pl.BlockSpec((p, n_chains * CHUNK), lambda g: (0, g)), col],
        out_specs=[col, col, col],
        compiler_params=_params("parallel"),
        name=name,
    )(qn, kn, vb, carries, do)


def _loss_head(h, target, *, name):
    p, d = h.shape
    n_blocks = p // CHUNK

    def body(h_ref, t_ref, sq_ref, dh_ref):
        i = pl.program_id(0)

        @pl.when(i == 0)
        def _():
            sq_ref[...] = jnp.zeros_like(sq_ref)
            dh_ref[...] = jnp.zeros_like(dh_ref)

        @pl.when(i > 0)
        def _():
            err = h_ref[...] - t_ref[...]
            sq_ref[...] += jnp.sum(err * err)
            dh_ref[...] = err * (1.0 / d)

    return pl.pallas_call(
        body,
        out_shape=[jax.ShapeDtypeStruct((8, 128), F32), jax.ShapeDtypeStruct((p, d), F32)],
        grid=(n_blocks,),
        in_specs=[pl.BlockSpec((CHUNK, d), lambda i: (i, 0)),
                  pl.BlockSpec((CHUNK, d), lambda i: (jnp.maximum(i - 1, 0), 0))],
        out_specs=[pl.BlockSpec((8, 128), lambda i: (0, 0)), pl.BlockSpec((CHUNK, d), lambda i: (i, 0))],
        compiler_params=_params("arbitrary"),
        name=name,
    )(h, target)


def _local_step(x, target, meta, norm_mix_g, norm_mlp_g, w_in, gn_g, conv_w, conv_b, ln_g, ln_b, qn_g, kn_g, later,
                reached=lambda point, after, grads=None: None):
    seq = x.shape[0]
    p = PAD_FRONT + N_META + seq
    d = D_MODEL
    tables = _retention_tables(p)
    row = lambda v: v.reshape(1, -1)
    h0 = jnp.concatenate([jnp.zeros((PAD_FRONT, d), F32), meta, x], axis=0)

    hn0 = _rmsnorm_fwd(h0, row(norm_mix_g[0]), name="l0_mix_norm")
    proj = _matmul(hn0, w_in, mode="nn", out_dtypes=(F32,), name="l0_proj")
    og, opre, sprev = _retention_fwd(proj, gn_g, tables, name="l0_retention")
    cb, y_conv = _conv_fwd(proj, conv_w, row(conv_b), row(ln_g), row(ln_b), name="l0_conv")
    cat = jnp.concatenate([og, cb], axis=1)
    w_out, w1_0, w2_0 = later("l0", cat)
    w1, w2 = [w1_0, None], [w2_0, None]
    h1 = _matmul(cat, w_out, mode="nn", out_dtypes=(F32,), epilogue=_add_epilogue, extras=(h0,), name="l0_mix_out")
    h2, mlp0 = _mlp_fwd(h1, row(norm_mlp_g[0]), w1[0], w2[0], name="l0_mlp")

    hn1 = _rmsnorm_fwd(h2, row(norm_mix_g[1]), name="l1_mix_norm")
    (w_qkv,) = later("qkv", hn1)
    qkv = _matmul(hn1, w_qkv, mode="nn", out_dtypes=(F32,), name="l1_qkv")
    qg_t, kg_t = jnp.tile(row(qn_g), (1, SB_HEADS)), jnp.tile(row(kn_g), (1, SB_HEADS))
    qn, kn, vb = _qknorm_fwd(qkv, qg_t, kg_t, name="l1_qknorm")
    o_sb, carries = _sb_fwd(qn, kn, vb, name="l1_stickbreak")
    w_o, w1[1], w2[1] = later("l1", o_sb)
    h3 = _matmul(o_sb, w_o, mode="nn", out_dtypes=(F32,), epilogue=_add_epilogue, extras=(h2,), name="l1_mix_out")
    h4, mlp1 = _mlp_fwd(h3, row(norm_mlp_g[1]), w1[1], w2[1], name="l1_mlp")

    sq, dh4 = _loss_head(h4, target, name="loss_head")

    dh3, dg_mlp1, dw1_1, dw2_1 = _mlp_bwd(h3, row(norm_mlp_g[1]), w1[1], w2[1], mlp1, dh4, name="l1_mlp_bwd")
    do_sb = _matmul(dh3, w_o, mode="nt", out_dtypes=(F32,), name="l1_do")
    dw_o = _matmul(o_sb, dh3, mode="tn", out_dtypes=(F32,), name="l1_dwo")
    dqn, dkn, dv = _sb_bwd(qn, kn, vb, carries, do_sb, name="l1_stickbreak_bwd")
    dqkv, dqg_t, dkg_t = _qknorm_bwd(qkv, qg_t, kg_t, dqn, dkn, dv, name="l1_qknorm_bwd")
    dw_qkv = _matmul(hn1, dqkv, mode="tn", out_dtypes=(F32,), name="l1_dwqkv")
    pin = lambda arr, tok: arr if tok is None else arr + tok[0:1, 0:1]
    tok = reached("l1_grads", dw_qkv, dict(odd_w_qkv=dw_qkv, odd_w_o=dw_o, mlp_w1_1=dw1_1, mlp_w2_1=dw2_1))
    dhn1 = _matmul(dqkv, w_qkv, mode="nt", out_dtypes=(F32,), name="l1_dhn", after=tok)
    dh2, dg_mix1 = _rmsnorm_bwd(h2, row(norm_mix_g[1]), dhn1, dh3, name="l1_mix_dnorm")
    tok = reached("l1_done", dh2)

    dh1, dg_mlp0, dw1_0, dw2_0 = _mlp_bwd(h1, row(norm_mlp_g[0]), w1[0], w2[0], mlp0, dh2, name="l0_mlp_bwd", after=tok)
    tok = reached("l0_mlp_grads", dh1, dict(mlp_w1_0=dw1_0, mlp_w2_0=dw2_0))
    dcat = _matmul(dh1, w_out, mode="nt", out_dtypes=(F32,), name="l0_dcat", after=tok)
    dw_out = _matmul(cat, dh1, mode="tn", out_dtypes=(F32,), name="l0_dwout")
    tok = reached("l0_dwout", dw_out, dict(even_w_out=dw_out))
    dq, dk, dvr, dgate_r, dgn = _retention_bwd(proj, pin(gn_g, tok), tables, opre, sprev, dcat, name="l0_retention_bwd")
    tok = reached("l0_retention_bwd", dq)
    da, dgate_c, dconv_w, dconv_b, dln_g, dln_b = _conv_bwd(proj, conv_w, pin(row(ln_g), tok), row(ln_b), y_conv, dcat,
                                                            name="l0_conv_bwd")
    tok = reached("l0_conv_bwd", da)
    dproj = jnp.concatenate([dq, dk, dvr, dgate_r, da, dgate_c], axis=1)
    dw_in = _matmul(hn0, dproj, mode="tn", out_dtypes=(F32,), name="l0_dwin", after=tok)
    dhn0 = _matmul(dproj, w_in, mode="nt", out_dtypes=(F32,), name="l0_dhn")
    dh0, dg_mix0 = _rmsnorm_bwd(h0, row(norm_mix_g[0]), dhn0, dh1, name="l0_mix_dnorm")

    fold = lambda t: t.reshape(SB_HEADS, SB_HEAD_DIM).sum(axis=0)
    grads = dict(
        x=dh0[PAD_FRONT + N_META:],
        meta=dh0[PAD_FRONT:PAD_FRONT + N_META],
        norm_mix_g=jnp.concatenate([dg_mix0, dg_mix1], axis=0),
        norm_mlp_g=jnp.concatenate([dg_mlp0, dg_mlp1], axis=0),
        even_w_in=dw_in,
        even_ret_gn_g=dgn.reshape(RET_HEADS, RET_V_DIM),
        even_conv_w=dconv_w,
        even_conv_b=dconv_b,
        even_conv_ln_g=dln_g,
        even_conv_ln_b=dln_b,
        even_w_out=dw_out,
        odd_w_qkv=dw_qkv,
        odd_q_norm_g=fold(dqg_t)[None],
        odd_k_norm_g=fold(dkg_t)[None],
        odd_w_o=dw_o,
        mlp_w1=(dw1_0, dw1_1),
        mlp_w2=(dw2_0, dw2_1),
    )
    return sq[0, 0], grads


def _position():
    x, y, c = lax.axis_index("x"), lax.axis_index("y"), lax.axis_index("c")
    other_chips = [(1 - x, y), (x, 1 - y), (1 - x, 1 - y)]
    return x, y, c, other_chips


def _shard_of(ref, kind, s, n):
    rows, cols = ref.shape
    if kind == "col":
        return ref.at[:, pl.ds(s * (cols // n), cols // n)]
    return ref.at[pl.ds(s * (rows // n), rows // n), :]


def _half_of(ref, kind, c):
    rows, cols = ref.shape
    if kind == "col":
        return ref.at[pl.ds(c * (rows // 2), rows // 2), :]
    return ref.at[:, pl.ds(c * (cols // 2), cols // 2)]


def _remote(src, dst, send_sems, recv_sems, idx, device):
    return pltpu.make_async_remote_copy(src_ref=src, dst_ref=dst, send_sem=send_sems.at[idx], recv_sem=recv_sems.at[idx],
                                        device_id=device, device_id_type=MESH)


def _cast_into_whole(w, kind, s_arr, *, name):
    rows, cols = w.shape
    tr = _pick(rows, (256, 128))
    nb = rows // tr
    if kind == "col":
        whole, o_spec = (rows, cols * N_CHIPS), pl.BlockSpec((tr, cols), lambda i, s_ref: (i, s_ref[0]))
    else:
        whole, o_spec = (rows * N_CHIPS, cols), pl.BlockSpec((tr, cols), lambda i, s_ref: (s_ref[0] * nb + i, 0))

    def body(s_ref, w_ref, o_ref):
        o_ref[...] = w_ref[...].astype(BF16)

    return pl.pallas_call(
        body,
        out_shape=jax.ShapeDtypeStruct(whole, BF16),
        grid_spec=pltpu.PrefetchScalarGridSpec(num_scalar_prefetch=1, grid=(nb,),
                                               in_specs=[pl.BlockSpec((tr, cols), lambda i, s_ref: (i, 0))],
                                               out_specs=o_spec),
        compiler_params=_params("parallel"),
        name=name,
    )(s_arr, w)


def _allgather_weights(wholes, kinds):
    n = len(wholes)

    def body(*refs):
        ins, outs = refs[:n], refs[n:2 * n]
        send_sems, recv_sems = refs[2 * n:]
        x, y, c, chips = _position()
        me_chip = 2 * x + y
        sibling = (x, y, 1 - c)
        sends = []
        for t in range(n):
            for k, (cx, cy) in enumerate(chips):
                src = _half_of(_shard_of(ins[t], kinds[t], me_chip, N_CHIPS), kinds[t], c)
                dst = _half_of(_shard_of(outs[t], kinds[t], me_chip, N_CHIPS), kinds[t], c)
                sends.append(_remote(src, dst, send_sems, recv_sems, 6 * t + k, (cx, cy, c)))
        for cp in sends:
            cp.start()
        passed = []
        for t in range(n):
            for k, (cx, cy) in enumerate(chips):
                landed = _half_of(_shard_of(outs[t], kinds[t], 2 * cx + cy, N_CHIPS), kinds[t], c)
                _remote(landed, landed, send_sems, recv_sems, 6 * t + k, (cx, cy, c)).wait_recv()
                fwd = _remote(landed, landed, send_sems, recv_sems, 6 * t + 3 + k, sibling)
                fwd.start()
                passed.append(fwd)
        for t in range(n):
            for k, (cx, cy) in enumerate(chips):
                theirs = _half_of(_shard_of(outs[t], kinds[t], 2 * cx + cy, N_CHIPS), kinds[t], 1 - c)
                _remote(theirs, theirs, send_sems, recv_sems, 6 * t + 3 + k, sibling).wait_recv()
        for cp in sends + passed:
            cp.wait_send()

    return pl.pallas_call(
        body,
        out_shape=[jax.ShapeDtypeStruct(w.shape, BF16) for w in wholes],
        in_specs=[ANY] * n,
        out_specs=[ANY] * n,
        input_output_aliases={t: t for t in range(n)},
        scratch_shapes=[pltpu.SemaphoreType.DMA((6 * n,)), pltpu.SemaphoreType.DMA((6 * n,))],
        name="allgather_weights",
    )(*wholes)


HBM = pl.BlockSpec(memory_space=pltpu.HBM)
SEM = pl.BlockSpec(memory_space=pltpu.SEMAPHORE)
DATAFLOW = pltpu.SideEffectType.DATAFLOW_SIDE_EFFECTING
TARGETS = 6


def _gather_copies(kinds, refs, _, send_sems, recv_sems):
    x, y, c, chips = _position()
    me_chip = 2 * x + y
    sends, lands = [], []
    for t, (ref, kind) in enumerate(zip(refs, kinds)):
        mine = _half_of(_shard_of(ref, kind, me_chip, N_CHIPS), kind, c)
        for k, (cx, cy) in enumerate(chips):
            for other_core in range(2):
                j = TARGETS * t + 2 * k + other_core
                peer_c = 1 - c if other_core else c
                sends.append(_remote(mine, mine, send_sems, recv_sems, j, (cx, cy, peer_c)))
                theirs = _half_of(_shard_of(ref, kind, 2 * cx + cy, N_CHIPS), kind, peer_c)
                lands.append(_remote(theirs, theirs, send_sems, recv_sems, j, (cx, cy, peer_c)))
    return sends, lands


def _pair_swap_copies(kinds, srcs, lands, send_sems, recv_sems):
    x, y, c, _ = _position()
    sibling = (x, y, 1 - c)
    sends = [_remote(_half_of(srcs[t], kinds[t], 1 - c), lands[t], send_sems, recv_sems, t, sibling) for t in range(len(srcs))]
    arrivals = [_remote(_half_of(srcs[t], kinds[t], c), lands[t], send_sems, recv_sems, t, sibling) for t in range(len(srcs))]
    return sends, arrivals


def _chip_exchange_copies(kinds, srcs, lands, send_sems, recv_sems):
    x, y, c, chips = _position()
    sends, arrivals = [], []
    for t in range(len(srcs)):
        for k, (cx, cy) in enumerate(chips):
            src = _shard_of(srcs[t], kinds[t], 2 * cx + cy, N_CHIPS)
            sends.append(_remote(src, lands[t].at[k], send_sems, recv_sems, 3 * t + k, (cx, cy, c)))
            arrivals.append(_remote(src, lands[t].at[k], send_sems, recv_sems, 3 * t + k, (cx, cy, c)))
    return sends, arrivals


def _pair_gather_copies(kinds, srcs, lands, send_sems, recv_sems):
    x, y, c, _ = _position()
    sibling = (x, y, 1 - c)
    sends, arrivals = [], []
    for t in range(len(srcs)):
        mine, theirs = _half_of(srcs[t], kinds[t], c), _half_of(srcs[t], kinds[t], 1 - c)
        sends.append(_remote(mine, mine, send_sems, recv_sems, t, sibling))
        arrivals.append(_remote(theirs, theirs, send_sems, recv_sems, t, sibling))
    return sends, arrivals


def _copies_start(plan, n_sems, srcs, lands, follows, *, name):
    ns, n = len(srcs), len(srcs) + len(lands)

    def body(*refs):
        send_sems, recv_sems = refs[n + 1], refs[n + 2]
        thru, token = refs[n + 3:2 * n + 3], refs[2 * n + 3]
        sends, _ = plan(thru[:ns], thru[ns:], send_sems, recv_sems)
        for cp in sends:
            cp.start()
        token[...] = jnp.zeros_like(token)

    arrays = [pltpu.with_memory_space_constraint(a, pltpu.HBM) for a in list(srcs) + list(lands)]
    outs = pl.pallas_call(
        body,
        name=name,
        out_shape=(pltpu.SemaphoreType.DMA((n_sems,)), pltpu.SemaphoreType.DMA((n_sems,)),
                   *[pltpu.HBM(a.shape, a.dtype) for a in arrays], jax.ShapeDtypeStruct((8, 128), F32)),
        in_specs=(*[HBM] * n, ANY),
        out_specs=(SEM, SEM, *[HBM] * n, pl.BlockSpec(memory_space=pltpu.VMEM)),
        input_output_aliases={t: 2 + t for t in range(n)},
        compiler_params=pltpu.CompilerParams(has_side_effects=DATAFLOW),
    )(*arrays, follows)
    return outs[0], outs[1], list(outs[2:2 + ns]), list(outs[2 + ns:2 + n]), outs[2 + n]


def _copies_wait(plan, started, follows, *, name):
    send_sems, recv_sems, srcs, lands, _ = started
    ns, n = len(srcs), len(srcs) + len(lands)

    def body(*refs):
        ins, s_sems, r_sems = refs[:n], refs[n], refs[n + 1]
        sends, arrivals = plan(ins[:ns], ins[ns:], s_sems, r_sems)
        for cp in sends:
            cp.wait_send()
        for cp in arrivals:
            cp.wait_recv()

    outs = pl.pallas_call(
        body,
        name=name,
        out_shape=tuple(pltpu.HBM(a.shape, a.dtype) for a in srcs + lands),
        in_specs=(*[HBM] * n, SEM, SEM, ANY),
        out_specs=tuple([HBM] * n),
        input_output_aliases={t: t for t in range(n)},
        compiler_params=pltpu.CompilerParams(has_side_effects=DATAFLOW),
    )(*srcs, *lands, send_sems, recv_sems, follows)
    return list(outs[:ns]), list(outs[ns:])


def _allgather8(block, *, name):
    rows, cols = block.shape

    def body(in_ref, out_ref, send_sems, recv_sems, local_sem):
        x, y, c, _ = _position()
        me = 4 * x + 2 * y + c
        mine = pltpu.make_async_copy(in_ref, out_ref.at[me], local_sem)
        mine.start()
        peers = []
        for flip in range(1, N_DEV):
            fx, fy, fc = (flip >> 2) & 1, (flip >> 1) & 1, flip & 1
            peers.append(((1 - x if fx else x), (1 - y if fy else y), (1 - c if fc else c)))
        sends = [_remote(in_ref, out_ref.at[me], send_sems, recv_sems, j, peer) for j, peer in enumerate(peers)]
        for cp in sends:
            cp.start()
        for j, (px, py, pc) in enumerate(peers):
            slot = out_ref.at[4 * px + 2 * py + pc]
            _remote(slot, slot, send_sems, recv_sems, j, (px, py, pc)).wait_recv()
        for cp in sends:
            cp.wait_send()
        mine.wait()

    vmem = pl.BlockSpec(memory_space=pltpu.VMEM)
    return pl.pallas_call(
        body,
        out_shape=jax.ShapeDtypeStruct((N_DEV, rows, cols), F32),
        in_specs=[vmem],
        out_specs=vmem,
        scratch_shapes=[pltpu.SemaphoreType.DMA((N_DEV - 1,)), pltpu.SemaphoreType.DMA((N_DEV - 1,)),
                        pltpu.SemaphoreType.DMA],
        name=name,
    )(block)


def _sum8(stack, *, name):
    _, rows, cols = stack.shape

    def body(s_ref, o_ref):
        acc = s_ref[0]
        for i in range(1, N_DEV):
            acc = acc + s_ref[i]
        o_ref[...] = acc

    return pl.pallas_call(body, out_shape=jax.ShapeDtypeStruct((rows, cols), F32), name=name)(stack)


def _swap_halves_in(grads, kinds):
    n = len(grads)

    def body(*refs):
        ins, outs = refs[:n], refs[n:2 * n]
        send_sems, recv_sems = refs[2 * n:]
        x, y, c, _ = _position()
        sibling = (x, y, 1 - c)
        sends = [_remote(_half_of(ins[t], kinds[t], 1 - c), outs[t], send_sems, recv_sems, t, sibling) for t in range(n)]
        for cp in sends:
            cp.start()
        for t in range(n):
            _remote(_half_of(ins[t], kinds[t], c), outs[t], send_sems, recv_sems, t, sibling).wait_recv()
        for cp in sends:
            cp.wait_send()

    def half(g, kind):
        rows, cols = g.shape
        return (rows // 2, cols) if kind == "col" else (rows, cols // 2)

    return pl.pallas_call(
        body,
        out_shape=[jax.ShapeDtypeStruct(half(g, k), F32) for g, k in zip(grads, kinds)],
        in_specs=[ANY] * n,
        out_specs=[ANY] * n,
        scratch_shapes=[pltpu.SemaphoreType.DMA((n,)), pltpu.SemaphoreType.DMA((n,))],
        name="reduce_core_pair",
    )(*grads)


def _half_add(grad, theirs, kind, c_arr, *, name):
    rows, cols = theirs.shape
    tr = _pick(rows, (256, 128))
    nb = rows // tr
    if kind == "col":
        g_spec = pl.BlockSpec((tr, cols), lambda i, c_ref: (c_ref[0] * nb + i, 0))
    else:
        g_spec = pl.BlockSpec((tr, cols), lambda i, c_ref: (i, c_ref[0]))
    t_spec = pl.BlockSpec((tr, cols), lambda i, c_ref: (i, 0))

    def body(c_ref, g_ref, t_ref, o32_ref, o16_ref):
        tot = g_ref[...] + t_ref[...]
        o32_ref[...] = tot
        o16_ref[...] = tot.astype(BF16)

    return pl.pallas_call(
        body,
        out_shape=[jax.ShapeDtypeStruct((rows, cols), F32), jax.ShapeDtypeStruct((rows, cols), BF16)],
        grid_spec=pltpu.PrefetchScalarGridSpec(num_scalar_prefetch=1, grid=(nb,), in_specs=[g_spec, t_spec],
                                               out_specs=[t_spec, t_spec]),
        compiler_params=_params("parallel"),
        name=name,
    )(c_arr, grad, theirs)


def _exchange_chips(parts, kinds):
    n = len(parts)

    def body(*refs):
        ins, outs = refs[:n], refs[n:2 * n]
        send_sems, recv_sems = refs[2 * n:]
        x, y, c, chips = _position()
        sends = []
        for t in range(n):
            for k, (cx, cy) in enumerate(chips):
                src = _shard_of(ins[t], kinds[t], 2 * cx + cy, N_CHIPS)
                sends.append(_remote(src, outs[t].at[k], send_sems, recv_sems, 3 * t + k, (cx, cy, c)))
        for cp in sends:
            cp.start()
        for t in range(n):
            for k, (cx, cy) in enumerate(chips):
                src = _shard_of(ins[t], kinds[t], 2 * cx + cy, N_CHIPS)
                _remote(src, outs[t].at[k], send_sems, recv_sems, 3 * t + k, (cx, cy, c)).wait_recv()
        for cp in sends:
            cp.wait_send()

    def piece(p, kind):
        rows, cols = p.shape
        return (3, rows, cols // N_CHIPS) if kind == "col" else (3, rows // N_CHIPS, cols)

    return pl.pallas_call(
        body,
        out_shape=[jax.ShapeDtypeStruct(piece(p, k), BF16) for p, k in zip(parts, kinds)],
        in_specs=[ANY] * n,
        out_specs=[ANY] * n,
        scratch_shapes=[pltpu.SemaphoreType.DMA((3 * n,)), pltpu.SemaphoreType.DMA((3 * n,))],
        name="reduce_chips",
    )(*parts)


def _shard_sum(part32, recv, kind, sc_arr, *, name):
    _, rows, cols = recv.shape
    tr = _pick(rows, (256, 128))
    nb = rows // tr
    if kind == "col":
        whole = (2 * rows, cols)
        p_spec = pl.BlockSpec((tr, cols), lambda i, sc: (i, sc[0]))
        o_spec = pl.BlockSpec((tr, cols), lambda i, sc: (sc[1] * nb + i, 0))
    else:
        whole = (rows, 2 * cols)
        p_spec = pl.BlockSpec((tr, cols), lambda i, sc: (sc[0] * nb + i, 0))
        o_spec = pl.BlockSpec((tr, cols), lambda i, sc: (i, sc[1]))
    r_spec = pl.BlockSpec((3, tr, cols), lambda i, sc: (0, i, 0))

    def body(sc_ref, p_ref, r_ref, o_ref):
        acc = p_ref[...]
        for k in range(3):
            acc = acc + r_ref[k].astype(F32)
        o_ref[...] = acc

    return pl.pallas_call(
        body,
        out_shape=jax.ShapeDtypeStruct(whole, F32),
        grid_spec=pltpu.PrefetchScalarGridSpec(num_scalar_prefetch=1, grid=(nb,), in_specs=[p_spec, r_spec],
                                               out_specs=o_spec),
        compiler_params=_params("parallel"),
        name=name,
    )(sc_arr, part32, recv)


def _swap_halves_out(shards, kinds):
    n = len(shards)

    def body(*refs):
        ins, outs = refs[:n], refs[n:2 * n]
        send_sems, recv_sems = refs[2 * n:]
        x, y, c, _ = _position()
        sibling = (x, y, 1 - c)
        sends = [_remote(_half_of(ins[t], kinds[t], c), _half_of(outs[t], kinds[t], c), send_sems, recv_sems, t, sibling)
                 for t in range(n)]
        for cp in sends:
            cp.start()
        for t in range(n):
            theirs = _half_of(outs[t], kinds[t], 1 - c)
            _remote(theirs, theirs, send_sems, recv_sems, t, sibling).wait_recv()
        for cp in sends:
            cp.wait_send()

    return pl.pallas_call(
        body,
        out_shape=[jax.ShapeDtypeStruct(s.shape, F32) for s in shards],
        in_specs=[ANY] * n,
        out_specs=[ANY] * n,
        input_output_aliases={t: t for t in range(n)},
        scratch_shapes=[pltpu.SemaphoreType.DMA((n,)), pltpu.SemaphoreType.DMA((n,))],
        name="gather_core_pair",
    )(*shards)


def _adamw(w, g, m, v, *, name):
    rows, cols = w.shape
    tr = _pick(rows, (256, 128)) if rows * cols > 64 * 1024 else rows

    def body(w_ref, g_ref, m_ref, v_ref, d_ref, nm_ref, nv_ref):
        gv = g_ref[...]
        nm = ADAM_B1 * m_ref[...] + (1.0 - ADAM_B1) * gv
        nv = ADAM_B2 * v_ref[...] + (1.0 - ADAM_B2) * jnp.square(gv)
        m_hat = nm / (1.0 - ADAM_B1 ** ADAM_STEP)
        v_hat = nv / (1.0 - ADAM_B2 ** ADAM_STEP)
        d_ref[...] = -ADAM_LR * (m_hat / (jnp.sqrt(v_hat) + ADAM_EPS) + ADAM_WD * w_ref[...])
        nm_ref[...] = nm
        nv_ref[...] = nv

    spec = pl.BlockSpec((tr, cols), lambda i: (i, 0))
    return pl.pallas_call(
        body,
        out_shape=[jax.ShapeDtypeStruct((rows, cols), F32)] * 3,
        grid=(rows // tr,),
        in_specs=[spec] * 4,
        out_specs=[spec] * 3,
        compiler_params=_params("parallel"),
        name=name,
    )(w, g, m, v)


BIG = ("even_w_in", "odd_w_qkv", "mlp_w1_0", "mlp_w1_1", "even_w_out", "odd_w_o", "mlp_w2_0", "mlp_w2_1")
BIG_KIND = ("col", "col", "col", "col", "row", "row", "row", "row")


class _TravellingReduction:
    def __init__(self, tag, names, kinds, c_arr, sc_arr):
        self.tag, self.names, self.kinds, self.c_arr, self.sc_arr = tag, names, kinds, c_arr, sc_arr
        self.swap = functools.partial(_pair_swap_copies, kinds)
        self.exchange = functools.partial(_chip_exchange_copies, kinds)
        self.gather = functools.partial(_pair_gather_copies, kinds)

    def pair_swap_start(self, grads, follows):
        half = lambda g, kind: (g.shape[0] // 2, g.shape[1]) if kind == "col" else (g.shape[0], g.shape[1] // 2)
        lands = [lax.empty(half(g, k), F32) for g, k in zip(grads, self.kinds)]
        self.started = _copies_start(self.swap, len(grads), grads, lands, follows, name=f"reduce_{self.tag}_pair_start")

    def pair_swap_finish(self, after):
        grads, theirs = _copies_wait(self.swap, self.started, after, name=f"reduce_{self.tag}_pair_wait")
        self.sums = [_half_add(g, th, k, self.c_arr, name="pair_sum_" + n)
                     for g, th, k, n in zip(grads, theirs, self.kinds, self.names)]

    def chips_start(self, follows):
        parts = [s16 for _, s16 in self.sums]
        piece = lambda p, kind: (3, p.shape[0], p.shape[1] // N_CHIPS) if kind == "col" else (3, p.shape[0] // N_CHIPS, p.shape[1])
        lands = [lax.empty(piece(p, k), BF16) for p, k in zip(parts, self.kinds)]
        self.started = _copies_start(self.exchange, 3 * len(parts), parts, lands, follows,
                                     name=f"reduce_{self.tag}_chips_start")

    def chips_finish(self, after):
        _, recv = _copies_wait(self.exchange, self.started, after, name=f"reduce_{self.tag}_chips_wait")
        self.halves = [_shard_sum(s32, r, k, self.sc_arr, name="chip_sum_" + n)
                       for (s32, _), r, k, n in zip(self.sums, recv, self.kinds, self.names)]

    def pair_gather_start(self, follows):
        self.started = _copies_start(self.gather, len(self.halves), self.halves, [], follows,
                                     name=f"reduce_{self.tag}_gather_start")

    def pair_gather_finish(self, after):
        shards, _ = _copies_wait(self.gather, self.started, after, name=f"reduce_{self.tag}_gather_wait")
        return dict(zip(self.names, shards))
SUBLANES = 8


def _pack_rows(parts, width):
    padded, offsets, r0 = [], [], 0
    for t in parts:
        rows = -(-t.shape[0] // SUBLANES) * SUBLANES
        padded.append(jnp.pad(t, ((0, rows - t.shape[0]), (0, width - t.shape[1]))))
        offsets.append(r0)
        r0 += rows
    return jnp.concatenate(padded, axis=0), offsets


def kernel(x, meta, norm_mix_g, norm_mlp_g, even_w_in, even_ret_gn_g, even_conv_w, even_conv_b, even_conv_ln_g, even_conv_ln_b, even_w_out, odd_w_qkv, odd_q_norm_g, odd_k_norm_g, odd_w_o, mlp_w1, mlp_w2, loss_target, m_meta, m_norm_mix_g, m_norm_mlp_g, m_even_w_in, m_even_ret_gn_g, m_even_conv_w, m_even_conv_b, m_even_conv_ln_g, m_even_conv_ln_b, m_even_w_out, m_odd_w_qkv, m_odd_q_norm_g, m_odd_k_norm_g, m_odd_w_o, m_mlp_w1, m_mlp_w2, v_meta, v_norm_mix_g, v_norm_mlp_g, v_even_w_in, v_even_ret_gn_g, v_even_conv_w, v_even_conv_b, v_even_conv_ln_g, v_even_conv_ln_b, v_even_w_out, v_odd_w_qkv, v_odd_q_norm_g, v_odd_k_norm_g, v_odd_w_o, v_mlp_w1, v_mlp_w2):
    d = D_MODEL
    xi, yi, ci = lax.axis_index("x"), lax.axis_index("y"), lax.axis_index("c")
    chip = 2 * xi + yi
    c_arr = jnp.reshape(ci, (1,)).astype(jnp.int32)
    s_arr = jnp.reshape(chip, (1,)).astype(jnp.int32)

    def split_big(w_in, w_qkv, w1, w_out, w_o, w2):
        return dict(zip(BIG, (w_in[0], w_qkv[0], w1[0], w1[1], w_out[0], w_o[0], w2[0], w2[1])))

    w_big = split_big(even_w_in, odd_w_qkv, mlp_w1, even_w_out, odd_w_o, mlp_w2)
    m_big = split_big(m_even_w_in, m_odd_w_qkv, m_mlp_w1, m_even_w_out, m_odd_w_o, m_mlp_w2)
    v_big = split_big(v_even_w_in, v_odd_w_qkv, v_mlp_w1, v_even_w_out, v_odd_w_o, v_mlp_w2)

    placed = {n: _cast_into_whole(w_big[n], k, s_arr, name="cast_" + n) for n, k in zip(BIG, BIG_KIND)}
    kind_of = dict(zip(BIG, BIG_KIND))
    (w_in_full,) = _allgather_weights([placed["even_w_in"]], [kind_of["even_w_in"]])
    packed, (r_meta, r_conv, r_gn) = _pack_rows([meta, even_conv_w[0], even_ret_gn_g[0]], d // N_CHIPS)
    gathered = _allgather8(packed, name="allgather_small_params")[0::2]
    groups = dict(l0=("even_w_out", "mlp_w1_0", "mlp_w2_0"), qkv=("odd_w_qkv",), l1=("odd_w_o", "mlp_w1_1", "mlp_w2_1"))
    in_flight, follows = {}, gathered[0, 0:1, 0:1] + w_in_full[0:1, 0:1].astype(F32)
    for group, names in groups.items():
        plan = functools.partial(_gather_copies, [kind_of[n] for n in names])
        in_flight[group] = (plan, _copies_start(plan, TARGETS * len(names), [placed[n] for n in names], [], follows,
                                                name="gather_" + group + "_start"))
        follows = in_flight[group][1][-1]
    started = follows[0:1, 0:1]

    def later(group, after):
        plan, state = in_flight[group]
        return _copies_wait(plan, state, after, name="gather_" + group + "_wait")[0]

    sc_arr = jnp.concatenate([s_arr, c_arr])
    early = ("odd_w_qkv", "odd_w_o", "mlp_w1_1", "mlp_w2_1"), ("mlp_w1_0", "mlp_w2_0"), ("even_w_out",)
    red_l1, red_m0, red_o0 = (_TravellingReduction(tag, names, [kind_of[n] for n in names], c_arr, sc_arr)
                              for tag, names in zip(("l1", "m0", "o0"), early))
    grad_big = {}

    def reached(point, after, grads=None):
        if point == "l1_grads":
            red_l1.pair_swap_start([grads[n] for n in red_l1.names], after)
            return red_l1.started[-1]
        if point == "l1_done":
            red_l1.pair_swap_finish(after)
            red_l1.chips_start(after)
            return red_l1.started[-1]
        if point == "l0_mlp_grads":
            red_m0.pair_swap_start([grads[n] for n in red_m0.names], after)
            return red_m0.started[-1]
        if point == "l0_dwout":
            red_m0.pair_swap_finish(after)
            red_m0.chips_start(after)
            red_o0.pair_swap_start([grads[n] for n in red_o0.names], red_m0.started[-1])
            return red_o0.started[-1]
        if point == "l0_retention_bwd":
            red_l1.chips_finish(after)
            red_l1.pair_gather_start(after)
            red_o0.pair_swap_finish(after)
            red_o0.chips_start(red_l1.started[-1])
            return red_o0.started[-1]
        if point == "l0_conv_bwd":
            red_m0.chips_finish(after)
            red_m0.pair_gather_start(after)
            grad_big.update(red_l1.pair_gather_finish(after))
            red_o0.chips_finish(after)
            red_o0.pair_gather_start(red_m0.started[-1])
            return red_o0.started[-1]
        return None

    across = lambda r0, rows, width: jnp.concatenate([gathered[s, r0:r0 + rows, 0:width] for s in range(N_CHIPS)], axis=1)
    meta_full = across(r_meta, N_META, d // N_CHIPS) + started
    conv_w_full = across(r_conv, CONV_WIDTH, d // N_CHIPS)
    gn_full = across(r_gn, RET_HEADS, RET_V_DIM // N_CHIPS)

    sq, g = _local_step(
        x[0], loss_target[0], meta_full, norm_mix_g, norm_mlp_g, w_in_full, gn_full, conv_w_full,
        even_conv_b[0], even_conv_ln_g[0], even_conv_ln_b[0], odd_q_norm_g[0], odd_k_norm_g[0], later, reached)
    grad_big.update(red_m0.pair_gather_finish(g["even_w_in"]))
    grad_big.update(red_o0.pair_gather_finish(g["even_w_in"]))
    loss = lax.psum(0.5 * sq / d, ("x", "y", "c"))

    small_names = ("norm_mix_g", "norm_mlp_g", "even_conv_b", "even_conv_ln_g", "even_conv_ln_b", "odd_q_norm_g",
                   "odd_k_norm_g", "meta", "even_conv_w", "even_ret_gn_g")
    pack, offsets = _pack_rows([g[n] for n in small_names], d)
    summed = _sum8(_allgather8(pack, name="allgather_small_grads"), name="sum_small_grads")
    small = {n: summed[r0:r0 + g[n].shape[0], 0:g[n].shape[1]] for n, r0 in zip(small_names, offsets)}
    for n in ("meta", "even_conv_w", "even_ret_gn_g"):
        width = small[n].shape[1] // N_CHIPS
        small[n] = lax.dynamic_slice_in_dim(small[n], chip * width, width, axis=1)

    last = ("even_w_in",)
    last_kinds = [kind_of[n] for n in last]
    g_last = [g[n] for n in last]
    theirs = _swap_halves_in(g_last, last_kinds)
    sums = [_half_add(gb, th, k, c_arr, name="pair_sum_" + n) for gb, th, k, n in zip(g_last, theirs, last_kinds, last)]
    recv = _exchange_chips([s16 for _, s16 in sums], last_kinds)
    halves = [_shard_sum(s32, r, k, sc_arr, name="chip_sum_" + n) for (s32, _), r, k, n in zip(sums, recv, last_kinds, last)]
    grad_big.update(zip(last, _swap_halves_out(halves, last_kinds)))

    stacked = dict(mlp_w1=(mlp_w1, m_mlp_w1, v_mlp_w1), mlp_w2=(mlp_w2, m_mlp_w2, v_mlp_w2))
    flat = lambda t: t.reshape(-1, t.shape[-1])
    for n in stacked:
        grad_big[n] = jnp.concatenate([grad_big.pop(n + "_0"), grad_big.pop(n + "_1")], axis=0)
    upd = {n: _adamw(w_big[n], grad_big[n], m_big[n], v_big[n], name="adamw_" + n) for n in BIG if n in grad_big}
    upd.update({n: _adamw(flat(w), grad_big[n], flat(m), flat(v), name="adamw_" + n) for n, (w, m, v) in stacked.items()})

    def join(name, idx, lead):
        t = upd[name][idx] if idx >= 0 else grad_big[name]
        if name in stacked:
            return t.reshape(stacked[name][0].shape)
        return t[None] if lead else t

    small_w = dict(meta=meta, norm_mix_g=norm_mix_g, norm_mlp_g=norm_mlp_g, even_ret_gn_g=even_ret_gn_g[0],
                   even_conv_w=even_conv_w[0], even_conv_b=even_conv_b, even_conv_ln_g=even_conv_ln_g,
                   even_conv_ln_b=even_conv_ln_b, odd_q_norm_g=odd_q_norm_g, odd_k_norm_g=odd_k_norm_g)
    small_m = dict(meta=m_meta, norm_mix_g=m_norm_mix_g, norm_mlp_g=m_norm_mlp_g, even_ret_gn_g=m_even_ret_gn_g[0],
                   even_conv_w=m_even_conv_w[0], even_conv_b=m_even_conv_b, even_conv_ln_g=m_even_conv_ln_g,
                   even_conv_ln_b=m_even_conv_ln_b, odd_q_norm_g=m_odd_q_norm_g, odd_k_norm_g=m_odd_k_norm_g)
    small_v = dict(meta=v_meta, norm_mix_g=v_norm_mix_g, norm_mlp_g=v_norm_mlp_g, even_ret_gn_g=v_even_ret_gn_g[0],
                   even_conv_w=v_even_conv_w[0], even_conv_b=v_even_conv_b, even_conv_ln_g=v_even_conv_ln_g,
                   even_conv_ln_b=v_even_conv_ln_b, odd_q_norm_g=v_odd_q_norm_g, odd_k_norm_g=v_odd_k_norm_g)
    small_upd = {n: _adamw(small_w[n], small[n], small_m[n], small_v[n], name="adamw_" + n) for n in small_w}
    leading = ("even_ret_gn_g", "even_conv_w")

    order = ("meta", "norm_mix_g", "norm_mlp_g", "even_w_in", "even_ret_gn_g", "even_conv_w", "even_conv_b",
             "even_conv_ln_g", "even_conv_ln_b", "even_w_out", "odd_w_qkv", "odd_q_norm_g", "odd_k_norm_g", "odd_w_o",
             "mlp_w1", "mlp_w2")
    big_lead = ("even_w_in", "even_w_out", "odd_w_qkv", "odd_w_o")

    def leaf(name, idx):
        if name in small_w:
            t = small_upd[name][idx] if idx >= 0 else small[name]
            return t[None] if name in leading else t
        return join(name, idx, name in big_lead)

    outs = [loss, g["x"][None]]
    for idx in (-1, 0, 1, 2):
        outs += [leaf(n, idx) for n in order]
    return tuple(outs)
```

```python
import functools

import jax
import jax.numpy as jnp
from jax import lax
from jax.experimental import pallas as pl
from jax.experimental.pallas import tpu as pltpu

F32 = jnp.float32
BF16 = jnp.bfloat16

D_MODEL = 1024
N_META = 16
CHUNK = 128
PAD_FRONT = (-N_META) % CHUNK
RET_HEADS = 4
RET_QK_DIM = 128
RET_V_DIM = 256
RET_QK_W = RET_HEADS * RET_QK_DIM
RET_V_W = RET_HEADS * RET_V_DIM
CONV_WIDTH = 31
CONV_HALO = 32
RET_DECAY_OFFSET = 5.0
ROPE_BASE = 10000.0
SB_HEADS = 16
SB_HEAD_DIM = 64
D_FF = 4 * D_MODEL
EPS = 1e-6
ADAM_LR = 0.001
ADAM_B1 = 0.9
ADAM_B2 = 0.999
ADAM_EPS = 1e-08
ADAM_WD = 0.01
ADAM_STEP = 10

N_CHIPS = 4
N_DEV = 8
VMEM_LIMIT = 56 * 1024 * 1024
MESH = pl.DeviceIdType.MESH
ANY = pl.BlockSpec(memory_space=pl.ANY)


def _params(*sem):
    return pltpu.CompilerParams(dimension_semantics=sem, vmem_limit_bytes=VMEM_LIMIT)


def _pick(n, cands):
    for c in cands:
        if n % c == 0:
            return c
    return n


def _sigmoid(x):
    return 1.0 / (1.0 + jnp.exp(-x))


def _dot(a, b):
    return lax.dot_general(a, b, (((1,), (0,)), ((), ())), preferred_element_type=F32)


def _dot_nt(a, b):
    return lax.dot_general(a, b, (((1,), (1,)), ((), ())), preferred_element_type=F32)


def _dot_tn(a, b):
    return lax.dot_general(a, b, (((0,), (0,)), ((), ())), preferred_element_type=F32)


def _split_dot(x, m):
    hi = x.astype(BF16)
    lo = (x - hi.astype(F32)).astype(BF16)
    return _dot(hi, m) + _dot(lo, m)


def _matmul(a, b, *, mode, out_dtypes, epilogue=None, extras=(), name, after=None):
    if mode == "nn":
        (m, k), (k2, n) = a.shape, b.shape
    elif mode == "nt":
        (m, k), (n, k2) = a.shape, b.shape
    else:
        (k, m), (k2, n) = a.shape, b.shape
    assert k == k2, (a.shape, b.shape, mode)
    tm = _pick(m, (1056, 1024, 768, 512, 384, 256, 128, 96))
    tn = _pick(n, (1024, 768, 512, 256, 128))
    tk = _pick(k, (1056, 1024, 768, 512, 384, 256, 128, 96))
    nk = k // tk
    dot = {"nn": _dot, "nt": _dot_nt, "tn": _dot_tn}[mode]
    n_extra, n_out = len(extras), len(out_dtypes)
    n_after = 0 if after is None else 1
    if epilogue is None:
        epilogue = lambda acc: (acc,)

    def body(a_ref, b_ref, *rest):
        extra_refs = rest[:n_extra]
        out_refs = rest[n_extra + n_after:n_extra + n_after + n_out]
        part = dot(a_ref[...].astype(BF16), b_ref[...].astype(BF16))

        def finish(acc):
            res = epilogue(acc, *[r[...] for r in extra_refs])
            for o_ref, r in zip(out_refs, res):
                o_ref[...] = r.astype(o_ref.dtype)

        if nk == 1:
            finish(part)
        else:
            acc_ref = rest[-1]
            kk = pl.program_id(2)

            @pl.when(kk == 0)
            def _():
                acc_ref[...] = part

            @pl.when(kk > 0)
            def _():
                acc_ref[...] += part

            @pl.when(kk == nk - 1)
            def _():
                finish(acc_ref[...])

    if mode == "nn":
        a_spec = pl.BlockSpec((tm, tk), lambda i, j, kk: (i, kk))
        b_spec = pl.BlockSpec((tk, tn), lambda i, j, kk: (kk, j))
    elif mode == "nt":
        a_spec = pl.BlockSpec((tm, tk), lambda i, j, kk: (i, kk))
        b_spec = pl.BlockSpec((tn, tk), lambda i, j, kk: (j, kk))
    else:
        a_spec = pl.BlockSpec((tk, tm), lambda i, j, kk: (kk, i))
        b_spec = pl.BlockSpec((tk, tn), lambda i, j, kk: (kk, j))
    o_spec = pl.BlockSpec((tm, tn), lambda i, j, kk: (i, j))
    outs = pl.pallas_call(
        body,
        out_shape=[jax.ShapeDtypeStruct((m, n), dt) for dt in out_dtypes],
        grid=(m // tm, n // tn, nk),
        in_specs=[a_spec, b_spec] + [o_spec] * n_extra + [ANY] * n_after,
        out_specs=[o_spec] * n_out,
        scratch_shapes=[pltpu.VMEM((tm, tn), F32)] if nk > 1 else [],
        compiler_params=_params("parallel", "parallel", "arbitrary"),
        name=name,
    )(a, b, *extras, *([] if after is None else [after]))
    return outs[0] if n_out == 1 else outs


def _add_epilogue(acc, res):
    return (res + acc,)


def _rmsnorm_fwd(x, g, *, name):
    p, d = x.shape
    rows = _pick(p, (384, 128, 96))

    def body(x_ref, g_ref, o_ref):
        xv = x_ref[...]
        r = lax.rsqrt(jnp.mean(xv * xv, axis=-1, keepdims=True) + EPS)
        o_ref[...] = (xv * r * g_ref[...]).astype(o_ref.dtype)

    return pl.pallas_call(
        body,
        out_shape=jax.ShapeDtypeStruct((p, d), BF16),
        grid=(p // rows,),
        in_specs=[pl.BlockSpec((rows, d), lambda i: (i, 0)), pl.BlockSpec((1, d), lambda i: (0, 0))],
        out_specs=pl.BlockSpec((rows, d), lambda i: (i, 0)),
        compiler_params=_params("parallel"),
        name=name,
    )(x, g)


def _rmsnorm_bwd(x, g, dy, dres, *, name):
    p, d = x.shape
    rows = _pick(p, (384, 128, 96))

    def body(x_ref, g_ref, dy_ref, dres_ref, dx_ref, dg_ref):
        xv = x_ref[...]
        r = lax.rsqrt(jnp.mean(xv * xv, axis=-1, keepdims=True) + EPS)
        dyv = dy_ref[...]
        gdy = dyv * g_ref[...]
        proj = jnp.mean(xv * gdy, axis=-1, keepdims=True)
        dx_ref[...] = dres_ref[...] + r * gdy - xv * (r * r * r) * proj
        part = jnp.sum(dyv * xv * r, axis=0, keepdims=True)

        @pl.when(pl.program_id(0) == 0)
        def _():
            dg_ref[...] = part

        @pl.when(pl.program_id(0) > 0)
        def _():
            dg_ref[...] += part

    row_spec = pl.BlockSpec((rows, d), lambda i: (i, 0))
    vec_spec = pl.BlockSpec((1, d), lambda i: (0, 0))
    return pl.pallas_call(
        body,
        out_shape=[jax.ShapeDtypeStruct((p, d), F32), jax.ShapeDtypeStruct((1, d), F32)],
        grid=(p // rows,),
        in_specs=[row_spec, vec_spec, row_spec, row_spec],
        out_specs=[row_spec, vec_spec],
        compiler_params=_params("arbitrary"),
        name=name,
    )(x, g, dy, dres)


def _mlp_fwd(h, g, w1, w2, *, name):
    hn = _rmsnorm_fwd(h, g, name=name + "_norm")

    def act(acc):
        r = jnp.maximum(acc, 0.0)
        return acc, r * r

    z, a2 = _matmul(hn, w1, mode="nn", out_dtypes=(F32, BF16), epilogue=act, name=name + "_up")
    out = _matmul(a2, w2, mode="nn", out_dtypes=(F32,), epilogue=_add_epilogue, extras=(h,), name=name + "_down")
    return out, (hn, z, a2)


def _mlp_bwd(h, g, w1, w2, saved, dout, *, name, after=None):
    hn, z, a2 = saved

    def dact(acc, zt):
        return (acc * (2.0 * jnp.maximum(zt, 0.0)),)

    dz = _matmul(dout, w2, mode="nt", out_dtypes=(BF16,), epilogue=dact, extras=(z,), name=name + "_dz", after=after)
    dw2 = _matmul(a2, dout, mode="tn", out_dtypes=(F32,), name=name + "_dw2")
    dw1 = _matmul(hn, dz, mode="tn", out_dtypes=(F32,), name=name + "_dw1")
    dhn = _matmul(dz, w1, mode="nt", out_dtypes=(F32,), name=name + "_dhn")
    dh, dg = _rmsnorm_bwd(h, g, dhn, dout, name=name + "_dnorm")
    return dh, dg, dw1, dw2


def _retention_tables(p):
    half = RET_QK_DIM // 2
    inv_freq = ROPE_BASE ** (-jnp.arange(half, dtype=F32) / half)
    ang = jnp.arange(p, dtype=F32)[:, None] * inv_freq[None, :]
    cos, sin = jnp.cos(ang), jnp.sin(ang)
    cosf = jnp.concatenate([cos, cos], axis=1)
    sins = jnp.concatenate([-sin, sin], axis=1)
    log_g = jnp.log1p(-jnp.exp2(-RET_DECAY_OFFSET - jnp.arange(RET_HEADS, dtype=F32)))
    idx = jnp.arange(CHUNK, dtype=F32)
    diff = idx[:, None] - idx[None, :]
    inner = jnp.where(diff[None] >= 0, jnp.exp(jnp.maximum(diff, 0.0)[None] * log_g[:, None, None]), 0.0)
    kdec = jnp.exp((CHUNK - 1 - idx)[None, :] * log_g[:, None])
    qdec = jnp.exp((idx + 1.0)[None, :] * log_g[:, None])
    cdec = jnp.exp(CHUNK * log_g)
    kdec = jnp.broadcast_to(kdec[:, :, None], (RET_HEADS, CHUNK, RET_QK_DIM))
    qdec = jnp.broadcast_to(qdec[:, :, None], (RET_HEADS, CHUNK, RET_QK_DIM))
    cdec = jnp.broadcast_to(cdec[:, None, None], (RET_HEADS, RET_QK_DIM, RET_V_DIM))
    return cosf, sins, inner, kdec, qdec, cdec


def _rot(x, cosf, sins):
    return x * cosf + pltpu.roll(x, RET_QK_DIM // 2, 1) * sins


def _rot_bwd(dy, cosf, sins):
    return dy * cosf + pltpu.roll(dy * sins, RET_QK_DIM // 2, 1)


def _ret_in_specs(chunk_of):
    q_spec = pl.BlockSpec((CHUNK, RET_QK_W), lambda s: (chunk_of(s), 0))
    k_spec = pl.BlockSpec((CHUNK, RET_QK_W), lambda s: (chunk_of(s), 1))
    v_spec = pl.BlockSpec((CHUNK, RET_V_W), lambda s: (chunk_of(s), 1))
    g_spec = pl.BlockSpec((CHUNK, RET_V_W), lambda s: (chunk_of(s), 2))
    rope_spec = pl.BlockSpec((CHUNK, RET_QK_DIM), lambda s: (chunk_of(s), 0))
    whole = lambda *shape: pl.BlockSpec(shape, lambda s: (0,) * len(shape))
    head_sq = whole(RET_HEADS, CHUNK, CHUNK)
    head_qk = whole(RET_HEADS, CHUNK, RET_QK_DIM)
    head_st = whole(RET_HEADS, RET_QK_DIM, RET_V_DIM)
    gam_spec = whole(RET_HEADS, 1, RET_V_DIM)
    return [q_spec, k_spec, v_spec, g_spec, rope_spec, rope_spec, head_sq, head_qk, head_qk, head_st, gam_spec]


def _ret_head_views(h, qk_refs, v_refs, head_refs):
    qk = pl.ds(h * RET_QK_DIM, RET_QK_DIM)
    vv = pl.ds(h * RET_V_DIM, RET_V_DIM)
    return [r.at[:, qk] for r in qk_refs], [r.at[:, vv] for r in v_refs], [r.at[h] for r in head_refs]


def _retention_fwd(proj, gn_g, tables, *, name):
    p = proj.shape[0]
    n_chunks = p // CHUNK
    scale = RET_QK_DIM ** -0.5

    def body(q_ref, k_ref, v_ref, g_ref, cos_ref, sin_ref, inner_ref, kdec_ref, qdec_ref, cdec_ref, gam_ref,
             og_ref, opre_ref, sprev_ref, s_scr):
        @pl.when(pl.program_id(0) == 0)
        def _():
            s_scr[...] = jnp.zeros_like(s_scr)

        for h in range(RET_HEADS):
            (q_h, k_h), (v_h, g_h, og_h, opre_h), tabs = _ret_head_views(
                h, (q_ref, k_ref), (v_ref, g_ref, og_ref, opre_ref),
                (inner_ref, kdec_ref, qdec_ref, cdec_ref, gam_ref, sprev_ref, s_scr))
            one_head(q_h, k_h, v_h, g_h, cos_ref, sin_ref, *tabs[:5], og_h, opre_h, *tabs[5:])

    def one_head(q_ref, k_ref, v_ref, g_ref, cos_ref, sin_ref, inner_ref, kdec_ref, qdec_ref, cdec_ref, gam_ref,
                 og_ref, opre_ref, sprev_ref, s_scr):
        cosf, sins = cos_ref[...], sin_ref[...]
        qr = _rot(q_ref[...], cosf, sins)
        kr = _rot(k_ref[...], cosf, sins) * scale
        vb = v_ref[...].astype(BF16)
        scores = _dot_nt(qr.astype(BF16), kr.astype(BF16)) * inner_ref[...]
        state = s_scr[...]
        sprev_ref[...] = state
        o = _dot(scores.astype(BF16), vb) + _dot((qr * qdec_ref[...]).astype(BF16), state.astype(BF16))
        kd = kr * kdec_ref[...]
        s_scr[...] = cdec_ref[...] * state + _dot(kd.T.astype(BF16), vb)
        opre_ref[...] = o
        mu = jnp.mean(o, axis=-1, keepdims=True)
        oc = o - mu
        var = jnp.mean(oc * oc, axis=-1, keepdims=True)
        on = oc * lax.rsqrt(var + EPS) * gam_ref[...]
        gv = g_ref[...]
        og_ref[...] = (gv * _sigmoid(gv) * on).astype(og_ref.dtype)

    chunk_of = lambda s: s
    out_v = pl.BlockSpec((CHUNK, RET_V_W), lambda s: (s, 0))
    return pl.pallas_call(
        body,
        out_shape=[
            jax.ShapeDtypeStruct((p, RET_V_W), BF16),
            jax.ShapeDtypeStruct((p, RET_V_W), F32),
            jax.ShapeDtypeStruct((RET_HEADS, n_chunks, RET_QK_DIM, RET_V_DIM), F32),
        ],
        grid=(n_chunks,),
        in_specs=_ret_in_specs(chunk_of),
        out_specs=[out_v, out_v, pl.BlockSpec((RET_HEADS, None, RET_QK_DIM, RET_V_DIM), lambda s: (0, s, 0, 0))],
        scratch_shapes=[pltpu.VMEM((RET_HEADS, RET_QK_DIM, RET_V_DIM), F32)],
        compiler_params=_params("arbitrary"),
        name=name,
    )(proj, proj, proj, proj, *tables, gn_g.reshape(RET_HEADS, 1, RET_V_DIM))


def _retention_bwd(proj, gn_g, tables, opre, sprev, dog, *, name):
    p = proj.shape[0]
    n_chunks = p // CHUNK
    scale = RET_QK_DIM ** -0.5

    def body(q_ref, k_ref, v_ref, g_ref, cos_ref, sin_ref, inner_ref, kdec_ref, qdec_ref, cdec_ref, gam_ref,
             opre_ref, sprev_ref, dog_ref, dq_ref, dk_ref, dv_ref, dg_ref, dgam_ref, ds_scr):
        first = pl.program_id(0) == 0

        @pl.when(first)
        def _():
            ds_scr[...] = jnp.zeros_like(ds_scr)

        for h in range(RET_HEADS):
            (q_h, k_h, dq_h, dk_h), (v_h, g_h, opre_h, dog_h, dv_h, dg_h), tabs = _ret_head_views(
                h, (q_ref, k_ref, dq_ref, dk_ref), (v_ref, g_ref, opre_ref, dog_ref, dv_ref, dg_ref),
                (inner_ref, kdec_ref, qdec_ref, cdec_ref, gam_ref, sprev_ref, dgam_ref, ds_scr))
            one_head(first, q_h, k_h, v_h, g_h, cos_ref, sin_ref, *tabs[:5], opre_h, tabs[5], dog_h,
                     dq_h, dk_h, dv_h, dg_h, tabs[6], tabs[7])

    def one_head(first, q_ref, k_ref, v_ref, g_ref, cos_ref, sin_ref, inner_ref, kdec_ref, qdec_ref, cdec_ref, gam_ref,
                 opre_ref, sprev_ref, dog_ref, dq_ref, dk_ref, dv_ref, dg_ref, dgam_ref, ds_scr):
        cosf, sins = cos_ref[...], sin_ref[...]
        qr = _rot(q_ref[...], cosf, sins)
        kr = _rot(k_ref[...], cosf, sins) * scale
        qb, kb = qr.astype(BF16), kr.astype(BF16)
        vb = v_ref[...].astype(BF16)
        inner = inner_ref[...]
        qdec, kdec = qdec_ref[...], kdec_ref[...]
        state_b = sprev_ref[...].astype(BF16)
        o = opre_ref[...]
        mu = jnp.mean(o, axis=-1, keepdims=True)
        oc = o - mu
        rstd = lax.rsqrt(jnp.mean(oc * oc, axis=-1, keepdims=True) + EPS)
        xhat = oc * rstd
        gam = gam_ref[...]
        on = xhat * gam
        gv = g_ref[...]
        sig = _sigmoid(gv)
        dogv = dog_ref[...]
        dg_ref[...] = (dogv * on * sig * (1.0 + gv * (1.0 - sig))).astype(dg_ref.dtype)
        don = dogv * gv * sig
        dgam_part = jnp.sum(don * xhat, axis=0, keepdims=True)

        @pl.when(first)
        def _():
            dgam_ref[...] = dgam_part

        @pl.when(jnp.logical_not(first))
        def _():
            dgam_ref[...] += dgam_part

        dxhat = don * gam
        do = rstd * (dxhat - jnp.mean(dxhat, axis=-1, keepdims=True)
                     - xhat * jnp.mean(dxhat * xhat, axis=-1, keepdims=True))
        dob = do.astype(BF16)
        scores_b = (_dot_nt(qb, kb) * inner).astype(BF16)
        da = (_dot_nt(dob, vb) * inner).astype(BF16)
        dv = _dot(scores_b.astype(F32).T.astype(BF16), dob)
        dqr = _dot(da, kb)
        dkr = _dot(da.astype(F32).T.astype(BF16), qb)
        dqr += _dot_nt(dob, state_b) * qdec
        ds_local = _dot((qr * qdec).T.astype(BF16), dob)
        gstate = ds_scr[...]
        gb = gstate.astype(BF16)
        kd_b = (kr * kdec).astype(BF16)
        dkr += _dot_nt(vb, gb) * kdec
        dv += _dot(kd_b, gb)
        ds_scr[...] = cdec_ref[...] * gstate + ds_local
        dq_ref[...] = _rot_bwd(dqr, cosf, sins).astype(dq_ref.dtype)
        dk_ref[...] = _rot_bwd(dkr * scale, cosf, sins).astype(dk_ref.dtype)
        dv_ref[...] = dv.astype(dv_ref.dtype)

    chunk_of = lambda s: n_chunks - 1 - s
    blk_v = pl.BlockSpec((CHUNK, RET_V_W), lambda s: (chunk_of(s), 0))
    blk_qk = pl.BlockSpec((CHUNK, RET_QK_W), lambda s: (chunk_of(s), 0))
    st_spec = pl.BlockSpec((RET_HEADS, None, RET_QK_DIM, RET_V_DIM), lambda s: (0, chunk_of(s), 0, 0))
    return pl.pallas_call(
        body,
        out_shape=[
            jax.ShapeDtypeStruct((p, RET_QK_W), BF16),
            jax.ShapeDtypeStruct((p, RET_QK_W), BF16),
            jax.ShapeDtypeStruct((p, RET_V_W), BF16),
            jax.ShapeDtypeStruct((p, RET_V_W), BF16),
            jax.ShapeDtypeStruct((RET_HEADS, 1, RET_V_DIM), F32),
        ],
        grid=(n_chunks,),
        in_specs=_ret_in_specs(chunk_of) + [blk_v, st_spec, blk_v],
        out_specs=[blk_qk, blk_qk, blk_v, blk_v, pl.BlockSpec((RET_HEADS, 1, RET_V_DIM), lambda s: (0, 0, 0))],
        scratch_shapes=[pltpu.VMEM((RET_HEADS, RET_QK_DIM, RET_V_DIM), F32)],
        compiler_params=_params("arbitrary"),
        name=name,
    )(proj, proj, proj, proj, *tables, gn_g.reshape(RET_HEADS, 1, RET_V_DIM), opre, sprev, dog)


def _conv_rows(p):
    return _pick(p, (384, 128))


CONV_CHUNK = 32
F32_SUBLANES = 8


def _shifted_rows(rows):
    return rows + CONV_HALO - F32_SUBLANES


def _shifted_copies(src_scr, sh_scr, n_rows):
    for s in range(1, F32_SUBLANES):
        sh_scr[s - 1] = src_scr[s:s + n_rows, :]


def _tap_rows(src_scr, sh_scr, off, r0, n):
    q, s = divmod(off, F32_SUBLANES)
    ref = src_scr if s == 0 else sh_scr.at[s - 1]
    return ref[pl.ds(pl.multiple_of(r0 + F32_SUBLANES * q, F32_SUBLANES), n), :]


def _ln_stats(y):
    mu = jnp.mean(y, axis=-1, keepdims=True)
    yc = y - mu
    rstd = lax.rsqrt(jnp.mean(yc * yc, axis=-1, keepdims=True) + EPS)
    return yc * rstd, rstd


def _conv_fwd(proj, conv_w, conv_b, ln_g, ln_b, *, name):
    p = proj.shape[0]
    c = D_MODEL
    rows = _conv_rows(p)
    hpb = rows // CONV_HALO
    a_col, gate_col = (2 * RET_QK_W + 2 * RET_V_W) // c, (2 * RET_QK_W + 2 * RET_V_W) // c + 1

    def body(a_ref, gate_ref, ah_ref, gateh_ref, w_ref, b_ref, lg_ref, lb_ref, c_ref, y_ref, hdn_scr, sh_scr):
        i = pl.program_id(0)
        hdn_scr[0:CONV_HALO, :] = ah_ref[...] * _sigmoid(gateh_ref[...])
        hdn_scr[CONV_HALO:, :] = a_ref[...] * _sigmoid(gate_ref[...])
        _shifted_copies(hdn_scr, sh_scr, _shifted_rows(rows))

        def chunk(j, _):
            r0 = pl.multiple_of(j * CONV_CHUNK, CONV_CHUNK)
            acc = jnp.zeros((CONV_CHUNK, c), F32)
            for w in range(CONV_WIDTH):
                off = CONV_HALO - (CONV_WIDTH - 1) + w
                acc += _tap_rows(hdn_scr, sh_scr, off, r0, CONV_CHUNK) * w_ref[w:w + 1, :]
            y_ref[pl.ds(r0, CONV_CHUNK), :] = acc + b_ref[...]
            return 0

        lax.fori_loop(0, rows // CONV_CHUNK, chunk, 0)
        y = y_ref[...]
        yhat, _ = _ln_stats(y)
        ln = yhat * lg_ref[...] + lb_ref[...]
        row = i * rows + lax.broadcasted_iota(jnp.int32, (rows, 1), 0)
        c_ref[...] = jnp.where(row >= PAD_FRONT, ln * _sigmoid(ln), 0.0).astype(c_ref.dtype)

    halo_idx = lambda i: jnp.maximum(i * hpb - 1, 0)
    vec = pl.BlockSpec((1, c), lambda i: (0, 0))
    return pl.pallas_call(
        body,
        out_shape=[jax.ShapeDtypeStruct((p, c), BF16), jax.ShapeDtypeStruct((p, c), F32)],
        grid=(p // rows,),
        in_specs=[
            pl.BlockSpec((rows, c), lambda i: (i, a_col)),
            pl.BlockSpec((rows, c), lambda i: (i, gate_col)),
            pl.BlockSpec((CONV_HALO, c), lambda i: (halo_idx(i), a_col)),
            pl.BlockSpec((CONV_HALO, c), lambda i: (halo_idx(i), gate_col)),
            pl.BlockSpec((CONV_WIDTH, c), lambda i: (0, 0)),
            vec, vec, vec,
        ],
        out_specs=[pl.BlockSpec((rows, c), lambda i: (i, 0)), pl.BlockSpec((rows, c), lambda i: (i, 0))],
        scratch_shapes=[pltpu.VMEM((CONV_HALO + rows, c), F32),
                        pltpu.VMEM((F32_SUBLANES - 1, _shifted_rows(rows), c), F32)],
        compiler_params=_params("parallel"),
        name=name,
    )(proj, proj, proj, proj, conv_w, conv_b, ln_g, ln_b)


def _conv_bwd(proj, conv_w, ln_g, ln_b, y, dcat, *, name):
    p = proj.shape[0]
    c = D_MODEL
    rows = _conv_rows(p)
    hpb = rows // CONV_HALO
    n_blocks = p // rows
    a_col, gate_col = (2 * RET_QK_W + 2 * RET_V_W) // c, (2 * RET_QK_W + 2 * RET_V_W) // c + 1

    def body(a_ref, gate_ref, ah_ref, gateh_ref, w_ref, lg_ref, lb_ref, y_ref, yh_ref, dc_ref, dch_ref,
             da_ref, dgate_ref, dw_ref, db_ref, dlg_ref, dlb_ref, hdn_scr, dy_scr, hdn_sh, dy_sh):
        i = pl.program_id(0)
        lg, lb = lg_ref[...], lb_ref[...]

        def ln_bwd(yv, dcv):
            yhat, rstd = _ln_stats(yv)
            ln = yhat * lg + lb
            sig = _sigmoid(ln)
            dln = dcv * sig * (1.0 + ln * (1.0 - sig))
            dyhat = dln * lg
            dyv = rstd * (dyhat - jnp.mean(dyhat, axis=-1, keepdims=True)
                          - yhat * jnp.mean(dyhat * yhat, axis=-1, keepdims=True))
            return dyv, dln, yhat

        row = i * rows + lax.broadcasted_iota(jnp.int32, (rows, 1), 0)
        dy, dln, yhat = ln_bwd(y_ref[...], jnp.where(row >= PAD_FRONT, dc_ref[...], 0.0))
        dy_halo, _, _ = ln_bwd(yh_ref[...], dch_ref[...])
        dy_scr[0:rows, :] = dy
        dy_scr[rows:, :] = jnp.where(i == n_blocks - 1, 0.0, dy_halo)
        hdn_scr[0:CONV_HALO, :] = ah_ref[...] * _sigmoid(gateh_ref[...])
        hdn_scr[CONV_HALO:, :] = a_ref[...] * _sigmoid(gate_ref[...])
        _shifted_copies(hdn_scr, hdn_sh, _shifted_rows(rows))
        _shifted_copies(dy_scr, dy_sh, _shifted_rows(rows))

        @pl.when(i == 0)
        def _():
            dw_ref[...] = jnp.zeros_like(dw_ref)
            db_ref[...] = jnp.zeros_like(db_ref)
            dlg_ref[...] = jnp.zeros_like(dlg_ref)
            dlb_ref[...] = jnp.zeros_like(dlb_ref)

        n_chunks = rows // CONV_CHUNK

        def input_grad(j, _):
            r0 = pl.multiple_of(j * CONV_CHUNK, CONV_CHUNK)
            dhdn = jnp.zeros((CONV_CHUNK, c), F32)
            for w in range(CONV_WIDTH):
                dhdn += _tap_rows(dy_scr, dy_sh, CONV_WIDTH - 1 - w, r0, CONV_CHUNK) * w_ref[w:w + 1, :]
            here = pl.ds(r0, CONV_CHUNK)
            sig_gate = _sigmoid(gate_ref[here, :])
            da_ref[here, :] = (dhdn * sig_gate).astype(da_ref.dtype)
            dgate_ref[here, :] = (dhdn * a_ref[here, :] * sig_gate * (1.0 - sig_gate)).astype(dgate_ref.dtype)
            return 0

        lax.fori_loop(0, n_chunks, input_grad, 0)
        for w in range(CONV_WIDTH):
            off = CONV_HALO - (CONV_WIDTH - 1) + w

            def tap_grad(j, acc, off=off):
                r0 = pl.multiple_of(j * CONV_CHUNK, CONV_CHUNK)
                prod = dy_scr[pl.ds(r0, CONV_CHUNK), :] * _tap_rows(hdn_scr, hdn_sh, off, r0, CONV_CHUNK)
                for k in range(CONV_CHUNK // F32_SUBLANES):
                    acc = acc + prod[k * F32_SUBLANES:(k + 1) * F32_SUBLANES]
                return acc

            acc = lax.fori_loop(0, n_chunks, tap_grad, jnp.zeros((F32_SUBLANES, c), F32))
            dw_ref[w:w + 1, :] += jnp.sum(acc, axis=0, keepdims=True)
        db_ref[...] += jnp.sum(dy, axis=0, keepdims=True)
        dlg_ref[...] += jnp.sum(dln * yhat, axis=0, keepdims=True)
        dlb_ref[...] += jnp.sum(dln, axis=0, keepdims=True)

    prev_halo = lambda i: jnp.maximum(i * hpb - 1, 0)
    next_halo = lambda i: jnp.minimum((i + 1) * hpb, p // CONV_HALO - 1)
    vec = pl.BlockSpec((1, c), lambda i: (0, 0))
    blk = lambda col: pl.BlockSpec((rows, c), lambda i: (i, col))
    outs = pl.pallas_call(
        body,
        out_shape=[
            jax.ShapeDtypeStruct((p, c), BF16),
            jax.ShapeDtypeStruct((p, c), BF16),
            jax.ShapeDtypeStruct((CONV_WIDTH + 1, c), F32),
            jax.ShapeDtypeStruct((1, c), F32),
            jax.ShapeDtypeStruct((1, c), F32),
            jax.ShapeDtypeStruct((1, c), F32),
        ],
        grid=(n_blocks,),
        in_specs=[
            blk(a_col), blk(gate_col),
            pl.BlockSpec((CONV_HALO, c), lambda i: (prev_halo(i), a_col)),
            pl.BlockSpec((CONV_HALO, c), lambda i: (prev_halo(i), gate_col)),
            pl.BlockSpec((CONV_WIDTH, c), lambda i: (0, 0)),
            vec, vec,
            blk(0),
            pl.BlockSpec((CONV_HALO, c), lambda i: (next_halo(i), 0)),
            blk(1),
            pl.BlockSpec((CONV_HALO, c), lambda i: (next_halo(i), 1)),
        ],
        out_specs=[blk(0), blk(0), pl.BlockSpec((CONV_WIDTH + 1, c), lambda i: (0, 0)), vec, vec, vec],
        scratch_shapes=[pltpu.VMEM((CONV_HALO + rows, c), F32), pltpu.VMEM((rows + CONV_HALO, c), F32),
                        pltpu.VMEM((F32_SUBLANES - 1, _shifted_rows(rows), c), F32),
                        pltpu.VMEM((F32_SUBLANES - 1, _shifted_rows(rows), c), F32)],
        compiler_params=_params("arbitrary"),
        name=name,
    )(proj, proj, proj, proj, conv_w, ln_g, ln_b, y, y, dcat, dcat)
    da, dgate, dw, db, dlg, dlb = outs
    return da, dgate, dw[:CONV_WIDTH], db, dlg, dlb


LANES = 128


def _group_matrix():
    r = jnp.arange(LANES)[:, None] // SB_HEAD_DIM
    c = jnp.arange(LANES)[None, :] // SB_HEAD_DIM
    return (r == c).astype(BF16)


def _head_sums(v, gm):
    return jnp.concatenate([_split_dot(v[:, j * LANES:(j + 1) * LANES], gm) for j in range(v.shape[1] // LANES)], axis=1)


def _qknorm_fwd(qkv, qg, kg, *, name):
    p = qkv.shape[0]
    d = D_MODEL
    rows = _pick(p, (384, 128, 96))

    def body(q_ref, k_ref, v_ref, qg_ref, kg_ref, gm_ref, qn_ref, kn_ref, vb_ref):
        gm = gm_ref[...]

        def norm(x, g):
            ms = _head_sums(x * x, gm) * (1.0 / SB_HEAD_DIM)
            return x * lax.rsqrt(ms + EPS) * g

        qn_ref[...] = norm(q_ref[...], qg_ref[...]).astype(BF16)
        kn_ref[...] = norm(k_ref[...], kg_ref[...]).astype(BF16)
        vb_ref[...] = v_ref[...].astype(BF16)

    blk = lambda col: pl.BlockSpec((rows, d), lambda i: (i, col))
    vec = pl.BlockSpec((1, d), lambda i: (0, 0))
    return pl.pallas_call(
        body,
        out_shape=[jax.ShapeDtypeStruct((p, d), BF16)] * 3,
        grid=(p // rows,),
        in_specs=[blk(0), blk(1), blk(2), vec, vec, pl.BlockSpec((LANES, LANES), lambda i: (0, 0))],
        out_specs=[blk(0)] * 3,
        compiler_params=_params("parallel"),
        name=name,
    )(qkv, qkv, qkv, qg, kg, _group_matrix())


def _qknorm_bwd(qkv, qg, kg, dqn, dkn, dv, *, name):
    p = qkv.shape[0]
    d = D_MODEL
    rows = _pick(p, (384, 128, 96))

    def body(q_ref, k_ref, qg_ref, kg_ref, gm_ref, dqn_ref, dkn_ref, dv_ref, dqkv_ref, dqg_ref, dkg_ref):
        gm = gm_ref[...]

        def bwd(x, g, dy):
            ms = _head_sums(x * x, gm) * (1.0 / SB_HEAD_DIM)
            r = lax.rsqrt(ms + EPS)
            gdy = dy * g
            proj = _head_sums(x * gdy, gm) * (1.0 / SB_HEAD_DIM)
            return r * gdy - x * (r * r * r) * proj, jnp.sum(dy * x * r, axis=0, keepdims=True)

        dq, dqg = bwd(q_ref[...], qg_ref[...], dqn_ref[...])
        dk, dkg = bwd(k_ref[...], kg_ref[...], dkn_ref[...])
        dqkv_ref[:, 0:d] = dq.astype(BF16)
        dqkv_ref[:, d:2 * d] = dk.astype(BF16)
        dqkv_ref[:, 2 * d:3 * d] = dv_ref[...].astype(BF16)

        @pl.when(pl.program_id(0) == 0)
        def _():
            dqg_ref[...] = dqg
            dkg_ref[...] = dkg

        @pl.when(pl.program_id(0) > 0)
        def _():
            dqg_ref[...] += dqg
            dkg_ref[...] += dkg

    blk = lambda col: pl.BlockSpec((rows, d), lambda i: (i, col))
    vec = pl.BlockSpec((1, d), lambda i: (0, 0))
    return pl.pallas_call(
        body,
        out_shape=[jax.ShapeDtypeStruct((p, 3 * d), BF16), jax.ShapeDtypeStruct((1, d), F32),
                   jax.ShapeDtypeStruct((1, d), F32)],
        grid=(p // rows,),
        in_specs=[blk(0), blk(1), vec, vec, pl.BlockSpec((LANES, LANES), lambda i: (0, 0)), blk(0), blk(0), blk(0)],
        out_specs=[pl.BlockSpec((rows, 3 * d), lambda i: (i, 0)), vec, vec],
        compiler_params=_params("arbitrary"),
        name=name,
    )(qkv, qkv, qg, kg, _group_matrix(), dqn, dkn, dv)


SB_PAIR = 2 * SB_HEAD_DIM
SB_GROUP = 8
SB_PAIRS_PER_STEP = 2
SB_PAIRS_PER_STEP_BWD = 1
SB_MASKED = -1e30


def _sb_consts():
    lane = lax.broadcasted_iota(jnp.int32, (CHUNK, SB_PAIR), 1)
    r = lax.broadcasted_iota(jnp.int32, (CHUNK, CHUNK), 0)
    c = lax.broadcasted_iota(jnp.int32, (CHUNK, CHUNK), 1)
    lo = (lane < SB_HEAD_DIM).astype(F32).astype(BF16)
    ones = jnp.ones((CHUNK, CHUNK), BF16)
    twice = lambda m: jnp.concatenate([jnp.concatenate([m, ones], axis=1)] * 2, axis=0)
    later, earlier = twice((r > c).astype(BF16)), twice((r < c).astype(BF16))
    not_before = (c >= r).astype(F32) * SB_MASKED
    padding = (c < PAD_FRONT).astype(F32) * SB_MASKED
    return (lo, 1.0 - lo), c, later, earlier, not_before, padding


def _sb_halves(t, head_lanes):
    return t * head_lanes[0], t * head_lanes[1]


def _sb_logits(qh, kg, biases):
    z = _dot_nt(qh, kg)
    tiles = []
    for b, bias in enumerate(biases):
        zt = z[:, b * CHUNK:(b + 1) * CHUNK]
        if bias is not None:
            zt = zt + bias
        ls_pos = jnp.minimum(zt, 0.0) - jnp.log(1.0 + jnp.exp(-jnp.abs(zt)))
        tiles.append((ls_pos, ls_pos - zt))
    return tiles


def _sb_block_sums(tiles, m):
    st = jnp.concatenate(tiles, axis=0)
    hi = st.astype(BF16)
    lo = (st - hi.astype(F32)).astype(BF16)
    tot = _dot(jnp.concatenate([hi, lo], axis=1), m)
    return [(tot[i * CHUNK:(i + 1) * CHUNK, 0:CHUNK], tot[i * CHUNK:(i + 1) * CHUNK, CHUNK:2 * CHUNK])
            for i in range(len(tiles))]


def _sb_plan(qi, padding, not_before):
    top = lax.div(qi, SB_GROUP)
    size = qi - SB_GROUP * top + 1

    def masks(n_b):
        pad_if_first = padding * (top == 0).astype(F32)
        m = [None] * n_b
        m[n_b - 1] = not_before
        m[0] = pad_if_first if m[0] is None else m[0] + pad_if_first
        return m

    return top, size, masks


def _once_if(cond, fn, carry):
    return lax.fori_loop(0, jnp.where(cond, 1, 0), lambda s, cr: fn(cr), carry)


def _sb_head_rows(tg, lanes, n_b):
    return jnp.concatenate([tg[b * CHUNK:(b + 1) * CHUNK] * lanes for b in range(n_b)], axis=0)


def _sb_fwd(qn, kn, vb, *, name):
    p = qn.shape[0]
    n_blocks = p // CHUNK
    n_pairs = SB_HEADS // 2
    scale = SB_HEAD_DIM ** -0.5

    n_step = SB_PAIRS_PER_STEP
    n_chains = 2 * n_step
    lanes_of = lambda pair: slice(pair * SB_PAIR, (pair + 1) * SB_PAIR)

    def body(q_ref, k_ref, v_ref, o_ref, car_ref):
        head_lanes, c, later, _, not_before, padding = _sb_consts()

        def q_block(qi, _):
            rows = pl.ds(pl.multiple_of(qi * CHUNK, CHUNK), CHUNK)
            qs = []
            for pair in range(n_step):
                qh = _sb_halves(q_ref[rows, lanes_of(pair)], head_lanes)
                qs += [qh[0] * scale, qh[1] * scale]

            def blocks(kb0, biases, carry):
                n_b = len(biases)
                accs, runs, savs = list(carry[:n_step]), list(carry[n_step:n_step + n_chains]), list(carry[n_step + n_chains:])
                krows = pl.ds(pl.multiple_of(kb0 * CHUNK, CHUNK), n_b * CHUNK)
                kgs = [k_ref[krows, lanes_of(pair)] for pair in range(n_step)]
                vgs = [v_ref[krows, lanes_of(pair)] for pair in range(n_step)]
                tiles = [_sb_logits(qs[ch], kgs[ch // 2], biases) for ch in range(n_chains)]
                sums = [_sb_block_sums([log_keep for _, log_keep in tiles[ch]], later) for ch in range(n_chains)]
                cols = [(c == kb0 + b).astype(F32) for b in range(n_b)]
                for ch in range(n_chains):
                    ws = [None] * n_b
                    for b in reversed(range(n_b)):
                        after, row_sum = sums[ch][b]
                        ws[b] = jnp.exp(tiles[ch][b][0] + after + runs[ch]).astype(BF16)
                        savs[ch] = savs[ch] + cols[b] * runs[ch]
                        runs[ch] = runs[ch] + row_sum
                    accs[ch // 2] = accs[ch // 2] + _dot(jnp.concatenate(ws, axis=1),
                                                         _sb_head_rows(vgs[ch // 2], head_lanes[ch % 2], n_b))
                return (*accs, *runs, *savs)

            zt = qs[0].astype(F32) * 0.0
            top, size, masks = _sb_plan(qi, padding, not_before)
            carry = (zt,) * (n_step + 2 * n_chains)
            for n_b in range(1, SB_GROUP + 1):
                carry = _once_if(size == n_b, lambda cr, n_b=n_b: blocks(SB_GROUP * top, masks(n_b), cr), carry)
            carry = lax.fori_loop(0, jnp.maximum(top - 1, 0),
                                  lambda it, cr: blocks(SB_GROUP * (top - 1 - it), [None] * SB_GROUP, cr), carry)
            carry = _once_if(top > 0, functools.partial(blocks, 0, [padding] + [None] * (SB_GROUP - 1)), carry)
            for pair in range(n_step):
                o_ref[rows, lanes_of(pair)] = carry[pair].astype(o_ref.dtype)
            for ch in range(n_chains):
                car_ref[rows, ch * CHUNK:(ch + 1) * CHUNK] = carry[n_step + n_chains + ch]
            return 0

        lax.fori_loop(0, n_blocks, q_block, 0)

    col = pl.BlockSpec((p, n_step * SB_PAIR), lambda g: (0, g))
    return pl.pallas_call(
        body,
        out_shape=[jax.ShapeDtypeStruct((p, D_MODEL), BF16), jax.ShapeDtypeStruct((p, n_pairs * 2 * CHUNK), F32)],
        grid=(n_pairs // n_step,),
        in_specs=[col, col, col],
        out_specs=[col, pl.BlockSpec((p, n_chains * CHUNK), lambda g: (0, g))],
        compiler_params=_params("parallel"),
        name=name,
    )(qn, kn, vb)


def _sb_bwd(qn, kn, vb, carries, do, *, name):
    p = qn.shape[0]
    n_blocks = p // CHUNK
    n_pairs = SB_HEADS // 2
    scale = SB_HEAD_DIM ** -0.5

    n_step = SB_PAIRS_PER_STEP_BWD
    n_chains = 2 * n_step
    lanes_of = lambda pair: slice(pair * SB_PAIR, (pair + 1) * SB_PAIR)

    def body(q_ref, k_ref, v_ref, car_ref, do_ref, dq_ref, dk_ref, dv_ref):
        head_lanes, c, later, earlier, not_before, padding = _sb_consts()
        dk_ref[...] = jnp.zeros_like(dk_ref)
        dv_ref[...] = jnp.zeros_like(dv_ref)

        def q_block(qi, _):
            rows = pl.ds(pl.multiple_of(qi * CHUNK, CHUNK), CHUNK)
            qs, doh, do2, q2 = [], [], [], []
            for pair in range(n_step):
                qh = _sb_halves(q_ref[rows, lanes_of(pair)], head_lanes)
                qs += [qh[0] * scale, qh[1] * scale]
                doh += list(_sb_halves(do_ref[rows, lanes_of(pair)].astype(BF16), head_lanes))
                do2.append(jnp.concatenate(doh[-2:], axis=0))
                q2.append(jnp.concatenate(qs[-2:], axis=0))
            sav = [car_ref[rows, ch * CHUNK:(ch + 1) * CHUNK] for ch in range(n_chains)]

            def blocks(kb0, biases, carry):
                n_b = len(biases)
                dq_accs, pres = list(carry[:n_step]), list(carry[n_step:])
                krows = pl.ds(pl.multiple_of(kb0 * CHUNK, CHUNK), n_b * CHUNK)
                kgs = [k_ref[krows, lanes_of(pair)] for pair in range(n_step)]
                vgs = [v_ref[krows, lanes_of(pair)] for pair in range(n_step)]
                cols = [(c == kb0 + b).astype(F32) for b in range(n_b)]
                block = lambda t, b: t[:, b * CHUNK:(b + 1) * CHUNK]
                tiles = [_sb_logits(qs[ch], kgs[ch // 2], biases) for ch in range(n_chains)]
                afters = [_sb_block_sums([log_keep for _, log_keep in tiles[ch]], later) for ch in range(n_chains)]
                dws = [_dot_nt(doh[ch], vgs[ch // 2]) for ch in range(n_chains)]
                ws, es, befores = [], [], []
                for ch in range(n_chains):
                    runs = [jnp.sum(cols[b] * sav[ch], axis=-1, keepdims=True) for b in range(n_b)]
                    ws.append([jnp.exp(tiles[ch][b][0] + afters[ch][b][0] + runs[b]) for b in range(n_b)])
                    es.append([ws[ch][b] * block(dws[ch], b) for b in range(n_b)])
                    befores.append(_sb_block_sums(es[ch], earlier))
                dz2, w2 = [], []
                for ch in range(n_chains):
                    dzs = []
                    for b in range(n_b):
                        before, row_sum = befores[ch][b]
                        sig = jnp.exp(tiles[ch][b][0])
                        e = es[ch][b]
                        dzs.append((e - (e + before + pres[ch]) * sig).astype(BF16))
                        pres[ch] = pres[ch] + row_sum
                    dz2.append(jnp.concatenate(dzs, axis=1))
                    w2.append(jnp.concatenate([t.astype(BF16) for t in ws[ch]], axis=1))
                    dq_accs[ch // 2] = dq_accs[ch // 2] + _dot(dz2[ch], _sb_head_rows(kgs[ch // 2], head_lanes[ch % 2], n_b))
                for pair in range(n_step):
                    both = slice(2 * pair, 2 * pair + 2)
                    dv_ref[krows, lanes_of(pair)] += _dot_tn(jnp.concatenate(w2[both], axis=0), do2[pair])
                    dk_ref[krows, lanes_of(pair)] += _dot_tn(jnp.concatenate(dz2[both], axis=0), q2[pair])
                return (*dq_accs, *pres)

            zt = qs[0].astype(F32) * 0.0
            top, size, masks = _sb_plan(qi, padding, not_before)
            carry = _once_if(top > 0, functools.partial(blocks, 0, [padding] + [None] * (SB_GROUP - 1)),
                             (zt,) * (n_step + n_chains))
            carry = lax.fori_loop(1, top, lambda g, cr: blocks(SB_GROUP * g, [None] * SB_GROUP, cr), carry)
            for n_b in range(1, SB_GROUP + 1):
                carry = _once_if(size == n_b, lambda cr, n_b=n_b: blocks(SB_GROUP * top, masks(n_b), cr), carry)
            for pair in range(n_step):
                dq_ref[rows, lanes_of(pair)] = carry[pair] * scale
            return 0

        lax.fori_loop(0, n_blocks, q_block, 0)

    col = pl.BlockSpec((p, n_step * SB_PAIR), lambda g: (0, g))
    return pl.pallas_call(
        body,
        out_shape=[jax.ShapeDtypeStruct((p, D_MODEL), F32)] * 3,
        grid=(n_pairs // n_step,),
        in_specs=[col, col, col, pl.BlockSpec((p, n_chains * CHUNK), lambda g: (0, g)), col],
        out_specs=[col, col, col],
        compiler_params=_params("parallel"),
        name=name,
    )(qn, kn, vb, carries, do)


def _loss_head(h, target, *, name):
    p, d = h.shape
    n_blocks = p // CHUNK

    def body(h_ref, t_ref, sq_ref, dh_ref):
        i = pl.program_id(0)

        @pl.when(i == 0)
        def _():
            sq_ref[...] = jnp.zeros_like(sq_ref)
            dh_ref[...] = jnp.zeros_like(dh_ref)

        @pl.when(i > 0)
        def _():
            err = h_ref[...] - t_ref[...]
            sq_ref[...] += jnp.sum(err * err)
            dh_ref[...] = err * (1.0 / d)

    return pl.pallas_call(
        body,
        out_shape=[jax.ShapeDtypeStruct((8, 128), F32), jax.ShapeDtypeStruct((p, d), F32)],
        grid=(n_blocks,),
        in_specs=[pl.BlockSpec((CHUNK, d), lambda i: (i, 0)),
                  pl.BlockSpec((CHUNK, d), lambda i: (jnp.maximum(i - 1, 0), 0))],
        out_specs=[pl.BlockSpec((8, 128), lambda i: (0, 0)), pl.BlockSpec((CHUNK, d), lambda i: (i, 0))],
        compiler_params=_params("arbitrary"),
        name=name,
    )(h, target)


def _local_step(x, target, meta, norm_mix_g, norm_mlp_g, w_in, gn_g, conv_w, conv_b, ln_g, ln_b, qn_g, kn_g, later,
                reached=lambda point, after, grads=None: None):
    seq = x.shape[0]
    p = PAD_FRONT + N_META + seq
    d = D_MODEL
    tables = _retention_tables(p)
    row = lambda v: v.reshape(1, -1)
    h0 = jnp.concatenate([jnp.zeros((PAD_FRONT, d), F32), meta, x], axis=0)

    hn0 = _rmsnorm_fwd(h0, row(norm_mix_g[0]), name="l0_mix_norm")
    proj = _matmul(hn0, w_in, mode="nn", out_dtypes=(F32,), name="l0_proj")
    og, opre, sprev = _retention_fwd(proj, gn_g, tables, name="l0_retention")
    cb, y_conv = _conv_fwd(proj, conv_w, row(conv_b), row(ln_g), row(ln_b), name="l0_conv")
    cat = jnp.concatenate([og, cb], axis=1)
    w_out, w1_0, w2_0 = later("l0", cat)
    w1, w2 = [w1_0, None], [w2_0, None]
    h1 = _matmul(cat, w_out, mode="nn", out_dtypes=(F32,), epilogue=_add_epilogue, extras=(h0,), name="l0_mix_out")
    h2, mlp0 = _mlp_fwd(h1, row(norm_mlp_g[0]), w1[0], w2[0], name="l0_mlp")

    hn1 = _rmsnorm_fwd(h2, row(norm_mix_g[1]), name="l1_mix_norm")
    (w_qkv,) = later("qkv", hn1)
    qkv = _matmul(hn1, w_qkv, mode="nn", out_dtypes=(F32,), name="l1_qkv")
    qg_t, kg_t = jnp.tile(row(qn_g), (1, SB_HEADS)), jnp.tile(row(kn_g), (1, SB_HEADS))
    qn, kn, vb = _qknorm_fwd(qkv, qg_t, kg_t, name="l1_qknorm")
    o_sb, carries = _sb_fwd(qn, kn, vb, name="l1_stickbreak")
    w_o, w1[1], w2[1] = later("l1", o_sb)
    h3 = _matmul(o_sb, w_o, mode="nn", out_dtypes=(F32,), epilogue=_add_epilogue, extras=(h2,), name="l1_mix_out")
    h4, mlp1 = _mlp_fwd(h3, row(norm_mlp_g[1]), w1[1], w2[1], name="l1_mlp")

    sq, dh4 = _loss_head(h4, target, name="loss_head")

    dh3, dg_mlp1, dw1_1, dw2_1 = _mlp_bwd(h3, row(norm_mlp_g[1]), w1[1], w2[1], mlp1, dh4, name="l1_mlp_bwd")
    do_sb = _matmul(dh3, w_o, mode="nt", out_dtypes=(F32,), name="l1_do")
    dw_o = _matmul(o_sb, dh3, mode="tn", out_dtypes=(F32,), name="l1_dwo")
    dqn, dkn, dv = _sb_bwd(qn, kn, vb, carries, do_sb, name="l1_stickbreak_bwd")
    dqkv, dqg_t, dkg_t = _qknorm_bwd(qkv, qg_t, kg_t, dqn, dkn, dv, name="l1_qknorm_bwd")
    dw_qkv = _matmul(hn1, dqkv, mode="tn", out_dtypes=(F32,), name="l1_dwqkv")
    pin = lambda arr, tok: arr if tok is None else arr + tok[0:1, 0:1]
    tok = reached("l1_grads", dw_qkv, dict(odd_w_qkv=dw_qkv, odd_w_o=dw_o, mlp_w1_1=dw1_1, mlp_w2_1=dw2_1))
    dhn1 = _matmul(dqkv, w_qkv, mode="nt", out_dtypes=(F32,), name="l1_dhn", after=tok)
    dh2, dg_mix1 = _rmsnorm_bwd(h2, row(norm_mix_g[1]), dhn1, dh3, name="l1_mix_dnorm")
    tok = reached("l1_done", dh2)

    dh1, dg_mlp0, dw1_0, dw2_0 = _mlp_bwd(h1, row(norm_mlp_g[0]), w1[0], w2[0], mlp0, dh2, name="l0_mlp_bwd", after=tok)
    tok = reached("l0_mlp_grads", dh1, dict(mlp_w1_0=dw1_0, mlp_w2_0=dw2_0))
    dcat = _matmul(dh1, w_out, mode="nt", out_dtypes=(F32,), name="l0_dcat", after=tok)
    dw_out = _matmul(cat, dh1, mode="tn", out_dtypes=(F32,), name="l0_dwout")
    tok = reached("l0_dwout", dw_out, dict(even_w_out=dw_out))
    dq, dk, dvr, dgate_r, dgn = _retention_bwd(proj, pin(gn_g, tok), tables, opre, sprev, dcat, name="l0_retention_bwd")
    tok = reached("l0_retention_bwd", dq)
    da, dgate_c, dconv_w, dconv_b, dln_g, dln_b = _conv_bwd(proj, conv_w, pin(row(ln_g), tok), row(ln_b), y_conv, dcat,
                                                            name="l0_conv_bwd")
    tok = reached("l0_conv_bwd", da)
    dproj = jnp.concatenate([dq, dk, dvr, dgate_r, da, dgate_c], axis=1)
    dw_in = _matmul(hn0, dproj, mode="tn", out_dtypes=(F32,), name="l0_dwin", after=tok)
    tok = reached("l0_dwin", dw_in, dict(even_w_in=dw_in))
    dhn0 = _matmul(dproj, w_in, mode="nt", out_dtypes=(F32,), name="l0_dhn", after=tok)
    tok = reached("l0_dhn", dhn0)
    dh0, dg_mix0 = _rmsnorm_bwd(h0, pin(row(norm_mix_g[0]), tok), dhn0, dh1, name="l0_mix_dnorm")

    fold = lambda t: t.reshape(SB_HEADS, SB_HEAD_DIM).sum(axis=0)
    grads = dict(
        x=dh0[PAD_FRONT + N_META:],
        meta=dh0[PAD_FRONT:PAD_FRONT + N_META],
        norm_mix_g=jnp.concatenate([dg_mix0, dg_mix1], axis=0),
        norm_mlp_g=jnp.concatenate([dg_mlp0, dg_mlp1], axis=0),
        even_w_in=dw_in,
        even_ret_gn_g=dgn.reshape(RET_HEADS, RET_V_DIM),
        even_conv_w=dconv_w,
        even_conv_b=dconv_b,
        even_conv_ln_g=dln_g,
        even_conv_ln_b=dln_b,
        even_w_out=dw_out,
        odd_w_qkv=dw_qkv,
        odd_q_norm_g=fold(dqg_t)[None],
        odd_k_norm_g=fold(dkg_t)[None],
        odd_w_o=dw_o,
        mlp_w1=(dw1_0, dw1_1),
        mlp_w2=(dw2_0, dw2_1),
    )
    return sq[0, 0], grads


def _position():
    x, y, c = lax.axis_index("x"), lax.axis_index("y"), lax.axis_index("c")
    other_chips = [(1 - x, y), (x, 1 - y), (1 - x, 1 - y)]
    return x, y, c, other_chips


def _shard_of(ref, kind, s, n):
    rows, cols = ref.shape
    if kind == "col":
        return ref.at[:, pl.ds(s * (cols // n), cols // n)]
    return ref.at[pl.ds(s * (rows // n), rows // n), :]


def _half_of(ref, kind, c):
    rows, cols = ref.shape
    if kind == "col":
        return ref.at[pl.ds(c * (rows // 2), rows // 2), :]
    return ref.at[:, pl.ds(c * (cols // 2), cols // 2)]


def _remote(src, dst, send_sems, recv_sems, idx, device):
    return pltpu.make_async_remote_copy(src_ref=src, dst_ref=dst, send_sem=send_sems.at[idx], recv_sem=recv_sems.at[idx],
                                        device_id=device, device_id_type=MESH)


def _cast_into_whole(w, kind, s_arr, *, name):
    rows, cols = w.shape
    tr = _pick(rows, (256, 128))
    nb = rows // tr
    if kind == "col":
        whole, o_spec = (rows, cols * N_CHIPS), pl.BlockSpec((tr, cols), lambda i, s_ref: (i, s_ref[0]))
    else:
        whole, o_spec = (rows * N_CHIPS, cols), pl.BlockSpec((tr, cols), lambda i, s_ref: (s_ref[0] * nb + i, 0))

    def body(s_ref, w_ref, o_ref):
        o_ref[...] = w_ref[...].astype(BF16)

    return pl.pallas_call(
        body,
        out_shape=jax.ShapeDtypeStruct(whole, BF16),
        grid_spec=pltpu.PrefetchScalarGridSpec(num_scalar_prefetch=1, grid=(nb,),
                                               in_specs=[pl.BlockSpec((tr, cols), lambda i, s_ref: (i, 0))],
                                               out_specs=o_spec),
        compiler_params=_params("parallel"),
        name=name,
    )(s_arr, w)


def _allgather_weights(wholes, kinds):
    n = len(wholes)

    def body(*refs):
        ins, outs = refs[:n], refs[n:2 * n]
        send_sems, recv_sems = refs[2 * n:]
        x, y, c, chips = _position()
        me_chip = 2 * x + y
        sibling = (x, y, 1 - c)
        sends = []
        for t in range(n):
            for k, (cx, cy) in enumerate(chips):
                src = _half_of(_shard_of(ins[t], kinds[t], me_chip, N_CHIPS), kinds[t], c)
                dst = _half_of(_shard_of(outs[t], kinds[t], me_chip, N_CHIPS), kinds[t], c)
                sends.append(_remote(src, dst, send_sems, recv_sems, 6 * t + k, (cx, cy, c)))
        for cp in sends:
            cp.start()
        passed = []
        for t in range(n):
            for k, (cx, cy) in enumerate(chips):
                landed = _half_of(_shard_of(outs[t], kinds[t], 2 * cx + cy, N_CHIPS), kinds[t], c)
                _remote(landed, landed, send_sems, recv_sems, 6 * t + k, (cx, cy, c)).wait_recv()
                fwd = _remote(landed, landed, send_sems, recv_sems, 6 * t + 3 + k, sibling)
                fwd.start()
                passed.append(fwd)
        for t in range(n):
            for k, (cx, cy) in enumerate(chips):
                theirs = _half_of(_shard_of(outs[t], kinds[t], 2 * cx + cy, N_CHIPS), kinds[t], 1 - c)
                _remote(theirs, theirs, send_sems, recv_sems, 6 * t + 3 + k, sibling).wait_recv()
        for cp in sends + passed:
            cp.wait_send()

    return pl.pallas_call(
        body,
        out_shape=[jax.ShapeDtypeStruct(w.shape, BF16) for w in wholes],
        in_specs=[ANY] * n,
        out_specs=[ANY] * n,
        input_output_aliases={t: t for t in range(n)},
        scratch_shapes=[pltpu.SemaphoreType.DMA((6 * n,)), pltpu.SemaphoreType.DMA((6 * n,))],
        name="allgather_weights",
    )(*wholes)


HBM = pl.BlockSpec(memory_space=pltpu.HBM)
SEM = pl.BlockSpec(memory_space=pltpu.SEMAPHORE)
DATAFLOW = pltpu.SideEffectType.DATAFLOW_SIDE_EFFECTING
TARGETS = 6


def _gather_copies(kinds, refs, _, send_sems, recv_sems):
    x, y, c, chips = _position()
    me_chip = 2 * x + y
    sends, lands = [], []
    for t, (ref, kind) in enumerate(zip(refs, kinds)):
        mine = _half_of(_shard_of(ref, kind, me_chip, N_CHIPS), kind, c)
        for k, (cx, cy) in enumerate(chips):
            for other_core in range(2):
                j = TARGETS * t + 2 * k + other_core
                peer_c = 1 - c if other_core else c
                sends.append(_remote(mine, mine, send_sems, recv_sems, j, (cx, cy, peer_c)))
                theirs = _half_of(_shard_of(ref, kind, 2 * cx + cy, N_CHIPS), kind, peer_c)
                lands.append(_remote(theirs, theirs, send_sems, recv_sems, j, (cx, cy, peer_c)))
    return sends, lands


def _pair_swap_copies(kinds, srcs, lands, send_sems, recv_sems):
    x, y, c, _ = _position()
    sibling = (x, y, 1 - c)
    sends = [_remote(_half_of(srcs[t], kinds[t], 1 - c), lands[t], send_sems, recv_sems, t, sibling) for t in range(len(srcs))]
    arrivals = [_remote(_half_of(srcs[t], kinds[t], c), lands[t], send_sems, recv_sems, t, sibling) for t in range(len(srcs))]
    return sends, arrivals


def _chip_exchange_copies(kinds, srcs, lands, send_sems, recv_sems):
    x, y, c, chips = _position()
    sends, arrivals = [], []
    for t in range(len(srcs)):
        for k, (cx, cy) in enumerate(chips):
            src = _shard_of(srcs[t], kinds[t], 2 * cx + cy, N_CHIPS)
            sends.append(_remote(src, lands[t].at[k], send_sems, recv_sems, 3 * t + k, (cx, cy, c)))
            arrivals.append(_remote(src, lands[t].at[k], send_sems, recv_sems, 3 * t + k, (cx, cy, c)))
    return sends, arrivals


def _pair_gather_copies(kinds, srcs, lands, send_sems, recv_sems):
    x, y, c, _ = _position()
    sibling = (x, y, 1 - c)
    sends, arrivals = [], []
    for t in range(len(srcs)):
        mine, theirs = _half_of(srcs[t], kinds[t], c), _half_of(srcs[t], kinds[t], 1 - c)
        sends.append(_remote(mine, mine, send_sems, recv_sems, t, sibling))
        arrivals.append(_remote(theirs, theirs, send_sems, recv_sems, t, sibling))
    return sends, arrivals


def _copies_start(plan, n_sems, srcs, lands, follows, *, name):
    ns, n = len(srcs), len(srcs) + len(lands)

    def body(*refs):
        send_sems, recv_sems = refs[n + 1], refs[n + 2]
        thru, token = refs[n + 3:2 * n + 3], refs[2 * n + 3]
        sends, _ = plan(thru[:ns], thru[ns:], send_sems, recv_sems)
        for cp in sends:
            cp.start()
        token[...] = jnp.zeros_like(token)

    arrays = [pltpu.with_memory_space_constraint(a, pltpu.HBM) for a in list(srcs) + list(lands)]
    outs = pl.pallas_call(
        body,
        name=name,
        out_shape=(pltpu.SemaphoreType.DMA((n_sems,)), pltpu.SemaphoreType.DMA((n_sems,)),
                   *[pltpu.HBM(a.shape, a.dtype) for a in arrays], jax.ShapeDtypeStruct((8, 128), F32)),
        in_specs=(*[HBM] * n, ANY),
        out_specs=(SEM, SEM, *[HBM] * n, pl.BlockSpec(memory_space=pltpu.VMEM)),
        input_output_aliases={t: 2 + t for t in range(n)},
        compiler_params=pltpu.CompilerParams(has_side_effects=DATAFLOW),
    )(*arrays, follows)
    return outs[0], outs[1], list(outs[2:2 + ns]), list(outs[2 + ns:2 + n]), outs[2 + n]


def _copies_wait(plan, started, follows, *, name):
    send_sems, recv_sems, srcs, lands, _ = started
    ns, n = len(srcs), len(srcs) + len(lands)

    def body(*refs):
        ins, s_sems, r_sems = refs[:n], refs[n], refs[n + 1]
        sends, arrivals = plan(ins[:ns], ins[ns:], s_sems, r_sems)
        for cp in sends:
            cp.wait_send()
        for cp in arrivals:
            cp.wait_recv()

    outs = pl.pallas_call(
        body,
        name=name,
        out_shape=tuple(pltpu.HBM(a.shape, a.dtype) for a in srcs + lands),
        in_specs=(*[HBM] * n, SEM, SEM, ANY),
        out_specs=tuple([HBM] * n),
        input_output_aliases={t: t for t in range(n)},
        compiler_params=pltpu.CompilerParams(has_side_effects=DATAFLOW),
    )(*srcs, *lands, send_sems, recv_sems, follows)
    return list(outs[:ns]), list(outs[ns:])


def _allgather8(block, *, name):
    rows, cols = block.shape

    def body(in_ref, out_ref, send_sems, recv_sems, local_sem):
        x, y, c, _ = _position()
        me = 4 * x + 2 * y + c
        mine = pltpu.make_async_copy(in_ref, out_ref.at[me], local_sem)
        mine.start()
        peers = []
        for flip in range(1, N_DEV):
            fx, fy, fc = (flip >> 2) & 1, (flip >> 1) & 1, flip & 1
            peers.append(((1 - x if fx else x), (1 - y if fy else y), (1 - c if fc else c)))
        sends = [_remote(in_ref, out_ref.at[me], send_sems, recv_sems, j, peer) for j, peer in enumerate(peers)]
        for cp in sends:
            cp.start()
        for j, (px, py, pc) in enumerate(peers):
            slot = out_ref.at[4 * px + 2 * py + pc]
            _remote(slot, slot, send_sems, recv_sems, j, (px, py, pc)).wait_recv()
        for cp in sends:
            cp.wait_send()
        mine.wait()

    vmem = pl.BlockSpec(memory_space=pltpu.VMEM)
    return pl.pallas_call(
        body,
        out_shape=jax.ShapeDtypeStruct((N_DEV, rows, cols), F32),
        in_specs=[vmem],
        out_specs=vmem,
        scratch_shapes=[pltpu.SemaphoreType.DMA((N_DEV - 1,)), pltpu.SemaphoreType.DMA((N_DEV - 1,)),
                        pltpu.SemaphoreType.DMA],
        name=name,
    )(block)


def _sum8(stack, *, name):
    _, rows, cols = stack.shape

    def body(s_ref, o_ref):
        acc = s_ref[0]
        for i in range(1, N_DEV):
            acc = acc + s_ref[i]
        o_ref[...] = acc

    return pl.pallas_call(body, out_shape=jax.ShapeDtypeStruct((rows, cols), F32), name=name)(stack)


def _half_add(grad, theirs, kind, c_arr, *, name):
    rows, cols = theirs.shape
    tr = _pick(rows, (256, 128))
    nb = rows // tr
    if kind == "col":
        g_spec = pl.BlockSpec((tr, cols), lambda i, c_ref: (c_ref[0] * nb + i, 0))
    else:
        g_spec = pl.BlockSpec((tr, cols), lambda i, c_ref: (i, c_ref[0]))
    t_spec = pl.BlockSpec((tr, cols), lambda i, c_ref: (i, 0))

    def body(c_ref, g_ref, t_ref, o32_ref, o16_ref):
        tot = g_ref[...] + t_ref[...]
        o32_ref[...] = tot
        o16_ref[...] = tot.astype(BF16)

    return pl.pallas_call(
        body,
        out_shape=[jax.ShapeDtypeStruct((rows, cols), F32), jax.ShapeDtypeStruct((rows, cols), BF16)],
        grid_spec=pltpu.PrefetchScalarGridSpec(num_scalar_prefetch=1, grid=(nb,), in_specs=[g_spec, t_spec],
                                               out_specs=[t_spec, t_spec]),
        compiler_params=_params("parallel"),
        name=name,
    )(c_arr, grad, theirs)


def _shard_sum(part32, recv, kind, sc_arr, *, name):
    _, rows, cols = recv.shape
    tr = _pick(rows, (256, 128))
    nb = rows // tr
    if kind == "col":
        whole = (2 * rows, cols)
        p_spec = pl.BlockSpec((tr, cols), lambda i, sc: (i, sc[0]))
        o_spec = pl.BlockSpec((tr, cols), lambda i, sc: (sc[1] * nb + i, 0))
    else:
        whole = (rows, 2 * cols)
        p_spec = pl.BlockSpec((tr, cols), lambda i, sc: (sc[0] * nb + i, 0))
        o_spec = pl.BlockSpec((tr, cols), lambda i, sc: (i, sc[1]))
    r_spec = pl.BlockSpec((3, tr, cols), lambda i, sc: (0, i, 0))

    def body(sc_ref, p_ref, r_ref, o_ref):
        acc = p_ref[...]
        for k in range(3):
            acc = acc + r_ref[k].astype(F32)
        o_ref[...] = acc

    return pl.pallas_call(
        body,
        out_shape=jax.ShapeDtypeStruct(whole, F32),
        grid_spec=pltpu.PrefetchScalarGridSpec(num_scalar_prefetch=1, grid=(nb,), in_specs=[p_spec, r_spec],
                                               out_specs=o_spec),
        compiler_params=_params("parallel"),
        name=name,
    )(sc_arr, part32, recv)


def _adamw(w, g, m, v, *, name):
    rows, cols = w.shape
    tr = _pick(rows, (256, 128)) if rows * cols > 64 * 1024 else rows

    def body(w_ref, g_ref, m_ref, v_ref, d_ref, nm_ref, nv_ref):
        gv = g_ref[...]
        nm = ADAM_B1 * m_ref[...] + (1.0 - ADAM_B1) * gv
        nv = ADAM_B2 * v_ref[...] + (1.0 - ADAM_B2) * jnp.square(gv)
        m_hat = nm / (1.0 - ADAM_B1 ** ADAM_STEP)
        v_hat = nv / (1.0 - ADAM_B2 ** ADAM_STEP)
        d_ref[...] = -ADAM_LR * (m_hat / (jnp.sqrt(v_hat) + ADAM_EPS) + ADAM_WD * w_ref[...])
        nm_ref[...] = nm
        nv_ref[...] = nv

    spec = pl.BlockSpec((tr, cols), lambda i: (i, 0))
    return pl.pallas_call(
        body,
        out_shape=[jax.ShapeDtypeStruct((rows, cols), F32)] * 3,
        grid=(rows // tr,),
        in_specs=[spec] * 4,
        out_specs=[spec] * 3,
        compiler_params=_params("parallel"),
        name=name,
    )(w, g, m, v)


BIG = ("even_w_in", "odd_w_qkv", "mlp_w1_0", "mlp_w1_1", "even_w_out", "odd_w_o", "mlp_w2_0", "mlp_w2_1")
BIG_KIND = ("col", "col", "col", "col", "row", "row", "row", "row")


class _TravellingReduction:
    def __init__(self, tag, names, kinds, c_arr, sc_arr):
        self.tag, self.names, self.kinds, self.c_arr, self.sc_arr = tag, names, kinds, c_arr, sc_arr
        self.swap = functools.partial(_pair_swap_copies, kinds)
        self.exchange = functools.partial(_chip_exchange_copies, kinds)
        self.gather = functools.partial(_pair_gather_copies, kinds)

    def pair_swap_start(self, grads, follows):
        half = lambda g, kind: (g.shape[0] // 2, g.shape[1]) if kind == "col" else (g.shape[0], g.shape[1] // 2)
        lands = [lax.empty(half(g, k), F32) for g, k in zip(grads, self.kinds)]
        self.started = _copies_start(self.swap, len(grads), grads, lands, follows, name=f"reduce_{self.tag}_pair_start")

    def pair_swap_finish(self, after):
        grads, theirs = _copies_wait(self.swap, self.started, after, name=f"reduce_{self.tag}_pair_wait")
        self.sums = [_half_add(g, th, k, self.c_arr, name="pair_sum_" + n)
                     for g, th, k, n in zip(grads, theirs, self.kinds, self.names)]

    def chips_start(self, follows):
        parts = [s16 for _, s16 in self.sums]
        piece = lambda p, kind: (3, p.shape[0], p.shape[1] // N_CHIPS) if kind == "col" else (3, p.shape[0] // N_CHIPS, p.shape[1])
        lands = [lax.empty(piece(p, k), BF16) for p, k in zip(parts, self.kinds)]
        self.started = _copies_start(self.exchange, 3 * len(parts), parts, lands, follows,
                                     name=f"reduce_{self.tag}_chips_start")

    def chips_finish(self, after):
        _, recv = _copies_wait(self.exchange, self.started, after, name=f"reduce_{self.tag}_chips_wait")
        self.halves = [_shard_sum(s32, r, k, self.sc_arr, name="chip_sum_" + n)
                       for (s32, _), r, k, n in zip(self.sums, recv, self.kinds, self.names)]

    def pair_gather_start(self, follows):
        self.started = _copies_start(self.gather, len(self.halves), self.halves, [], follows,
                                     name=f"reduce_{self.tag}_gather_start")

    def pair_gather_finish(self, after):
        shards, _ = _copies_wait(self.gather, self.started, after, name=f"reduce_{self.tag}_gather_wait")
        return dict(zip(self.names, shards))
SUBLANES = 8


def _pack_rows(parts, width):
    padded, offsets, r0 = [], [], 0
    for t in parts:
        rows = -(-t.shape[0] // SUBLANES) * SUBLANES
        padded.append(jnp.pad(t, ((0, rows - t.shape[0]), (0, width - t.shape[1]))))
        offsets.append(r0)
        r0 += rows
    return jnp.concatenate(padded, axis=0), offsets


def kernel(x, meta, norm_mix_g, norm_mlp_g, even_w_in, even_ret_gn_g, even_conv_w, even_conv_b, even_conv_ln_g, even_conv_ln_b, even_w_out, odd_w_qkv, odd_q_norm_g, odd_k_norm_g, odd_w_o, mlp_w1, mlp_w2, loss_target, m_meta, m_norm_mix_g, m_norm_mlp_g, m_even_w_in, m_even_ret_gn_g, m_even_conv_w, m_even_conv_b, m_even_conv_ln_g, m_even_conv_ln_b, m_even_w_out, m_odd_w_qkv, m_odd_q_norm_g, m_odd_k_norm_g, m_odd_w_o, m_mlp_w1, m_mlp_w2, v_meta, v_norm_mix_g, v_norm_mlp_g, v_even_w_in, v_even_ret_gn_g, v_even_conv_w, v_even_conv_b, v_even_conv_ln_g, v_even_conv_ln_b, v_even_w_out, v_odd_w_qkv, v_odd_q_norm_g, v_odd_k_norm_g, v_odd_w_o, v_mlp_w1, v_mlp_w2):
    d = D_MODEL
    xi, yi, ci = lax.axis_index("x"), lax.axis_index("y"), lax.axis_index("c")
    chip = 2 * xi + yi
    c_arr = jnp.reshape(ci, (1,)).astype(jnp.int32)
    s_arr = jnp.reshape(chip, (1,)).astype(jnp.int32)

    def split_big(w_in, w_qkv, w1, w_out, w_o, w2):
        return dict(zip(BIG, (w_in[0], w_qkv[0], w1[0], w1[1], w_out[0], w_o[0], w2[0], w2[1])))

    w_big = split_big(even_w_in, odd_w_qkv, mlp_w1, even_w_out, odd_w_o, mlp_w2)
    m_big = split_big(m_even_w_in, m_odd_w_qkv, m_mlp_w1, m_even_w_out, m_odd_w_o, m_mlp_w2)
    v_big = split_big(v_even_w_in, v_odd_w_qkv, v_mlp_w1, v_even_w_out, v_odd_w_o, v_mlp_w2)

    placed = {n: _cast_into_whole(w_big[n], k, s_arr, name="cast_" + n) for n, k in zip(BIG, BIG_KIND)}
    kind_of = dict(zip(BIG, BIG_KIND))
    (w_in_full,) = _allgather_weights([placed["even_w_in"]], [kind_of["even_w_in"]])
    packed, (r_meta, r_conv, r_gn) = _pack_rows([meta, even_conv_w[0], even_ret_gn_g[0]], d // N_CHIPS)
    gathered = _allgather8(packed, name="allgather_small_params")[0::2]
    groups = dict(l0=("even_w_out", "mlp_w1_0", "mlp_w2_0"), qkv=("odd_w_qkv",), l1=("odd_w_o", "mlp_w1_1", "mlp_w2_1"))
    in_flight, follows = {}, gathered[0, 0:1, 0:1] + w_in_full[0:1, 0:1].astype(F32)
    for group, names in groups.items():
        plan = functools.partial(_gather_copies, [kind_of[n] for n in names])
        in_flight[group] = (plan, _copies_start(plan, TARGETS * len(names), [placed[n] for n in names], [], follows,
                                                name="gather_" + group + "_start"))
        follows = in_flight[group][1][-1]
    started = follows[0:1, 0:1]

    def later(group, after):
        plan, state = in_flight[group]
        return _copies_wait(plan, state, after, name="gather_" + group + "_wait")[0]

    sc_arr = jnp.concatenate([s_arr, c_arr])
    early = ("odd_w_qkv", "odd_w_o", "mlp_w1_1", "mlp_w2_1"), ("mlp_w1_0", "mlp_w2_0"), ("even_w_out",), ("even_w_in",)
    red_l1, red_m0, red_o0, red_i0 = (_TravellingReduction(tag, names, [kind_of[n] for n in names], c_arr, sc_arr)
                                      for tag, names in zip(("l1", "m0", "o0", "i0"), early))
    grad_big = {}

    def reached(point, after, grads=None):
        if point == "l1_grads":
            red_l1.pair_swap_start([grads[n] for n in red_l1.names], after)
            return red_l1.started[-1]
        if point == "l1_done":
            red_l1.pair_swap_finish(after)
            red_l1.chips_start(after)
            return red_l1.started[-1]
        if point == "l0_mlp_grads":
            red_m0.pair_swap_start([grads[n] for n in red_m0.names], after)
            return red_m0.started[-1]
        if point == "l0_dwout":
            red_m0.pair_swap_finish(after)
            red_m0.chips_start(after)
            red_o0.pair_swap_start([grads[n] for n in red_o0.names], red_m0.started[-1])
            return red_o0.started[-1]
        if point == "l0_retention_bwd":
            red_l1.chips_finish(after)
            red_l1.pair_gather_start(after)
            red_o0.pair_swap_finish(after)
            red_o0.chips_start(red_l1.started[-1])
            return red_o0.started[-1]
        if point == "l0_conv_bwd":
            red_m0.chips_finish(after)
            red_m0.pair_gather_start(after)
            grad_big.update(red_l1.pair_gather_finish(after))
            red_o0.chips_finish(after)
            red_o0.pair_gather_start(red_m0.started[-1])
            return red_o0.started[-1]
        if point == "l0_dwin":
            grad_big.update(red_m0.pair_gather_finish(after))
            grad_big.update(red_o0.pair_gather_finish(after))
            red_i0.pair_swap_start([grads[n] for n in red_i0.names], after)
            return red_i0.started[-1]
        if point == "l0_dhn":
            red_i0.pair_swap_finish(after)
            red_i0.chips_start(after)
            return red_i0.started[-1]
        return None

    across = lambda r0, rows, width: jnp.concatenate([gathered[s, r0:r0 + rows, 0:width] for s in range(N_CHIPS)], axis=1)
    meta_full = across(r_meta, N_META, d // N_CHIPS) + started
    conv_w_full = across(r_conv, CONV_WIDTH, d // N_CHIPS)
    gn_full = across(r_gn, RET_HEADS, RET_V_DIM // N_CHIPS)

    sq, g = _local_step(
        x[0], loss_target[0], meta_full, norm_mix_g, norm_mlp_g, w_in_full, gn_full, conv_w_full,
        even_conv_b[0], even_conv_ln_g[0], even_conv_ln_b[0], odd_q_norm_g[0], odd_k_norm_g[0], later, reached)
    red_i0.chips_finish(g["x"])
    red_i0.pair_gather_start(g["x"])
    grad_big.update(red_i0.pair_gather_finish(g["x"]))
    loss = lax.psum(0.5 * sq / d, ("x", "y", "c"))

    small_names = ("norm_mix_g", "norm_mlp_g", "even_conv_b", "even_conv_ln_g", "even_conv_ln_b", "odd_q_norm_g",
                   "odd_k_norm_g", "meta", "even_conv_w", "even_ret_gn_g")
    pack, offsets = _pack_rows([g[n] for n in small_names], d)
    summed = _sum8(_allgather8(pack, name="allgather_small_grads"), name="sum_small_grads")
    small = {n: summed[r0:r0 + g[n].shape[0], 0:g[n].shape[1]] for n, r0 in zip(small_names, offsets)}
    for n in ("meta", "even_conv_w", "even_ret_gn_g"):
        width = small[n].shape[1] // N_CHIPS
        small[n] = lax.dynamic_slice_in_dim(small[n], chip * width, width, axis=1)

    stacked = dict(mlp_w1=(mlp_w1, m_mlp_w1, v_mlp_w1), mlp_w2=(mlp_w2, m_mlp_w2, v_mlp_w2))
    flat = lambda t: t.reshape(-1, t.shape[-1])
    for n in stacked:
        grad_big[n] = jnp.concatenate([grad_big.pop(n + "_0"), grad_big.pop(n + "_1")], axis=0)
    upd = {n: _adamw(w_big[n], grad_big[n], m_big[n], v_big[n], name="adamw_" + n) for n in BIG if n in grad_big}
    upd.update({n: _adamw(flat(w), grad_big[n], flat(m), flat(v), name="adamw_" + n) for n, (w, m, v) in stacked.items()})

    def join(name, idx, lead):
        t = upd[name][idx] if idx >= 0 else grad_big[name]
        if name in stacked:
            return t.reshape(stacked[name][0].shape)
        return t[None] if lead else t

    small_w = dict(meta=meta, norm_mix_g=norm_mix_g, norm_mlp_g=norm_mlp_g, even_ret_gn_g=even_ret_gn_g[0],
                   even_conv_w=even_conv_w[0], even_conv_b=even_conv_b, even_conv_ln_g=even_conv_ln_g,
                   even_conv_ln_b=even_conv_ln_b, odd_q_norm_g=odd_q_norm_g, odd_k_norm_g=odd_k_norm_g)
    small_m = dict(meta=m_meta, norm_mix_g=m_norm_mix_g, norm_mlp_g=m_norm_mlp_g, even_ret_gn_g=m_even_ret_gn_g[0],
                   even_conv_w=m_even_conv_w[0], even_conv_b=m_even_conv_b, even_conv_ln_g=m_even_conv_ln_g,
                   even_conv_ln_b=m_even_conv_ln_b, odd_q_norm_g=m_odd_q_norm_g, odd_k_norm_g=m_odd_k_norm_g)
    small_v = dict(meta=v_meta, norm_mix_g=v_norm_mix_g, norm_mlp_g=v_norm_mlp_g, even_ret_gn_g=v_even_ret_gn_g[0],
                   even_conv_w=v_even_conv_w[0], even_conv_b=v_even_conv_b, even_conv_ln_g=v_even_conv_ln_g,
                   even_conv_ln_b=v_even_conv_ln_b, odd_q_norm_g=v_odd_q_norm_g, odd_k_norm_g=v_odd_k_norm_g)
    small_upd = {n: _adamw(small_w[n], small[n], small_m[n], small_v[n], name="adamw_" + n) for n in small_w}
    leading = ("even_ret_gn_g", "even_conv_w")

    order = ("meta", "norm_mix_g", "norm_mlp_g", "even_w_in", "even_ret_gn_g", "even_conv_w", "even_conv_b",
             "even_conv_ln_g", "even_conv_ln_b", "even_w_out", "odd_w_qkv", "odd_q_norm_g", "odd_k_norm_g", "odd_w_o",
             "mlp_w1", "mlp_w2")
    big_lead = ("even_w_in", "even_w_out", "odd_w_qkv", "odd_w_o")

    def leaf(name, idx):
        if name in small_w:
            t = small_upd[name][idx] if idx >= 0 else small[name]
            return t[None] if name in leading else t
        return join(name, idx, name in big_lead)

    outs = [loss, g["x"][None]]
    for idx in (-1, 0, 1, 2):
        outs += [leaf(n, idx) for n in order]
    return tuple(outs)
```

```python
import functools

import jax
import jax.numpy as jnp
from jax import lax
from jax.experimental import pallas as pl
from jax.experimental.pallas import tpu as pltpu

F32 = jnp.float32
BF16 = jnp.bfloat16

D_MODEL = 1024
N_META = 16
CHUNK = 128
PAD_FRONT = (-N_META) % CHUNK
RET_HEADS = 4
RET_QK_DIM = 128
RET_V_DIM = 256
RET_QK_W = RET_HEADS * RET_QK_DIM
RET_V_W = RET_HEADS * RET_V_DIM
CONV_WIDTH = 31
CONV_HALO = 32
RET_DECAY_OFFSET = 5.0
ROPE_BASE = 10000.0
SB_HEADS = 16
SB_HEAD_DIM = 64
D_FF = 4 * D_MODEL
EPS = 1e-6
ADAM_LR = 0.001
ADAM_B1 = 0.9
ADAM_B2 = 0.999
ADAM_EPS = 1e-08
ADAM_WD = 0.01
ADAM_STEP = 10

N_CHIPS = 4
N_DEV = 8
VMEM_LIMIT = 56 * 1024 * 1024
MESH = pl.DeviceIdType.MESH
ANY = pl.BlockSpec(memory_space=pl.ANY)


def _params(*sem):
    return pltpu.CompilerParams(dimension_semantics=sem, vmem_limit_bytes=VMEM_LIMIT)


def _pick(n, cands):
    for c in cands:
        if n % c == 0:
            return c
    return n


def _sigmoid(x):
    return 1.0 / (1.0 + jnp.exp(-x))


def _dot(a, b):
    return lax.dot_general(a, b, (((1,), (0,)), ((), ())), preferred_element_type=F32)


def _dot_nt(a, b):
    return lax.dot_general(a, b, (((1,), (1,)), ((), ())), preferred_element_type=F32)


def _dot_tn(a, b):
    return lax.dot_general(a, b, (((0,), (0,)), ((), ())), preferred_element_type=F32)


def _split_dot(x, m):
    hi = x.astype(BF16)
    lo = (x - hi.astype(F32)).astype(BF16)
    return _dot(hi, m) + _dot(lo, m)


def _matmul(a, b, *, mode, out_dtypes, epilogue=None, extras=(), name, after=None):
    if mode == "nn":
        (m, k), (k2, n) = a.shape, b.shape
    elif mode == "nt":
        (m, k), (n, k2) = a.shape, b.shape
    else:
        (k, m), (k2, n) = a.shape, b.shape
    assert k == k2, (a.shape, b.shape, mode)
    tm = _pick(m, (1056, 1024, 768, 512, 384, 256, 128, 96))
    tn = _pick(n, (1024, 768, 512, 256, 128))
    tk = _pick(k, (1056, 1024, 768, 512, 384, 256, 128, 96))
    nk = k // tk
    dot = {"nn": _dot, "nt": _dot_nt, "tn": _dot_tn}[mode]
    n_extra, n_out = len(extras), len(out_dtypes)
    n_after = 0 if after is None else 1
    if epilogue is None:
        epilogue = lambda acc: (acc,)

    def body(a_ref, b_ref, *rest):
        extra_refs = rest[:n_extra]
        out_refs = rest[n_extra + n_after:n_extra + n_after + n_out]
        part = dot(a_ref[...].astype(BF16), b_ref[...].astype(BF16))

        def finish(acc):
            res = epilogue(acc, *[r[...] for r in extra_refs])
            for o_ref, r in zip(out_refs, res):
                o_ref[...] = r.astype(o_ref.dtype)

        if nk == 1:
            finish(part)
        else:
            acc_ref = rest[-1]
            kk = pl.program_id(2)

            @pl.when(kk == 0)
            def _():
                acc_ref[...] = part

            @pl.when(kk > 0)
            def _():
                acc_ref[...] += part

            @pl.when(kk == nk - 1)
            def _():
                finish(acc_ref[...])

    if mode == "nn":
        a_spec = pl.BlockSpec((tm, tk), lambda i, j, kk: (i, kk))
        b_spec = pl.BlockSpec((tk, tn), lambda i, j, kk: (kk, j))
    elif mode == "nt":
        a_spec = pl.BlockSpec((tm, tk), lambda i, j, kk: (i, kk))
        b_spec = pl.BlockSpec((tn, tk), lambda i, j, kk: (j, kk))
    else:
        a_spec = pl.BlockSpec((tk, tm), lambda i, j, kk: (kk, i))
        b_spec = pl.BlockSpec((tk, tn), lambda i, j, kk: (kk, j))
    o_spec = pl.BlockSpec((tm, tn), lambda i, j, kk: (i, j))
    outs = pl.pallas_call(
        body,
        out_shape=[jax.ShapeDtypeStruct((m, n), dt) for dt in out_dtypes],
        grid=(m // tm, n // tn, nk),
        in_specs=[a_spec, b_spec] + [o_spec] * n_extra + [ANY] * n_after,
        out_specs=[o_spec] * n_out,
        scratch_shapes=[pltpu.VMEM((tm, tn), F32)] if nk > 1 else [],
        compiler_params=_params("parallel", "parallel", "arbitrary"),
        name=name,
    )(a, b, *extras, *([] if after is None else [after]))
    return outs[0] if n_out == 1 else outs


def _add_epilogue(acc, res):
    return (res + acc,)


def _rmsnorm_fwd(x, g, *, name):
    p, d = x.shape
    rows = _pick(p, (384, 128, 96))

    def body(x_ref, g_ref, o_ref):
        xv = x_ref[...]
        r = lax.rsqrt(jnp.mean(xv * xv, axis=-1, keepdims=True) + EPS)
        o_ref[...] = (xv * r * g_ref[...]).astype(o_ref.dtype)

    return pl.pallas_call(
        body,
        out_shape=jax.ShapeDtypeStruct((p, d), BF16),
        grid=(p // rows,),
        in_specs=[pl.BlockSpec((rows, d), lambda i: (i, 0)), pl.BlockSpec((1, d), lambda i: (0, 0))],
        out_specs=pl.BlockSpec((rows, d), lambda i: (i, 0)),
        compiler_params=_params("parallel"),
        name=name,
    )(x, g)


def _rmsnorm_bwd(x, g, dy, dres, *, name):
    p, d = x.shape
    rows = _pick(p, (384, 128, 96))

    def body(x_ref, g_ref, dy_ref, dres_ref, dx_ref, dg_ref):
        xv = x_ref[...]
        r = lax.rsqrt(jnp.mean(xv * xv, axis=-1, keepdims=True) + EPS)
        dyv = dy_ref[...]
        gdy = dyv * g_ref[...]
        proj = jnp.mean(xv * gdy, axis=-1, keepdims=True)
        dx_ref[...] = dres_ref[...] + r * gdy - xv * (r * r * r) * proj
        part = jnp.sum(dyv * xv * r, axis=0, keepdims=True)

        @pl.when(pl.program_id(0) == 0)
        def _():
            dg_ref[...] = part

        @pl.when(pl.program_id(0) > 0)
        def _():
            dg_ref[...] += part

    row_spec = pl.BlockSpec((rows, d), lambda i: (i, 0))
    vec_spec = pl.BlockSpec((1, d), lambda i: (0, 0))
    return pl.pallas_call(
        body,
        out_shape=[jax.ShapeDtypeStruct((p, d), F32), jax.ShapeDtypeStruct((1, d), F32)],
        grid=(p // rows,),
        in_specs=[row_spec, vec_spec, row_spec, row_spec],
        out_specs=[row_spec, vec_spec],
        compiler_params=_params("arbitrary"),
        name=name,
    )(x, g, dy, dres)


def _mlp_fwd(h, g, w1, w2, *, name):
    hn = _rmsnorm_fwd(h, g, name=name + "_norm")

    def act(acc):
        r = jnp.maximum(acc, 0.0)
        return acc, r * r

    z, a2 = _matmul(hn, w1, mode="nn", out_dtypes=(F32, BF16), epilogue=act, name=name + "_up")
    out = _matmul(a2, w2, mode="nn", out_dtypes=(F32,), epilogue=_add_epilogue, extras=(h,), name=name + "_down")
    return out, (hn, z, a2)


def _mlp_bwd(h, g, w1, w2, saved, dout, *, name, after=None):
    hn, z, a2 = saved

    def dact(acc, zt):
        return (acc * (2.0 * jnp.maximum(zt, 0.0)),)

    dz = _matmul(dout, w2, mode="nt", out_dtypes=(BF16,), epilogue=dact, extras=(z,), name=name + "_dz", after=after)
    dw2 = _matmul(a2, dout, mode="tn", out_dtypes=(F32,), name=name + "_dw2")
    dw1 = _matmul(hn, dz, mode="tn", out_dtypes=(F32,), name=name + "_dw1")
    dhn = _matmul(dz, w1, mode="nt", out_dtypes=(F32,), name=name + "_dhn")
    dh, dg = _rmsnorm_bwd(h, g, dhn, dout, name=name + "_dnorm")
    return dh, dg, dw1, dw2


def _retention_tables(p):
    half = RET_QK_DIM // 2
    inv_freq = ROPE_BASE ** (-jnp.arange(half, dtype=F32) / half)
    ang = jnp.arange(p, dtype=F32)[:, None] * inv_freq[None, :]
    cos, sin = jnp.cos(ang), jnp.sin(ang)
    cosf = jnp.concatenate([cos, cos], axis=1)
    sins = jnp.concatenate([-sin, sin], axis=1)
    log_g = jnp.log1p(-jnp.exp2(-RET_DECAY_OFFSET - jnp.arange(RET_HEADS, dtype=F32)))
    idx = jnp.arange(CHUNK, dtype=F32)
    diff = idx[:, None] - idx[None, :]
    inner = jnp.where(diff[None] >= 0, jnp.exp(jnp.maximum(diff, 0.0)[None] * log_g[:, None, None]), 0.0)
    kdec = jnp.exp((CHUNK - 1 - idx)[None, :] * log_g[:, None])
    qdec = jnp.exp((idx + 1.0)[None, :] * log_g[:, None])
    cdec = jnp.exp(CHUNK * log_g)
    kdec = jnp.broadcast_to(kdec[:, :, None], (RET_HEADS, CHUNK, RET_QK_DIM))
    qdec = jnp.broadcast_to(qdec[:, :, None], (RET_HEADS, CHUNK, RET_QK_DIM))
    cdec = jnp.broadcast_to(cdec[:, None, None], (RET_HEADS, RET_QK_DIM, RET_V_DIM))
    return cosf, sins, inner, kdec, qdec, cdec


def _rot(x, cosf, sins):
    return x * cosf + pltpu.roll(x, RET_QK_DIM // 2, 1) * sins


def _rot_bwd(dy, cosf, sins):
    return dy * cosf + pltpu.roll(dy * sins, RET_QK_DIM // 2, 1)


def _ret_in_specs(chunk_of):
    q_spec = pl.BlockSpec((CHUNK, RET_QK_W), lambda s: (chunk_of(s), 0))
    k_spec = pl.BlockSpec((CHUNK, RET_QK_W), lambda s: (chunk_of(s), 1))
    v_spec = pl.BlockSpec((CHUNK, RET_V_W), lambda s: (chunk_of(s), 1))
    g_spec = pl.BlockSpec((CHUNK, RET_V_W), lambda s: (chunk_of(s), 2))
    rope_spec = pl.BlockSpec((CHUNK, RET_QK_DIM), lambda s: (chunk_of(s), 0))
    whole = lambda *shape: pl.BlockSpec(shape, lambda s: (0,) * len(shape))
    head_sq = whole(RET_HEADS, CHUNK, CHUNK)
    head_qk = whole(RET_HEADS, CHUNK, RET_QK_DIM)
    head_st = whole(RET_HEADS, RET_QK_DIM, RET_V_DIM)
    gam_spec = whole(RET_HEADS, 1, RET_V_DIM)
    return [q_spec, k_spec, v_spec, g_spec, rope_spec, rope_spec, head_sq, head_qk, head_qk, head_st, gam_spec]


def _ret_head_views(h, qk_refs, v_refs, head_refs):
    qk = pl.ds(h * RET_QK_DIM, RET_QK_DIM)
    vv = pl.ds(h * RET_V_DIM, RET_V_DIM)
    return [r.at[:, qk] for r in qk_refs], [r.at[:, vv] for r in v_refs], [r.at[h] for r in head_refs]


def _retention_fwd(proj, gn_g, tables, *, name):
    p = proj.shape[0]
    n_chunks = p // CHUNK
    scale = RET_QK_DIM ** -0.5

    def body(q_ref, k_ref, v_ref, g_ref, cos_ref, sin_ref, inner_ref, kdec_ref, qdec_ref, cdec_ref, gam_ref,
             og_ref, opre_ref, sprev_ref, s_scr):
        @pl.when(pl.program_id(0) == 0)
        def _():
            s_scr[...] = jnp.zeros_like(s_scr)

        for h in range(RET_HEADS):
            (q_h, k_h), (v_h, g_h, og_h, opre_h), tabs = _ret_head_views(
                h, (q_ref, k_ref), (v_ref, g_ref, og_ref, opre_ref),
                (inner_ref, kdec_ref, qdec_ref, cdec_ref, gam_ref, sprev_ref, s_scr))
            one_head(q_h, k_h, v_h, g_h, cos_ref, sin_ref, *tabs[:5], og_h, opre_h, *tabs[5:])

    def one_head(q_ref, k_ref, v_ref, g_ref, cos_ref, sin_ref, inner_ref, kdec_ref, qdec_ref, cdec_ref, gam_ref,
                 og_ref, opre_ref, sprev_ref, s_scr):
        cosf, sins = cos_ref[...], sin_ref[...]
        qr = _rot(q_ref[...], cosf, sins)
        kr = _rot(k_ref[...], cosf, sins) * scale
        vb = v_ref[...].astype(BF16)
        scores = _dot_nt(qr.astype(BF16), kr.astype(BF16)) * inner_ref[...]
        state = s_scr[...]
        sprev_ref[...] = state
        o = _dot(scores.astype(BF16), vb) + _dot((qr * qdec_ref[...]).astype(BF16), state.astype(BF16))
        kd = kr * kdec_ref[...]
        s_scr[...] = cdec_ref[...] * state + _dot(kd.T.astype(BF16), vb)
        opre_ref[...] = o
        mu = jnp.mean(o, axis=-1, keepdims=True)
        oc = o - mu
        var = jnp.mean(oc * oc, axis=-1, keepdims=True)
        on = oc * lax.rsqrt(var + EPS) * gam_ref[...]
        gv = g_ref[...]
        og_ref[...] = (gv * _sigmoid(gv) * on).astype(og_ref.dtype)

    chunk_of = lambda s: s
    out_v = pl.BlockSpec((CHUNK, RET_V_W), lambda s: (s, 0))
    return pl.pallas_call(
        body,
        out_shape=[
            jax.ShapeDtypeStruct((p, RET_V_W), BF16),
            jax.ShapeDtypeStruct((p, RET_V_W), F32),
            jax.ShapeDtypeStruct((RET_HEADS, n_chunks, RET_QK_DIM, RET_V_DIM), F32),
        ],
        grid=(n_chunks,),
        in_specs=_ret_in_specs(chunk_of),
        out_specs=[out_v, out_v, pl.BlockSpec((RET_HEADS, None, RET_QK_DIM, RET_V_DIM), lambda s: (0, s, 0, 0))],
        scratch_shapes=[pltpu.VMEM((RET_HEADS, RET_QK_DIM, RET_V_DIM), F32)],
        compiler_params=_params("arbitrary"),
        name=name,
    )(proj, proj, proj, proj, *tables, gn_g.reshape(RET_HEADS, 1, RET_V_DIM))


def _retention_bwd(proj, gn_g, tables, opre, sprev, dog, *, name):
    p = proj.shape[0]
    n_chunks = p // CHUNK
    scale = RET_QK_DIM ** -0.5

    def body(q_ref, k_ref, v_ref, g_ref, cos_ref, sin_ref, inner_ref, kdec_ref, qdec_ref, cdec_ref, gam_ref,
             opre_ref, sprev_ref, dog_ref, dq_ref, dk_ref, dv_ref, dg_ref, dgam_ref, ds_scr):
        first = pl.program_id(0) == 0

        @pl.when(first)
        def _():
            ds_scr[...] = jnp.zeros_like(ds_scr)

        for h in range(RET_HEADS):
            (q_h, k_h, dq_h, dk_h), (v_h, g_h, opre_h, dog_h, dv_h, dg_h), tabs = _ret_head_views(
                h, (q_ref, k_ref, dq_ref, dk_ref), (v_ref, g_ref, opre_ref, dog_ref, dv_ref, dg_ref),
                (inner_ref, kdec_ref, qdec_ref, cdec_ref, gam_ref, sprev_ref, dgam_ref, ds_scr))
            one_head(first, q_h, k_h, v_h, g_h, cos_ref, sin_ref, *tabs[:5], opre_h, tabs[5], dog_h,
                     dq_h, dk_h, dv_h, dg_h, tabs[6], tabs[7])

    def one_head(first, q_ref, k_ref, v_ref, g_ref, cos_ref, sin_ref, inner_ref, kdec_ref, qdec_ref, cdec_ref, gam_ref,
                 opre_ref, sprev_ref, dog_ref, dq_ref, dk_ref, dv_ref, dg_ref, dgam_ref, ds_scr):
        cosf, sins = cos_ref[...], sin_ref[...]
        qr = _rot(q_ref[...], cosf, sins)
        kr = _rot(k_ref[...], cosf, sins) * scale
        qb, kb = qr.astype(BF16), kr.astype(BF16)
        vb = v_ref[...].astype(BF16)
        inner = inner_ref[...]
        qdec, kdec = qdec_ref[...], kdec_ref[...]
        state_b = sprev_ref[...].astype(BF16)
        o = opre_ref[...]
        mu = jnp.mean(o, axis=-1, keepdims=True)
        oc = o - mu
        rstd = lax.rsqrt(jnp.mean(oc * oc, axis=-1, keepdims=True) + EPS)
        xhat = oc * rstd
        gam = gam_ref[...]
        on = xhat * gam
        gv = g_ref[...]
        sig = _sigmoid(gv)
        dogv = dog_ref[...]
        dg_ref[...] = (dogv * on * sig * (1.0 + gv * (1.0 - sig))).astype(dg_ref.dtype)
        don = dogv * gv * sig
        dgam_part = jnp.sum(don * xhat, axis=0, keepdims=True)

        @pl.when(first)
        def _():
            dgam_ref[...] = dgam_part

        @pl.when(jnp.logical_not(first))
        def _():
            dgam_ref[...] += dgam_part

        dxhat = don * gam
        do = rstd * (dxhat - jnp.mean(dxhat, axis=-1, keepdims=True)
                     - xhat * jnp.mean(dxhat * xhat, axis=-1, keepdims=True))
        dob = do.astype(BF16)
        scores_b = (_dot_nt(qb, kb) * inner).astype(BF16)
        da = (_dot_nt(dob, vb) * inner).astype(BF16)
        dv = _dot(scores_b.astype(F32).T.astype(BF16), dob)
        dqr = _dot(da, kb)
        dkr = _dot(da.astype(F32).T.astype(BF16), qb)
        dqr += _dot_nt(dob, state_b) * qdec
        ds_local = _dot((qr * qdec).T.astype(BF16), dob)
        gstate = ds_scr[...]
        gb = gstate.astype(BF16)
        kd_b = (kr * kdec).astype(BF16)
        dkr += _dot_nt(vb, gb) * kdec
        dv += _dot(kd_b, gb)
        ds_scr[...] = cdec_ref[...] * gstate + ds_local
        dq_ref[...] = _rot_bwd(dqr, cosf, sins).astype(dq_ref.dtype)
        dk_ref[...] = _rot_bwd(dkr * scale, cosf, sins).astype(dk_ref.dtype)
        dv_ref[...] = dv.astype(dv_ref.dtype)

    chunk_of = lambda s: n_chunks - 1 - s
    blk_v = pl.BlockSpec((CHUNK, RET_V_W), lambda s: (chunk_of(s), 0))
    blk_qk = pl.BlockSpec((CHUNK, RET_QK_W), lambda s: (chunk_of(s), 0))
    st_spec = pl.BlockSpec((RET_HEADS, None, RET_QK_DIM, RET_V_DIM), lambda s: (0, chunk_of(s), 0, 0))
    return pl.pallas_call(
        body,
        out_shape=[
            jax.ShapeDtypeStruct((p, RET_QK_W), BF16),
            jax.ShapeDtypeStruct((p, RET_QK_W), BF16),
            jax.ShapeDtypeStruct((p, RET_V_W), BF16),
            jax.ShapeDtypeStruct((p, RET_V_W), BF16),
            jax.ShapeDtypeStruct((RET_HEADS, 1, RET_V_DIM), F32),
        ],
        grid=(n_chunks,),
        in_specs=_ret_in_specs(chunk_of) + [blk_v, st_spec, blk_v],
        out_specs=[blk_qk, blk_qk, blk_v, blk_v, pl.BlockSpec((RET_HEADS, 1, RET_V_DIM), lambda s: (0, 0, 0))],
        scratch_shapes=[pltpu.VMEM((RET_HEADS, RET_QK_DIM, RET_V_DIM), F32)],
        compiler_params=_params("arbitrary"),
        name=name,
    )(proj, proj, proj, proj, *tables, gn_g.reshape(RET_HEADS, 1, RET_V_DIM), opre, sprev, dog)


def _conv_rows(p):
    return _pick(p, (384, 128))


CONV_CHUNK = 32
F32_SUBLANES = 8


def _shifted_rows(rows):
    return rows + CONV_HALO - F32_SUBLANES


def _shifted_copies(src_scr, sh_scr, n_rows):
    for s in range(1, F32_SUBLANES):
        sh_scr[s - 1] = src_scr[s:s + n_rows, :]


def _tap_rows(src_scr, sh_scr, off, r0, n):
    q, s = divmod(off, F32_SUBLANES)
    ref = src_scr if s == 0 else sh_scr.at[s - 1]
    return ref[pl.ds(pl.multiple_of(r0 + F32_SUBLANES * q, F32_SUBLANES), n), :]


def _ln_stats(y):
    mu = jnp.mean(y, axis=-1, keepdims=True)
    yc = y - mu
    rstd = lax.rsqrt(jnp.mean(yc * yc, axis=-1, keepdims=True) + EPS)
    return yc * rstd, rstd


def _conv_fwd(proj, conv_w, conv_b, ln_g, ln_b, *, name):
    p = proj.shape[0]
    c = D_MODEL
    rows = _conv_rows(p)
    hpb = rows // CONV_HALO
    a_col, gate_col = (2 * RET_QK_W + 2 * RET_V_W) // c, (2 * RET_QK_W + 2 * RET_V_W) // c + 1

    def body(a_ref, gate_ref, ah_ref, gateh_ref, w_ref, b_ref, lg_ref, lb_ref, c_ref, y_ref, hdn_scr, sh_scr):
        i = pl.program_id(0)
        hdn_scr[0:CONV_HALO, :] = ah_ref[...] * _sigmoid(gateh_ref[...])
        hdn_scr[CONV_HALO:, :] = a_ref[...] * _sigmoid(gate_ref[...])
        _shifted_copies(hdn_scr, sh_scr, _shifted_rows(rows))

        def chunk(j, _):
            r0 = pl.multiple_of(j * CONV_CHUNK, CONV_CHUNK)
            acc = jnp.zeros((CONV_CHUNK, c), F32)
            for w in range(CONV_WIDTH):
                off = CONV_HALO - (CONV_WIDTH - 1) + w
                acc += _tap_rows(hdn_scr, sh_scr, off, r0, CONV_CHUNK) * w_ref[w:w + 1, :]
            y_ref[pl.ds(r0, CONV_CHUNK), :] = acc + b_ref[...]
            return 0

        lax.fori_loop(0, rows // CONV_CHUNK, chunk, 0)
        y = y_ref[...]
        yhat, _ = _ln_stats(y)
        ln = yhat * lg_ref[...] + lb_ref[...]
        row = i * rows + lax.broadcasted_iota(jnp.int32, (rows, 1), 0)
        c_ref[...] = jnp.where(row >= PAD_FRONT, ln * _sigmoid(ln), 0.0).astype(c_ref.dtype)

    halo_idx = lambda i: jnp.maximum(i * hpb - 1, 0)
    vec = pl.BlockSpec((1, c), lambda i: (0, 0))
    return pl.pallas_call(
        body,
        out_shape=[jax.ShapeDtypeStruct((p, c), BF16), jax.ShapeDtypeStruct((p, c), F32)],
        grid=(p // rows,),
        in_specs=[
            pl.BlockSpec((rows, c), lambda i: (i, a_col)),
            pl.BlockSpec((rows, c), lambda i: (i, gate_col)),
            pl.BlockSpec((CONV_HALO, c), lambda i: (halo_idx(i), a_col)),
            pl.BlockSpec((CONV_HALO, c), lambda i: (halo_idx(i), gate_col)),
            pl.BlockSpec((CONV_WIDTH, c), lambda i: (0, 0)),
            vec, vec, vec,
        ],
        out_specs=[pl.BlockSpec((rows, c), lambda i: (i, 0)), pl.BlockSpec((rows, c), lambda i: (i, 0))],
        scratch_shapes=[pltpu.VMEM((CONV_HALO + rows, c), F32),
                        pltpu.VMEM((F32_SUBLANES - 1, _shifted_rows(rows), c), F32)],
        compiler_params=_params("parallel"),
        name=name,
    )(proj, proj, proj, proj, conv_w, conv_b, ln_g, ln_b)


def _conv_bwd(proj, conv_w, ln_g, ln_b, y, dcat, *, name):
    p = proj.shape[0]
    c = D_MODEL
    rows = _conv_rows(p)
    hpb = rows // CONV_HALO
    n_blocks = p // rows
    a_col, gate_col = (2 * RET_QK_W + 2 * RET_V_W) // c, (2 * RET_QK_W + 2 * RET_V_W) // c + 1

    def body(a_ref, gate_ref, ah_ref, gateh_ref, w_ref, lg_ref, lb_ref, y_ref, yh_ref, dc_ref, dch_ref,
             da_ref, dgate_ref, dw_ref, db_ref, dlg_ref, dlb_ref, hdn_scr, dy_scr, hdn_sh, dy_sh):
        i = pl.program_id(0)
        lg, lb = lg_ref[...], lb_ref[...]

        def ln_bwd(yv, dcv):
            yhat, rstd = _ln_stats(yv)
            ln = yhat * lg + lb
            sig = _sigmoid(ln)
            dln = dcv * sig * (1.0 + ln * (1.0 - sig))
            dyhat = dln * lg
            dyv = rstd * (dyhat - jnp.mean(dyhat, axis=-1, keepdims=True)
                          - yhat * jnp.mean(dyhat * yhat, axis=-1, keepdims=True))
            return dyv, dln, yhat

        row = i * rows + lax.broadcasted_iota(jnp.int32, (rows, 1), 0)
        dy, dln, yhat = ln_bwd(y_ref[...], jnp.where(row >= PAD_FRONT, dc_ref[...], 0.0))
        dy_halo, _, _ = ln_bwd(yh_ref[...], dch_ref[...])
        dy_scr[0:rows, :] = dy
        dy_scr[rows:, :] = jnp.where(i == n_blocks - 1, 0.0, dy_halo)
        hdn_scr[0:CONV_HALO, :] = ah_ref[...] * _sigmoid(gateh_ref[...])
        hdn_scr[CONV_HALO:, :] = a_ref[...] * _sigmoid(gate_ref[...])
        _shifted_copies(hdn_scr, hdn_sh, _shifted_rows(rows))
        _shifted_copies(dy_scr, dy_sh, _shifted_rows(rows))

        @pl.when(i == 0)
        def _():
            dw_ref[...] = jnp.zeros_like(dw_ref)
            db_ref[...] = jnp.zeros_like(db_ref)
            dlg_ref[...] = jnp.zeros_like(dlg_ref)
            dlb_ref[...] = jnp.zeros_like(dlb_ref)

        n_chunks = rows // CONV_CHUNK

        def input_grad(j, _):
            r0 = pl.multiple_of(j * CONV_CHUNK, CONV_CHUNK)
            dhdn = jnp.zeros((CONV_CHUNK, c), F32)
            for w in range(CONV_WIDTH):
                dhdn += _tap_rows(dy_scr, dy_sh, CONV_WIDTH - 1 - w, r0, CONV_CHUNK) * w_ref[w:w + 1, :]
            here = pl.ds(r0, CONV_CHUNK)
            sig_gate = _sigmoid(gate_ref[here, :])
            da_ref[here, :] = (dhdn * sig_gate).astype(da_ref.dtype)
            dgate_ref[here, :] = (dhdn * a_ref[here, :] * sig_gate * (1.0 - sig_gate)).astype(dgate_ref.dtype)
            return 0

        lax.fori_loop(0, n_chunks, input_grad, 0)
        for w in range(CONV_WIDTH):
            off = CONV_HALO - (CONV_WIDTH - 1) + w

            def tap_grad(j, acc, off=off):
                r0 = pl.multiple_of(j * CONV_CHUNK, CONV_CHUNK)
                prod = dy_scr[pl.ds(r0, CONV_CHUNK), :] * _tap_rows(hdn_scr, hdn_sh, off, r0, CONV_CHUNK)
                for k in range(CONV_CHUNK // F32_SUBLANES):
                    acc = acc + prod[k * F32_SUBLANES:(k + 1) * F32_SUBLANES]
                return acc

            acc = lax.fori_loop(0, n_chunks, tap_grad, jnp.zeros((F32_SUBLANES, c), F32))
            dw_ref[w:w + 1, :] += jnp.sum(acc, axis=0, keepdims=True)
        db_ref[...] += jnp.sum(dy, axis=0, keepdims=True)
        dlg_ref[...] += jnp.sum(dln * yhat, axis=0, keepdims=True)
        dlb_ref[...] += jnp.sum(dln, axis=0, keepdims=True)

    prev_halo = lambda i: jnp.maximum(i * hpb - 1, 0)
    next_halo = lambda i: jnp.minimum((i + 1) * hpb, p // CONV_HALO - 1)
    vec = pl.BlockSpec((1, c), lambda i: (0, 0))
    blk = lambda col: pl.BlockSpec((rows, c), lambda i: (i, col))
    outs = pl.pallas_call(
        body,
        out_shape=[
            jax.ShapeDtypeStruct((p, c), BF16),
            jax.ShapeDtypeStruct((p, c), BF16),
            jax.ShapeDtypeStruct((CONV_WIDTH + 1, c), F32),
            jax.ShapeDtypeStruct((1, c), F32),
            jax.ShapeDtypeStruct((1, c), F32),
            jax.ShapeDtypeStruct((1, c), F32),
        ],
        grid=(n_blocks,),
        in_specs=[
            blk(a_col), blk(gate_col),
            pl.BlockSpec((CONV_HALO, c), lambda i: (prev_halo(i), a_col)),
            pl.BlockSpec((CONV_HALO, c), lambda i: (prev_halo(i), gate_col)),
            pl.BlockSpec((CONV_WIDTH, c), lambda i: (0, 0)),
            vec, vec,
            blk(0),
            pl.BlockSpec((CONV_HALO, c), lambda i: (next_halo(i), 0)),
            blk(1),
            pl.BlockSpec((CONV_HALO, c), lambda i: (next_halo(i), 1)),
        ],
        out_specs=[blk(0), blk(0), pl.BlockSpec((CONV_WIDTH + 1, c), lambda i: (0, 0)), vec, vec, vec],
        scratch_shapes=[pltpu.VMEM((CONV_HALO + rows, c), F32), pltpu.VMEM((rows + CONV_HALO, c), F32),
                        pltpu.VMEM((F32_SUBLANES - 1, _shifted_rows(rows), c), F32),
                        pltpu.VMEM((F32_SUBLANES - 1, _shifted_rows(rows), c), F32)],
        compiler_params=_params("arbitrary"),
        name=name,
    )(proj, proj, proj, proj, conv_w, ln_g, ln_b, y, y, dcat, dcat)
    da, dgate, dw, db, dlg, dlb = outs
    return da, dgate, dw[:CONV_WIDTH], db, dlg, dlb


LANES = 128


def _group_matrix():
    r = jnp.arange(LANES)[:, None] // SB_HEAD_DIM
    c = jnp.arange(LANES)[None, :] // SB_HEAD_DIM
    return (r == c).astype(BF16)


def _head_sums(v, gm):
    return jnp.concatenate([_split_dot(v[:, j * LANES:(j + 1) * LANES], gm) for j in range(v.shape[1] // LANES)], axis=1)


def _qknorm_fwd(qkv, qg, kg, *, name):
    p = qkv.shape[0]
    d = D_MODEL
    rows = _pick(p, (384, 128, 96))

    def body(q_ref, k_ref, v_ref, qg_ref, kg_ref, gm_ref, qn_ref, kn_ref, vb_ref):
        gm = gm_ref[...]

        def norm(x, g):
            ms = _head_sums(x * x, gm) * (1.0 / SB_HEAD_DIM)
            return x * lax.rsqrt(ms + EPS) * g

        qn_ref[...] = norm(q_ref[...], qg_ref[...]).astype(BF16)
        kn_ref[...] = norm(k_ref[...], kg_ref[...]).astype(BF16)
        vb_ref[...] = v_ref[...].astype(BF16)

    blk = lambda col: pl.BlockSpec((rows, d), lambda i: (i, col))
    vec = pl.BlockSpec((1, d), lambda i: (0, 0))
    return pl.pallas_call(
        body,
        out_shape=[jax.ShapeDtypeStruct((p, d), BF16)] * 3,
        grid=(p // rows,),
        in_specs=[blk(0), blk(1), blk(2), vec, vec, pl.BlockSpec((LANES, LANES), lambda i: (0, 0))],
        out_specs=[blk(0)] * 3,
        compiler_params=_params("parallel"),
        name=name,
    )(qkv, qkv, qkv, qg, kg, _group_matrix())


def _qknorm_bwd(qkv, qg, kg, dqn, dkn, dv, *, name):
    p = qkv.shape[0]
    d = D_MODEL
    rows = _pick(p, (384, 128, 96))

    def body(q_ref, k_ref, qg_ref, kg_ref, gm_ref, dqn_ref, dkn_ref, dv_ref, dqkv_ref, dqg_ref, dkg_ref):
        gm = gm_ref[...]

        def bwd(x, g, dy):
            ms = _head_sums(x * x, gm) * (1.0 / SB_HEAD_DIM)
            r = lax.rsqrt(ms + EPS)
            gdy = dy * g
            proj = _head_sums(x * gdy, gm) * (1.0 / SB_HEAD_DIM)
            return r * gdy - x * (r * r * r) * proj, jnp.sum(dy * x * r, axis=0, keepdims=True)

        dq, dqg = bwd(q_ref[...], qg_ref[...], dqn_ref[...])
        dk, dkg = bwd(k_ref[...], kg_ref[...], dkn_ref[...])
        dqkv_ref[:, 0:d] = dq.astype(BF16)
        dqkv_ref[:, d:2 * d] = dk.astype(BF16)
        dqkv_ref[:, 2 * d:3 * d] = dv_ref[...].astype(BF16)

        @pl.when(pl.program_id(0) == 0)
        def _():
            dqg_ref[...] = dqg
            dkg_ref[...] = dkg

        @pl.when(pl.program_id(0) > 0)
        def _():
            dqg_ref[...] += dqg
            dkg_ref[...] += dkg

    blk = lambda col: pl.BlockSpec((rows, d), lambda i: (i, col))
    vec = pl.BlockSpec((1, d), lambda i: (0, 0))
    return pl.pallas_call(
        body,
        out_shape=[jax.ShapeDtypeStruct((p, 3 * d), BF16), jax.ShapeDtypeStruct((1, d), F32),
                   jax.ShapeDtypeStruct((1, d), F32)],
        grid=(p // rows,),
        in_specs=[blk(0), blk(1), vec, vec, pl.BlockSpec((LANES, LANES), lambda i: (0, 0)), blk(0), blk(0), blk(0)],
        out_specs=[pl.BlockSpec((rows, 3 * d), lambda i: (i, 0)), vec, vec],
        compiler_params=_params("arbitrary"),
        name=name,
    )(qkv, qkv, qg, kg, _group_matrix(), dqn, dkn, dv)


SB_PAIR = 2 * SB_HEAD_DIM
SB_GROUP = 8
SB_PAIRS_PER_STEP = 2
SB_PAIRS_PER_STEP_BWD = 1
SB_MASKED = -1e30


def _sb_consts():
    lane = lax.broadcasted_iota(jnp.int32, (CHUNK, SB_PAIR), 1)
    r = lax.broadcasted_iota(jnp.int32, (CHUNK, CHUNK), 0)
    c = lax.broadcasted_iota(jnp.int32, (CHUNK, CHUNK), 1)
    lo = (lane < SB_HEAD_DIM).astype(F32).astype(BF16)
    ones = jnp.ones((CHUNK, CHUNK), BF16)
    twice = lambda m: jnp.concatenate([jnp.concatenate([m, ones], axis=1)] * 2, axis=0)
    later, earlier = twice((r > c).astype(BF16)), twice((r < c).astype(BF16))
    not_before = (c >= r).astype(F32) * SB_MASKED
    padding = (c < PAD_FRONT).astype(F32) * SB_MASKED
    return (lo, 1.0 - lo), c, later, earlier, not_before, padding


def _sb_halves(t, head_lanes):
    return t * head_lanes[0], t * head_lanes[1]


def _sb_logits(qh, kg, biases):
    z = _dot_nt(qh, kg)
    tiles = []
    for b, bias in enumerate(biases):
        zt = z[:, b * CHUNK:(b + 1) * CHUNK]
        if bias is not None:
            zt = zt + bias
        ls_pos = jnp.minimum(zt, 0.0) - jnp.log(1.0 + jnp.exp(-jnp.abs(zt)))
        tiles.append((ls_pos, ls_pos - zt))
    return tiles


def _sb_block_sums(tiles, m):
    st = jnp.concatenate(tiles, axis=0)
    hi = st.astype(BF16)
    lo = (st - hi.astype(F32)).astype(BF16)
    tot = _dot(jnp.concatenate([hi, lo], axis=1), m)
    return [(tot[i * CHUNK:(i + 1) * CHUNK, 0:CHUNK], tot[i * CHUNK:(i + 1) * CHUNK, CHUNK:2 * CHUNK])
            for i in range(len(tiles))]


def _sb_plan(qi, padding, not_before):
    top = lax.div(qi, SB_GROUP)
    size = qi - SB_GROUP * top + 1

    def masks(n_b):
        pad_if_first = padding * (top == 0).astype(F32)
        m = [None] * n_b
        m[n_b - 1] = not_before
        m[0] = pad_if_first if m[0] is None else m[0] + pad_if_first
        return m

    return top, size, masks


def _once_if(cond, fn, carry):
    return lax.fori_loop(0, jnp.where(cond, 1, 0), lambda s, cr: fn(cr), carry)


def _sb_head_rows(tg, lanes, n_b):
    return jnp.concatenate([tg[b * CHUNK:(b + 1) * CHUNK] * lanes for b in range(n_b)], axis=0)


def _sb_fwd(qn, kn, vb, *, name):
    p = qn.shape[0]
    n_blocks = p // CHUNK
    n_pairs = SB_HEADS // 2
    scale = SB_HEAD_DIM ** -0.5

    n_step = SB_PAIRS_PER_STEP
    n_chains = 2 * n_step
    lanes_of = lambda pair: slice(pair * SB_PAIR, (pair + 1) * SB_PAIR)

    def body(q_ref, k_ref, v_ref, o_ref, car_ref):
        head_lanes, c, later, _, not_before, padding = _sb_consts()

        def q_block(qi, _):
            rows = pl.ds(pl.multiple_of(qi * CHUNK, CHUNK), CHUNK)
            qs = []
            for pair in range(n_step):
                qh = _sb_halves(q_ref[rows, lanes_of(pair)], head_lanes)
                qs += [qh[0] * scale, qh[1] * scale]

            def blocks(kb0, biases, carry):
                n_b = len(biases)
                accs, runs, savs = list(carry[:n_step]), list(carry[n_step:n_step + n_chains]), list(carry[n_step + n_chains:])
                krows = pl.ds(pl.multiple_of(kb0 * CHUNK, CHUNK), n_b * CHUNK)
                kgs = [k_ref[krows, lanes_of(pair)] for pair in range(n_step)]
                vgs = [v_ref[krows, lanes_of(pair)] for pair in range(n_step)]
                tiles = [_sb_logits(qs[ch], kgs[ch // 2], biases) for ch in range(n_chains)]
                sums = [_sb_block_sums([log_keep for _, log_keep in tiles[ch]], later) for ch in range(n_chains)]
                cols = [(c == kb0 + b).astype(F32) for b in range(n_b)]
                for ch in range(n_chains):
                    ws = [None] * n_b
                    for b in reversed(range(n_b)):
                        after, row_sum = sums[ch][b]
                        ws[b] = jnp.exp(tiles[ch][b][0] + after + runs[ch]).astype(BF16)
                        savs[ch] = savs[ch] + cols[b] * runs[ch]
                        runs[ch] = runs[ch] + row_sum
                    accs[ch // 2] = accs[ch // 2] + _dot(jnp.concatenate(ws, axis=1),
                                                         _sb_head_rows(vgs[ch // 2], head_lanes[ch % 2], n_b))
                return (*accs, *runs, *savs)

            zt = qs[0].astype(F32) * 0.0
            top, size, masks = _sb_plan(qi, padding, not_before)
            carry = (zt,) * (n_step + 2 * n_chains)
            for n_b in range(1, SB_GROUP + 1):
                carry = _once_if(size == n_b, lambda cr, n_b=n_b: blocks(SB_GROUP * top, masks(n_b), cr), carry)
            carry = lax.fori_loop(0, jnp.maximum(top - 1, 0),
                                  lambda it, cr: blocks(SB_GROUP * (top - 1 - it), [None] * SB_GROUP, cr), carry)
            carry = _once_if(top > 0, functools.partial(blocks, 0, [padding] + [None] * (SB_GROUP - 1)), carry)
            for pair in range(n_step):
                o_ref[rows, lanes_of(pair)] = carry[pair].astype(o_ref.dtype)
            for ch in range(n_chains):
                car_ref[rows, ch * CHUNK:(ch + 1) * CHUNK] = carry[n_step + n_chains + ch]
            return 0

        lax.fori_loop(0, n_blocks, q_block, 0)

    col = pl.BlockSpec((p, n_step * SB_PAIR), lambda g: (0, g))
    return pl.pallas_call(
        body,
        out_shape=[jax.ShapeDtypeStruct((p, D_MODEL), BF16), jax.ShapeDtypeStruct((p, n_pairs * 2 * CHUNK), F32)],
        grid=(n_pairs // n_step,),
        in_specs=[col, col, col],
        out_specs=[col, pl.BlockSpec((p, n_chains * CHUNK), lambda g: (0, g))],
        compiler_params=_params("parallel"),
        name=name,
    )(qn, kn, vb)


def _sb_bwd(qn, kn, vb, carries, do, *, name):
    p = qn.shape[0]
    n_blocks = p // CHUNK
    n_pairs = SB_HEADS // 2
    scale = SB_HEAD_DIM ** -0.5

    n_step = SB_PAIRS_PER_STEP_BWD
    n_chains = 2 * n_step
    lanes_of = lambda pair: slice(pair * SB_PAIR, (pair + 1) * SB_PAIR)

    def body(q_ref, k_ref, v_ref, car_ref, do_ref, dq_ref, dk_ref, dv_ref):
        head_lanes, c, later, earlier, not_before, padding = _sb_consts()
        dk_ref[...] = jnp.zeros_like(dk_ref)
        dv_ref[...] = jnp.zeros_like(dv_ref)

        def q_block(qi, _):
            rows = pl.ds(pl.multiple_of(qi * CHUNK, CHUNK), CHUNK)
            qs, doh, do2, q2 = [], [], [], []
            for pair in range(n_step):
                qh = _sb_halves(q_ref[rows, lanes_of(pair)], head_lanes)
                qs += [qh[0] * scale, qh[1] * scale]
                doh += list(_sb_halves(do_ref[rows, lanes_of(pair)].astype(BF16), head_lanes))
                do2.append(jnp.concatenate(doh[-2:], axis=0))
                q2.append(jnp.concatenate(qs[-2:], axis=0))
            sav = [car_ref[rows, ch * CHUNK:(ch + 1) * CHUNK] for ch in range(n_chains)]

            def blocks(kb0, biases, carry):
                n_b = len(biases)
                dq_accs, pres = list(carry[:n_step]), list(carry[n_step:])
                krows = pl.ds(pl.multiple_of(kb0 * CHUNK, CHUNK), n_b * CHUNK)
                kgs = [k_ref[krows, lanes_of(pair)] for pair in range(n_step)]
                vgs = [v_ref[krows, lanes_of(pair)] for pair in range(n_step)]
                cols = [(c == kb0 + b).astype(F32) for b in range(n_b)]
                block = lambda t, b: t[:, b * CHUNK:(b + 1) * CHUNK]
                tiles = [_sb_logits(qs[ch], kgs[ch // 2], biases) for ch in range(n_chains)]
                afters = [_sb_block_sums([log_keep for _, log_keep in tiles[ch]], later) for ch in range(n_chains)]
                dws = [_dot_nt(doh[ch], vgs[ch // 2]) for ch in range(n_chains)]
                ws, es, befores = [], [], []
                for ch in range(n_chains):
                    runs = [jnp.sum(cols[b] * sav[ch], axis=-1, keepdims=True) for b in range(n_b)]
                    ws.append([jnp.exp(tiles[ch][b][0] + afters[ch][b][0] + runs[b]) for b in range(n_b)])
                    es.append([ws[ch][b] * block(dws[ch], b) for b in range(n_b)])
                    befores.append(_sb_block_sums(es[ch], earlier))
                dz2, w2 = [], []
                for ch in range(n_chains):
                    dzs = []
                    for b in range(n_b):
                        before, row_sum = befores[ch][b]
                        sig = jnp.exp(tiles[ch][b][0])
                        e = es[ch][b]
                        dzs.append((e - (e + before + pres[ch]) * sig).astype(BF16))
                        pres[ch] = pres[ch] + row_sum
                    dz2.append(jnp.concatenate(dzs, axis=1))
                    w2.append(jnp.concatenate([t.astype(BF16) for t in ws[ch]], axis=1))
                    dq_accs[ch // 2] = dq_accs[ch // 2] + _dot(dz2[ch], _sb_head_rows(kgs[ch // 2], head_lanes[ch % 2], n_b))
                for pair in range(n_step):
                    both = slice(2 * pair, 2 * pair + 2)
                    dv_ref[krows, lanes_of(pair)] += _dot_tn(jnp.concatenate(w2[both], axis=0), do2[pair])
                    dk_ref[krows, lanes_of(pair)] += _dot_tn(jnp.concatenate(dz2[both], axis=0), q2[pair])
                return (*dq_accs, *pres)

            zt = qs[0].astype(F32) * 0.0
            top, size, masks = _sb_plan(qi, padding, not_before)
            carry = _once_if(top > 0, functools.partial(blocks, 0, [padding] + [None] * (SB_GROUP - 1)),
                             (zt,) * (n_step + n_chains))
            carry = lax.fori_loop(1, top, lambda g, cr: blocks(SB_GROUP * g, [None] * SB_GROUP, cr), carry)
            for n_b in range(1, SB_GROUP + 1):
                carry = _once_if(size == n_b, lambda cr, n_b=n_b: blocks(SB_GROUP * top, masks(n_b), cr), carry)
            for pair in range(n_step):
                dq_ref[rows, lanes_of(pair)] = carry[pair] * scale
            return 0

        lax.fori_loop(0, n_blocks, q_block, 0)

    col = pl.BlockSpec((p, n_step * SB_PAIR), lambda g: (0, g))
    return pl.pallas_call(
        body,
        out_shape=[jax.ShapeDtypeStruct((p, D_MODEL), F32)] * 3,
        grid=(n_pairs // n_step,),
        in_specs=[col, col, col, pl.BlockSpec((p, n_chains * CHUNK), lambda g: (0, g)), col],
        out_specs=[col, col, col],
        compiler_params=_params("parallel"),
        name=name,
    )(qn, kn, vb, carries, do)


def _loss_head(h, target, *, name):
    p, d = h.shape
    n_blocks = p // CHUNK

    def body(h_ref, t_ref, sq_ref, dh_ref):
        i = pl.program_id(0)

        @pl.when(i == 0)
        def _():
            sq_ref[...] = jnp.zeros_like(sq_ref)
            dh_ref[...] = jnp.zeros_like(dh_ref)

        @pl.when(i > 0)
        def _():
            err = h_ref[...] - t_ref[...]
            sq_ref[...] += jnp.sum(err * err)
            dh_ref[...] = err * (1.0 / d)

    return pl.pallas_call(
        body,
        out_shape=[jax.ShapeDtypeStruct((8, 128), F32), jax.ShapeDtypeStruct((p, d), F32)],
        grid=(n_blocks,),
        in_specs=[pl.BlockSpec((CHUNK, d), lambda i: (i, 0)),
                  pl.BlockSpec((CHUNK, d), lambda i: (jnp.maximum(i - 1, 0), 0))],
        out_specs=[pl.BlockSpec((8, 128), lambda i: (0, 0)), pl.BlockSpec((CHUNK, d), lambda i: (i, 0))],
        compiler_params=_params("arbitrary"),
        name=name,
    )(h, target)


def _local_step(x, target, meta, norm_mix_g, norm_mlp_g, w_in, gn_g, conv_w, conv_b, ln_g, ln_b, qn_g, kn_g, later,
                reached=lambda point, after, grads=None: None):
    seq = x.shape[0]
    p = PAD_FRONT + N_META + seq
    d = D_MODEL
    tables = _retention_tables(p)
    row = lambda v: v.reshape(1, -1)
    h0 = jnp.concatenate([jnp.zeros((PAD_FRONT, d), F32), meta, x], axis=0)

    hn0 = _rmsnorm_fwd(h0, row(norm_mix_g[0]), name="l0_mix_norm")
    proj = _matmul(hn0, w_in, mode="nn", out_dtypes=(F32,), name="l0_proj")
    og, opre, sprev = _retention_fwd(proj, gn_g, tables, name="l0_retention")
    cb, y_conv = _conv_fwd(proj, conv_w, row(conv_b), row(ln_g), row(ln_b), name="l0_conv")
    cat = jnp.concatenate([og, cb], axis=1)
    w_out, w1_0, w2_0 = later("l0", cat)
    w1, w2 = [w1_0, None], [w2_0, None]
    h1 = _matmul(cat, w_out, mode="nn", out_dtypes=(F32,), epilogue=_add_epilogue, extras=(h0,), name="l0_mix_out")
    h2, mlp0 = _mlp_fwd(h1, row(norm_mlp_g[0]), w1[0], w2[0], name="l0_mlp")

    hn1 = _rmsnorm_fwd(h2, row(norm_mix_g[1]), name="l1_mix_norm")
    (w_qkv,) = later("qkv", hn1)
    qkv = _matmul(hn1, w_qkv, mode="nn", out_dtypes=(F32,), name="l1_qkv")
    qg_t, kg_t = jnp.tile(row(qn_g), (1, SB_HEADS)), jnp.tile(row(kn_g), (1, SB_HEADS))
    qn, kn, vb = _qknorm_fwd(qkv, qg_t, kg_t, name="l1_qknorm")
    o_sb, carries = _sb_fwd(qn, kn, vb, name="l1_stickbreak")
    w_o, w1[1], w2[1] = later("l1", o_sb)
    h3 = _matmul(o_sb, w_o, mode="nn", out_dtypes=(F32,), epilogue=_add_epilogue, extras=(h2,), name="l1_mix_out")
    h4, mlp1 = _mlp_fwd(h3, row(norm_mlp_g[1]), w1[1], w2[1], name="l1_mlp")

    sq, dh4 = _loss_head(h4, target, name="loss_head")

    dh3, dg_mlp1, dw1_1, dw2_1 = _mlp_bwd(h3, row(norm_mlp_g[1]), w1[1], w2[1], mlp1, dh4, name="l1_mlp_bwd")
    do_sb = _matmul(dh3, w_o, mode="nt", out_dtypes=(F32,), name="l1_do")
    dw_o = _matmul(o_sb, dh3, mode="tn", out_dtypes=(F32,), name="l1_dwo")
    dqn, dkn, dv = _sb_bwd(qn, kn, vb, carries, do_sb, name="l1_stickbreak_bwd")
    dqkv, dqg_t, dkg_t = _qknorm_bwd(qkv, qg_t, kg_t, dqn, dkn, dv, name="l1_qknorm_bwd")
    dw_qkv = _matmul(hn1, dqkv, mode="tn", out_dtypes=(F32,), name="l1_dwqkv")
    pin = lambda arr, tok: arr if tok is None else arr + tok[0:1, 0:1]
    tok = reached("l1_grads", dqkv, dict(odd_w_qkv=dw_qkv, odd_w_o=dw_o, mlp_w1_1=dw1_1, mlp_w2_1=dw2_1))
    dhn1 = _matmul(dqkv, w_qkv, mode="nt", out_dtypes=(F32,), name="l1_dhn", after=tok)
    dh2, dg_mix1 = _rmsnorm_bwd(h2, row(norm_mix_g[1]), dhn1, dh3, name="l1_mix_dnorm")
    tok = reached("l1_done", dh2)

    dh1, dg_mlp0, dw1_0, dw2_0 = _mlp_bwd(h1, row(norm_mlp_g[0]), w1[0], w2[0], mlp0, dh2, name="l0_mlp_bwd", after=tok)
    tok = reached("l0_mlp_grads", dh1, dict(mlp_w1_0=dw1_0, mlp_w2_0=dw2_0))
    dcat = _matmul(dh1, w_out, mode="nt", out_dtypes=(F32,), name="l0_dcat", after=tok)
    dw_out = _matmul(cat, dh1, mode="tn", out_dtypes=(F32,), name="l0_dwout")
    tok = reached("l0_dwout", dcat, dict(even_w_out=dw_out))
    dq, dk, dvr, dgate_r, dgn = _retention_bwd(proj, pin(gn_g, tok), tables, opre, sprev, dcat, name="l0_retention_bwd")
    tok = reached("l0_retention_bwd", dq)
    da, dgate_c, dconv_w, dconv_b, dln_g, dln_b = _conv_bwd(proj, conv_w, pin(row(ln_g), tok), row(ln_b), y_conv, dcat,
                                                            name="l0_conv_bwd")
    tok = reached("l0_conv_bwd", da)
    dproj = jnp.concatenate([dq, dk, dvr, dgate_r, da, dgate_c], axis=1)
    dw_in = _matmul(hn0, dproj, mode="tn", out_dtypes=(F32,), name="l0_dwin", after=tok)
    tok = reached("l0_dwin", dproj, dict(even_w_in=dw_in))
    dhn0 = _matmul(dproj, w_in, mode="nt", out_dtypes=(F32,), name="l0_dhn", after=tok)
    tok = reached("l0_dhn", dhn0)
    dh0, dg_mix0 = _rmsnorm_bwd(h0, pin(row(norm_mix_g[0]), tok), dhn0, dh1, name="l0_mix_dnorm")

    fold = lambda t: t.reshape(SB_HEADS, SB_HEAD_DIM).sum(axis=0)
    grads = dict(
        x=dh0[PAD_FRONT + N_META:],
        meta=dh0[PAD_FRONT:PAD_FRONT + N_META],
        norm_mix_g=jnp.concatenate([dg_mix0, dg_mix1], axis=0),
        norm_mlp_g=jnp.concatenate([dg_mlp0, dg_mlp1], axis=0),
        even_w_in=dw_in,
        even_ret_gn_g=dgn.reshape(RET_HEADS, RET_V_DIM),
        even_conv_w=dconv_w,
        even_conv_b=dconv_b,
        even_conv_ln_g=dln_g,
        even_conv_ln_b=dln_b,
        even_w_out=dw_out,
        odd_w_qkv=dw_qkv,
        odd_q_norm_g=fold(dqg_t)[None],
        odd_k_norm_g=fold(dkg_t)[None],
        odd_w_o=dw_o,
        mlp_w1=(dw1_0, dw1_1),
        mlp_w2=(dw2_0, dw2_1),
    )
    return sq[0, 0], grads


def _position():
    x, y, c = lax.axis_index("x"), lax.axis_index("y"), lax.axis_index("c")
    other_chips = [(1 - x, y), (x, 1 - y), (1 - x, 1 - y)]
    return x, y, c, other_chips


def _shard_of(ref, kind, s, n):
    rows, cols = ref.shape
    if kind == "col":
        return ref.at[:, pl.ds(s * (cols // n), cols // n)]
    return ref.at[pl.ds(s * (rows // n), rows // n), :]


def _half_of(ref, kind, c):
    rows, cols = ref.shape
    if kind == "col":
        return ref.at[pl.ds(c * (rows // 2), rows // 2), :]
    return ref.at[:, pl.ds(c * (cols // 2), cols // 2)]


def _remote(src, dst, send_sems, recv_sems, idx, device):
    return pltpu.make_async_remote_copy(src_ref=src, dst_ref=dst, send_sem=send_sems.at[idx], recv_sem=recv_sems.at[idx],
                                        device_id=device, device_id_type=MESH)


def _cast_into_whole(w, kind, s_arr, *, name):
    rows, cols = w.shape
    tr = _pick(rows, (256, 128))
    nb = rows // tr
    if kind == "col":
        whole, o_spec = (rows, cols * N_CHIPS), pl.BlockSpec((tr, cols), lambda i, s_ref: (i, s_ref[0]))
    else:
        whole, o_spec = (rows * N_CHIPS, cols), pl.BlockSpec((tr, cols), lambda i, s_ref: (s_ref[0] * nb + i, 0))

    def body(s_ref, w_ref, o_ref):
        o_ref[...] = w_ref[...].astype(BF16)

    return pl.pallas_call(
        body,
        out_shape=jax.ShapeDtypeStruct(whole, BF16),
        grid_spec=pltpu.PrefetchScalarGridSpec(num_scalar_prefetch=1, grid=(nb,),
                                               in_specs=[pl.BlockSpec((tr, cols), lambda i, s_ref: (i, 0))],
                                               out_specs=o_spec),
        compiler_params=_params("parallel"),
        name=name,
    )(s_arr, w)


def _allgather_weights(wholes, kinds):
    n = len(wholes)

    def body(*refs):
        ins, outs = refs[:n], refs[n:2 * n]
        send_sems, recv_sems = refs[2 * n:]
        x, y, c, chips = _position()
        me_chip = 2 * x + y
        sibling = (x, y, 1 - c)
        sends = []
        for t in range(n):
            for k, (cx, cy) in enumerate(chips):
                src = _half_of(_shard_of(ins[t], kinds[t], me_chip, N_CHIPS), kinds[t], c)
                dst = _half_of(_shard_of(outs[t], kinds[t], me_chip, N_CHIPS), kinds[t], c)
                sends.append(_remote(src, dst, send_sems, recv_sems, 6 * t + k, (cx, cy, c)))
        for cp in sends:
            cp.start()
        passed = []
        for t in range(n):
            for k, (cx, cy) in enumerate(chips):
                landed = _half_of(_shard_of(outs[t], kinds[t], 2 * cx + cy, N_CHIPS), kinds[t], c)
                _remote(landed, landed, send_sems, recv_sems, 6 * t + k, (cx, cy, c)).wait_recv()
                fwd = _remote(landed, landed, send_sems, recv_sems, 6 * t + 3 + k, sibling)
                fwd.start()
                passed.append(fwd)
        for t in range(n):
            for k, (cx, cy) in enumerate(chips):
                theirs = _half_of(_shard_of(outs[t], kinds[t], 2 * cx + cy, N_CHIPS), kinds[t], 1 - c)
                _remote(theirs, theirs, send_sems, recv_sems, 6 * t + 3 + k, sibling).wait_recv()
        for cp in sends + passed:
            cp.wait_send()

    return pl.pallas_call(
        body,
        out_shape=[jax.ShapeDtypeStruct(w.shape, BF16) for w in wholes],
        in_specs=[ANY] * n,
        out_specs=[ANY] * n,
        input_output_aliases={t: t for t in range(n)},
        scratch_shapes=[pltpu.SemaphoreType.DMA((6 * n,)), pltpu.SemaphoreType.DMA((6 * n,))],
        name="allgather_weights",
    )(*wholes)


HBM = pl.BlockSpec(memory_space=pltpu.HBM)
SEM = pl.BlockSpec(memory_space=pltpu.SEMAPHORE)
DATAFLOW = pltpu.SideEffectType.DATAFLOW_SIDE_EFFECTING
TARGETS = 6


def _gather_copies(kinds, refs, _, send_sems, recv_sems):
    x, y, c, chips = _position()
    me_chip = 2 * x + y
    sends, lands = [], []
    for t, (ref, kind) in enumerate(zip(refs, kinds)):
        mine = _half_of(_shard_of(ref, kind, me_chip, N_CHIPS), kind, c)
        for k, (cx, cy) in enumerate(chips):
            for other_core in range(2):
                j = TARGETS * t + 2 * k + other_core
                peer_c = 1 - c if other_core else c
                sends.append(_remote(mine, mine, send_sems, recv_sems, j, (cx, cy, peer_c)))
                theirs = _half_of(_shard_of(ref, kind, 2 * cx + cy, N_CHIPS), kind, peer_c)
                lands.append(_remote(theirs, theirs, send_sems, recv_sems, j, (cx, cy, peer_c)))
    return sends, lands


def _pair_swap_copies(kinds, srcs, lands, send_sems, recv_sems):
    x, y, c, _ = _position()
    sibling = (x, y, 1 - c)
    sends = [_remote(_half_of(srcs[t], kinds[t], 1 - c), lands[t], send_sems, recv_sems, t, sibling) for t in range(len(srcs))]
    arrivals = [_remote(_half_of(srcs[t], kinds[t], c), lands[t], send_sems, recv_sems, t, sibling) for t in range(len(srcs))]
    return sends, arrivals


def _chip_exchange_copies(kinds, srcs, lands, send_sems, recv_sems):
    x, y, c, chips = _position()
    sends, arrivals = [], []
    for t in range(len(srcs)):
        for k, (cx, cy) in enumerate(chips):
            src = _shard_of(srcs[t], kinds[t], 2 * cx + cy, N_CHIPS)
            sends.append(_remote(src, lands[t].at[k], send_sems, recv_sems, 3 * t + k, (cx, cy, c)))
            arrivals.append(_remote(src, lands[t].at[k], send_sems, recv_sems, 3 * t + k, (cx, cy, c)))
    return sends, arrivals


def _pair_gather_copies(kinds, srcs, lands, send_sems, recv_sems):
    x, y, c, _ = _position()
    sibling = (x, y, 1 - c)
    sends, arrivals = [], []
    for t in range(len(srcs)):
        mine, theirs = _half_of(srcs[t], kinds[t], c), _half_of(srcs[t], kinds[t], 1 - c)
        sends.append(_remote(mine, mine, send_sems, recv_sems, t, sibling))
        arrivals.append(_remote(theirs, theirs, send_sems, recv_sems, t, sibling))
    return sends, arrivals


def _copies_start(plan, n_sems, srcs, lands, follows, *, name):
    ns, n = len(srcs), len(srcs) + len(lands)

    def body(*refs):
        send_sems, recv_sems = refs[n + 1], refs[n + 2]
        thru, token = refs[n + 3:2 * n + 3], refs[2 * n + 3]
        sends, _ = plan(thru[:ns], thru[ns:], send_sems, recv_sems)
        for cp in sends:
            cp.start()
        token[...] = jnp.zeros_like(token)

    arrays = [pltpu.with_memory_space_constraint(a, pltpu.HBM) for a in list(srcs) + list(lands)]
    outs = pl.pallas_call(
        body,
        name=name,
        out_shape=(pltpu.SemaphoreType.DMA((n_sems,)), pltpu.SemaphoreType.DMA((n_sems,)),
                   *[pltpu.HBM(a.shape, a.dtype) for a in arrays], jax.ShapeDtypeStruct((8, 128), F32)),
        in_specs=(*[HBM] * n, ANY),
        out_specs=(SEM, SEM, *[HBM] * n, pl.BlockSpec(memory_space=pltpu.VMEM)),
        input_output_aliases={t: 2 + t for t in range(n)},
        compiler_params=pltpu.CompilerParams(has_side_effects=DATAFLOW),
    )(*arrays, follows)
    return outs[0], outs[1], list(outs[2:2 + ns]), list(outs[2 + ns:2 + n]), outs[2 + n]


def _copies_wait(plan, started, follows, *, name):
    send_sems, recv_sems, srcs, lands, _ = started
    ns, n = len(srcs), len(srcs) + len(lands)

    def body(*refs):
        ins, s_sems, r_sems = refs[:n], refs[n], refs[n + 1]
        sends, arrivals = plan(ins[:ns], ins[ns:], s_sems, r_sems)
        for cp in sends:
            cp.wait_send()
        for cp in arrivals:
            cp.wait_recv()

    outs = pl.pallas_call(
        body,
        name=name,
        out_shape=tuple(pltpu.HBM(a.shape, a.dtype) for a in srcs + lands),
        in_specs=(*[HBM] * n, SEM, SEM, ANY),
        out_specs=tuple([HBM] * n),
        input_output_aliases={t: t for t in range(n)},
        compiler_params=pltpu.CompilerParams(has_side_effects=DATAFLOW),
    )(*srcs, *lands, send_sems, recv_sems, follows)
    return list(outs[:ns]), list(outs[ns:])


def _allgather8(block, *, name):
    rows, cols = block.shape

    def body(in_ref, out_ref, send_sems, recv_sems, local_sem):
        x, y, c, _ = _position()
        me = 4 * x + 2 * y + c
        mine = pltpu.make_async_copy(in_ref, out_ref.at[me], local_sem)
        mine.start()
        peers = []
        for flip in range(1, N_DEV):
            fx, fy, fc = (flip >> 2) & 1, (flip >> 1) & 1, flip & 1
            peers.append(((1 - x if fx else x), (1 - y if fy else y), (1 - c if fc else c)))
        sends = [_remote(in_ref, out_ref.at[me], send_sems, recv_sems, j, peer) for j, peer in enumerate(peers)]
        for cp in sends:
            cp.start()
        for j, (px, py, pc) in enumerate(peers):
            slot = out_ref.at[4 * px + 2 * py + pc]
            _remote(slot, slot, send_sems, recv_sems, j, (px, py, pc)).wait_recv()
        for cp in sends:
            cp.wait_send()
        mine.wait()

    vmem = pl.BlockSpec(memory_space=pltpu.VMEM)
    return pl.pallas_call(
        body,
        out_shape=jax.ShapeDtypeStruct((N_DEV, rows, cols), F32),
        in_specs=[vmem],
        out_specs=vmem,
        scratch_shapes=[pltpu.SemaphoreType.DMA((N_DEV - 1,)), pltpu.SemaphoreType.DMA((N_DEV - 1,)),
                        pltpu.SemaphoreType.DMA],
        name=name,
    )(block)


def _sum8(stack, *, name):
    _, rows, cols = stack.shape

    def body(s_ref, o_ref):
        acc = s_ref[0]
        for i in range(1, N_DEV):
            acc = acc + s_ref[i]
        o_ref[...] = acc

    return pl.pallas_call(body, out_shape=jax.ShapeDtypeStruct((rows, cols), F32), name=name)(stack)


def _half_add(grad, theirs, kind, c_arr, *, name):
    rows, cols = theirs.shape
    tr = _pick(rows, (256, 128))
    nb = rows // tr
    if kind == "col":
        g_spec = pl.BlockSpec((tr, cols), lambda i, c_ref: (c_ref[0] * nb + i, 0))
    else:
        g_spec = pl.BlockSpec((tr, cols), lambda i, c_ref: (i, c_ref[0]))
    t_spec = pl.BlockSpec((tr, cols), lambda i, c_ref: (i, 0))

    def body(c_ref, g_ref, t_ref, o32_ref, o16_ref):
        tot = g_ref[...] + t_ref[...]
        o32_ref[...] = tot
        o16_ref[...] = tot.astype(BF16)

    return pl.pallas_call(
        body,
        out_shape=[jax.ShapeDtypeStruct((rows, cols), F32), jax.ShapeDtypeStruct((rows, cols), BF16)],
        grid_spec=pltpu.PrefetchScalarGridSpec(num_scalar_prefetch=1, grid=(nb,), in_specs=[g_spec, t_spec],
                                               out_specs=[t_spec, t_spec]),
        compiler_params=_params("parallel"),
        name=name,
    )(c_arr, grad, theirs)


def _shard_sum(part32, recv, kind, sc_arr, *, name):
    _, rows, cols = recv.shape
    tr = _pick(rows, (256, 128))
    nb = rows // tr
    if kind == "col":
        whole = (2 * rows, cols)
        p_spec = pl.BlockSpec((tr, cols), lambda i, sc: (i, sc[0]))
        o_spec = pl.BlockSpec((tr, cols), lambda i, sc: (sc[1] * nb + i, 0))
    else:
        whole = (rows, 2 * cols)
        p_spec = pl.BlockSpec((tr, cols), lambda i, sc: (sc[0] * nb + i, 0))
        o_spec = pl.BlockSpec((tr, cols), lambda i, sc: (i, sc[1]))
    r_spec = pl.BlockSpec((3, tr, cols), lambda i, sc: (0, i, 0))

    def body(sc_ref, p_ref, r_ref, o_ref):
        acc = p_ref[...]
        for k in range(3):
            acc = acc + r_ref[k].astype(F32)
        o_ref[...] = acc

    return pl.pallas_call(
        body,
        out_shape=jax.ShapeDtypeStruct(whole, F32),
        grid_spec=pltpu.PrefetchScalarGridSpec(num_scalar_prefetch=1, grid=(nb,), in_specs=[p_spec, r_spec],
                                               out_specs=o_spec),
        compiler_params=_params("parallel"),
        name=name,
    )(sc_arr, part32, recv)


def _adamw(w, g, m, v, *, name):
    rows, cols = w.shape
    tr = _pick(rows, (256, 128)) if rows * cols > 64 * 1024 else rows

    def body(w_ref, g_ref, m_ref, v_ref, d_ref, nm_ref, nv_ref):
        gv = g_ref[...]
        nm = ADAM_B1 * m_ref[...] + (1.0 - ADAM_B1) * gv
        nv = ADAM_B2 * v_ref[...] + (1.0 - ADAM_B2) * jnp.square(gv)
        m_hat = nm / (1.0 - ADAM_B1 ** ADAM_STEP)
        v_hat = nv / (1.0 - ADAM_B2 ** ADAM_STEP)
        d_ref[...] = -ADAM_LR * (m_hat / (jnp.sqrt(v_hat) + ADAM_EPS) + ADAM_WD * w_ref[...])
        nm_ref[...] = nm
        nv_ref[...] = nv

    spec = pl.BlockSpec((tr, cols), lambda i: (i, 0))
    return pl.pallas_call(
        body,
        out_shape=[jax.ShapeDtypeStruct((rows, cols), F32)] * 3,
        grid=(rows // tr,),
        in_specs=[spec] * 4,
        out_specs=[spec] * 3,
        compiler_params=_params("parallel"),
        name=name,
    )(w, g, m, v)


BIG = ("even_w_in", "odd_w_qkv", "mlp_w1_0", "mlp_w1_1", "even_w_out", "odd_w_o", "mlp_w2_0", "mlp_w2_1")
BIG_KIND = ("col", "col", "col", "col", "row", "row", "row", "row")


class _TravellingReduction:
    def __init__(self, tag, names, kinds, c_arr, sc_arr):
        self.tag, self.names, self.kinds, self.c_arr, self.sc_arr = tag, names, kinds, c_arr, sc_arr
        self.swap = functools.partial(_pair_swap_copies, kinds)
        self.exchange = functools.partial(_chip_exchange_copies, kinds)
        self.gather = functools.partial(_pair_gather_copies, kinds)

    def pair_swap_start(self, grads, follows):
        half = lambda g, kind: (g.shape[0] // 2, g.shape[1]) if kind == "col" else (g.shape[0], g.shape[1] // 2)
        lands = [lax.empty(half(g, k), F32) for g, k in zip(grads, self.kinds)]
        self.started = _copies_start(self.swap, len(grads), grads, lands, follows, name=f"reduce_{self.tag}_pair_start")

    def pair_swap_finish(self, after):
        grads, theirs = _copies_wait(self.swap, self.started, after, name=f"reduce_{self.tag}_pair_wait")
        self.sums = [_half_add(g, th, k, self.c_arr, name="pair_sum_" + n)
                     for g, th, k, n in zip(grads, theirs, self.kinds, self.names)]

    def chips_start(self, follows):
        parts = [s16 for _, s16 in self.sums]
        piece = lambda p, kind: (3, p.shape[0], p.shape[1] // N_CHIPS) if kind == "col" else (3, p.shape[0] // N_CHIPS, p.shape[1])
        lands = [lax.empty(piece(p, k), BF16) for p, k in zip(parts, self.kinds)]
        self.started = _copies_start(self.exchange, 3 * len(parts), parts, lands, follows,
                                     name=f"reduce_{self.tag}_chips_start")

    def chips_finish(self, after):
        _, recv = _copies_wait(self.exchange, self.started, after, name=f"reduce_{self.tag}_chips_wait")
        self.halves = [_shard_sum(s32, r, k, self.sc_arr, name="chip_sum_" + n)
                       for (s32, _), r, k, n in zip(self.sums, recv, self.kinds, self.names)]

    def pair_gather_start(self, follows):
        self.started = _copies_start(self.gather, len(self.halves), self.halves, [], follows,
                                     name=f"reduce_{self.tag}_gather_start")

    def pair_gather_finish(self, after):
        shards, _ = _copies_wait(self.gather, self.started, after, name=f"reduce_{self.tag}_gather_wait")
        return dict(zip(self.names, shards))
SUBLANES = 8


def _pack_rows(parts, width):
    padded, offsets, r0 = [], [], 0
    for t in parts:
        rows = -(-t.shape[0] // SUBLANES) * SUBLANES
        padded.append(jnp.pad(t, ((0, rows - t.shape[0]), (0, width - t.shape[1]))))
        offsets.append(r0)
        r0 += rows
    return jnp.concatenate(padded, axis=0), offsets


def kernel(x, meta, norm_mix_g, norm_mlp_g, even_w_in, even_ret_gn_g, even_conv_w, even_conv_b, even_conv_ln_g, even_conv_ln_b, even_w_out, odd_w_qkv, odd_q_norm_g, odd_k_norm_g, odd_w_o, mlp_w1, mlp_w2, loss_target, m_meta, m_norm_mix_g, m_norm_mlp_g, m_even_w_in, m_even_ret_gn_g, m_even_conv_w, m_even_conv_b, m_even_conv_ln_g, m_even_conv_ln_b, m_even_w_out, m_odd_w_qkv, m_odd_q_norm_g, m_odd_k_norm_g, m_odd_w_o, m_mlp_w1, m_mlp_w2, v_meta, v_norm_mix_g, v_norm_mlp_g, v_even_w_in, v_even_ret_gn_g, v_even_conv_w, v_even_conv_b, v_even_conv_ln_g, v_even_conv_ln_b, v_even_w_out, v_odd_w_qkv, v_odd_q_norm_g, v_odd_k_norm_g, v_odd_w_o, v_mlp_w1, v_mlp_w2):
    d = D_MODEL
    xi, yi, ci = lax.axis_index("x"), lax.axis_index("y"), lax.axis_index("c")
    chip = 2 * xi + yi
    c_arr = jnp.reshape(ci, (1,)).astype(jnp.int32)
    s_arr = jnp.reshape(chip, (1,)).astype(jnp.int32)

    def split_big(w_in, w_qkv, w1, w_out, w_o, w2):
        return dict(zip(BIG, (w_in[0], w_qkv[0], w1[0], w1[1], w_out[0], w_o[0], w2[0], w2[1])))

    w_big = split_big(even_w_in, odd_w_qkv, mlp_w1, even_w_out, odd_w_o, mlp_w2)
    m_big = split_big(m_even_w_in, m_odd_w_qkv, m_mlp_w1, m_even_w_out, m_odd_w_o, m_mlp_w2)
    v_big = split_big(v_even_w_in, v_odd_w_qkv, v_mlp_w1, v_even_w_out, v_odd_w_o, v_mlp_w2)

    placed = {n: _cast_into_whole(w_big[n], k, s_arr, name="cast_" + n) for n, k in zip(BIG, BIG_KIND)}
    kind_of = dict(zip(BIG, BIG_KIND))
    (w_in_full,) = _allgather_weights([placed["even_w_in"]], [kind_of["even_w_in"]])
    packed, (r_meta, r_conv, r_gn) = _pack_rows([meta, even_conv_w[0], even_ret_gn_g[0]], d // N_CHIPS)
    gathered = _allgather8(packed, name="allgather_small_params")[0::2]
    groups = dict(l0=("even_w_out", "mlp_w1_0", "mlp_w2_0"), qkv=("odd_w_qkv",), l1=("odd_w_o", "mlp_w1_1", "mlp_w2_1"))
    in_flight, follows = {}, gathered[0, 0:1, 0:1] + w_in_full[0:1, 0:1].astype(F32)
    for group, names in groups.items():
        plan = functools.partial(_gather_copies, [kind_of[n] for n in names])
        in_flight[group] = (plan, _copies_start(plan, TARGETS * len(names), [placed[n] for n in names], [], follows,
                                                name="gather_" + group + "_start"))
        follows = in_flight[group][1][-1]
    started = follows[0:1, 0:1]

    def later(group, after):
        plan, state = in_flight[group]
        return _copies_wait(plan, state, after, name="gather_" + group + "_wait")[0]

    sc_arr = jnp.concatenate([s_arr, c_arr])
    early = ("odd_w_qkv", "odd_w_o", "mlp_w1_1", "mlp_w2_1"), ("mlp_w1_0", "mlp_w2_0"), ("even_w_out",), ("even_w_in",)
    red_l1, red_m0, red_o0, red_i0 = (_TravellingReduction(tag, names, [kind_of[n] for n in names], c_arr, sc_arr)
                                      for tag, names in zip(("l1", "m0", "o0", "i0"), early))
    grad_big = {}

    def reached(point, after, grads=None):
        if point == "l1_grads":
            red_l1.pair_swap_start([grads[n] for n in red_l1.names], after)
            return red_l1.started[-1]
        if point == "l1_done":
            red_l1.pair_swap_finish(after)
            red_l1.chips_start(after)
            return red_l1.started[-1]
        if point == "l0_mlp_grads":
            red_m0.pair_swap_start([grads[n] for n in red_m0.names], after)
            return red_m0.started[-1]
        if point == "l0_dwout":
            red_m0.pair_swap_finish(after)
            red_m0.chips_start(after)
            red_o0.pair_swap_start([grads[n] for n in red_o0.names], red_m0.started[-1])
            return red_o0.started[-1]
        if point == "l0_retention_bwd":
            red_l1.chips_finish(after)
            red_l1.pair_gather_start(after)
            red_o0.pair_swap_finish(after)
            red_o0.chips_start(red_l1.started[-1])
            return red_o0.started[-1]
        if point == "l0_conv_bwd":
            red_m0.chips_finish(after)
            red_m0.pair_gather_start(after)
            grad_big.update(red_l1.pair_gather_finish(after))
            red_o0.chips_finish(after)
            red_o0.pair_gather_start(red_m0.started[-1])
            return red_o0.started[-1]
        if point == "l0_dwin":
            grad_big.update(red_m0.pair_gather_finish(after))
            grad_big.update(red_o0.pair_gather_finish(after))
            red_i0.pair_swap_start([grads[n] for n in red_i0.names], after)
            return red_i0.started[-1]
        if point == "l0_dhn":
            red_i0.pair_swap_finish(after)
            red_i0.chips_start(after)
            return red_i0.started[-1]
        return None

    across = lambda r0, rows, width: jnp.concatenate([gathered[s, r0:r0 + rows, 0:width] for s in range(N_CHIPS)], axis=1)
    meta_full = across(r_meta, N_META, d // N_CHIPS) + started
    conv_w_full = across(r_conv, CONV_WIDTH, d // N_CHIPS)
    gn_full = across(r_gn, RET_HEADS, RET_V_DIM // N_CHIPS)

    sq, g = _local_step(
        x[0], loss_target[0], meta_full, norm_mix_g, norm_mlp_g, w_in_full, gn_full, conv_w_full,
        even_conv_b[0], even_conv_ln_g[0], even_conv_ln_b[0], odd_q_norm_g[0], odd_k_norm_g[0], later, reached)
    red_i0.chips_finish(g["x"])
    red_i0.pair_gather_start(g["x"])
    grad_big.update(red_i0.pair_gather_finish(g["x"]))
    loss = lax.psum(0.5 * sq / d, ("x", "y", "c"))

    small_names = ("norm_mix_g", "norm_mlp_g", "even_conv_b", "even_conv_ln_g", "even_conv_ln_b", "odd_q_norm_g",
                   "odd_k_norm_g", "meta", "even_conv_w", "even_ret_gn_g")
    pack, offsets = _pack_rows([g[n] for n in small_names], d)
    summed = _sum8(_allgather8(pack, name="allgather_small_grads"), name="sum_small_grads")
    small = {n: summed[r0:r0 + g[n].shape[0], 0:g[n].shape[1]] for n, r0 in zip(small_names, offsets)}
    for n in ("meta", "even_conv_w", "even_ret_gn_g"):
        width = small[n].shape[1] // N_CHIPS
        small[n] = lax.dynamic_slice_in_dim(small[n], chip * width, width, axis=1)

    stacked = dict(mlp_w1=(mlp_w1, m_mlp_w1, v_mlp_w1), mlp_w2=(mlp_w2, m_mlp_w2, v_mlp_w2))
    flat = lambda t: t.reshape(-1, t.shape[-1])
    for n in stacked:
        grad_big[n] = jnp.concatenate([grad_big.pop(n + "_0"), grad_big.pop(n + "_1")], axis=0)
    upd = {n: _adamw(w_big[n], grad_big[n], m_big[n], v_big[n], name="adamw_" + n) for n in BIG if n in grad_big}
    upd.update({n: _adamw(flat(w), grad_big[n], flat(m), flat(v), name="adamw_" + n) for n, (w, m, v) in stacked.items()})

    def join(name, idx, lead):
        t = upd[name][idx] if idx >= 0 else grad_big[name]
        if name in stacked:
            return t.reshape(stacked[name][0].shape)
        return t[None] if lead else t

    small_w = dict(meta=meta, norm_mix_g=norm_mix_g, norm_mlp_g=norm_mlp_g, even_ret_gn_g=even_ret_gn_g[0],
                   even_conv_w=even_conv_w[0], even_conv_b=even_conv_b, even_conv_ln_g=even_conv_ln_g,
                   even_conv_ln_b=even_conv_ln_b, odd_q_norm_g=odd_q_norm_g, odd_k_norm_g=odd_k_norm_g)
    small_m = dict(meta=m_meta, norm_mix_g=m_norm_mix_g, norm_mlp_g=m_norm_mlp_g, even_ret_gn_g=m_even_ret_gn_g[0],
                   even_conv_w=m_even_conv_w[0], even_conv_b=m_even_conv_b, even_conv_ln_g=m_even_conv_ln_g,
                   even_conv_ln_b=m_even_conv_ln_b, odd_q_norm_g=m_odd_q_norm_g, odd_k_norm_g=m_odd_k_norm_g)
    small_v = dict(meta=v_meta, norm_mix_g=v_norm_mix_g, norm_mlp_g=v_norm_mlp_g, even_ret_gn_g=v_even_ret_gn_g[0],
                   even_conv_w=v_even_conv_w[0], even_conv_b=v_even_conv_b, even_conv_ln_g=v_even_conv_ln_g,
                   even_conv_ln_b=v_even_conv_ln_b, odd_q_norm_g=v_odd_q_norm_g, odd_k_norm_g=v_odd_k_norm_g)
    small_upd = {n: _adamw(small_w[n], small[n], small_m[n], small_v[n], name="adamw_" + n) for n in small_w}
    leading = ("even_ret_gn_g", "even_conv_w")

    order = ("meta", "norm_mix_g", "norm_mlp_g", "even_w_in", "even_ret_gn_g", "even_conv_w", "even_conv_b",
             "even_conv_ln_g", "even_conv_ln_b", "even_w_out", "odd_w_qkv", "odd_q_norm_g", "odd_k_norm_g", "odd_w_o",
             "mlp_w1", "mlp_w2")
    big_lead = ("even_w_in", "even_w_out", "odd_w_qkv", "odd_w_o")

    def leaf(name, idx):
        if name in small_w:
            t = small_upd[name][idx] if idx >= 0 else small[name]
            return t[None] if name in leading else t
        return join(name, idx, name in big_lead)

    outs = [loss, g["x"][None]]
    for idx in (-1, 0, 1, 2):
        outs += [leaf(n, idx) for n in order]
    return tuple(outs)
```

```python
import functools

import jax
import jax.numpy as jnp
from jax import lax
from jax.experimental import pallas as pl
from jax.experimental.pallas import tpu as pltpu

F32 = jnp.float32
BF16 = jnp.bfloat16

D_MODEL = 1024
N_META = 16
CHUNK = 128
PAD_FRONT = (-N_META) % CHUNK
RET_HEADS = 4
RET_QK_DIM = 128
RET_V_DIM = 256
RET_QK_W = RET_HEADS * RET_QK_DIM
RET_V_W = RET_HEADS * RET_V_DIM
CONV_WIDTH = 31
CONV_HALO = 32
RET_DECAY_OFFSET = 5.0
ROPE_BASE = 10000.0
SB_HEADS = 16
SB_HEAD_DIM = 64
D_FF = 4 * D_MODEL
EPS = 1e-6
ADAM_LR = 0.001
ADAM_B1 = 0.9
ADAM_B2 = 0.999
ADAM_EPS = 1e-08
ADAM_WD = 0.01
ADAM_STEP = 10

N_CHIPS = 4
N_DEV = 8
VMEM_LIMIT = 56 * 1024 * 1024
MESH = pl.DeviceIdType.MESH
ANY = pl.BlockSpec(memory_space=pl.ANY)


def _params(*sem):
    return pltpu.CompilerParams(dimension_semantics=sem, vmem_limit_bytes=VMEM_LIMIT)


def _pick(n, cands):
    for c in cands:
        if n % c == 0:
            return c
    return n


def _sigmoid(x):
    return 1.0 / (1.0 + jnp.exp(-x))


def _dot(a, b):
    return lax.dot_general(a, b, (((1,), (0,)), ((), ())), preferred_element_type=F32)


def _dot_nt(a, b):
    return lax.dot_general(a, b, (((1,), (1,)), ((), ())), preferred_element_type=F32)


def _dot_tn(a, b):
    return lax.dot_general(a, b, (((0,), (0,)), ((), ())), preferred_element_type=F32)


def _split_dot(x, m):
    hi = x.astype(BF16)
    lo = (x - hi.astype(F32)).astype(BF16)
    return _dot(hi, m) + _dot(lo, m)


def _matmul(a, b, *, mode, out_dtypes, epilogue=None, extras=(), name, after=None):
    if mode == "nn":
        (m, k), (k2, n) = a.shape, b.shape
    elif mode == "nt":
        (m, k), (n, k2) = a.shape, b.shape
    else:
        (k, m), (k2, n) = a.shape, b.shape
    assert k == k2, (a.shape, b.shape, mode)
    tm = _pick(m, (1056, 1024, 768, 512, 384, 256, 128, 96))
    tn = _pick(n, (1024, 768, 512, 256, 128))
    tk = _pick(k, (1056, 1024, 768, 512, 384, 256, 128, 96))
    nk = k // tk
    dot = {"nn": _dot, "nt": _dot_nt, "tn": _dot_tn}[mode]
    n_extra, n_out = len(extras), len(out_dtypes)
    n_after = 0 if after is None else 1
    if epilogue is None:
        epilogue = lambda acc: (acc,)

    def body(a_ref, b_ref, *rest):
        extra_refs = rest[:n_extra]
        out_refs = rest[n_extra + n_after:n_extra + n_after + n_out]
        part = dot(a_ref[...].astype(BF16), b_ref[...].astype(BF16))

        def finish(acc):
            res = epilogue(acc, *[r[...] for r in extra_refs])
            for o_ref, r in zip(out_refs, res):
                o_ref[...] = r.astype(o_ref.dtype)

        if nk == 1:
            finish(part)
        else:
            acc_ref = rest[-1]
            kk = pl.program_id(2)

            @pl.when(kk == 0)
            def _():
                acc_ref[...] = part

            @pl.when(kk > 0)
            def _():
                acc_ref[...] += part

            @pl.when(kk == nk - 1)
            def _():
                finish(acc_ref[...])

    if mode == "nn":
        a_spec = pl.BlockSpec((tm, tk), lambda i, j, kk: (i, kk))
        b_spec = pl.BlockSpec((tk, tn), lambda i, j, kk: (kk, j))
    elif mode == "nt":
        a_spec = pl.BlockSpec((tm, tk), lambda i, j, kk: (i, kk))
        b_spec = pl.BlockSpec((tn, tk), lambda i, j, kk: (j, kk))
    else:
        a_spec = pl.BlockSpec((tk, tm), lambda i, j, kk: (kk, i))
        b_spec = pl.BlockSpec((tk, tn), lambda i, j, kk: (kk, j))
    o_spec = pl.BlockSpec((tm, tn), lambda i, j, kk: (i, j))
    outs = pl.pallas_call(
        body,
        out_shape=[jax.ShapeDtypeStruct((m, n), dt) for dt in out_dtypes],
        grid=(m // tm, n // tn, nk),
        in_specs=[a_spec, b_spec] + [o_spec] * n_extra + [ANY] * n_after,
        out_specs=[o_spec] * n_out,
        scratch_shapes=[pltpu.VMEM((tm, tn), F32)] if nk > 1 else [],
        compiler_params=_params("parallel", "parallel", "arbitrary"),
        name=name,
    )(a, b, *extras, *([] if after is None else [after]))
    return outs[0] if n_out == 1 else outs


def _add_epilogue(acc, res):
    return (res + acc,)


def _rmsnorm_fwd(x, g, *, name):
    p, d = x.shape
    rows = _pick(p, (384, 128, 96))

    def body(x_ref, g_ref, o_ref):
        xv = x_ref[...]
        r = lax.rsqrt(jnp.mean(xv * xv, axis=-1, keepdims=True) + EPS)
        o_ref[...] = (xv * r * g_ref[...]).astype(o_ref.dtype)

    return pl.pallas_call(
        body,
        out_shape=jax.ShapeDtypeStruct((p, d), BF16),
        grid=(p // rows,),
        in_specs=[pl.BlockSpec((rows, d), lambda i: (i, 0)), pl.BlockSpec((1, d), lambda i: (0, 0))],
        out_specs=pl.BlockSpec((rows, d), lambda i: (i, 0)),
        compiler_params=_params("parallel"),
        name=name,
    )(x, g)


def _rmsnorm_bwd(x, g, dy, dres, *, name):
    p, d = x.shape
    rows = _pick(p, (384, 128, 96))

    def body(x_ref, g_ref, dy_ref, dres_ref, dx_ref, dg_ref):
        xv = x_ref[...]
        r = lax.rsqrt(jnp.mean(xv * xv, axis=-1, keepdims=True) + EPS)
        dyv = dy_ref[...]
        gdy = dyv * g_ref[...]
        proj = jnp.mean(xv * gdy, axis=-1, keepdims=True)
        dx_ref[...] = dres_ref[...] + r * gdy - xv * (r * r * r) * proj
        part = jnp.sum(dyv * xv * r, axis=0, keepdims=True)

        @pl.when(pl.program_id(0) == 0)
        def _():
            dg_ref[...] = part

        @pl.when(pl.program_id(0) > 0)
        def _():
            dg_ref[...] += part

    row_spec = pl.BlockSpec((rows, d), lambda i: (i, 0))
    vec_spec = pl.BlockSpec((1, d), lambda i: (0, 0))
    return pl.pallas_call(
        body,
        out_shape=[jax.ShapeDtypeStruct((p, d), F32), jax.ShapeDtypeStruct((1, d), F32)],
        grid=(p // rows,),
        in_specs=[row_spec, vec_spec, row_spec, row_spec],
        out_specs=[row_spec, vec_spec],
        compiler_params=_params("arbitrary"),
        name=name,
    )(x, g, dy, dres)


def _mlp_fwd(h, g, w1, w2, *, name):
    hn = _rmsnorm_fwd(h, g, name=name + "_norm")

    def act(acc):
        r = jnp.maximum(acc, 0.0)
        return acc, r * r

    z, a2 = _matmul(hn, w1, mode="nn", out_dtypes=(F32, BF16), epilogue=act, name=name + "_up")
    out = _matmul(a2, w2, mode="nn", out_dtypes=(F32,), epilogue=_add_epilogue, extras=(h,), name=name + "_down")
    return out, (hn, z, a2)


def _mlp_bwd(h, g, w1, w2, saved, dout, *, name, after=None):
    hn, z, a2 = saved

    def dact(acc, zt):
        return (acc * (2.0 * jnp.maximum(zt, 0.0)),)

    dz = _matmul(dout, w2, mode="nt", out_dtypes=(BF16,), epilogue=dact, extras=(z,), name=name + "_dz", after=after)
    dw2 = _matmul(a2, dout, mode="tn", out_dtypes=(F32,), name=name + "_dw2")
    dw1 = _matmul(hn, dz, mode="tn", out_dtypes=(F32,), name=name + "_dw1")
    dhn = _matmul(dz, w1, mode="nt", out_dtypes=(F32,), name=name + "_dhn")
    dh, dg = _rmsnorm_bwd(h, g, dhn, dout, name=name + "_dnorm")
    return dh, dg, dw1, dw2


def _retention_tables(p):
    half = RET_QK_DIM // 2
    inv_freq = ROPE_BASE ** (-jnp.arange(half, dtype=F32) / half)
    ang = jnp.arange(p, dtype=F32)[:, None] * inv_freq[None, :]
    cos, sin = jnp.cos(ang), jnp.sin(ang)
    cosf = jnp.concatenate([cos, cos], axis=1)
    sins = jnp.concatenate([-sin, sin], axis=1)
    log_g = jnp.log1p(-jnp.exp2(-RET_DECAY_OFFSET - jnp.arange(RET_HEADS, dtype=F32)))
    idx = jnp.arange(CHUNK, dtype=F32)
    diff = idx[:, None] - idx[None, :]
    inner = jnp.where(diff[None] >= 0, jnp.exp(jnp.maximum(diff, 0.0)[None] * log_g[:, None, None]), 0.0)
    kdec = jnp.exp((CHUNK - 1 - idx)[None, :] * log_g[:, None])
    qdec = jnp.exp((idx + 1.0)[None, :] * log_g[:, None])
    cdec = jnp.exp(CHUNK * log_g)
    kdec = jnp.broadcast_to(kdec[:, :, None], (RET_HEADS, CHUNK, RET_QK_DIM))
    qdec = jnp.broadcast_to(qdec[:, :, None], (RET_HEADS, CHUNK, RET_QK_DIM))
    cdec = jnp.broadcast_to(cdec[:, None, None], (RET_HEADS, RET_QK_DIM, RET_V_DIM))
    return cosf, sins, inner, kdec, qdec, cdec


def _rot(x, cosf, sins):
    return x * cosf + pltpu.roll(x, RET_QK_DIM // 2, 1) * sins


def _rot_bwd(dy, cosf, sins):
    return dy * cosf + pltpu.roll(dy * sins, RET_QK_DIM // 2, 1)


def _ret_in_specs(chunk_of):
    q_spec = pl.BlockSpec((CHUNK, RET_QK_W), lambda s: (chunk_of(s), 0))
    k_spec = pl.BlockSpec((CHUNK, RET_QK_W), lambda s: (chunk_of(s), 1))
    v_spec = pl.BlockSpec((CHUNK, RET_V_W), lambda s: (chunk_of(s), 1))
    g_spec = pl.BlockSpec((CHUNK, RET_V_W), lambda s: (chunk_of(s), 2))
    rope_spec = pl.BlockSpec((CHUNK, RET_QK_DIM), lambda s: (chunk_of(s), 0))
    whole = lambda *shape: pl.BlockSpec(shape, lambda s: (0,) * len(shape))
    head_sq = whole(RET_HEADS, CHUNK, CHUNK)
    head_qk = whole(RET_HEADS, CHUNK, RET_QK_DIM)
    head_st = whole(RET_HEADS, RET_QK_DIM, RET_V_DIM)
    gam_spec = whole(RET_HEADS, 1, RET_V_DIM)
    return [q_spec, k_spec, v_spec, g_spec, rope_spec, rope_spec, head_sq, head_qk, head_qk, head_st, gam_spec]


def _ret_head_views(h, qk_refs, v_refs, head_refs):
    qk = pl.ds(h * RET_QK_DIM, RET_QK_DIM)
    vv = pl.ds(h * RET_V_DIM, RET_V_DIM)
    return [r.at[:, qk] for r in qk_refs], [r.at[:, vv] for r in v_refs], [r.at[h] for r in head_refs]


def _retention_fwd(proj, gn_g, tables, *, name):
    p = proj.shape[0]
    n_chunks = p // CHUNK
    scale = RET_QK_DIM ** -0.5

    def body(q_ref, k_ref, v_ref, g_ref, cos_ref, sin_ref, inner_ref, kdec_ref, qdec_ref, cdec_ref, gam_ref,
             og_ref, opre_ref, sprev_ref, s_scr):
        @pl.when(pl.program_id(0) == 0)
        def _():
            s_scr[...] = jnp.zeros_like(s_scr)

        for h in range(RET_HEADS):
            (q_h, k_h), (v_h, g_h, og_h, opre_h), tabs = _ret_head_views(
                h, (q_ref, k_ref), (v_ref, g_ref, og_ref, opre_ref),
                (inner_ref, kdec_ref, qdec_ref, cdec_ref, gam_ref, sprev_ref, s_scr))
            one_head(q_h, k_h, v_h, g_h, cos_ref, sin_ref, *tabs[:5], og_h, opre_h, *tabs[5:])

    def one_head(q_ref, k_ref, v_ref, g_ref, cos_ref, sin_ref, inner_ref, kdec_ref, qdec_ref, cdec_ref, gam_ref,
                 og_ref, opre_ref, sprev_ref, s_scr):
        cosf, sins = cos_ref[...], sin_ref[...]
        qr = _rot(q_ref[...], cosf, sins)
        kr = _rot(k_ref[...], cosf, sins) * scale
        vb = v_ref[...].astype(BF16)
        scores = _dot_nt(qr.astype(BF16), kr.astype(BF16)) * inner_ref[...]
        state = s_scr[...]
        sprev_ref[...] = state
        o = _dot(scores.astype(BF16), vb) + _dot((qr * qdec_ref[...]).astype(BF16), state.astype(BF16))
        kd = kr * kdec_ref[...]
        s_scr[...] = cdec_ref[...] * state + _dot(kd.T.astype(BF16), vb)
        opre_ref[...] = o
        mu = jnp.mean(o, axis=-1, keepdims=True)
        oc = o - mu
        var = jnp.mean(oc * oc, axis=-1, keepdims=True)
        on = oc * lax.rsqrt(var + EPS) * gam_ref[...]
        gv = g_ref[...]
        og_ref[...] = (gv * _sigmoid(gv) * on).astype(og_ref.dtype)

    chunk_of = lambda s: s
    out_v = pl.BlockSpec((CHUNK, RET_V_W), lambda s: (s, 0))
    return pl.pallas_call(
        body,
        out_shape=[
            jax.ShapeDtypeStruct((p, RET_V_W), BF16),
            jax.ShapeDtypeStruct((p, RET_V_W), F32),
            jax.ShapeDtypeStruct((RET_HEADS, n_chunks, RET_QK_DIM, RET_V_DIM), F32),
        ],
        grid=(n_chunks,),
        in_specs=_ret_in_specs(chunk_of),
        out_specs=[out_v, out_v, pl.BlockSpec((RET_HEADS, None, RET_QK_DIM, RET_V_DIM), lambda s: (0, s, 0, 0))],
        scratch_shapes=[pltpu.VMEM((RET_HEADS, RET_QK_DIM, RET_V_DIM), F32)],
        compiler_params=_params("arbitrary"),
        name=name,
    )(proj, proj, proj, proj, *tables, gn_g.reshape(RET_HEADS, 1, RET_V_DIM))


def _retention_bwd(proj, gn_g, tables, opre, sprev, dog, *, name):
    p = proj.shape[0]
    n_chunks = p // CHUNK
    scale = RET_QK_DIM ** -0.5

    def body(q_ref, k_ref, v_ref, g_ref, cos_ref, sin_ref, inner_ref, kdec_ref, qdec_ref, cdec_ref, gam_ref,
             opre_ref, sprev_ref, dog_ref, dq_ref, dk_ref, dv_ref, dg_ref, dgam_ref, ds_scr):
        first = pl.program_id(0) == 0

        @pl.when(first)
        def _():
            ds_scr[...] = jnp.zeros_like(ds_scr)

        for h in range(RET_HEADS):
            (q_h, k_h, dq_h, dk_h), (v_h, g_h, opre_h, dog_h, dv_h, dg_h), tabs = _ret_head_views(
                h, (q_ref, k_ref, dq_ref, dk_ref), (v_ref, g_ref, opre_ref, dog_ref, dv_ref, dg_ref),
                (inner_ref, kdec_ref, qdec_ref, cdec_ref, gam_ref, sprev_ref, dgam_ref, ds_scr))
            one_head(first, q_h, k_h, v_h, g_h, cos_ref, sin_ref, *tabs[:5], opre_h, tabs[5], dog_h,
                     dq_h, dk_h, dv_h, dg_h, tabs[6], tabs[7])

    def one_head(first, q_ref, k_ref, v_ref, g_ref, cos_ref, sin_ref, inner_ref, kdec_ref, qdec_ref, cdec_ref, gam_ref,
                 opre_ref, sprev_ref, dog_ref, dq_ref, dk_ref, dv_ref, dg_ref, dgam_ref, ds_scr):
        cosf, sins = cos_ref[...], sin_ref[...]
        qr = _rot(q_ref[...], cosf, sins)
        kr = _rot(k_ref[...], cosf, sins) * scale
        qb, kb = qr.astype(BF16), kr.astype(BF16)
        vb = v_ref[...].astype(BF16)
        inner = inner_ref[...]
        qdec, kdec = qdec_ref[...], kdec_ref[...]
        state_b = sprev_ref[...].astype(BF16)
        o = opre_ref[...]
        mu = jnp.mean(o, axis=-1, keepdims=True)
        oc = o - mu
        rstd = lax.rsqrt(jnp.mean(oc * oc, axis=-1, keepdims=True) + EPS)
        xhat = oc * rstd
        gam = gam_ref[...]
        on = xhat * gam
        gv = g_ref[...]
        sig = _sigmoid(gv)
        dogv = dog_ref[...]
        dg_ref[...] = (dogv * on * sig * (1.0 + gv * (1.0 - sig))).astype(dg_ref.dtype)
        don = dogv * gv * sig
        dgam_part = jnp.sum(don * xhat, axis=0, keepdims=True)

        @pl.when(first)
        def _():
            dgam_ref[...] = dgam_part

        @pl.when(jnp.logical_not(first))
        def _():
            dgam_ref[...] += dgam_part

        dxhat = don * gam
        do = rstd * (dxhat - jnp.mean(dxhat, axis=-1, keepdims=True)
                     - xhat * jnp.mean(dxhat * xhat, axis=-1, keepdims=True))
        dob = do.astype(BF16)
        scores_b = (_dot_nt(qb, kb) * inner).astype(BF16)
        da = (_dot_nt(dob, vb) * inner).astype(BF16)
        dv = _dot(scores_b.astype(F32).T.astype(BF16), dob)
        dqr = _dot(da, kb)
        dkr = _dot(da.astype(F32).T.astype(BF16), qb)
        dqr += _dot_nt(dob, state_b) * qdec
        ds_local = _dot((qr * qdec).T.astype(BF16), dob)
        gstate = ds_scr[...]
        gb = gstate.astype(BF16)
        kd_b = (kr * kdec).astype(BF16)
        dkr += _dot_nt(vb, gb) * kdec
        dv += _dot(kd_b, gb)
        ds_scr[...] = cdec_ref[...] * gstate + ds_local
        dq_ref[...] = _rot_bwd(dqr, cosf, sins).astype(dq_ref.dtype)
        dk_ref[...] = _rot_bwd(dkr * scale, cosf, sins).astype(dk_ref.dtype)
        dv_ref[...] = dv.astype(dv_ref.dtype)

    chunk_of = lambda s: n_chunks - 1 - s
    blk_v = pl.BlockSpec((CHUNK, RET_V_W), lambda s: (chunk_of(s), 0))
    blk_qk = pl.BlockSpec((CHUNK, RET_QK_W), lambda s: (chunk_of(s), 0))
    st_spec = pl.BlockSpec((RET_HEADS, None, RET_QK_DIM, RET_V_DIM), lambda s: (0, chunk_of(s), 0, 0))
    return pl.pallas_call(
        body,
        out_shape=[
            jax.ShapeDtypeStruct((p, RET_QK_W), BF16),
            jax.ShapeDtypeStruct((p, RET_QK_W), BF16),
            jax.ShapeDtypeStruct((p, RET_V_W), BF16),
            jax.ShapeDtypeStruct((p, RET_V_W), BF16),
            jax.ShapeDtypeStruct((RET_HEADS, 1, RET_V_DIM), F32),
        ],
        grid=(n_chunks,),
        in_specs=_ret_in_specs(chunk_of) + [blk_v, st_spec, blk_v],
        out_specs=[blk_qk, blk_qk, blk_v, blk_v, pl.BlockSpec((RET_HEADS, 1, RET_V_DIM), lambda s: (0, 0, 0))],
        scratch_shapes=[pltpu.VMEM((RET_HEADS, RET_QK_DIM, RET_V_DIM), F32)],
        compiler_params=_params("arbitrary"),
        name=name,
    )(proj, proj, proj, proj, *tables, gn_g.reshape(RET_HEADS, 1, RET_V_DIM), opre, sprev, dog)


def _conv_rows(p):
    return _pick(p, (384, 128))


CONV_CHUNK = 32
F32_SUBLANES = 8


def _shifted_rows(rows):
    return rows + CONV_HALO - F32_SUBLANES


def _shifted_copies(src_scr, sh_scr, n_rows):
    for s in range(1, F32_SUBLANES):
        sh_scr[s - 1] = src_scr[s:s + n_rows, :]


def _tap_rows(src_scr, sh_scr, off, r0, n):
    q, s = divmod(off, F32_SUBLANES)
    ref = src_scr if s == 0 else sh_scr.at[s - 1]
    return ref[pl.ds(pl.multiple_of(r0 + F32_SUBLANES * q, F32_SUBLANES), n), :]


def _ln_stats(y):
    mu = jnp.mean(y, axis=-1, keepdims=True)
    yc = y - mu
    rstd = lax.rsqrt(jnp.mean(yc * yc, axis=-1, keepdims=True) + EPS)
    return yc * rstd, rstd


def _conv_fwd(proj, conv_w, conv_b, ln_g, ln_b, *, name):
    p = proj.shape[0]
    c = D_MODEL
    rows = _conv_rows(p)
    hpb = rows // CONV_HALO
    a_col, gate_col = (2 * RET_QK_W + 2 * RET_V_W) // c, (2 * RET_QK_W + 2 * RET_V_W) // c + 1

    def body(a_ref, gate_ref, ah_ref, gateh_ref, w_ref, b_ref, lg_ref, lb_ref, c_ref, y_ref, hdn_scr, sh_scr):
        i = pl.program_id(0)
        hdn_scr[0:CONV_HALO, :] = ah_ref[...] * _sigmoid(gateh_ref[...])
        hdn_scr[CONV_HALO:, :] = a_ref[...] * _sigmoid(gate_ref[...])
        _shifted_copies(hdn_scr, sh_scr, _shifted_rows(rows))

        def chunk(j, _):
            r0 = pl.multiple_of(j * CONV_CHUNK, CONV_CHUNK)
            acc = jnp.zeros((CONV_CHUNK, c), F32)
            for w in range(CONV_WIDTH):
                off = CONV_HALO - (CONV_WIDTH - 1) + w
                acc += _tap_rows(hdn_scr, sh_scr, off, r0, CONV_CHUNK) * w_ref[w:w + 1, :]
            y_ref[pl.ds(r0, CONV_CHUNK), :] = acc + b_ref[...]
            return 0

        lax.fori_loop(0, rows // CONV_CHUNK, chunk, 0)
        y = y_ref[...]
        yhat, _ = _ln_stats(y)
        ln = yhat * lg_ref[...] + lb_ref[...]
        row = i * rows + lax.broadcasted_iota(jnp.int32, (rows, 1), 0)
        c_ref[...] = jnp.where(row >= PAD_FRONT, ln * _sigmoid(ln), 0.0).astype(c_ref.dtype)

    halo_idx = lambda i: jnp.maximum(i * hpb - 1, 0)
    vec = pl.BlockSpec((1, c), lambda i: (0, 0))
    return pl.pallas_call(
        body,
        out_shape=[jax.ShapeDtypeStruct((p, c), BF16), jax.ShapeDtypeStruct((p, c), F32)],
        grid=(p // rows,),
        in_specs=[
            pl.BlockSpec((rows, c), lambda i: (i, a_col)),
            pl.BlockSpec((rows, c), lambda i: (i, gate_col)),
            pl.BlockSpec((CONV_HALO, c), lambda i: (halo_idx(i), a_col)),
            pl.BlockSpec((CONV_HALO, c), lambda i: (halo_idx(i), gate_col)),
            pl.BlockSpec((CONV_WIDTH, c), lambda i: (0, 0)),
            vec, vec, vec,
        ],
        out_specs=[pl.BlockSpec((rows, c), lambda i: (i, 0)), pl.BlockSpec((rows, c), lambda i: (i, 0))],
        scratch_shapes=[pltpu.VMEM((CONV_HALO + rows, c), F32),
                        pltpu.VMEM((F32_SUBLANES - 1, _shifted_rows(rows), c), F32)],
        compiler_params=_params("parallel"),
        name=name,
    )(proj, proj, proj, proj, conv_w, conv_b, ln_g, ln_b)


def _conv_bwd(proj, conv_w, ln_g, ln_b, y, dcat, *, name):
    p = proj.shape[0]
    c = D_MODEL
    rows = _conv_rows(p)
    hpb = rows // CONV_HALO
    n_blocks = p // rows
    a_col, gate_col = (2 * RET_QK_W + 2 * RET_V_W) // c, (2 * RET_QK_W + 2 * RET_V_W) // c + 1

    def body(a_ref, gate_ref, ah_ref, gateh_ref, w_ref, lg_ref, lb_ref, y_ref, yh_ref, dc_ref, dch_ref,
             da_ref, dgate_ref, dw_ref, db_ref, dlg_ref, dlb_ref, hdn_scr, dy_scr, hdn_sh, dy_sh):
        i = pl.program_id(0)
        lg, lb = lg_ref[...], lb_ref[...]

        def ln_bwd(yv, dcv):
            yhat, rstd = _ln_stats(yv)
            ln = yhat * lg + lb
            sig = _sigmoid(ln)
            dln = dcv * sig * (1.0 + ln * (1.0 - sig))
            dyhat = dln * lg
            dyv = rstd * (dyhat - jnp.mean(dyhat, axis=-1, keepdims=True)
                          - yhat * jnp.mean(dyhat * yhat, axis=-1, keepdims=True))
            return dyv, dln, yhat

        row = i * rows + lax.broadcasted_iota(jnp.int32, (rows, 1), 0)
        dy, dln, yhat = ln_bwd(y_ref[...], jnp.where(row >= PAD_FRONT, dc_ref[...], 0.0))
        dy_halo, _, _ = ln_bwd(yh_ref[...], dch_ref[...])
        dy_scr[0:rows, :] = dy
        dy_scr[rows:, :] = jnp.where(i == n_blocks - 1, 0.0, dy_halo)
        hdn_scr[0:CONV_HALO, :] = ah_ref[...] * _sigmoid(gateh_ref[...])
        hdn_scr[CONV_HALO:, :] = a_ref[...] * _sigmoid(gate_ref[...])
        _shifted_copies(hdn_scr, hdn_sh, _shifted_rows(rows))
        _shifted_copies(dy_scr, dy_sh, _shifted_rows(rows))

        @pl.when(i == 0)
        def _():
            dw_ref[...] = jnp.zeros_like(dw_ref)
            db_ref[...] = jnp.zeros_like(db_ref)
            dlg_ref[...] = jnp.zeros_like(dlg_ref)
            dlb_ref[...] = jnp.zeros_like(dlb_ref)

        n_chunks = rows // CONV_CHUNK

        def input_grad(j, _):
            r0 = pl.multiple_of(j * CONV_CHUNK, CONV_CHUNK)
            dhdn = jnp.zeros((CONV_CHUNK, c), F32)
            for w in range(CONV_WIDTH):
                dhdn += _tap_rows(dy_scr, dy_sh, CONV_WIDTH - 1 - w, r0, CONV_CHUNK) * w_ref[w:w + 1, :]
            here = pl.ds(r0, CONV_CHUNK)
            sig_gate = _sigmoid(gate_ref[here, :])
            da_ref[here, :] = (dhdn * sig_gate).astype(da_ref.dtype)
            dgate_ref[here, :] = (dhdn * a_ref[here, :] * sig_gate * (1.0 - sig_gate)).astype(dgate_ref.dtype)
            return 0

        lax.fori_loop(0, n_chunks, input_grad, 0)
        for w in range(CONV_WIDTH):
            off = CONV_HALO - (CONV_WIDTH - 1) + w

            def tap_grad(j, acc, off=off):
                r0 = pl.multiple_of(j * CONV_CHUNK, CONV_CHUNK)
                prod = dy_scr[pl.ds(r0, CONV_CHUNK), :] * _tap_rows(hdn_scr, hdn_sh, off, r0, CONV_CHUNK)
                for k in range(CONV_CHUNK // F32_SUBLANES):
                    acc = acc + prod[k * F32_SUBLANES:(k + 1) * F32_SUBLANES]
                return acc

            acc = lax.fori_loop(0, n_chunks, tap_grad, jnp.zeros((F32_SUBLANES, c), F32))
            dw_ref[w:w + 1, :] += jnp.sum(acc, axis=0, keepdims=True)
        db_ref[...] += jnp.sum(dy, axis=0, keepdims=True)
        dlg_ref[...] += jnp.sum(dln * yhat, axis=0, keepdims=True)
        dlb_ref[...] += jnp.sum(dln, axis=0, keepdims=True)

    prev_halo = lambda i: jnp.maximum(i * hpb - 1, 0)
    next_halo = lambda i: jnp.minimum((i + 1) * hpb, p // CONV_HALO - 1)
    vec = pl.BlockSpec((1, c), lambda i: (0, 0))
    blk = lambda col: pl.BlockSpec((rows, c), lambda i: (i, col))
    outs = pl.pallas_call(
        body,
        out_shape=[
            jax.ShapeDtypeStruct((p, c), BF16),
            jax.ShapeDtypeStruct((p, c), BF16),
            jax.ShapeDtypeStruct((CONV_WIDTH + 1, c), F32),
            jax.ShapeDtypeStruct((1, c), F32),
            jax.ShapeDtypeStruct((1, c), F32),
            jax.ShapeDtypeStruct((1, c), F32),
        ],
        grid=(n_blocks,),
        in_specs=[
            blk(a_col), blk(gate_col),
            pl.BlockSpec((CONV_HALO, c), lambda i: (prev_halo(i), a_col)),
            pl.BlockSpec((CONV_HALO, c), lambda i: (prev_halo(i), gate_col)),
            pl.BlockSpec((CONV_WIDTH, c), lambda i: (0, 0)),
            vec, vec,
            blk(0),
            pl.BlockSpec((CONV_HALO, c), lambda i: (next_halo(i), 0)),
            blk(1),
            pl.BlockSpec((CONV_HALO, c), lambda i: (next_halo(i), 1)),
        ],
        out_specs=[blk(0), blk(0), pl.BlockSpec((CONV_WIDTH + 1, c), lambda i: (0, 0)), vec, vec, vec],
        scratch_shapes=[pltpu.VMEM((CONV_HALO + rows, c), F32), pltpu.VMEM((rows + CONV_HALO, c), F32),
                        pltpu.VMEM((F32_SUBLANES - 1, _shifted_rows(rows), c), F32),
                        pltpu.VMEM((F32_SUBLANES - 1, _shifted_rows(rows), c), F32)],
        compiler_params=_params("arbitrary"),
        name=name,
    )(proj, proj, proj, proj, conv_w, ln_g, ln_b, y, y, dcat, dcat)
    da, dgate, dw, db, dlg, dlb = outs
    return da, dgate, dw[:CONV_WIDTH], db, dlg, dlb


LANES = 128


def _group_matrix():
    r = jnp.arange(LANES)[:, None] // SB_HEAD_DIM
    c = jnp.arange(LANES)[None, :] // SB_HEAD_DIM
    return (r == c).astype(BF16)


def _head_sums(v, gm):
    return jnp.concatenate([_split_dot(v[:, j * LANES:(j + 1) * LANES], gm) for j in range(v.shape[1] // LANES)], axis=1)


def _qknorm_fwd(qkv, qg, kg, *, name):
    p = qkv.shape[0]
    d = D_MODEL
    rows = _pick(p, (384, 128, 96))

    def body(q_ref, k_ref, v_ref, qg_ref, kg_ref, gm_ref, qn_ref, kn_ref, vb_ref):
        gm = gm_ref[...]

        def norm(x, g):
            ms = _head_sums(x * x, gm) * (1.0 / SB_HEAD_DIM)
            return x * lax.rsqrt(ms + EPS) * g

        qn_ref[...] = norm(q_ref[...], qg_ref[...]).astype(BF16)
        kn_ref[...] = norm(k_ref[...], kg_ref[...]).astype(BF16)
        vb_ref[...] = v_ref[...].astype(BF16)

    blk = lambda col: pl.BlockSpec((rows, d), lambda i: (i, col))
    vec = pl.BlockSpec((1, d), lambda i: (0, 0))
    return pl.pallas_call(
        body,
        out_shape=[jax.ShapeDtypeStruct((p, d), BF16)] * 3,
        grid=(p // rows,),
        in_specs=[blk(0), blk(1), blk(2), vec, vec, pl.BlockSpec((LANES, LANES), lambda i: (0, 0))],
        out_specs=[blk(0)] * 3,
        compiler_params=_params("parallel"),
        name=name,
    )(qkv, qkv, qkv, qg, kg, _group_matrix())


def _qknorm_bwd(qkv, qg, kg, dqn, dkn, dv, *, name):
    p = qkv.shape[0]
    d = D_MODEL
    rows = _pick(p, (384, 128, 96))

    def body(q_ref, k_ref, qg_ref, kg_ref, gm_ref, dqn_ref, dkn_ref, dv_ref, dqkv_ref, dqg_ref, dkg_ref):
        gm = gm_ref[...]

        def bwd(x, g, dy):
            ms = _head_sums(x * x, gm) * (1.0 / SB_HEAD_DIM)
            r = lax.rsqrt(ms + EPS)
            gdy = dy * g
            proj = _head_sums(x * gdy, gm) * (1.0 / SB_HEAD_DIM)
            return r * gdy - x * (r * r * r) * proj, jnp.sum(dy * x * r, axis=0, keepdims=True)

        dq, dqg = bwd(q_ref[...], qg_ref[...], dqn_ref[...])
        dk, dkg = bwd(k_ref[...], kg_ref[...], dkn_ref[...])
        dqkv_ref[:, 0:d] = dq.astype(BF16)
        dqkv_ref[:, d:2 * d] = dk.astype(BF16)
        dqkv_ref[:, 2 * d:3 * d] = dv_ref[...].astype(BF16)

        @pl.when(pl.program_id(0) == 0)
        def _():
            dqg_ref[...] = dqg
            dkg_ref[...] = dkg

        @pl.when(pl.program_id(0) > 0)
        def _():
            dqg_ref[...] += dqg
            dkg_ref[...] += dkg

    blk = lambda col: pl.BlockSpec((rows, d), lambda i: (i, col))
    vec = pl.BlockSpec((1, d), lambda i: (0, 0))
    return pl.pallas_call(
        body,
        out_shape=[jax.ShapeDtypeStruct((p, 3 * d), BF16), jax.ShapeDtypeStruct((1, d), F32),
                   jax.ShapeDtypeStruct((1, d), F32)],
        grid=(p // rows,),
        in_specs=[blk(0), blk(1), vec, vec, pl.BlockSpec((LANES, LANES), lambda i: (0, 0)), blk(0), blk(0), blk(0)],
        out_specs=[pl.BlockSpec((rows, 3 * d), lambda i: (i, 0)), vec, vec],
        compiler_params=_params("arbitrary"),
        name=name,
    )(qkv, qkv, qg, kg, _group_matrix(), dqn, dkn, dv)


SB_PAIR = 2 * SB_HEAD_DIM
SB_GROUP = 8
SB_PAIRS_PER_STEP = 2
SB_PAIRS_PER_STEP_BWD = 2
SB_MASKED = -1e30


def _sb_consts():
    lane = lax.broadcasted_iota(jnp.int32, (CHUNK, SB_PAIR), 1)
    r = lax.broadcasted_iota(jnp.int32, (CHUNK, CHUNK), 0)
    c = lax.broadcasted_iota(jnp.int32, (CHUNK, CHUNK), 1)
    lo = (lane < SB_HEAD_DIM).astype(F32).astype(BF16)
    ones = jnp.ones((CHUNK, CHUNK), BF16)
    twice = lambda m: jnp.concatenate([jnp.concatenate([m, ones], axis=1)] * 2, axis=0)
    later, earlier = twice((r > c).astype(BF16)), twice((r < c).astype(BF16))
    not_before = (c >= r).astype(F32) * SB_MASKED
    padding = (c < PAD_FRONT).astype(F32) * SB_MASKED
    return (lo, 1.0 - lo), c, later, earlier, not_before, padding


def _sb_halves(t, head_lanes):
    return t * head_lanes[0], t * head_lanes[1]


def _sb_logits(qh, kg, biases):
    z = _dot_nt(qh, kg)
    tiles = []
    for b, bias in enumerate(biases):
        zt = z[:, b * CHUNK:(b + 1) * CHUNK]
        if bias is not None:
            zt = zt + bias
        ls_pos = jnp.minimum(zt, 0.0) - jnp.log(1.0 + jnp.exp(-jnp.abs(zt)))
        tiles.append((ls_pos, ls_pos - zt))
    return tiles


def _sb_block_sums(tiles, m):
    st = jnp.concatenate(tiles, axis=0)
    hi = st.astype(BF16)
    lo = (st - hi.astype(F32)).astype(BF16)
    tot = _dot(jnp.concatenate([hi, lo], axis=1), m)
    return [(tot[i * CHUNK:(i + 1) * CHUNK, 0:CHUNK], tot[i * CHUNK:(i + 1) * CHUNK, CHUNK:2 * CHUNK])
            for i in range(len(tiles))]


def _sb_plan(qi, padding, not_before):
    top = lax.div(qi, SB_GROUP)
    size = qi - SB_GROUP * top + 1

    def masks(n_b):
        pad_if_first = padding * (top == 0).astype(F32)
        m = [None] * n_b
        m[n_b - 1] = not_before
        m[0] = pad_if_first if m[0] is None else m[0] + pad_if_first
        return m

    return top, size, masks


def _once_if(cond, fn, carry):
    return lax.fori_loop(0, jnp.where(cond, 1, 0), lambda s, cr: fn(cr), carry)


def _sb_head_rows(tg, lanes, n_b):
    return jnp.concatenate([tg[b * CHUNK:(b + 1) * CHUNK] * lanes for b in range(n_b)], axis=0)


def _sb_fwd(qn, kn, vb, *, name):
    p = qn.shape[0]
    n_blocks = p // CHUNK
    n_pairs = SB_HEADS // 2
    scale = SB_HEAD_DIM ** -0.5

    n_step = SB_PAIRS_PER_STEP
    n_chains = 2 * n_step
    lanes_of = lambda pair: slice(pair * SB_PAIR, (pair + 1) * SB_PAIR)

    def body(q_ref, k_ref, v_ref, o_ref, car_ref):
        head_lanes, c, later, _, not_before, padding = _sb_consts()

        def q_block(qi, _):
            rows = pl.ds(pl.multiple_of(qi * CHUNK, CHUNK), CHUNK)
            qs = []
            for pair in range(n_step):
                qh = _sb_halves(q_ref[rows, lanes_of(pair)], head_lanes)
                qs += [qh[0] * scale, qh[1] * scale]

            def blocks(kb0, biases, carry):
                n_b = len(biases)
                accs, runs, savs = list(carry[:n_step]), list(carry[n_step:n_step + n_chains]), list(carry[n_step + n_chains:])
                krows = pl.ds(pl.multiple_of(kb0 * CHUNK, CHUNK), n_b * CHUNK)
                kgs = [k_ref[krows, lanes_of(pair)] for pair in range(n_step)]
                vgs = [v_ref[krows, lanes_of(pair)] for pair in range(n_step)]
                tiles = [_sb_logits(qs[ch], kgs[ch // 2], biases) for ch in range(n_chains)]
                sums = [_sb_block_sums([log_keep for _, log_keep in tiles[ch]], later) for ch in range(n_chains)]
                cols = [(c == kb0 + b).astype(F32) for b in range(n_b)]
                for ch in range(n_chains):
                    ws = [None] * n_b
                    for b in reversed(range(n_b)):
                        after, row_sum = sums[ch][b]
                        ws[b] = jnp.exp(tiles[ch][b][0] + after + runs[ch]).astype(BF16)
                        savs[ch] = savs[ch] + cols[b] * runs[ch]
                        runs[ch] = runs[ch] + row_sum
                    accs[ch // 2] = accs[ch // 2] + _dot(jnp.concatenate(ws, axis=1),
                                                         _sb_head_rows(vgs[ch // 2], head_lanes[ch % 2], n_b))
                return (*accs, *runs, *savs)

            zt = qs[0].astype(F32) * 0.0
            top, size, masks = _sb_plan(qi, padding, not_before)
            carry = (zt,) * (n_step + 2 * n_chains)
            for n_b in range(1, SB_GROUP + 1):
                carry = _once_if(size == n_b, lambda cr, n_b=n_b: blocks(SB_GROUP * top, masks(n_b), cr), carry)
            carry = lax.fori_loop(0, jnp.maximum(top - 1, 0),
                                  lambda it, cr: blocks(SB_GROUP * (top - 1 - it), [None] * SB_GROUP, cr), carry)
            carry = _once_if(top > 0, functools.partial(blocks, 0, [padding] + [None] * (SB_GROUP - 1)), carry)
            for pair in range(n_step):
                o_ref[rows, lanes_of(pair)] = carry[pair].astype(o_ref.dtype)
            for ch in range(n_chains):
                car_ref[rows, ch * CHUNK:(ch + 1) * CHUNK] = carry[n_step + n_chains + ch]
            return 0

        lax.fori_loop(0, n_blocks, q_block, 0)

    col = pl.BlockSpec((p, n_step * SB_PAIR), lambda g: (0, g))
    return pl.pallas_call(
        body,
        out_shape=[jax.ShapeDtypeStruct((p, D_MODEL), BF16), jax.ShapeDtypeStruct((p, n_pairs * 2 * CHUNK), F32)],
        grid=(n_pairs // n_step,),
        in_specs=[col, col, col],
        out_specs=[col, pl.BlockSpec((p, n_chains * CHUNK), lambda g: (0, g))],
        compiler_params=_params("parallel"),
        name=name,
    )(qn, kn, vb)


def _sb_bwd(qn, kn, vb, carries, do, *, name):
    p = qn.shape[0]
    n_blocks = p // CHUNK
    n_pairs = SB_HEADS // 2
    scale = SB_HEAD_DIM ** -0.5

    n_step = SB_PAIRS_PER_STEP_BWD
    n_chains = 2 * n_step
    lanes_of = lambda pair: slice(pair * SB_PAIR, (pair + 1) * SB_PAIR)

    def body(q_ref, k_ref, v_ref, car_hbm, do_hbm, dq_ref, dk_ref, dv_ref, car_buf, do_buf, sems):
        head_lanes, c, later, earlier, not_before, padding = _sb_consts()
        dk_ref[...] = jnp.zeros_like(dk_ref)
        dv_ref[...] = jnp.zeros_like(dv_ref)
        step = pl.program_id(0)

        def fetch(qi, slot):
            rows = pl.ds(pl.multiple_of(qi * CHUNK, CHUNK), CHUNK)
            car_cols = pl.ds(pl.multiple_of(step * (n_chains * CHUNK), CHUNK), n_chains * CHUNK)
            do_cols = pl.ds(pl.multiple_of(step * (n_step * SB_PAIR), SB_PAIR), n_step * SB_PAIR)
            return (pltpu.make_async_copy(car_hbm.at[rows, car_cols], car_buf.at[slot], sems.at[0, slot]),
                    pltpu.make_async_copy(do_hbm.at[rows, do_cols], do_buf.at[slot], sems.at[1, slot]))

        for cp in fetch(0, 0):
            cp.start()

        def q_block(qi, _):
            rows = pl.ds(pl.multiple_of(qi * CHUNK, CHUNK), CHUNK)
            slot = lax.rem(qi, 2)
            for cp in fetch(qi, slot):
                cp.wait()

            @pl.when(qi + 1 < n_blocks)
            def _():
                for cp in fetch(qi + 1, 1 - slot):
                    cp.start()

            do_rows, car_rows = do_buf[slot], car_buf[slot]
            qs, doh, do2, q2 = [], [], [], []
            for pair in range(n_step):
                qh = _sb_halves(q_ref[rows, lanes_of(pair)], head_lanes)
                qs += [qh[0] * scale, qh[1] * scale]
                doh += list(_sb_halves(do_rows[:, lanes_of(pair)].astype(BF16), head_lanes))
                do2.append(jnp.concatenate(doh[-2:], axis=0))
                q2.append(jnp.concatenate(qs[-2:], axis=0))
            sav = [car_rows[:, ch * CHUNK:(ch + 1) * CHUNK] for ch in range(n_chains)]

            def blocks(kb0, biases, carry):
                n_b = len(biases)
                dq_accs, pres = list(carry[:n_step]), list(carry[n_step:])
                krows = pl.ds(pl.multiple_of(kb0 * CHUNK, CHUNK), n_b * CHUNK)
                kgs = [k_ref[krows, lanes_of(pair)] for pair in range(n_step)]
                vgs = [v_ref[krows, lanes_of(pair)] for pair in range(n_step)]
                cols = [(c == kb0 + b).astype(F32) for b in range(n_b)]
                block = lambda t, b: t[:, b * CHUNK:(b + 1) * CHUNK]
                tiles = [_sb_logits(qs[ch], kgs[ch // 2], biases) for ch in range(n_chains)]
                afters = [_sb_block_sums([log_keep for _, log_keep in tiles[ch]], later) for ch in range(n_chains)]
                dws = [_dot_nt(doh[ch], vgs[ch // 2]) for ch in range(n_chains)]
                ws, es, befores = [], [], []
                for ch in range(n_chains):
                    runs = [jnp.sum(cols[b] * sav[ch], axis=-1, keepdims=True) for b in range(n_b)]
                    ws.append([jnp.exp(tiles[ch][b][0] + afters[ch][b][0] + runs[b]) for b in range(n_b)])
                    es.append([ws[ch][b] * block(dws[ch], b) for b in range(n_b)])
                    befores.append(_sb_block_sums(es[ch], earlier))
                dz2, w2 = [], []
                for ch in range(n_chains):
                    dzs = []
                    for b in range(n_b):
                        before, row_sum = befores[ch][b]
                        sig = jnp.exp(tiles[ch][b][0])
                        e = es[ch][b]
                        dzs.append((e - (e + before + pres[ch]) * sig).astype(BF16))
                        pres[ch] = pres[ch] + row_sum
                    dz2.append(jnp.concatenate(dzs, axis=1))
                    w2.append(jnp.concatenate([t.astype(BF16) for t in ws[ch]], axis=1))
                    dq_accs[ch // 2] = dq_accs[ch // 2] + _dot(dz2[ch], _sb_head_rows(kgs[ch // 2], head_lanes[ch % 2], n_b))
                for pair in range(n_step):
                    both = slice(2 * pair, 2 * pair + 2)
                    dv_ref[krows, lanes_of(pair)] += _dot_tn(jnp.concatenate(w2[both], axis=0), do2[pair])
                    dk_ref[krows, lanes_of(pair)] += _dot_tn(jnp.concatenate(dz2[both], axis=0), q2[pair])
                return (*dq_accs, *pres)

            zt = qs[0].astype(F32) * 0.0
            top, size, masks = _sb_plan(qi, padding, not_before)
            carry = _once_if(top > 0, functools.partial(blocks, 0, [padding] + [None] * (SB_GROUP - 1)),
                             (zt,) * (n_step + n_chains))
            carry = lax.fori_loop(1, top, lambda g, cr: blocks(SB_GROUP * g, [None] * SB_GROUP, cr), carry)
            for n_b in range(1, SB_GROUP + 1):
                carry = _once_if(size == n_b, lambda cr, n_b=n_b: blocks(SB_GROUP * top, masks(n_b), cr), carry)
            for pair in range(n_step):
                dq_ref[rows, lanes_of(pair)] = carry[pair] * scale
            return 0

        lax.fori_loop(0, n_blocks, q_block, 0)

    col = pl.BlockSpec((p, n_step * SB_PAIR), lambda g: (0, g))
    return pl.pallas_call(
        body,
        out_shape=[jax.ShapeDtypeStruct((p, D_MODEL), F32)] * 3,
        grid=(n_pairs // n_step,),
        in_specs=[col, col, col, ANY, ANY],
        out_specs=[col, col, col],
        scratch_shapes=[pltpu.VMEM((2, CHUNK, n_chains * CHUNK), F32), pltpu.VMEM((2, CHUNK, n_step * SB_PAIR), F32),
                        pltpu.SemaphoreType.DMA((2, 2))],
        compiler_params=_params("arbitrary"),
        name=name,
    )(qn, kn, vb, carries, do)


def _loss_head(h, target, *, name):
    p, d = h.shape
    n_blocks = p // CHUNK

    def body(h_ref, t_ref, sq_ref, dh_ref):
        i = pl.program_id(0)

        @pl.when(i == 0)
        def _():
            sq_ref[...] = jnp.zeros_like(sq_ref)
            dh_ref[...] = jnp.zeros_like(dh_ref)

        @pl.when(i > 0)
        def _():
            err = h_ref[...] - t_ref[...]
            sq_ref[...] += jnp.sum(err * err)
            dh_ref[...] = err * (1.0 / d)

    return pl.pallas_call(
        body,
        out_shape=[jax.ShapeDtypeStruct((8, 128), F32), jax.ShapeDtypeStruct((p, d), F32)],
        grid=(n_blocks,),
        in_specs=[pl.BlockSpec((CHUNK, d), lambda i: (i, 0)),
                  pl.BlockSpec((CHUNK, d), lambda i: (jnp.maximum(i - 1, 0), 0))],
        out_specs=[pl.BlockSpec((8, 128), lambda i: (0, 0)), pl.BlockSpec((CHUNK, d), lambda i: (i, 0))],
        compiler_params=_params("arbitrary"),
        name=name,
    )(h, target)


def _local_step(x, target, meta, norm_mix_g, norm_mlp_g, w_in, gn_g, conv_w, conv_b, ln_g, ln_b, qn_g, kn_g, later,
                reached=lambda point, after, grads=None: None):
    seq = x.shape[0]
    p = PAD_FRONT + N_META + seq
    d = D_MODEL
    tables = _retention_tables(p)
    row = lambda v: v.reshape(1, -1)
    h0 = jnp.concatenate([jnp.zeros((PAD_FRONT, d), F32), meta, x], axis=0)

    hn0 = _rmsnorm_fwd(h0, row(norm_mix_g[0]), name="l0_mix_norm")
    proj = _matmul(hn0, w_in, mode="nn", out_dtypes=(F32,), name="l0_proj")
    og, opre, sprev = _retention_fwd(proj, gn_g, tables, name="l0_retention")
    cb, y_conv = _conv_fwd(proj, conv_w, row(conv_b), row(ln_g), row(ln_b), name="l0_conv")
    cat = jnp.concatenate([og, cb], axis=1)
    w_out, w1_0, w2_0 = later("l0", cat)
    w1, w2 = [w1_0, None], [w2_0, None]
    h1 = _matmul(cat, w_out, mode="nn", out_dtypes=(F32,), epilogue=_add_epilogue, extras=(h0,), name="l0_mix_out")
    h2, mlp0 = _mlp_fwd(h1, row(norm_mlp_g[0]), w1[0], w2[0], name="l0_mlp")

    hn1 = _rmsnorm_fwd(h2, row(norm_mix_g[1]), name="l1_mix_norm")
    (w_qkv,) = later("qkv", hn1)
    qkv = _matmul(hn1, w_qkv, mode="nn", out_dtypes=(F32,), name="l1_qkv")
    qg_t, kg_t = jnp.tile(row(qn_g), (1, SB_HEADS)), jnp.tile(row(kn_g), (1, SB_HEADS))
    qn, kn, vb = _qknorm_fwd(qkv, qg_t, kg_t, name="l1_qknorm")
    o_sb, carries = _sb_fwd(qn, kn, vb, name="l1_stickbreak")
    w_o, w1[1], w2[1] = later("l1", o_sb)
    h3 = _matmul(o_sb, w_o, mode="nn", out_dtypes=(F32,), epilogue=_add_epilogue, extras=(h2,), name="l1_mix_out")
    h4, mlp1 = _mlp_fwd(h3, row(norm_mlp_g[1]), w1[1], w2[1], name="l1_mlp")

    sq, dh4 = _loss_head(h4, target, name="loss_head")

    dh3, dg_mlp1, dw1_1, dw2_1 = _mlp_bwd(h3, row(norm_mlp_g[1]), w1[1], w2[1], mlp1, dh4, name="l1_mlp_bwd")
    do_sb = _matmul(dh3, w_o, mode="nt", out_dtypes=(F32,), name="l1_do")
    dw_o = _matmul(o_sb, dh3, mode="tn", out_dtypes=(F32,), name="l1_dwo")
    dqn, dkn, dv = _sb_bwd(qn, kn, vb, carries, do_sb, name="l1_stickbreak_bwd")
    dqkv, dqg_t, dkg_t = _qknorm_bwd(qkv, qg_t, kg_t, dqn, dkn, dv, name="l1_qknorm_bwd")
    dw_qkv = _matmul(hn1, dqkv, mode="tn", out_dtypes=(F32,), name="l1_dwqkv")
    pin = lambda arr, tok: arr if tok is None else arr + tok[0:1, 0:1]
    tok = reached("l1_grads", dqkv, dict(odd_w_qkv=dw_qkv, odd_w_o=dw_o, mlp_w1_1=dw1_1, mlp_w2_1=dw2_1))
    dhn1 = _matmul(dqkv, w_qkv, mode="nt", out_dtypes=(F32,), name="l1_dhn", after=tok)
    dh2, dg_mix1 = _rmsnorm_bwd(h2, row(norm_mix_g[1]), dhn1, dh3, name="l1_mix_dnorm")
    tok = reached("l1_done", dh2)

    dh1, dg_mlp0, dw1_0, dw2_0 = _mlp_bwd(h1, row(norm_mlp_g[0]), w1[0], w2[0], mlp0, dh2, name="l0_mlp_bwd", after=tok)
    tok = reached("l0_mlp_grads", dh1, dict(mlp_w1_0=dw1_0, mlp_w2_0=dw2_0))
    dcat = _matmul(dh1, w_out, mode="nt", out_dtypes=(F32,), name="l0_dcat", after=tok)
    dw_out = _matmul(cat, dh1, mode="tn", out_dtypes=(F32,), name="l0_dwout")
    tok = reached("l0_dwout", dcat, dict(even_w_out=dw_out))
    dq, dk, dvr, dgate_r, dgn = _retention_bwd(proj, pin(gn_g, tok), tables, opre, sprev, dcat, name="l0_retention_bwd")
    tok = reached("l0_retention_bwd", dq)
    da, dgate_c, dconv_w, dconv_b, dln_g, dln_b = _conv_bwd(proj, conv_w, pin(row(ln_g), tok), row(ln_b), y_conv, dcat,
                                                            name="l0_conv_bwd")
    tok = reached("l0_conv_bwd", da)
    dproj = jnp.concatenate([dq, dk, dvr, dgate_r, da, dgate_c], axis=1)
    dw_in = _matmul(hn0, dproj, mode="tn", out_dtypes=(F32,), name="l0_dwin", after=tok)
    tok = reached("l0_dwin", dproj, dict(even_w_in=dw_in))
    dhn0 = _matmul(dproj, w_in, mode="nt", out_dtypes=(F32,), name="l0_dhn", after=tok)
    tok = reached("l0_dhn", dhn0)
    dh0, dg_mix0 = _rmsnorm_bwd(h0, pin(row(norm_mix_g[0]), tok), dhn0, dh1, name="l0_mix_dnorm")

    fold = lambda t: t.reshape(SB_HEADS, SB_HEAD_DIM).sum(axis=0)
    grads = dict(
        x=dh0[PAD_FRONT + N_META:],
        meta=dh0[PAD_FRONT:PAD_FRONT + N_META],
        norm_mix_g=jnp.concatenate([dg_mix0, dg_mix1], axis=0),
        norm_mlp_g=jnp.concatenate([dg_mlp0, dg_mlp1], axis=0),
        even_w_in=dw_in,
        even_ret_gn_g=dgn.reshape(RET_HEADS, RET_V_DIM),
        even_conv_w=dconv_w,
        even_conv_b=dconv_b,
        even_conv_ln_g=dln_g,
        even_conv_ln_b=dln_b,
        even_w_out=dw_out,
        odd_w_qkv=dw_qkv,
        odd_q_norm_g=fold(dqg_t)[None],
        odd_k_norm_g=fold(dkg_t)[None],
        odd_w_o=dw_o,
        mlp_w1=(dw1_0, dw1_1),
        mlp_w2=(dw2_0, dw2_1),
    )
    return sq[0, 0], grads


def _position():
    x, y, c = lax.axis_index("x"), lax.axis_index("y"), lax.axis_index("c")
    other_chips = [(1 - x, y), (x, 1 - y), (1 - x, 1 - y)]
    return x, y, c, other_chips


def _shard_of(ref, kind, s, n):
    rows, cols = ref.shape
    if kind == "col":
        return ref.at[:, pl.ds(s * (cols // n), cols // n)]
    return ref.at[pl.ds(s * (rows // n), rows // n), :]


def _half_of(ref, kind, c):
    rows, cols = ref.shape
    if kind == "col":
        return ref.at[pl.ds(c * (rows // 2), rows // 2), :]
    return ref.at[:, pl.ds(c * (cols // 2), cols // 2)]


def _remote(src, dst, send_sems, recv_sems, idx, device):
    return pltpu.make_async_remote_copy(src_ref=src, dst_ref=dst, send_sem=send_sems.at[idx], recv_sem=recv_sems.at[idx],
                                        device_id=device, device_id_type=MESH)


def _cast_into_whole(w, kind, s_arr, *, name):
    rows, cols = w.shape
    tr = _pick(rows, (256, 128))
    nb = rows // tr
    if kind == "col":
        whole, o_spec = (rows, cols * N_CHIPS), pl.BlockSpec((tr, cols), lambda i, s_ref: (i, s_ref[0]))
    else:
        whole, o_spec = (rows * N_CHIPS, cols), pl.BlockSpec((tr, cols), lambda i, s_ref: (s_ref[0] * nb + i, 0))

    def body(s_ref, w_ref, o_ref):
        o_ref[...] = w_ref[...].astype(BF16)

    return pl.pallas_call(
        body,
        out_shape=jax.ShapeDtypeStruct(whole, BF16),
        grid_spec=pltpu.PrefetchScalarGridSpec(num_scalar_prefetch=1, grid=(nb,),
                                               in_specs=[pl.BlockSpec((tr, cols), lambda i, s_ref: (i, 0))],
                                               out_specs=o_spec),
        compiler_params=_params("parallel"),
        name=name,
    )(s_arr, w)


def _allgather_weights(wholes, kinds):
    n = len(wholes)

    def body(*refs):
        ins, outs = refs[:n], refs[n:2 * n]
        send_sems, recv_sems = refs[2 * n:]
        x, y, c, chips = _position()
        me_chip = 2 * x + y
        sibling = (x, y, 1 - c)
        sends = []
        for t in range(n):
            for k, (cx, cy) in enumerate(chips):
                src = _half_of(_shard_of(ins[t], kinds[t], me_chip, N_CHIPS), kinds[t], c)
                dst = _half_of(_shard_of(outs[t], kinds[t], me_chip, N_CHIPS), kinds[t], c)
                sends.append(_remote(src, dst, send_sems, recv_sems, 6 * t + k, (cx, cy, c)))
        for cp in sends:
            cp.start()
        passed = []
        for t in range(n):
            for k, (cx, cy) in enumerate(chips):
                landed = _half_of(_shard_of(outs[t], kinds[t], 2 * cx + cy, N_CHIPS), kinds[t], c)
                _remote(landed, landed, send_sems, recv_sems, 6 * t + k, (cx, cy, c)).wait_recv()
                fwd = _remote(landed, landed, send_sems, recv_sems, 6 * t + 3 + k, sibling)
                fwd.start()
                passed.append(fwd)
        for t in range(n):
            for k, (cx, cy) in enumerate(chips):
                theirs = _half_of(_shard_of(outs[t], kinds[t], 2 * cx + cy, N_CHIPS), kinds[t], 1 - c)
                _remote(theirs, theirs, send_sems, recv_sems, 6 * t + 3 + k, sibling).wait_recv()
        for cp in sends + passed:
            cp.wait_send()

    return pl.pallas_call(
        body,
        out_shape=[jax.ShapeDtypeStruct(w.shape, BF16) for w in wholes],
        in_specs=[ANY] * n,
        out_specs=[ANY] * n,
        input_output_aliases={t: t for t in range(n)},
        scratch_shapes=[pltpu.SemaphoreType.DMA((6 * n,)), pltpu.SemaphoreType.DMA((6 * n,))],
        name="allgather_weights",
    )(*wholes)


HBM = pl.BlockSpec(memory_space=pltpu.HBM)
SEM = pl.BlockSpec(memory_space=pltpu.SEMAPHORE)
DATAFLOW = pltpu.SideEffectType.DATAFLOW_SIDE_EFFECTING
TARGETS = 6


def _gather_copies(kinds, refs, _, send_sems, recv_sems):
    x, y, c, chips = _position()
    me_chip = 2 * x + y
    sends, lands = [], []
    for t, (ref, kind) in enumerate(zip(refs, kinds)):
        mine = _half_of(_shard_of(ref, kind, me_chip, N_CHIPS), kind, c)
        for k, (cx, cy) in enumerate(chips):
            for other_core in range(2):
                j = TARGETS * t + 2 * k + other_core
                peer_c = 1 - c if other_core else c
                sends.append(_remote(mine, mine, send_sems, recv_sems, j, (cx, cy, peer_c)))
                theirs = _half_of(_shard_of(ref, kind, 2 * cx + cy, N_CHIPS), kind, peer_c)
                lands.append(_remote(theirs, theirs, send_sems, recv_sems, j, (cx, cy, peer_c)))
    return sends, lands


def _pair_swap_copies(kinds, srcs, lands, send_sems, recv_sems):
    x, y, c, _ = _position()
    sibling = (x, y, 1 - c)
    sends = [_remote(_half_of(srcs[t], kinds[t], 1 - c), lands[t], send_sems, recv_sems, t, sibling) for t in range(len(srcs))]
    arrivals = [_remote(_half_of(srcs[t], kinds[t], c), lands[t], send_sems, recv_sems, t, sibling) for t in range(len(srcs))]
    return sends, arrivals


def _chip_exchange_copies(kinds, srcs, lands, send_sems, recv_sems):
    x, y, c, chips = _position()
    sends, arrivals = [], []
    for t in range(len(srcs)):
        for k, (cx, cy) in enumerate(chips):
            src = _shard_of(srcs[t], kinds[t], 2 * cx + cy, N_CHIPS)
            sends.append(_remote(src, lands[t].at[k], send_sems, recv_sems, 3 * t + k, (cx, cy, c)))
            arrivals.append(_remote(src, lands[t].at[k], send_sems, recv_sems, 3 * t + k, (cx, cy, c)))
    return sends, arrivals


def _pair_gather_copies(kinds, srcs, lands, send_sems, recv_sems):
    x, y, c, _ = _position()
    sibling = (x, y, 1 - c)
    sends, arrivals = [], []
    for t in range(len(srcs)):
        mine, theirs = _half_of(srcs[t], kinds[t], c), _half_of(srcs[t], kinds[t], 1 - c)
        sends.append(_remote(mine, mine, send_sems, recv_sems, t, sibling))
        arrivals.append(_remote(theirs, theirs, send_sems, recv_sems, t, sibling))
    return sends, arrivals


def _copies_start(plan, n_sems, srcs, lands, follows, *, name):
    ns, n = len(srcs), len(srcs) + len(lands)

    def body(*refs):
        send_sems, recv_sems = refs[n + 1], refs[n + 2]
        thru, token = refs[n + 3:2 * n + 3], refs[2 * n + 3]
        sends, _ = plan(thru[:ns], thru[ns:], send_sems, recv_sems)
        for cp in sends:
            cp.start()
        token[...] = jnp.zeros_like(token)

    arrays = [pltpu.with_memory_space_constraint(a, pltpu.HBM) for a in list(srcs) + list(lands)]
    outs = pl.pallas_call(
        body,
        name=name,
        out_shape=(pltpu.SemaphoreType.DMA((n_sems,)), pltpu.SemaphoreType.DMA((n_sems,)),
                   *[pltpu.HBM(a.shape, a.dtype) for a in arrays], jax.ShapeDtypeStruct((8, 128), F32)),
        in_specs=(*[HBM] * n, ANY),
        out_specs=(SEM, SEM, *[HBM] * n, pl.BlockSpec(memory_space=pltpu.VMEM)),
        input_output_aliases={t: 2 + t for t in range(n)},
        compiler_params=pltpu.CompilerParams(has_side_effects=DATAFLOW),
    )(*arrays, follows)
    return outs[0], outs[1], list(outs[2:2 + ns]), list(outs[2 + ns:2 + n]), outs[2 + n]


def _copies_wait(plan, started, follows, *, name):
    send_sems, recv_sems, srcs, lands, _ = started
    ns, n = len(srcs), len(srcs) + len(lands)

    def body(*refs):
        ins, s_sems, r_sems = refs[:n], refs[n], refs[n + 1]
        sends, arrivals = plan(ins[:ns], ins[ns:], s_sems, r_sems)
        for cp in sends:
            cp.wait_send()
        for cp in arrivals:
            cp.wait_recv()

    outs = pl.pallas_call(
        body,
        name=name,
        out_shape=tuple(pltpu.HBM(a.shape, a.dtype) for a in srcs + lands),
        in_specs=(*[HBM] * n, SEM, SEM, ANY),
        out_specs=tuple([HBM] * n),
        input_output_aliases={t: t for t in range(n)},
        compiler_params=pltpu.CompilerParams(has_side_effects=DATAFLOW),
    )(*srcs, *lands, send_sems, recv_sems, follows)
    return list(outs[:ns]), list(outs[ns:])


def _allgather8(block, *, name):
    rows, cols = block.shape

    def body(in_ref, out_ref, send_sems, recv_sems, local_sem):
        x, y, c, _ = _position()
        me = 4 * x + 2 * y + c
        mine = pltpu.make_async_copy(in_ref, out_ref.at[me], local_sem)
        mine.start()
        peers = []
        for flip in range(1, N_DEV):
            fx, fy, fc = (flip >> 2) & 1, (flip >> 1) & 1, flip & 1
            peers.append(((1 - x if fx else x), (1 - y if fy else y), (1 - c if fc else c)))
        sends = [_remote(in_ref, out_ref.at[me], send_sems, recv_sems, j, peer) for j, peer in enumerate(peers)]
        for cp in sends:
            cp.start()
        for j, (px, py, pc) in enumerate(peers):
            slot = out_ref.at[4 * px + 2 * py + pc]
            _remote(slot, slot, send_sems, recv_sems, j, (px, py, pc)).wait_recv()
        for cp in sends:
            cp.wait_send()
        mine.wait()

    vmem = pl.BlockSpec(memory_space=pltpu.VMEM)
    return pl.pallas_call(
        body,
        out_shape=jax.ShapeDtypeStruct((N_DEV, rows, cols), F32),
        in_specs=[vmem],
        out_specs=vmem,
        scratch_shapes=[pltpu.SemaphoreType.DMA((N_DEV - 1,)), pltpu.SemaphoreType.DMA((N_DEV - 1,)),
                        pltpu.SemaphoreType.DMA],
        name=name,
    )(block)


def _sum8(stack, *, name):
    _, rows, cols = stack.shape

    def body(s_ref, o_ref):
        acc = s_ref[0]
        for i in range(1, N_DEV):
            acc = acc + s_ref[i]
        o_ref[...] = acc

    return pl.pallas_call(body, out_shape=jax.ShapeDtypeStruct((rows, cols), F32), name=name)(stack)


def _half_add(grad, theirs, kind, c_arr, *, name):
    rows, cols = theirs.shape
    tr = _pick(rows, (256, 128))
    nb = rows // tr
    if kind == "col":
        g_spec = pl.BlockSpec((tr, cols), lambda i, c_ref: (c_ref[0] * nb + i, 0))
    else:
        g_spec = pl.BlockSpec((tr, cols), lambda i, c_ref: (i, c_ref[0]))
    t_spec = pl.BlockSpec((tr, cols), lambda i, c_ref: (i, 0))

    def body(c_ref, g_ref, t_ref, o32_ref, o16_ref):
        tot = g_ref[...] + t_ref[...]
        o32_ref[...] = tot
        o16_ref[...] = tot.astype(BF16)

    return pl.pallas_call(
        body,
        out_shape=[jax.ShapeDtypeStruct((rows, cols), F32), jax.ShapeDtypeStruct((rows, cols), BF16)],
        grid_spec=pltpu.PrefetchScalarGridSpec(num_scalar_prefetch=1, grid=(nb,), in_specs=[g_spec, t_spec],
                                               out_specs=[t_spec, t_spec]),
        compiler_params=_params("parallel"),
        name=name,
    )(c_arr, grad, theirs)


def _shard_sum(part32, recv, kind, sc_arr, *, name):
    _, rows, cols = recv.shape
    tr = _pick(rows, (256, 128))
    nb = rows // tr
    if kind == "col":
        whole = (2 * rows, cols)
        p_spec = pl.BlockSpec((tr, cols), lambda i, sc: (i, sc[0]))
        o_spec = pl.BlockSpec((tr, cols), lambda i, sc: (sc[1] * nb + i, 0))
    else:
        whole = (rows, 2 * cols)
        p_spec = pl.BlockSpec((tr, cols), lambda i, sc: (sc[0] * nb + i, 0))
        o_spec = pl.BlockSpec((tr, cols), lambda i, sc: (i, sc[1]))
    r_spec = pl.BlockSpec((3, tr, cols), lambda i, sc: (0, i, 0))

    def body(sc_ref, p_ref, r_ref, o_ref):
        acc = p_ref[...]
        for k in range(3):
            acc = acc + r_ref[k].astype(F32)
        o_ref[...] = acc

    return pl.pallas_call(
        body,
        out_shape=jax.ShapeDtypeStruct(whole, F32),
        grid_spec=pltpu.PrefetchScalarGridSpec(num_scalar_prefetch=1, grid=(nb,), in_specs=[p_spec, r_spec],
                                               out_specs=o_spec),
        compiler_params=_params("parallel"),
        name=name,
    )(sc_arr, part32, recv)


def _adamw(w, g, m, v, *, name):
    rows, cols = w.shape
    tr = _pick(rows, (256, 128)) if rows * cols > 64 * 1024 else rows

    def body(w_ref, g_ref, m_ref, v_ref, d_ref, nm_ref, nv_ref):
        gv = g_ref[...]
        nm = ADAM_B1 * m_ref[...] + (1.0 - ADAM_B1) * gv
        nv = ADAM_B2 * v_ref[...] + (1.0 - ADAM_B2) * jnp.square(gv)
        m_hat = nm / (1.0 - ADAM_B1 ** ADAM_STEP)
        v_hat = nv / (1.0 - ADAM_B2 ** ADAM_STEP)
        d_ref[...] = -ADAM_LR * (m_hat / (jnp.sqrt(v_hat) + ADAM_EPS) + ADAM_WD * w_ref[...])
        nm_ref[...] = nm
        nv_ref[...] = nv

    spec = pl.BlockSpec((tr, cols), lambda i: (i, 0))
    return pl.pallas_call(
        body,
        out_shape=[jax.ShapeDtypeStruct((rows, cols), F32)] * 3,
        grid=(rows // tr,),
        in_specs=[spec] * 4,
        out_specs=[spec] * 3,
        compiler_params=_params("parallel"),
        name=name,
    )(w, g, m, v)


BIG = ("even_w_in", "odd_w_qkv", "mlp_w1_0", "mlp_w1_1", "even_w_out", "odd_w_o", "mlp_w2_0", "mlp_w2_1")
BIG_KIND = ("col", "col", "col", "col", "row", "row", "row", "row")


class _TravellingReduction:
    def __init__(self, tag, names, kinds, c_arr, sc_arr):
        self.tag, self.names, self.kinds, self.c_arr, self.sc_arr = tag, names, kinds, c_arr, sc_arr
        self.swap = functools.partial(_pair_swap_copies, kinds)
        self.exchange = functools.partial(_chip_exchange_copies, kinds)
        self.gather = functools.partial(_pair_gather_copies, kinds)

    def pair_swap_start(self, grads, follows):
        half = lambda g, kind: (g.shape[0] // 2, g.shape[1]) if kind == "col" else (g.shape[0], g.shape[1] // 2)
        lands = [lax.empty(half(g, k), F32) for g, k in zip(grads, self.kinds)]
        self.started = _copies_start(self.swap, len(grads), grads, lands, follows, name=f"reduce_{self.tag}_pair_start")

    def pair_swap_finish(self, after):
        grads, theirs = _copies_wait(self.swap, self.started, after, name=f"reduce_{self.tag}_pair_wait")
        self.sums = [_half_add(g, th, k, self.c_arr, name="pair_sum_" + n)
                     for g, th, k, n in zip(grads, theirs, self.kinds, self.names)]

    def chips_start(self, follows):
        parts = [s16 for _, s16 in self.sums]
        piece = lambda p, kind: (3, p.shape[0], p.shape[1] // N_CHIPS) if kind == "col" else (3, p.shape[0] // N_CHIPS, p.shape[1])
        lands = [lax.empty(piece(p, k), BF16) for p, k in zip(parts, self.kinds)]
        self.started = _copies_start(self.exchange, 3 * len(parts), parts, lands, follows,
                                     name=f"reduce_{self.tag}_chips_start")

    def chips_finish(self, after):
        _, recv = _copies_wait(self.exchange, self.started, after, name=f"reduce_{self.tag}_chips_wait")
        self.halves = [_shard_sum(s32, r, k, self.sc_arr, name="chip_sum_" + n)
                       for (s32, _), r, k, n in zip(self.sums, recv, self.kinds, self.names)]

    def pair_gather_start(self, follows):
        self.started = _copies_start(self.gather, len(self.halves), self.halves, [], follows,
                                     name=f"reduce_{self.tag}_gather_start")

    def pair_gather_finish(self, after):
        shards, _ = _copies_wait(self.gather, self.started, after, name=f"reduce_{self.tag}_gather_wait")
        return dict(zip(self.names, shards))
SUBLANES = 8


def _pack_rows(parts, width):
    padded, offsets, r0 = [], [], 0
    for t in parts:
        rows = -(-t.shape[0] // SUBLANES) * SUBLANES
        padded.append(jnp.pad(t, ((0, rows - t.shape[0]), (0, width - t.shape[1]))))
        offsets.append(r0)
        r0 += rows
    return jnp.concatenate(padded, axis=0), offsets


def kernel(x, meta, norm_mix_g, norm_mlp_g, even_w_in, even_ret_gn_g, even_conv_w, even_conv_b, even_conv_ln_g, even_conv_ln_b, even_w_out, odd_w_qkv, odd_q_norm_g, odd_k_norm_g, odd_w_o, mlp_w1, mlp_w2, loss_target, m_meta, m_norm_mix_g, m_norm_mlp_g, m_even_w_in, m_even_ret_gn_g, m_even_conv_w, m_even_conv_b, m_even_conv_ln_g, m_even_conv_ln_b, m_even_w_out, m_odd_w_qkv, m_odd_q_norm_g, m_odd_k_norm_g, m_odd_w_o, m_mlp_w1, m_mlp_w2, v_meta, v_norm_mix_g, v_norm_mlp_g, v_even_w_in, v_even_ret_gn_g, v_even_conv_w, v_even_conv_b, v_even_conv_ln_g, v_even_conv_ln_b, v_even_w_out, v_odd_w_qkv, v_odd_q_norm_g, v_odd_k_norm_g, v_odd_w_o, v_mlp_w1, v_mlp_w2):
    d = D_MODEL
    xi, yi, ci = lax.axis_index("x"), lax.axis_index("y"), lax.axis_index("c")
    chip = 2 * xi + yi
    c_arr = jnp.reshape(ci, (1,)).astype(jnp.int32)
    s_arr = jnp.reshape(chip, (1,)).astype(jnp.int32)

    def split_big(w_in, w_qkv, w1, w_out, w_o, w2):
        return dict(zip(BIG, (w_in[0], w_qkv[0], w1[0], w1[1], w_out[0], w_o[0], w2[0], w2[1])))

    w_big = split_big(even_w_in, odd_w_qkv, mlp_w1, even_w_out, odd_w_o, mlp_w2)
    m_big = split_big(m_even_w_in, m_odd_w_qkv, m_mlp_w1, m_even_w_out, m_odd_w_o, m_mlp_w2)
    v_big = split_big(v_even_w_in, v_odd_w_qkv, v_mlp_w1, v_even_w_out, v_odd_w_o, v_mlp_w2)

    placed = {n: _cast_into_whole(w_big[n], k, s_arr, name="cast_" + n) for n, k in zip(BIG, BIG_KIND)}
    kind_of = dict(zip(BIG, BIG_KIND))
    (w_in_full,) = _allgather_weights([placed["even_w_in"]], [kind_of["even_w_in"]])
    packed, (r_meta, r_conv, r_gn) = _pack_rows([meta, even_conv_w[0], even_ret_gn_g[0]], d // N_CHIPS)
    gathered = _allgather8(packed, name="allgather_small_params")[0::2]
    groups = dict(l0=("even_w_out", "mlp_w1_0", "mlp_w2_0"), qkv=("odd_w_qkv",), l1=("odd_w_o", "mlp_w1_1", "mlp_w2_1"))
    in_flight, follows = {}, gathered[0, 0:1, 0:1] + w_in_full[0:1, 0:1].astype(F32)
    for group, names in groups.items():
        plan = functools.partial(_gather_copies, [kind_of[n] for n in names])
        in_flight[group] = (plan, _copies_start(plan, TARGETS * len(names), [placed[n] for n in names], [], follows,
                                                name="gather_" + group + "_start"))
        follows = in_flight[group][1][-1]
    started = follows[0:1, 0:1]

    def later(group, after):
        plan, state = in_flight[group]
        return _copies_wait(plan, state, after, name="gather_" + group + "_wait")[0]

    sc_arr = jnp.concatenate([s_arr, c_arr])
    early = ("odd_w_qkv", "odd_w_o", "mlp_w1_1", "mlp_w2_1"), ("mlp_w1_0", "mlp_w2_0"), ("even_w_out",), ("even_w_in",)
    red_l1, red_m0, red_o0, red_i0 = (_TravellingReduction(tag, names, [kind_of[n] for n in names], c_arr, sc_arr)
                                      for tag, names in zip(("l1", "m0", "o0", "i0"), early))
    grad_big = {}

    def reached(point, after, grads=None):
        if point == "l1_grads":
            red_l1.pair_swap_start([grads[n] for n in red_l1.names], after)
            return red_l1.started[-1]
        if point == "l1_done":
            red_l1.pair_swap_finish(after)
            red_l1.chips_start(after)
            return red_l1.started[-1]
        if point == "l0_mlp_grads":
            red_m0.pair_swap_start([grads[n] for n in red_m0.names], after)
            return red_m0.started[-1]
        if point == "l0_dwout":
            red_m0.pair_swap_finish(after)
            red_m0.chips_start(after)
            red_o0.pair_swap_start([grads[n] for n in red_o0.names], red_m0.started[-1])
            return red_o0.started[-1]
        if point == "l0_retention_bwd":
            red_l1.chips_finish(after)
            red_l1.pair_gather_start(after)
            red_o0.pair_swap_finish(after)
            red_o0.chips_start(red_l1.started[-1])
            return red_o0.started[-1]
        if point == "l0_conv_bwd":
            red_m0.chips_finish(after)
            red_m0.pair_gather_start(after)
            grad_big.update(red_l1.pair_gather_finish(after))
            red_o0.chips_finish(after)
            red_o0.pair_gather_start(red_m0.started[-1])
            return red_o0.started[-1]
        if point == "l0_dwin":
            grad_big.update(red_m0.pair_gather_finish(after))
            grad_big.update(red_o0.pair_gather_finish(after))
            red_i0.pair_swap_start([grads[n] for n in red_i0.names], after)
            return red_i0.started[-1]
        if point == "l0_dhn":
            red_i0.pair_swap_finish(after)
            red_i0.chips_start(after)
            return red_i0.started[-1]
        return None

    across = lambda r0, rows, width: jnp.concatenate([gathered[s, r0:r0 + rows, 0:width] for s in range(N_CHIPS)], axis=1)
    meta_full = across(r_meta, N_META, d // N_CHIPS) + started
    conv_w_full = across(r_conv, CONV_WIDTH, d // N_CHIPS)
    gn_full = across(r_gn, RET_HEADS, RET_V_DIM // N_CHIPS)

    sq, g = _local_step(
        x[0], loss_target[0], meta_full, norm_mix_g, norm_mlp_g, w_in_full, gn_full, conv_w_full,
        even_conv_b[0], even_conv_ln_g[0], even_conv_ln_b[0], odd_q_norm_g[0], odd_k_norm_g[0], later, reached)
    red_i0.chips_finish(g["x"])
    red_i0.pair_gather_start(g["x"])
    grad_big.update(red_i0.pair_gather_finish(g["x"]))
    loss = lax.psum(0.5 * sq / d, ("x", "y", "c"))

    small_names = ("norm_mix_g", "norm_mlp_g", "even_conv_b", "even_conv_ln_g", "even_conv_ln_b", "odd_q_norm_g",
                   "odd_k_norm_g", "meta", "even_conv_w", "even_ret_gn_g")
    pack, offsets = _pack_rows([g[n] for n in small_names], d)
    summed = _sum8(_allgather8(pack, name="allgather_small_grads"), name="sum_small_grads")
    small = {n: summed[r0:r0 + g[n].shape[0], 0:g[n].shape[1]] for n, r0 in zip(small_names, offsets)}
    for n in ("meta", "even_conv_w", "even_ret_gn_g"):
        width = small[n].shape[1] // N_CHIPS
        small[n] = lax.dynamic_slice_in_dim(small[n], chip * width, width, axis=1)

    stacked = dict(mlp_w1=(mlp_w1, m_mlp_w1, v_mlp_w1), mlp_w2=(mlp_w2, m_mlp_w2, v_mlp_w2))
    flat = lambda t: t.reshape(-1, t.shape[-1])
    for n in stacked:
        grad_big[n] = jnp.concatenate([grad_big.pop(n + "_0"), grad_big.pop(n + "_1")], axis=0)
    upd = {n: _adamw(w_big[n], grad_big[n], m_big[n], v_big[n], name="adamw_" + n) for n in BIG if n in grad_big}
    upd.update({n: _adamw(flat(w), grad_big[n], flat(m), flat(v), name="adamw_" + n) for n, (w, m, v) in stacked.items()})

    def join(name, idx, lead):
        t = upd[name][idx] if idx >= 0 else grad_big[name]
        if name in stacked:
            return t.reshape(stacked[name][0].shape)
        return t[None] if lead else t

    small_w = dict(meta=meta, norm_mix_g=norm_mix_g, norm_mlp_g=norm_mlp_g, even_ret_gn_g=even_ret_gn_g[0],
                   even_conv_w=even_conv_w[0], even_conv_b=even_conv_b, even_conv_ln_g=even_conv_ln_g,
                   even_conv_ln_b=even_conv_ln_b, odd_q_norm_g=odd_q_norm_g, odd_k_norm_g=odd_k_norm_g)
    small_m = dict(meta=m_meta, norm_mix_g=m_norm_mix_g, norm_mlp_g=m_norm_mlp_g, even_ret_gn_g=m_even_ret_gn_g[0],
                   even_conv_w=m_even_conv_w[0], even_conv_b=m_even_conv_b, even_conv_ln_g=m_even_conv_ln_g,
                   even_conv_ln_b=m_even_conv_ln_b, odd_q_norm_g=m_odd_q_norm_g, odd_k_norm_g=m_odd_k_norm_g)
    small_v = dict(meta=v_meta, norm_mix_g=v_norm_mix_g, norm_mlp_g=v_norm_mlp_g, even_ret_gn_g=v_even_ret_gn_g[0],
                   even_conv_w=v_even_conv_w[0], even_conv_b=v_even_conv_b, even_conv_ln_g=v_even_conv_ln_g,
                   even_conv_ln_b=v_even_conv_ln_b, odd_q_norm_g=v_odd_q_norm_g, odd_k_norm_g=v_odd_k_norm_g)
    small_upd = {n: _adamw(small_w[n], small[n], small_m[n], small_v[n], name="adamw_" + n) for n in small_w}
    leading = ("even_ret_gn_g", "even_conv_w")

    order = ("meta", "norm_mix_g", "norm_mlp_g", "even_w_in", "even_ret_gn_g", "even_conv_w", "even_conv_b",
             "even_conv_ln_g", "even_conv_ln_b", "even_w_out", "odd_w_qkv", "odd_q_norm_g", "odd_k_norm_g", "odd_w_o",
             "mlp_w1", "mlp_w2")
    big_lead = ("even_w_in", "even_w_out", "odd_w_qkv", "odd_w_o")

    def leaf(name, idx):
        if name in small_w:
            t = small_upd[name][idx] if idx >= 0 else small[name]
            return t[None] if name in leading else t
        return join(name, idx, name in big_lead)

    outs = [loss, g["x"][None]]
    for idx in (-1, 0, 1, 2):
        outs += [leaf(n, idx) for n in order]
    return tuple(outs)
```

```python
import functools

import jax
import jax.numpy as jnp
from jax import lax
from jax.experimental import pallas as pl
from jax.experimental.pallas import tpu as pltpu

F32 = jnp.float32
BF16 = jnp.bfloat16

D_MODEL = 1024
N_META = 16
CHUNK = 128
PAD_FRONT = (-N_META) % CHUNK
RET_HEADS = 4
RET_QK_DIM = 128
RET_V_DIM = 256
RET_QK_W = RET_HEADS * RET_QK_DIM
RET_V_W = RET_HEADS * RET_V_DIM
CONV_WIDTH = 31
CONV_HALO = 32
RET_DECAY_OFFSET = 5.0
ROPE_BASE = 10000.0
SB_HEADS = 16
SB_HEAD_DIM = 64
D_FF = 4 * D_MODEL
EPS = 1e-6
ADAM_LR = 0.001
ADAM_B1 = 0.9
ADAM_B2 = 0.999
ADAM_EPS = 1e-08
ADAM_WD = 0.01
ADAM_STEP = 10

N_CHIPS = 4
N_DEV = 8
VMEM_LIMIT = 56 * 1024 * 1024
MESH = pl.DeviceIdType.MESH
ANY = pl.BlockSpec(memory_space=pl.ANY)


def _params(*sem):
    return pltpu.CompilerParams(dimension_semantics=sem, vmem_limit_bytes=VMEM_LIMIT)


def _pick(n, cands):
    for c in cands:
        if n % c == 0:
            return c
    return n


def _sigmoid(x):
    return 1.0 / (1.0 + jnp.exp(-x))


def _dot(a, b):
    return lax.dot_general(a, b, (((1,), (0,)), ((), ())), preferred_element_type=F32)


def _dot_nt(a, b):
    return lax.dot_general(a, b, (((1,), (1,)), ((), ())), preferred_element_type=F32)


def _dot_tn(a, b):
    return lax.dot_general(a, b, (((0,), (0,)), ((), ())), preferred_element_type=F32)


def _split_dot(x, m):
    hi = x.astype(BF16)
    lo = (x - hi.astype(F32)).astype(BF16)
    return _dot(hi, m) + _dot(lo, m)


def _matmul(a, b, *, mode, out_dtypes, epilogue=None, extras=(), name, after=None):
    if mode == "nn":
        (m, k), (k2, n) = a.shape, b.shape
    elif mode == "nt":
        (m, k), (n, k2) = a.shape, b.shape
    else:
        (k, m), (k2, n) = a.shape, b.shape
    assert k == k2, (a.shape, b.shape, mode)
    tm = _pick(m, (1056, 1024, 768, 512, 384, 256, 128, 96))
    tn = _pick(n, (1024, 768, 512, 256, 128))
    tk = _pick(k, (1056, 1024, 768, 512, 384, 256, 128, 96))
    nk = k // tk
    dot = {"nn": _dot, "nt": _dot_nt, "tn": _dot_tn}[mode]
    n_extra, n_out = len(extras), len(out_dtypes)
    n_after = 0 if after is None else 1
    if epilogue is None:
        epilogue = lambda acc: (acc,)

    def body(a_ref, b_ref, *rest):
        extra_refs = rest[:n_extra]
        out_refs = rest[n_extra + n_after:n_extra + n_after + n_out]
        part = dot(a_ref[...].astype(BF16), b_ref[...].astype(BF16))

        def finish(acc):
            res = epilogue(acc, *[r[...] for r in extra_refs])
            for o_ref, r in zip(out_refs, res):
                o_ref[...] = r.astype(o_ref.dtype)

        if nk == 1:
            finish(part)
        else:
            acc_ref = rest[-1]
            kk = pl.program_id(2)

            @pl.when(kk == 0)
            def _():
                acc_ref[...] = part

            @pl.when(kk > 0)
            def _():
                acc_ref[...] += part

            @pl.when(kk == nk - 1)
            def _():
                finish(acc_ref[...])

    if mode == "nn":
        a_spec = pl.BlockSpec((tm, tk), lambda i, j, kk: (i, kk))
        b_spec = pl.BlockSpec((tk, tn), lambda i, j, kk: (kk, j))
    elif mode == "nt":
        a_spec = pl.BlockSpec((tm, tk), lambda i, j, kk: (i, kk))
        b_spec = pl.BlockSpec((tn, tk), lambda i, j, kk: (j, kk))
    else:
        a_spec = pl.BlockSpec((tk, tm), lambda i, j, kk: (kk, i))
        b_spec = pl.BlockSpec((tk, tn), lambda i, j, kk: (kk, j))
    o_spec = pl.BlockSpec((tm, tn), lambda i, j, kk: (i, j))
    outs = pl.pallas_call(
        body,
        out_shape=[jax.ShapeDtypeStruct((m, n), dt) for dt in out_dtypes],
        grid=(m // tm, n // tn, nk),
        in_specs=[a_spec, b_spec] + [o_spec] * n_extra + [ANY] * n_after,
        out_specs=[o_spec] * n_out,
        scratch_shapes=[pltpu.VMEM((tm, tn), F32)] if nk > 1 else [],
        compiler_params=_params("parallel", "parallel", "arbitrary"),
        name=name,
    )(a, b, *extras, *([] if after is None else [after]))
    return outs[0] if n_out == 1 else outs


def _add_epilogue(acc, res):
    return (res + acc,)


def _rmsnorm_fwd(x, g, *, name):
    p, d = x.shape
    rows = _pick(p, (384, 128, 96))

    def body(x_ref, g_ref, o_ref):
        xv = x_ref[...]
        r = lax.rsqrt(jnp.mean(xv * xv, axis=-1, keepdims=True) + EPS)
        o_ref[...] = (xv * r * g_ref[...]).astype(o_ref.dtype)

    return pl.pallas_call(
        body,
        out_shape=jax.ShapeDtypeStruct((p, d), BF16),
        grid=(p // rows,),
        in_specs=[pl.BlockSpec((rows, d), lambda i: (i, 0)), pl.BlockSpec((1, d), lambda i: (0, 0))],
        out_specs=pl.BlockSpec((rows, d), lambda i: (i, 0)),
        compiler_params=_params("parallel"),
        name=name,
    )(x, g)


def _rmsnorm_bwd(x, g, dy, dres, *, name):
    p, d = x.shape
    rows = _pick(p, (384, 128, 96))

    def body(x_ref, g_ref, dy_ref, dres_ref, dx_ref, dx16_ref, dg_ref):
        xv = x_ref[...]
        r = lax.rsqrt(jnp.mean(xv * xv, axis=-1, keepdims=True) + EPS)
        dyv = dy_ref[...]
        gdy = dyv * g_ref[...]
        proj = jnp.mean(xv * gdy, axis=-1, keepdims=True)
        dx = dres_ref[...] + r * gdy - xv * (r * r * r) * proj
        dx_ref[...] = dx
        dx16_ref[...] = dx.astype(BF16)
        part = jnp.sum(dyv * xv * r, axis=0, keepdims=True)

        @pl.when(pl.program_id(0) == 0)
        def _():
            dg_ref[...] = part

        @pl.when(pl.program_id(0) > 0)
        def _():
            dg_ref[...] += part

    row_spec = pl.BlockSpec((rows, d), lambda i: (i, 0))
    vec_spec = pl.BlockSpec((1, d), lambda i: (0, 0))
    return pl.pallas_call(
        body,
        out_shape=[jax.ShapeDtypeStruct((p, d), F32), jax.ShapeDtypeStruct((p, d), BF16), jax.ShapeDtypeStruct((1, d), F32)],
        grid=(p // rows,),
        in_specs=[row_spec, vec_spec, row_spec, row_spec],
        out_specs=[row_spec, row_spec, vec_spec],
        compiler_params=_params("arbitrary"),
        name=name,
    )(x, g, dy, dres)


def _mlp_fwd(h, g, w1, w2, *, name):
    hn = _rmsnorm_fwd(h, g, name=name + "_norm")

    def act(acc):
        r = jnp.maximum(acc, 0.0)
        return acc, r * r

    z, a2 = _matmul(hn, w1, mode="nn", out_dtypes=(F32, BF16), epilogue=act, name=name + "_up")
    out = _matmul(a2, w2, mode="nn", out_dtypes=(F32,), epilogue=_add_epilogue, extras=(h,), name=name + "_down")
    return out, (hn, z, a2)


def _mlp_bwd(h, g, w1, w2, saved, dout, dout16, *, name, after=None):
    hn, z, a2 = saved

    def dact(acc, zt):
        return (acc * (2.0 * jnp.maximum(zt, 0.0)),)

    dz = _matmul(dout16, w2, mode="nt", out_dtypes=(BF16,), epilogue=dact, extras=(z,), name=name + "_dz", after=after)
    dw2 = _matmul(a2, dout16, mode="tn", out_dtypes=(F32,), name=name + "_dw2")
    dw1 = _matmul(hn, dz, mode="tn", out_dtypes=(F32,), name=name + "_dw1")
    dhn = _matmul(dz, w1, mode="nt", out_dtypes=(F32,), name=name + "_dhn")
    dh, dh16, dg = _rmsnorm_bwd(h, g, dhn, dout, name=name + "_dnorm")
    return dh, dh16, dg, dw1, dw2


def _retention_tables(p):
    half = RET_QK_DIM // 2
    inv_freq = ROPE_BASE ** (-jnp.arange(half, dtype=F32) / half)
    ang = jnp.arange(p, dtype=F32)[:, None] * inv_freq[None, :]
    cos, sin = jnp.cos(ang), jnp.sin(ang)
    cosf = jnp.concatenate([cos, cos], axis=1)
    sins = jnp.concatenate([-sin, sin], axis=1)
    log_g = jnp.log1p(-jnp.exp2(-RET_DECAY_OFFSET - jnp.arange(RET_HEADS, dtype=F32)))
    idx = jnp.arange(CHUNK, dtype=F32)
    diff = idx[:, None] - idx[None, :]
    inner = jnp.where(diff[None] >= 0, jnp.exp(jnp.maximum(diff, 0.0)[None] * log_g[:, None, None]), 0.0)
    kdec = jnp.exp((CHUNK - 1 - idx)[None, :] * log_g[:, None])
    qdec = jnp.exp((idx + 1.0)[None, :] * log_g[:, None])
    cdec = jnp.exp(CHUNK * log_g)
    kdec = jnp.broadcast_to(kdec[:, :, None], (RET_HEADS, CHUNK, RET_QK_DIM))
    qdec = jnp.broadcast_to(qdec[:, :, None], (RET_HEADS, CHUNK, RET_QK_DIM))
    cdec = jnp.broadcast_to(cdec[:, None, None], (RET_HEADS, RET_QK_DIM, RET_V_DIM))
    return cosf, sins, inner, kdec, qdec, cdec


def _rot(x, cosf, sins):
    return x * cosf + pltpu.roll(x, RET_QK_DIM // 2, 1) * sins


def _rot_bwd(dy, cosf, sins):
    return dy * cosf + pltpu.roll(dy * sins, RET_QK_DIM // 2, 1)


def _ret_in_specs(chunk_of):
    q_spec = pl.BlockSpec((CHUNK, RET_QK_W), lambda s: (chunk_of(s), 0))
    k_spec = pl.BlockSpec((CHUNK, RET_QK_W), lambda s: (chunk_of(s), 1))
    v_spec = pl.BlockSpec((CHUNK, RET_V_W), lambda s: (chunk_of(s), 1))
    g_spec = pl.BlockSpec((CHUNK, RET_V_W), lambda s: (chunk_of(s), 2))
    rope_spec = pl.BlockSpec((CHUNK, RET_QK_DIM), lambda s: (chunk_of(s), 0))
    whole = lambda *shape: pl.BlockSpec(shape, lambda s: (0,) * len(shape))
    head_sq = whole(RET_HEADS, CHUNK, CHUNK)
    head_qk = whole(RET_HEADS, CHUNK, RET_QK_DIM)
    head_st = whole(RET_HEADS, RET_QK_DIM, RET_V_DIM)
    gam_spec = whole(RET_HEADS, 1, RET_V_DIM)
    return [q_spec, k_spec, v_spec, g_spec, rope_spec, rope_spec, head_sq, head_qk, head_qk, head_st, gam_spec]


def _ret_head_views(h, qk_refs, v_refs, head_refs):
    qk = pl.ds(h * RET_QK_DIM, RET_QK_DIM)
    vv = pl.ds(h * RET_V_DIM, RET_V_DIM)
    return [r.at[:, qk] for r in qk_refs], [r.at[:, vv] for r in v_refs], [r.at[h] for r in head_refs]


def _retention_fwd(proj, gn_g, tables, *, name):
    p = proj.shape[0]
    n_chunks = p // CHUNK
    scale = RET_QK_DIM ** -0.5

    def body(q_ref, k_ref, v_ref, g_ref, cos_ref, sin_ref, inner_ref, kdec_ref, qdec_ref, cdec_ref, gam_ref,
             og_ref, opre_ref, sprev_ref, s_scr):
        @pl.when(pl.program_id(0) == 0)
        def _():
            s_scr[...] = jnp.zeros_like(s_scr)

        for h in range(RET_HEADS):
            (q_h, k_h), (v_h, g_h, og_h, opre_h), tabs = _ret_head_views(
                h, (q_ref, k_ref), (v_ref, g_ref, og_ref, opre_ref),
                (inner_ref, kdec_ref, qdec_ref, cdec_ref, gam_ref, sprev_ref, s_scr))
            one_head(q_h, k_h, v_h, g_h, cos_ref, sin_ref, *tabs[:5], og_h, opre_h, *tabs[5:])

    def one_head(q_ref, k_ref, v_ref, g_ref, cos_ref, sin_ref, inner_ref, kdec_ref, qdec_ref, cdec_ref, gam_ref,
                 og_ref, opre_ref, sprev_ref, s_scr):
        cosf, sins = cos_ref[...], sin_ref[...]
        qr = _rot(q_ref[...], cosf, sins)
        kr = _rot(k_ref[...], cosf, sins) * scale
        vb = v_ref[...].astype(BF16)
        scores = _dot_nt(qr.astype(BF16), kr.astype(BF16)) * inner_ref[...]
        state = s_scr[...]
        sprev_ref[...] = state
        o = _dot(scores.astype(BF16), vb) + _dot((qr * qdec_ref[...]).astype(BF16), state.astype(BF16))
        kd = kr * kdec_ref[...]
        s_scr[...] = cdec_ref[...] * state + _dot(kd.T.astype(BF16), vb)
        opre_ref[...] = o
        mu = jnp.mean(o, axis=-1, keepdims=True)
        oc = o - mu
        var = jnp.mean(oc * oc, axis=-1, keepdims=True)
        on = oc * lax.rsqrt(var + EPS) * gam_ref[...]
        gv = g_ref[...]
        og_ref[...] = (gv * _sigmoid(gv) * on).astype(og_ref.dtype)

    chunk_of = lambda s: s
    out_v = pl.BlockSpec((CHUNK, RET_V_W), lambda s: (s, 0))
    return pl.pallas_call(
        body,
        out_shape=[
            jax.ShapeDtypeStruct((p, RET_V_W), BF16),
            jax.ShapeDtypeStruct((p, RET_V_W), F32),
            jax.ShapeDtypeStruct((RET_HEADS, n_chunks, RET_QK_DIM, RET_V_DIM), F32),
        ],
        grid=(n_chunks,),
        in_specs=_ret_in_specs(chunk_of),
        out_specs=[out_v, out_v, pl.BlockSpec((RET_HEADS, None, RET_QK_DIM, RET_V_DIM), lambda s: (0, s, 0, 0))],
        scratch_shapes=[pltpu.VMEM((RET_HEADS, RET_QK_DIM, RET_V_DIM), F32)],
        compiler_params=_params("arbitrary"),
        name=name,
    )(proj, proj, proj, proj, *tables, gn_g.reshape(RET_HEADS, 1, RET_V_DIM))


def _retention_bwd(proj, gn_g, tables, opre, sprev, dog, *, name):
    p = proj.shape[0]
    n_chunks = p // CHUNK
    scale = RET_QK_DIM ** -0.5

    def body(q_ref, k_ref, v_ref, g_ref, cos_ref, sin_ref, inner_ref, kdec_ref, qdec_ref, cdec_ref, gam_ref,
             opre_ref, sprev_ref, dog_ref, dq_ref, dk_ref, dv_ref, dg_ref, dgam_ref, ds_scr):
        first = pl.program_id(0) == 0

        @pl.when(first)
        def _():
            ds_scr[...] = jnp.zeros_like(ds_scr)

        for h in range(RET_HEADS):
            (q_h, k_h, dq_h, dk_h), (v_h, g_h, opre_h, dog_h, dv_h, dg_h), tabs = _ret_head_views(
                h, (q_ref, k_ref, dq_ref, dk_ref), (v_ref, g_ref, opre_ref, dog_ref, dv_ref, dg_ref),
                (inner_ref, kdec_ref, qdec_ref, cdec_ref, gam_ref, sprev_ref, dgam_ref, ds_scr))
            one_head(first, q_h, k_h, v_h, g_h, cos_ref, sin_ref, *tabs[:5], opre_h, tabs[5], dog_h,
                     dq_h, dk_h, dv_h, dg_h, tabs[6], tabs[7])

    def one_head(first, q_ref, k_ref, v_ref, g_ref, cos_ref, sin_ref, inner_ref, kdec_ref, qdec_ref, cdec_ref, gam_ref,
                 opre_ref, sprev_ref, dog_ref, dq_ref, dk_ref, dv_ref, dg_ref, dgam_ref, ds_scr):
        cosf, sins = cos_ref[...], sin_ref[...]
        qr = _rot(q_ref[...], cosf, sins)
        kr = _rot(k_ref[...], cosf, sins) * scale
        qb, kb = qr.astype(BF16), kr.astype(BF16)
        vb = v_ref[...].astype(BF16)
        inner = inner_ref[...]
        qdec, kdec = qdec_ref[...], kdec_ref[...]
        state_b = sprev_ref[...].astype(BF16)
        o = opre_ref[...]
        mu = jnp.mean(o, axis=-1, keepdims=True)
        oc = o - mu
        rstd = lax.rsqrt(jnp.mean(oc * oc, axis=-1, keepdims=True) + EPS)
        xhat = oc * rstd
        gam = gam_ref[...]
        on = xhat * gam
        gv = g_ref[...]
        sig = _sigmoid(gv)
        dogv = dog_ref[...]
        dg_ref[...] = (dogv * on * sig * (1.0 + gv * (1.0 - sig))).astype(dg_ref.dtype)
        don = dogv * gv * sig
        dgam_part = jnp.sum(don * xhat, axis=0, keepdims=True)

        @pl.when(first)
        def _():
            dgam_ref[...] = dgam_part

        @pl.when(jnp.logical_not(first))
        def _():
            dgam_ref[...] += dgam_part

        dxhat = don * gam
        do = rstd * (dxhat - jnp.mean(dxhat, axis=-1, keepdims=True)
                     - xhat * jnp.mean(dxhat * xhat, axis=-1, keepdims=True))
        dob = do.astype(BF16)
        scores_b = (_dot_nt(qb, kb) * inner).astype(BF16)
        da = (_dot_nt(dob, vb) * inner).astype(BF16)
        dv = _dot(scores_b.astype(F32).T.astype(BF16), dob)
        dqr = _dot(da, kb)
        dkr = _dot(da.astype(F32).T.astype(BF16), qb)
        dqr += _dot_nt(dob, state_b) * qdec
        ds_local = _dot((qr * qdec).T.astype(BF16), dob)
        gstate = ds_scr[...]
        gb = gstate.astype(BF16)
        kd_b = (kr * kdec).astype(BF16)
        dkr += _dot_nt(vb, gb) * kdec
        dv += _dot(kd_b, gb)
        ds_scr[...] = cdec_ref[...] * gstate + ds_local
        dq_ref[...] = _rot_bwd(dqr, cosf, sins).astype(dq_ref.dtype)
        dk_ref[...] = _rot_bwd(dkr * scale, cosf, sins).astype(dk_ref.dtype)
        dv_ref[...] = dv.astype(dv_ref.dtype)

    chunk_of = lambda s: n_chunks - 1 - s
    blk_v = pl.BlockSpec((CHUNK, RET_V_W), lambda s: (chunk_of(s), 0))
    blk_qk = pl.BlockSpec((CHUNK, RET_QK_W), lambda s: (chunk_of(s), 0))
    st_spec = pl.BlockSpec((RET_HEADS, None, RET_QK_DIM, RET_V_DIM), lambda s: (0, chunk_of(s), 0, 0))
    return pl.pallas_call(
        body,
        out_shape=[
            jax.ShapeDtypeStruct((p, RET_QK_W), BF16),
            jax.ShapeDtypeStruct((p, RET_QK_W), BF16),
            jax.ShapeDtypeStruct((p, RET_V_W), BF16),
            jax.ShapeDtypeStruct((p, RET_V_W), BF16),
            jax.ShapeDtypeStruct((RET_HEADS, 1, RET_V_DIM), F32),
        ],
        grid=(n_chunks,),
        in_specs=_ret_in_specs(chunk_of) + [blk_v, st_spec, blk_v],
        out_specs=[blk_qk, blk_qk, blk_v, blk_v, pl.BlockSpec((RET_HEADS, 1, RET_V_DIM), lambda s: (0, 0, 0))],
        scratch_shapes=[pltpu.VMEM((RET_HEADS, RET_QK_DIM, RET_V_DIM), F32)],
        compiler_params=_params("arbitrary"),
        name=name,
    )(proj, proj, proj, proj, *tables, gn_g.reshape(RET_HEADS, 1, RET_V_DIM), opre, sprev, dog)


def _conv_rows(p):
    return _pick(p, (384, 128))


CONV_CHUNK = 32
F32_SUBLANES = 8


def _shifted_rows(rows):
    return rows + CONV_HALO - F32_SUBLANES


def _shifted_copies(src_scr, sh_scr, n_rows):
    for s in range(1, F32_SUBLANES):
        sh_scr[s - 1] = src_scr[s:s + n_rows, :]


def _tap_rows(src_scr, sh_scr, off, r0, n):
    q, s = divmod(off, F32_SUBLANES)
    ref = src_scr if s == 0 else sh_scr.at[s - 1]
    return ref[pl.ds(pl.multiple_of(r0 + F32_SUBLANES * q, F32_SUBLANES), n), :]


def _ln_stats(y):
    mu = jnp.mean(y, axis=-1, keepdims=True)
    yc = y - mu
    rstd = lax.rsqrt(jnp.mean(yc * yc, axis=-1, keepdims=True) + EPS)
    return yc * rstd, rstd


def _conv_fwd(proj, conv_w, conv_b, ln_g, ln_b, *, name):
    p = proj.shape[0]
    c = D_MODEL
    rows = _conv_rows(p)
    hpb = rows // CONV_HALO
    a_col, gate_col = (2 * RET_QK_W + 2 * RET_V_W) // c, (2 * RET_QK_W + 2 * RET_V_W) // c + 1

    def body(a_ref, gate_ref, ah_ref, gateh_ref, w_ref, b_ref, lg_ref, lb_ref, c_ref, y_ref, hdn_scr, sh_scr):
        i = pl.program_id(0)
        hdn_scr[0:CONV_HALO, :] = ah_ref[...] * _sigmoid(gateh_ref[...])
        hdn_scr[CONV_HALO:, :] = a_ref[...] * _sigmoid(gate_ref[...])
        _shifted_copies(hdn_scr, sh_scr, _shifted_rows(rows))

        def chunk(j, _):
            r0 = pl.multiple_of(j * CONV_CHUNK, CONV_CHUNK)
            acc = jnp.zeros((CONV_CHUNK, c), F32)
            for w in range(CONV_WIDTH):
                off = CONV_HALO - (CONV_WIDTH - 1) + w
                acc += _tap_rows(hdn_scr, sh_scr, off, r0, CONV_CHUNK) * w_ref[w:w + 1, :]
            y_ref[pl.ds(r0, CONV_CHUNK), :] = acc + b_ref[...]
            return 0

        lax.fori_loop(0, rows // CONV_CHUNK, chunk, 0)
        y = y_ref[...]
        yhat, _ = _ln_stats(y)
        ln = yhat * lg_ref[...] + lb_ref[...]
        row = i * rows + lax.broadcasted_iota(jnp.int32, (rows, 1), 0)
        c_ref[...] = jnp.where(row >= PAD_FRONT, ln * _sigmoid(ln), 0.0).astype(c_ref.dtype)

    halo_idx = lambda i: jnp.maximum(i * hpb - 1, 0)
    vec = pl.BlockSpec((1, c), lambda i: (0, 0))
    return pl.pallas_call(
        body,
        out_shape=[jax.ShapeDtypeStruct((p, c), BF16), jax.ShapeDtypeStruct((p, c), F32)],
        grid=(p // rows,),
        in_specs=[
            pl.BlockSpec((rows, c), lambda i: (i, a_col)),
            pl.BlockSpec((rows, c), lambda i: (i, gate_col)),
            pl.BlockSpec((CONV_HALO, c), lambda i: (halo_idx(i), a_col)),
            pl.BlockSpec((CONV_HALO, c), lambda i: (halo_idx(i), gate_col)),
            pl.BlockSpec((CONV_WIDTH, c), lambda i: (0, 0)),
            vec, vec, vec,
        ],
        out_specs=[pl.BlockSpec((rows, c), lambda i: (i, 0)), pl.BlockSpec((rows, c), lambda i: (i, 0))],
        scratch_shapes=[pltpu.VMEM((CONV_HALO + rows, c), F32),
                        pltpu.VMEM((F32_SUBLANES - 1, _shifted_rows(rows), c), F32)],
        compiler_params=_params("parallel"),
        name=name,
    )(proj, proj, proj, proj, conv_w, conv_b, ln_g, ln_b)


def _conv_bwd(proj, conv_w, ln_g, ln_b, y, dcat, *, name):
    p = proj.shape[0]
    c = D_MODEL
    rows = _conv_rows(p)
    hpb = rows // CONV_HALO
    n_blocks = p // rows
    a_col, gate_col = (2 * RET_QK_W + 2 * RET_V_W) // c, (2 * RET_QK_W + 2 * RET_V_W) // c + 1

    def body(a_ref, gate_ref, ah_ref, gateh_ref, w_ref, lg_ref, lb_ref, y_ref, yh_ref, dc_ref, dch_ref,
             da_ref, dgate_ref, dw_ref, db_ref, dlg_ref, dlb_ref, hdn_scr, dy_scr, hdn_sh, dy_sh):
        i = pl.program_id(0)
        lg, lb = lg_ref[...], lb_ref[...]

        def ln_bwd(yv, dcv):
            yhat, rstd = _ln_stats(yv)
            ln = yhat * lg + lb
            sig = _sigmoid(ln)
            dln = dcv * sig * (1.0 + ln * (1.0 - sig))
            dyhat = dln * lg
            dyv = rstd * (dyhat - jnp.mean(dyhat, axis=-1, keepdims=True)
                          - yhat * jnp.mean(dyhat * yhat, axis=-1, keepdims=True))
            return dyv, dln, yhat

        row = i * rows + lax.broadcasted_iota(jnp.int32, (rows, 1), 0)
        dy, dln, yhat = ln_bwd(y_ref[...], jnp.where(row >= PAD_FRONT, dc_ref[...], 0.0))
        dy_halo, _, _ = ln_bwd(yh_ref[...], dch_ref[...])
        dy_scr[0:rows, :] = dy
        dy_scr[rows:, :] = jnp.where(i == n_blocks - 1, 0.0, dy_halo)
        hdn_scr[0:CONV_HALO, :] = ah_ref[...] * _sigmoid(gateh_ref[...])
        hdn_scr[CONV_HALO:, :] = a_ref[...] * _sigmoid(gate_ref[...])
        _shifted_copies(hdn_scr, hdn_sh, _shifted_rows(rows))
        _shifted_copies(dy_scr, dy_sh, _shifted_rows(rows))

        @pl.when(i == 0)
        def _():
            dw_ref[...] = jnp.zeros_like(dw_ref)
            db_ref[...] = jnp.zeros_like(db_ref)
            dlg_ref[...] = jnp.zeros_like(dlg_ref)
            dlb_ref[...] = jnp.zeros_like(dlb_ref)

        n_chunks = rows // CONV_CHUNK

        def input_grad(j, _):
            r0 = pl.multiple_of(j * CONV_CHUNK, CONV_CHUNK)
            dhdn = jnp.zeros((CONV_CHUNK, c), F32)
            for w in range(CONV_WIDTH):
                dhdn += _tap_rows(dy_scr, dy_sh, CONV_WIDTH - 1 - w, r0, CONV_CHUNK) * w_ref[w:w + 1, :]
            here = pl.ds(r0, CONV_CHUNK)
            sig_gate = _sigmoid(gate_ref[here, :])
            da_ref[here, :] = (dhdn * sig_gate).astype(da_ref.dtype)
            dgate_ref[here, :] = (dhdn * a_ref[here, :] * sig_gate * (1.0 - sig_gate)).astype(dgate_ref.dtype)
            return 0

        lax.fori_loop(0, n_chunks, input_grad, 0)
        for w in range(CONV_WIDTH):
            off = CONV_HALO - (CONV_WIDTH - 1) + w

            def tap_grad(j, acc, off=off):
                r0 = pl.multiple_of(j * CONV_CHUNK, CONV_CHUNK)
                prod = dy_scr[pl.ds(r0, CONV_CHUNK), :] * _tap_rows(hdn_scr, hdn_sh, off, r0, CONV_CHUNK)
                for k in range(CONV_CHUNK // F32_SUBLANES):
                    acc = acc + prod[k * F32_SUBLANES:(k + 1) * F32_SUBLANES]
                return acc

            acc = lax.fori_loop(0, n_chunks, tap_grad, jnp.zeros((F32_SUBLANES, c), F32))
            dw_ref[w:w + 1, :] += jnp.sum(acc, axis=0, keepdims=True)
        db_ref[...] += jnp.sum(dy, axis=0, keepdims=True)
        dlg_ref[...] += jnp.sum(dln * yhat, axis=0, keepdims=True)
        dlb_ref[...] += jnp.sum(dln, axis=0, keepdims=True)

    prev_halo = lambda i: jnp.maximum(i * hpb - 1, 0)
    next_halo = lambda i: jnp.minimum((i + 1) * hpb, p // CONV_HALO - 1)
    vec = pl.BlockSpec((1, c), lambda i: (0, 0))
    blk = lambda col: pl.BlockSpec((rows, c), lambda i: (i, col))
    outs = pl.pallas_call(
        body,
        out_shape=[
            jax.ShapeDtypeStruct((p, c), BF16),
            jax.ShapeDtypeStruct((p, c), BF16),
            jax.ShapeDtypeStruct((CONV_WIDTH + 1, c), F32),
            jax.ShapeDtypeStruct((1, c), F32),
            jax.ShapeDtypeStruct((1, c), F32),
            jax.ShapeDtypeStruct((1, c), F32),
        ],
        grid=(n_blocks,),
        in_specs=[
            blk(a_col), blk(gate_col),
            pl.BlockSpec((CONV_HALO, c), lambda i: (prev_halo(i), a_col)),
            pl.BlockSpec((CONV_HALO, c), lambda i: (prev_halo(i), gate_col)),
            pl.BlockSpec((CONV_WIDTH, c), lambda i: (0, 0)),
            vec, vec,
            blk(0),
            pl.BlockSpec((CONV_HALO, c), lambda i: (next_halo(i), 0)),
            blk(1),
            pl.BlockSpec((CONV_HALO, c), lambda i: (next_halo(i), 1)),
        ],
        out_specs=[blk(0), blk(0), pl.BlockSpec((CONV_WIDTH + 1, c), lambda i: (0, 0)), vec, vec, vec],
        scratch_shapes=[pltpu.VMEM((CONV_HALO + rows, c), F32), pltpu.VMEM((rows + CONV_HALO, c), F32),
                        pltpu.VMEM((F32_SUBLANES - 1, _shifted_rows(rows), c), F32),
                        pltpu.VMEM((F32_SUBLANES - 1, _shifted_rows(rows), c), F32)],
        compiler_params=_params("arbitrary"),
        name=name,
    )(proj, proj, proj, proj, conv_w, ln_g, ln_b, y, y, dcat, dcat)
    da, dgate, dw, db, dlg, dlb = outs
    return da, dgate, dw[:CONV_WIDTH], db, dlg, dlb


LANES = 128


def _group_matrix():
    r = jnp.arange(LANES)[:, None] // SB_HEAD_DIM
    c = jnp.arange(LANES)[None, :] // SB_HEAD_DIM
    return (r == c).astype(BF16)


def _head_sums(v, gm):
    return jnp.concatenate([_split_dot(v[:, j * LANES:(j + 1) * LANES], gm) for j in range(v.shape[1] // LANES)], axis=1)


def _qknorm_fwd(qkv, qg, kg, *, name):
    p = qkv.shape[0]
    d = D_MODEL
    rows = _pick(p, (384, 128, 96))

    def body(q_ref, k_ref, v_ref, qg_ref, kg_ref, gm_ref, qn_ref, kn_ref, vb_ref):
        gm = gm_ref[...]

        def norm(x, g):
            ms = _head_sums(x * x, gm) * (1.0 / SB_HEAD_DIM)
            return x * lax.rsqrt(ms + EPS) * g

        qn_ref[...] = norm(q_ref[...], qg_ref[...]).astype(BF16)
        kn_ref[...] = norm(k_ref[...], kg_ref[...]).astype(BF16)
        vb_ref[...] = v_ref[...].astype(BF16)

    blk = lambda col: pl.BlockSpec((rows, d), lambda i: (i, col))
    vec = pl.BlockSpec((1, d), lambda i: (0, 0))
    return pl.pallas_call(
        body,
        out_shape=[jax.ShapeDtypeStruct((p, d), BF16)] * 3,
        grid=(p // rows,),
        in_specs=[blk(0), blk(1), blk(2), vec, vec, pl.BlockSpec((LANES, LANES), lambda i: (0, 0))],
        out_specs=[blk(0)] * 3,
        compiler_params=_params("parallel"),
        name=name,
    )(qkv, qkv, qkv, qg, kg, _group_matrix())


def _qknorm_bwd(qkv, qg, kg, dqn, dkn, dv, *, name):
    p = qkv.shape[0]
    d = D_MODEL
    rows = _pick(p, (384, 128, 96))

    def body(q_ref, k_ref, qg_ref, kg_ref, gm_ref, dqn_ref, dkn_ref, dv_ref, dqkv_ref, dqg_ref, dkg_ref):
        gm = gm_ref[...]

        def bwd(x, g, dy):
            ms = _head_sums(x * x, gm) * (1.0 / SB_HEAD_DIM)
            r = lax.rsqrt(ms + EPS)
            gdy = dy * g
            proj = _head_sums(x * gdy, gm) * (1.0 / SB_HEAD_DIM)
            return r * gdy - x * (r * r * r) * proj, jnp.sum(dy * x * r, axis=0, keepdims=True)

        dq, dqg = bwd(q_ref[...], qg_ref[...], dqn_ref[...])
        dk, dkg = bwd(k_ref[...], kg_ref[...], dkn_ref[...])
        dqkv_ref[:, 0:d] = dq.astype(BF16)
        dqkv_ref[:, d:2 * d] = dk.astype(BF16)
        dqkv_ref[:, 2 * d:3 * d] = dv_ref[...].astype(BF16)

        @pl.when(pl.program_id(0) == 0)
        def _():
            dqg_ref[...] = dqg
            dkg_ref[...] = dkg

        @pl.when(pl.program_id(0) > 0)
        def _():
            dqg_ref[...] += dqg
            dkg_ref[...] += dkg

    blk = lambda col: pl.BlockSpec((rows, d), lambda i: (i, col))
    vec = pl.BlockSpec((1, d), lambda i: (0, 0))
    return pl.pallas_call(
        body,
        out_shape=[jax.ShapeDtypeStruct((p, 3 * d), BF16), jax.ShapeDtypeStruct((1, d), F32),
                   jax.ShapeDtypeStruct((1, d), F32)],
        grid=(p // rows,),
        in_specs=[blk(0), blk(1), vec, vec, pl.BlockSpec((LANES, LANES), lambda i: (0, 0)), blk(0), blk(0), blk(0)],
        out_specs=[pl.BlockSpec((rows, 3 * d), lambda i: (i, 0)), vec, vec],
        compiler_params=_params("arbitrary"),
        name=name,
    )(qkv, qkv, qg, kg, _group_matrix(), dqn, dkn, dv)


SB_PAIR = 2 * SB_HEAD_DIM
SB_GROUP = 8
SB_PAIRS_PER_STEP = 2
SB_PAIRS_PER_STEP_BWD = 2
SB_MASKED = -1e30


def _sb_consts():
    lane = lax.broadcasted_iota(jnp.int32, (CHUNK, SB_PAIR), 1)
    r = lax.broadcasted_iota(jnp.int32, (CHUNK, CHUNK), 0)
    c = lax.broadcasted_iota(jnp.int32, (CHUNK, CHUNK), 1)
    lo = (lane < SB_HEAD_DIM).astype(F32).astype(BF16)
    ones = jnp.ones((CHUNK, CHUNK), BF16)
    twice = lambda m: jnp.concatenate([jnp.concatenate([m, ones], axis=1)] * 2, axis=0)
    later, earlier = twice((r > c).astype(BF16)), twice((r < c).astype(BF16))
    not_before = (c >= r).astype(F32) * SB_MASKED
    padding = (c < PAD_FRONT).astype(F32) * SB_MASKED
    return (lo, 1.0 - lo), c, later, earlier, not_before, padding


def _sb_halves(t, head_lanes):
    return t * head_lanes[0], t * head_lanes[1]


def _sb_logits(qh, kg, biases):
    z = _dot_nt(qh, kg)
    tiles = []
    for b, bias in enumerate(biases):
        zt = z[:, b * CHUNK:(b + 1) * CHUNK]
        if bias is not None:
            zt = zt + bias
        ls_pos = jnp.minimum(zt, 0.0) - jnp.log(1.0 + jnp.exp(-jnp.abs(zt)))
        tiles.append((ls_pos, ls_pos - zt))
    return tiles


def _sb_block_sums(tiles, m):
    st = jnp.concatenate(tiles, axis=0)
    hi = st.astype(BF16)
    lo = (st - hi.astype(F32)).astype(BF16)
    tot = _dot(jnp.concatenate([hi, lo], axis=1), m)
    return [(tot[i * CHUNK:(i + 1) * CHUNK, 0:CHUNK], tot[i * CHUNK:(i + 1) * CHUNK, CHUNK:2 * CHUNK])
            for i in range(len(tiles))]


def _sb_plan(qi, padding, not_before):
    top = lax.div(qi, SB_GROUP)
    size = qi - SB_GROUP * top + 1

    def masks(n_b):
        pad_if_first = padding * (top == 0).astype(F32)
        m = [None] * n_b
        m[n_b - 1] = not_before
        m[0] = pad_if_first if m[0] is None else m[0] + pad_if_first
        return m

    return top, size, masks


def _once_if(cond, fn, carry):
    return lax.fori_loop(0, jnp.where(cond, 1, 0), lambda s, cr: fn(cr), carry)


def _sb_head_rows(tg, lanes, n_b):
    return jnp.concatenate([tg[b * CHUNK:(b + 1) * CHUNK] * lanes for b in range(n_b)], axis=0)


def _sb_fwd(qn, kn, vb, *, name):
    p = qn.shape[0]
    n_blocks = p // CHUNK
    n_pairs = SB_HEADS // 2
    scale = SB_HEAD_DIM ** -0.5

    n_step = SB_PAIRS_PER_STEP
    n_chains = 2 * n_step
    lanes_of = lambda pair: slice(pair * SB_PAIR, (pair + 1) * SB_PAIR)

    def body(q_ref, k_ref, v_ref, o_ref, car_ref):
        head_lanes, c, later, _, not_before, padding = _sb_consts()

        def q_block(qi, _):
            rows = pl.ds(pl.multiple_of(qi * CHUNK, CHUNK), CHUNK)
            qs = []
            for pair in range(n_step):
                qh = _sb_halves(q_ref[rows, lanes_of(pair)], head_lanes)
                qs += [qh[0] * scale, qh[1] * scale]

            def blocks(kb0, biases, carry):
                n_b = len(biases)
                accs, runs, savs = list(carry[:n_step]), list(carry[n_step:n_step + n_chains]), list(carry[n_step + n_chains:])
                krows = pl.ds(pl.multiple_of(kb0 * CHUNK, CHUNK), n_b * CHUNK)
                kgs = [k_ref[krows, lanes_of(pair)] for pair in range(n_step)]
                vgs = [v_ref[krows, lanes_of(pair)] for pair in range(n_step)]
                tiles = [_sb_logits(qs[ch], kgs[ch // 2], biases) for ch in range(n_chains)]
                sums = [_sb_block_sums([log_keep for _, log_keep in tiles[ch]], later) for ch in range(n_chains)]
                cols = [(c == kb0 + b).astype(F32) for b in range(n_b)]
                for ch in range(n_chains):
                    ws = [None] * n_b
                    for b in reversed(range(n_b)):
                        after, row_sum = sums[ch][b]
                        ws[b] = jnp.exp(tiles[ch][b][0] + after + runs[ch]).astype(BF16)
                        savs[ch] = savs[ch] + cols[b] * runs[ch]
                        runs[ch] = runs[ch] + row_sum
                    accs[ch // 2] = accs[ch // 2] + _dot(jnp.concatenate(ws, axis=1),
                                                         _sb_head_rows(vgs[ch // 2], head_lanes[ch % 2], n_b))
                return (*accs, *runs, *savs)

            zt = qs[0].astype(F32) * 0.0
            top, size, masks = _sb_plan(qi, padding, not_before)
            carry = (zt,) * (n_step + 2 * n_chains)
            for n_b in range(1, SB_GROUP + 1):
                carry = _once_if(size == n_b, lambda cr, n_b=n_b: blocks(SB_GROUP * top, masks(n_b), cr), carry)
            carry = lax.fori_loop(0, jnp.maximum(top - 1, 0),
                                  lambda it, cr: blocks(SB_GROUP * (top - 1 - it), [None] * SB_GROUP, cr), carry)
            carry = _once_if(top > 0, functools.partial(blocks, 0, [padding] + [None] * (SB_GROUP - 1)), carry)
            for pair in range(n_step):
                o_ref[rows, lanes_of(pair)] = carry[pair].astype(o_ref.dtype)
            for ch in range(n_chains):
                car_ref[rows, ch * CHUNK:(ch + 1) * CHUNK] = carry[n_step + n_chains + ch]
            return 0

        lax.fori_loop(0, n_blocks, q_block, 0)

    col = pl.BlockSpec((p, n_step * SB_PAIR), lambda g: (0, g))
    return pl.pallas_call(
        body,
        out_shape=[jax.ShapeDtypeStruct((p, D_MODEL), BF16), jax.ShapeDtypeStruct((p, n_pairs * 2 * CHUNK), F32)],
        grid=(n_pairs // n_step,),
        in_specs=[col, col, col],
        out_specs=[col, pl.BlockSpec((p, n_chains * CHUNK), lambda g: (0, g))],
        compiler_params=_params("parallel"),
        name=name,
    )(qn, kn, vb)


def _sb_bwd(qn, kn, vb, carries, do, *, name):
    p = qn.shape[0]
    n_blocks = p // CHUNK
    n_pairs = SB_HEADS // 2
    scale = SB_HEAD_DIM ** -0.5

    n_step = SB_PAIRS_PER_STEP_BWD
    n_chains = 2 * n_step
    lanes_of = lambda pair: slice(pair * SB_PAIR, (pair + 1) * SB_PAIR)

    def body(q_ref, k_ref, v_ref, car_hbm, do_hbm, dq_ref, dk_ref, dv_ref, car_buf, do_buf, sems):
        head_lanes, c, later, earlier, not_before, padding = _sb_consts()
        dk_ref[...] = jnp.zeros_like(dk_ref)
        dv_ref[...] = jnp.zeros_like(dv_ref)
        step = pl.program_id(0)

        def fetch(qi, slot):
            rows = pl.ds(pl.multiple_of(qi * CHUNK, CHUNK), CHUNK)
            car_cols = pl.ds(pl.multiple_of(step * (n_chains * CHUNK), CHUNK), n_chains * CHUNK)
            do_cols = pl.ds(pl.multiple_of(step * (n_step * SB_PAIR), SB_PAIR), n_step * SB_PAIR)
            return (pltpu.make_async_copy(car_hbm.at[rows, car_cols], car_buf.at[slot], sems.at[0, slot]),
                    pltpu.make_async_copy(do_hbm.at[rows, do_cols], do_buf.at[slot], sems.at[1, slot]))

        for cp in fetch(0, 0):
            cp.start()

        def q_block(qi, _):
            rows = pl.ds(pl.multiple_of(qi * CHUNK, CHUNK), CHUNK)
            slot = lax.rem(qi, 2)
            for cp in fetch(qi, slot):
                cp.wait()

            @pl.when(qi + 1 < n_blocks)
            def _():
                for cp in fetch(qi + 1, 1 - slot):
                    cp.start()

            do_rows, car_rows = do_buf[slot], car_buf[slot]
            qs, doh, do2, q2 = [], [], [], []
            for pair in range(n_step):
                qh = _sb_halves(q_ref[rows, lanes_of(pair)], head_lanes)
                qs += [qh[0] * scale, qh[1] * scale]
                doh += list(_sb_halves(do_rows[:, lanes_of(pair)].astype(BF16), head_lanes))
                do2.append(jnp.concatenate(doh[-2:], axis=0))
                q2.append(jnp.concatenate(qs[-2:], axis=0))
            sav = [car_rows[:, ch * CHUNK:(ch + 1) * CHUNK] for ch in range(n_chains)]

            def blocks(kb0, biases, carry):
                n_b = len(biases)
                dq_accs, pres = list(carry[:n_step]), list(carry[n_step:])
                krows = pl.ds(pl.multiple_of(kb0 * CHUNK, CHUNK), n_b * CHUNK)
                kgs = [k_ref[krows, lanes_of(pair)] for pair in range(n_step)]
                vgs = [v_ref[krows, lanes_of(pair)] for pair in range(n_step)]
                cols = [(c == kb0 + b).astype(F32) for b in range(n_b)]
                block = lambda t, b: t[:, b * CHUNK:(b + 1) * CHUNK]
                tiles = [_sb_logits(qs[ch], kgs[ch // 2], biases) for ch in range(n_chains)]
                afters = [_sb_block_sums([log_keep for _, log_keep in tiles[ch]], later) for ch in range(n_chains)]
                dws = [_dot_nt(doh[ch], vgs[ch // 2]) for ch in range(n_chains)]
                ws, es, befores = [], [], []
                for ch in range(n_chains):
                    runs = [jnp.sum(cols[b] * sav[ch], axis=-1, keepdims=True) for b in range(n_b)]
                    ws.append([jnp.exp(tiles[ch][b][0] + afters[ch][b][0] + runs[b]) for b in range(n_b)])
                    es.append([ws[ch][b] * block(dws[ch], b) for b in range(n_b)])
                    befores.append(_sb_block_sums(es[ch], earlier))
                dz2, w2 = [], []
                for ch in range(n_chains):
                    dzs = []
                    for b in range(n_b):
                        before, row_sum = befores[ch][b]
                        sig = jnp.exp(tiles[ch][b][0])
                        e = es[ch][b]
                        dzs.append((e - (e + before + pres[ch]) * sig).astype(BF16))
                        pres[ch] = pres[ch] + row_sum
                    dz2.append(jnp.concatenate(dzs, axis=1))
                    w2.append(jnp.concatenate([t.astype(BF16) for t in ws[ch]], axis=1))
                    dq_accs[ch // 2] = dq_accs[ch // 2] + _dot(dz2[ch], _sb_head_rows(kgs[ch // 2], head_lanes[ch % 2], n_b))
                for pair in range(n_step):
                    both = slice(2 * pair, 2 * pair + 2)
                    dv_ref[krows, lanes_of(pair)] += _dot_tn(jnp.concatenate(w2[both], axis=0), do2[pair])
                    dk_ref[krows, lanes_of(pair)] += _dot_tn(jnp.concatenate(dz2[both], axis=0), q2[pair])
                return (*dq_accs, *pres)

            zt = qs[0].astype(F32) * 0.0
            top, size, masks = _sb_plan(qi, padding, not_before)
            carry = _once_if(top > 0, functools.partial(blocks, 0, [padding] + [None] * (SB_GROUP - 1)),
                             (zt,) * (n_step + n_chains))
            carry = lax.fori_loop(1, top, lambda g, cr: blocks(SB_GROUP * g, [None] * SB_GROUP, cr), carry)
            for n_b in range(1, SB_GROUP + 1):
                carry = _once_if(size == n_b, lambda cr, n_b=n_b: blocks(SB_GROUP * top, masks(n_b), cr), carry)
            for pair in range(n_step):
                dq_ref[rows, lanes_of(pair)] = carry[pair] * scale
            return 0

        lax.fori_loop(0, n_blocks, q_block, 0)

    col = pl.BlockSpec((p, n_step * SB_PAIR), lambda g: (0, g))
    return pl.pallas_call(
        body,
        out_shape=[jax.ShapeDtypeStruct((p, D_MODEL), F32)] * 3,
        grid=(n_pairs // n_step,),
        in_specs=[col, col, col, ANY, ANY],
        out_specs=[col, col, col],
        scratch_shapes=[pltpu.VMEM((2, CHUNK, n_chains * CHUNK), F32), pltpu.VMEM((2, CHUNK, n_step * SB_PAIR), F32),
                        pltpu.SemaphoreType.DMA((2, 2))],
        compiler_params=_params("arbitrary"),
        name=name,
    )(qn, kn, vb, carries, do)


def _loss_head(h, target, *, name):
    p, d = h.shape
    n_blocks = p // CHUNK

    def body(h_ref, t_ref, sq_ref, dh_ref, dh16_ref):
        i = pl.program_id(0)

        @pl.when(i == 0)
        def _():
            sq_ref[...] = jnp.zeros_like(sq_ref)
            dh_ref[...] = jnp.zeros_like(dh_ref)
            dh16_ref[...] = jnp.zeros_like(dh16_ref)

        @pl.when(i > 0)
        def _():
            err = h_ref[...] - t_ref[...]
            sq_ref[...] += jnp.sum(err * err)
            dh = err * (1.0 / d)
            dh_ref[...] = dh
            dh16_ref[...] = dh.astype(BF16)

    rows_spec = pl.BlockSpec((CHUNK, d), lambda i: (i, 0))
    return pl.pallas_call(
        body,
        out_shape=[jax.ShapeDtypeStruct((8, 128), F32), jax.ShapeDtypeStruct((p, d), F32), jax.ShapeDtypeStruct((p, d), BF16)],
        grid=(n_blocks,),
        in_specs=[rows_spec, pl.BlockSpec((CHUNK, d), lambda i: (jnp.maximum(i - 1, 0), 0))],
        out_specs=[pl.BlockSpec((8, 128), lambda i: (0, 0)), rows_spec, rows_spec],
        compiler_params=_params("arbitrary"),
        name=name,
    )(h, target)


def _local_step(x, target, meta, norm_mix_g, norm_mlp_g, w_in, gn_g, conv_w, conv_b, ln_g, ln_b, qn_g, kn_g, later,
                reached=lambda point, after, grads=None: None):
    seq = x.shape[0]
    p = PAD_FRONT + N_META + seq
    d = D_MODEL
    tables = _retention_tables(p)
    row = lambda v: v.reshape(1, -1)
    h0 = jnp.concatenate([jnp.zeros((PAD_FRONT, d), F32), meta, x], axis=0)

    hn0 = _rmsnorm_fwd(h0, row(norm_mix_g[0]), name="l0_mix_norm")
    proj = _matmul(hn0, w_in, mode="nn", out_dtypes=(F32,), name="l0_proj")
    og, opre, sprev = _retention_fwd(proj, gn_g, tables, name="l0_retention")
    cb, y_conv = _conv_fwd(proj, conv_w, row(conv_b), row(ln_g), row(ln_b), name="l0_conv")
    cat = jnp.concatenate([og, cb], axis=1)
    w_out, w1_0, w2_0 = later("l0", cat)
    w1, w2 = [w1_0, None], [w2_0, None]
    h1 = _matmul(cat, w_out, mode="nn", out_dtypes=(F32,), epilogue=_add_epilogue, extras=(h0,), name="l0_mix_out")
    h2, mlp0 = _mlp_fwd(h1, row(norm_mlp_g[0]), w1[0], w2[0], name="l0_mlp")

    hn1 = _rmsnorm_fwd(h2, row(norm_mix_g[1]), name="l1_mix_norm")
    (w_qkv,) = later("qkv", hn1)
    qkv = _matmul(hn1, w_qkv, mode="nn", out_dtypes=(F32,), name="l1_qkv")
    qg_t, kg_t = jnp.tile(row(qn_g), (1, SB_HEADS)), jnp.tile(row(kn_g), (1, SB_HEADS))
    qn, kn, vb = _qknorm_fwd(qkv, qg_t, kg_t, name="l1_qknorm")
    o_sb, carries = _sb_fwd(qn, kn, vb, name="l1_stickbreak")
    w_o, w1[1], w2[1] = later("l1", o_sb)
    h3 = _matmul(o_sb, w_o, mode="nn", out_dtypes=(F32,), epilogue=_add_epilogue, extras=(h2,), name="l1_mix_out")
    h4, mlp1 = _mlp_fwd(h3, row(norm_mlp_g[1]), w1[1], w2[1], name="l1_mlp")

    sq, dh4, dh4_16 = _loss_head(h4, target, name="loss_head")

    dh3, dh3_16, dg_mlp1, dw1_1, dw2_1 = _mlp_bwd(h3, row(norm_mlp_g[1]), w1[1], w2[1], mlp1, dh4, dh4_16, name="l1_mlp_bwd")
    do_sb = _matmul(dh3_16, w_o, mode="nt", out_dtypes=(F32,), name="l1_do")
    dw_o = _matmul(o_sb, dh3_16, mode="tn", out_dtypes=(F32,), name="l1_dwo")
    dqn, dkn, dv = _sb_bwd(qn, kn, vb, carries, do_sb, name="l1_stickbreak_bwd")
    dqkv, dqg_t, dkg_t = _qknorm_bwd(qkv, qg_t, kg_t, dqn, dkn, dv, name="l1_qknorm_bwd")
    dw_qkv = _matmul(hn1, dqkv, mode="tn", out_dtypes=(F32,), name="l1_dwqkv")
    pin = lambda arr, tok: arr if tok is None else arr + tok[0:1, 0:1]
    tok = reached("l1_grads", dqkv, dict(odd_w_qkv=dw_qkv, odd_w_o=dw_o, mlp_w1_1=dw1_1, mlp_w2_1=dw2_1))
    dhn1 = _matmul(dqkv, w_qkv, mode="nt", out_dtypes=(F32,), name="l1_dhn", after=tok)
    dh2, dh2_16, dg_mix1 = _rmsnorm_bwd(h2, row(norm_mix_g[1]), dhn1, dh3, name="l1_mix_dnorm")
    tok = reached("l1_done", dh2)

    dh1, dh1_16, dg_mlp0, dw1_0, dw2_0 = _mlp_bwd(h1, row(norm_mlp_g[0]), w1[0], w2[0], mlp0, dh2, dh2_16,
                                                 name="l0_mlp_bwd", after=tok)
    tok = reached("l0_mlp_grads", dh1, dict(mlp_w1_0=dw1_0, mlp_w2_0=dw2_0))
    dcat = _matmul(dh1_16, w_out, mode="nt", out_dtypes=(F32,), name="l0_dcat", after=tok)
    dw_out = _matmul(cat, dh1_16, mode="tn", out_dtypes=(F32,), name="l0_dwout")
    tok = reached("l0_dwout", dcat, dict(even_w_out=dw_out))
    dq, dk, dvr, dgate_r, dgn = _retention_bwd(proj, pin(gn_g, tok), tables, opre, sprev, dcat, name="l0_retention_bwd")
    tok = reached("l0_retention_bwd", dq)
    da, dgate_c, dconv_w, dconv_b, dln_g, dln_b = _conv_bwd(proj, conv_w, pin(row(ln_g), tok), row(ln_b), y_conv, dcat,
                                                            name="l0_conv_bwd")
    tok = reached("l0_conv_bwd", da)
    dproj = jnp.concatenate([dq, dk, dvr, dgate_r, da, dgate_c], axis=1)
    dw_in = _matmul(hn0, dproj, mode="tn", out_dtypes=(F32,), name="l0_dwin", after=tok)
    tok = reached("l0_dwin", dproj, dict(even_w_in=dw_in))
    dhn0 = _matmul(dproj, w_in, mode="nt", out_dtypes=(F32,), name="l0_dhn", after=tok)
    tok = reached("l0_dhn", dhn0)
    dh0, _, dg_mix0 = _rmsnorm_bwd(h0, pin(row(norm_mix_g[0]), tok), dhn0, dh1, name="l0_mix_dnorm")

    fold = lambda t: t.reshape(SB_HEADS, SB_HEAD_DIM).sum(axis=0)
    grads = dict(
        x=dh0[PAD_FRONT + N_META:],
        meta=dh0[PAD_FRONT:PAD_FRONT + N_META],
        norm_mix_g=jnp.concatenate([dg_mix0, dg_mix1], axis=0),
        norm_mlp_g=jnp.concatenate([dg_mlp0, dg_mlp1], axis=0),
        even_w_in=dw_in,
        even_ret_gn_g=dgn.reshape(RET_HEADS, RET_V_DIM),
        even_conv_w=dconv_w,
        even_conv_b=dconv_b,
        even_conv_ln_g=dln_g,
        even_conv_ln_b=dln_b,
        even_w_out=dw_out,
        odd_w_qkv=dw_qkv,
        odd_q_norm_g=fold(dqg_t)[None],
        odd_k_norm_g=fold(dkg_t)[None],
        odd_w_o=dw_o,
        mlp_w1=(dw1_0, dw1_1),
        mlp_w2=(dw2_0, dw2_1),
    )
    return sq[0, 0], grads


def _position():
    x, y, c = lax.axis_index("x"), lax.axis_index("y"), lax.axis_index("c")
    other_chips = [(1 - x, y), (x, 1 - y), (1 - x, 1 - y)]
    return x, y, c, other_chips


def _shard_of(ref, kind, s, n):
    rows, cols = ref.shape
    if kind == "col":
        return ref.at[:, pl.ds(s * (cols // n), cols // n)]
    return ref.at[pl.ds(s * (rows // n), rows // n), :]


def _half_of(ref, kind, c):
    rows, cols = ref.shape
    if kind == "col":
        return ref.at[pl.ds(c * (rows // 2), rows // 2), :]
    return ref.at[:, pl.ds(c * (cols // 2), cols // 2)]


def _remote(src, dst, send_sems, recv_sems, idx, device):
    return pltpu.make_async_remote_copy(src_ref=src, dst_ref=dst, send_sem=send_sems.at[idx], recv_sem=recv_sems.at[idx],
                                        device_id=device, device_id_type=MESH)


def _cast_into_whole(w, kind, s_arr, *, name):
    rows, cols = w.shape
    tr = _pick(rows, (256, 128))
    nb = rows // tr
    if kind == "col":
        whole, o_spec = (rows, cols * N_CHIPS), pl.BlockSpec((tr, cols), lambda i, s_ref: (i, s_ref[0]))
    else:
        whole, o_spec = (rows * N_CHIPS, cols), pl.BlockSpec((tr, cols), lambda i, s_ref: (s_ref[0] * nb + i, 0))

    def body(s_ref, w_ref, o_ref):
        o_ref[...] = w_ref[...].astype(BF16)

    return pl.pallas_call(
        body,
        out_shape=jax.ShapeDtypeStruct(whole, BF16),
        grid_spec=pltpu.PrefetchScalarGridSpec(num_scalar_prefetch=1, grid=(nb,),
                                               in_specs=[pl.BlockSpec((tr, cols), lambda i, s_ref: (i, 0))],
                                               out_specs=o_spec),
        compiler_params=_params("parallel"),
        name=name,
    )(s_arr, w)


def _allgather_weights(wholes, kinds):
    n = len(wholes)

    def body(*refs):
        ins, outs = refs[:n], refs[n:2 * n]
        send_sems, recv_sems = refs[2 * n:]
        x, y, c, chips = _position()
        me_chip = 2 * x + y
        sibling = (x, y, 1 - c)
        sends = []
        for t in range(n):
            for k, (cx, cy) in enumerate(chips):
                src = _half_of(_shard_of(ins[t], kinds[t], me_chip, N_CHIPS), kinds[t], c)
                dst = _half_of(_shard_of(outs[t], kinds[t], me_chip, N_CHIPS), kinds[t], c)
                sends.append(_remote(src, dst, send_sems, recv_sems, 6 * t + k, (cx, cy, c)))
        for cp in sends:
            cp.start()
        passed = []
        for t in range(n):
            for k, (cx, cy) in enumerate(chips):
                landed = _half_of(_shard_of(outs[t], kinds[t], 2 * cx + cy, N_CHIPS), kinds[t], c)
                _remote(landed, landed, send_sems, recv_sems, 6 * t + k, (cx, cy, c)).wait_recv()
                fwd = _remote(landed, landed, send_sems, recv_sems, 6 * t + 3 + k, sibling)
                fwd.start()
                passed.append(fwd)
        for t in range(n):
            for k, (cx, cy) in enumerate(chips):
                theirs = _half_of(_shard_of(outs[t], kinds[t], 2 * cx + cy, N_CHIPS), kinds[t], 1 - c)
                _remote(theirs, theirs, send_sems, recv_sems, 6 * t + 3 + k, sibling).wait_recv()
        for cp in sends + passed:
            cp.wait_send()

    return pl.pallas_call(
        body,
        out_shape=[jax.ShapeDtypeStruct(w.shape, BF16) for w in wholes],
        in_specs=[ANY] * n,
        out_specs=[ANY] * n,
        input_output_aliases={t: t for t in range(n)},
        scratch_shapes=[pltpu.SemaphoreType.DMA((6 * n,)), pltpu.SemaphoreType.DMA((6 * n,))],
        name="allgather_weights",
    )(*wholes)


HBM = pl.BlockSpec(memory_space=pltpu.HBM)
SEM = pl.BlockSpec(memory_space=pltpu.SEMAPHORE)
DATAFLOW = pltpu.SideEffectType.DATAFLOW_SIDE_EFFECTING
TARGETS = 6


def _gather_copies(kinds, refs, _, send_sems, recv_sems):
    x, y, c, chips = _position()
    me_chip = 2 * x + y
    sends, lands = [], []
    for t, (ref, kind) in enumerate(zip(refs, kinds)):
        mine = _half_of(_shard_of(ref, kind, me_chip, N_CHIPS), kind, c)
        for k, (cx, cy) in enumerate(chips):
            for other_core in range(2):
                j = TARGETS * t + 2 * k + other_core
                peer_c = 1 - c if other_core else c
                sends.append(_remote(mine, mine, send_sems, recv_sems, j, (cx, cy, peer_c)))
                theirs = _half_of(_shard_of(ref, kind, 2 * cx + cy, N_CHIPS), kind, peer_c)
                lands.append(_remote(theirs, theirs, send_sems, recv_sems, j, (cx, cy, peer_c)))
    return sends, lands


def _pair_swap_copies(kinds, srcs, lands, send_sems, recv_sems):
    x, y, c, _ = _position()
    sibling = (x, y, 1 - c)
    sends = [_remote(_half_of(srcs[t], kinds[t], 1 - c), lands[t], send_sems, recv_sems, t, sibling) for t in range(len(srcs))]
    arrivals = [_remote(_half_of(srcs[t], kinds[t], c), lands[t], send_sems, recv_sems, t, sibling) for t in range(len(srcs))]
    return sends, arrivals


def _chip_exchange_copies(kinds, srcs, lands, send_sems, recv_sems):
    x, y, c, chips = _position()
    sends, arrivals = [], []
    for t in range(len(srcs)):
        for k, (cx, cy) in enumerate(chips):
            src = _shard_of(srcs[t], kinds[t], 2 * cx + cy, N_CHIPS)
            sends.append(_remote(src, lands[t].at[k], send_sems, recv_sems, 3 * t + k, (cx, cy, c)))
            arrivals.append(_remote(src, lands[t].at[k], send_sems, recv_sems, 3 * t + k, (cx, cy, c)))
    return sends, arrivals


def _pair_gather_copies(kinds, srcs, lands, send_sems, recv_sems):
    x, y, c, _ = _position()
    sibling = (x, y, 1 - c)
    sends, arrivals = [], []
    for t in range(len(srcs)):
        mine, theirs = _half_of(srcs[t], kinds[t], c), _half_of(srcs[t], kinds[t], 1 - c)
        sends.append(_remote(mine, mine, send_sems, recv_sems, t, sibling))
        arrivals.append(_remote(theirs, theirs, send_sems, recv_sems, t, sibling))
    return sends, arrivals


def _copies_start(plan, n_sems, srcs, lands, follows, *, name):
    ns, n = len(srcs), len(srcs) + len(lands)

    def body(*refs):
        send_sems, recv_sems = refs[n + 1], refs[n + 2]
        thru, token = refs[n + 3:2 * n + 3], refs[2 * n + 3]
        sends, _ = plan(thru[:ns], thru[ns:], send_sems, recv_sems)
        for cp in sends:
            cp.start()
        token[...] = jnp.zeros_like(token)

    arrays = [pltpu.with_memory_space_constraint(a, pltpu.HBM) for a in list(srcs) + list(lands)]
    outs = pl.pallas_call(
        body,
        name=name,
        out_shape=(pltpu.SemaphoreType.DMA((n_sems,)), pltpu.SemaphoreType.DMA((n_sems,)),
                   *[pltpu.HBM(a.shape, a.dtype) for a in arrays], jax.ShapeDtypeStruct((8, 128), F32)),
        in_specs=(*[HBM] * n, ANY),
        out_specs=(SEM, SEM, *[HBM] * n, pl.BlockSpec(memory_space=pltpu.VMEM)),
        input_output_aliases={t: 2 + t for t in range(n)},
        compiler_params=pltpu.CompilerParams(has_side_effects=DATAFLOW),
    )(*arrays, follows)
    return outs[0], outs[1], list(outs[2:2 + ns]), list(outs[2 + ns:2 + n]), outs[2 + n]


def _copies_wait(plan, started, follows, *, name):
    send_sems, recv_sems, srcs, lands, _ = started
    ns, n = len(srcs), len(srcs) + len(lands)

    def body(*refs):
        ins, s_sems, r_sems = refs[:n], refs[n], refs[n + 1]
        sends, arrivals = plan(ins[:ns], ins[ns:], s_sems, r_sems)
        for cp in sends:
            cp.wait_send()
        for cp in arrivals:
            cp.wait_recv()

    outs = pl.pallas_call(
        body,
        name=name,
        out_shape=tuple(pltpu.HBM(a.shape, a.dtype) for a in srcs + lands),
        in_specs=(*[HBM] * n, SEM, SEM, ANY),
        out_specs=tuple([HBM] * n),
        input_output_aliases={t: t for t in range(n)},
        compiler_params=pltpu.CompilerParams(has_side_effects=DATAFLOW),
    )(*srcs, *lands, send_sems, recv_sems, follows)
    return list(outs[:ns]), list(outs[ns:])


def _allgather8(block, *, name):
    rows, cols = block.shape

    def body(in_ref, out_ref, send_sems, recv_sems, local_sem):
        x, y, c, _ = _position()
        me = 4 * x + 2 * y + c
        mine = pltpu.make_async_copy(in_ref, out_ref.at[me], local_sem)
        mine.start()
        peers = []
        for flip in range(1, N_DEV):
            fx, fy, fc = (flip >> 2) & 1, (flip >> 1) & 1, flip & 1
            peers.append(((1 - x if fx else x), (1 - y if fy else y), (1 - c if fc else c)))
        sends = [_remote(in_ref, out_ref.at[me], send_sems, recv_sems, j, peer) for j, peer in enumerate(peers)]
        for cp in sends:
            cp.start()
        for j, (px, py, pc) in enumerate(peers):
            slot = out_ref.at[4 * px + 2 * py + pc]
            _remote(slot, slot, send_sems, recv_sems, j, (px, py, pc)).wait_recv()
        for cp in sends:
            cp.wait_send()
        mine.wait()

    vmem = pl.BlockSpec(memory_space=pltpu.VMEM)
    return pl.pallas_call(
        body,
        out_shape=jax.ShapeDtypeStruct((N_DEV, rows, cols), F32),
        in_specs=[vmem],
        out_specs=vmem,
        scratch_shapes=[pltpu.SemaphoreType.DMA((N_DEV - 1,)), pltpu.SemaphoreType.DMA((N_DEV - 1,)),
                        pltpu.SemaphoreType.DMA],
        name=name,
    )(block)


def _sum8(stack, *, name):
    _, rows, cols = stack.shape

    def body(s_ref, o_ref):
        acc = s_ref[0]
        for i in range(1, N_DEV):
            acc = acc + s_ref[i]
        o_ref[...] = acc

    return pl.pallas_call(body, out_shape=jax.ShapeDtypeStruct((rows, cols), F32), name=name)(stack)


def _half_add(grad, theirs, kind, c_arr, *, name):
    rows, cols = theirs.shape
    tr = _pick(rows, (256, 128))
    nb = rows // tr
    if kind == "col":
        g_spec = pl.BlockSpec((tr, cols), lambda i, c_ref: (c_ref[0] * nb + i, 0))
    else:
        g_spec = pl.BlockSpec((tr, cols), lambda i, c_ref: (i, c_ref[0]))
    t_spec = pl.BlockSpec((tr, cols), lambda i, c_ref: (i, 0))

    def body(c_ref, g_ref, t_ref, o32_ref, o16_ref):
        tot = g_ref[...] + t_ref[...]
        o32_ref[...] = tot
        o16_ref[...] = tot.astype(BF16)

    return pl.pallas_call(
        body,
        out_shape=[jax.ShapeDtypeStruct((rows, cols), F32), jax.ShapeDtypeStruct((rows, cols), BF16)],
        grid_spec=pltpu.PrefetchScalarGridSpec(num_scalar_prefetch=1, grid=(nb,), in_specs=[g_spec, t_spec],
                                               out_specs=[t_spec, t_spec]),
        compiler_params=_params("parallel"),
        name=name,
    )(c_arr, grad, theirs)


def _shard_sum(part32, recv, kind, sc_arr, *, name):
    _, rows, cols = recv.shape
    tr = _pick(rows, (256, 128))
    nb = rows // tr
    if kind == "col":
        whole = (2 * rows, cols)
        p_spec = pl.BlockSpec((tr, cols), lambda i, sc: (i, sc[0]))
        o_spec = pl.BlockSpec((tr, cols), lambda i, sc: (sc[1] * nb + i, 0))
    else:
        whole = (rows, 2 * cols)
        p_spec = pl.BlockSpec((tr, cols), lambda i, sc: (sc[0] * nb + i, 0))
        o_spec = pl.BlockSpec((tr, cols), lambda i, sc: (i, sc[1]))
    r_spec = pl.BlockSpec((3, tr, cols), lambda i, sc: (0, i, 0))

    def body(sc_ref, p_ref, r_ref, o_ref):
        acc = p_ref[...]
        for k in range(3):
            acc = acc + r_ref[k].astype(F32)
        o_ref[...] = acc

    return pl.pallas_call(
        body,
        out_shape=jax.ShapeDtypeStruct(whole, F32),
        grid_spec=pltpu.PrefetchScalarGridSpec(num_scalar_prefetch=1, grid=(nb,), in_specs=[p_spec, r_spec],
                                               out_specs=o_spec),
        compiler_params=_params("parallel"),
        name=name,
    )(sc_arr, part32, recv)


def _adamw(w, g, m, v, *, name):
    rows, cols = w.shape
    tr = _pick(rows, (256, 128)) if rows * cols > 64 * 1024 else rows

    def body(w_ref, g_ref, m_ref, v_ref, d_ref, nm_ref, nv_ref):
        gv = g_ref[...]
        nm = ADAM_B1 * m_ref[...] + (1.0 - ADAM_B1) * gv
        nv = ADAM_B2 * v_ref[...] + (1.0 - ADAM_B2) * jnp.square(gv)
        m_hat = nm / (1.0 - ADAM_B1 ** ADAM_STEP)
        v_hat = nv / (1.0 - ADAM_B2 ** ADAM_STEP)
        d_ref[...] = -ADAM_LR * (m_hat / (jnp.sqrt(v_hat) + ADAM_EPS) + ADAM_WD * w_ref[...])
        nm_ref[...] = nm
        nv_ref[...] = nv

    spec = pl.BlockSpec((tr, cols), lambda i: (i, 0))
    return pl.pallas_call(
        body,
        out_shape=[jax.ShapeDtypeStruct((rows, cols), F32)] * 3,
        grid=(rows // tr,),
        in_specs=[spec] * 4,
        out_specs=[spec] * 3,
        compiler_params=_params("parallel"),
        name=name,
    )(w, g, m, v)


BIG = ("even_w_in", "odd_w_qkv", "mlp_w1_0", "mlp_w1_1", "even_w_out", "odd_w_o", "mlp_w2_0", "mlp_w2_1")
BIG_KIND = ("col", "col", "col", "col", "row", "row", "row", "row")


class _TravellingReduction:
    def __init__(self, tag, names, kinds, c_arr, sc_arr):
        self.tag, self.names, self.kinds, self.c_arr, self.sc_arr = tag, names, kinds, c_arr, sc_arr
        self.swap = functools.partial(_pair_swap_copies, kinds)
        self.exchange = functools.partial(_chip_exchange_copies, kinds)
        self.gather = functools.partial(_pair_gather_copies, kinds)

    def pair_swap_start(self, grads, follows):
        half = lambda g, kind: (g.shape[0] // 2, g.shape[1]) if kind == "col" else (g.shape[0], g.shape[1] // 2)
        lands = [lax.empty(half(g, k), F32) for g, k in zip(grads, self.kinds)]
        self.started = _copies_start(self.swap, len(grads), grads, lands, follows, name=f"reduce_{self.tag}_pair_start")

    def pair_swap_finish(self, after):
        grads, theirs = _copies_wait(self.swap, self.started, after, name=f"reduce_{self.tag}_pair_wait")
        self.sums = [_half_add(g, th, k, self.c_arr, name="pair_sum_" + n)
                     for g, th, k, n in zip(grads, theirs, self.kinds, self.names)]

    def chips_start(self, follows):
        parts = [s16 for _, s16 in self.sums]
        piece = lambda p, kind: (3, p.shape[0], p.shape[1] // N_CHIPS) if kind == "col" else (3, p.shape[0] // N_CHIPS, p.shape[1])
        lands = [lax.empty(piece(p, k), BF16) for p, k in zip(parts, self.kinds)]
        self.started = _copies_start(self.exchange, 3 * len(parts), parts, lands, follows,
                                     name=f"reduce_{self.tag}_chips_start")

    def chips_finish(self, after):
        _, recv = _copies_wait(self.exchange, self.started, after, name=f"reduce_{self.tag}_chips_wait")
        self.halves = [_shard_sum(s32, r, k, self.sc_arr, name="chip_sum_" + n)
                       for (s32, _), r, k, n in zip(self.sums, recv, self.kinds, self.names)]

    def pair_gather_start(self, follows):
        self.started = _copies_start(self.gather, len(self.halves), self.halves, [], follows,
                                     name=f"reduce_{self.tag}_gather_start")

    def pair_gather_finish(self, after):
        shards, _ = _copies_wait(self.gather, self.started, after, name=f"reduce_{self.tag}_gather_wait")
        return dict(zip(self.names, shards))
SUBLANES = 8


def _pack_rows(parts, width):
    padded, offsets, r0 = [], [], 0
    for t in parts:
        rows = -(-t.shape[0] // SUBLANES) * SUBLANES
        padded.append(jnp.pad(t, ((0, rows - t.shape[0]), (0, width - t.shape[1]))))
        offsets.append(r0)
        r0 += rows
    return jnp.concatenate(padded, axis=0), offsets


def kernel(x, meta, norm_mix_g, norm_mlp_g, even_w_in, even_ret_gn_g, even_conv_w, even_conv_b, even_conv_ln_g, even_conv_ln_b, even_w_out, odd_w_qkv, odd_q_norm_g, odd_k_norm_g, odd_w_o, mlp_w1, mlp_w2, loss_target, m_meta, m_norm_mix_g, m_norm_mlp_g, m_even_w_in, m_even_ret_gn_g, m_even_conv_w, m_even_conv_b, m_even_conv_ln_g, m_even_conv_ln_b, m_even_w_out, m_odd_w_qkv, m_odd_q_norm_g, m_odd_k_norm_g, m_odd_w_o, m_mlp_w1, m_mlp_w2, v_meta, v_norm_mix_g, v_norm_mlp_g, v_even_w_in, v_even_ret_gn_g, v_even_conv_w, v_even_conv_b, v_even_conv_ln_g, v_even_conv_ln_b, v_even_w_out, v_odd_w_qkv, v_odd_q_norm_g, v_odd_k_norm_g, v_odd_w_o, v_mlp_w1, v_mlp_w2):
    d = D_MODEL
    xi, yi, ci = lax.axis_index("x"), lax.axis_index("y"), lax.axis_index("c")
    chip = 2 * xi + yi
    c_arr = jnp.reshape(ci, (1,)).astype(jnp.int32)
    s_arr = jnp.reshape(chip, (1,)).astype(jnp.int32)

    def split_big(w_in, w_qkv, w1, w_out, w_o, w2):
        return dict(zip(BIG, (w_in[0], w_qkv[0], w1[0], w1[1], w_out[0], w_o[0], w2[0], w2[1])))

    w_big = split_big(even_w_in, odd_w_qkv, mlp_w1, even_w_out, odd_w_o, mlp_w2)
    m_big = split_big(m_even_w_in, m_odd_w_qkv, m_mlp_w1, m_even_w_out, m_odd_w_o, m_mlp_w2)
    v_big = split_big(v_even_w_in, v_odd_w_qkv, v_mlp_w1, v_even_w_out, v_odd_w_o, v_mlp_w2)

    placed = {n: _cast_into_whole(w_big[n], k, s_arr, name="cast_" + n) for n, k in zip(BIG, BIG_KIND)}
    kind_of = dict(zip(BIG, BIG_KIND))
    (w_in_full,) = _allgather_weights([placed["even_w_in"]], [kind_of["even_w_in"]])
    packed, (r_meta, r_conv, r_gn) = _pack_rows([meta, even_conv_w[0], even_ret_gn_g[0]], d // N_CHIPS)
    gathered = _allgather8(packed, name="allgather_small_params")[0::2]
    groups = dict(l0=("even_w_out", "mlp_w1_0", "mlp_w2_0"), qkv=("odd_w_qkv",), l1=("odd_w_o", "mlp_w1_1", "mlp_w2_1"))
    in_flight, follows = {}, gathered[0, 0:1, 0:1] + w_in_full[0:1, 0:1].astype(F32)
    for group, names in groups.items():
        plan = functools.partial(_gather_copies, [kind_of[n] for n in names])
        in_flight[group] = (plan, _copies_start(plan, TARGETS * len(names), [placed[n] for n in names], [], follows,
                                                name="gather_" + group + "_start"))
        follows = in_flight[group][1][-1]
    started = follows[0:1, 0:1]

    def later(group, after):
        plan, state = in_flight[group]
        return _copies_wait(plan, state, after, name="gather_" + group + "_wait")[0]

    sc_arr = jnp.concatenate([s_arr, c_arr])
    early = ("odd_w_qkv", "odd_w_o", "mlp_w1_1", "mlp_w2_1"), ("mlp_w1_0", "mlp_w2_0"), ("even_w_out",), ("even_w_in",)
    red_l1, red_m0, red_o0, red_i0 = (_TravellingReduction(tag, names, [kind_of[n] for n in names], c_arr, sc_arr)
                                      for tag, names in zip(("l1", "m0", "o0", "i0"), early))
    grad_big = {}

    def reached(point, after, grads=None):
        if point == "l1_grads":
            red_l1.pair_swap_start([grads[n] for n in red_l1.names], after)
            return red_l1.started[-1]
        if point == "l1_done":
            red_l1.pair_swap_finish(after)
            red_l1.chips_start(after)
            return red_l1.started[-1]
        if point == "l0_mlp_grads":
            red_m0.pair_swap_start([grads[n] for n in red_m0.names], after)
            return red_m0.started[-1]
        if point == "l0_dwout":
            red_m0.pair_swap_finish(after)
            red_m0.chips_start(after)
            red_o0.pair_swap_start([grads[n] for n in red_o0.names], red_m0.started[-1])
            return red_o0.started[-1]
        if point == "l0_retention_bwd":
            red_l1.chips_finish(after)
            red_l1.pair_gather_start(after)
            red_o0.pair_swap_finish(after)
            red_o0.chips_start(red_l1.started[-1])
            return red_o0.started[-1]
        if point == "l0_conv_bwd":
            red_m0.chips_finish(after)
            red_m0.pair_gather_start(after)
            grad_big.update(red_l1.pair_gather_finish(after))
            red_o0.chips_finish(after)
            red_o0.pair_gather_start(red_m0.started[-1])
            return red_o0.started[-1]
        if point == "l0_dwin":
            grad_big.update(red_m0.pair_gather_finish(after))
            grad_big.update(red_o0.pair_gather_finish(after))
            red_i0.pair_swap_start([grads[n] for n in red_i0.names], after)
            return red_i0.started[-1]
        if point == "l0_dhn":
            red_i0.pair_swap_finish(after)
            red_i0.chips_start(after)
            return red_i0.started[-1]
        return None

    across = lambda r0, rows, width: jnp.concatenate([gathered[s, r0:r0 + rows, 0:width] for s in range(N_CHIPS)], axis=1)
    meta_full = across(r_meta, N_META, d // N_CHIPS) + started
    conv_w_full = across(r_conv, CONV_WIDTH, d // N_CHIPS)
    gn_full = across(r_gn, RET_HEADS, RET_V_DIM // N_CHIPS)

    sq, g = _local_step(
        x[0], loss_target[0], meta_full, norm_mix_g, norm_mlp_g, w_in_full, gn_full, conv_w_full,
        even_conv_b[0], even_conv_ln_g[0], even_conv_ln_b[0], odd_q_norm_g[0], odd_k_norm_g[0], later, reached)
    red_i0.chips_finish(g["x"])
    red_i0.pair_gather_start(g["x"])
    grad_big.update(red_i0.pair_gather_finish(g["x"]))
    loss = lax.psum(0.5 * sq / d, ("x", "y", "c"))

    small_names = ("norm_mix_g", "norm_mlp_g", "even_conv_b", "even_conv_ln_g", "even_conv_ln_b", "odd_q_norm_g",
                   "odd_k_norm_g", "meta", "even_conv_w", "even_ret_gn_g")
    pack, offsets = _pack_rows([g[n] for n in small_names], d)
    summed = _sum8(_allgather8(pack, name="allgather_small_grads"), name="sum_small_grads")
    small = {n: summed[r0:r0 + g[n].shape[0], 0:g[n].shape[1]] for n, r0 in zip(small_names, offsets)}
    for n in ("meta", "even_conv_w", "even_ret_gn_g"):
        width = small[n].shape[1] // N_CHIPS
        small[n] = lax.dynamic_slice_in_dim(small[n], chip * width, width, axis=1)

    stacked = dict(mlp_w1=(mlp_w1, m_mlp_w1, v_mlp_w1), mlp_w2=(mlp_w2, m_mlp_w2, v_mlp_w2))
    flat = lambda t: t.reshape(-1, t.shape[-1])
    for n in stacked:
        grad_big[n] = jnp.concatenate([grad_big.pop(n + "_0"), grad_big.pop(n + "_1")], axis=0)
    upd = {n: _adamw(w_big[n], grad_big[n], m_big[n], v_big[n], name="adamw_" + n) for n in BIG if n in grad_big}
    upd.update({n: _adamw(flat(w), grad_big[n], flat(m), flat(v), name="adamw_" + n) for n, (w, m, v) in stacked.items()})

    def join(name, idx, lead):
        t = upd[name][idx] if idx >= 0 else grad_big[name]
        if name in stacked:
            return t.reshape(stacked[name][0].shape)
        return t[None] if lead else t

    small_w = dict(meta=meta, norm_mix_g=norm_mix_g, norm_mlp_g=norm_mlp_g, even_ret_gn_g=even_ret_gn_g[0],
                   even_conv_w=even_conv_w[0], even_conv_b=even_conv_b, even_conv_ln_g=even_conv_ln_g,
                   even_conv_ln_b=even_conv_ln_b, odd_q_norm_g=odd_q_norm_g, odd_k_norm_g=odd_k_norm_g)
    small_m = dict(meta=m_meta, norm_mix_g=m_norm_mix_g, norm_mlp_g=m_norm_mlp_g, even_ret_gn_g=m_even_ret_gn_g[0],
                   even_conv_w=m_even_conv_w[0], even_conv_b=m_even_conv_b, even_conv_ln_g=m_even_conv_ln_g,
                   even_conv_ln_b=m_even_conv_ln_b, odd_q_norm_g=m_odd_q_norm_g, odd_k_norm_g=m_odd_k_norm_g)
    small_v = dict(meta=v_meta, norm_mix_g=v_norm_mix_g, norm_mlp_g=v_norm_mlp_g, even_ret_gn_g=v_even_ret_gn_g[0],
                   even_conv_w=v_even_conv_w[0], even_conv_b=v_even_conv_b, even_conv_ln_g=v_even_conv_ln_g,
                   even_conv_ln_b=v_even_conv_ln_b, odd_q_norm_g=v_odd_q_norm_g, odd_k_norm_g=v_odd_k_norm_g)
    small_upd = {n: _adamw(small_w[n], small[n], small_m[n], small_v[n], name="adamw_" + n) for n in small_w}
    leading = ("even_ret_gn_g", "even_conv_w")

    order = ("meta", "norm_mix_g", "norm_mlp_g", "even_w_in", "even_ret_gn_g", "even_conv_w", "even_conv_b",
             "even_conv_ln_g", "even_conv_ln_b", "even_w_out", "odd_w_qkv", "odd_q_norm_g", "odd_k_norm_g", "odd_w_o",
             "mlp_w1", "mlp_w2")
    big_lead = ("even_w_in", "even_w_out", "odd_w_qkv", "odd_w_o")

    def leaf(name, idx):
        if name in small_w:
            t = small_upd[name][idx] if idx >= 0 else small[name]
            return t[None] if name in leading else t
        return join(name, idx, name in big_lead)

    outs = [loss, g["x"][None]]
    for idx in (-1, 0, 1, 2):
        outs += [leaf(n, idx) for n in order]
    return tuple(outs)
```
